```python
import math
import jax, jax.numpy as jnp
from jax import lax
import numpy as np

D_MODEL = 1024
BATCH = 4
SEQ = 4096
DEPTH = 2
DEC_BATCH = 128
DEC_SEQ = 8
PAST_LEN = 16384
PAGE_SIZE = 128

N_HEADS = 16
KV_HEADS = 4
GROUP = N_HEADS // KV_HEADS
HEAD_DIM = 64
Q_DIM = N_HEADS * HEAD_DIM
KV_DIM = KV_HEADS * HEAD_DIM
QKV_DIM = Q_DIM + 2 * KV_DIM
WINDOW = 128
BLOCK = 128
ROPE_THETA = 10000.0
SSM_GROUP_CH = 16
SSM_GROUPS = D_MODEL // SSM_GROUP_CH
SSM_STATE = 64
N_EXPERT_GROUPS = 4
EXPERTS_PER_GROUP = 8
N_EXPERTS = N_EXPERT_GROUPS * EXPERTS_PER_GROUP
TOP_K = 2
EXPERT_FF = 128
N_ATTN_LAYERS = (DEPTH + 1) // 2
N_SSM_LAYERS = DEPTH // 2
NORM_EPS = 1e-5
F32 = jnp.float32

kernel_name = 'hybrid_swa_sink_s5_hmoe_step'


def rmsnorm(x, g):
    xf = x.astype(F32)
    y = xf * lax.rsqrt(jnp.mean(xf * xf, axis=-1, keepdims=True) + NORM_EPS)
    return (y * g.astype(F32)).astype(x.dtype)


def rotary(x, pos):
    half = HEAD_DIM // 2
    inv = 1.0 / (ROPE_THETA ** (jnp.arange(half, dtype=F32) * (2.0 / HEAD_DIM)))
    ang = pos.astype(F32)[:, None] * inv[None, :]
    cos = jnp.cos(ang)[:, None, :]
    sin = jnp.sin(ang)[:, None, :]
    xf = x.astype(F32)
    x1, x2 = xf[..., :half], xf[..., half:]
    return jnp.concatenate([x1 * cos - x2 * sin, x2 * cos + x1 * sin], axis=-1).astype(x.dtype)


def qkv_proj(xn, w_qkv, b_qkv, pos):
    lead = xn.shape[:-1]
    qkv = xn @ w_qkv + b_qkv
    q = qkv[..., :Q_DIM].reshape(*lead, N_HEADS, HEAD_DIM)
    k = qkv[..., Q_DIM:Q_DIM + KV_DIM].reshape(*lead, KV_HEADS, HEAD_DIM)
    v = qkv[..., Q_DIM + KV_DIM:].reshape(*lead, KV_HEADS, HEAD_DIM)
    return rotary(q, pos), rotary(k, pos), v


def attend(q, k, v, mask, sinks):
    s = jnp.einsum('...qhgd,...khd->...hgqk', q, k).astype(F32) * (HEAD_DIM ** -0.5)
    s = jnp.where(mask, s, -jnp.inf)
    sink = sinks.astype(F32).reshape(KV_HEADS, GROUP, 1, 1)
    m = jnp.maximum(jnp.max(s, axis=-1, keepdims=True), sink)
    p = jnp.exp(s - m)
    denom = jnp.sum(p, axis=-1, keepdims=True) + jnp.exp(sink - m)
    w = (p / denom).astype(v.dtype)
    return jnp.einsum('...hgqk,...khd->...qhgd', w, v)


def swa_prompt(xn, w_qkv, b_qkv, w_o, b_o, sinks):
    bsz, seq, _ = xn.shape
    nb = seq // BLOCK
    q, k, v = qkv_proj(xn, w_qkv, b_qkv, jnp.arange(seq, dtype=jnp.int32))
    qb = q.reshape(bsz, nb, BLOCK, KV_HEADS, GROUP, HEAD_DIM)
    kb = k.reshape(bsz, nb, BLOCK, KV_HEADS, HEAD_DIM)
    vb = v.reshape(bsz, nb, BLOCK, KV_HEADS, HEAD_DIM)

    def with_prev(t):
        prev = jnp.concatenate([jnp.zeros_like(t[:, :1]), t[:, :-1]], axis=1)
        return jnp.concatenate([prev, t], axis=2)

    blk = jnp.arange(nb, dtype=jnp.int32)[:, None, None] * BLOCK
    qpos = blk + jnp.arange(BLOCK, dtype=jnp.int32)[None, :, None]
    kpos = blk - BLOCK + jnp.arange(2 * BLOCK, dtype=jnp.int32)[None, None, :]
    mask = (kpos >= 0) & (kpos <= qpos) & (qpos - kpos <= WINDOW)
    o = attend(qb, with_prev(kb), with_prev(vb), mask[:, None, None], sinks)
    y = o.reshape(bsz, seq, Q_DIM) @ w_o + b_o
    return y, k[:, seq - WINDOW:], v[:, seq - WINDOW:]


def swa_sample(xn, ck, cv, w_qkv, b_qkv, w_o, b_o, sinks):
    bsz, n_new, _ = xn.shape
    rows = ck.shape[1]
    qpos = PAST_LEN + jnp.arange(n_new, dtype=jnp.int32)
    q, k, v = qkv_proj(xn, w_qkv, b_qkv, qpos)
    kk = jnp.concatenate([ck.astype(k.dtype), k], axis=1)
    vv = jnp.concatenate([cv.astype(v.dtype), v], axis=1)
    kpos = jnp.concatenate([PAST_LEN - rows + jnp.arange(rows, dtype=jnp.int32), qpos])
    mask = (kpos[None, :] <= qpos[:, None]) & (qpos[:, None] - kpos[None, :] <= WINDOW)
    o = attend(q.reshape(bsz, n_new, KV_HEADS, GROUP, HEAD_DIM), kk, vv, mask, sinks)
    y = o.reshape(bsz, n_new, Q_DIM) @ w_o + b_o
    return y, kk[:, -rows:], vv[:, -rows:]


def complex_affine_combine(e1, e2):
    a1r, a1i, b1r, b1i = e1
    a2r, a2i, b2r, b2i = e2
    ar = a1r * a2r - a1i * a2i
    ai = a1r * a2i + a1i * a2r
    br = a2r * b1r - a2i * b1i + b2r
    bi = a2r * b1i + a2i * b1r + b2i
    return (ar, ai, br, bi)


def s5_mix(xn, h0_re, h0_im, a_re, a_im, log_dt, b_re, b_im, c_re, c_im, d, w_a, w_b):
    bt, seq, _ = xn.shape
    u = xn.astype(F32)
    ug = u.reshape(bt, seq, SSM_GROUPS, SSM_GROUP_CH)
    delta = jnp.exp(log_dt.astype(F32))[:, None]
    lr, li = a_re.astype(F32), a_im.astype(F32)
    mag = jnp.exp(delta * lr)
    abar_r = mag * jnp.cos(delta * li)
    abar_i = mag * jnp.sin(delta * li)
    nr, ni = abar_r - 1.0, abar_i
    den = lr * lr + li * li
    coef_r = ((nr * lr + ni * li) / den)[..., None]
    coef_i = ((ni * lr - nr * li) / den)[..., None]
    br, bi = b_re.astype(F32), b_im.astype(F32)
    bbar_r = coef_r * br - coef_i * bi
    bbar_i = coef_r * bi + coef_i * br
    bu_r = jnp.einsum('blgc,gpc->lbgp', ug, bbar_r)
    bu_i = jnp.einsum('blgc,gpc->lbgp', ug, bbar_i)
    a_r = jnp.broadcast_to(abar_r, (seq, 1, SSM_GROUPS, SSM_STATE))
    a_i = jnp.broadcast_to(abar_i, (seq, 1, SSM_GROUPS, SSM_STATE))
    cum_r, cum_i, h_r, h_i = lax.associative_scan(complex_affine_combine, (a_r, a_i, bu_r, bu_i), axis=0)
    g0r, g0i = h0_re.astype(F32), h0_im.astype(F32)
    h_r = h_r + cum_r * g0r - cum_i * g0i
    h_i = h_i + cum_r * g0i + cum_i * g0r
    y = (jnp.einsum('lbgp,gcp->blgc', h_r, c_re.astype(F32))
         - jnp.einsum('lbgp,gcp->blgc', h_i, c_im.astype(F32)))
    y = y.reshape(bt, seq, D_MODEL) + d.astype(F32) * u
    z = jax.nn.gelu(y)
    out = (z @ w_a.astype(F32)) * jax.nn.sigmoid(z @ w_b.astype(F32))
    return out.astype(xn.dtype), h_r[-1], h_i[-1]


def hier_moe(xn, w_rg, b_rg, w_re, b_re, w_gate, w_up, w_down):
    lead = xn.shape[:-1]
    t = xn.reshape(-1, D_MODEL)
    g_prob = jax.nn.softmax((t @ w_rg).astype(F32) + b_rg.astype(F32), axis=-1)
    g_val, g_idx = lax.top_k(g_prob, 1)
    e_logit = ((t @ w_re).astype(F32) + b_re.astype(F32)).reshape(-1, N_EXPERT_GROUPS, EXPERTS_PER_GROUP)
    sel = jnp.take_along_axis(e_logit, g_idx[:, :, None], axis=1)[:, 0]
    e_val, e_idx = lax.top_k(sel, TOP_K)
    e_w = jax.nn.softmax(e_val, axis=-1)
    within = jnp.sum(jax.nn.one_hot(e_idx, EXPERTS_PER_GROUP, dtype=F32) * e_w[..., None], axis=1)
    combine = (jax.nn.one_hot(g_idx[:, 0], N_EXPERT_GROUPS, dtype=F32)[:, :, None]
               * (g_val * within)[:, None, :]).reshape(-1, N_EXPERTS)
    g = jnp.einsum('td,edf->tef', t, w_gate)
    u = jnp.einsum('td,edf->tef', t, w_up)
    hid = jax.nn.silu(g) * u * combine[..., None].astype(g.dtype)
    out = jnp.einsum('tef,efd->td', hid, w_down)
    return out.reshape(*lead, D_MODEL).astype(xn.dtype)


def setup_inputs(seed: int = 0) -> dict:
    key = jax.random.key(seed)
    ks = iter(jax.random.split(key, 40))

    def nrm(shape, scale):
        return jax.random.normal(next(ks), shape, F32) * scale

    rows = min(WINDOW, PAST_LEN)
    na, ns = N_ATTN_LAYERS, N_SSM_LAYERS
    n_idx = jnp.arange(SSM_STATE, dtype=F32)
    return {
        'x_prompt': nrm((BATCH, SEQ, D_MODEL), 1.0),
        'x_sample': nrm((DEC_BATCH, DEC_SEQ, D_MODEL), 1.0),
        'cache_k': nrm((na, DEC_BATCH, rows, KV_HEADS, HEAD_DIM), 1.0),
        'cache_v': nrm((na, DEC_BATCH, rows, KV_HEADS, HEAD_DIM), 1.0),
        'state_ssm_re': nrm((ns, DEC_BATCH, SSM_GROUPS, SSM_STATE), 0.5),
        'state_ssm_im': nrm((ns, DEC_BATCH, SSM_GROUPS, SSM_STATE), 0.5),
        'norm_mix': 1.0 + nrm((DEPTH, D_MODEL), 0.05),
        'norm_ffn': 1.0 + nrm((DEPTH, D_MODEL), 0.05),
        'norm_final': 1.0 + nrm((D_MODEL,), 0.05),
        'attn_w_qkv': nrm((na, D_MODEL, QKV_DIM), D_MODEL ** -0.5),
        'attn_b_qkv': nrm((na, QKV_DIM), 0.01),
        'attn_w_o': nrm((na, Q_DIM, D_MODEL), Q_DIM ** -0.5),
        'attn_b_o': nrm((na, D_MODEL), 0.01),
        'attn_sinks': nrm((na, N_HEADS), 0.5),
        'ssm_a_re': -0.5 + nrm((ns, SSM_GROUPS, SSM_STATE), 0.01),
        'ssm_a_im': math.pi * n_idx + nrm((ns, SSM_GROUPS, SSM_STATE), 0.01),
        'ssm_log_dt': jax.random.uniform(next(ks), (ns, SSM_GROUPS), F32, math.log(1e-3), math.log(1e-1)),
        'ssm_b_re': nrm((ns, SSM_GROUPS, SSM_STATE, SSM_GROUP_CH), (2 * SSM_GROUP_CH) ** -0.5),
        'ssm_b_im': nrm((ns, SSM_GROUPS, SSM_STATE, SSM_GROUP_CH), (2 * SSM_GROUP_CH) ** -0.5),
        'ssm_c_re': nrm((ns, SSM_GROUPS, SSM_GROUP_CH, SSM_STATE), SSM_STATE ** -0.5),
        'ssm_c_im': nrm((ns, SSM_GROUPS, SSM_GROUP_CH, SSM_STATE), SSM_STATE ** -0.5),
        'ssm_d': nrm((ns, D_MODEL), 0.5),
        'ssm_w_glu_a': nrm((ns, D_MODEL, D_MODEL), D_MODEL ** -0.5),
        'ssm_w_glu_b': nrm((ns, D_MODEL, D_MODEL), D_MODEL ** -0.5),
        'moe_w_router_group': nrm((DEPTH, D_MODEL, N_EXPERT_GROUPS), D_MODEL ** -0.5),
        'moe_b_router_group': nrm((DEPTH, N_EXPERT_GROUPS), 0.01),
        'moe_w_router_expert': nrm((DEPTH, D_MODEL, N_EXPERTS), D_MODEL ** -0.5),
        'moe_b_router_expert': nrm((DEPTH, N_EXPERTS), 0.01),
        'moe_w_gate': nrm((DEPTH, N_EXPERTS, D_MODEL, EXPERT_FF), D_MODEL ** -0.5),
        'moe_w_up': nrm((DEPTH, N_EXPERTS, D_MODEL, EXPERT_FF), D_MODEL ** -0.5),
        'moe_w_down': nrm((DEPTH, N_EXPERTS, EXPERT_FF, D_MODEL), EXPERT_FF ** -0.5),
    }


def reference(x_prompt, x_sample, cache_k, cache_v, state_ssm_re, state_ssm_im,
              norm_mix, norm_ffn, norm_final,
              attn_w_qkv, attn_b_qkv, attn_w_o, attn_b_o, attn_sinks,
              ssm_a_re, ssm_a_im, ssm_log_dt, ssm_b_re, ssm_b_im, ssm_c_re, ssm_c_im,
              ssm_d, ssm_w_glu_a, ssm_w_glu_b,
              moe_w_router_group, moe_b_router_group, moe_w_router_expert, moe_b_router_expert,
              moe_w_gate, moe_w_up, moe_w_down):
    xp, xs = x_prompt, x_sample
    k_p, k_s, v_p, v_s = [], [], [], []
    hr_p, hr_s, hi_p, hi_s = [], [], [], []
    for layer in range(DEPTH):
        np_ = rmsnorm(xp, norm_mix[layer])
        ns_ = rmsnorm(xs, norm_mix[layer])
        if layer % 2 == 0:
            a = layer // 2
            attn = (attn_w_qkv[a], attn_b_qkv[a], attn_w_o[a], attn_b_o[a], attn_sinks[a])
            mp, kp_new, vp_new = swa_prompt(np_, *attn)
            ms, ks_new, vs_new = swa_sample(ns_, cache_k[a], cache_v[a], *attn)
            k_p.append(kp_new)
            v_p.append(vp_new)
            k_s.append(ks_new)
            v_s.append(vs_new)
        else:
            s = layer // 2
            ssm = (ssm_a_re[s], ssm_a_im[s], ssm_log_dt[s], ssm_b_re[s], ssm_b_im[s],
                   ssm_c_re[s], ssm_c_im[s], ssm_d[s], ssm_w_glu_a[s], ssm_w_glu_b[s])
            h0 = jnp.zeros((xp.shape[0], SSM_GROUPS, SSM_STATE), F32)
            mp, rp_new, ip_new = s5_mix(np_, h0, h0, *ssm)
            ms, rs_new, is_new = s5_mix(ns_, state_ssm_re[s], state_ssm_im[s], *ssm)
            hr_p.append(rp_new)
            hi_p.append(ip_new)
            hr_s.append(rs_new)
            hi_s.append(is_new)
        xp = xp + mp
        xs = xs + ms
        moe_w = (moe_w_router_group[layer], moe_b_router_group[layer], moe_w_router_expert[layer],
                 moe_b_router_expert[layer], moe_w_gate[layer], moe_w_up[layer], moe_w_down[layer])
        xp = xp + hier_moe(rmsnorm(xp, norm_ffn[layer]), *moe_w)
        xs = xs + hier_moe(rmsnorm(xs, norm_ffn[layer]), *moe_w)
    y_prompt = rmsnorm(xp, norm_final)
    y_sample = rmsnorm(xs, norm_final)
    return (y_prompt, y_sample,
            jnp.stack(k_p), jnp.stack(k_s), jnp.stack(v_p), jnp.stack(v_s),
            jnp.stack(hr_p), jnp.stack(hr_s), jnp.stack(hi_p), jnp.stack(hi_s))
```

```python
import functools
import math

import jax
import jax.numpy as jnp
from jax import lax
from jax.experimental import pallas as pl
from jax.experimental.pallas import tpu as pltpu

F32 = jnp.float32
BF16 = jnp.bfloat16

D_MODEL = 1024
N_HEADS = 16
KV_HEADS = 4
HEAD_DIM = 64
Q_DIM = N_HEADS * HEAD_DIM
KV_DIM = KV_HEADS * HEAD_DIM
QKV_DIM = Q_DIM + 2 * KV_DIM
WINDOW = 128
BLOCK = 128
ROPE_THETA = 10000.0
PAST_LEN = 16384
SSM_GROUP_CH = 16
SSM_GROUPS = D_MODEL // SSM_GROUP_CH
SSM_STATE = 64
N_EXPERT_GROUPS = 4
EXPERTS_PER_GROUP = 8
N_EXPERTS = N_EXPERT_GROUPS * EXPERTS_PER_GROUP
EXPERT_FF = 128
NORM_EPS = 1e-5

LANES = 128
VMEM_LIMIT = 56 * 1024 * 1024
S5_CHUNK = 16
S5_GB = 8
MOE_CHUNKS = 4


def _cparams(sem):
    return pltpu.CompilerParams(dimension_semantics=sem, vmem_limit_bytes=VMEM_LIMIT)


def _rms(x, g):
    return x * lax.rsqrt(jnp.mean(x * x, axis=-1, keepdims=True) + NORM_EPS) * g


def _dot(a, b):
    return jnp.dot(a, b, preferred_element_type=F32)


def _dot_nt(a, b):
    return lax.dot_general(a, b, (((1,), (1,)), ((), ())), preferred_element_type=F32)


def _sigmoid(x):
    return 1.0 / (1.0 + jnp.exp(-x))


def _qkv_kernel(x_ref, g_ref, w_ref, b_ref, cos_ref, sin_ref, q_ref, k_ref, v_ref):
    xn = _rms(x_ref[...], g_ref[...])
    qkv = _dot(xn.astype(BF16), w_ref[...]) + b_ref[...]
    cos = cos_ref[...]
    sin = sin_ref[...]
    lane = lax.broadcasted_iota(jnp.int32, cos.shape, 1)
    first_half = (lane % HEAD_DIM) < (HEAD_DIM // 2)
    n_rot = (Q_DIM + KV_DIM) // LANES
    for c in range(n_rot):
        blk = qkv[:, c * LANES:(c + 1) * LANES]
        partner = jnp.where(first_half,
                            pltpu.roll(blk, LANES - HEAD_DIM // 2, 1),
                            pltpu.roll(blk, HEAD_DIM // 2, 1))
        rot = blk * cos + partner * sin
        if c < Q_DIM // LANES:
            q_ref[:, c * LANES:(c + 1) * LANES] = rot.astype(q_ref.dtype)
        else:
            k_ref[:, c * LANES - Q_DIM:(c + 1) * LANES - Q_DIM] = rot
    v_ref[...] = qkv[:, Q_DIM + KV_DIM:]


def _qkv_call(x2d, gain, w_bf, bias, cos_t, sin_t, q_dtype, tm):
    t = x2d.shape[0]
    n_pos = cos_t.shape[0] // tm
    return pl.pallas_call(
        _qkv_kernel,
        grid=(t // tm,),
        in_specs=[
            pl.BlockSpec((tm, D_MODEL), lambda i: (i, 0)),
            pl.BlockSpec((1, D_MODEL), lambda i: (0, 0)),
            pl.BlockSpec((D_MODEL, QKV_DIM), lambda i: (0, 0)),
            pl.BlockSpec((1, QKV_DIM), lambda i: (0, 0)),
            pl.BlockSpec((tm, LANES), lambda i: (i % n_pos, 0)),
            pl.BlockSpec((tm, LANES), lambda i: (i % n_pos, 0)),
        ],
        out_specs=[
            pl.BlockSpec((tm, Q_DIM), lambda i: (i, 0)),
            pl.BlockSpec((tm, KV_DIM), lambda i: (i, 0)),
            pl.BlockSpec((tm, KV_DIM), lambda i: (i, 0)),
        ],
        out_shape=[
            jax.ShapeDtypeStruct((t, Q_DIM), q_dtype),
            jax.ShapeDtypeStruct((t, KV_DIM), F32),
            jax.ShapeDtypeStruct((t, KV_DIM), F32),
        ],
        compiler_params=_cparams(("parallel",)),
        name="qkv_rope",
    )(x2d, gain, w_bf, bias, cos_t, sin_t)


def _split_kv_pair(pair_f32, head_in_low_half):
    lane = lax.broadcasted_iota(jnp.int32, pair_f32.shape, 1)
    if head_in_low_half:
        lo = jnp.where(lane < HEAD_DIM, pair_f32, 0.0)
        hi = pltpu.roll(lo, HEAD_DIM, 1)
    else:
        hi = jnp.where(lane >= HEAD_DIM, pair_f32, 0.0)
        lo = pltpu.roll(hi, HEAD_DIM, 1)
    return lo.astype(BF16), hi.astype(BF16)


def _attend_group(q0, q1, k_pair, v_pair, head_in_low_half, mask, sinks4):
    rows = q0.shape[0]
    k_lo, k_hi = _split_kv_pair(k_pair, head_in_low_half)
    v_lo, v_hi = _split_kv_pair(v_pair, head_in_low_half)
    qcat = jnp.concatenate([q0, q1], axis=0).astype(BF16)
    row = lax.broadcasted_iota(jnp.int32, (2 * rows, 1), 0)
    top = row < rows
    acc = None
    for half, (kk, vv) in enumerate(((k_lo, v_lo), (k_hi, v_hi))):
        sink = jnp.where(top, sinks4[half], sinks4[2 + half])
        s = _dot_nt(qcat, kk) * (HEAD_DIM ** -0.5)
        s = jnp.where(mask, s, -jnp.inf)
        m = jnp.maximum(jnp.max(s, axis=-1, keepdims=True), sink)
        p = jnp.exp(s - m)
        denom = jnp.sum(p, axis=-1, keepdims=True) + jnp.exp(sink - m)
        o = _dot(p.astype(BF16), vv) * (1.0 / denom)
        acc = o if acc is None else acc + o
    return acc[:rows], acc[rows:]


def _attn_prompt_kernel(sink_ref, q_ref, kp_ref, kc_ref, vp_ref, vc_ref, o_ref):
    n = pl.program_id(1)
    k2 = jnp.concatenate([kp_ref[0], kc_ref[0]], axis=0)
    v2 = jnp.concatenate([vp_ref[0], vc_ref[0]], axis=0)
    row = lax.broadcasted_iota(jnp.int32, (2 * BLOCK, 2 * BLOCK), 0) % BLOCK
    col = lax.broadcasted_iota(jnp.int32, (2 * BLOCK, 2 * BLOCK), 1)
    mask = (col >= row) & (col <= row + WINDOW) & ((n > 0) | (col >= BLOCK))
    group = N_HEADS // KV_HEADS
    for j in range(KV_HEADS):
        pj = (j // 2) * LANES
        base = j * group * HEAD_DIM
        q0 = q_ref[0, :, base:base + LANES]
        q1 = q_ref[0, :, base + LANES:base + 2 * LANES]
        sinks4 = [sink_ref[j * group + 0], sink_ref[j * group + 1],
                  sink_ref[j * group + 2], sink_ref[j * group + 3]]
        o0, o1 = _attend_group(q0, q1, k2[:, pj:pj + LANES], v2[:, pj:pj + LANES],
                               j % 2 == 0, mask, sinks4)
        o_ref[0, :, base:base + LANES] = o0.astype(o_ref.dtype)
        o_ref[0, :, base + LANES:base + 2 * LANES] = o1.astype(o_ref.dtype)


def _attn_prompt_call(sinks, q, k, v):
    b, l, _ = q.shape
    nb = l // BLOCK
    kv_cur = pl.BlockSpec((1, BLOCK, KV_DIM), lambda i, n: (i, n, 0))
    kv_prev = pl.BlockSpec((1, BLOCK, KV_DIM), lambda i, n: (i, jnp.maximum(n - 1, 0), 0))
    return pl.pallas_call(
        _attn_prompt_kernel,
        grid=(b, nb),
        in_specs=[
            pl.BlockSpec(memory_space=pltpu.SMEM),
            pl.BlockSpec((1, BLOCK, Q_DIM), lambda i, n: (i, n, 0)),
            kv_prev, kv_cur, kv_prev, kv_cur,
        ],
        out_specs=pl.BlockSpec((1, BLOCK, Q_DIM), lambda i, n: (i, n, 0)),
        out_shape=jax.ShapeDtypeStruct((b, l, Q_DIM), BF16),
        compiler_params=_cparams(("parallel", "parallel")),
        name="attn_prompt",
    )(sinks, q, k, k, v, v)


def _attn_sample_kernel(n_new, seqs, sink_ref, q_ref, kn_ref, vn_ref, ck_ref, cv_ref,
                        o_ref, nk_ref, nv_ref):
    rows = ck_ref.shape[1]
    keys = 2 * rows
    row = lax.broadcasted_iota(jnp.int32, (2 * n_new, keys), 0) % n_new
    col = lax.broadcasted_iota(jnp.int32, (2 * n_new, keys), 1)
    mask = ((col < rows) & (col >= row)) | ((col >= rows) & (col - rows <= row))
    group = N_HEADS // KV_HEADS
    pad = jnp.zeros((rows - n_new, KV_DIM), F32)

    def body(sb, carry):
        r0 = pl.multiple_of(sb * n_new, n_new)
        k_new = kn_ref[pl.ds(r0, n_new), :]
        v_new = vn_ref[pl.ds(r0, n_new), :]
        k_all = jnp.concatenate([ck_ref[sb], k_new, pad], axis=0)
        v_all = jnp.concatenate([cv_ref[sb], v_new, pad], axis=0)
        for j in range(KV_HEADS):
            pj = (j // 2) * LANES
            base = j * group * HEAD_DIM
            q0 = q_ref[pl.ds(r0, n_new), base:base + LANES]
            q1 = q_ref[pl.ds(r0, n_new), base + LANES:base + 2 * LANES]
            sinks4 = [sink_ref[j * group + 0], sink_ref[j * group + 1],
                      sink_ref[j * group + 2], sink_ref[j * group + 3]]
            o0, o1 = _attend_group(q0, q1, k_all[:, pj:pj + LANES], v_all[:, pj:pj + LANES],
                                   j % 2 == 0, mask, sinks4)
            o_ref[pl.ds(r0, n_new), base:base + LANES] = o0
            o_ref[pl.ds(r0, n_new), base + LANES:base + 2 * LANES] = o1
        nk_ref[sb, 0:rows - n_new, :] = ck_ref[sb, n_new:rows, :]
        nk_ref[sb, rows - n_new:rows, :] = k_new
        nv_ref[sb, 0:rows - n_new, :] = cv_ref[sb, n_new:rows, :]
        nv_ref[sb, rows - n_new:rows, :] = v_new
        return carry

    lax.fori_loop(0, seqs, body, 0)


def _attn_sample_call(sinks, q, k_new, v_new, cache_k, cache_v, n_new, seqs):
    b, rows, _ = cache_k.shape
    tok = pl.BlockSpec((seqs * n_new, Q_DIM), lambda i: (i, 0))
    tok_kv = pl.BlockSpec((seqs * n_new, KV_DIM), lambda i: (i, 0))
    cache = pl.BlockSpec((seqs, rows, KV_DIM), lambda i: (i, 0, 0))
    return pl.pallas_call(
        functools.partial(_attn_sample_kernel, n_new, seqs),
        grid=(b // seqs,),
        in_specs=[pl.BlockSpec(memory_space=pltpu.SMEM), tok, tok_kv, tok_kv, cache, cache],
        out_specs=[tok, cache, cache],
        out_shape=[
            jax.ShapeDtypeStruct((b * n_new, Q_DIM), F32),
            jax.ShapeDtypeStruct((b, rows, KV_DIM), F32),
            jax.ShapeDtypeStruct((b, rows, KV_DIM), F32),
        ],
        compiler_params=_cparams(("parallel",)),
        name="attn_sample",
    )(sinks, q, k_new, v_new, cache_k, cache_v)


def _router_epilogue(x1, gn_ref, wrh_ref, wrl_ref, br_ref, xn_ref, comb_ref):
    xn = _rms(x1, gn_ref[...])
    xh = xn.astype(BF16)
    xn_ref[...] = xh
    xl = (xn - xh.astype(F32)).astype(BF16)
    logits = _dot(xh, wrh_ref[...]) + _dot(xl, wrh_ref[...]) + _dot(xh, wrl_ref[...]) + br_ref[...]
    lane = lax.broadcasted_iota(jnp.int32, logits.shape, 1).astype(F32)
    big = jnp.float32(4 * LANES)
    neg = -jnp.inf
    gl = jnp.where((lane >= N_EXPERTS) & (lane < N_EXPERTS + N_EXPERT_GROUPS), logits, neg)
    gmax = jnp.max(gl, axis=-1, keepdims=True)
    g_val = 1.0 / jnp.sum(jnp.exp(gl - gmax), axis=-1, keepdims=True)
    g_idx = jnp.min(jnp.where(gl == gmax, lane, big), axis=-1, keepdims=True) - N_EXPERTS
    lo = g_idx * EXPERTS_PER_GROUP
    el = jnp.where((lane >= lo) & (lane < lo + EXPERTS_PER_GROUP), logits, neg)
    e1 = jnp.max(el, axis=-1, keepdims=True)
    i1 = jnp.min(jnp.where(el == e1, lane, big), axis=-1, keepdims=True)
    el2 = jnp.where(lane == i1, neg, el)
    e2 = jnp.max(el2, axis=-1, keepdims=True)
    i2 = jnp.min(jnp.where(el2 == e2, lane, big), axis=-1, keepdims=True)
    t = jnp.exp(e2 - e1)
    w1 = 1.0 / (1.0 + t)
    w2 = t / (1.0 + t)
    comb_ref[...] = g_val * (jnp.where(lane == i1, w1, 0.0) + jnp.where(lane == i2, w2, 0.0))


def _oproj_kernel(o_ref, x_ref, wo_ref, bo_ref, gn_ref, wrh_ref, wrl_ref, br_ref,
                  x1_ref, xn_ref, comb_ref):
    x1 = x_ref[...] + _dot(o_ref[...].astype(BF16), wo_ref[...]) + bo_ref[...]
    x1_ref[...] = x1
    _router_epilogue(x1, gn_ref, wrh_ref, wrl_ref, br_ref, xn_ref, comb_ref)


def _gelu_tanh(x):
    return x * (0.5 * (1.0 + jnp.tanh(math.sqrt(2.0 / math.pi) * (x + 0.044715 * (x * x * x)))))


def _glu_kernel(y_ref, x_ref, gm_ref, d_ref, wa_ref, wb_ref, gn_ref, wrh_ref, wrl_ref, br_ref,
                x1_ref, xn_ref, comb_ref):
    x = x_ref[...]
    u = _rms(x, gm_ref[...])
    z = _gelu_tanh(y_ref[...] + d_ref[...] * u).astype(BF16)
    mix = _dot(z, wa_ref[...]) * _sigmoid(_dot(z, wb_ref[...]))
    x1 = x + mix
    x1_ref[...] = x1
    _router_epilogue(x1, gn_ref, wrh_ref, wrl_ref, br_ref, xn_ref, comb_ref)


def _row_spec(tm, width):
    return pl.BlockSpec((tm, width), lambda i: (i, 0))


def _const_spec(shape):
    return pl.BlockSpec(shape, lambda i: (0,) * len(shape))


def _router_specs():
    return [_const_spec((1, D_MODEL)), _const_spec((D_MODEL, LANES)),
            _const_spec((D_MODEL, LANES)), _const_spec((1, LANES))]


def _mixer_out(t, tm):
    specs = [_row_spec(tm, D_MODEL), _row_spec(tm, D_MODEL), _row_spec(tm, LANES)]
    shapes = [jax.ShapeDtypeStruct((t, D_MODEL), F32), jax.ShapeDtypeStruct((t, D_MODEL), BF16),
              jax.ShapeDtypeStruct((t, LANES), F32)]
    return specs, shapes


def _oproj_call(o2d, x2d, wo_bf, bo, gn, wrh, wrl, br, tm):
    t = x2d.shape[0]
    out_specs, out_shape = _mixer_out(t, tm)
    return pl.pallas_call(
        _oproj_kernel,
        grid=(t // tm,),
        in_specs=[_row_spec(tm, Q_DIM), _row_spec(tm, D_MODEL), _const_spec((Q_DIM, D_MODEL)),
                  _const_spec((1, D_MODEL))] + _router_specs(),
        out_specs=out_specs, out_shape=out_shape,
        compiler_params=_cparams(("parallel",)),
        name="oproj_router",
    )(o2d, x2d, wo_bf, bo, gn, wrh, wrl, br)


def _glu_call(y2d, x2d, gm, d, wa_bf, wb_bf, gn, wrh, wrl, br, tm):
    t = x2d.shape[0]
    out_specs, out_shape = _mixer_out(t, tm)
    return pl.pallas_call(
        _glu_kernel,
        grid=(t // tm,),
        in_specs=[_row_spec(tm, D_MODEL), _row_spec(tm, D_MODEL), _const_spec((1, D_MODEL)),
                  _const_spec((1, D_MODEL)), _const_spec((D_MODEL, D_MODEL)),
                  _const_spec((D_MODEL, D_MODEL))] + _router_specs(),
        out_specs=out_specs, out_shape=out_shape,
        compiler_params=_cparams(("parallel",)),
        name="glu_router",
    )(y2d, x2d, gm, d, wa_bf, wb_bf, gn, wrh, wrl, br)


def _moe_kernel(emit_x, xn_ref, comb_ref, x_ref, wg_ref, wu_ref, wd_ref, ex_ref, gnext_ref,
                *out_and_scratch):
    acc_ref = out_and_scratch[-1]
    outs = out_and_scratch[:-1]
    c = pl.program_id(1)

    @pl.when(c == 0)
    def _():
        acc_ref[...] = jnp.zeros_like(acc_ref)

    xb = xn_ref[...]
    g = _dot(xb, wg_ref[...])
    u = _dot(xb, wu_ref[...])
    comb = comb_ref[...]
    ch = comb.astype(BF16)
    cl = (comb - ch.astype(F32)).astype(BF16)
    scale = _dot(ch, ex_ref[...]) + _dot(cl, ex_ref[...])
    hid = (g * _sigmoid(g)) * u * scale
    acc_ref[...] += _dot(hid.astype(BF16), wd_ref[...])

    @pl.when(c == pl.num_programs(1) - 1)
    def _():
        x2 = x_ref[...] + acc_ref[...]
        normed = _rms(x2, gnext_ref[...])
        if emit_x:
            outs[0][...] = x2
            outs[1][...] = normed.astype(outs[1].dtype)
        else:
            outs[0][...] = normed.astype(outs[0].dtype)


def _moe_call(xn_bf, comb, x2d, wg, wu, wd, expand, gnext, emit_x, tm):
    t = x2d.shape[0]
    ff = N_EXPERTS * EXPERT_FF // MOE_CHUNKS
    row = lambda w: pl.BlockSpec((tm, w), lambda i, c: (i, 0))
    if emit_x:
        out_specs = [row(D_MODEL), row(D_MODEL)]
        out_shape = [jax.ShapeDtypeStruct((t, D_MODEL), F32), jax.ShapeDtypeStruct((t, D_MODEL), BF16)]
    else:
        out_specs = [row(D_MODEL)]
        out_shape = [jax.ShapeDtypeStruct((t, D_MODEL), F32)]
    return pl.pallas_call(
        functools.partial(_moe_kernel, emit_x),
        grid=(t // tm, MOE_CHUNKS),
        in_specs=[
            row(D_MODEL), row(LANES), row(D_MODEL),
            pl.BlockSpec((D_MODEL, ff), lambda i, c: (0, c)),
            pl.BlockSpec((D_MODEL, ff), lambda i, c: (0, c)),
            pl.BlockSpec((ff, D_MODEL), lambda i, c: (c, 0)),
            pl.BlockSpec((LANES, ff), lambda i, c: (0, c)),
            pl.BlockSpec((1, D_MODEL), lambda i, c: (0, 0)),
        ],
        out_specs=out_specs, out_shape=out_shape,
        scratch_shapes=[pltpu.VMEM((tm, D_MODEL), F32)],
        compiler_params=_cparams(("parallel", "arbitrary")),
        name="moe_dense",
    )(xn_bf, comb, x2d, wg, wu, wd, expand, gnext)


def _s5_state_in(u_ref_val, wre_ref, wim_ref, sre_ref, sim_ref, pair_w):
    for m in range(S5_GB // 2):
        up = u_ref_val(m * pair_w, pair_w)
        sre_ref[:, m * LANES:(m + 1) * LANES] = _dot(up, wre_ref[m])
        sim_ref[:, m * LANES:(m + 1) * LANES] = _dot(up, wim_ref[m])


def _s5_outputs(u_ref_val, hre, him, m_ref, gre_ref, gim_ref, y_store, pair_w):
    gw = pair_w // 2
    for m in range(S5_GB // 2):
        hr = hre(m).astype(BF16)
        hi = him(m).astype(BF16)
        y = _dot(hr, gre_ref[m]) + _dot(hi, gim_ref[m])
        y0 = y[:, :gw] + _dot(u_ref_val(m * pair_w, gw), m_ref[2 * m])
        y1 = y[:, gw:] + _dot(u_ref_val(m * pair_w + gw, gw), m_ref[2 * m + 1])
        y_store(m * pair_w, gw, y0)
        y_store(m * pair_w + gw, gw, y1)


def _s5_prompt_kernel(u_ref, wre_ref, wim_ref, m_ref, gre_ref, gim_ref, aqr_ref, aqi_ref,
                      y_ref, her_ref, hei_ref, sre, sim, hre, him):
    pair_w = 2 * S5_CHUNK * SSM_GROUP_CH
    n_chunks = u_ref.shape[1]
    u_val = lambda off, w: u_ref[0, :, off:off + w]
    _s5_state_in(u_val, wre_ref, wim_ref, sre, sim, pair_w)
    ar = aqr_ref[...]
    ai = aqi_ref[...]

    def step(n, carry):
        hr, hi = carry
        hre[pl.ds(n, 1), :] = hr
        him[pl.ds(n, 1), :] = hi
        sr = sre[pl.ds(n, 1), :]
        si = sim[pl.ds(n, 1), :]
        return ar * hr - ai * hi + sr, ar * hi + ai * hr + si

    zero = jnp.zeros(ar.shape, F32)
    hr, hi = lax.fori_loop(0, n_chunks, step, (zero, zero))
    her_ref[0] = hr
    hei_ref[0] = hi

    def y_store(off, w, val):
        y_ref[0, :, off:off + w] = val

    _s5_outputs(u_val, lambda m: hre[:, m * LANES:(m + 1) * LANES],
                lambda m: him[:, m * LANES:(m + 1) * LANES],
                m_ref, gre_ref, gim_ref, y_store, pair_w)


def _s5_sample_kernel(n_new, u_ref, h0r_ref, h0i_ref, wre_ref, wim_ref, m_ref, gre_ref, gim_ref,
                      aqr_ref, aqi_ref, y_ref, hnr_ref, hni_ref, sre, sim):
    pair_w = 2 * n_new * SSM_GROUP_CH
    u_val = lambda off, w: u_ref[:, off:off + w]
    _s5_state_in(u_val, wre_ref, wim_ref, sre, sim, pair_w)
    ar = aqr_ref[...]
    ai = aqi_ref[...]
    h0r = h0r_ref[...]
    h0i = h0i_ref[...]
    hnr_ref[...] = ar * h0r - ai * h0i + sre[...]
    hni_ref[...] = ar * h0i + ai * h0r + sim[...]

    def y_store(off, w, val):
        y_ref[:, off:off + w] = val

    _s5_outputs(u_val, lambda m: h0r_ref[:, m * LANES:(m + 1) * LANES],
                lambda m: h0i_ref[:, m * LANES:(m + 1) * LANES],
                m_ref, gre_ref, gim_ref, y_store, pair_w)


def _s5_weight_specs(q, idx):
    qc = q * SSM_GROUP_CH
    np_ = S5_GB // 2
    st = S5_GB * SSM_STATE
    return [
        pl.BlockSpec((np_, 2 * qc, LANES), lambda *a: (idx(*a), 0, 0)),
        pl.BlockSpec((np_, 2 * qc, LANES), lambda *a: (idx(*a), 0, 0)),
        pl.BlockSpec((S5_GB, qc, qc), lambda *a: (idx(*a), 0, 0)),
        pl.BlockSpec((np_, LANES, 2 * qc), lambda *a: (idx(*a), 0, 0)),
        pl.BlockSpec((np_, LANES, 2 * qc), lambda *a: (idx(*a), 0, 0)),
        pl.BlockSpec((1, st), lambda *a: (0, idx(*a))),
        pl.BlockSpec((1, st), lambda *a: (0, idx(*a))),
    ]


def _s5_prompt_call(uf, w):
    b, n_chunks, width = uf.shape
    gbw = S5_GB * S5_CHUNK * SSM_GROUP_CH
    st = S5_GB * SSM_STATE
    n_gb = SSM_GROUPS // S5_GB
    gb_of = lambda g, i: g
    return pl.pallas_call(
        _s5_prompt_kernel,
        grid=(n_gb, b),
        in_specs=[pl.BlockSpec((1, n_chunks, gbw), lambda g, i: (i, 0, g))] + _s5_weight_specs(S5_CHUNK, gb_of),
        out_specs=[
            pl.BlockSpec((1, n_chunks, gbw), lambda g, i: (i, 0, g)),
            pl.BlockSpec((1, 1, st), lambda g, i: (i, 0, g)),
            pl.BlockSpec((1, 1, st), lambda g, i: (i, 0, g)),
        ],
        out_shape=[
            jax.ShapeDtypeStruct((b, n_chunks, width), F32),
            jax.ShapeDtypeStruct((b, 1, SSM_GROUPS * SSM_STATE), F32),
            jax.ShapeDtypeStruct((b, 1, SSM_GROUPS * SSM_STATE), F32),
        ],
        scratch_shapes=[pltpu.VMEM((n_chunks, st), F32)] * 4,
        compiler_params=_cparams(("parallel", "parallel")),
        name="s5_prompt",
    )(uf, *w)


def _s5_sample_call(uf, h0r, h0i, w, n_new):
    b, width = uf.shape
    gbw = S5_GB * n_new * SSM_GROUP_CH
    st = S5_GB * SSM_STATE
    n_gb = SSM_GROUPS // S5_GB
    gb_of = lambda g: g
    state = pl.BlockSpec((b, st), lambda g: (0, g))
    return pl.pallas_call(
        functools.partial(_s5_sample_kernel, n_new),
        grid=(n_gb,),
        in_specs=[pl.BlockSpec((b, gbw), lambda g: (0, g)), state, state] + _s5_weight_specs(n_new, gb_of),
        out_specs=[pl.BlockSpec((b, gbw), lambda g: (0, g)), state, state],
        out_shape=[
            jax.ShapeDtypeStruct((b, width), F32),
            jax.ShapeDtypeStruct((b, SSM_GROUPS * SSM_STATE), F32),
            jax.ShapeDtypeStruct((b, SSM_GROUPS * SSM_STATE), F32),
        ],
        scratch_shapes=[pltpu.VMEM((b, st), F32)] * 2,
        compiler_params=_cparams(("parallel",)),
        name="s5_sample",
    )(uf, h0r, h0i, *w)


def _block_diag_pairs(w):
    g, r, n = w.shape
    w2 = w.reshape(g // 2, 2, r, n)
    z = jnp.zeros((g // 2, r, n), w.dtype)
    top = jnp.concatenate([w2[:, 0], z], axis=2)
    bot = jnp.concatenate([z, w2[:, 1]], axis=2)
    return jnp.concatenate([top, bot], axis=1)


def _s5_discretize(a_re, a_im, log_dt, b_re, b_im):
    delta = jnp.exp(log_dt.astype(F32))[:, None]
    lr, li = a_re.astype(F32), a_im.astype(F32)
    mag = jnp.exp(delta * lr)
    abar_r = mag * jnp.cos(delta * li)
    abar_i = mag * jnp.sin(delta * li)
    nr, ni = abar_r - 1.0, abar_i
    den = lr * lr + li * li
    coef_r = ((nr * lr + ni * li) / den)[..., None]
    coef_i = ((ni * lr - nr * li) / den)[..., None]
    br, bi = b_re.astype(F32), b_im.astype(F32)
    return abar_r, abar_i, coef_r * br - coef_i * bi, coef_r * bi + coef_i * br


def _s5_chunk_weights(abar_r, abar_i, bbar_r, bbar_i, c_re, c_im, q):
    g, p = abar_r.shape
    c = SSM_GROUP_CH
    hi = lax.Precision.HIGHEST
    pows_r, pows_i = [jnp.ones_like(abar_r)], [jnp.zeros_like(abar_r)]
    for _ in range(q):
        pr, pi = pows_r[-1], pows_i[-1]
        pows_r.append(pr * abar_r - pi * abar_i)
        pows_i.append(pr * abar_i + pi * abar_r)
    pw_r, pw_i = jnp.stack(pows_r), jnp.stack(pows_i)
    rev_r, rev_i = pw_r[:q][::-1][..., None], pw_i[:q][::-1][..., None]
    wst_r = (rev_r * bbar_r[None] - rev_i * bbar_i[None]).transpose(1, 0, 3, 2).reshape(g, q * c, p)
    wst_i = (rev_r * bbar_i[None] + rev_i * bbar_r[None]).transpose(1, 0, 3, 2).reshape(g, q * c, p)
    x_r = pw_r[:q, :, :, None] * bbar_r[None] - pw_i[:q, :, :, None] * bbar_i[None]
    x_i = pw_r[:q, :, :, None] * bbar_i[None] + pw_i[:q, :, :, None] * bbar_r[None]
    cr, ci = c_re.astype(F32), c_im.astype(F32)
    ker = (jnp.einsum('gcp,tgpd->gtdc', cr, x_r, precision=hi)
           - jnp.einsum('gcp,tgpd->gtdc', ci, x_i, precision=hi))
    idx = jnp.arange(q)
    tau = idx[None, :] - idx[:, None]
    m5 = jnp.where((tau >= 0)[None, :, :, None, None], ker[:, jnp.clip(tau, 0, q - 1)], 0.0)
    m = m5.transpose(0, 1, 3, 2, 4).reshape(g, q * c, q * c)
    cr_t, ci_t = cr.transpose(0, 2, 1)[:, :, None, :], ci.transpose(0, 2, 1)[:, :, None, :]
    pr_t = pw_r[1:q + 1].transpose(1, 2, 0)[..., None]
    pi_t = pw_i[1:q + 1].transpose(1, 2, 0)[..., None]
    g_re = (cr_t * pr_t - ci_t * pi_t).reshape(g, p, q * c)
    g_im = (-(cr_t * pi_t + ci_t * pr_t)).reshape(g, p, q * c)
    return (_block_diag_pairs(wst_r).astype(BF16), _block_diag_pairs(wst_i).astype(BF16),
            m.astype(BF16), _block_diag_pairs(g_re).astype(BF16), _block_diag_pairs(g_im).astype(BF16),
            pw_r[q].reshape(1, g * p), pw_i[q].reshape(1, g * p))


def _rope_tables(pos):
    half = HEAD_DIM // 2
    inv = 1.0 / (ROPE_THETA ** (jnp.arange(half, dtype=F32) * (2.0 / HEAD_DIM)))
    ang = pos.astype(F32)[:, None] * inv[None, :]
    cos, sin = jnp.cos(ang), jnp.sin(ang)
    return jnp.tile(cos, (1, 4)), jnp.concatenate([-sin, sin, -sin, sin], axis=1)


def _router_weights(w_rg, b_rg, w_re, b_re):
    pad = LANES - N_EXPERTS - N_EXPERT_GROUPS
    w = jnp.concatenate([w_re, w_rg, jnp.zeros((D_MODEL, pad), F32)], axis=1)
    b = jnp.concatenate([b_re, b_rg, jnp.zeros((pad,), F32)]).reshape(1, LANES)
    wh = w.astype(BF16)
    wl = (w - wh.astype(F32)).astype(BF16)
    return wh, wl, b


def _moe_weights(w_gate, w_up, w_down):
    ff = N_EXPERTS * EXPERT_FF
    wg = w_gate.astype(BF16).transpose(1, 0, 2).reshape(D_MODEL, ff)
    wu = w_up.astype(BF16).transpose(1, 0, 2).reshape(D_MODEL, ff)
    wd = w_down.astype(BF16).reshape(ff, D_MODEL)
    return wg, wu, wd


def _expand_matrix():
    col = jnp.arange(N_EXPERTS * EXPERT_FF) // EXPERT_FF
    return (jnp.arange(LANES)[:, None] == col[None, :]).astype(BF16)


def kernel(x_prompt, x_sample, cache_k, cache_v, state_ssm_re, state_ssm_im, norm_mix, norm_ffn, norm_final, attn_w_qkv, attn_b_qkv, attn_w_o, attn_b_o, attn_sinks, ssm_a_re, ssm_a_im, ssm_log_dt, ssm_b_re, ssm_b_im, ssm_c_re, ssm_c_im, ssm_d, ssm_w_glu_a, ssm_w_glu_b, moe_w_router_group, moe_b_router_group, moe_w_router_expert, moe_b_router_expert, moe_w_gate, moe_w_up, moe_w_down):
    bsz, seq, _ = x_prompt.shape
    dbs, n_new, _ = x_sample.shape
    rows = cache_k.shape[2]
    tp, ts = bsz * seq, dbs * n_new
    xp = x_prompt.reshape(tp, D_MODEL)
    xs = x_sample.reshape(ts, D_MODEL)
    tm = 512

    row1 = lambda v: v.reshape(1, -1).astype(F32)
    routers = [_router_weights(moe_w_router_group[l], moe_b_router_group[l],
                               moe_w_router_expert[l], moe_b_router_expert[l]) for l in range(2)]
    experts = [_moe_weights(moe_w_gate[l], moe_w_up[l], moe_w_down[l]) for l in range(2)]
    expand = _expand_matrix()

    wqkv = attn_w_qkv[0].astype(BF16)
    bqkv = row1(attn_b_qkv[0])
    wo = attn_w_o[0].astype(BF16)
    bo = row1(attn_b_o[0])
    sinks = attn_sinks[0].astype(F32)
    g_mix0, g_ffn0 = row1(norm_mix[0]), row1(norm_ffn[0])
    cos_p, sin_p = _rope_tables(jnp.arange(seq, dtype=jnp.int32))
    pos_s = jnp.tile(PAST_LEN + jnp.arange(n_new, dtype=jnp.int32), dbs)
    cos_s, sin_s = _rope_tables(pos_s)

    qp, kp, vp = _qkv_call(xp, g_mix0, wqkv, bqkv, cos_p, sin_p, BF16, tm)
    qs, ks, vs = _qkv_call(xs, g_mix0, wqkv, bqkv, cos_s, sin_s, F32, tm)
    op = _attn_prompt_call(sinks, qp.reshape(bsz, seq, Q_DIM), kp.reshape(bsz, seq, KV_DIM),
                           vp.reshape(bsz, seq, KV_DIM))
    os_, nks, nvs = _attn_sample_call(sinks, qs, ks, vs, cache_k[0].reshape(dbs, rows, KV_DIM),
                                      cache_v[0].reshape(dbs, rows, KV_DIM), n_new, 8)
    wrh, wrl, br = routers[0]
    xp1, xpn, cp = _oproj_call(op.reshape(tp, Q_DIM), xp, wo, bo, g_ffn0, wrh, wrl, br, tm)
    xs1, xsn, cs = _oproj_call(os_, xs, wo, bo, g_ffn0, wrh, wrl, br, tm)
    g_mix1 = row1(norm_mix[1])
    xp2, up = _moe_call(xpn, cp, xp1, *experts[0], expand, g_mix1, True, tm)
    xs2, us = _moe_call(xsn, cs, xs1, *experts[0], expand, g_mix1, True, tm)

    disc = _s5_discretize(ssm_a_re[0], ssm_a_im[0], ssm_log_dt[0], ssm_b_re[0], ssm_b_im[0])
    w_p = _s5_chunk_weights(*disc, ssm_c_re[0], ssm_c_im[0], S5_CHUNK)
    w_s = _s5_chunk_weights(*disc, ssm_c_re[0], ssm_c_im[0], n_new)
    n_chunks = seq // S5_CHUNK
    gch = (SSM_GROUPS, SSM_GROUP_CH)
    uf_p = up.reshape(bsz, n_chunks, S5_CHUNK, *gch).transpose(0, 1, 3, 2, 4).reshape(bsz, n_chunks, -1)
    uf_s = us.reshape(dbs, n_new, *gch).transpose(0, 2, 1, 3).reshape(dbs, -1)
    yf_p, hpr, hpi = _s5_prompt_call(uf_p, w_p)
    h0r = state_ssm_re[0].reshape(dbs, -1).astype(F32)
    h0i = state_ssm_im[0].reshape(dbs, -1).astype(F32)
    yf_s, hsr, hsi = _s5_sample_call(uf_s, h0r, h0i, w_s, n_new)
    y_p = yf_p.reshape(bsz, n_chunks, SSM_GROUPS, S5_CHUNK, SSM_GROUP_CH).transpose(0, 1, 3, 2, 4).reshape(tp, D_MODEL)
    y_s = yf_s.reshape(dbs, SSM_GROUPS, n_new, SSM_GROUP_CH).transpose(0, 2, 1, 3).reshape(ts, D_MODEL)

    wa, wb = ssm_w_glu_a[0].astype(BF16), ssm_w_glu_b[0].astype(BF16)
    d_row, g_ffn1 = row1(ssm_d[0]), row1(norm_ffn[1])
    wrh, wrl, br = routers[1]
    xp3, xpn, cp = _glu_call(y_p, xp2, g_mix1, d_row, wa, wb, g_ffn1, wrh, wrl, br, tm)
    xs3, xsn, cs = _glu_call(y_s, xs2, g_mix1, d_row, wa, wb, g_ffn1, wrh, wrl, br, tm)
    g_fin = row1(norm_final)
    (yp,) = _moe_call(xpn, cp, xp3, *experts[1], expand, g_fin, False, tm)
    (ys,) = _moe_call(xsn, cs, xs3, *experts[1], expand, g_fin, False, tm)

    kv5 = lambda a, n: a.reshape(1, n, rows, KV_HEADS, HEAD_DIM)
    st4 = lambda a, n: a.reshape(1, n, SSM_GROUPS, SSM_STATE)
    k_last = kp.reshape(bsz, seq, KV_DIM)[:, seq - WINDOW:]
    v_last = vp.reshape(bsz, seq, KV_DIM)[:, seq - WINDOW:]
    return (yp.reshape(bsz, seq, D_MODEL), ys.reshape(dbs, n_new, D_MODEL),
            k_last.reshape(1, bsz, WINDOW, KV_HEADS, HEAD_DIM), kv5(nks, dbs),
            v_last.reshape(1, bsz, WINDOW, KV_HEADS, HEAD_DIM), kv5(nvs, dbs),
            st4(hpr, bsz), st4(hsr, dbs), st4(hpi, bsz), st4(hsi, dbs))
```

```python
import functools
import math

import jax
import jax.numpy as jnp
from jax import lax
from jax.experimental import pallas as pl
from jax.experimental.pallas import tpu as pltpu

F32 = jnp.float32
BF16 = jnp.bfloat16

D_MODEL = 1024
N_HEADS = 16
KV_HEADS = 4
HEAD_DIM = 64
Q_DIM = N_HEADS * HEAD_DIM
KV_DIM = KV_HEADS * HEAD_DIM
QKV_DIM = Q_DIM + 2 * KV_DIM
WINDOW = 128
BLOCK = 128
ROPE_THETA = 10000.0
PAST_LEN = 16384
SSM_GROUP_CH = 16
SSM_GROUPS = D_MODEL // SSM_GROUP_CH
SSM_STATE = 64
N_EXPERT_GROUPS = 4
EXPERTS_PER_GROUP = 8
N_EXPERTS = N_EXPERT_GROUPS * EXPERTS_PER_GROUP
EXPERT_FF = 128
NORM_EPS = 1e-5

LANES = 128
VMEM_LIMIT = 56 * 1024 * 1024
S5_CHUNK = 16
S5_GB = 8
MOE_CHUNKS = 4


def _cparams(sem):
    return pltpu.CompilerParams(dimension_semantics=sem, vmem_limit_bytes=VMEM_LIMIT)


def _rms(x, g):
    return x * lax.rsqrt(jnp.mean(x * x, axis=-1, keepdims=True) + NORM_EPS) * g


def _dot(a, b):
    return jnp.dot(a, b, preferred_element_type=F32)


def _dot_nt(a, b):
    return lax.dot_general(a, b, (((1,), (1,)), ((), ())), preferred_element_type=F32)


def _sigmoid(x):
    return 1.0 / (1.0 + jnp.exp(-x))


def _qkv_kernel(x_ref, g_ref, w_ref, b_ref, cos_ref, sin_ref, q_ref, k_ref, v_ref):
    xn = _rms(x_ref[...], g_ref[...])
    qkv = _dot(xn.astype(BF16), w_ref[...]) + b_ref[...]
    cos = cos_ref[...]
    sin = sin_ref[...]
    lane = lax.broadcasted_iota(jnp.int32, cos.shape, 1)
    first_half = (lane % HEAD_DIM) < (HEAD_DIM // 2)
    n_rot = (Q_DIM + KV_DIM) // LANES
    for c in range(n_rot):
        blk = qkv[:, c * LANES:(c + 1) * LANES]
        partner = jnp.where(first_half,
                            pltpu.roll(blk, LANES - HEAD_DIM // 2, 1),
                            pltpu.roll(blk, HEAD_DIM // 2, 1))
        rot = blk * cos + partner * sin
        if c < Q_DIM // LANES:
            q_ref[:, c * LANES:(c + 1) * LANES] = rot.astype(q_ref.dtype)
        else:
            k_ref[:, c * LANES - Q_DIM:(c + 1) * LANES - Q_DIM] = rot
    v_ref[...] = qkv[:, Q_DIM + KV_DIM:]


def _qkv_call(x2d, gain, w_bf, bias, cos_t, sin_t, q_dtype, tm):
    t = x2d.shape[0]
    n_pos = cos_t.shape[0] // tm
    return pl.pallas_call(
        _qkv_kernel,
        grid=(t // tm,),
        in_specs=[
            pl.BlockSpec((tm, D_MODEL), lambda i: (i, 0)),
            pl.BlockSpec((1, D_MODEL), lambda i: (0, 0)),
            pl.BlockSpec((D_MODEL, QKV_DIM), lambda i: (0, 0)),
            pl.BlockSpec((1, QKV_DIM), lambda i: (0, 0)),
            pl.BlockSpec((tm, LANES), lambda i: (i % n_pos, 0)),
            pl.BlockSpec((tm, LANES), lambda i: (i % n_pos, 0)),
        ],
        out_specs=[
            pl.BlockSpec((tm, Q_DIM), lambda i: (i, 0)),
            pl.BlockSpec((tm, KV_DIM), lambda i: (i, 0)),
            pl.BlockSpec((tm, KV_DIM), lambda i: (i, 0)),
        ],
        out_shape=[
            jax.ShapeDtypeStruct((t, Q_DIM), q_dtype),
            jax.ShapeDtypeStruct((t, KV_DIM), F32),
            jax.ShapeDtypeStruct((t, KV_DIM), F32),
        ],
        compiler_params=_cparams(("parallel",)),
        name="qkv_rope",
    )(x2d, gain, w_bf, bias, cos_t, sin_t)


def _split_kv_pair(pair_f32, head_in_low_half):
    lane = lax.broadcasted_iota(jnp.int32, pair_f32.shape, 1)
    if head_in_low_half:
        lo = jnp.where(lane < HEAD_DIM, pair_f32, 0.0)
        hi = pltpu.roll(lo, HEAD_DIM, 1)
    else:
        hi = jnp.where(lane >= HEAD_DIM, pair_f32, 0.0)
        lo = pltpu.roll(hi, HEAD_DIM, 1)
    return lo.astype(BF16), hi.astype(BF16)


def _attend_group(q0, q1, k_pair, v_pair, head_in_low_half, mask, sinks4):
    rows = q0.shape[0]
    k_lo, k_hi = _split_kv_pair(k_pair, head_in_low_half)
    v_lo, v_hi = _split_kv_pair(v_pair, head_in_low_half)
    qcat = jnp.concatenate([q0, q1], axis=0).astype(BF16)
    row = lax.broadcasted_iota(jnp.int32, (2 * rows, 1), 0)
    top = row < rows
    acc = None
    for half, (kk, vv) in enumerate(((k_lo, v_lo), (k_hi, v_hi))):
        sink = jnp.where(top, sinks4[half], sinks4[2 + half])
        s = _dot_nt(qcat, kk) * (HEAD_DIM ** -0.5)
        s = jnp.where(mask, s, -jnp.inf)
        m = jnp.maximum(jnp.max(s, axis=-1, keepdims=True), sink)
        p = jnp.exp(s - m)
        denom = jnp.sum(p, axis=-1, keepdims=True) + jnp.exp(sink - m)
        o = _dot(p.astype(BF16), vv) * (1.0 / denom)
        acc = o if acc is None else acc + o
    return acc[:rows], acc[rows:]


def _attn_prompt_kernel(sink_ref, q_ref, kp_ref, kc_ref, vp_ref, vc_ref, o_ref):
    n = pl.program_id(1)
    k2 = jnp.concatenate([kp_ref[0], kc_ref[0]], axis=0)
    v2 = jnp.concatenate([vp_ref[0], vc_ref[0]], axis=0)
    row = lax.broadcasted_iota(jnp.int32, (2 * BLOCK, 2 * BLOCK), 0) % BLOCK
    col = lax.broadcasted_iota(jnp.int32, (2 * BLOCK, 2 * BLOCK), 1)
    mask = (col >= row) & (col <= row + WINDOW) & ((n > 0) | (col >= BLOCK))
    group = N_HEADS // KV_HEADS
    for j in range(KV_HEADS):
        pj = (j // 2) * LANES
        base = j * group * HEAD_DIM
        q0 = q_ref[0, :, base:base + LANES]
        q1 = q_ref[0, :, base + LANES:base + 2 * LANES]
        sinks4 = [sink_ref[j * group + 0], sink_ref[j * group + 1],
                  sink_ref[j * group + 2], sink_ref[j * group + 3]]
        o0, o1 = _attend_group(q0, q1, k2[:, pj:pj + LANES], v2[:, pj:pj + LANES],
                               j % 2 == 0, mask, sinks4)
        o_ref[0, :, base:base + LANES] = o0.astype(o_ref.dtype)
        o_ref[0, :, base + LANES:base + 2 * LANES] = o1.astype(o_ref.dtype)


def _attn_prompt_call(sinks, q, k, v):
    b, l, _ = q.shape
    nb = l // BLOCK
    kv_cur = pl.BlockSpec((1, BLOCK, KV_DIM), lambda i, n: (i, n, 0))
    kv_prev = pl.BlockSpec((1, BLOCK, KV_DIM), lambda i, n: (i, jnp.maximum(n - 1, 0), 0))
    return pl.pallas_call(
        _attn_prompt_kernel,
        grid=(b, nb),
        in_specs=[
            pl.BlockSpec(memory_space=pltpu.SMEM),
            pl.BlockSpec((1, BLOCK, Q_DIM), lambda i, n: (i, n, 0)),
            kv_prev, kv_cur, kv_prev, kv_cur,
        ],
        out_specs=pl.BlockSpec((1, BLOCK, Q_DIM), lambda i, n: (i, n, 0)),
        out_shape=jax.ShapeDtypeStruct((b, l, Q_DIM), BF16),
        compiler_params=_cparams(("parallel", "parallel")),
        name="attn_prompt",
    )(sinks, q, k, k, v, v)


def _attn_sample_kernel(n_new, seqs, sink_ref, q_ref, kn_ref, vn_ref, ck_ref, cv_ref,
                        o_ref, nk_ref, nv_ref):
    rows = ck_ref.shape[1]
    keys = 2 * rows
    n_cols = N_HEADS * n_new
    group = N_HEADS // KV_HEADS
    key = lax.broadcasted_iota(jnp.int32, (keys, n_cols), 0)
    qry = lax.broadcasted_iota(jnp.int32, (keys, n_cols), 1) % n_new
    mask = ((key < rows) & (key >= qry)) | ((key >= rows) & (key - rows <= qry))
    low = lax.broadcasted_iota(jnp.int32, (n_new, LANES), 1) < HEAD_DIM
    sink = sink_ref[...]
    pad = jnp.zeros((rows - n_new, KV_DIM), F32)
    zero_blk = jnp.zeros((n_new, LANES), F32)

    def body(sb, carry):
        r0 = pl.multiple_of(sb * n_new, n_new)
        k_new = kn_ref[pl.ds(r0, n_new), :]
        v_new = vn_ref[pl.ds(r0, n_new), :]
        k_all = jnp.concatenate([ck_ref[sb], k_new, pad], axis=0).astype(BF16)
        v_all = jnp.concatenate([cv_ref[sb], v_new, pad], axis=0).astype(BF16)
        blocks = []
        for h in range(N_HEADS):
            kvh = h // group
            blk = q_ref[pl.ds(r0, n_new), (h // 2) * LANES:(h // 2 + 1) * LANES]
            if (h % 2) != (kvh % 2):
                blk = pltpu.roll(blk, HEAD_DIM, 1)
            blk = jnp.where(low if kvh % 2 == 0 else ~low, blk, 0.0)
            blocks.append(jnp.concatenate([blk, zero_blk] if kvh < 2 else [zero_blk, blk], axis=1))
        qz = jnp.concatenate(blocks, axis=0).astype(BF16)
        s = _dot_nt(k_all, qz) * (HEAD_DIM ** -0.5)
        s = jnp.where(mask, s, -jnp.inf)
        m = jnp.maximum(jnp.max(s, axis=0, keepdims=True), sink)
        p = jnp.exp(s - m)
        denom = jnp.sum(p, axis=0, keepdims=True) + jnp.exp(sink - m)
        w = (p * (1.0 / denom)).T.astype(BF16)
        oz = _dot(w, v_all)
        for c in range(N_HEADS // 2):
            parts = []
            for h in (2 * c, 2 * c + 1):
                kvh = h // group
                blk = oz[h * n_new:(h + 1) * n_new, (kvh // 2) * LANES:(kvh // 2 + 1) * LANES]
                if (h % 2) != (kvh % 2):
                    blk = pltpu.roll(blk, HEAD_DIM, 1)
                parts.append(blk)
            o_ref[pl.ds(r0, n_new), c * LANES:(c + 1) * LANES] = jnp.where(low, parts[0], parts[1])
        nk_ref[sb, 0:rows - n_new, :] = ck_ref[sb, n_new:rows, :]
        nk_ref[sb, rows - n_new:rows, :] = k_new
        nv_ref[sb, 0:rows - n_new, :] = cv_ref[sb, n_new:rows, :]
        nv_ref[sb, rows - n_new:rows, :] = v_new
        return carry

    lax.fori_loop(0, seqs, body, 0)


def _attn_sample_call(sinks, q, k_new, v_new, cache_k, cache_v, n_new, seqs):
    b, rows, _ = cache_k.shape
    tok = pl.BlockSpec((seqs * n_new, Q_DIM), lambda i: (i, 0))
    tok_kv = pl.BlockSpec((seqs * n_new, KV_DIM), lambda i: (i, 0))
    cache = pl.BlockSpec((seqs, rows, KV_DIM), lambda i: (i, 0, 0))
    return pl.pallas_call(
        functools.partial(_attn_sample_kernel, n_new, seqs),
        grid=(b // seqs,),
        in_specs=[pl.BlockSpec((1, N_HEADS * n_new), lambda i: (0, 0)), tok, tok_kv, tok_kv, cache, cache],
        out_specs=[tok, cache, cache],
        out_shape=[
            jax.ShapeDtypeStruct((b * n_new, Q_DIM), F32),
            jax.ShapeDtypeStruct((b, rows, KV_DIM), F32),
            jax.ShapeDtypeStruct((b, rows, KV_DIM), F32),
        ],
        compiler_params=_cparams(("parallel",)),
        name="attn_sample",
    )(sinks, q, k_new, v_new, cache_k, cache_v)


def _router_epilogue(x1, gn_ref, wrh_ref, wrl_ref, br_ref, xn_ref, comb_ref):
    xn = _rms(x1, gn_ref[...])
    xh = xn.astype(BF16)
    xn_ref[...] = xh
    xl = (xn - xh.astype(F32)).astype(BF16)
    logits = _dot(xh, wrh_ref[...]) + _dot(xl, wrh_ref[...]) + _dot(xh, wrl_ref[...]) + br_ref[...]
    lane = lax.broadcasted_iota(jnp.int32, logits.shape, 1).astype(F32)
    big = jnp.float32(4 * LANES)
    neg = -jnp.inf
    gl = jnp.where((lane >= N_EXPERTS) & (lane < N_EXPERTS + N_EXPERT_GROUPS), logits, neg)
    gmax = jnp.max(gl, axis=-1, keepdims=True)
    g_val = 1.0 / jnp.sum(jnp.exp(gl - gmax), axis=-1, keepdims=True)
    g_idx = jnp.min(jnp.where(gl == gmax, lane, big), axis=-1, keepdims=True) - N_EXPERTS
    lo = g_idx * EXPERTS_PER_GROUP
    el = jnp.where((lane >= lo) & (lane < lo + EXPERTS_PER_GROUP), logits, neg)
    e1 = jnp.max(el, axis=-1, keepdims=True)
    i1 = jnp.min(jnp.where(el == e1, lane, big), axis=-1, keepdims=True)
    el2 = jnp.where(lane == i1, neg, el)
    e2 = jnp.max(el2, axis=-1, keepdims=True)
    i2 = jnp.min(jnp.where(el2 == e2, lane, big), axis=-1, keepdims=True)
    t = jnp.exp(e2 - e1)
    w1 = 1.0 / (1.0 + t)
    w2 = t / (1.0 + t)
    comb_ref[...] = g_val * (jnp.where(lane == i1, w1, 0.0) + jnp.where(lane == i2, w2, 0.0))


def _oproj_kernel(o_ref, x_ref, wo_ref, bo_ref, gn_ref, wrh_ref, wrl_ref, br_ref,
                  x1_ref, xn_ref, comb_ref):
    x1 = x_ref[...] + _dot(o_ref[...].astype(BF16), wo_ref[...]) + bo_ref[...]
    x1_ref[...] = x1
    _router_epilogue(x1, gn_ref, wrh_ref, wrl_ref, br_ref, xn_ref, comb_ref)


def _gelu_tanh(x):
    return x * (0.5 * (1.0 + jnp.tanh(math.sqrt(2.0 / math.pi) * (x + 0.044715 * (x * x * x)))))


def _glu_kernel(y_ref, x_ref, gm_ref, d_ref, wa_ref, wb_ref, gn_ref, wrh_ref, wrl_ref, br_ref,
                x1_ref, xn_ref, comb_ref):
    x = x_ref[...]
    u = _rms(x, gm_ref[...])
    z = _gelu_tanh(y_ref[...] + d_ref[...] * u).astype(BF16)
    mix = _dot(z, wa_ref[...]) * _sigmoid(_dot(z, wb_ref[...]))
    x1 = x + mix
    x1_ref[...] = x1
    _router_epilogue(x1, gn_ref, wrh_ref, wrl_ref, br_ref, xn_ref, comb_ref)


def _row_spec(tm, width):
    return pl.BlockSpec((tm, width), lambda i: (i, 0))


def _const_spec(shape):
    return pl.BlockSpec(shape, lambda i: (0,) * len(shape))


def _router_specs():
    return [_const_spec((1, D_MODEL)), _const_spec((D_MODEL, LANES)),
            _const_spec((D_MODEL, LANES)), _const_spec((1, LANES))]


def _mixer_out(t, tm):
    specs = [_row_spec(tm, D_MODEL), _row_spec(tm, D_MODEL), _row_spec(tm, LANES)]
    shapes = [jax.ShapeDtypeStruct((t, D_MODEL), F32), jax.ShapeDtypeStruct((t, D_MODEL), BF16),
              jax.ShapeDtypeStruct((t, LANES), F32)]
    return specs, shapes


def _oproj_call(o2d, x2d, wo_bf, bo, gn, wrh, wrl, br, tm):
    t = x2d.shape[0]
    out_specs, out_shape = _mixer_out(t, tm)
    return pl.pallas_call(
        _oproj_kernel,
        grid=(t // tm,),
        in_specs=[_row_spec(tm, Q_DIM), _row_spec(tm, D_MODEL), _const_spec((Q_DIM, D_MODEL)),
                  _const_spec((1, D_MODEL))] + _router_specs(),
        out_specs=out_specs, out_shape=out_shape,
        compiler_params=_cparams(("parallel",)),
        name="oproj_router",
    )(o2d, x2d, wo_bf, bo, gn, wrh, wrl, br)


def _glu_call(y2d, x2d, gm, d, wa_bf, wb_bf, gn, wrh, wrl, br, tm):
    t = x2d.shape[0]
    out_specs, out_shape = _mixer_out(t, tm)
    return pl.pallas_call(
        _glu_kernel,
        grid=(t // tm,),
        in_specs=[_row_spec(tm, D_MODEL), _row_spec(tm, D_MODEL), _const_spec((1, D_MODEL)),
                  _const_spec((1, D_MODEL)), _const_spec((D_MODEL, D_MODEL)),
                  _const_spec((D_MODEL, D_MODEL))] + _router_specs(),
        out_specs=out_specs, out_shape=out_shape,
        compiler_params=_cparams(("parallel",)),
        name="glu_router",
    )(y2d, x2d, gm, d, wa_bf, wb_bf, gn, wrh, wrl, br)


def _moe_kernel(emit_x, xn_ref, comb_ref, x_ref, wg_ref, wu_ref, wd_ref, ex_ref, gnext_ref,
                *out_and_scratch):
    acc_ref = out_and_scratch[-1]
    outs = out_and_scratch[:-1]
    c = pl.program_id(1)

    @pl.when(c == 0)
    def _():
        acc_ref[...] = jnp.zeros_like(acc_ref)

    xb = xn_ref[...]
    g = _dot(xb, wg_ref[...])
    u = _dot(xb, wu_ref[...])
    comb = comb_ref[...]
    ch = comb.astype(BF16)
    cl = (comb - ch.astype(F32)).astype(BF16)
    scale = _dot(ch, ex_ref[...]) + _dot(cl, ex_ref[...])
    hid = (g * _sigmoid(g)) * u * scale
    acc_ref[...] += _dot(hid.astype(BF16), wd_ref[...])

    @pl.when(c == pl.num_programs(1) - 1)
    def _():
        x2 = x_ref[...] + acc_ref[...]
        normed = _rms(x2, gnext_ref[...])
        if emit_x:
            outs[0][...] = x2
            outs[1][...] = normed.astype(outs[1].dtype)
        else:
            outs[0][...] = normed.astype(outs[0].dtype)


def _moe_call(xn_bf, comb, x2d, wg, wu, wd, expand, gnext, emit_x, tm):
    t = x2d.shape[0]
    ff = N_EXPERTS * EXPERT_FF // MOE_CHUNKS
    row = lambda w: pl.BlockSpec((tm, w), lambda i, c: (i, 0))
    if emit_x:
        out_specs = [row(D_MODEL), row(D_MODEL)]
        out_shape = [jax.ShapeDtypeStruct((t, D_MODEL), F32), jax.ShapeDtypeStruct((t, D_MODEL), F32)]
    else:
        out_specs = [row(D_MODEL)]
        out_shape = [jax.ShapeDtypeStruct((t, D_MODEL), F32)]
    return pl.pallas_call(
        functools.partial(_moe_kernel, emit_x),
        grid=(t // tm, MOE_CHUNKS),
        in_specs=[
            row(D_MODEL), row(LANES), row(D_MODEL),
            pl.BlockSpec((D_MODEL, ff), lambda i, c: (0, c)),
            pl.BlockSpec((D_MODEL, ff), lambda i, c: (0, c)),
            pl.BlockSpec((ff, D_MODEL), lambda i, c: (c, 0)),
            pl.BlockSpec((LANES, ff), lambda i, c: (0, c)),
            pl.BlockSpec((1, D_MODEL), lambda i, c: (0, 0)),
        ],
        out_specs=out_specs, out_shape=out_shape,
        scratch_shapes=[pltpu.VMEM((tm, D_MODEL), F32)],
        compiler_params=_cparams(("parallel", "arbitrary")),
        name="moe_dense",
    )(xn_bf, comb, x2d, wg, wu, wd, expand, gnext)


def _s5_state_in(u_ref_val, wre_ref, wim_ref, sre_ref, sim_ref, pair_w):
    for m in range(S5_GB // 2):
        up = u_ref_val(m * pair_w, pair_w)
        sre_ref[:, m * LANES:(m + 1) * LANES] = _dot(up, wre_ref[m])
        sim_ref[:, m * LANES:(m + 1) * LANES] = _dot(up, wim_ref[m])


def _s5_outputs(u_ref_val, hre, him, m_ref, gre_ref, gim_ref, y_store, pair_w):
    gw = pair_w // 2
    for m in range(S5_GB // 2):
        hr = hre(m).astype(BF16)
        hi = him(m).astype(BF16)
        y = _dot(hr, gre_ref[m]) + _dot(hi, gim_ref[m])
        y0 = y[:, :gw] + _dot(u_ref_val(m * pair_w, gw), m_ref[2 * m])
        y1 = y[:, gw:] + _dot(u_ref_val(m * pair_w + gw, gw), m_ref[2 * m + 1])
        y_store(m * pair_w, gw, y0)
        y_store(m * pair_w + gw, gw, y1)


def _s5_prompt_kernel(u_ref, wre_ref, wim_ref, m_ref, gre_ref, gim_ref, aqr_ref, aqi_ref,
                      y_ref, her_ref, hei_ref, ut, uflat, sre, sim, hre, him, yflat, yt):
    gc = SSM_GROUP_CH
    qc = S5_CHUNK * gc
    pair_w = 2 * qc
    n_chunks = u_ref.shape[1] // S5_CHUNK
    for s in range(S5_CHUNK):
        xs = u_ref[0, pl.ds(s, n_chunks, stride=S5_CHUNK), :]
        ut[:, s * gc:(s + 1) * gc, :] = xs.T.reshape(S5_GB, gc, n_chunks)
    for g in range(S5_GB):
        uflat[:, g * qc:(g + 1) * qc] = ut[g].T.astype(BF16)
    u_val = lambda off, w: uflat[:, off:off + w]
    _s5_state_in(u_val, wre_ref, wim_ref, sre, sim, pair_w)
    ar = aqr_ref[...]
    ai = aqi_ref[...]

    def step(n, carry):
        hr, hi = carry
        hre[pl.ds(n, 1), :] = hr
        him[pl.ds(n, 1), :] = hi
        sr = sre[pl.ds(n, 1), :]
        si = sim[pl.ds(n, 1), :]
        return ar * hr - ai * hi + sr, ar * hi + ai * hr + si

    zero = jnp.zeros(ar.shape, F32)
    hr, hi = lax.fori_loop(0, n_chunks, step, (zero, zero))
    her_ref[0] = hr
    hei_ref[0] = hi

    def y_store(off, w, val):
        yflat[:, off:off + w] = val

    _s5_outputs(u_val, lambda m: hre[:, m * LANES:(m + 1) * LANES],
                lambda m: him[:, m * LANES:(m + 1) * LANES],
                m_ref, gre_ref, gim_ref, y_store, pair_w)
    for g in range(S5_GB):
        yt[g] = yflat[:, g * qc:(g + 1) * qc].T
    for t in range(S5_CHUNK):
        z = yt[:, t * gc:(t + 1) * gc, :].reshape(S5_GB * gc, n_chunks)
        y_ref[0, pl.ds(t, n_chunks, stride=S5_CHUNK), :] = z.T


def _s5_sample_kernel(n_new, u_ref, h0r_ref, h0i_ref, wre_ref, wim_ref, m_ref, gre_ref, gim_ref,
                      aqr_ref, aqi_ref, y_ref, hnr_ref, hni_ref, sre, sim):
    pair_w = 2 * n_new * SSM_GROUP_CH
    u_val = lambda off, w: u_ref[:, off:off + w]
    _s5_state_in(u_val, wre_ref, wim_ref, sre, sim, pair_w)
    ar = aqr_ref[...]
    ai = aqi_ref[...]
    h0r = h0r_ref[...]
    h0i = h0i_ref[...]
    hnr_ref[...] = ar * h0r - ai * h0i + sre[...]
    hni_ref[...] = ar * h0i + ai * h0r + sim[...]

    def y_store(off, w, val):
        y_ref[:, off:off + w] = val

    _s5_outputs(u_val, lambda m: h0r_ref[:, m * LANES:(m + 1) * LANES],
                lambda m: h0i_ref[:, m * LANES:(m + 1) * LANES],
                m_ref, gre_ref, gim_ref, y_store, pair_w)


def _s5_weight_specs(q, idx):
    qc = q * SSM_GROUP_CH
    np_ = S5_GB // 2
    st = S5_GB * SSM_STATE
    return [
        pl.BlockSpec((np_, 2 * qc, LANES), lambda *a: (idx(*a), 0, 0)),
        pl.BlockSpec((np_, 2 * qc, LANES), lambda *a: (idx(*a), 0, 0)),
        pl.BlockSpec((S5_GB, qc, qc), lambda *a: (idx(*a), 0, 0)),
        pl.BlockSpec((np_, LANES, 2 * qc), lambda *a: (idx(*a), 0, 0)),
        pl.BlockSpec((np_, LANES, 2 * qc), lambda *a: (idx(*a), 0, 0)),
        pl.BlockSpec((1, st), lambda *a: (0, idx(*a))),
        pl.BlockSpec((1, st), lambda *a: (0, idx(*a))),
    ]


def _s5_prompt_call(u, w):
    b, seq, _ = u.shape
    n_chunks = seq // S5_CHUNK
    gbl = S5_GB * SSM_GROUP_CH
    qc = S5_CHUNK * SSM_GROUP_CH
    st = S5_GB * SSM_STATE
    n_gb = SSM_GROUPS // S5_GB
    gb_of = lambda g, i: g
    tok = pl.BlockSpec((1, seq, gbl), lambda g, i: (i, 0, g))
    return pl.pallas_call(
        _s5_prompt_kernel,
        grid=(n_gb, b),
        in_specs=[tok] + _s5_weight_specs(S5_CHUNK, gb_of),
        out_specs=[
            tok,
            pl.BlockSpec((1, 1, st), lambda g, i: (i, 0, g)),
            pl.BlockSpec((1, 1, st), lambda g, i: (i, 0, g)),
        ],
        out_shape=[
            jax.ShapeDtypeStruct((b, seq, D_MODEL), F32),
            jax.ShapeDtypeStruct((b, 1, SSM_GROUPS * SSM_STATE), F32),
            jax.ShapeDtypeStruct((b, 1, SSM_GROUPS * SSM_STATE), F32),
        ],
        scratch_shapes=[
            pltpu.VMEM((S5_GB, qc, n_chunks), F32),
            pltpu.VMEM((n_chunks, S5_GB * qc), BF16),
            pltpu.VMEM((n_chunks, st), F32), pltpu.VMEM((n_chunks, st), F32),
            pltpu.VMEM((n_chunks, st), F32), pltpu.VMEM((n_chunks, st), F32),
            pltpu.VMEM((n_chunks, S5_GB * qc), F32),
            pltpu.VMEM((S5_GB, qc, n_chunks), F32),
        ],
        compiler_params=_cparams(("parallel", "parallel")),
        name="s5_prompt",
    )(u, *w)


def _s5_sample_call(uf, h0r, h0i, w, n_new):
    b, width = uf.shape
    gbw = S5_GB * n_new * SSM_GROUP_CH
    st = S5_GB * SSM_STATE
    n_gb = SSM_GROUPS // S5_GB
    gb_of = lambda g: g
    state = pl.BlockSpec((b, st), lambda g: (0, g))
    return pl.pallas_call(
        functools.partial(_s5_sample_kernel, n_new),
        grid=(n_gb,),
        in_specs=[pl.BlockSpec((b, gbw), lambda g: (0, g)), state, state] + _s5_weight_specs(n_new, gb_of),
        out_specs=[pl.BlockSpec((b, gbw), lambda g: (0, g)), state, state],
        out_shape=[
            jax.ShapeDtypeStruct((b, width), F32),
            jax.ShapeDtypeStruct((b, SSM_GROUPS * SSM_STATE), F32),
            jax.ShapeDtypeStruct((b, SSM_GROUPS * SSM_STATE), F32),
        ],
        scratch_shapes=[pltpu.VMEM((b, st), F32)] * 2,
        compiler_params=_cparams(("parallel",)),
        name="s5_sample",
    )(uf, h0r, h0i, *w)


def _block_diag_pairs(w):
    g, r, n = w.shape
    w2 = w.reshape(g // 2, 2, r, n)
    z = jnp.zeros((g // 2, r, n), w.dtype)
    top = jnp.concatenate([w2[:, 0], z], axis=2)
    bot = jnp.concatenate([z, w2[:, 1]], axis=2)
    return jnp.concatenate([top, bot], axis=1)


def _s5_discretize(a_re, a_im, log_dt, b_re, b_im):
    delta = jnp.exp(log_dt.astype(F32))[:, None]
    lr, li = a_re.astype(F32), a_im.astype(F32)
    mag = jnp.exp(delta * lr)
    abar_r = mag * jnp.cos(delta * li)
    abar_i = mag * jnp.sin(delta * li)
    nr, ni = abar_r - 1.0, abar_i
    den = lr * lr + li * li
    coef_r = ((nr * lr + ni * li) / den)[..., None]
    coef_i = ((ni * lr - nr * li) / den)[..., None]
    br, bi = b_re.astype(F32), b_im.astype(F32)
    return abar_r, abar_i, coef_r * br - coef_i * bi, coef_r * bi + coef_i * br


def _s5_chunk_weights(abar_r, abar_i, bbar_r, bbar_i, c_re, c_im, q):
    g, p = abar_r.shape
    c = SSM_GROUP_CH
    hi = lax.Precision.HIGHEST
    pows_r, pows_i = [jnp.ones_like(abar_r)], [jnp.zeros_like(abar_r)]
    for _ in range(q):
        pr, pi = pows_r[-1], pows_i[-1]
        pows_r.append(pr * abar_r - pi * abar_i)
        pows_i.append(pr * abar_i + pi * abar_r)
    pw_r, pw_i = jnp.stack(pows_r), jnp.stack(pows_i)
    rev_r, rev_i = pw_r[:q][::-1][..., None], pw_i[:q][::-1][..., None]
    wst_r = (rev_r * bbar_r[None] - rev_i * bbar_i[None]).transpose(1, 0, 3, 2).reshape(g, q * c, p)
    wst_i = (rev_r * bbar_i[None] + rev_i * bbar_r[None]).transpose(1, 0, 3, 2).reshape(g, q * c, p)
    x_r = pw_r[:q, :, :, None] * bbar_r[None] - pw_i[:q, :, :, None] * bbar_i[None]
    x_i = pw_r[:q, :, :, None] * bbar_i[None] + pw_i[:q, :, :, None] * bbar_r[None]
    cr, ci = c_re.astype(F32), c_im.astype(F32)
    ker = (jnp.einsum('gcp,tgpd->gtdc', cr, x_r, precision=hi)
           - jnp.einsum('gcp,tgpd->gtdc', ci, x_i, precision=hi))
    idx = jnp.arange(q)
    tau = idx[None, :] - idx[:, None]
    m5 = jnp.where((tau >= 0)[None, :, :, None, None], ker[:, jnp.clip(tau, 0, q - 1)], 0.0)
    m = m5.transpose(0, 1, 3, 2, 4).reshape(g, q * c, q * c)
    cr_t, ci_t = cr.transpose(0, 2, 1)[:, :, None, :], ci.transpose(0, 2, 1)[:, :, None, :]
    pr_t = pw_r[1:q + 1].transpose(1, 2, 0)[..., None]
    pi_t = pw_i[1:q + 1].transpose(1, 2, 0)[..., None]
    g_re = (cr_t * pr_t - ci_t * pi_t).reshape(g, p, q * c)
    g_im = (-(cr_t * pi_t + ci_t * pr_t)).reshape(g, p, q * c)
    return (_block_diag_pairs(wst_r).astype(BF16), _block_diag_pairs(wst_i).astype(BF16),
            m.astype(BF16), _block_diag_pairs(g_re).astype(BF16), _block_diag_pairs(g_im).astype(BF16),
            pw_r[q].reshape(1, g * p), pw_i[q].reshape(1, g * p))


def _rope_tables(pos):
    half = HEAD_DIM // 2
    inv = 1.0 / (ROPE_THETA ** (jnp.arange(half, dtype=F32) * (2.0 / HEAD_DIM)))
    ang = pos.astype(F32)[:, None] * inv[None, :]
    cos, sin = jnp.cos(ang), jnp.sin(ang)
    return jnp.tile(cos, (1, 4)), jnp.concatenate([-sin, sin, -sin, sin], axis=1)


def _router_weights(w_rg, b_rg, w_re, b_re):
    pad = LANES - N_EXPERTS - N_EXPERT_GROUPS
    w = jnp.concatenate([w_re, w_rg, jnp.zeros((D_MODEL, pad), F32)], axis=1)
    b = jnp.concatenate([b_re, b_rg, jnp.zeros((pad,), F32)]).reshape(1, LANES)
    wh = w.astype(BF16)
    wl = (w - wh.astype(F32)).astype(BF16)
    return wh, wl, b


def _moe_weights(w_gate, w_up, w_down):
    ff = N_EXPERTS * EXPERT_FF
    wg = w_gate.astype(BF16).transpose(1, 0, 2).reshape(D_MODEL, ff)
    wu = w_up.astype(BF16).transpose(1, 0, 2).reshape(D_MODEL, ff)
    wd = w_down.astype(BF16).reshape(ff, D_MODEL)
    return wg, wu, wd


def _expand_matrix():
    col = jnp.arange(N_EXPERTS * EXPERT_FF) // EXPERT_FF
    return (jnp.arange(LANES)[:, None] == col[None, :]).astype(BF16)


def kernel(x_prompt, x_sample, cache_k, cache_v, state_ssm_re, state_ssm_im, norm_mix, norm_ffn, norm_final, attn_w_qkv, attn_b_qkv, attn_w_o, attn_b_o, attn_sinks, ssm_a_re, ssm_a_im, ssm_log_dt, ssm_b_re, ssm_b_im, ssm_c_re, ssm_c_im, ssm_d, ssm_w_glu_a, ssm_w_glu_b, moe_w_router_group, moe_b_router_group, moe_w_router_expert, moe_b_router_expert, moe_w_gate, moe_w_up, moe_w_down):
    bsz, seq, _ = x_prompt.shape
    dbs, n_new, _ = x_sample.shape
    rows = cache_k.shape[2]
    tp, ts = bsz * seq, dbs * n_new
    xp = x_prompt.reshape(tp, D_MODEL)
    xs = x_sample.reshape(ts, D_MODEL)
    tm = 512

    row1 = lambda v: v.reshape(1, -1).astype(F32)
    routers = [_router_weights(moe_w_router_group[l], moe_b_router_group[l],
                               moe_w_router_expert[l], moe_b_router_expert[l]) for l in range(2)]
    experts = [_moe_weights(moe_w_gate[l], moe_w_up[l], moe_w_down[l]) for l in range(2)]
    expand = _expand_matrix()

    wqkv = attn_w_qkv[0].astype(BF16)
    bqkv = row1(attn_b_qkv[0])
    wo = attn_w_o[0].astype(BF16)
    bo = row1(attn_b_o[0])
    sinks = attn_sinks[0].astype(F32)
    g_mix0, g_ffn0 = row1(norm_mix[0]), row1(norm_ffn[0])
    cos_p, sin_p = _rope_tables(jnp.arange(seq, dtype=jnp.int32))
    pos_s = jnp.tile(PAST_LEN + jnp.arange(n_new, dtype=jnp.int32), dbs)
    cos_s, sin_s = _rope_tables(pos_s)

    qp, kp, vp = _qkv_call(xp, g_mix0, wqkv, bqkv, cos_p, sin_p, BF16, tm)
    qs, ks, vs = _qkv_call(xs, g_mix0, wqkv, bqkv, cos_s, sin_s, F32, tm)
    op = _attn_prompt_call(sinks, qp.reshape(bsz, seq, Q_DIM), kp.reshape(bsz, seq, KV_DIM),
                           vp.reshape(bsz, seq, KV_DIM))
    os_, nks, nvs = _attn_sample_call(jnp.repeat(sinks, n_new).reshape(1, -1), qs, ks, vs, cache_k[0].reshape(dbs, rows, KV_DIM),
                                      cache_v[0].reshape(dbs, rows, KV_DIM), n_new, 8)
    wrh, wrl, br = routers[0]
    xp1, xpn, cp = _oproj_call(op.reshape(tp, Q_DIM), xp, wo, bo, g_ffn0, wrh, wrl, br, tm)
    xs1, xsn, cs = _oproj_call(os_, xs, wo, bo, g_ffn0, wrh, wrl, br, tm)
    g_mix1 = row1(norm_mix[1])
    xp2, up = _moe_call(xpn, cp, xp1, *experts[0], expand, g_mix1, True, tm)
    xs2, us = _moe_call(xsn, cs, xs1, *experts[0], expand, g_mix1, True, tm)

    disc = _s5_discretize(ssm_a_re[0], ssm_a_im[0], ssm_log_dt[0], ssm_b_re[0], ssm_b_im[0])
    w_p = _s5_chunk_weights(*disc, ssm_c_re[0], ssm_c_im[0], S5_CHUNK)
    w_s = _s5_chunk_weights(*disc, ssm_c_re[0], ssm_c_im[0], n_new)
    gch = (SSM_GROUPS, SSM_GROUP_CH)
    uf_s = us.astype(BF16).reshape(dbs, n_new, *gch).transpose(0, 2, 1, 3).reshape(dbs, -1)
    y_p, hpr, hpi = _s5_prompt_call(up.reshape(bsz, seq, D_MODEL), w_p)
    y_p = y_p.reshape(tp, D_MODEL)
    h0r = state_ssm_re[0].reshape(dbs, -1).astype(F32)
    h0i = state_ssm_im[0].reshape(dbs, -1).astype(F32)
    yf_s, hsr, hsi = _s5_sample_call(uf_s, h0r, h0i, w_s, n_new)
    y_s = yf_s.reshape(dbs, SSM_GROUPS, n_new, SSM_GROUP_CH).transpose(0, 2, 1, 3).reshape(ts, D_MODEL)

    wa, wb = ssm_w_glu_a[0].astype(BF16), ssm_w_glu_b[0].astype(BF16)
    d_row, g_ffn1 = row1(ssm_d[0]), row1(norm_ffn[1])
    wrh, wrl, br = routers[1]
    xp3, xpn, cp = _glu_call(y_p, xp2, g_mix1, d_row, wa, wb, g_ffn1, wrh, wrl, br, tm)
    xs3, xsn, cs = _glu_call(y_s, xs2, g_mix1, d_row, wa, wb, g_ffn1, wrh, wrl, br, tm)
    g_fin = row1(norm_final)
    (yp,) = _moe_call(xpn, cp, xp3, *experts[1], expand, g_fin, False, tm)
    (ys,) = _moe_call(xsn, cs, xs3, *experts[1], expand, g_fin, False, tm)

    kv5 = lambda a, n: a.reshape(1, n, rows, KV_HEADS, HEAD_DIM)
    st4 = lambda a, n: a.reshape(1, n, SSM_GROUPS, SSM_STATE)
    k_last = kp.reshape(bsz, seq, KV_DIM)[:, seq - WINDOW:]
    v_last = vp.reshape(bsz, seq, KV_DIM)[:, seq - WINDOW:]
    return (yp.reshape(bsz, seq, D_MODEL), ys.reshape(dbs, n_new, D_MODEL),
            k_last.reshape(1, bsz, WINDOW, KV_HEADS, HEAD_DIM), kv5(nks, dbs),
            v_last.reshape(1, bsz, WINDOW, KV_HEADS, HEAD_DIM), kv5(nvs, dbs),
            st4(hpr, bsz), st4(hsr, dbs), st4(hpi, bsz), st4(hsi, dbs))
```

```python
import functools
import math

import jax
import jax.numpy as jnp
from jax import lax
from jax.experimental import pallas as pl
from jax.experimental.pallas import tpu as pltpu

F32 = jnp.float32
BF16 = jnp.bfloat16
U32 = jnp.uint32
I32 = jnp.int32

D_MODEL = 1024
N_HEADS = 16
KV_HEADS = 4
HEAD_DIM = 64
Q_DIM = N_HEADS * HEAD_DIM
KV_DIM = KV_HEADS * HEAD_DIM
QKV_DIM = Q_DIM + 2 * KV_DIM
WINDOW = 128
BLOCK = 128
ROPE_THETA = 10000.0
PAST_LEN = 16384
SSM_GROUP_CH = 16
SSM_GROUPS = D_MODEL // SSM_GROUP_CH
SSM_STATE = 64
N_EXPERT_GROUPS = 4
EXPERTS_PER_GROUP = 8
N_EXPERTS = N_EXPERT_GROUPS * EXPERTS_PER_GROUP
EXPERT_FF = 128
NORM_EPS = 1e-5

LANES = 128
VMEM_LIMIT = 56 * 1024 * 1024
S5_CHUNK = 16
S5_GB = 8
TOKEN_TILE = 512
MOE_TILE = TOKEN_TILE
PAY_X = D_MODEL // 2
PAY_W = PAY_X + LANES
HI16 = 0xFFFF0000


def _cparams(sem):
    return pltpu.CompilerParams(dimension_semantics=sem, vmem_limit_bytes=VMEM_LIMIT)


def _rms(x, g):
    return x * lax.rsqrt(jnp.mean(x * x, axis=-1, keepdims=True) + NORM_EPS) * g


def _dot(a, b):
    return jnp.dot(a, b, preferred_element_type=F32)


def _dot_nt(a, b):
    return lax.dot_general(a, b, (((1,), (1,)), ((), ())), preferred_element_type=F32)


def _sigmoid(x):
    return 1.0 / (1.0 + jnp.exp(-x))


def _qkv_kernel(x_ref, g_ref, w_ref, b_ref, cos_ref, sin_ref, q_ref, k_ref, v_ref):
    xn = _rms(x_ref[...], g_ref[...])
    qkv = _dot(xn.astype(BF16), w_ref[...]) + b_ref[...]
    cos = cos_ref[...]
    sin = sin_ref[...]
    lane = lax.broadcasted_iota(I32, cos.shape, 1)
    first_half = (lane % HEAD_DIM) < (HEAD_DIM // 2)
    n_rot = (Q_DIM + KV_DIM) // LANES
    for c in range(n_rot):
        blk = qkv[:, c * LANES:(c + 1) * LANES]
        partner = jnp.where(first_half,
                            pltpu.roll(blk, LANES - HEAD_DIM // 2, 1),
                            pltpu.roll(blk, HEAD_DIM // 2, 1))
        rot = blk * cos + partner * sin
        if c < Q_DIM // LANES:
            q_ref[:, c * LANES:(c + 1) * LANES] = rot.astype(q_ref.dtype)
        else:
            k_ref[:, c * LANES - Q_DIM:(c + 1) * LANES - Q_DIM] = rot
    v_ref[...] = qkv[:, Q_DIM + KV_DIM:]


def _qkv_call(x2d, gain, w_bf, bias, cos_t, sin_t, q_dtype):
    t = x2d.shape[0]
    tm = TOKEN_TILE
    n_pos = cos_t.shape[0] // tm
    return pl.pallas_call(
        _qkv_kernel,
        grid=(t // tm,),
        in_specs=[
            pl.BlockSpec((tm, D_MODEL), lambda i: (i, 0)),
            pl.BlockSpec((1, D_MODEL), lambda i: (0, 0)),
            pl.BlockSpec((D_MODEL, QKV_DIM), lambda i: (0, 0)),
            pl.BlockSpec((1, QKV_DIM), lambda i: (0, 0)),
            pl.BlockSpec((tm, LANES), lambda i: (i % n_pos, 0)),
            pl.BlockSpec((tm, LANES), lambda i: (i % n_pos, 0)),
        ],
        out_specs=[
            pl.BlockSpec((tm, Q_DIM), lambda i: (i, 0)),
            pl.BlockSpec((tm, KV_DIM), lambda i: (i, 0)),
            pl.BlockSpec((tm, KV_DIM), lambda i: (i, 0)),
        ],
        out_shape=[
            jax.ShapeDtypeStruct((t, Q_DIM), q_dtype),
            jax.ShapeDtypeStruct((t, KV_DIM), F32),
            jax.ShapeDtypeStruct((t, KV_DIM), F32),
        ],
        compiler_params=_cparams(("parallel",)),
        name="qkv_rope",
    )(x2d, gain, w_bf, bias, cos_t, sin_t)


def _split_kv_pair(pair_f32, head_in_low_half):
    lane = lax.broadcasted_iota(I32, pair_f32.shape, 1)
    if head_in_low_half:
        lo = jnp.where(lane < HEAD_DIM, pair_f32, 0.0)
        hi = pltpu.roll(lo, HEAD_DIM, 1)
    else:
        hi = jnp.where(lane >= HEAD_DIM, pair_f32, 0.0)
        lo = pltpu.roll(hi, HEAD_DIM, 1)
    return lo.astype(BF16), hi.astype(BF16)


def _attend_group(q0, q1, k_pair, v_pair, head_in_low_half, mask, sinks4):
    rows = q0.shape[0]
    k_lo, k_hi = _split_kv_pair(k_pair, head_in_low_half)
    v_lo, v_hi = _split_kv_pair(v_pair, head_in_low_half)
    qcat = jnp.concatenate([q0, q1], axis=0).astype(BF16)
    row = lax.broadcasted_iota(I32, (2 * rows, 1), 0)
    top = row < rows
    acc = None
    for half, (kk, vv) in enumerate(((k_lo, v_lo), (k_hi, v_hi))):
        sink = jnp.where(top, sinks4[half], sinks4[2 + half])
        s = _dot_nt(qcat, kk) * (HEAD_DIM ** -0.5)
        s = jnp.where(mask, s, -jnp.inf)
        m = jnp.maximum(jnp.max(s, axis=-1, keepdims=True), sink)
        p = jnp.exp(s - m)
        denom = jnp.sum(p, axis=-1, keepdims=True) + jnp.exp(sink - m)
        o = _dot(p.astype(BF16), vv) * (1.0 / denom)
        acc = o if acc is None else acc + o
    return acc[:rows], acc[rows:]


def _attn_prompt_kernel(sink_ref, q_ref, kp_ref, kc_ref, vp_ref, vc_ref, o_ref):
    n = pl.program_id(1)
    k2 = jnp.concatenate([kp_ref[0], kc_ref[0]], axis=0)
    v2 = jnp.concatenate([vp_ref[0], vc_ref[0]], axis=0)
    row = lax.broadcasted_iota(I32, (2 * BLOCK, 2 * BLOCK), 0) % BLOCK
    col = lax.broadcasted_iota(I32, (2 * BLOCK, 2 * BLOCK), 1)
    mask = (col >= row) & (col <= row + WINDOW) & ((n > 0) | (col >= BLOCK))
    group = N_HEADS // KV_HEADS
    for j in range(KV_HEADS):
        pj = (j // 2) * LANES
        base = j * group * HEAD_DIM
        q0 = q_ref[0, :, base:base + LANES]
        q1 = q_ref[0, :, base + LANES:base + 2 * LANES]
        sinks4 = [sink_ref[j * group + 0], sink_ref[j * group + 1],
                  sink_ref[j * group + 2], sink_ref[j * group + 3]]
        o0, o1 = _attend_group(q0, q1, k2[:, pj:pj + LANES], v2[:, pj:pj + LANES],
                               j % 2 == 0, mask, sinks4)
        o_ref[0, :, base:base + LANES] = o0.astype(o_ref.dtype)
        o_ref[0, :, base + LANES:base + 2 * LANES] = o1.astype(o_ref.dtype)


def _attn_prompt_call(sinks, q, k, v):
    b, l, _ = q.shape
    nb = l // BLOCK
    kv_cur = pl.BlockSpec((1, BLOCK, KV_DIM), lambda i, n: (i, n, 0))
    kv_prev = pl.BlockSpec((1, BLOCK, KV_DIM), lambda i, n: (i, jnp.maximum(n - 1, 0), 0))
    return pl.pallas_call(
        _attn_prompt_kernel,
        grid=(b, nb),
        in_specs=[
            pl.BlockSpec(memory_space=pltpu.SMEM),
            pl.BlockSpec((1, BLOCK, Q_DIM), lambda i, n: (i, n, 0)),
            kv_prev, kv_cur, kv_prev, kv_cur,
        ],
        out_specs=pl.BlockSpec((1, BLOCK, Q_DIM), lambda i, n: (i, n, 0)),
        out_shape=jax.ShapeDtypeStruct((b, l, Q_DIM), BF16),
        compiler_params=_cparams(("parallel", "parallel")),
        name="attn_prompt",
    )(sinks, q, k, k, v, v)


def _attn_sample_kernel(n_new, seqs, sink_ref, q_ref, kn_ref, vn_ref, ck_ref, cv_ref,
                        o_ref, nk_ref, nv_ref):
    rows = ck_ref.shape[1]
    keys = 2 * rows
    n_cols = N_HEADS * n_new
    group = N_HEADS // KV_HEADS
    key = lax.broadcasted_iota(I32, (keys, n_cols), 0)
    qry = lax.broadcasted_iota(I32, (keys, n_cols), 1) % n_new
    mask = ((key < rows) & (key >= qry)) | ((key >= rows) & (key - rows <= qry))
    low = lax.broadcasted_iota(I32, (n_new, LANES), 1) < HEAD_DIM
    sink = sink_ref[...]
    pad = jnp.zeros((rows - n_new, KV_DIM), F32)
    zero_blk = jnp.zeros((n_new, LANES), F32)

    def body(sb, carry):
        r0 = pl.multiple_of(sb * n_new, n_new)
        k_new = kn_ref[pl.ds(r0, n_new), :]
        v_new = vn_ref[pl.ds(r0, n_new), :]
        k_all = jnp.concatenate([ck_ref[sb], k_new, pad], axis=0).astype(BF16)
        v_all = jnp.concatenate([cv_ref[sb], v_new, pad], axis=0).astype(BF16)
        blocks = []
        for h in range(N_HEADS):
            kvh = h // group
            blk = q_ref[pl.ds(r0, n_new), (h // 2) * LANES:(h // 2 + 1) * LANES]
            if (h % 2) != (kvh % 2):
                blk = pltpu.roll(blk, HEAD_DIM, 1)
            blk = jnp.where(low if kvh % 2 == 0 else ~low, blk, 0.0)
            blocks.append(jnp.concatenate([blk, zero_blk] if kvh < 2 else [zero_blk, blk], axis=1))
        qz = jnp.concatenate(blocks, axis=0).astype(BF16)
        s = _dot_nt(k_all, qz) * (HEAD_DIM ** -0.5)
        s = jnp.where(mask, s, -jnp.inf)
        m = jnp.maximum(jnp.max(s, axis=0, keepdims=True), sink)
        p = jnp.exp(s - m)
        denom = jnp.sum(p, axis=0, keepdims=True) + jnp.exp(sink - m)
        w = (p * (1.0 / denom)).T.astype(BF16)
        oz = _dot(w, v_all)
        for c in range(N_HEADS // 2):
            parts = []
            for h in (2 * c, 2 * c + 1):
                kvh = h // group
                blk = oz[h * n_new:(h + 1) * n_new, (kvh // 2) * LANES:(kvh // 2 + 1) * LANES]
                if (h % 2) != (kvh % 2):
                    blk = pltpu.roll(blk, HEAD_DIM, 1)
                parts.append(blk)
            o_ref[pl.ds(r0, n_new), c * LANES:(c + 1) * LANES] = jnp.where(low, parts[0], parts[1])
        nk_ref[sb, 0:rows - n_new, :] = ck_ref[sb, n_new:rows, :]
        nk_ref[sb, rows - n_new:rows, :] = k_new
        nv_ref[sb, 0:rows - n_new, :] = cv_ref[sb, n_new:rows, :]
        nv_ref[sb, rows - n_new:rows, :] = v_new
        return carry

    lax.fori_loop(0, seqs, body, 0)


def _attn_sample_call(sink_row, q, k_new, v_new, cache_k, cache_v, n_new, seqs):
    b, rows, _ = cache_k.shape
    tok = pl.BlockSpec((seqs * n_new, Q_DIM), lambda i: (i, 0))
    tok_kv = pl.BlockSpec((seqs * n_new, KV_DIM), lambda i: (i, 0))
    cache = pl.BlockSpec((seqs, rows, KV_DIM), lambda i: (i, 0, 0))
    return pl.pallas_call(
        functools.partial(_attn_sample_kernel, n_new, seqs),
        grid=(b // seqs,),
        in_specs=[pl.BlockSpec((1, N_HEADS * n_new), lambda i: (0, 0)), tok, tok_kv, tok_kv, cache, cache],
        out_specs=[tok, cache, cache],
        out_shape=[
            jax.ShapeDtypeStruct((b * n_new, Q_DIM), F32),
            jax.ShapeDtypeStruct((b, rows, KV_DIM), F32),
            jax.ShapeDtypeStruct((b, rows, KV_DIM), F32),
        ],
        compiler_params=_cparams(("parallel",)),
        name="attn_sample",
    )(sink_row, q, k_new, v_new, cache_k, cache_v)


def _moe_cap(t_total):
    return t_total + MOE_TILE


def _route_scatter(final, cap, x1, gn_ref, wrh_ref, wrl_ref, br_ref, triu_ref, cntc_in, cntr_in,
                   dest_ref, cntc_out, cntr_out, xs_hbm, pay, zrows, dest_v, dest_s, cntc, cntr, sem):
    i = pl.program_id(0)
    n = pl.num_programs(0)
    slot = i % 2
    tm = x1.shape[0]

    @pl.when(i == 0)
    def _():
        cntc[...] = cntc_in[...]
        cntr[...] = cntr_in[...]

    xn = _rms(x1, gn_ref[...])
    xh = xn.astype(BF16)
    xl = (xn - xh.astype(F32)).astype(BF16)
    logits = _dot(xh, wrh_ref[...]) + _dot(xl, wrh_ref[...]) + _dot(xh, wrl_ref[...]) + br_ref[...]
    lane = lax.broadcasted_iota(I32, logits.shape, 1).astype(F32)
    big = jnp.float32(4 * LANES)
    neg = -jnp.inf
    gl = jnp.where((lane >= N_EXPERTS) & (lane < N_EXPERTS + N_EXPERT_GROUPS), logits, neg)
    gmax = jnp.max(gl, axis=-1, keepdims=True)
    g_val = 1.0 / jnp.sum(jnp.exp(gl - gmax), axis=-1, keepdims=True)
    g_idx = jnp.min(jnp.where(gl == gmax, lane, big), axis=-1, keepdims=True) - N_EXPERTS
    lo = g_idx * EXPERTS_PER_GROUP
    el = jnp.where((lane >= lo) & (lane < lo + EXPERTS_PER_GROUP), logits, neg)
    e1 = jnp.max(el, axis=-1, keepdims=True)
    i1 = jnp.min(jnp.where(el == e1, lane, big), axis=-1, keepdims=True)
    el2 = jnp.where(lane == i1, neg, el)
    e2 = jnp.max(el2, axis=-1, keepdims=True)
    i2 = jnp.min(jnp.where(el2 == e2, lane, big), axis=-1, keepdims=True)
    t = jnp.exp(e2 - e1)
    w1 = 1.0 / (1.0 + t)
    w2 = t / (1.0 + t)
    wts = g_val * (jnp.where(lane == i1 - lo, w1, 0.0) + jnp.where(lane == i2 - lo, w2, 0.0))

    def wait_slot(s):
        pltpu.make_async_copy(pay.at[s], xs_hbm.at[pl.ds(0, tm), :], sem.at[s]).wait()

    @pl.when(i >= 2)
    def _():
        wait_slot(slot)

    xhf = xh.astype(F32)
    wa = lax.bitcast_convert_type(xhf[:, :PAY_X], U32)
    wb = lax.bitcast_convert_type(xhf[:, PAY_X:], U32)
    pay[slot, :, :PAY_X] = (wa >> 16) | (wb & jnp.uint32(HI16))
    pay[slot, :, PAY_X:] = lax.bitcast_convert_type(wts, U32)

    oh = (lane == g_idx).astype(F32)
    oht = oh.T
    rank = _dot(oht.astype(BF16), triu_ref[...])
    grp = lax.broadcasted_iota(I32, (LANES, 1), 0).astype(F32)
    base = grp * float(cap) + cntc[:, 0:1]
    dest = jnp.sum(oht * (rank + base), axis=0, keepdims=True).astype(I32)
    cntc[...] = cntc[...] + jnp.sum(oht, axis=1, keepdims=True)
    cntr[...] = cntr[...] + jnp.sum(oh, axis=0, keepdims=True)
    dest_ref[0] = dest
    dest_v[...] = dest
    pltpu.sync_copy(dest_v, dest_s)

    def issue(r, carry):
        pltpu.make_async_copy(pay.at[slot, pl.ds(r, 1), :],
                              xs_hbm.at[pl.ds(dest_s[0, r], 1), :], sem.at[slot]).start()
        return carry

    lax.fori_loop(0, tm, issue, 0, unroll=8)

    @pl.when(i == n - 1)
    def _():
        cntc_out[...] = cntc[...]
        cntr_out[...] = cntr[...]

        @pl.when(n >= 2)
        def _():
            wait_slot(1 - slot)

        wait_slot(slot)
        if final:
            zrows[...] = jnp.zeros_like(zrows)
            dest_v[:, 0:LANES] = cntr[0:1, :].astype(I32)
            pltpu.sync_copy(dest_v, dest_s)
            for g in range(N_EXPERT_GROUPS):
                c_g = dest_s[0, g]
                c_up = ((c_g + 7) // 8) * 8
                for k in range(7):
                    @pl.when(c_g + k < c_up)
                    def _():
                        one = pltpu.make_async_copy(zrows.at[pl.ds(0, 1), :],
                                                    xs_hbm.at[pl.ds(g * cap + c_g + k, 1), :], sem.at[1])
                        one.start()
                        one.wait()
                start = pl.multiple_of(g * cap + c_up, 8)
                pltpu.make_async_copy(zrows, xs_hbm.at[pl.ds(start, MOE_TILE), :], sem.at[0]).start()
            for g in range(N_EXPERT_GROUPS):
                pltpu.make_async_copy(zrows, xs_hbm.at[pl.ds(0, MOE_TILE), :], sem.at[0]).wait()


def _oproj_kernel(has_prev, final, cap, *refs):
    (o_ref, x_ref, wo_ref, bo_ref) = refs[:4]
    rest = refs[4:]
    if has_prev:
        rest = rest[:7] + rest[8:]
    x1 = x_ref[...] + _dot(o_ref[...].astype(BF16), wo_ref[...]) + bo_ref[...]
    x1_ref = rest[7]
    x1_ref[...] = x1
    _route_scatter(final, cap, x1, *rest[:7], *rest[8:])


def _gelu_tanh(x):
    return x * (0.5 * (1.0 + jnp.tanh(math.sqrt(2.0 / math.pi) * (x + 0.044715 * (x * x * x)))))


def _glu_kernel(has_prev, final, cap, *refs):
    (y_ref, x_ref, gm_ref, d_ref, wa_ref, wb_ref) = refs[:6]
    rest = refs[6:]
    if has_prev:
        rest = rest[:7] + rest[8:]
    x = x_ref[...]
    u = _rms(x, gm_ref[...])
    z = _gelu_tanh(y_ref[...] + d_ref[...] * u).astype(BF16)
    x1 = x + _dot(z, wa_ref[...]) * _sigmoid(_dot(z, wb_ref[...]))
    x1_ref = rest[7]
    x1_ref[...] = x1
    _route_scatter(final, cap, x1, *rest[:7], *rest[8:])


def _row_spec(tm, width):
    return pl.BlockSpec((tm, width), lambda i: (i, 0))


def _const_spec(shape):
    return pl.BlockSpec(shape, lambda i: (0,) * len(shape))


def _mixer_call(body, name, lead_args, lead_specs, x2d, router, triu, cnt, xs_prev, final, cap):
    t = x2d.shape[0]
    tm = TOKEN_TILE
    n_tiles = t // tm
    gn, wrh, wrl, br = router
    cntc, cntr = cnt
    has_prev = xs_prev is not None
    in_specs = lead_specs + [
        _const_spec((1, D_MODEL)), _const_spec((D_MODEL, LANES)), _const_spec((D_MODEL, LANES)),
        _const_spec((1, LANES)), _const_spec((tm, tm)), _const_spec((LANES, LANES)), _const_spec((8, LANES)),
    ]
    args = list(lead_args) + [gn, wrh, wrl, br, triu, cntc, cntr]
    aliases = {}
    if has_prev:
        in_specs.append(pl.BlockSpec(memory_space=pl.ANY))
        args.append(xs_prev)
        aliases = {len(args) - 1: 4}
    x1, dest, cntc2, cntr2, xs = pl.pallas_call(
        functools.partial(body, has_prev, final, cap),
        grid=(n_tiles,),
        in_specs=in_specs,
        out_specs=[
            _row_spec(tm, D_MODEL),
            pl.BlockSpec((1, 1, tm), lambda i: (i, 0, 0)),
            _const_spec((LANES, LANES)), _const_spec((8, LANES)),
            pl.BlockSpec(memory_space=pl.ANY),
        ],
        out_shape=[
            jax.ShapeDtypeStruct((t, D_MODEL), F32),
            jax.ShapeDtypeStruct((n_tiles, 1, tm), I32),
            jax.ShapeDtypeStruct((LANES, LANES), F32),
            jax.ShapeDtypeStruct((8, LANES), F32),
            jax.ShapeDtypeStruct((N_EXPERT_GROUPS * cap, PAY_W), U32),
        ],
        scratch_shapes=[
            pltpu.VMEM((2, tm, PAY_W), U32),
            pltpu.VMEM((MOE_TILE, PAY_W), U32),
            pltpu.VMEM((1, tm), I32),
            pltpu.SMEM((1, tm), I32),
            pltpu.VMEM((LANES, LANES), F32),
            pltpu.VMEM((8, LANES), F32),
            pltpu.SemaphoreType.DMA((2,)),
        ],
        input_output_aliases=aliases,
        compiler_params=_cparams(("arbitrary",)),
        name=name,
    )(*args)
    return x1, dest, (cntc2, cntr2), xs


def _oproj_call(o2d, x2d, wo_bf, bo, router, triu, cnt, xs_prev, final, cap):
    tm = TOKEN_TILE
    lead_specs = [_row_spec(tm, Q_DIM), _row_spec(tm, D_MODEL), _const_spec((Q_DIM, D_MODEL)),
                  _const_spec((1, D_MODEL))]
    return _mixer_call(_oproj_kernel, "oproj_route", [o2d, x2d, wo_bf, bo], lead_specs, x2d,
                       router, triu, cnt, xs_prev, final, cap)


def _glu_call(y2d, x2d, gm, d, wa_bf, wb_bf, router, triu, cnt, xs_prev, final, cap):
    tm = TOKEN_TILE
    lead_specs = [_row_spec(tm, D_MODEL), _row_spec(tm, D_MODEL), _const_spec((1, D_MODEL)),
                  _const_spec((1, D_MODEL)), _const_spec((D_MODEL, D_MODEL)),
                  _const_spec((D_MODEL, D_MODEL))]
    return _mixer_call(_glu_kernel, "glu_route", [y2d, x2d, gm, d, wa_bf, wb_bf], lead_specs, x2d,
                       router, triu, cnt, xs_prev, final, cap)


def _expert_kernel(blk_in_ref, blk_out_ref, grp_ref, valid_ref, xs_ref, wg_ref, wu_ref, wd_ref,
                   ys_ref, wgu, wdn, hid):
    j = pl.program_id(0)
    f = EXPERT_FF
    changed = (j == 0) | (grp_ref[j] != grp_ref[jnp.maximum(j - 1, 0)])

    @pl.when(changed)
    def _():
        for e in range(EXPERTS_PER_GROUP):
            wgu[e, :, :f] = wg_ref[e].astype(BF16)
            wgu[e, :, f:] = wu_ref[e].astype(BF16)
            wdn[e * f:(e + 1) * f, :] = wd_ref[e].astype(BF16)

    @pl.when(valid_ref[j] == 1)
    def _():
        words = xs_ref[:, :PAY_X]
        xa = lax.bitcast_convert_type(words << 16, F32)
        xb = lax.bitcast_convert_type(words & jnp.uint32(HI16), F32)
        x = jnp.concatenate([xa, xb], axis=1).astype(BF16)
        wts = lax.bitcast_convert_type(xs_ref[:, PAY_X:], F32)
        for e in range(EXPERTS_PER_GROUP):
            gu = _dot(x, wgu[e])
            g, u = gu[:, :f], gu[:, f:]
            hid[:, e * f:(e + 1) * f] = ((g * _sigmoid(g)) * u * wts[:, e:e + 1]).astype(BF16)
        ys_ref[...] = _dot(hid[...], wdn[...])

    @pl.when(valid_ref[j] == 0)
    def _():
        ys_ref[...] = jnp.zeros_like(ys_ref)


def _expert_tile_map(cnt_row, cap, n_steps):
    per = cap // MOE_TILE
    c = cnt_row[0, :N_EXPERT_GROUPS].astype(I32)
    tiles = (c + MOE_TILE - 1) // MOE_TILE
    ends = jnp.cumsum(tiles)
    starts = ends - tiles
    total = ends[-1]
    j = jnp.arange(n_steps, dtype=I32)
    jj = jnp.minimum(j, total - 1)
    g = jnp.sum((jj[:, None] >= ends[None, :]).astype(I32), axis=1)
    blk_in = g * per + jj - starts[g]
    valid = (j < total).astype(I32)
    blk_out = jnp.where(valid == 1, blk_in, N_EXPERT_GROUPS * per)
    return blk_in, blk_out, g, valid


def _expert_call(cnt_row, xs, w_gate, w_up, w_down, cap, t_total):
    n_steps = t_total // MOE_TILE + N_EXPERT_GROUPS
    maps = _expert_tile_map(cnt_row, cap, n_steps)
    e, f = EXPERTS_PER_GROUP, EXPERT_FF
    grid_spec = pltpu.PrefetchScalarGridSpec(
        num_scalar_prefetch=4,
        grid=(n_steps,),
        in_specs=[
            pl.BlockSpec((MOE_TILE, PAY_W), lambda j, bi, bo, gr, va: (bi[j], 0)),
            pl.BlockSpec((e, D_MODEL, f), lambda j, bi, bo, gr, va: (gr[j], 0, 0)),
            pl.BlockSpec((e, D_MODEL, f), lambda j, bi, bo, gr, va: (gr[j], 0, 0)),
            pl.BlockSpec((e, f, D_MODEL), lambda j, bi, bo, gr, va: (gr[j], 0, 0)),
        ],
        out_specs=pl.BlockSpec((MOE_TILE, D_MODEL), lambda j, bi, bo, gr, va: (bo[j], 0)),
        scratch_shapes=[
            pltpu.VMEM((e, D_MODEL, 2 * f), BF16),
            pltpu.VMEM((e * f, D_MODEL), BF16),
            pltpu.VMEM((MOE_TILE, e * f), BF16),
        ],
    )
    return pl.pallas_call(
        _expert_kernel,
        grid_spec=grid_spec,
        out_shape=jax.ShapeDtypeStruct((N_EXPERT_GROUPS * cap + MOE_TILE, D_MODEL), F32),
        compiler_params=_cparams(("arbitrary",)),
        name="moe_experts",
    )(*maps, xs, w_gate, w_up, w_down)


def _combine_kernel(emit_x, dest_ref, destn_ref, x_ref, gnext_ref, ys_hbm, *rest):
    outs, (ybuf, sem) = rest[:-2], rest[-2:]
    i = pl.program_id(0)
    n = pl.num_programs(0)
    slot = i % 2
    tm = x_ref.shape[0]

    def gather(dref, s):
        def issue(r, carry):
            pltpu.make_async_copy(ys_hbm.at[pl.ds(dref[0, 0, r], 1), :],
                                  ybuf.at[s, pl.ds(r, 1), :], sem.at[s]).start()
            return carry
        lax.fori_loop(0, tm, issue, 0, unroll=8)

    @pl.when(i == 0)
    def _():
        gather(dest_ref, 0)

    @pl.when(i + 1 < n)
    def _():
        gather(destn_ref, 1 - slot)

    pltpu.make_async_copy(ys_hbm.at[pl.ds(0, tm), :], ybuf.at[slot], sem.at[slot]).wait()
    x2 = x_ref[...] + ybuf[slot]
    normed = _rms(x2, gnext_ref[...])
    if emit_x:
        outs[0][...] = x2
        outs[1][...] = normed
    else:
        outs[0][...] = normed


def _combine_call(dest, x2d, ys, gnext, emit_x):
    t = x2d.shape[0]
    tm = TOKEN_TILE
    n_tiles = t // tm
    n_out = 2 if emit_x else 1
    return pl.pallas_call(
        functools.partial(_combine_kernel, emit_x),
        grid=(n_tiles,),
        in_specs=[
            pl.BlockSpec((1, 1, tm), lambda i: (i, 0, 0), memory_space=pltpu.SMEM),
            pl.BlockSpec((1, 1, tm), lambda i: (jnp.minimum(i + 1, n_tiles - 1), 0, 0),
                         memory_space=pltpu.SMEM),
            _row_spec(tm, D_MODEL), _const_spec((1, D_MODEL)),
            pl.BlockSpec(memory_space=pl.ANY),
        ],
        out_specs=[_row_spec(tm, D_MODEL)] * n_out,
        out_shape=[jax.ShapeDtypeStruct((t, D_MODEL), F32)] * n_out,
        scratch_shapes=[pltpu.VMEM((2, tm, D_MODEL), F32), pltpu.SemaphoreType.DMA((2,))],
        compiler_params=_cparams(("arbitrary",)),
        name="moe_combine",
    )(dest, dest, x2d, gnext, ys)


def _s5_state_in(u_ref_val, wre_ref, wim_ref, sre_ref, sim_ref, pair_w):
    for m in range(S5_GB // 2):
        up = u_ref_val(m * pair_w, pair_w)
        sre_ref[:, m * LANES:(m + 1) * LANES] = _dot(up, wre_ref[m])
        sim_ref[:, m * LANES:(m + 1) * LANES] = _dot(up, wim_ref[m])


def _s5_outputs(u_ref_val, hre, him, m_ref, gre_ref, gim_ref, y_store, pair_w):
    gw = pair_w // 2
    for m in range(S5_GB // 2):
        hr = hre(m).astype(BF16)
        hi = him(m).astype(BF16)
        y = _dot(hr, gre_ref[m]) + _dot(hi, gim_ref[m])
        y0 = y[:, :gw] + _dot(u_ref_val(m * pair_w, gw), m_ref[2 * m])
        y1 = y[:, gw:] + _dot(u_ref_val(m * pair_w + gw, gw), m_ref[2 * m + 1])
        y_store(m * pair_w, gw, y0)
        y_store(m * pair_w + gw, gw, y1)


def _s5_prompt_kernel(u_ref, wre_ref, wim_ref, m_ref, gre_ref, gim_ref, aqr_ref, aqi_ref,
                      y_ref, her_ref, hei_ref, ut, uflat, sre, sim, hre, him, yflat, yt):
    gc = SSM_GROUP_CH
    qc = S5_CHUNK * gc
    pair_w = 2 * qc
    n_chunks = u_ref.shape[1] // S5_CHUNK
    for s in range(S5_CHUNK):
        xs = u_ref[0, pl.ds(s, n_chunks, stride=S5_CHUNK), :]
        ut[:, s * gc:(s + 1) * gc, :] = xs.T.reshape(S5_GB, gc, n_chunks)
    for g in range(S5_GB):
        uflat[:, g * qc:(g + 1) * qc] = ut[g].T.astype(BF16)
    u_val = lambda off, w: uflat[:, off:off + w]
    _s5_state_in(u_val, wre_ref, wim_ref, sre, sim, pair_w)
    ar = aqr_ref[...]
    ai = aqi_ref[...]

    def step(n, carry):
        hr, hi = carry
        hre[pl.ds(n, 1), :] = hr
        him[pl.ds(n, 1), :] = hi
        sr = sre[pl.ds(n, 1), :]
        si = sim[pl.ds(n, 1), :]
        return ar * hr - ai * hi + sr, ar * hi + ai * hr + si

    zero = jnp.zeros(ar.shape, F32)
    hr, hi = lax.fori_loop(0, n_chunks, step, (zero, zero))
    her_ref[0] = hr
    hei_ref[0] = hi

    def y_store(off, w, val):
        yflat[:, off:off + w] = val

    _s5_outputs(u_val, lambda m: hre[:, m * LANES:(m + 1) * LANES],
                lambda m: him[:, m * LANES:(m + 1) * LANES],
                m_ref, gre_ref, gim_ref, y_store, pair_w)
    for g in range(S5_GB):
        yt[g] = yflat[:, g * qc:(g + 1) * qc].T
    for t in range(S5_CHUNK):
        z = yt[:, t * gc:(t + 1) * gc, :].reshape(S5_GB * gc, n_chunks)
        y_ref[0, pl.ds(t, n_chunks, stride=S5_CHUNK), :] = z.T


def _s5_sample_kernel(n_new, u_ref, h0r_ref, h0i_ref, wre_ref, wim_ref, m_ref, gre_ref, gim_ref,
                      aqr_ref, aqi_ref, y_ref, hnr_ref, hni_ref, sre, sim):
    pair_w = 2 * n_new * SSM_GROUP_CH
    u_val = lambda off, w: u_ref[:, off:off + w]
    _s5_state_in(u_val, wre_ref, wim_ref, sre, sim, pair_w)
    ar = aqr_ref[...]
    ai = aqi_ref[...]
    h0r = h0r_ref[...]
    h0i = h0i_ref[...]
    hnr_ref[...] = ar * h0r - ai * h0i + sre[...]
    hni_ref[...] = ar * h0i + ai * h0r + sim[...]

    def y_store(off, w, val):
        y_ref[:, off:off + w] = val

    _s5_outputs(u_val, lambda m: h0r_ref[:, m * LANES:(m + 1) * LANES],
                lambda m: h0i_ref[:, m * LANES:(m + 1) * LANES],
                m_ref, gre_ref, gim_ref, y_store, pair_w)


def _s5_weight_specs(q, idx):
    qc = q * SSM_GROUP_CH
    np_ = S5_GB // 2
    st = S5_GB * SSM_STATE
    return [
        pl.BlockSpec((np_, 2 * qc, LANES), lambda *a: (idx(*a), 0, 0)),
        pl.BlockSpec((np_, 2 * qc, LANES), lambda *a: (idx(*a), 0, 0)),
        pl.BlockSpec((S5_GB, qc, qc), lambda *a: (idx(*a), 0, 0)),
        pl.BlockSpec((np_, LANES, 2 * qc), lambda *a: (idx(*a), 0, 0)),
        pl.BlockSpec((np_, LANES, 2 * qc), lambda *a: (idx(*a), 0, 0)),
        pl.BlockSpec((1, st), lambda *a: (0, idx(*a))),
        pl.BlockSpec((1, st), lambda *a: (0, idx(*a))),
    ]


def _s5_prompt_call(u, w):
    b, seq, _ = u.shape
    n_chunks = seq // S5_CHUNK
    gbl = S5_GB * SSM_GROUP_CH
    qc = S5_CHUNK * SSM_GROUP_CH
    st = S5_GB * SSM_STATE
    n_gb = SSM_GROUPS // S5_GB
    gb_of = lambda g, i: g
    tok = pl.BlockSpec((1, seq, gbl), lambda g, i: (i, 0, g))
    return pl.pallas_call(
        _s5_prompt_kernel,
        grid=(n_gb, b),
        in_specs=[tok] + _s5_weight_specs(S5_CHUNK, gb_of),
        out_specs=[
            tok,
            pl.BlockSpec((1, 1, st), lambda g, i: (i, 0, g)),
            pl.BlockSpec((1, 1, st), lambda g, i: (i, 0, g)),
        ],
        out_shape=[
            jax.ShapeDtypeStruct((b, seq, D_MODEL), F32),
            jax.ShapeDtypeStruct((b, 1, SSM_GROUPS * SSM_STATE), F32),
            jax.ShapeDtypeStruct((b, 1, SSM_GROUPS * SSM_STATE), F32),
        ],
        scratch_shapes=[
            pltpu.VMEM((S5_GB, qc, n_chunks), F32),
            pltpu.VMEM((n_chunks, S5_GB * qc), BF16),
            pltpu.VMEM((n_chunks, st), F32), pltpu.VMEM((n_chunks, st), F32),
            pltpu.VMEM((n_chunks, st), F32), pltpu.VMEM((n_chunks, st), F32),
            pltpu.VMEM((n_chunks, S5_GB * qc), F32),
            pltpu.VMEM((S5_GB, qc, n_chunks), F32),
        ],
        compiler_params=_cparams(("parallel", "parallel")),
        name="s5_prompt",
    )(u, *w)


def _s5_sample_call(uf, h0r, h0i, w, n_new):
    b, width = uf.shape
    gbw = S5_GB * n_new * SSM_GROUP_CH
    st = S5_GB * SSM_STATE
    n_gb = SSM_GROUPS // S5_GB
    gb_of = lambda g: g
    state = pl.BlockSpec((b, st), lambda g: (0, g))
    return pl.pallas_call(
        functools.partial(_s5_sample_kernel, n_new),
        grid=(n_gb,),
        in_specs=[pl.BlockSpec((b, gbw), lambda g: (0, g)), state, state] + _s5_weight_specs(n_new, gb_of),
        out_specs=[pl.BlockSpec((b, gbw), lambda g: (0, g)), state, state],
        out_shape=[
            jax.ShapeDtypeStruct((b, width), F32),
            jax.ShapeDtypeStruct((b, SSM_GROUPS * SSM_STATE), F32),
            jax.ShapeDtypeStruct((b, SSM_GROUPS * SSM_STATE), F32),
        ],
        scratch_shapes=[pltpu.VMEM((b, st), F32)] * 2,
        compiler_params=_cparams(("parallel",)),
        name="s5_sample",
    )(uf, h0r, h0i, *w)


def _block_diag_pairs(w):
    g, r, n = w.shape
    w2 = w.reshape(g // 2, 2, r, n)
    z = jnp.zeros((g // 2, r, n), w.dtype)
    top = jnp.concatenate([w2[:, 0], z], axis=2)
    bot = jnp.concatenate([z, w2[:, 1]], axis=2)
    return jnp.concatenate([top, bot], axis=1)


def _s5_discretize(a_re, a_im, log_dt, b_re, b_im):
    delta = jnp.exp(log_dt.astype(F32))[:, None]
    lr, li = a_re.astype(F32), a_im.astype(F32)
    mag = jnp.exp(delta * lr)
    abar_r = mag * jnp.cos(delta * li)
    abar_i = mag * jnp.sin(delta * li)
    nr, ni = abar_r - 1.0, abar_i
    den = lr * lr + li * li
    coef_r = ((nr * lr + ni * li) / den)[..., None]
    coef_i = ((ni * lr - nr * li) / den)[..., None]
    br, bi = b_re.astype(F32), b_im.astype(F32)
    return abar_r, abar_i, coef_r * br - coef_i * bi, coef_r * bi + coef_i * br


def _s5_chunk_weights(abar_r, abar_i, bbar_r, bbar_i, c_re, c_im, q):
    g, p = abar_r.shape
    c = SSM_GROUP_CH
    hi = lax.Precision.HIGHEST
    pows_r, pows_i = [jnp.ones_like(abar_r)], [jnp.zeros_like(abar_r)]
    for _ in range(q):
        pr, pi = pows_r[-1], pows_i[-1]
        pows_r.append(pr * abar_r - pi * abar_i)
        pows_i.append(pr * abar_i + pi * abar_r)
    pw_r, pw_i = jnp.stack(pows_r), jnp.stack(pows_i)
    rev_r, rev_i = pw_r[:q][::-1][..., None], pw_i[:q][::-1][..., None]
    wst_r = (rev_r * bbar_r[None] - rev_i * bbar_i[None]).transpose(1, 0, 3, 2).reshape(g, q * c, p)
    wst_i = (rev_r * bbar_i[None] + rev_i * bbar_r[None]).transpose(1, 0, 3, 2).reshape(g, q * c, p)
    x_r = pw_r[:q, :, :, None] * bbar_r[None] - pw_i[:q, :, :, None] * bbar_i[None]
    x_i = pw_r[:q, :, :, None] * bbar_i[None] + pw_i[:q, :, :, None] * bbar_r[None]
    cr, ci = c_re.astype(F32), c_im.astype(F32)
    ker = (jnp.einsum('gcp,tgpd->gtdc', cr, x_r, precision=hi)
           - jnp.einsum('gcp,tgpd->gtdc', ci, x_i, precision=hi))
    idx = jnp.arange(q)
    tau = idx[None, :] - idx[:, None]
    m5 = jnp.where((tau >= 0)[None, :, :, None, None], ker[:, jnp.clip(tau, 0, q - 1)], 0.0)
    m = m5.transpose(0, 1, 3, 2, 4).reshape(g, q * c, q * c)
    cr_t, ci_t = cr.transpose(0, 2, 1)[:, :, None, :], ci.transpose(0, 2, 1)[:, :, None, :]
    pr_t = pw_r[1:q + 1].transpose(1, 2, 0)[..., None]
    pi_t = pw_i[1:q + 1].transpose(1, 2, 0)[..., None]
    g_re = (cr_t * pr_t - ci_t * pi_t).reshape(g, p, q * c)
    g_im = (-(cr_t * pi_t + ci_t * pr_t)).reshape(g, p, q * c)
    return (_block_diag_pairs(wst_r).astype(BF16), _block_diag_pairs(wst_i).astype(BF16),
            m.astype(BF16), _block_diag_pairs(g_re).astype(BF16), _block_diag_pairs(g_im).astype(BF16),
            pw_r[q].reshape(1, g * p), pw_i[q].reshape(1, g * p))


def _rope_tables(pos):
    half = HEAD_DIM // 2
    inv = 1.0 / (ROPE_THETA ** (jnp.arange(half, dtype=F32) * (2.0 / HEAD_DIM)))
    ang = pos.astype(F32)[:, None] * inv[None, :]
    cos, sin = jnp.cos(ang), jnp.sin(ang)
    return jnp.tile(cos, (1, 4)), jnp.concatenate([-sin, sin, -sin, sin], axis=1)


def _router_weights(gain, w_rg, b_rg, w_re, b_re):
    pad = LANES - N_EXPERTS - N_EXPERT_GROUPS
    w = jnp.concatenate([w_re, w_rg, jnp.zeros((D_MODEL, pad), F32)], axis=1)
    b = jnp.concatenate([b_re, b_rg, jnp.zeros((pad,), F32)]).reshape(1, LANES)
    wh = w.astype(BF16)
    wl = (w - wh.astype(F32)).astype(BF16)
    return gain.reshape(1, -1).astype(F32), wh, wl, b


def kernel(x_prompt, x_sample, cache_k, cache_v, state_ssm_re, state_ssm_im, norm_mix, norm_ffn, norm_final, attn_w_qkv, attn_b_qkv, attn_w_o, attn_b_o, attn_sinks, ssm_a_re, ssm_a_im, ssm_log_dt, ssm_b_re, ssm_b_im, ssm_c_re, ssm_c_im, ssm_d, ssm_w_glu_a, ssm_w_glu_b, moe_w_router_group, moe_b_router_group, moe_w_router_expert, moe_b_router_expert, moe_w_gate, moe_w_up, moe_w_down):
    bsz, seq, _ = x_prompt.shape
    dbs, n_new, _ = x_sample.shape
    rows = cache_k.shape[2]
    tp, ts = bsz * seq, dbs * n_new
    cap = _moe_cap(tp + ts)
    xp = x_prompt.reshape(tp, D_MODEL)
    xs = x_sample.reshape(ts, D_MODEL)

    row1 = lambda v: v.reshape(1, -1).astype(F32)
    routers = [_router_weights(norm_ffn[l], moe_w_router_group[l], moe_b_router_group[l],
                               moe_w_router_expert[l], moe_b_router_expert[l]) for l in range(2)]
    triu = jnp.triu(jnp.ones((TOKEN_TILE, TOKEN_TILE), F32), 1).astype(BF16)
    cnt0 = (jnp.zeros((LANES, LANES), F32), jnp.zeros((8, LANES), F32))

    wqkv = attn_w_qkv[0].astype(BF16)
    bqkv = row1(attn_b_qkv[0])
    wo = attn_w_o[0].astype(BF16)
    bo = row1(attn_b_o[0])
    sinks = attn_sinks[0].astype(F32)
    g_mix0, g_mix1 = row1(norm_mix[0]), row1(norm_mix[1])
    cos_p, sin_p = _rope_tables(jnp.arange(seq, dtype=I32))
    pos_s = jnp.tile(PAST_LEN + jnp.arange(n_new, dtype=I32), dbs)
    cos_s, sin_s = _rope_tables(pos_s)

    qp, kp, vp = _qkv_call(xp, g_mix0, wqkv, bqkv, cos_p, sin_p, BF16)
    qs, ks, vs = _qkv_call(xs, g_mix0, wqkv, bqkv, cos_s, sin_s, F32)
    op = _attn_prompt_call(sinks, qp.reshape(bsz, seq, Q_DIM), kp.reshape(bsz, seq, KV_DIM),
                           vp.reshape(bsz, seq, KV_DIM))
    os_, nks, nvs = _attn_sample_call(jnp.repeat(sinks, n_new).reshape(1, -1), qs, ks, vs,
                                      cache_k[0].reshape(dbs, rows, KV_DIM),
                                      cache_v[0].reshape(dbs, rows, KV_DIM), n_new, 8)
    xp1, dest_p, cnt, rows_x = _oproj_call(op.reshape(tp, Q_DIM), xp, wo, bo, routers[0], triu,
                                           cnt0, None, False, cap)
    xs1, dest_s, cnt, rows_x = _oproj_call(os_, xs, wo, bo, routers[0], triu, cnt, rows_x, True, cap)
    rows_y = _expert_call(cnt[1], rows_x, moe_w_gate[0], moe_w_up[0], moe_w_down[0], cap, tp + ts)
    xp2, up = _combine_call(dest_p, xp1, rows_y, g_mix1, True)
    xs2, us = _combine_call(dest_s, xs1, rows_y, g_mix1, True)

    disc = _s5_discretize(ssm_a_re[0], ssm_a_im[0], ssm_log_dt[0], ssm_b_re[0], ssm_b_im[0])
    w_p = _s5_chunk_weights(*disc, ssm_c_re[0], ssm_c_im[0], S5_CHUNK)
    w_s = _s5_chunk_weights(*disc, ssm_c_re[0], ssm_c_im[0], n_new)
    gch = (SSM_GROUPS, SSM_GROUP_CH)
    uf_s = us.astype(BF16).reshape(dbs, n_new, *gch).transpose(0, 2, 1, 3).reshape(dbs, -1)
    y_p, hpr, hpi = _s5_prompt_call(up.reshape(bsz, seq, D_MODEL), w_p)
    y_p = y_p.reshape(tp, D_MODEL)
    h0r = state_ssm_re[0].reshape(dbs, -1).astype(F32)
    h0i = state_ssm_im[0].reshape(dbs, -1).astype(F32)
    yf_s, hsr, hsi = _s5_sample_call(uf_s, h0r, h0i, w_s, n_new)
    y_s = yf_s.reshape(dbs, SSM_GROUPS, n_new, SSM_GROUP_CH).transpose(0, 2, 1, 3).reshape(ts, D_MODEL)

    wa, wb = ssm_w_glu_a[0].astype(BF16), ssm_w_glu_b[0].astype(BF16)
    d_row = row1(ssm_d[0])
    xp3, dest_p, cnt, rows_x = _glu_call(y_p, xp2, g_mix1, d_row, wa, wb, routers[1], triu,
                                         cnt0, None, False, cap)
    xs3, dest_s, cnt, rows_x = _glu_call(y_s, xs2, g_mix1, d_row, wa, wb, routers[1], triu,
                                         cnt, rows_x, True, cap)
    rows_y = _expert_call(cnt[1], rows_x, moe_w_gate[1], moe_w_up[1], moe_w_down[1], cap, tp + ts)
    g_fin = row1(norm_final)
    (yp,) = _combine_call(dest_p, xp3, rows_y, g_fin, False)
    (ys,) = _combine_call(dest_s, xs3, rows_y, g_fin, False)

    kv5 = lambda a, n: a.reshape(1, n, rows, KV_HEADS, HEAD_DIM)
    st4 = lambda a, n: a.reshape(1, n, SSM_GROUPS, SSM_STATE)
    k_last = kp.reshape(bsz, seq, KV_DIM)[:, seq - WINDOW:]
    v_last = vp.reshape(bsz, seq, KV_DIM)[:, seq - WINDOW:]
    return (yp.reshape(bsz, seq, D_MODEL), ys.reshape(dbs, n_new, D_MODEL),
            k_last.reshape(1, bsz, WINDOW, KV_HEADS, HEAD_DIM), kv5(nks, dbs),
            v_last.reshape(1, bsz, WINDOW, KV_HEADS, HEAD_DIM), kv5(nvs, dbs),
            st4(hpr, bsz), st4(hsr, dbs), st4(hpi, bsz), st4(hsi, dbs))
```

```python
import functools
import math

import jax
import jax.numpy as jnp
from jax import lax
from jax.experimental import pallas as pl
from jax.experimental.pallas import tpu as pltpu

F32 = jnp.float32
BF16 = jnp.bfloat16
I32 = jnp.int32

D_MODEL = 1024
N_HEADS = 16
KV_HEADS = 4
HEAD_DIM = 64
Q_DIM = N_HEADS * HEAD_DIM
KV_DIM = KV_HEADS * HEAD_DIM
QKV_DIM = Q_DIM + 2 * KV_DIM
WINDOW = 128
BLOCK = 128
ROPE_THETA = 10000.0
PAST_LEN = 16384
SSM_GROUP_CH = 16
SSM_GROUPS = D_MODEL // SSM_GROUP_CH
SSM_STATE = 64
N_EXPERT_GROUPS = 4
EXPERTS_PER_GROUP = 8
N_EXPERTS = N_EXPERT_GROUPS * EXPERTS_PER_GROUP
EXPERT_FF = 128
NORM_EPS = 1e-5

LANES = 128
VMEM_LIMIT = 56 * 1024 * 1024
S5_CHUNK = 16
S5_GB = 8
TOKEN_TILE = 512
MOE_TILE = TOKEN_TILE
PAY_X = D_MODEL
PAY_W = PAY_X + LANES


def _cparams(sem):
    return pltpu.CompilerParams(dimension_semantics=sem, vmem_limit_bytes=VMEM_LIMIT)


def _rms(x, g):
    return x * lax.rsqrt(jnp.mean(x * x, axis=-1, keepdims=True) + NORM_EPS) * g


def _dot(a, b):
    return jnp.dot(a, b, preferred_element_type=F32)


def _dot_nt(a, b):
    return lax.dot_general(a, b, (((1,), (1,)), ((), ())), preferred_element_type=F32)


def _sigmoid(x):
    return 1.0 / (1.0 + jnp.exp(-x))


def _qkv_kernel(x_ref, g_ref, w_ref, b_ref, cos_ref, sin_ref, q_ref, k_ref, v_ref):
    xn = _rms(x_ref[...], g_ref[...])
    qkv = _dot(xn.astype(BF16), w_ref[...]) + b_ref[...]
    cos = cos_ref[...]
    sin = sin_ref[...]
    lane = lax.broadcasted_iota(I32, cos.shape, 1)
    first_half = (lane % HEAD_DIM) < (HEAD_DIM // 2)
    n_rot = (Q_DIM + KV_DIM) // LANES
    for c in range(n_rot):
        blk = qkv[:, c * LANES:(c + 1) * LANES]
        partner = jnp.where(first_half,
                            pltpu.roll(blk, LANES - HEAD_DIM // 2, 1),
                            pltpu.roll(blk, HEAD_DIM // 2, 1))
        rot = blk * cos + partner * sin
        if c < Q_DIM // LANES:
            q_ref[:, c * LANES:(c + 1) * LANES] = rot.astype(q_ref.dtype)
        else:
            k_ref[:, c * LANES - Q_DIM:(c + 1) * LANES - Q_DIM] = rot
    v_ref[...] = qkv[:, Q_DIM + KV_DIM:]


def _qkv_call(x2d, gain, w_bf, bias, cos_t, sin_t, q_dtype):
    t = x2d.shape[0]
    tm = TOKEN_TILE
    n_pos = cos_t.shape[0] // tm
    return pl.pallas_call(
        _qkv_kernel,
        grid=(t // tm,),
        in_specs=[
            pl.BlockSpec((tm, D_MODEL), lambda i: (i, 0)),
            pl.BlockSpec((1, D_MODEL), lambda i: (0, 0)),
            pl.BlockSpec((D_MODEL, QKV_DIM), lambda i: (0, 0)),
            pl.BlockSpec((1, QKV_DIM), lambda i: (0, 0)),
            pl.BlockSpec((tm, LANES), lambda i: (i % n_pos, 0)),
            pl.BlockSpec((tm, LANES), lambda i: (i % n_pos, 0)),
        ],
        out_specs=[
            pl.BlockSpec((tm, Q_DIM), lambda i: (i, 0)),
            pl.BlockSpec((tm, KV_DIM), lambda i: (i, 0)),
            pl.BlockSpec((tm, KV_DIM), lambda i: (i, 0)),
        ],
        out_shape=[
            jax.ShapeDtypeStruct((t, Q_DIM), q_dtype),
            jax.ShapeDtypeStruct((t, KV_DIM), F32),
            jax.ShapeDtypeStruct((t, KV_DIM), F32),
        ],
        compiler_params=_cparams(("parallel",)),
        name="qkv_rope",
    )(x2d, gain, w_bf, bias, cos_t, sin_t)


def _split_kv_pair(pair_f32, head_in_low_half):
    lane = lax.broadcasted_iota(I32, pair_f32.shape, 1)
    if head_in_low_half:
        lo = jnp.where(lane < HEAD_DIM, pair_f32, 0.0)
        hi = pltpu.roll(lo, HEAD_DIM, 1)
    else:
        hi = jnp.where(lane >= HEAD_DIM, pair_f32, 0.0)
        lo = pltpu.roll(hi, HEAD_DIM, 1)
    return lo.astype(BF16), hi.astype(BF16)


def _attend_group(q0, q1, k_pair, v_pair, head_in_low_half, mask, sinks4):
    rows = q0.shape[0]
    k_lo, k_hi = _split_kv_pair(k_pair, head_in_low_half)
    v_lo, v_hi = _split_kv_pair(v_pair, head_in_low_half)
    qcat = jnp.concatenate([q0, q1], axis=0).astype(BF16)
    row = lax.broadcasted_iota(I32, (2 * rows, 1), 0)
    top = row < rows
    acc = None
    for half, (kk, vv) in enumerate(((k_lo, v_lo), (k_hi, v_hi))):
        sink = jnp.where(top, sinks4[half], sinks4[2 + half])
        s = _dot_nt(qcat, kk) * (HEAD_DIM ** -0.5)
        s = jnp.where(mask, s, -jnp.inf)
        m = jnp.maximum(jnp.max(s, axis=-1, keepdims=True), sink)
        p = jnp.exp(s - m)
        denom = jnp.sum(p, axis=-1, keepdims=True) + jnp.exp(sink - m)
        o = _dot(p.astype(BF16), vv) * (1.0 / denom)
        acc = o if acc is None else acc + o
    return acc[:rows], acc[rows:]


def _attn_prompt_kernel(sink_ref, q_ref, kp_ref, kc_ref, vp_ref, vc_ref, o_ref):
    n = pl.program_id(1)
    k2 = jnp.concatenate([kp_ref[0], kc_ref[0]], axis=0)
    v2 = jnp.concatenate([vp_ref[0], vc_ref[0]], axis=0)
    row = lax.broadcasted_iota(I32, (2 * BLOCK, 2 * BLOCK), 0) % BLOCK
    col = lax.broadcasted_iota(I32, (2 * BLOCK, 2 * BLOCK), 1)
    mask = (col >= row) & (col <= row + WINDOW) & ((n > 0) | (col >= BLOCK))
    group = N_HEADS // KV_HEADS
    for j in range(KV_HEADS):
        pj = (j // 2) * LANES
        base = j * group * HEAD_DIM
        q0 = q_ref[0, :, base:base + LANES]
        q1 = q_ref[0, :, base + LANES:base + 2 * LANES]
        sinks4 = [sink_ref[j * group + 0], sink_ref[j * group + 1],
                  sink_ref[j * group + 2], sink_ref[j * group + 3]]
        o0, o1 = _attend_group(q0, q1, k2[:, pj:pj + LANES], v2[:, pj:pj + LANES],
                               j % 2 == 0, mask, sinks4)
        o_ref[0, :, base:base + LANES] = o0.astype(o_ref.dtype)
        o_ref[0, :, base + LANES:base + 2 * LANES] = o1.astype(o_ref.dtype)


def _attn_prompt_call(sinks, q, k, v):
    b, l, _ = q.shape
    nb = l // BLOCK
    kv_cur = pl.BlockSpec((1, BLOCK, KV_DIM), lambda i, n: (i, n, 0))
    kv_prev = pl.BlockSpec((1, BLOCK, KV_DIM), lambda i, n: (i, jnp.maximum(n - 1, 0), 0))
    return pl.pallas_call(
        _attn_prompt_kernel,
        grid=(b, nb),
        in_specs=[
            pl.BlockSpec(memory_space=pltpu.SMEM),
            pl.BlockSpec((1, BLOCK, Q_DIM), lambda i, n: (i, n, 0)),
            kv_prev, kv_cur, kv_prev, kv_cur,
        ],
        out_specs=pl.BlockSpec((1, BLOCK, Q_DIM), lambda i, n: (i, n, 0)),
        out_shape=jax.ShapeDtypeStruct((b, l, Q_DIM), BF16),
        compiler_params=_cparams(("parallel", "parallel")),
        name="attn_prompt",
    )(sinks, q, k, k, v, v)


def _attn_sample_kernel(n_new, seqs, sink_ref, q_ref, kn_ref, vn_ref, ck_ref, cv_ref,
                        o_ref, nk_ref, nv_ref):
    rows = ck_ref.shape[1]
    keys = 2 * rows
    n_cols = N_HEADS * n_new
    group = N_HEADS // KV_HEADS
    key = lax.broadcasted_iota(I32, (keys, n_cols), 0)
    qry = lax.broadcasted_iota(I32, (keys, n_cols), 1) % n_new
    mask = ((key < rows) & (key >= qry)) | ((key >= rows) & (key - rows <= qry))
    low = lax.broadcasted_iota(I32, (n_new, LANES), 1) < HEAD_DIM
    sink = sink_ref[...]
    pad = jnp.zeros((rows - n_new, KV_DIM), F32)
    zero_blk = jnp.zeros((n_new, LANES), F32)

    def body(sb, carry):
        r0 = pl.multiple_of(sb * n_new, n_new)
        k_new = kn_ref[pl.ds(r0, n_new), :]
        v_new = vn_ref[pl.ds(r0, n_new), :]
        k_all = jnp.concatenate([ck_ref[sb], k_new, pad], axis=0).astype(BF16)
        v_all = jnp.concatenate([cv_ref[sb], v_new, pad], axis=0).astype(BF16)
        blocks = []
        for h in range(N_HEADS):
            kvh = h // group
            blk = q_ref[pl.ds(r0, n_new), (h // 2) * LANES:(h // 2 + 1) * LANES]
            if (h % 2) != (kvh % 2):
                blk = pltpu.roll(blk, HEAD_DIM, 1)
            blk = jnp.where(low if kvh % 2 == 0 else ~low, blk, 0.0)
            blocks.append(jnp.concatenate([blk, zero_blk] if kvh < 2 else [zero_blk, blk], axis=1))
        qz = jnp.concatenate(blocks, axis=0).astype(BF16)
        s = _dot_nt(k_all, qz) * (HEAD_DIM ** -0.5)
        s = jnp.where(mask, s, -jnp.inf)
        m = jnp.maximum(jnp.max(s, axis=0, keepdims=True), sink)
        p = jnp.exp(s - m)
        denom = jnp.sum(p, axis=0, keepdims=True) + jnp.exp(sink - m)
        w = (p * (1.0 / denom)).T.astype(BF16)
        oz = _dot(w, v_all)
        for c in range(N_HEADS // 2):
            parts = []
            for h in (2 * c, 2 * c + 1):
                kvh = h // group
                blk = oz[h * n_new:(h + 1) * n_new, (kvh // 2) * LANES:(kvh // 2 + 1) * LANES]
                if (h % 2) != (kvh % 2):
                    blk = pltpu.roll(blk, HEAD_DIM, 1)
                parts.append(blk)
            o_ref[pl.ds(r0, n_new), c * LANES:(c + 1) * LANES] = jnp.where(low, parts[0], parts[1])
        nk_ref[sb, 0:rows - n_new, :] = ck_ref[sb, n_new:rows, :]
        nk_ref[sb, rows - n_new:rows, :] = k_new
        nv_ref[sb, 0:rows - n_new, :] = cv_ref[sb, n_new:rows, :]
        nv_ref[sb, rows - n_new:rows, :] = v_new
        return carry

    lax.fori_loop(0, seqs, body, 0)


def _attn_sample_call(sink_row, q, k_new, v_new, cache_k, cache_v, n_new, seqs):
    b, rows, _ = cache_k.shape
    tok = pl.BlockSpec((seqs * n_new, Q_DIM), lambda i: (i, 0))
    tok_kv = pl.BlockSpec((seqs * n_new, KV_DIM), lambda i: (i, 0))
    cache = pl.BlockSpec((seqs, rows, KV_DIM), lambda i: (i, 0, 0))
    return pl.pallas_call(
        functools.partial(_attn_sample_kernel, n_new, seqs),
        grid=(b // seqs,),
        in_specs=[pl.BlockSpec((1, N_HEADS * n_new), lambda i: (0, 0)), tok, tok_kv, tok_kv, cache, cache],
        out_specs=[tok, cache, cache],
        out_shape=[
            jax.ShapeDtypeStruct((b * n_new, Q_DIM), F32),
            jax.ShapeDtypeStruct((b, rows, KV_DIM), F32),
            jax.ShapeDtypeStruct((b, rows, KV_DIM), F32),
        ],
        compiler_params=_cparams(("parallel",)),
        name="attn_sample",
    )(sink_row, q, k_new, v_new, cache_k, cache_v)


def _moe_cap(t_total):
    return t_total + MOE_TILE


def _route_scatter(final, cap, x1, gn_ref, wrh_ref, wrl_ref, br_ref, triu_ref, cntc_in, cntr_in,
                   dest_ref, cntc_out, cntr_out, xs_hbm, pay, zrows, dest_v, dest_s, cntc, cntr, sem):
    i = pl.program_id(0)
    n = pl.num_programs(0)
    slot = i % 2
    tm = x1.shape[0]

    @pl.when(i == 0)
    def _():
        cntc[...] = cntc_in[...]
        cntr[...] = cntr_in[...]

    xn = _rms(x1, gn_ref[...])
    xh = xn.astype(BF16)
    xl = (xn - xh.astype(F32)).astype(BF16)
    logits = _dot(xh, wrh_ref[...]) + _dot(xl, wrh_ref[...]) + _dot(xh, wrl_ref[...]) + br_ref[...]
    lane = lax.broadcasted_iota(I32, logits.shape, 1).astype(F32)
    big = jnp.float32(4 * LANES)
    neg = -jnp.inf
    gl = jnp.where((lane >= N_EXPERTS) & (lane < N_EXPERTS + N_EXPERT_GROUPS), logits, neg)
    gmax = jnp.max(gl, axis=-1, keepdims=True)
    g_val = 1.0 / jnp.sum(jnp.exp(gl - gmax), axis=-1, keepdims=True)
    g_idx = jnp.min(jnp.where(gl == gmax, lane, big), axis=-1, keepdims=True) - N_EXPERTS
    lo = g_idx * EXPERTS_PER_GROUP
    el = jnp.where((lane >= lo) & (lane < lo + EXPERTS_PER_GROUP), logits, neg)
    e1 = jnp.max(el, axis=-1, keepdims=True)
    i1 = jnp.min(jnp.where(el == e1, lane, big), axis=-1, keepdims=True)
    el2 = jnp.where(lane == i1, neg, el)
    e2 = jnp.max(el2, axis=-1, keepdims=True)
    i2 = jnp.min(jnp.where(el2 == e2, lane, big), axis=-1, keepdims=True)
    t = jnp.exp(e2 - e1)
    w1 = 1.0 / (1.0 + t)
    w2 = t / (1.0 + t)
    wts = g_val * (jnp.where(lane == i1 - lo, w1, 0.0) + jnp.where(lane == i2 - lo, w2, 0.0))

    def wait_slot(s):
        pltpu.make_async_copy(pay.at[s], xs_hbm.at[pl.ds(0, tm), :], sem.at[s]).wait()

    @pl.when(i >= 2)
    def _():
        wait_slot(slot)

    pay[slot, :, :PAY_X] = xn
    pay[slot, :, PAY_X:] = wts

    oh = (lane == g_idx).astype(F32)
    oht = oh.T
    rank = _dot(oht.astype(BF16), triu_ref[...])
    grp = lax.broadcasted_iota(I32, (LANES, 1), 0).astype(F32)
    base = grp * float(cap) + cntc[:, 0:1]
    dest = jnp.sum(oht * (rank + base), axis=0, keepdims=True).astype(I32)
    cntc[...] = cntc[...] + jnp.sum(oht, axis=1, keepdims=True)
    cntr[...] = cntr[...] + jnp.sum(oh, axis=0, keepdims=True)
    dest_ref[0] = dest
    dest_v[...] = dest
    pltpu.sync_copy(dest_v, dest_s)

    for r in range(tm):
        pltpu.make_async_copy(pay.at[slot, pl.ds(r, 1), :],
                              xs_hbm.at[pl.ds(dest_s[0, r], 1), :], sem.at[slot]).start()

    @pl.when(i == n - 1)
    def _():
        cntc_out[...] = cntc[...]
        cntr_out[...] = cntr[...]

        @pl.when(n >= 2)
        def _():
            wait_slot(1 - slot)

        wait_slot(slot)
        if final:
            zrows[...] = jnp.zeros_like(zrows)
            dest_v[:, 0:LANES] = cntr[0:1, :].astype(I32)
            pltpu.sync_copy(dest_v, dest_s)
            for g in range(N_EXPERT_GROUPS):
                c_g = dest_s[0, g]
                c_up = ((c_g + 7) // 8) * 8
                for k in range(7):
                    @pl.when(c_g + k < c_up)
                    def _():
                        one = pltpu.make_async_copy(zrows.at[pl.ds(0, 1), :],
                                                    xs_hbm.at[pl.ds(g * cap + c_g + k, 1), :], sem.at[1])
                        one.start()
                        one.wait()
                start = pl.multiple_of(g * cap + c_up, 8)
                pltpu.make_async_copy(zrows, xs_hbm.at[pl.ds(start, MOE_TILE), :], sem.at[0]).start()
            for g in range(N_EXPERT_GROUPS):
                pltpu.make_async_copy(zrows, xs_hbm.at[pl.ds(0, MOE_TILE), :], sem.at[0]).wait()


def _oproj_kernel(has_prev, final, cap, *refs):
    (o_ref, x_ref, wo_ref, bo_ref) = refs[:4]
    rest = refs[4:]
    if has_prev:
        rest = rest[:7] + rest[8:]
    x1 = x_ref[...] + _dot(o_ref[...].astype(BF16), wo_ref[...]) + bo_ref[...]
    x1_ref = rest[7]
    x1_ref[...] = x1
    _route_scatter(final, cap, x1, *rest[:7], *rest[8:])


def _gelu_tanh(x):
    return x * (0.5 * (1.0 + jnp.tanh(math.sqrt(2.0 / math.pi) * (x + 0.044715 * (x * x * x)))))


def _glu_kernel(has_prev, final, cap, *refs):
    (y_ref, x_ref, gm_ref, d_ref, wa_ref, wb_ref) = refs[:6]
    rest = refs[6:]
    if has_prev:
        rest = rest[:7] + rest[8:]
    x = x_ref[...]
    u = _rms(x, gm_ref[...])
    z = _gelu_tanh(y_ref[...] + d_ref[...] * u).astype(BF16)
    x1 = x + _dot(z, wa_ref[...]) * _sigmoid(_dot(z, wb_ref[...]))
    x1_ref = rest[7]
    x1_ref[...] = x1
    _route_scatter(final, cap, x1, *rest[:7], *rest[8:])


def _row_spec(tm, width):
    return pl.BlockSpec((tm, width), lambda i: (i, 0))


def _const_spec(shape):
    return pl.BlockSpec(shape, lambda i: (0,) * len(shape))


def _mixer_call(body, name, lead_args, lead_specs, x2d, router, triu, cnt, xs_prev, final, cap):
    t = x2d.shape[0]
    tm = TOKEN_TILE
    n_tiles = t // tm
    gn, wrh, wrl, br = router
    cntc, cntr = cnt
    has_prev = xs_prev is not None
    in_specs = lead_specs + [
        _const_spec((1, D_MODEL)), _const_spec((D_MODEL, LANES)), _const_spec((D_MODEL, LANES)),
        _const_spec((1, LANES)), _const_spec((tm, tm)), _const_spec((LANES, LANES)), _const_spec((8, LANES)),
    ]
    args = list(lead_args) + [gn, wrh, wrl, br, triu, cntc, cntr]
    aliases = {}
    if has_prev:
        in_specs.append(pl.BlockSpec(memory_space=pl.ANY))
        args.append(xs_prev)
        aliases = {len(args) - 1: 4}
    x1, dest, cntc2, cntr2, xs = pl.pallas_call(
        functools.partial(body, has_prev, final, cap),
        grid=(n_tiles,),
        in_specs=in_specs,
        out_specs=[
            _row_spec(tm, D_MODEL),
            pl.BlockSpec((1, 1, tm), lambda i: (i, 0, 0)),
            _const_spec((LANES, LANES)), _const_spec((8, LANES)),
            pl.BlockSpec(memory_space=pl.ANY),
        ],
        out_shape=[
            jax.ShapeDtypeStruct((t, D_MODEL), F32),
            jax.ShapeDtypeStruct((n_tiles, 1, tm), I32),
            jax.ShapeDtypeStruct((LANES, LANES), F32),
            jax.ShapeDtypeStruct((8, LANES), F32),
            jax.ShapeDtypeStruct((N_EXPERT_GROUPS * cap, PAY_W), F32),
        ],
        scratch_shapes=[
            pltpu.VMEM((2, tm, PAY_W), F32),
            pltpu.VMEM((MOE_TILE, PAY_W), F32),
            pltpu.VMEM((1, tm), I32),
            pltpu.SMEM((1, tm), I32),
            pltpu.VMEM((LANES, LANES), F32),
            pltpu.VMEM((8, LANES), F32),
            pltpu.SemaphoreType.DMA((2,)),
        ],
        input_output_aliases=aliases,
        compiler_params=_cparams(("arbitrary",)),
        name=name,
    )(*args)
    return x1, dest, (cntc2, cntr2), xs


def _oproj_call(o2d, x2d, wo_bf, bo, router, triu, cnt, xs_prev, final, cap):
    tm = TOKEN_TILE
    lead_specs = [_row_spec(tm, Q_DIM), _row_spec(tm, D_MODEL), _const_spec((Q_DIM, D_MODEL)),
                  _const_spec((1, D_MODEL))]
    return _mixer_call(_oproj_kernel, "oproj_route", [o2d, x2d, wo_bf, bo], lead_specs, x2d,
                       router, triu, cnt, xs_prev, final, cap)


def _glu_call(y2d, x2d, gm, d, wa_bf, wb_bf, router, triu, cnt, xs_prev, final, cap):
    tm = TOKEN_TILE
    lead_specs = [_row_spec(tm, D_MODEL), _row_spec(tm, D_MODEL), _const_spec((1, D_MODEL)),
                  _const_spec((1, D_MODEL)), _const_spec((D_MODEL, D_MODEL)),
                  _const_spec((D_MODEL, D_MODEL))]
    return _mixer_call(_glu_kernel, "glu_route", [y2d, x2d, gm, d, wa_bf, wb_bf], lead_specs, x2d,
                       router, triu, cnt, xs_prev, final, cap)


def _expert_kernel(blk_in_ref, blk_out_ref, grp_ref, valid_ref, xs_ref, wg_ref, wu_ref, wd_ref,
                   ys_ref, wgu, wdn, hid):
    j = pl.program_id(0)
    f = EXPERT_FF
    changed = (j == 0) | (grp_ref[j] != grp_ref[jnp.maximum(j - 1, 0)])

    @pl.when(changed)
    def _():
        for e in range(EXPERTS_PER_GROUP):
            wgu[e, :, :f] = wg_ref[e].astype(BF16)
            wgu[e, :, f:] = wu_ref[e].astype(BF16)
            wdn[e * f:(e + 1) * f, :] = wd_ref[e].astype(BF16)

    @pl.when(valid_ref[j] == 1)
    def _():
        x = xs_ref[:, :PAY_X].astype(BF16)
        wts = xs_ref[:, PAY_X:]
        for e in range(EXPERTS_PER_GROUP):
            gu = _dot(x, wgu[e])
            g, u = gu[:, :f], gu[:, f:]
            hid[:, e * f:(e + 1) * f] = ((g * _sigmoid(g)) * u * wts[:, e:e + 1]).astype(BF16)
        ys_ref[...] = _dot(hid[...], wdn[...])

    @pl.when(valid_ref[j] == 0)
    def _():
        ys_ref[...] = jnp.zeros_like(ys_ref)


def _expert_tile_map(cnt_row, cap, n_steps):
    per = cap // MOE_TILE
    c = cnt_row[0, :N_EXPERT_GROUPS].astype(I32)
    tiles = (c + MOE_TILE - 1) // MOE_TILE
    ends = jnp.cumsum(tiles)
    starts = ends - tiles
    total = ends[-1]
    j = jnp.arange(n_steps, dtype=I32)
    jj = jnp.minimum(j, total - 1)
    g = jnp.sum((jj[:, None] >= ends[None, :]).astype(I32), axis=1)
    blk_in = g * per + jj - starts[g]
    valid = (j < total).astype(I32)
    blk_out = jnp.where(valid == 1, blk_in, N_EXPERT_GROUPS * per)
    return blk_in, blk_out, g, valid


def _expert_call(cnt_row, xs, w_gate, w_up, w_down, cap, t_total):
    n_steps = t_total // MOE_TILE + N_EXPERT_GROUPS
    maps = _expert_tile_map(cnt_row, cap, n_steps)
    e, f = EXPERTS_PER_GROUP, EXPERT_FF
    grid_spec = pltpu.PrefetchScalarGridSpec(
        num_scalar_prefetch=4,
        grid=(n_steps,),
        in_specs=[
            pl.BlockSpec((MOE_TILE, PAY_W), lambda j, bi, bo, gr, va: (bi[j], 0)),
            pl.BlockSpec((e, D_MODEL, f), lambda j, bi, bo, gr, va: (gr[j], 0, 0)),
            pl.BlockSpec((e, D_MODEL, f), lambda j, bi, bo, gr, va: (gr[j], 0, 0)),
            pl.BlockSpec((e, f, D_MODEL), lambda j, bi, bo, gr, va: (gr[j], 0, 0)),
        ],
        out_specs=pl.BlockSpec((MOE_TILE, D_MODEL), lambda j, bi, bo, gr, va: (bo[j], 0)),
        scratch_shapes=[
            pltpu.VMEM((e, D_MODEL, 2 * f), BF16),
            pltpu.VMEM((e * f, D_MODEL), BF16),
            pltpu.VMEM((MOE_TILE, e * f), BF16),
        ],
    )
    return pl.pallas_call(
        _expert_kernel,
        grid_spec=grid_spec,
        out_shape=jax.ShapeDtypeStruct((N_EXPERT_GROUPS * cap + MOE_TILE, D_MODEL), F32),
        compiler_params=_cparams(("arbitrary",)),
        name="moe_experts",
    )(*maps, xs, w_gate, w_up, w_down)


def _combine_kernel(emit_x, dest_ref, x_ref, gnext_ref, ys_hbm, *rest):
    outs, (ybuf, sem) = rest[:-2], rest[-2:]
    i = pl.program_id(0)
    n_tiles = pl.num_programs(0) - 1
    slot = i % 2
    tm = x_ref.shape[0]

    @pl.when(i < n_tiles)
    def _():
        for r in range(tm):
            pltpu.make_async_copy(ys_hbm.at[pl.ds(dest_ref[0, 0, r], 1), :],
                                  ybuf.at[slot, pl.ds(r, 1), :], sem.at[slot]).start()

    @pl.when(i >= 1)
    def _():
        prev = 1 - slot
        pltpu.make_async_copy(ys_hbm.at[pl.ds(0, tm), :], ybuf.at[prev], sem.at[prev]).wait()
        x2 = x_ref[...] + ybuf[prev]
        normed = _rms(x2, gnext_ref[...])
        if emit_x:
            outs[0][...] = x2
            outs[1][...] = normed
        else:
            outs[0][...] = normed


def _combine_call(dest, x2d, ys, gnext, emit_x):
    t = x2d.shape[0]
    tm = TOKEN_TILE
    n_tiles = t // tm
    n_out = 2 if emit_x else 1
    done = lambda i: (jnp.maximum(i - 1, 0), 0)
    return pl.pallas_call(
        functools.partial(_combine_kernel, emit_x),
        grid=(n_tiles + 1,),
        in_specs=[
            pl.BlockSpec((1, 1, tm), lambda i: (jnp.minimum(i, n_tiles - 1), 0, 0),
                         memory_space=pltpu.SMEM),
            pl.BlockSpec((tm, D_MODEL), done), _const_spec((1, D_MODEL)),
            pl.BlockSpec(memory_space=pl.ANY),
        ],
        out_specs=[pl.BlockSpec((tm, D_MODEL), done)] * n_out,
        out_shape=[jax.ShapeDtypeStruct((t, D_MODEL), F32)] * n_out,
        scratch_shapes=[pltpu.VMEM((2, tm, D_MODEL), F32), pltpu.SemaphoreType.DMA((2,))],
        compiler_params=_cparams(("arbitrary",)),
        name="moe_combine",
    )(dest, x2d, gnext, ys)


def _s5_state_in(u_ref_val, wre_ref, wim_ref, sre_ref, sim_ref, pair_w):
    for m in range(S5_GB // 2):
        up = u_ref_val(m * pair_w, pair_w)
        sre_ref[:, m * LANES:(m + 1) * LANES] = _dot(up, wre_ref[m])
        sim_ref[:, m * LANES:(m + 1) * LANES] = _dot(up, wim_ref[m])


def _s5_outputs(u_ref_val, hre, him, m_ref, gre_ref, gim_ref, y_store, pair_w):
    gw = pair_w // 2
    for m in range(S5_GB // 2):
        hr = hre(m).astype(BF16)
        hi = him(m).astype(BF16)
        y = _dot(hr, gre_ref[m]) + _dot(hi, gim_ref[m])
        y0 = y[:, :gw] + _dot(u_ref_val(m * pair_w, gw), m_ref[2 * m])
        y1 = y[:, gw:] + _dot(u_ref_val(m * pair_w + gw, gw), m_ref[2 * m + 1])
        y_store(m * pair_w, gw, y0)
        y_store(m * pair_w + gw, gw, y1)


def _s5_prompt_kernel(u_ref, wre_ref, wim_ref, m_ref, gre_ref, gim_ref, aqr_ref, aqi_ref,
                      y_ref, her_ref, hei_ref, ut, uflat, sre, sim, hre, him, yflat, yt):
    gc = SSM_GROUP_CH
    qc = S5_CHUNK * gc
    pair_w = 2 * qc
    n_chunks = u_ref.shape[1] // S5_CHUNK
    for s in range(S5_CHUNK):
        xs = u_ref[0, pl.ds(s, n_chunks, stride=S5_CHUNK), :]
        ut[:, s * gc:(s + 1) * gc, :] = xs.T.reshape(S5_GB, gc, n_chunks)
    for g in range(S5_GB):
        uflat[:, g * qc:(g + 1) * qc] = ut[g].T.astype(BF16)
    u_val = lambda off, w: uflat[:, off:off + w]
    _s5_state_in(u_val, wre_ref, wim_ref, sre, sim, pair_w)
    ar = aqr_ref[...]
    ai = aqi_ref[...]

    def step(n, carry):
        hr, hi = carry
        hre[pl.ds(n, 1), :] = hr
        him[pl.ds(n, 1), :] = hi
        sr = sre[pl.ds(n, 1), :]
        si = sim[pl.ds(n, 1), :]
        return ar * hr - ai * hi + sr, ar * hi + ai * hr + si

    zero = jnp.zeros(ar.shape, F32)
    hr, hi = lax.fori_loop(0, n_chunks, step, (zero, zero))
    her_ref[0] = hr
    hei_ref[0] = hi

    def y_store(off, w, val):
        yflat[:, off:off + w] = val

    _s5_outputs(u_val, lambda m: hre[:, m * LANES:(m + 1) * LANES],
                lambda m: him[:, m * LANES:(m + 1) * LANES],
                m_ref, gre_ref, gim_ref, y_store, pair_w)
    for g in range(S5_GB):
        yt[g] = yflat[:, g * qc:(g + 1) * qc].T
    for t in range(S5_CHUNK):
        z = yt[:, t * gc:(t + 1) * gc, :].reshape(S5_GB * gc, n_chunks)
        y_ref[0, pl.ds(t, n_chunks, stride=S5_CHUNK), :] = z.T


def _s5_sample_kernel(n_new, u_ref, h0r_ref, h0i_ref, wre_ref, wim_ref, m_ref, gre_ref, gim_ref,
                      aqr_ref, aqi_ref, y_ref, hnr_ref, hni_ref, sre, sim):
    pair_w = 2 * n_new * SSM_GROUP_CH
    u_val = lambda off, w: u_ref[:, off:off + w]
    _s5_state_in(u_val, wre_ref, wim_ref, sre, sim, pair_w)
    ar = aqr_ref[...]
    ai = aqi_ref[...]
    h0r = h0r_ref[...]
    h0i = h0i_ref[...]
    hnr_ref[...] = ar * h0r - ai * h0i + sre[...]
    hni_ref[...] = ar * h0i + ai * h0r + sim[...]

    def y_store(off, w, val):
        y_ref[:, off:off + w] = val

    _s5_outputs(u_val, lambda m: h0r_ref[:, m * LANES:(m + 1) * LANES],
                lambda m: h0i_ref[:, m * LANES:(m + 1) * LANES],
                m_ref, gre_ref, gim_ref, y_store, pair_w)


def _s5_weight_specs(q, idx):
    qc = q * SSM_GROUP_CH
    np_ = S5_GB // 2
    st = S5_GB * SSM_STATE
    return [
        pl.BlockSpec((np_, 2 * qc, LANES), lambda *a: (idx(*a), 0, 0)),
        pl.BlockSpec((np_, 2 * qc, LANES), lambda *a: (idx(*a), 0, 0)),
        pl.BlockSpec((S5_GB, qc, qc), lambda *a: (idx(*a), 0, 0)),
        pl.BlockSpec((np_, LANES, 2 * qc), lambda *a: (idx(*a), 0, 0)),
        pl.BlockSpec((np_, LANES, 2 * qc), lambda *a: (idx(*a), 0, 0)),
        pl.BlockSpec((1, st), lambda *a: (0, idx(*a))),
        pl.BlockSpec((1, st), lambda *a: (0, idx(*a))),
    ]


def _s5_prompt_call(u, w):
    b, seq, _ = u.shape
    n_chunks = seq // S5_CHUNK
    gbl = S5_GB * SSM_GROUP_CH
    qc = S5_CHUNK * SSM_GROUP_CH
    st = S5_GB * SSM_STATE
    n_gb = SSM_GROUPS // S5_GB
    gb_of = lambda g, i: g
    tok = pl.BlockSpec((1, seq, gbl), lambda g, i: (i, 0, g))
    return pl.pallas_call(
        _s5_prompt_kernel,
        grid=(n_gb, b),
        in_specs=[tok] + _s5_weight_specs(S5_CHUNK, gb_of),
        out_specs=[
            tok,
            pl.BlockSpec((1, 1, st), lambda g, i: (i, 0, g)),
            pl.BlockSpec((1, 1, st), lambda g, i: (i, 0, g)),
        ],
        out_shape=[
            jax.ShapeDtypeStruct((b, seq, D_MODEL), F32),
            jax.ShapeDtypeStruct((b, 1, SSM_GROUPS * SSM_STATE), F32),
            jax.ShapeDtypeStruct((b, 1, SSM_GROUPS * SSM_STATE), F32),
        ],
        scratch_shapes=[
            pltpu.VMEM((S5_GB, qc, n_chunks), F32),
            pltpu.VMEM((n_chunks, S5_GB * qc), BF16),
            pltpu.VMEM((n_chunks, st), F32), pltpu.VMEM((n_chunks, st), F32),
            pltpu.VMEM((n_chunks, st), F32), pltpu.VMEM((n_chunks, st), F32),
            pltpu.VMEM((n_chunks, S5_GB * qc), F32),
            pltpu.VMEM((S5_GB, qc, n_chunks), F32),
        ],
        compiler_params=_cparams(("parallel", "parallel")),
        name="s5_prompt",
    )(u, *w)


def _s5_sample_call(uf, h0r, h0i, w, n_new):
    b, width = uf.shape
    gbw = S5_GB * n_new * SSM_GROUP_CH
    st = S5_GB * SSM_STATE
    n_gb = SSM_GROUPS // S5_GB
    gb_of = lambda g: g
    state = pl.BlockSpec((b, st), lambda g: (0, g))
    return pl.pallas_call(
        functools.partial(_s5_sample_kernel, n_new),
        grid=(n_gb,),
        in_specs=[pl.BlockSpec((b, gbw), lambda g: (0, g)), state, state] + _s5_weight_specs(n_new, gb_of),
        out_specs=[pl.BlockSpec((b, gbw), lambda g: (0, g)), state, state],
        out_shape=[
            jax.ShapeDtypeStruct((b, width), F32),
            jax.ShapeDtypeStruct((b, SSM_GROUPS * SSM_STATE), F32),
            jax.ShapeDtypeStruct((b, SSM_GROUPS * SSM_STATE), F32),
        ],
        scratch_shapes=[pltpu.VMEM((b, st), F32)] * 2,
        compiler_params=_cparams(("parallel",)),
        name="s5_sample",
    )(uf, h0r, h0i, *w)


def _block_diag_pairs(w):
    g, r, n = w.shape
    w2 = w.reshape(g // 2, 2, r, n)
    z = jnp.zeros((g // 2, r, n), w.dtype)
    top = jnp.concatenate([w2[:, 0], z], axis=2)
    bot = jnp.concatenate([z, w2[:, 1]], axis=2)
    return jnp.concatenate([top, bot], axis=1)


def _s5_discretize(a_re, a_im, log_dt, b_re, b_im):
    delta = jnp.exp(log_dt.astype(F32))[:, None]
    lr, li = a_re.astype(F32), a_im.astype(F32)
    mag = jnp.exp(delta * lr)
    abar_r = mag * jnp.cos(delta * li)
    abar_i = mag * jnp.sin(delta * li)
    nr, ni = abar_r - 1.0, abar_i
    den = lr * lr + li * li
    coef_r = ((nr * lr + ni * li) / den)[..., None]
    coef_i = ((ni * lr - nr * li) / den)[..., None]
    br, bi = b_re.astype(F32), b_im.astype(F32)
    return abar_r, abar_i, coef_r * br - coef_i * bi, coef_r * bi + coef_i * br


def _s5_chunk_weights(abar_r, abar_i, bbar_r, bbar_i, c_re, c_im, q):
    g, p = abar_r.shape
    c = SSM_GROUP_CH
    hi = lax.Precision.HIGHEST
    pows_r, pows_i = [jnp.ones_like(abar_r)], [jnp.zeros_like(abar_r)]
    for _ in range(q):
        pr, pi = pows_r[-1], pows_i[-1]
        pows_r.append(pr * abar_r - pi * abar_i)
        pows_i.append(pr * abar_i + pi * abar_r)
    pw_r, pw_i = jnp.stack(pows_r), jnp.stack(pows_i)
    rev_r, rev_i = pw_r[:q][::-1][..., None], pw_i[:q][::-1][..., None]
    wst_r = (rev_r * bbar_r[None] - rev_i * bbar_i[None]).transpose(1, 0, 3, 2).reshape(g, q * c, p)
    wst_i = (rev_r * bbar_i[None] + rev_i * bbar_r[None]).transpose(1, 0, 3, 2).reshape(g, q * c, p)
    x_r = pw_r[:q, :, :, None] * bbar_r[None] - pw_i[:q, :, :, None] * bbar_i[None]
    x_i = pw_r[:q, :, :, None] * bbar_i[None] + pw_i[:q, :, :, None] * bbar_r[None]
    cr, ci = c_re.astype(F32), c_im.astype(F32)
    ker = (jnp.einsum('gcp,tgpd->gtdc', cr, x_r, precision=hi)
           - jnp.einsum('gcp,tgpd->gtdc', ci, x_i, precision=hi))
    idx = jnp.arange(q)
    tau = idx[None, :] - idx[:, None]
    m5 = jnp.where((tau >= 0)[None, :, :, None, None], ker[:, jnp.clip(tau, 0, q - 1)], 0.0)
    m = m5.transpose(0, 1, 3, 2, 4).reshape(g, q * c, q * c)
    cr_t, ci_t = cr.transpose(0, 2, 1)[:, :, None, :], ci.transpose(0, 2, 1)[:, :, None, :]
    pr_t = pw_r[1:q + 1].transpose(1, 2, 0)[..., None]
    pi_t = pw_i[1:q + 1].transpose(1, 2, 0)[..., None]
    g_re = (cr_t * pr_t - ci_t * pi_t).reshape(g, p, q * c)
    g_im = (-(cr_t * pi_t + ci_t * pr_t)).reshape(g, p, q * c)
    return (_block_diag_pairs(wst_r).astype(BF16), _block_diag_pairs(wst_i).astype(BF16),
            m.astype(BF16), _block_diag_pairs(g_re).astype(BF16), _block_diag_pairs(g_im).astype(BF16),
            pw_r[q].reshape(1, g * p), pw_i[q].reshape(1, g * p))


def _rope_tables(pos):
    half = HEAD_DIM // 2
    inv = 1.0 / (ROPE_THETA ** (jnp.arange(half, dtype=F32) * (2.0 / HEAD_DIM)))
    ang = pos.astype(F32)[:, None] * inv[None, :]
    cos, sin = jnp.cos(ang), jnp.sin(ang)
    return jnp.tile(cos, (1, 4)), jnp.concatenate([-sin, sin, -sin, sin], axis=1)


def _router_weights(gain, w_rg, b_rg, w_re, b_re):
    pad = LANES - N_EXPERTS - N_EXPERT_GROUPS
    w = jnp.concatenate([w_re, w_rg, jnp.zeros((D_MODEL, pad), F32)], axis=1)
    b = jnp.concatenate([b_re, b_rg, jnp.zeros((pad,), F32)]).reshape(1, LANES)
    wh = w.astype(BF16)
    wl = (w - wh.astype(F32)).astype(BF16)
    return gain.reshape(1, -1).astype(F32), wh, wl, b


def kernel(x_prompt, x_sample, cache_k, cache_v, state_ssm_re, state_ssm_im, norm_mix, norm_ffn, norm_final, attn_w_qkv, attn_b_qkv, attn_w_o, attn_b_o, attn_sinks, ssm_a_re, ssm_a_im, ssm_log_dt, ssm_b_re, ssm_b_im, ssm_c_re, ssm_c_im, ssm_d, ssm_w_glu_a, ssm_w_glu_b, moe_w_router_group, moe_b_router_group, moe_w_router_expert, moe_b_router_expert, moe_w_gate, moe_w_up, moe_w_down):
    bsz, seq, _ = x_prompt.shape
    dbs, n_new, _ = x_sample.shape
    rows = cache_k.shape[2]
    tp, ts = bsz * seq, dbs * n_new
    cap = _moe_cap(tp + ts)
    xp = x_prompt.reshape(tp, D_MODEL)
    xs = x_sample.reshape(ts, D_MODEL)

    row1 = lambda v: v.reshape(1, -1).astype(F32)
    routers = [_router_weights(norm_ffn[l], moe_w_router_group[l], moe_b_router_group[l],
                               moe_w_router_expert[l], moe_b_router_expert[l]) for l in range(2)]
    triu = jnp.triu(jnp.ones((TOKEN_TILE, TOKEN_TILE), F32), 1).astype(BF16)
    cnt0 = (jnp.zeros((LANES, LANES), F32), jnp.zeros((8, LANES), F32))

    wqkv = attn_w_qkv[0].astype(BF16)
    bqkv = row1(attn_b_qkv[0])
    wo = attn_w_o[0].astype(BF16)
    bo = row1(attn_b_o[0])
    sinks = attn_sinks[0].astype(F32)
    g_mix0, g_mix1 = row1(norm_mix[0]), row1(norm_mix[1])
    cos_p, sin_p = _rope_tables(jnp.arange(seq, dtype=I32))
    pos_s = jnp.tile(PAST_LEN + jnp.arange(n_new, dtype=I32), dbs)
    cos_s, sin_s = _rope_tables(pos_s)

    qp, kp, vp = _qkv_call(xp, g_mix0, wqkv, bqkv, cos_p, sin_p, BF16)
    qs, ks, vs = _qkv_call(xs, g_mix0, wqkv, bqkv, cos_s, sin_s, F32)
    op = _attn_prompt_call(sinks, qp.reshape(bsz, seq, Q_DIM), kp.reshape(bsz, seq, KV_DIM),
                           vp.reshape(bsz, seq, KV_DIM))
    os_, nks, nvs = _attn_sample_call(jnp.repeat(sinks, n_new).reshape(1, -1), qs, ks, vs,
                                      cache_k[0].reshape(dbs, rows, KV_DIM),
                                      cache_v[0].reshape(dbs, rows, KV_DIM), n_new, 8)
    xp1, dest_p, cnt, rows_x = _oproj_call(op.reshape(tp, Q_DIM), xp, wo, bo, routers[0], triu,
                                           cnt0, None, False, cap)
    xs1, dest_s, cnt, rows_x = _oproj_call(os_, xs, wo, bo, routers[0], triu, cnt, rows_x, True, cap)
    rows_y = _expert_call(cnt[1], rows_x, moe_w_gate[0], moe_w_up[0], moe_w_down[0], cap, tp + ts)
    xp2, up = _combine_call(dest_p, xp1, rows_y, g_mix1, True)
    xs2, us = _combine_call(dest_s, xs1, rows_y, g_mix1, True)

    disc = _s5_discretize(ssm_a_re[0], ssm_a_im[0], ssm_log_dt[0], ssm_b_re[0], ssm_b_im[0])
    w_p = _s5_chunk_weights(*disc, ssm_c_re[0], ssm_c_im[0], S5_CHUNK)
    w_s = _s5_chunk_weights(*disc, ssm_c_re[0], ssm_c_im[0], n_new)
    gch = (SSM_GROUPS, SSM_GROUP_CH)
    uf_s = us.astype(BF16).reshape(dbs, n_new, *gch).transpose(0, 2, 1, 3).reshape(dbs, -1)
    y_p, hpr, hpi = _s5_prompt_call(up.reshape(bsz, seq, D_MODEL), w_p)
    y_p = y_p.reshape(tp, D_MODEL)
    h0r = state_ssm_re[0].reshape(dbs, -1).astype(F32)
    h0i = state_ssm_im[0].reshape(dbs, -1).astype(F32)
    yf_s, hsr, hsi = _s5_sample_call(uf_s, h0r, h0i, w_s, n_new)
    y_s = yf_s.reshape(dbs, SSM_GROUPS, n_new, SSM_GROUP_CH).transpose(0, 2, 1, 3).reshape(ts, D_MODEL)

    wa, wb = ssm_w_glu_a[0].astype(BF16), ssm_w_glu_b[0].astype(BF16)
    d_row = row1(ssm_d[0])
    xp3, dest_p, cnt, rows_x = _glu_call(y_p, xp2, g_mix1, d_row, wa, wb, routers[1], triu,
                                         cnt0, None, False, cap)
    xs3, dest_s, cnt, rows_x = _glu_call(y_s, xs2, g_mix1, d_row, wa, wb, routers[1], triu,
                                         cnt, rows_x, True, cap)
    rows_y = _expert_call(cnt[1], rows_x, moe_w_gate[1], moe_w_up[1], moe_w_down[1], cap, tp + ts)
    g_fin = row1(norm_final)
    (yp,) = _combine_call(dest_p, xp3, rows_y, g_fin, False)
    (ys,) = _combine_call(dest_s, xs3, rows_y, g_fin, False)

    kv5 = lambda a, n: a.reshape(1, n, rows, KV_HEADS, HEAD_DIM)
    st4 = lambda a, n: a.reshape(1, n, SSM_GROUPS, SSM_STATE)
    k_last = kp.reshape(bsz, seq, KV_DIM)[:, seq - WINDOW:]
    v_last = vp.reshape(bsz, seq, KV_DIM)[:, seq - WINDOW:]
    return (yp.reshape(bsz, seq, D_MODEL), ys.reshape(dbs, n_new, D_MODEL),
            k_last.reshape(1, bsz, WINDOW, KV_HEADS, HEAD_DIM), kv5(nks, dbs),
            v_last.reshape(1, bsz, WINDOW, KV_HEADS, HEAD_DIM), kv5(nvs, dbs),
            st4(hpr, bsz), st4(hsr, dbs), st4(hpi, bsz), st4(hsi, dbs))
```

```python
import functools
import math

import jax
import jax.numpy as jnp
from jax import lax
from jax.experimental import pallas as pl
from jax.experimental.pallas import tpu as pltpu

F32 = jnp.float32
BF16 = jnp.bfloat16
I32 = jnp.int32

D_MODEL = 1024
N_HEADS = 16
KV_HEADS = 4
HEAD_DIM = 64
Q_DIM = N_HEADS * HEAD_DIM
KV_DIM = KV_HEADS * HEAD_DIM
QKV_DIM = Q_DIM + 2 * KV_DIM
WINDOW = 128
BLOCK = 128
ROPE_THETA = 10000.0
PAST_LEN = 16384
SSM_GROUP_CH = 16
SSM_GROUPS = D_MODEL // SSM_GROUP_CH
SSM_STATE = 64
N_EXPERT_GROUPS = 4
EXPERTS_PER_GROUP = 8
N_EXPERTS = N_EXPERT_GROUPS * EXPERTS_PER_GROUP
EXPERT_FF = 128
NORM_EPS = 1e-5

LANES = 128
VMEM_LIMIT = 56 * 1024 * 1024
S5_CHUNK = 16
S5_GB = 8
TOKEN_TILE = 512
MOE_TILE = TOKEN_TILE
PAY_X = D_MODEL
PAY_W = PAY_X + LANES


def _cparams(sem):
    return pltpu.CompilerParams(dimension_semantics=sem, vmem_limit_bytes=VMEM_LIMIT)


def _rms(x, g):
    return x * lax.rsqrt(jnp.mean(x * x, axis=-1, keepdims=True) + NORM_EPS) * g


def _dot(a, b):
    return jnp.dot(a, b, preferred_element_type=F32)


def _dot_nt(a, b):
    return lax.dot_general(a, b, (((1,), (1,)), ((), ())), preferred_element_type=F32)


def _sigmoid(x):
    return 1.0 / (1.0 + jnp.exp(-x))


def _qkv_kernel(x_ref, g_ref, w_ref, b_ref, cos_ref, sin_ref, q_ref, k_ref, v_ref):
    xn = _rms(x_ref[...], g_ref[...])
    qkv = _dot(xn.astype(BF16), w_ref[...]) + b_ref[...]
    cos = cos_ref[...]
    sin = sin_ref[...]
    lane = lax.broadcasted_iota(I32, cos.shape, 1)
    first_half = (lane % HEAD_DIM) < (HEAD_DIM // 2)
    n_rot = (Q_DIM + KV_DIM) // LANES
    for c in range(n_rot):
        blk = qkv[:, c * LANES:(c + 1) * LANES]
        partner = jnp.where(first_half,
                            pltpu.roll(blk, LANES - HEAD_DIM // 2, 1),
                            pltpu.roll(blk, HEAD_DIM // 2, 1))
        rot = blk * cos + partner * sin
        if c < Q_DIM // LANES:
            q_ref[:, c * LANES:(c + 1) * LANES] = rot.astype(q_ref.dtype)
        else:
            k_ref[:, c * LANES - Q_DIM:(c + 1) * LANES - Q_DIM] = rot
    v_ref[...] = qkv[:, Q_DIM + KV_DIM:]


def _qkv_call(x2d, gain, w_bf, bias, cos_t, sin_t, q_dtype):
    t = x2d.shape[0]
    tm = TOKEN_TILE
    n_pos = cos_t.shape[0] // tm
    return pl.pallas_call(
        _qkv_kernel,
        grid=(t // tm,),
        in_specs=[
            pl.BlockSpec((tm, D_MODEL), lambda i: (i, 0)),
            pl.BlockSpec((1, D_MODEL), lambda i: (0, 0)),
            pl.BlockSpec((D_MODEL, QKV_DIM), lambda i: (0, 0)),
            pl.BlockSpec((1, QKV_DIM), lambda i: (0, 0)),
            pl.BlockSpec((tm, LANES), lambda i: (i % n_pos, 0)),
            pl.BlockSpec((tm, LANES), lambda i: (i % n_pos, 0)),
        ],
        out_specs=[
            pl.BlockSpec((tm, Q_DIM), lambda i: (i, 0)),
            pl.BlockSpec((tm, KV_DIM), lambda i: (i, 0)),
            pl.BlockSpec((tm, KV_DIM), lambda i: (i, 0)),
        ],
        out_shape=[
            jax.ShapeDtypeStruct((t, Q_DIM), q_dtype),
            jax.ShapeDtypeStruct((t, KV_DIM), F32),
            jax.ShapeDtypeStruct((t, KV_DIM), F32),
        ],
        compiler_params=_cparams(("parallel",)),
        name="qkv_rope",
    )(x2d, gain, w_bf, bias, cos_t, sin_t)


def _split_kv_pair(pair_f32, head_in_low_half):
    lane = lax.broadcasted_iota(I32, pair_f32.shape, 1)
    if head_in_low_half:
        lo = jnp.where(lane < HEAD_DIM, pair_f32, 0.0)
        hi = pltpu.roll(lo, HEAD_DIM, 1)
    else:
        hi = jnp.where(lane >= HEAD_DIM, pair_f32, 0.0)
        lo = pltpu.roll(hi, HEAD_DIM, 1)
    return lo.astype(BF16), hi.astype(BF16)


def _attend_group(q0, q1, k_pair, v_pair, head_in_low_half, mask, sinks4):
    rows = q0.shape[0]
    k_lo, k_hi = _split_kv_pair(k_pair, head_in_low_half)
    v_lo, v_hi = _split_kv_pair(v_pair, head_in_low_half)
    qcat = jnp.concatenate([q0, q1], axis=0).astype(BF16)
    row = lax.broadcasted_iota(I32, (2 * rows, 1), 0)
    top = row < rows
    acc = None
    for half, (kk, vv) in enumerate(((k_lo, v_lo), (k_hi, v_hi))):
        sink = jnp.where(top, sinks4[half], sinks4[2 + half])
        s = _dot_nt(qcat, kk) * (HEAD_DIM ** -0.5)
        s = jnp.where(mask, s, -jnp.inf)
        m = jnp.maximum(jnp.max(s, axis=-1, keepdims=True), sink)
        p = jnp.exp(s - m)
        denom = jnp.sum(p, axis=-1, keepdims=True) + jnp.exp(sink - m)
        o = _dot(p.astype(BF16), vv) * (1.0 / denom)
        acc = o if acc is None else acc + o
    return acc[:rows], acc[rows:]


def _attn_prompt_kernel(sink_ref, q_ref, kp_ref, kc_ref, vp_ref, vc_ref, o_ref):
    n = pl.program_id(1)
    k2 = jnp.concatenate([kp_ref[0], kc_ref[0]], axis=0)
    v2 = jnp.concatenate([vp_ref[0], vc_ref[0]], axis=0)
    row = lax.broadcasted_iota(I32, (2 * BLOCK, 2 * BLOCK), 0) % BLOCK
    col = lax.broadcasted_iota(I32, (2 * BLOCK, 2 * BLOCK), 1)
    mask = (col >= row) & (col <= row + WINDOW) & ((n > 0) | (col >= BLOCK))
    group = N_HEADS // KV_HEADS
    for j in range(KV_HEADS):
        pj = (j // 2) * LANES
        base = j * group * HEAD_DIM
        q0 = q_ref[0, :, base:base + LANES]
        q1 = q_ref[0, :, base + LANES:base + 2 * LANES]
        sinks4 = [sink_ref[j * group + 0], sink_ref[j * group + 1],
                  sink_ref[j * group + 2], sink_ref[j * group + 3]]
        o0, o1 = _attend_group(q0, q1, k2[:, pj:pj + LANES], v2[:, pj:pj + LANES],
                               j % 2 == 0, mask, sinks4)
        o_ref[0, :, base:base + LANES] = o0.astype(o_ref.dtype)
        o_ref[0, :, base + LANES:base + 2 * LANES] = o1.astype(o_ref.dtype)


def _attn_prompt_call(sinks, q, k, v):
    b, l, _ = q.shape
    nb = l // BLOCK
    kv_cur = pl.BlockSpec((1, BLOCK, KV_DIM), lambda i, n: (i, n, 0))
    kv_prev = pl.BlockSpec((1, BLOCK, KV_DIM), lambda i, n: (i, jnp.maximum(n - 1, 0), 0))
    return pl.pallas_call(
        _attn_prompt_kernel,
        grid=(b, nb),
        in_specs=[
            pl.BlockSpec(memory_space=pltpu.SMEM),
            pl.BlockSpec((1, BLOCK, Q_DIM), lambda i, n: (i, n, 0)),
            kv_prev, kv_cur, kv_prev, kv_cur,
        ],
        out_specs=pl.BlockSpec((1, BLOCK, Q_DIM), lambda i, n: (i, n, 0)),
        out_shape=jax.ShapeDtypeStruct((b, l, Q_DIM), BF16),
        compiler_params=_cparams(("parallel", "parallel")),
        name="attn_prompt",
    )(sinks, q, k, k, v, v)


def _attn_sample_kernel(n_new, seqs, sink_ref, q_ref, kn_ref, vn_ref, ck_ref, cv_ref,
                        o_ref, nk_ref, nv_ref):
    rows = ck_ref.shape[1]
    keys = 2 * rows
    n_cols = N_HEADS * n_new
    group = N_HEADS // KV_HEADS
    key = lax.broadcasted_iota(I32, (keys, n_cols), 0)
    qry = lax.broadcasted_iota(I32, (keys, n_cols), 1) % n_new
    mask = ((key < rows) & (key >= qry)) | ((key >= rows) & (key - rows <= qry))
    low = lax.broadcasted_iota(I32, (n_new, LANES), 1) < HEAD_DIM
    sink = sink_ref[...]
    pad = jnp.zeros((rows - n_new, KV_DIM), F32)
    zero_blk = jnp.zeros((n_new, LANES), F32)

    def body(sb, carry):
        r0 = pl.multiple_of(sb * n_new, n_new)
        k_new = kn_ref[pl.ds(r0, n_new), :]
        v_new = vn_ref[pl.ds(r0, n_new), :]
        k_all = jnp.concatenate([ck_ref[sb], k_new, pad], axis=0).astype(BF16)
        v_all = jnp.concatenate([cv_ref[sb], v_new, pad], axis=0).astype(BF16)
        blocks = []
        for h in range(N_HEADS):
            kvh = h // group
            blk = q_ref[pl.ds(r0, n_new), (h // 2) * LANES:(h // 2 + 1) * LANES]
            if (h % 2) != (kvh % 2):
                blk = pltpu.roll(blk, HEAD_DIM, 1)
            blk = jnp.where(low if kvh % 2 == 0 else ~low, blk, 0.0)
            blocks.append(jnp.concatenate([blk, zero_blk] if kvh < 2 else [zero_blk, blk], axis=1))
        qz = jnp.concatenate(blocks, axis=0).astype(BF16)
        s = _dot_nt(k_all, qz) * (HEAD_DIM ** -0.5)
        s = jnp.where(mask, s, -jnp.inf)
        m = jnp.maximum(jnp.max(s, axis=0, keepdims=True), sink)
        p = jnp.exp(s - m)
        denom = jnp.sum(p, axis=0, keepdims=True) + jnp.exp(sink - m)
        w = (p * (1.0 / denom)).T.astype(BF16)
        oz = _dot(w, v_all)
        for c in range(N_HEADS // 2):
            parts = []
            for h in (2 * c, 2 * c + 1):
                kvh = h // group
                blk = oz[h * n_new:(h + 1) * n_new, (kvh // 2) * LANES:(kvh // 2 + 1) * LANES]
                if (h % 2) != (kvh % 2):
                    blk = pltpu.roll(blk, HEAD_DIM, 1)
                parts.append(blk)
            o_ref[pl.ds(r0, n_new), c * LANES:(c + 1) * LANES] = jnp.where(low, parts[0], parts[1])
        nk_ref[sb, 0:rows - n_new, :] = ck_ref[sb, n_new:rows, :]
        nk_ref[sb, rows - n_new:rows, :] = k_new
        nv_ref[sb, 0:rows - n_new, :] = cv_ref[sb, n_new:rows, :]
        nv_ref[sb, rows - n_new:rows, :] = v_new
        return carry

    lax.fori_loop(0, seqs, body, 0)


def _attn_sample_call(sink_row, q, k_new, v_new, cache_k, cache_v, n_new, seqs):
    b, rows, _ = cache_k.shape
    tok = pl.BlockSpec((seqs * n_new, Q_DIM), lambda i: (i, 0))
    tok_kv = pl.BlockSpec((seqs * n_new, KV_DIM), lambda i: (i, 0))
    cache = pl.BlockSpec((seqs, rows, KV_DIM), lambda i: (i, 0, 0))
    return pl.pallas_call(
        functools.partial(_attn_sample_kernel, n_new, seqs),
        grid=(b // seqs,),
        in_specs=[pl.BlockSpec((1, N_HEADS * n_new), lambda i: (0, 0)), tok, tok_kv, tok_kv, cache, cache],
        out_specs=[tok, cache, cache],
        out_shape=[
            jax.ShapeDtypeStruct((b * n_new, Q_DIM), F32),
            jax.ShapeDtypeStruct((b, rows, KV_DIM), F32),
            jax.ShapeDtypeStruct((b, rows, KV_DIM), F32),
        ],
        compiler_params=_cparams(("parallel",)),
        name="attn_sample",
    )(sink_row, q, k_new, v_new, cache_k, cache_v)


def _moe_cap(t_total):
    return t_total + MOE_TILE


def _route_scatter(final, cap, x1, gn_ref, wrh_ref, wrl_ref, br_ref, triu_ref, cntc_in, cntr_in,
                   dest_ref, cntc_out, cntr_out, xs_hbm, pay, zrows, dest_v, dest_s, cntc, cntr, sem):
    i = pl.program_id(0)
    n = pl.num_programs(0)
    slot = i % 2
    tm = x1.shape[0]

    @pl.when(i == 0)
    def _():
        cntc[...] = cntc_in[...]
        cntr[...] = cntr_in[...]

    xn = _rms(x1, gn_ref[...])
    xh = xn.astype(BF16)
    xl = (xn - xh.astype(F32)).astype(BF16)
    logits = _dot(xh, wrh_ref[...]) + _dot(xl, wrh_ref[...]) + _dot(xh, wrl_ref[...]) + br_ref[...]
    lane = lax.broadcasted_iota(I32, logits.shape, 1).astype(F32)
    big = jnp.float32(4 * LANES)
    neg = -jnp.inf
    gl = jnp.where((lane >= N_EXPERTS) & (lane < N_EXPERTS + N_EXPERT_GROUPS), logits, neg)
    gmax = jnp.max(gl, axis=-1, keepdims=True)
    g_val = 1.0 / jnp.sum(jnp.exp(gl - gmax), axis=-1, keepdims=True)
    g_idx = jnp.min(jnp.where(gl == gmax, lane, big), axis=-1, keepdims=True) - N_EXPERTS
    lo = g_idx * EXPERTS_PER_GROUP
    el = jnp.where((lane >= lo) & (lane < lo + EXPERTS_PER_GROUP), logits, neg)
    e1 = jnp.max(el, axis=-1, keepdims=True)
    i1 = jnp.min(jnp.where(el == e1, lane, big), axis=-1, keepdims=True)
    el2 = jnp.where(lane == i1, neg, el)
    e2 = jnp.max(el2, axis=-1, keepdims=True)
    i2 = jnp.min(jnp.where(el2 == e2, lane, big), axis=-1, keepdims=True)
    t = jnp.exp(e2 - e1)
    w1 = 1.0 / (1.0 + t)
    w2 = t / (1.0 + t)
    wts = g_val * (jnp.where(lane == i1 - lo, w1, 0.0) + jnp.where(lane == i2 - lo, w2, 0.0))

    def wait_slot(s):
        pltpu.make_async_copy(pay.at[s], xs_hbm.at[pl.ds(0, tm), :], sem.at[s]).wait()

    @pl.when(i >= 2)
    def _():
        wait_slot(slot)

    pay[slot, :, :PAY_X] = xn
    pay[slot, :, PAY_X:] = wts

    oh = (lane == g_idx).astype(F32)
    oht = oh.T
    rank = _dot(oht.astype(BF16), triu_ref[...])
    grp = lax.broadcasted_iota(I32, (LANES, 1), 0).astype(F32)
    base = grp * float(cap) + cntc[:, 0:1]
    dest = jnp.sum(oht * (rank + base), axis=0, keepdims=True).astype(I32)
    cntc[...] = cntc[...] + jnp.sum(oht, axis=1, keepdims=True)
    cntr[...] = cntr[...] + jnp.sum(oh, axis=0, keepdims=True)
    dest_ref[0] = dest
    dest_v[...] = dest
    pltpu.sync_copy(dest_v, dest_s)

    for r in range(tm):
        pltpu.make_async_copy(pay.at[slot, pl.ds(r, 1), :],
                              xs_hbm.at[pl.ds(dest_s[0, r], 1), :], sem.at[slot]).start()

    @pl.when(i == n - 1)
    def _():
        cntc_out[...] = cntc[...]
        cntr_out[...] = cntr[...]

        @pl.when(n >= 2)
        def _():
            wait_slot(1 - slot)

        wait_slot(slot)
        if final:
            zrows[...] = jnp.zeros_like(zrows)
            dest_v[:, 0:LANES] = cntr[0:1, :].astype(I32)
            pltpu.sync_copy(dest_v, dest_s)
            for g in range(N_EXPERT_GROUPS):
                c_g = dest_s[0, g]
                c_up = ((c_g + 7) // 8) * 8
                for k in range(7):
                    @pl.when(c_g + k < c_up)
                    def _():
                        one = pltpu.make_async_copy(zrows.at[pl.ds(0, 1), :],
                                                    xs_hbm.at[pl.ds(g * cap + c_g + k, 1), :], sem.at[1])
                        one.start()
                        one.wait()
                start = pl.multiple_of(g * cap + c_up, 8)
                pltpu.make_async_copy(zrows, xs_hbm.at[pl.ds(start, MOE_TILE), :], sem.at[0]).start()
            for g in range(N_EXPERT_GROUPS):
                pltpu.make_async_copy(zrows, xs_hbm.at[pl.ds(0, MOE_TILE), :], sem.at[0]).wait()


def _oproj_kernel(has_prev, final, cap, *refs):
    (o_ref, x_ref, wo_ref, bo_ref) = refs[:4]
    rest = refs[4:]
    if has_prev:
        rest = rest[:7] + rest[8:]
    x1 = x_ref[...] + _dot(o_ref[...].astype(BF16), wo_ref[...]) + bo_ref[...]
    x1_ref = rest[7]
    x1_ref[...] = x1
    _route_scatter(final, cap, x1, *rest[:7], *rest[8:])


def _gelu_tanh(x):
    return x * (0.5 * (1.0 + jnp.tanh(math.sqrt(2.0 / math.pi) * (x + 0.044715 * (x * x * x)))))


def _glu_kernel(has_prev, final, cap, *refs):
    (y_ref, x_ref, gm_ref, d_ref, wa_ref, wb_ref) = refs[:6]
    rest = refs[6:]
    if has_prev:
        rest = rest[:7] + rest[8:]
    x = x_ref[...]
    u = _rms(x, gm_ref[...])
    z = _gelu_tanh(y_ref[...] + d_ref[...] * u).astype(BF16)
    x1 = x + _dot(z, wa_ref[...]) * _sigmoid(_dot(z, wb_ref[...]))
    x1_ref = rest[7]
    x1_ref[...] = x1
    _route_scatter(final, cap, x1, *rest[:7], *rest[8:])


def _row_spec(tm, width):
    return pl.BlockSpec((tm, width), lambda i: (i, 0))


def _const_spec(shape):
    return pl.BlockSpec(shape, lambda i: (0,) * len(shape))


def _mixer_call(body, name, lead_args, lead_specs, x2d, router, triu, cnt, xs_prev, final, cap):
    t = x2d.shape[0]
    tm = TOKEN_TILE
    n_tiles = t // tm
    gn, wrh, wrl, br = router
    cntc, cntr = cnt
    has_prev = xs_prev is not None
    in_specs = lead_specs + [
        _const_spec((1, D_MODEL)), _const_spec((D_MODEL, LANES)), _const_spec((D_MODEL, LANES)),
        _const_spec((1, LANES)), _const_spec((tm, tm)), _const_spec((LANES, LANES)), _const_spec((8, LANES)),
    ]
    args = list(lead_args) + [gn, wrh, wrl, br, triu, cntc, cntr]
    aliases = {}
    if has_prev:
        in_specs.append(pl.BlockSpec(memory_space=pl.ANY))
        args.append(xs_prev)
        aliases = {len(args) - 1: 4}
    x1, dest, cntc2, cntr2, xs = pl.pallas_call(
        functools.partial(body, has_prev, final, cap),
        grid=(n_tiles,),
        in_specs=in_specs,
        out_specs=[
            _row_spec(tm, D_MODEL),
            pl.BlockSpec((1, 1, tm), lambda i: (i, 0, 0)),
            _const_spec((LANES, LANES)), _const_spec((8, LANES)),
            pl.BlockSpec(memory_space=pl.ANY),
        ],
        out_shape=[
            jax.ShapeDtypeStruct((t, D_MODEL), F32),
            jax.ShapeDtypeStruct((n_tiles, 1, tm), I32),
            jax.ShapeDtypeStruct((LANES, LANES), F32),
            jax.ShapeDtypeStruct((8, LANES), F32),
            jax.ShapeDtypeStruct((N_EXPERT_GROUPS * cap, PAY_W), F32),
        ],
        scratch_shapes=[
            pltpu.VMEM((2, tm, PAY_W), F32),
            pltpu.VMEM((MOE_TILE, PAY_W), F32),
            pltpu.VMEM((1, tm), I32),
            pltpu.SMEM((1, tm), I32),
            pltpu.VMEM((LANES, LANES), F32),
            pltpu.VMEM((8, LANES), F32),
            pltpu.SemaphoreType.DMA((2,)),
        ],
        input_output_aliases=aliases,
        compiler_params=_cparams(("arbitrary",)),
        name=name,
    )(*args)
    return x1, dest, (cntc2, cntr2), xs


def _oproj_call(o2d, x2d, wo_bf, bo, router, triu, cnt, xs_prev, final, cap):
    tm = TOKEN_TILE
    lead_specs = [_row_spec(tm, Q_DIM), _row_spec(tm, D_MODEL), _const_spec((Q_DIM, D_MODEL)),
                  _const_spec((1, D_MODEL))]
    return _mixer_call(_oproj_kernel, "oproj_route", [o2d, x2d, wo_bf, bo], lead_specs, x2d,
                       router, triu, cnt, xs_prev, final, cap)


def _glu_call(y2d, x2d, gm, d, wa_bf, wb_bf, router, triu, cnt, xs_prev, final, cap):
    tm = TOKEN_TILE
    lead_specs = [_row_spec(tm, D_MODEL), _row_spec(tm, D_MODEL), _const_spec((1, D_MODEL)),
                  _const_spec((1, D_MODEL)), _const_spec((D_MODEL, D_MODEL)),
                  _const_spec((D_MODEL, D_MODEL))]
    return _mixer_call(_glu_kernel, "glu_route", [y2d, x2d, gm, d, wa_bf, wb_bf], lead_specs, x2d,
                       router, triu, cnt, xs_prev, final, cap)


def _expert_kernel(blk_in_ref, blk_out_ref, grp_ref, valid_ref, xs_ref, wg_ref, wu_ref, wd_ref,
                   ys_ref, wgu, wdn, hid):
    j = pl.program_id(0)
    f = EXPERT_FF
    changed = (j == 0) | (grp_ref[j] != grp_ref[jnp.maximum(j - 1, 0)])

    @pl.when(changed)
    def _():
        for e in range(EXPERTS_PER_GROUP):
            wgu[e, :, :f] = wg_ref[e].astype(BF16)
            wgu[e, :, f:] = wu_ref[e].astype(BF16)
            wdn[e * f:(e + 1) * f, :] = wd_ref[e].astype(BF16)

    @pl.when(valid_ref[j] == 1)
    def _():
        x = xs_ref[:, :PAY_X].astype(BF16)
        wts = xs_ref[:, PAY_X:]
        for e in range(EXPERTS_PER_GROUP):
            gu = _dot(x, wgu[e])
            g, u = gu[:, :f], gu[:, f:]
            hid[:, e * f:(e + 1) * f] = ((g * _sigmoid(g)) * u * wts[:, e:e + 1]).astype(BF16)
        ys_ref[...] = _dot(hid[...], wdn[...])

    @pl.when(valid_ref[j] == 0)
    def _():
        ys_ref[...] = jnp.zeros_like(ys_ref)


def _expert_tile_map(cnt_row, cap, n_steps):
    per = cap // MOE_TILE
    c = cnt_row[0, :N_EXPERT_GROUPS].astype(I32)
    tiles = (c + MOE_TILE - 1) // MOE_TILE
    ends = jnp.cumsum(tiles)
    starts = ends - tiles
    total = ends[-1]
    j = jnp.arange(n_steps, dtype=I32)
    jj = jnp.minimum(j, total - 1)
    g = jnp.sum((jj[:, None] >= ends[None, :]).astype(I32), axis=1)
    blk_in = g * per + jj - starts[g]
    valid = (j < total).astype(I32)
    blk_out = jnp.where(valid == 1, blk_in, N_EXPERT_GROUPS * per)
    return blk_in, blk_out, g, valid


def _expert_call(cnt_row, xs, w_gate, w_up, w_down, layer, cap, t_total):
    n_steps = t_total // MOE_TILE + N_EXPERT_GROUPS
    blk_in, blk_out, grp, valid = _expert_tile_map(cnt_row, cap, n_steps)
    maps = (blk_in, blk_out, grp + layer * N_EXPERT_GROUPS, valid)
    e, f = EXPERTS_PER_GROUP, EXPERT_FF
    w_gate = w_gate.reshape(-1, D_MODEL, f)
    w_up = w_up.reshape(-1, D_MODEL, f)
    w_down = w_down.reshape(-1, f, D_MODEL)
    grid_spec = pltpu.PrefetchScalarGridSpec(
        num_scalar_prefetch=4,
        grid=(n_steps,),
        in_specs=[
            pl.BlockSpec((MOE_TILE, PAY_W), lambda j, bi, bo, gr, va: (bi[j], 0)),
            pl.BlockSpec((e, D_MODEL, f), lambda j, bi, bo, gr, va: (gr[j], 0, 0)),
            pl.BlockSpec((e, D_MODEL, f), lambda j, bi, bo, gr, va: (gr[j], 0, 0)),
            pl.BlockSpec((e, f, D_MODEL), lambda j, bi, bo, gr, va: (gr[j], 0, 0)),
        ],
        out_specs=pl.BlockSpec((MOE_TILE, D_MODEL), lambda j, bi, bo, gr, va: (bo[j], 0)),
        scratch_shapes=[
            pltpu.VMEM((e, D_MODEL, 2 * f), BF16),
            pltpu.VMEM((e * f, D_MODEL), BF16),
            pltpu.VMEM((MOE_TILE, e * f), BF16),
        ],
    )
    return pl.pallas_call(
        _expert_kernel,
        grid_spec=grid_spec,
        out_shape=jax.ShapeDtypeStruct((N_EXPERT_GROUPS * cap + MOE_TILE, D_MODEL), F32),
        compiler_params=_cparams(("arbitrary",)),
        name="moe_experts",
    )(*maps, xs, w_gate, w_up, w_down)


def _combine_kernel(emit_x, dest_ref, x_ref, gnext_ref, ys_hbm, *rest):
    outs, (ybuf, sem) = rest[:-2], rest[-2:]
    i = pl.program_id(0)
    n_tiles = pl.num_programs(0) - 1
    slot = i % 2
    tm = x_ref.shape[0]

    @pl.when(i < n_tiles)
    def _():
        for r in range(tm):
            pltpu.make_async_copy(ys_hbm.at[pl.ds(dest_ref[0, 0, r], 1), :],
                                  ybuf.at[slot, pl.ds(r, 1), :], sem.at[slot]).start()

    @pl.when(i >= 1)
    def _():
        prev = 1 - slot
        pltpu.make_async_copy(ys_hbm.at[pl.ds(0, tm), :], ybuf.at[prev], sem.at[prev]).wait()
        x2 = x_ref[...] + ybuf[prev]
        normed = _rms(x2, gnext_ref[...])
        if emit_x:
            outs[0][...] = x2
            outs[1][...] = normed
        else:
            outs[0][...] = normed


def _combine_call(dest, x2d, ys, gnext, emit_x):
    t = x2d.shape[0]
    tm = TOKEN_TILE
    n_tiles = t // tm
    n_out = 2 if emit_x else 1
    done = lambda i: (jnp.maximum(i - 1, 0), 0)
    return pl.pallas_call(
        functools.partial(_combine_kernel, emit_x),
        grid=(n_tiles + 1,),
        in_specs=[
            pl.BlockSpec((1, 1, tm), lambda i: (jnp.minimum(i, n_tiles - 1), 0, 0),
                         memory_space=pltpu.SMEM),
            pl.BlockSpec((tm, D_MODEL), done), _const_spec((1, D_MODEL)),
            pl.BlockSpec(memory_space=pl.ANY),
        ],
        out_specs=[pl.BlockSpec((tm, D_MODEL), done)] * n_out,
        out_shape=[jax.ShapeDtypeStruct((t, D_MODEL), F32)] * n_out,
        scratch_shapes=[pltpu.VMEM((2, tm, D_MODEL), F32), pltpu.SemaphoreType.DMA((2,))],
        compiler_params=_cparams(("arbitrary",)),
        name="moe_combine",
    )(dest, x2d, gnext, ys)


def _s5_state_in(u_ref_val, wre_ref, wim_ref, sre_ref, sim_ref, pair_w):
    for m in range(S5_GB // 2):
        up = u_ref_val(m * pair_w, pair_w)
        sre_ref[:, m * LANES:(m + 1) * LANES] = _dot(up, wre_ref[m])
        sim_ref[:, m * LANES:(m + 1) * LANES] = _dot(up, wim_ref[m])


def _s5_outputs(u_ref_val, hre, him, m_ref, gre_ref, gim_ref, y_store, pair_w):
    gw = pair_w // 2
    for m in range(S5_GB // 2):
        hr = hre(m).astype(BF16)
        hi = him(m).astype(BF16)
        y = _dot(hr, gre_ref[m]) + _dot(hi, gim_ref[m])
        y0 = y[:, :gw] + _dot(u_ref_val(m * pair_w, gw), m_ref[2 * m])
        y1 = y[:, gw:] + _dot(u_ref_val(m * pair_w + gw, gw), m_ref[2 * m + 1])
        y_store(m * pair_w, gw, y0)
        y_store(m * pair_w + gw, gw, y1)


def _s5_flatten(load_rows, q, n, ut, uflat):
    gc = SSM_GROUP_CH
    qc = q * gc
    for s in range(q):
        ut[:, s * gc:(s + 1) * gc, :] = load_rows(s).T.reshape(S5_GB, gc, n)
    for g in range(S5_GB):
        uflat[:, g * qc:(g + 1) * qc] = ut[g].T.astype(BF16)


def _s5_unflatten(yflat, yt, store_rows, q, n):
    gc = SSM_GROUP_CH
    qc = q * gc
    for g in range(S5_GB):
        yt[g] = yflat[:, g * qc:(g + 1) * qc].T
    for t in range(q):
        store_rows(t, yt[:, t * gc:(t + 1) * gc, :].reshape(S5_GB * gc, n).T)


def _s5_prompt_kernel(u_ref, wre_ref, wim_ref, m_ref, gre_ref, gim_ref, aqr_ref, aqi_ref,
                      y_ref, her_ref, hei_ref, ut, uflat, sre, sim, hre, him, yflat, yt):
    pair_w = 2 * S5_CHUNK * SSM_GROUP_CH
    n_chunks = u_ref.shape[1] // S5_CHUNK
    _s5_flatten(lambda s: u_ref[0, pl.ds(s, n_chunks, stride=S5_CHUNK), :],
                S5_CHUNK, n_chunks, ut, uflat)
    u_val = lambda off, w: uflat[:, off:off + w]
    _s5_state_in(u_val, wre_ref, wim_ref, sre, sim, pair_w)
    ar = aqr_ref[...]
    ai = aqi_ref[...]

    def step(n, carry):
        hr, hi = carry
        hre[pl.ds(n, 1), :] = hr
        him[pl.ds(n, 1), :] = hi
        sr = sre[pl.ds(n, 1), :]
        si = sim[pl.ds(n, 1), :]
        return ar * hr - ai * hi + sr, ar * hi + ai * hr + si

    zero = jnp.zeros(ar.shape, F32)
    hr, hi = lax.fori_loop(0, n_chunks, step, (zero, zero))
    her_ref[0] = hr
    hei_ref[0] = hi

    def y_store(off, w, val):
        yflat[:, off:off + w] = val

    _s5_outputs(u_val, lambda m: hre[:, m * LANES:(m + 1) * LANES],
                lambda m: him[:, m * LANES:(m + 1) * LANES],
                m_ref, gre_ref, gim_ref, y_store, pair_w)

    def store_rows(t, val):
        y_ref[0, pl.ds(t, n_chunks, stride=S5_CHUNK), :] = val

    _s5_unflatten(yflat, yt, store_rows, S5_CHUNK, n_chunks)


def _s5_sample_kernel(n_new, u_ref, h0r_ref, h0i_ref, wre_ref, wim_ref, m_ref, gre_ref, gim_ref,
                      aqr_ref, aqi_ref, y_ref, hnr_ref, hni_ref, ut, uflat, sre, sim, yflat, yt):
    pair_w = 2 * n_new * SSM_GROUP_CH
    seqs = h0r_ref.shape[0]
    _s5_flatten(lambda s: u_ref[pl.ds(s, seqs, stride=n_new), :], n_new, seqs, ut, uflat)
    u_val = lambda off, w: uflat[:, off:off + w]
    _s5_state_in(u_val, wre_ref, wim_ref, sre, sim, pair_w)
    ar = aqr_ref[...]
    ai = aqi_ref[...]
    h0r = h0r_ref[...]
    h0i = h0i_ref[...]
    hnr_ref[...] = ar * h0r - ai * h0i + sre[...]
    hni_ref[...] = ar * h0i + ai * h0r + sim[...]

    def y_store(off, w, val):
        yflat[:, off:off + w] = val

    _s5_outputs(u_val, lambda m: h0r_ref[:, m * LANES:(m + 1) * LANES],
                lambda m: h0i_ref[:, m * LANES:(m + 1) * LANES],
                m_ref, gre_ref, gim_ref, y_store, pair_w)

    def store_rows(t, val):
        y_ref[pl.ds(t, seqs, stride=n_new), :] = val

    _s5_unflatten(yflat, yt, store_rows, n_new, seqs)


def _s5_weight_specs(q, idx):
    qc = q * SSM_GROUP_CH
    np_ = S5_GB // 2
    st = S5_GB * SSM_STATE
    return [
        pl.BlockSpec((np_, 2 * qc, LANES), lambda *a: (idx(*a), 0, 0)),
        pl.BlockSpec((np_, 2 * qc, LANES), lambda *a: (idx(*a), 0, 0)),
        pl.BlockSpec((S5_GB, qc, qc), lambda *a: (idx(*a), 0, 0)),
        pl.BlockSpec((np_, LANES, 2 * qc), lambda *a: (idx(*a), 0, 0)),
        pl.BlockSpec((np_, LANES, 2 * qc), lambda *a: (idx(*a), 0, 0)),
        pl.BlockSpec((1, st), lambda *a: (0, idx(*a))),
        pl.BlockSpec((1, st), lambda *a: (0, idx(*a))),
    ]


def _s5_prompt_call(u, w):
    b, seq, _ = u.shape
    n_chunks = seq // S5_CHUNK
    gbl = S5_GB * SSM_GROUP_CH
    qc = S5_CHUNK * SSM_GROUP_CH
    st = S5_GB * SSM_STATE
    n_gb = SSM_GROUPS // S5_GB
    gb_of = lambda g, i: g
    tok = pl.BlockSpec((1, seq, gbl), lambda g, i: (i, 0, g))
    return pl.pallas_call(
        _s5_prompt_kernel,
        grid=(n_gb, b),
        in_specs=[tok] + _s5_weight_specs(S5_CHUNK, gb_of),
        out_specs=[
            tok,
            pl.BlockSpec((1, 1, st), lambda g, i: (i, 0, g)),
            pl.BlockSpec((1, 1, st), lambda g, i: (i, 0, g)),
        ],
        out_shape=[
            jax.ShapeDtypeStruct((b, seq, D_MODEL), F32),
            jax.ShapeDtypeStruct((b, 1, SSM_GROUPS * SSM_STATE), F32),
            jax.ShapeDtypeStruct((b, 1, SSM_GROUPS * SSM_STATE), F32),
        ],
        scratch_shapes=[
            pltpu.VMEM((S5_GB, qc, n_chunks), F32),
            pltpu.VMEM((n_chunks, S5_GB * qc), BF16),
            pltpu.VMEM((n_chunks, st), F32), pltpu.VMEM((n_chunks, st), F32),
            pltpu.VMEM((n_chunks, st), F32), pltpu.VMEM((n_chunks, st), F32),
            pltpu.VMEM((n_chunks, S5_GB * qc), F32),
            pltpu.VMEM((S5_GB, qc, n_chunks), F32),
        ],
        compiler_params=_cparams(("parallel", "parallel")),
        name="s5_prompt",
    )(u, *w)


def _s5_sample_call(u2d, h0r, h0i, w, n_new):
    t = u2d.shape[0]
    b = t // n_new
    gbl = S5_GB * SSM_GROUP_CH
    qc = n_new * SSM_GROUP_CH
    st = S5_GB * SSM_STATE
    n_gb = SSM_GROUPS // S5_GB
    gb_of = lambda g: g
    state = pl.BlockSpec((b, st), lambda g: (0, g))
    tok = pl.BlockSpec((t, gbl), lambda g: (0, g))
    return pl.pallas_call(
        functools.partial(_s5_sample_kernel, n_new),
        grid=(n_gb,),
        in_specs=[tok, state, state] + _s5_weight_specs(n_new, gb_of),
        out_specs=[tok, state, state],
        out_shape=[
            jax.ShapeDtypeStruct((t, D_MODEL), F32),
            jax.ShapeDtypeStruct((b, SSM_GROUPS * SSM_STATE), F32),
            jax.ShapeDtypeStruct((b, SSM_GROUPS * SSM_STATE), F32),
        ],
        scratch_shapes=[
            pltpu.VMEM((S5_GB, qc, b), F32),
            pltpu.VMEM((b, S5_GB * qc), BF16),
            pltpu.VMEM((b, st), F32), pltpu.VMEM((b, st), F32),
            pltpu.VMEM((b, S5_GB * qc), F32),
            pltpu.VMEM((S5_GB, qc, b), F32),
        ],
        compiler_params=_cparams(("parallel",)),
        name="s5_sample",
    )(u2d, h0r, h0i, *w)


def _block_diag_pairs(w):
    g, r, n = w.shape
    w2 = w.reshape(g // 2, 2, r, n)
    z = jnp.zeros((g // 2, r, n), w.dtype)
    top = jnp.concatenate([w2[:, 0], z], axis=2)
    bot = jnp.concatenate([z, w2[:, 1]], axis=2)
    return jnp.concatenate([top, bot], axis=1)


def _s5_discretize(a_re, a_im, log_dt, b_re, b_im):
    delta = jnp.exp(log_dt.astype(F32))[:, None]
    lr, li = a_re.astype(F32), a_im.astype(F32)
    mag = jnp.exp(delta * lr)
    abar_r = mag * jnp.cos(delta * li)
    abar_i = mag * jnp.sin(delta * li)
    nr, ni = abar_r - 1.0, abar_i
    den = lr * lr + li * li
    coef_r = ((nr * lr + ni * li) / den)[..., None]
    coef_i = ((ni * lr - nr * li) / den)[..., None]
    br, bi = b_re.astype(F32), b_im.astype(F32)
    return delta * lr, delta * li, coef_r * br - coef_i * bi, coef_r * bi + coef_i * br


def _s5_chunk_weights(log_mag, phase, bbar_r, bbar_i, c_re, c_im, q):
    g, p = log_mag.shape
    c = SSM_GROUP_CH
    hi = lax.Precision.HIGHEST
    k = jnp.arange(q + 1, dtype=F32)[:, None, None]
    mag = jnp.exp(k * log_mag[None])
    pw_r, pw_i = mag * jnp.cos(k * phase[None]), mag * jnp.sin(k * phase[None])
    rev_r, rev_i = pw_r[:q][::-1][..., None], pw_i[:q][::-1][..., None]
    wst_r = (rev_r * bbar_r[None] - rev_i * bbar_i[None]).transpose(1, 0, 3, 2).reshape(g, q * c, p)
    wst_i = (rev_r * bbar_i[None] + rev_i * bbar_r[None]).transpose(1, 0, 3, 2).reshape(g, q * c, p)
    x_r = pw_r[:q, :, :, None] * bbar_r[None] - pw_i[:q, :, :, None] * bbar_i[None]
    x_i = pw_r[:q, :, :, None] * bbar_i[None] + pw_i[:q, :, :, None] * bbar_r[None]
    cr, ci = c_re.astype(F32), c_im.astype(F32)
    ker = (jnp.einsum('gcp,tgpd->gtdc', cr, x_r, precision=hi)
           - jnp.einsum('gcp,tgpd->gtdc', ci, x_i, precision=hi))
    idx = jnp.arange(q)
    place = (idx[None, :, None] - idx[:, None, None] == idx[None, None, :]).astype(BF16)
    m = jnp.einsum('stk,gkdc->gsdtc', place, ker.astype(BF16),
                   preferred_element_type=F32).reshape(g, q * c, q * c)
    cr_t, ci_t = cr.transpose(0, 2, 1)[:, :, None, :], ci.transpose(0, 2, 1)[:, :, None, :]
    pr_t = pw_r[1:q + 1].transpose(1, 2, 0)[..., None]
    pi_t = pw_i[1:q + 1].transpose(1, 2, 0)[..., None]
    g_re = (cr_t * pr_t - ci_t * pi_t).reshape(g, p, q * c)
    g_im = (-(cr_t * pi_t + ci_t * pr_t)).reshape(g, p, q * c)
    return (_block_diag_pairs(wst_r).astype(BF16), _block_diag_pairs(wst_i).astype(BF16),
            m.astype(BF16), _block_diag_pairs(g_re).astype(BF16), _block_diag_pairs(g_im).astype(BF16),
            pw_r[q].reshape(1, g * p), pw_i[q].reshape(1, g * p))


def _rope_tables(pos):
    half = HEAD_DIM // 2
    inv = 1.0 / (ROPE_THETA ** (jnp.arange(half, dtype=F32) * (2.0 / HEAD_DIM)))
    ang = pos.astype(F32)[:, None] * inv[None, :]
    cos, sin = jnp.cos(ang), jnp.sin(ang)
    return jnp.tile(cos, (1, 4)), jnp.concatenate([-sin, sin, -sin, sin], axis=1)


def _router_weights(gain, w_rg, b_rg, w_re, b_re):
    pad = LANES - N_EXPERTS - N_EXPERT_GROUPS
    w = jnp.concatenate([w_re, w_rg, jnp.zeros((D_MODEL, pad), F32)], axis=1)
    b = jnp.concatenate([b_re, b_rg, jnp.zeros((pad,), F32)]).reshape(1, LANES)
    wh = w.astype(BF16)
    wl = (w - wh.astype(F32)).astype(BF16)
    return gain.reshape(1, -1).astype(F32), wh, wl, b


def kernel(x_prompt, x_sample, cache_k, cache_v, state_ssm_re, state_ssm_im, norm_mix, norm_ffn, norm_final, attn_w_qkv, attn_b_qkv, attn_w_o, attn_b_o, attn_sinks, ssm_a_re, ssm_a_im, ssm_log_dt, ssm_b_re, ssm_b_im, ssm_c_re, ssm_c_im, ssm_d, ssm_w_glu_a, ssm_w_glu_b, moe_w_router_group, moe_b_router_group, moe_w_router_expert, moe_b_router_expert, moe_w_gate, moe_w_up, moe_w_down):
    bsz, seq, _ = x_prompt.shape
    dbs, n_new, _ = x_sample.shape
    rows = cache_k.shape[2]
    tp, ts = bsz * seq, dbs * n_new
    cap = _moe_cap(tp + ts)
    xp = x_prompt.reshape(tp, D_MODEL)
    xs = x_sample.reshape(ts, D_MODEL)

    row1 = lambda v: v.reshape(1, -1).astype(F32)
    routers = [_router_weights(norm_ffn[l], moe_w_router_group[l], moe_b_router_group[l],
                               moe_w_router_expert[l], moe_b_router_expert[l]) for l in range(2)]
    triu = jnp.triu(jnp.ones((TOKEN_TILE, TOKEN_TILE), F32), 1).astype(BF16)
    cnt0 = (jnp.zeros((LANES, LANES), F32), jnp.zeros((8, LANES), F32))

    wqkv = attn_w_qkv[0].astype(BF16)
    bqkv = row1(attn_b_qkv[0])
    wo = attn_w_o[0].astype(BF16)
    bo = row1(attn_b_o[0])
    sinks = attn_sinks[0].astype(F32)
    g_mix0, g_mix1 = row1(norm_mix[0]), row1(norm_mix[1])
    cos_p, sin_p = _rope_tables(jnp.arange(seq, dtype=I32))
    pos_s = jnp.tile(PAST_LEN + jnp.arange(n_new, dtype=I32), dbs)
    cos_s, sin_s = _rope_tables(pos_s)

    qp, kp, vp = _qkv_call(xp, g_mix0, wqkv, bqkv, cos_p, sin_p, BF16)
    qs, ks, vs = _qkv_call(xs, g_mix0, wqkv, bqkv, cos_s, sin_s, F32)
    op = _attn_prompt_call(sinks, qp.reshape(bsz, seq, Q_DIM), kp.reshape(bsz, seq, KV_DIM),
                           vp.reshape(bsz, seq, KV_DIM))
    os_, nks, nvs = _attn_sample_call(jnp.repeat(sinks, n_new).reshape(1, -1), qs, ks, vs,
                                      cache_k[0].reshape(dbs, rows, KV_DIM),
                                      cache_v[0].reshape(dbs, rows, KV_DIM), n_new, 8)
    xp1, dest_p, cnt, rows_x = _oproj_call(op.reshape(tp, Q_DIM), xp, wo, bo, routers[0], triu,
                                           cnt0, None, False, cap)
    xs1, dest_s, cnt, rows_x = _oproj_call(os_, xs, wo, bo, routers[0], triu, cnt, rows_x, True, cap)
    rows_y = _expert_call(cnt[1], rows_x, moe_w_gate, moe_w_up, moe_w_down, 0, cap, tp + ts)
    xp2, up = _combine_call(dest_p, xp1, rows_y, g_mix1, True)
    xs2, us = _combine_call(dest_s, xs1, rows_y, g_mix1, True)

    disc = _s5_discretize(ssm_a_re[0], ssm_a_im[0], ssm_log_dt[0], ssm_b_re[0], ssm_b_im[0])
    w_p = _s5_chunk_weights(*disc, ssm_c_re[0], ssm_c_im[0], S5_CHUNK)
    w_s = _s5_chunk_weights(*disc, ssm_c_re[0], ssm_c_im[0], n_new)
    y_p, hpr, hpi = _s5_prompt_call(up.reshape(bsz, seq, D_MODEL), w_p)
    y_p = y_p.reshape(tp, D_MODEL)
    h0r = state_ssm_re[0].reshape(dbs, -1).astype(F32)
    h0i = state_ssm_im[0].reshape(dbs, -1).astype(F32)
    y_s, hsr, hsi = _s5_sample_call(us, h0r, h0i, w_s, n_new)

    wa, wb = ssm_w_glu_a[0].astype(BF16), ssm_w_glu_b[0].astype(BF16)
    d_row = row1(ssm_d[0])
    xp3, dest_p, cnt, rows_x = _glu_call(y_p, xp2, g_mix1, d_row, wa, wb, routers[1], triu,
                                         cnt0, None, False, cap)
    xs3, dest_s, cnt, rows_x = _glu_call(y_s, xs2, g_mix1, d_row, wa, wb, routers[1], triu,
                                         cnt, rows_x, True, cap)
    rows_y = _expert_call(cnt[1], rows_x, moe_w_gate, moe_w_up, moe_w_down, 1, cap, tp + ts)
    g_fin = row1(norm_final)
    (yp,) = _combine_call(dest_p, xp3, rows_y, g_fin, False)
    (ys,) = _combine_call(dest_s, xs3, rows_y, g_fin, False)

    kv5 = lambda a, n: a.reshape(1, n, rows, KV_HEADS, HEAD_DIM)
    st4 = lambda a, n: a.reshape(1, n, SSM_GROUPS, SSM_STATE)
    k_last = kp.reshape(bsz, seq, KV_DIM)[:, seq - WINDOW:]
    v_last = vp.reshape(bsz, seq, KV_DIM)[:, seq - WINDOW:]
    return (yp.reshape(bsz, seq, D_MODEL), ys.reshape(dbs, n_new, D_MODEL),
            k_last.reshape(1, bsz, WINDOW, KV_HEADS, HEAD_DIM), kv5(nks, dbs),
            v_last.reshape(1, bsz, WINDOW, KV_HEADS, HEAD_DIM), kv5(nvs, dbs),
            st4(hpr, bsz), st4(hsr, dbs), st4(hpi, bsz), st4(hsi, dbs))
```

```python
import functools
import math

import jax
import jax.numpy as jnp
from jax import lax
from jax.experimental import pallas as pl
from jax.experimental.pallas import tpu as pltpu

F32 = jnp.float32
BF16 = jnp.bfloat16
I32 = jnp.int32

D_MODEL = 1024
N_HEADS = 16
KV_HEADS = 4
HEAD_DIM = 64
Q_DIM = N_HEADS * HEAD_DIM
KV_DIM = KV_HEADS * HEAD_DIM
QKV_DIM = Q_DIM + 2 * KV_DIM
WINDOW = 128
BLOCK = 128
ROPE_THETA = 10000.0
PAST_LEN = 16384
SSM_GROUP_CH = 16
SSM_GROUPS = D_MODEL // SSM_GROUP_CH
SSM_STATE = 64
N_EXPERT_GROUPS = 4
EXPERTS_PER_GROUP = 8
N_EXPERTS = N_EXPERT_GROUPS * EXPERTS_PER_GROUP
EXPERT_FF = 128
NORM_EPS = 1e-5

LANES = 128
VMEM_LIMIT = 56 * 1024 * 1024
S5_CHUNK = 16
S5_GB = 8
TOKEN_TILE = 512
MOE_TILE = TOKEN_TILE
PAY_X = D_MODEL
PAY_W = PAY_X + LANES


def _cparams(sem):
    return pltpu.CompilerParams(dimension_semantics=sem, vmem_limit_bytes=VMEM_LIMIT)


def _rms(x, g):
    return x * lax.rsqrt(jnp.mean(x * x, axis=-1, keepdims=True) + NORM_EPS) * g


def _dot(a, b):
    return jnp.dot(a, b, preferred_element_type=F32)


def _dot_nt(a, b):
    return lax.dot_general(a, b, (((1,), (1,)), ((), ())), preferred_element_type=F32)


def _sigmoid(x):
    return 1.0 / (1.0 + jnp.exp(-x))


def _qkv_kernel(x_ref, g_ref, w_ref, b_ref, cos_ref, sin_ref, q_ref, k_ref, v_ref):
    xn = _rms(x_ref[...], g_ref[...])
    qkv = _dot(xn.astype(BF16), w_ref[...]) + b_ref[...]
    cos = cos_ref[...]
    sin = sin_ref[...]
    lane = lax.broadcasted_iota(I32, cos.shape, 1)
    first_half = (lane % HEAD_DIM) < (HEAD_DIM // 2)
    n_rot = (Q_DIM + KV_DIM) // LANES
    for c in range(n_rot):
        blk = qkv[:, c * LANES:(c + 1) * LANES]
        partner = jnp.where(first_half,
                            pltpu.roll(blk, LANES - HEAD_DIM // 2, 1),
                            pltpu.roll(blk, HEAD_DIM // 2, 1))
        rot = blk * cos + partner * sin
        if c < Q_DIM // LANES:
            q_ref[:, c * LANES:(c + 1) * LANES] = rot.astype(q_ref.dtype)
        else:
            k_ref[:, c * LANES - Q_DIM:(c + 1) * LANES - Q_DIM] = rot
    v_ref[...] = qkv[:, Q_DIM + KV_DIM:]


def _qkv_call(x2d, gain, w_bf, bias, cos_t, sin_t, q_dtype):
    t = x2d.shape[0]
    tm = TOKEN_TILE
    n_pos = cos_t.shape[0] // tm
    return pl.pallas_call(
        _qkv_kernel,
        grid=(t // tm,),
        in_specs=[
            pl.BlockSpec((tm, D_MODEL), lambda i: (i, 0)),
            pl.BlockSpec((1, D_MODEL), lambda i: (0, 0)),
            pl.BlockSpec((D_MODEL, QKV_DIM), lambda i: (0, 0)),
            pl.BlockSpec((1, QKV_DIM), lambda i: (0, 0)),
            pl.BlockSpec((tm, LANES), lambda i: (i % n_pos, 0)),
            pl.BlockSpec((tm, LANES), lambda i: (i % n_pos, 0)),
        ],
        out_specs=[
            pl.BlockSpec((tm, Q_DIM), lambda i: (i, 0)),
            pl.BlockSpec((tm, KV_DIM), lambda i: (i, 0)),
            pl.BlockSpec((tm, KV_DIM), lambda i: (i, 0)),
        ],
        out_shape=[
            jax.ShapeDtypeStruct((t, Q_DIM), q_dtype),
            jax.ShapeDtypeStruct((t, KV_DIM), F32),
            jax.ShapeDtypeStruct((t, KV_DIM), F32),
        ],
        compiler_params=_cparams(("parallel",)),
        name="qkv_rope",
    )(x2d, gain, w_bf, bias, cos_t, sin_t)


def _place_head(blk, low, h):
    kvh = h // (N_HEADS // KV_HEADS)
    if (h % 2) != (kvh % 2):
        blk = pltpu.roll(blk, HEAD_DIM, 1)
    blk = jnp.where(low if kvh % 2 == 0 else ~low, blk, 0.0)
    zero = jnp.zeros_like(blk)
    return jnp.concatenate([blk, zero] if kvh < 2 else [zero, blk], axis=1)


def _take_head(oz, h, rows):
    kvh = h // (N_HEADS // KV_HEADS)
    blk = oz[rows, (kvh // 2) * LANES:(kvh // 2 + 1) * LANES]
    if (h % 2) != (kvh % 2):
        blk = pltpu.roll(blk, HEAD_DIM, 1)
    return blk


def _attn_prompt_kernel(sink_ref, q_ref, kp_ref, kc_ref, vp_ref, vc_ref, o_ref):
    n = pl.program_id(1)
    k2 = jnp.concatenate([kp_ref[0], kc_ref[0]], axis=0).astype(BF16)
    v2 = jnp.concatenate([vp_ref[0], vc_ref[0]], axis=0).astype(BF16)
    key = lax.broadcasted_iota(I32, (2 * BLOCK, 2 * BLOCK), 0)
    qry = lax.broadcasted_iota(I32, (2 * BLOCK, 2 * BLOCK), 1) % BLOCK
    mask = (key >= qry) & (key <= qry + WINDOW) & ((n > 0) | (key >= BLOCK))
    low = lax.broadcasted_iota(I32, (BLOCK, LANES), 1) < HEAD_DIM
    n_pairs = N_HEADS // 2

    def scores(pr):
        qblk = q_ref[0, :, pr * LANES:(pr + 1) * LANES].astype(F32)
        qz = jnp.concatenate([_place_head(qblk, low, h) for h in (2 * pr, 2 * pr + 1)],
                             axis=0).astype(BF16)
        s = _dot_nt(k2, qz) * (HEAD_DIM ** -0.5)
        return jnp.where(mask, s, -jnp.inf)

    def finish(pr, s):
        sink = sink_ref[:, 2 * pr * BLOCK:(2 * pr + 2) * BLOCK]
        m = jnp.maximum(jnp.max(s, axis=0, keepdims=True), sink)
        p = jnp.exp(s - m)
        denom = jnp.sum(p, axis=0, keepdims=True) + jnp.exp(sink - m)
        w = (p * (1.0 / denom)).T.astype(BF16)
        oz = _dot(w, v2)
        parts = [_take_head(oz, h, slice(i * BLOCK, (i + 1) * BLOCK))
                 for i, h in enumerate((2 * pr, 2 * pr + 1))]
        o_ref[0, :, pr * LANES:(pr + 1) * LANES] = jnp.where(low, parts[0], parts[1]).astype(o_ref.dtype)

    ahead = 2
    pending = [scores(pr) for pr in range(ahead)]
    for pr in range(n_pairs):
        if pr + ahead < n_pairs:
            pending.append(scores(pr + ahead))
        finish(pr, pending.pop(0))


def _attn_prompt_call(sinks, q, k, v):
    b, l, _ = q.shape
    nb = l // BLOCK
    kv_cur = pl.BlockSpec((1, BLOCK, KV_DIM), lambda i, n: (i, n, 0))
    kv_prev = pl.BlockSpec((1, BLOCK, KV_DIM), lambda i, n: (i, jnp.maximum(n - 1, 0), 0))
    return pl.pallas_call(
        _attn_prompt_kernel,
        grid=(b, nb),
        in_specs=[
            pl.BlockSpec((1, N_HEADS * BLOCK), lambda i, n: (0, 0)),
            pl.BlockSpec((1, BLOCK, Q_DIM), lambda i, n: (i, n, 0)),
            kv_prev, kv_cur, kv_prev, kv_cur,
        ],
        out_specs=pl.BlockSpec((1, BLOCK, Q_DIM), lambda i, n: (i, n, 0)),
        out_shape=jax.ShapeDtypeStruct((b, l, Q_DIM), BF16),
        compiler_params=_cparams(("parallel", "parallel")),
        name="attn_prompt",
    )(sinks, q, k, k, v, v)


def _attn_sample_kernel(n_new, seqs, sink_ref, q_ref, kn_ref, vn_ref, ck_ref, cv_ref,
                        o_ref, nk_ref, nv_ref):
    rows = ck_ref.shape[1]
    keys = 2 * rows
    n_cols = N_HEADS * n_new
    key = lax.broadcasted_iota(I32, (keys, n_cols), 0)
    qry = lax.broadcasted_iota(I32, (keys, n_cols), 1) % n_new
    mask = ((key < rows) & (key >= qry)) | ((key >= rows) & (key - rows <= qry))
    low = lax.broadcasted_iota(I32, (n_new, LANES), 1) < HEAD_DIM
    sink = sink_ref[...]
    pad = jnp.zeros((rows - n_new, KV_DIM), F32)

    def body(sb, carry):
        r0 = pl.multiple_of(sb * n_new, n_new)
        k_new = kn_ref[pl.ds(r0, n_new), :]
        v_new = vn_ref[pl.ds(r0, n_new), :]
        k_all = jnp.concatenate([ck_ref[sb], k_new, pad], axis=0).astype(BF16)
        v_all = jnp.concatenate([cv_ref[sb], v_new, pad], axis=0).astype(BF16)
        qz = jnp.concatenate(
            [_place_head(q_ref[pl.ds(r0, n_new), (h // 2) * LANES:(h // 2 + 1) * LANES], low, h)
             for h in range(N_HEADS)], axis=0).astype(BF16)
        s = _dot_nt(k_all, qz) * (HEAD_DIM ** -0.5)
        s = jnp.where(mask, s, -jnp.inf)
        m = jnp.maximum(jnp.max(s, axis=0, keepdims=True), sink)
        p = jnp.exp(s - m)
        denom = jnp.sum(p, axis=0, keepdims=True) + jnp.exp(sink - m)
        w = (p * (1.0 / denom)).T.astype(BF16)
        oz = _dot(w, v_all)
        for c in range(N_HEADS // 2):
            parts = [_take_head(oz, h, slice(h * n_new, (h + 1) * n_new)) for h in (2 * c, 2 * c + 1)]
            o_ref[pl.ds(r0, n_new), c * LANES:(c + 1) * LANES] = jnp.where(low, parts[0], parts[1])
        nk_ref[sb, 0:rows - n_new, :] = ck_ref[sb, n_new:rows, :]
        nk_ref[sb, rows - n_new:rows, :] = k_new
        nv_ref[sb, 0:rows - n_new, :] = cv_ref[sb, n_new:rows, :]
        nv_ref[sb, rows - n_new:rows, :] = v_new
        return carry

    lax.fori_loop(0, seqs, body, 0)


def _attn_sample_call(sink_row, q, k_new, v_new, cache_k, cache_v, n_new, seqs):
    b, rows, _ = cache_k.shape
    tok = pl.BlockSpec((seqs * n_new, Q_DIM), lambda i: (i, 0))
    tok_kv = pl.BlockSpec((seqs * n_new, KV_DIM), lambda i: (i, 0))
    cache = pl.BlockSpec((seqs, rows, KV_DIM), lambda i: (i, 0, 0))
    return pl.pallas_call(
        functools.partial(_attn_sample_kernel, n_new, seqs),
        grid=(b // seqs,),
        in_specs=[pl.BlockSpec((1, N_HEADS * n_new), lambda i: (0, 0)), tok, tok_kv, tok_kv, cache, cache],
        out_specs=[tok, cache, cache],
        out_shape=[
            jax.ShapeDtypeStruct((b * n_new, Q_DIM), F32),
            jax.ShapeDtypeStruct((b, rows, KV_DIM), F32),
            jax.ShapeDtypeStruct((b, rows, KV_DIM), F32),
        ],
        compiler_params=_cparams(("parallel",)),
        name="attn_sample",
    )(sink_row, q, k_new, v_new, cache_k, cache_v)


def _moe_cap(t_total):
    return t_total + MOE_TILE


def _route_scatter(final, cap, x1, gn_ref, wrh_ref, wrl_ref, br_ref, triu_ref, cntc_in, cntr_in,
                   dest_ref, cntc_out, cntr_out, xs_hbm, pay, zrows, dest_v, dest_s, cntc, cntr, sem):
    i = pl.program_id(0)
    n = pl.num_programs(0)
    slot = i % 2
    tm = x1.shape[0]

    @pl.when(i == 0)
    def _():
        cntc[...] = cntc_in[...]
        cntr[...] = cntr_in[...]

    xn = _rms(x1, gn_ref[...])
    xh = xn.astype(BF16)
    xl = (xn - xh.astype(F32)).astype(BF16)
    logits = _dot(xh, wrh_ref[...]) + _dot(xl, wrh_ref[...]) + _dot(xh, wrl_ref[...]) + br_ref[...]
    lane = lax.broadcasted_iota(I32, logits.shape, 1).astype(F32)
    big = jnp.float32(4 * LANES)
    neg = -jnp.inf
    gl = jnp.where((lane >= N_EXPERTS) & (lane < N_EXPERTS + N_EXPERT_GROUPS), logits, neg)
    gmax = jnp.max(gl, axis=-1, keepdims=True)
    g_val = 1.0 / jnp.sum(jnp.exp(gl - gmax), axis=-1, keepdims=True)
    g_idx = jnp.min(jnp.where(gl == gmax, lane, big), axis=-1, keepdims=True) - N_EXPERTS
    lo = g_idx * EXPERTS_PER_GROUP
    el = jnp.where((lane >= lo) & (lane < lo + EXPERTS_PER_GROUP), logits, neg)
    e1 = jnp.max(el, axis=-1, keepdims=True)
    i1 = jnp.min(jnp.where(el == e1, lane, big), axis=-1, keepdims=True)
    el2 = jnp.where(lane == i1, neg, el)
    e2 = jnp.max(el2, axis=-1, keepdims=True)
    i2 = jnp.min(jnp.where(el2 == e2, lane, big), axis=-1, keepdims=True)
    t = jnp.exp(e2 - e1)
    w1 = 1.0 / (1.0 + t)
    w2 = t / (1.0 + t)
    wts = g_val * (jnp.where(lane == i1 - lo, w1, 0.0) + jnp.where(lane == i2 - lo, w2, 0.0))

    def wait_slot(s):
        pltpu.make_async_copy(pay.at[s], xs_hbm.at[pl.ds(0, tm), :], sem.at[s]).wait()

    @pl.when(i >= 2)
    def _():
        wait_slot(slot)

    pay[slot, :, :PAY_X] = xn
    pay[slot, :, PAY_X:] = wts

    oh = (lane == g_idx).astype(F32)
    oht = oh.T
    rank = _dot(oht.astype(BF16), triu_ref[...])
    grp = lax.broadcasted_iota(I32, (LANES, 1), 0).astype(F32)
    base = grp * float(cap) + cntc[:, 0:1]
    dest = jnp.sum(oht * (rank + base), axis=0, keepdims=True).astype(I32)
    cntc[...] = cntc[...] + jnp.sum(oht, axis=1, keepdims=True)
    cntr[...] = cntr[...] + jnp.sum(oh, axis=0, keepdims=True)
    dest_ref[0] = dest
    dest_v[...] = dest
    pltpu.sync_copy(dest_v, dest_s)

    for s in range(2):
        @pl.when(slot == s)
        def _():
            for r in range(tm):
                pltpu.make_async_copy(pay.at[s, pl.ds(r, 1), :],
                                      xs_hbm.at[pl.ds(dest_s[0, r], 1), :], sem.at[s]).start()

    @pl.when(i == n - 1)
    def _():
        cntc_out[...] = cntc[...]
        cntr_out[...] = cntr[...]

        @pl.when(n >= 2)
        def _():
            wait_slot(1 - slot)

        wait_slot(slot)
        if final:
            zrows[...] = jnp.zeros_like(zrows)
            dest_v[:, 0:LANES] = cntr[0:1, :].astype(I32)
            pltpu.sync_copy(dest_v, dest_s)
            for g in range(N_EXPERT_GROUPS):
                c_g = dest_s[0, g]
                c_up = ((c_g + 7) // 8) * 8
                for k in range(7):
                    @pl.when(c_g + k < c_up)
                    def _():
                        one = pltpu.make_async_copy(zrows.at[pl.ds(0, 1), :],
                                                    xs_hbm.at[pl.ds(g * cap + c_g + k, 1), :], sem.at[1])
                        one.start()
                        one.wait()
                start = pl.multiple_of(g * cap + c_up, 8)
                pltpu.make_async_copy(zrows, xs_hbm.at[pl.ds(start, MOE_TILE), :], sem.at[0]).start()
            for g in range(N_EXPERT_GROUPS):
                pltpu.make_async_copy(zrows, xs_hbm.at[pl.ds(0, MOE_TILE), :], sem.at[0]).wait()


def _oproj_kernel(has_prev, final, cap, *refs):
    (o_ref, x_ref, wo_ref, bo_ref) = refs[:4]
    rest = refs[4:]
    if has_prev:
        rest = rest[:7] + rest[8:]
    x1 = x_ref[...] + _dot(o_ref[...].astype(BF16), wo_ref[...]) + bo_ref[...]
    x1_ref = rest[7]
    x1_ref[...] = x1
    _route_scatter(final, cap, x1, *rest[:7], *rest[8:])


def _gelu_tanh(x):
    return x * (0.5 * (1.0 + jnp.tanh(math.sqrt(2.0 / math.pi) * (x + 0.044715 * (x * x * x)))))


def _glu_kernel(has_prev, final, cap, *refs):
    (y_ref, x_ref, gm_ref, d_ref, wa_ref, wb_ref) = refs[:6]
    rest = refs[6:]
    if has_prev:
        rest = rest[:7] + rest[8:]
    x = x_ref[...]
    u = _rms(x, gm_ref[...])
    z = _gelu_tanh(y_ref[...] + d_ref[...] * u).astype(BF16)
    x1 = x + _dot(z, wa_ref[...]) * _sigmoid(_dot(z, wb_ref[...]))
    x1_ref = rest[7]
    x1_ref[...] = x1
    _route_scatter(final, cap, x1, *rest[:7], *rest[8:])


def _row_spec(tm, width):
    return pl.BlockSpec((tm, width), lambda i: (i, 0))


def _const_spec(shape):
    return pl.BlockSpec(shape, lambda i: (0,) * len(shape))


def _mixer_call(body, name, lead_args, lead_specs, x2d, router, triu, cnt, xs_prev, final, cap):
    t = x2d.shape[0]
    tm = TOKEN_TILE
    n_tiles = t // tm
    gn, wrh, wrl, br = router
    cntc, cntr = cnt
    has_prev = xs_prev is not None
    in_specs = lead_specs + [
        _const_spec((1, D_MODEL)), _const_spec((D_MODEL, LANES)), _const_spec((D_MODEL, LANES)),
        _const_spec((1, LANES)), _const_spec((tm, tm)), _const_spec((LANES, LANES)), _const_spec((8, LANES)),
    ]
    args = list(lead_args) + [gn, wrh, wrl, br, triu, cntc, cntr]
    aliases = {}
    if has_prev:
        in_specs.append(pl.BlockSpec(memory_space=pl.ANY))
        args.append(xs_prev)
        aliases = {len(args) - 1: 4}
    x1, dest, cntc2, cntr2, xs = pl.pallas_call(
        functools.partial(body, has_prev, final, cap),
        grid=(n_tiles,),
        in_specs=in_specs,
        out_specs=[
            _row_spec(tm, D_MODEL),
            pl.BlockSpec((1, 1, tm), lambda i: (i, 0, 0)),
            _const_spec((LANES, LANES)), _const_spec((8, LANES)),
            pl.BlockSpec(memory_space=pl.ANY),
        ],
        out_shape=[
            jax.ShapeDtypeStruct((t, D_MODEL), F32),
            jax.ShapeDtypeStruct((n_tiles, 1, tm), I32),
            jax.ShapeDtypeStruct((LANES, LANES), F32),
            jax.ShapeDtypeStruct((8, LANES), F32),
            jax.ShapeDtypeStruct((N_EXPERT_GROUPS * cap, PAY_W), F32),
        ],
        scratch_shapes=[
            pltpu.VMEM((2, tm, PAY_W), F32),
            pltpu.VMEM((MOE_TILE, PAY_W), F32),
            pltpu.VMEM((1, tm), I32),
            pltpu.SMEM((1, tm), I32),
            pltpu.VMEM((LANES, LANES), F32),
            pltpu.VMEM((8, LANES), F32),
            pltpu.SemaphoreType.DMA((2,)),
        ],
        input_output_aliases=aliases,
        compiler_params=_cparams(("arbitrary",)),
        name=name,
    )(*args)
    return x1, dest, (cntc2, cntr2), xs


def _oproj_call(o2d, x2d, wo_bf, bo, router, triu, cnt, xs_prev, final, cap):
    tm = TOKEN_TILE
    lead_specs = [_row_spec(tm, Q_DIM), _row_spec(tm, D_MODEL), _const_spec((Q_DIM, D_MODEL)),
                  _const_spec((1, D_MODEL))]
    return _mixer_call(_oproj_kernel, "oproj_route", [o2d, x2d, wo_bf, bo], lead_specs, x2d,
                       router, triu, cnt, xs_prev, final, cap)


def _glu_call(y2d, x2d, gm, d, wa_bf, wb_bf, router, triu, cnt, xs_prev, final, cap):
    tm = TOKEN_TILE
    lead_specs = [_row_spec(tm, D_MODEL), _row_spec(tm, D_MODEL), _const_spec((1, D_MODEL)),
                  _const_spec((1, D_MODEL)), _const_spec((D_MODEL, D_MODEL)),
                  _const_spec((D_MODEL, D_MODEL))]
    return _mixer_call(_glu_kernel, "glu_route", [y2d, x2d, gm, d, wa_bf, wb_bf], lead_specs, x2d,
                       router, triu, cnt, xs_prev, final, cap)


def _expert_kernel(blk_in_ref, blk_out_ref, grp_ref, valid_ref, xs_ref, wg_ref, wu_ref, wd_ref,
                   ys_ref, wgu, wdn, hid):
    j = pl.program_id(0)
    f = EXPERT_FF
    changed = (j == 0) | (grp_ref[j] != grp_ref[jnp.maximum(j - 1, 0)])

    @pl.when(changed)
    def _():
        for e in range(EXPERTS_PER_GROUP):
            wgu[e, :, :f] = wg_ref[e].astype(BF16)
            wgu[e, :, f:] = wu_ref[e].astype(BF16)
            wdn[e * f:(e + 1) * f, :] = wd_ref[e].astype(BF16)

    @pl.when(valid_ref[j] == 1)
    def _():
        x = xs_ref[:, :PAY_X].astype(BF16)
        wts = xs_ref[:, PAY_X:]
        for e in range(EXPERTS_PER_GROUP):
            gu = _dot(x, wgu[e])
            g, u = gu[:, :f], gu[:, f:]
            hid[:, e * f:(e + 1) * f] = ((g * _sigmoid(g)) * u * wts[:, e:e + 1]).astype(BF16)
        ys_ref[...] = _dot(hid[...], wdn[...])

    @pl.when(valid_ref[j] == 0)
    def _():
        ys_ref[...] = jnp.zeros_like(ys_ref)


def _expert_tile_map(cnt_row, cap, n_steps):
    per = cap // MOE_TILE
    c = cnt_row[0, :N_EXPERT_GROUPS].astype(I32)
    tiles = (c + MOE_TILE - 1) // MOE_TILE
    ends = jnp.cumsum(tiles)
    starts = ends - tiles
    total = ends[-1]
    j = jnp.arange(n_steps, dtype=I32)
    jj = jnp.maximum(jnp.minimum(j, total - 1), 0)
    g = jnp.sum((jj[:, None] >= ends[None, :]).astype(I32), axis=1)
    blk_in = g * per + jj - starts[g]
    valid = (j < total).astype(I32)
    blk_out = jnp.where(valid == 1, blk_in, N_EXPERT_GROUPS * per)
    return blk_in, blk_out, g, valid


def _expert_call(cnt_row, xs, w_gate, w_up, w_down, layer, cap, t_total):
    n_steps = t_total // MOE_TILE + N_EXPERT_GROUPS
    blk_in, blk_out, grp, valid = _expert_tile_map(cnt_row, cap, n_steps)
    maps = (blk_in, blk_out, grp + layer * N_EXPERT_GROUPS, valid)
    e, f = EXPERTS_PER_GROUP, EXPERT_FF
    w_gate = w_gate.reshape(-1, D_MODEL, f)
    w_up = w_up.reshape(-1, D_MODEL, f)
    w_down = w_down.reshape(-1, f, D_MODEL)
    grid_spec = pltpu.PrefetchScalarGridSpec(
        num_scalar_prefetch=4,
        grid=(n_steps,),
        in_specs=[
            pl.BlockSpec((MOE_TILE, PAY_W), lambda j, bi, bo, gr, va: (bi[j], 0)),
            pl.BlockSpec((e, D_MODEL, f), lambda j, bi, bo, gr, va: (gr[j], 0, 0)),
            pl.BlockSpec((e, D_MODEL, f), lambda j, bi, bo, gr, va: (gr[j], 0, 0)),
            pl.BlockSpec((e, f, D_MODEL), lambda j, bi, bo, gr, va: (gr[j], 0, 0)),
        ],
        out_specs=pl.BlockSpec((MOE_TILE, D_MODEL), lambda j, bi, bo, gr, va: (bo[j], 0)),
        scratch_shapes=[
            pltpu.VMEM((e, D_MODEL, 2 * f), BF16),
            pltpu.VMEM((e * f, D_MODEL), BF16),
            pltpu.VMEM((MOE_TILE, e * f), BF16),
        ],
    )
    return pl.pallas_call(
        _expert_kernel,
        grid_spec=grid_spec,
        out_shape=jax.ShapeDtypeStruct((N_EXPERT_GROUPS * cap + MOE_TILE, D_MODEL), F32),
        compiler_params=_cparams(("arbitrary",)),
        name="moe_experts",
    )(*maps, xs, w_gate, w_up, w_down)


def _combine_kernel(emit_x, dest_ref, x_ref, gnext_ref, ys_hbm, *rest):
    outs, (ybuf, sem) = rest[:-2], rest[-2:]
    i = pl.program_id(0)
    n_tiles = pl.num_programs(0) - 1
    slot = i % 2
    tm = x_ref.shape[0]

    for s in range(2):
        @pl.when((i < n_tiles) & (slot == s))
        def _():
            for r in range(tm):
                pltpu.make_async_copy(ys_hbm.at[pl.ds(dest_ref[0, 0, r], 1), :],
                                      ybuf.at[s, pl.ds(r, 1), :], sem.at[s]).start()

    @pl.when(i >= 1)
    def _():
        prev = 1 - slot
        pltpu.make_async_copy(ys_hbm.at[pl.ds(0, tm), :], ybuf.at[prev], sem.at[prev]).wait()
        x2 = x_ref[...] + ybuf[prev]
        normed = _rms(x2, gnext_ref[...])
        if emit_x:
            outs[0][...] = x2
            outs[1][...] = normed
        else:
            outs[0][...] = normed


def _combine_call(dest, x2d, ys, gnext, emit_x):
    t = x2d.shape[0]
    tm = TOKEN_TILE
    n_tiles = t // tm
    n_out = 2 if emit_x else 1
    done = lambda i: (jnp.maximum(i - 1, 0), 0)
    return pl.pallas_call(
        functools.partial(_combine_kernel, emit_x),
        grid=(n_tiles + 1,),
        in_specs=[
            pl.BlockSpec((1, 1, tm), lambda i: (jnp.minimum(i, n_tiles - 1), 0, 0),
                         memory_space=pltpu.SMEM),
            pl.BlockSpec((tm, D_MODEL), done), _const_spec((1, D_MODEL)),
            pl.BlockSpec(memory_space=pl.ANY),
        ],
        out_specs=[pl.BlockSpec((tm, D_MODEL), done)] * n_out,
        out_shape=[jax.ShapeDtypeStruct((t, D_MODEL), F32)] * n_out,
        scratch_shapes=[pltpu.VMEM((2, tm, D_MODEL), F32), pltpu.SemaphoreType.DMA((2,))],
        compiler_params=_cparams(("arbitrary",)),
        name="moe_combine",
    )(dest, x2d, gnext, ys)


def _s5_state_in(u_ref_val, wre_ref, wim_ref, sre_ref, sim_ref, pair_w):
    for m in range(S5_GB // 2):
        up = u_ref_val(m * pair_w, pair_w)
        sre_ref[:, m * LANES:(m + 1) * LANES] = _dot(up, wre_ref[m])
        sim_ref[:, m * LANES:(m + 1) * LANES] = _dot(up, wim_ref[m])


def _s5_outputs(u_ref_val, hre, him, m_ref, gre_ref, gim_ref, y_store, pair_w):
    gw = pair_w // 2
    for m in range(S5_GB // 2):
        hr = hre(m).astype(BF16)
        hi = him(m).astype(BF16)
        y = _dot(hr, gre_ref[m]) + _dot(hi, gim_ref[m])
        y0 = y[:, :gw] + _dot(u_ref_val(m * pair_w, gw), m_ref[2 * m])
        y1 = y[:, gw:] + _dot(u_ref_val(m * pair_w + gw, gw), m_ref[2 * m + 1])
        y_store(m * pair_w, gw, y0)
        y_store(m * pair_w + gw, gw, y1)


def _s5_flatten(load_rows, q, n, ut, uflat):
    gc = SSM_GROUP_CH
    qc = q * gc
    for s in range(q):
        ut[:, s * gc:(s + 1) * gc, :] = load_rows(s).T.reshape(S5_GB, gc, n)
    for g in range(S5_GB):
        uflat[:, g * qc:(g + 1) * qc] = ut[g].T.astype(BF16)


def _s5_unflatten(yflat, yt, store_rows, q, n):
    gc = SSM_GROUP_CH
    qc = q * gc
    for g in range(S5_GB):
        yt[g] = yflat[:, g * qc:(g + 1) * qc].T
    for t in range(q):
        store_rows(t, yt[:, t * gc:(t + 1) * gc, :].reshape(S5_GB * gc, n).T)


def _s5_prompt_kernel(u_ref, wre_ref, wim_ref, m_ref, gre_ref, gim_ref, aqr_ref, aqi_ref,
                      y_ref, her_ref, hei_ref, ut, uflat, sre, sim, hre, him, yflat, yt):
    pair_w = 2 * S5_CHUNK * SSM_GROUP_CH
    n_chunks = u_ref.shape[1] // S5_CHUNK
    _s5_flatten(lambda s: u_ref[0, pl.ds(s, n_chunks, stride=S5_CHUNK), :],
                S5_CHUNK, n_chunks, ut, uflat)
    u_val = lambda off, w: uflat[:, off:off + w]
    _s5_state_in(u_val, wre_ref, wim_ref, sre, sim, pair_w)
    ar = aqr_ref[...]
    ai = aqi_ref[...]

    def step(n, carry):
        hr, hi = carry
        hre[pl.ds(n, 1), :] = hr
        him[pl.ds(n, 1), :] = hi
        sr = sre[pl.ds(n, 1), :]
        si = sim[pl.ds(n, 1), :]
        return ar * hr - ai * hi + sr, ar * hi + ai * hr + si

    zero = jnp.zeros(ar.shape, F32)
    hr, hi = lax.fori_loop(0, n_chunks, step, (zero, zero))
    her_ref[0] = hr
    hei_ref[0] = hi

    def y_store(off, w, val):
        yflat[:, off:off + w] = val

    _s5_outputs(u_val, lambda m: hre[:, m * LANES:(m + 1) * LANES],
                lambda m: him[:, m * LANES:(m + 1) * LANES],
                m_ref, gre_ref, gim_ref, y_store, pair_w)

    def store_rows(t, val):
        y_ref[0, pl.ds(t, n_chunks, stride=S5_CHUNK), :] = val

    _s5_unflatten(yflat, yt, store_rows, S5_CHUNK, n_chunks)


def _s5_sample_kernel(n_new, u_ref, h0r_ref, h0i_ref, wre_ref, wim_ref, m_ref, gre_ref, gim_ref,
                      aqr_ref, aqi_ref, y_ref, hnr_ref, hni_ref, ut, uflat, sre, sim, yflat, yt):
    pair_w = 2 * n_new * SSM_GROUP_CH
    seqs = h0r_ref.shape[0]
    _s5_flatten(lambda s: u_ref[pl.ds(s, seqs, stride=n_new), :], n_new, seqs, ut, uflat)
    u_val = lambda off, w: uflat[:, off:off + w]
    _s5_state_in(u_val, wre_ref, wim_ref, sre, sim, pair_w)
    ar = aqr_ref[...]
    ai = aqi_ref[...]
    h0r = h0r_ref[...]
    h0i = h0i_ref[...]
    hnr_ref[...] = ar * h0r - ai * h0i + sre[...]
    hni_ref[...] = ar * h0i + ai * h0r + sim[...]

    def y_store(off, w, val):
        yflat[:, off:off + w] = val

    _s5_outputs(u_val, lambda m: h0r_ref[:, m * LANES:(m + 1) * LANES],
                lambda m: h0i_ref[:, m * LANES:(m + 1) * LANES],
                m_ref, gre_ref, gim_ref, y_store, pair_w)

    def store_rows(t, val):
        y_ref[pl.ds(t, seqs, stride=n_new), :] = val

    _s5_unflatten(yflat, yt, store_rows, n_new, seqs)


def _s5_weight_specs(q, idx):
    qc = q * SSM_GROUP_CH
    np_ = S5_GB // 2
    st = S5_GB * SSM_STATE
    return [
        pl.BlockSpec((np_, 2 * qc, LANES), lambda *a: (idx(*a), 0, 0)),
        pl.BlockSpec((np_, 2 * qc, LANES), lambda *a: (idx(*a), 0, 0)),
        pl.BlockSpec((S5_GB, qc, qc), lambda *a: (idx(*a), 0, 0)),
        pl.BlockSpec((np_, LANES, 2 * qc), lambda *a: (idx(*a), 0, 0)),
        pl.BlockSpec((np_, LANES, 2 * qc), lambda *a: (idx(*a), 0, 0)),
        pl.BlockSpec((1, st), lambda *a: (0, idx(*a))),
        pl.BlockSpec((1, st), lambda *a: (0, idx(*a))),
    ]


def _s5_prompt_call(u, w):
    b, seq, _ = u.shape
    n_chunks = seq // S5_CHUNK
    gbl = S5_GB * SSM_GROUP_CH
    qc = S5_CHUNK * SSM_GROUP_CH
    st = S5_GB * SSM_STATE
    n_gb = SSM_GROUPS // S5_GB
    gb_of = lambda g, i: g
    tok = pl.BlockSpec((1, seq, gbl), lambda g, i: (i, 0, g))
    return pl.pallas_call(
        _s5_prompt_kernel,
        grid=(n_gb, b),
        in_specs=[tok] + _s5_weight_specs(S5_CHUNK, gb_of),
        out_specs=[
            tok,
            pl.BlockSpec((1, 1, st), lambda g, i: (i, 0, g)),
            pl.BlockSpec((1, 1, st), lambda g, i: (i, 0, g)),
        ],
        out_shape=[
            jax.ShapeDtypeStruct((b, seq, D_MODEL), F32),
            jax.ShapeDtypeStruct((b, 1, SSM_GROUPS * SSM_STATE), F32),
            jax.ShapeDtypeStruct((b, 1, SSM_GROUPS * SSM_STATE), F32),
        ],
        scratch_shapes=[
            pltpu.VMEM((S5_GB, qc, n_chunks), F32),
            pltpu.VMEM((n_chunks, S5_GB * qc), BF16),
            pltpu.VMEM((n_chunks, st), F32), pltpu.VMEM((n_chunks, st), F32),
            pltpu.VMEM((n_chunks, st), F32), pltpu.VMEM((n_chunks, st), F32),
            pltpu.VMEM((n_chunks, S5_GB * qc), F32),
            pltpu.VMEM((S5_GB, qc, n_chunks), F32),
        ],
        compiler_params=_cparams(("parallel", "parallel")),
        name="s5_prompt",
    )(u, *w)


def _s5_sample_call(u2d, h0r, h0i, w, n_new):
    t = u2d.shape[0]
    b = t // n_new
    gbl = S5_GB * SSM_GROUP_CH
    qc = n_new * SSM_GROUP_CH
    st = S5_GB * SSM_STATE
    n_gb = SSM_GROUPS // S5_GB
    gb_of = lambda g: g
    state = pl.BlockSpec((b, st), lambda g: (0, g))
    tok = pl.BlockSpec((t, gbl), lambda g: (0, g))
    return pl.pallas_call(
        functools.partial(_s5_sample_kernel, n_new),
        grid=(n_gb,),
        in_specs=[tok, state, state] + _s5_weight_specs(n_new, gb_of),
        out_specs=[tok, state, state],
        out_shape=[
            jax.ShapeDtypeStruct((t, D_MODEL), F32),
            jax.ShapeDtypeStruct((b, SSM_GROUPS * SSM_STATE), F32),
            jax.ShapeDtypeStruct((b, SSM_GROUPS * SSM_STATE), F32),
        ],
        scratch_shapes=[
            pltpu.VMEM((S5_GB, qc, b), F32),
            pltpu.VMEM((b, S5_GB * qc), BF16),
            pltpu.VMEM((b, st), F32), pltpu.VMEM((b, st), F32),
            pltpu.VMEM((b, S5_GB * qc), F32),
            pltpu.VMEM((S5_GB, qc, b), F32),
        ],
        compiler_params=_cparams(("parallel",)),
        name="s5_sample",
    )(u2d, h0r, h0i, *w)


def _block_diag_pairs(w):
    g, r, n = w.shape
    w2 = w.reshape(g // 2, 2, r, n)
    z = jnp.zeros((g // 2, r, n), w.dtype)
    top = jnp.concatenate([w2[:, 0], z], axis=2)
    bot = jnp.concatenate([z, w2[:, 1]], axis=2)
    return jnp.concatenate([top, bot], axis=1)


def _s5_discretize(a_re, a_im, log_dt, b_re, b_im):
    delta = jnp.exp(log_dt.astype(F32))[:, None]
    lr, li = a_re.astype(F32), a_im.astype(F32)
    mag = jnp.exp(delta * lr)
    abar_r = mag * jnp.cos(delta * li)
    abar_i = mag * jnp.sin(delta * li)
    nr, ni = abar_r - 1.0, abar_i
    den = lr * lr + li * li
    coef_r = ((nr * lr + ni * li) / den)[..., None]
    coef_i = ((ni * lr - nr * li) / den)[..., None]
    br, bi = b_re.astype(F32), b_im.astype(F32)
    return delta * lr, delta * li, coef_r * br - coef_i * bi, coef_r * bi + coef_i * br


def _s5_chunk_weights(log_mag, phase, bbar_r, bbar_i, c_re, c_im, q):
    g, p = log_mag.shape
    c = SSM_GROUP_CH
    hi = lax.Precision.HIGHEST
    k = jnp.arange(q + 1, dtype=F32)[:, None, None]
    mag = jnp.exp(k * log_mag[None])
    pw_r, pw_i = mag * jnp.cos(k * phase[None]), mag * jnp.sin(k * phase[None])
    rev_r, rev_i = pw_r[:q][::-1][..., None], pw_i[:q][::-1][..., None]
    wst_r = (rev_r * bbar_r[None] - rev_i * bbar_i[None]).transpose(1, 0, 3, 2).reshape(g, q * c, p)
    wst_i = (rev_r * bbar_i[None] + rev_i * bbar_r[None]).transpose(1, 0, 3, 2).reshape(g, q * c, p)
    x_r = pw_r[:q, :, :, None] * bbar_r[None] - pw_i[:q, :, :, None] * bbar_i[None]
    x_i = pw_r[:q, :, :, None] * bbar_i[None] + pw_i[:q, :, :, None] * bbar_r[None]
    cr, ci = c_re.astype(F32), c_im.astype(F32)
    ker = (jnp.einsum('gcp,tgpd->gtdc', cr, x_r, precision=hi)
           - jnp.einsum('gcp,tgpd->gtdc', ci, x_i, precision=hi))
    idx = jnp.arange(q)
    place = (idx[None, :, None] - idx[:, None, None] == idx[None, None, :]).astype(BF16)
    m = jnp.einsum('stk,gkdc->gsdtc', place, ker.astype(BF16),
                   preferred_element_type=F32).reshape(g, q * c, q * c)
    cr_t, ci_t = cr.transpose(0, 2, 1)[:, :, None, :], ci.transpose(0, 2, 1)[:, :, None, :]
    pr_t = pw_r[1:q + 1].transpose(1, 2, 0)[..., None]
    pi_t = pw_i[1:q + 1].transpose(1, 2, 0)[..., None]
    g_re = (cr_t * pr_t - ci_t * pi_t).reshape(g, p, q * c)
    g_im = (-(cr_t * pi_t + ci_t * pr_t)).reshape(g, p, q * c)
    return (_block_diag_pairs(wst_r).astype(BF16), _block_diag_pairs(wst_i).astype(BF16),
            m.astype(BF16), _block_diag_pairs(g_re).astype(BF16), _block_diag_pairs(g_im).astype(BF16),
            pw_r[q].reshape(1, g * p), pw_i[q].reshape(1, g * p))


def _rope_tables(pos):
    half = HEAD_DIM // 2
    inv = 1.0 / (ROPE_THETA ** (jnp.arange(half, dtype=F32) * (2.0 / HEAD_DIM)))
    ang = pos.astype(F32)[:, None] * inv[None, :]
    cos, sin = jnp.cos(ang), jnp.sin(ang)
    return jnp.tile(cos, (1, 4)), jnp.concatenate([-sin, sin, -sin, sin], axis=1)


def _router_weights(gain, w_rg, b_rg, w_re, b_re):
    pad = LANES - N_EXPERTS - N_EXPERT_GROUPS
    w = jnp.concatenate([w_re, w_rg, jnp.zeros((D_MODEL, pad), F32)], axis=1)
    b = jnp.concatenate([b_re, b_rg, jnp.zeros((pad,), F32)]).reshape(1, LANES)
    wh = w.astype(BF16)
    wl = (w - wh.astype(F32)).astype(BF16)
    return gain.reshape(1, -1).astype(F32), wh, wl, b


def kernel(x_prompt, x_sample, cache_k, cache_v, state_ssm_re, state_ssm_im, norm_mix, norm_ffn, norm_final, attn_w_qkv, attn_b_qkv, attn_w_o, attn_b_o, attn_sinks, ssm_a_re, ssm_a_im, ssm_log_dt, ssm_b_re, ssm_b_im, ssm_c_re, ssm_c_im, ssm_d, ssm_w_glu_a, ssm_w_glu_b, moe_w_router_group, moe_b_router_group, moe_w_router_expert, moe_b_router_expert, moe_w_gate, moe_w_up, moe_w_down):
    bsz, seq, _ = x_prompt.shape
    dbs, n_new, _ = x_sample.shape
    rows = cache_k.shape[2]
    tp, ts = bsz * seq, dbs * n_new
    cap = _moe_cap(tp + ts)
    xp = x_prompt.reshape(tp, D_MODEL)
    xs = x_sample.reshape(ts, D_MODEL)

    row1 = lambda v: v.reshape(1, -1).astype(F32)
    routers = [_router_weights(norm_ffn[l], moe_w_router_group[l], moe_b_router_group[l],
                               moe_w_router_expert[l], moe_b_router_expert[l]) for l in range(2)]
    triu = jnp.triu(jnp.ones((TOKEN_TILE, TOKEN_TILE), F32), 1).astype(BF16)
    cnt0 = (jnp.zeros((LANES, LANES), F32), jnp.zeros((8, LANES), F32))

    wqkv = attn_w_qkv[0].astype(BF16)
    bqkv = row1(attn_b_qkv[0])
    wo = attn_w_o[0].astype(BF16)
    bo = row1(attn_b_o[0])
    sinks = attn_sinks[0].astype(F32)
    g_mix0, g_mix1 = row1(norm_mix[0]), row1(norm_mix[1])
    cos_p, sin_p = _rope_tables(jnp.arange(seq, dtype=I32))
    pos_s = jnp.tile(PAST_LEN + jnp.arange(n_new, dtype=I32), dbs)
    cos_s, sin_s = _rope_tables(pos_s)

    qp, kp, vp = _qkv_call(xp, g_mix0, wqkv, bqkv, cos_p, sin_p, BF16)
    qs, ks, vs = _qkv_call(xs, g_mix0, wqkv, bqkv, cos_s, sin_s, F32)
    op = _attn_prompt_call(jnp.repeat(sinks, BLOCK).reshape(1, -1), qp.reshape(bsz, seq, Q_DIM),
                           kp.reshape(bsz, seq, KV_DIM),
                           vp.reshape(bsz, seq, KV_DIM))
    os_, nks, nvs = _attn_sample_call(jnp.repeat(sinks, n_new).reshape(1, -1), qs, ks, vs,
                                      cache_k[0].reshape(dbs, rows, KV_DIM),
                                      cache_v[0].reshape(dbs, rows, KV_DIM), n_new, 8)
    xp1, dest_p, cnt, rows_x = _oproj_call(op.reshape(tp, Q_DIM), xp, wo, bo, routers[0], triu,
                                           cnt0, None, False, cap)
    xs1, dest_s, cnt, rows_x = _oproj_call(os_, xs, wo, bo, routers[0], triu, cnt, rows_x, True, cap)
    rows_y = _expert_call(cnt[1], rows_x, moe_w_gate, moe_w_up, moe_w_down, 0, cap, tp + ts)
    xp2, up = _combine_call(dest_p, xp1, rows_y, g_mix1, True)
    xs2, us = _combine_call(dest_s, xs1, rows_y, g_mix1, True)

    disc = _s5_discretize(ssm_a_re[0], ssm_a_im[0], ssm_log_dt[0], ssm_b_re[0], ssm_b_im[0])
    w_p = _s5_chunk_weights(*disc, ssm_c_re[0], ssm_c_im[0], S5_CHUNK)
    w_s = _s5_chunk_weights(*disc, ssm_c_re[0], ssm_c_im[0], n_new)
    y_p, hpr, hpi = _s5_prompt_call(up.reshape(bsz, seq, D_MODEL), w_p)
    y_p = y_p.reshape(tp, D_MODEL)
    h0r = state_ssm_re[0].reshape(dbs, -1).astype(F32)
    h0i = state_ssm_im[0].reshape(dbs, -1).astype(F32)
    y_s, hsr, hsi = _s5_sample_call(us, h0r, h0i, w_s, n_new)

    wa, wb = ssm_w_glu_a[0].astype(BF16), ssm_w_glu_b[0].astype(BF16)
    d_row = row1(ssm_d[0])
    xp3, dest_p, cnt, rows_x = _glu_call(y_p, xp2, g_mix1, d_row, wa, wb, routers[1], triu,
                                         cnt0, None, False, cap)
    xs3, dest_s, cnt, rows_x = _glu_call(y_s, xs2, g_mix1, d_row, wa, wb, routers[1], triu,
                                         cnt, rows_x, True, cap)
    rows_y = _expert_call(cnt[1], rows_x, moe_w_gate, moe_w_up, moe_w_down, 1, cap, tp + ts)
    g_fin = row1(norm_final)
    (yp,) = _combine_call(dest_p, xp3, rows_y, g_fin, False)
    (ys,) = _combine_call(dest_s, xs3, rows_y, g_fin, False)

    kv5 = lambda a, n: a.reshape(1, n, rows, KV_HEADS, HEAD_DIM)
    st4 = lambda a, n: a.reshape(1, n, SSM_GROUPS, SSM_STATE)
    k_last = kp.reshape(bsz, seq, KV_DIM)[:, seq - WINDOW:]
    v_last = vp.reshape(bsz, seq, KV_DIM)[:, seq - WINDOW:]
    return (yp.reshape(bsz, seq, D_MODEL), ys.reshape(dbs, n_new, D_MODEL),
            k_last.reshape(1, bsz, WINDOW, KV_HEADS, HEAD_DIM), kv5(nks, dbs),
            v_last.reshape(1, bsz, WINDOW, KV_HEADS, HEAD_DIM), kv5(nvs, dbs),
            st4(hpr, bsz), st4(hsr, dbs), st4(hpi, bsz), st4(hsi, dbs))
```

```python
import functools
import math

import jax
import jax.numpy as jnp
from jax import lax
from jax.experimental import pallas as pl
from jax.experimental.pallas import tpu as pltpu

F32 = jnp.float32
BF16 = jnp.bfloat16
I32 = jnp.int32

D_MODEL = 1024
N_HEADS = 16
KV_HEADS = 4
HEAD_DIM = 64
Q_DIM = N_HEADS * HEAD_DIM
KV_DIM = KV_HEADS * HEAD_DIM
QKV_DIM = Q_DIM + 2 * KV_DIM
WINDOW = 128
BLOCK = 128
ROPE_THETA = 10000.0
PAST_LEN = 16384
SSM_GROUP_CH = 16
SSM_GROUPS = D_MODEL // SSM_GROUP_CH
SSM_STATE = 64
N_EXPERT_GROUPS = 4
EXPERTS_PER_GROUP = 8
N_EXPERTS = N_EXPERT_GROUPS * EXPERTS_PER_GROUP
EXPERT_FF = 128
NORM_EPS = 1e-5

LANES = 128
VMEM_LIMIT = 56 * 1024 * 1024
S5_CHUNK = 16
S5_GB = 8
TOKEN_TILE = 512
MOE_TILE = TOKEN_TILE
PAY_X = D_MODEL
PAY_W = PAY_X + LANES


def _cparams(sem):
    return pltpu.CompilerParams(dimension_semantics=sem, vmem_limit_bytes=VMEM_LIMIT)


def _rms(x, g):
    return x * lax.rsqrt(jnp.mean(x * x, axis=-1, keepdims=True) + NORM_EPS) * g


def _dot(a, b):
    return jnp.dot(a, b, preferred_element_type=F32)


def _dot_nt(a, b):
    return lax.dot_general(a, b, (((1,), (1,)), ((), ())), preferred_element_type=F32)


def _sigmoid(x):
    return 1.0 / (1.0 + jnp.exp(-x))


def _qkv_kernel(x_ref, g_ref, w_ref, b_ref, cos_ref, sin_ref, q_ref, k_ref, v_ref):
    xn = _rms(x_ref[...], g_ref[...])
    qkv = _dot(xn.astype(BF16), w_ref[...]) + b_ref[...]
    cos = cos_ref[...]
    sin = sin_ref[...]
    lane = lax.broadcasted_iota(I32, cos.shape, 1)
    first_half = (lane % HEAD_DIM) < (HEAD_DIM // 2)
    n_rot = (Q_DIM + KV_DIM) // LANES
    for c in range(n_rot):
        blk = qkv[:, c * LANES:(c + 1) * LANES]
        partner = jnp.where(first_half,
                            pltpu.roll(blk, LANES - HEAD_DIM // 2, 1),
                            pltpu.roll(blk, HEAD_DIM // 2, 1))
        rot = blk * cos + partner * sin
        if c < Q_DIM // LANES:
            q_ref[:, c * LANES:(c + 1) * LANES] = (rot * (HEAD_DIM ** -0.5)).astype(q_ref.dtype)
        else:
            k_ref[:, c * LANES - Q_DIM:(c + 1) * LANES - Q_DIM] = rot
    v_ref[...] = qkv[:, Q_DIM + KV_DIM:]


def _qkv_call(x2d, gain, w_bf, bias, cos_t, sin_t, q_dtype):
    t = x2d.shape[0]
    tm = TOKEN_TILE
    n_pos = cos_t.shape[0] // tm
    return pl.pallas_call(
        _qkv_kernel,
        grid=(t // tm,),
        in_specs=[
            pl.BlockSpec((tm, D_MODEL), lambda i: (i, 0)),
            pl.BlockSpec((1, D_MODEL), lambda i: (0, 0)),
            pl.BlockSpec((D_MODEL, QKV_DIM), lambda i: (0, 0)),
            pl.BlockSpec((1, QKV_DIM), lambda i: (0, 0)),
            pl.BlockSpec((tm, LANES), lambda i: (i % n_pos, 0)),
            pl.BlockSpec((tm, LANES), lambda i: (i % n_pos, 0)),
        ],
        out_specs=[
            pl.BlockSpec((tm, Q_DIM), lambda i: (i, 0)),
            pl.BlockSpec((tm, KV_DIM), lambda i: (i, 0)),
            pl.BlockSpec((tm, KV_DIM), lambda i: (i, 0)),
        ],
        out_shape=[
            jax.ShapeDtypeStruct((t, Q_DIM), q_dtype),
            jax.ShapeDtypeStruct((t, KV_DIM), F32),
            jax.ShapeDtypeStruct((t, KV_DIM), F32),
        ],
        compiler_params=_cparams(("parallel",)),
        name="qkv_rope",
    )(x2d, gain, w_bf, bias, cos_t, sin_t)


HEAD_SLOTS = tuple(8 * c + 4 * half + i for c in range(2) for i in range(4) for half in range(2))
LOG2E = math.log2(math.e)


def _attn_prompt_kernel(sink_ref, q_ref, kp_ref, kc_ref, vp_ref, vc_ref, o_ref):
    n = pl.program_id(1)
    k2 = jnp.concatenate([kp_ref[0], kc_ref[0]], axis=0).astype(BF16)
    v2 = jnp.concatenate([vp_ref[0], vc_ref[0]], axis=0).astype(BF16)
    key = lax.broadcasted_iota(I32, (2 * BLOCK, 2 * BLOCK), 0)
    qry = lax.broadcasted_iota(I32, (2 * BLOCK, 2 * BLOCK), 1) % BLOCK
    mask = (key >= qry) & (key <= qry + WINDOW) & ((n > 0) | (key >= BLOCK))
    low = lax.broadcasted_iota(I32, (BLOCK, LANES), 1) < HEAD_DIM
    keep_low = low.astype(BF16)
    keep_high = 1 - keep_low
    n_pairs = N_HEADS // 2
    chunk = lambda pr: slice((pr // (n_pairs // 2)) * LANES, (pr // (n_pairs // 2) + 1) * LANES)

    def scores(pr):
        qblk = q_ref[0, :, pr * LANES:(pr + 1) * LANES]
        qz = jnp.concatenate([qblk * keep_low, qblk * keep_high], axis=0)
        s = _dot_nt(k2[:, chunk(pr)], qz) * LOG2E
        return jnp.where(mask, s, -jnp.inf)

    def weights(pr, s):
        sink = sink_ref[:, 2 * pr * BLOCK:(2 * pr + 2) * BLOCK] * LOG2E
        m = jnp.maximum(jnp.max(s, axis=0, keepdims=True), sink)
        p = jnp.exp2(s - m)
        denom = jnp.sum(p, axis=0, keepdims=True) + jnp.exp2(sink - m)
        return (p * (1.0 / denom)).T.astype(BF16)

    def finish(pr, w):
        oz = _dot(w, v2[:, chunk(pr)])
        o_ref[0, :, pr * LANES:(pr + 1) * LANES] = jnp.where(low, oz[:BLOCK], oz[BLOCK:]).astype(o_ref.dtype)

    all_s = [scores(pr) for pr in range(n_pairs)]
    all_w = [weights(pr, s) for pr, s in enumerate(all_s)]
    for pr, w in enumerate(all_w):
        finish(pr, w)


def _attn_prompt_call(sinks, q, k, v):
    b, l, _ = q.shape
    nb = l // BLOCK
    kv_cur = pl.BlockSpec((1, BLOCK, KV_DIM), lambda i, n: (i, n, 0))
    kv_prev = pl.BlockSpec((1, BLOCK, KV_DIM), lambda i, n: (i, jnp.maximum(n - 1, 0), 0))
    return pl.pallas_call(
        _attn_prompt_kernel,
        grid=(b, nb),
        in_specs=[
            pl.BlockSpec((1, N_HEADS * BLOCK), lambda i, n: (0, 0)),
            pl.BlockSpec((1, BLOCK, Q_DIM), lambda i, n: (i, n, 0)),
            kv_prev, kv_cur, kv_prev, kv_cur,
        ],
        out_specs=pl.BlockSpec((1, BLOCK, Q_DIM), lambda i, n: (i, n, 0)),
        out_shape=jax.ShapeDtypeStruct((b, l, Q_DIM), BF16),
        compiler_params=_cparams(("parallel", "parallel")),
        name="attn_prompt",
    )(sinks, q, k, k, v, v)


def _attn_sample_kernel(n_new, seqs, sink_ref, q_ref, kn_ref, vn_ref, ck_ref, cv_ref,
                        o_ref, nk_ref, nv_ref):
    rows = ck_ref.shape[1]
    keys = 2 * rows
    n_cols = N_HEADS * n_new
    key = lax.broadcasted_iota(I32, (keys, n_cols), 0)
    qry = lax.broadcasted_iota(I32, (keys, n_cols), 1) % n_new
    mask = ((key < rows) & (key >= qry)) | ((key >= rows) & (key - rows <= qry))
    low = lax.broadcasted_iota(I32, (n_new, LANES), 1) < HEAD_DIM
    sink = sink_ref[...]
    pad = jnp.zeros((rows - n_new, KV_DIM), F32)

    def body(sb, carry):
        r0 = pl.multiple_of(sb * n_new, n_new)
        k_new = kn_ref[pl.ds(r0, n_new), :]
        v_new = vn_ref[pl.ds(r0, n_new), :]
        k_all = jnp.concatenate([ck_ref[sb], k_new, pad], axis=0).astype(BF16)
        v_all = jnp.concatenate([cv_ref[sb], v_new, pad], axis=0).astype(BF16)
        blocks = []
        for slot in range(N_HEADS):
            pair = q_ref[pl.ds(r0, n_new), (slot // 2) * LANES:(slot // 2 + 1) * LANES]
            blk = jnp.where(low if slot % 2 == 0 else ~low, pair, 0.0)
            zero = jnp.zeros_like(blk)
            blocks.append(jnp.concatenate([blk, zero] if slot < N_HEADS // 2 else [zero, blk], axis=1))
        qz = jnp.concatenate(blocks, axis=0).astype(BF16)
        s = _dot_nt(k_all, qz)
        s = jnp.where(mask, s, -jnp.inf)
        m = jnp.maximum(jnp.max(s, axis=0, keepdims=True), sink)
        p = jnp.exp(s - m)
        denom = jnp.sum(p, axis=0, keepdims=True) + jnp.exp(sink - m)
        w = (p * (1.0 / denom)).T.astype(BF16)
        oz = _dot(w, v_all)
        for pr in range(N_HEADS // 2):
            lanes = slice((pr // (N_HEADS // 4)) * LANES, (pr // (N_HEADS // 4) + 1) * LANES)
            a = oz[2 * pr * n_new:(2 * pr + 1) * n_new, lanes]
            b = oz[(2 * pr + 1) * n_new:(2 * pr + 2) * n_new, lanes]
            o_ref[pl.ds(r0, n_new), pr * LANES:(pr + 1) * LANES] = jnp.where(low, a, b)
        nk_ref[sb, 0:rows - n_new, :] = ck_ref[sb, n_new:rows, :]
        nk_ref[sb, rows - n_new:rows, :] = k_new
        nv_ref[sb, 0:rows - n_new, :] = cv_ref[sb, n_new:rows, :]
        nv_ref[sb, rows - n_new:rows, :] = v_new
        return carry

    lax.fori_loop(0, seqs, body, 0)


def _attn_sample_call(sink_row, q, k_new, v_new, cache_k, cache_v, n_new, seqs):
    b, rows, _ = cache_k.shape
    tok = pl.BlockSpec((seqs * n_new, Q_DIM), lambda i: (i, 0))
    tok_kv = pl.BlockSpec((seqs * n_new, KV_DIM), lambda i: (i, 0))
    cache = pl.BlockSpec((seqs, rows, KV_DIM), lambda i: (i, 0, 0))
    return pl.pallas_call(
        functools.partial(_attn_sample_kernel, n_new, seqs),
        grid=(b // seqs,),
        in_specs=[pl.BlockSpec((1, N_HEADS * n_new), lambda i: (0, 0)), tok, tok_kv, tok_kv, cache, cache],
        out_specs=[tok, cache, cache],
        out_shape=[
            jax.ShapeDtypeStruct((b * n_new, Q_DIM), F32),
            jax.ShapeDtypeStruct((b, rows, KV_DIM), F32),
            jax.ShapeDtypeStruct((b, rows, KV_DIM), F32),
        ],
        compiler_params=_cparams(("parallel",)),
        name="attn_sample",
    )(sink_row, q, k_new, v_new, cache_k, cache_v)


def _moe_cap(t_total):
    return t_total + MOE_TILE


def _route_scatter(final, cap, x1, gn_ref, wrh_ref, wrl_ref, br_ref, triu_ref, cntc_in, cntr_in,
                   dest_ref, cntc_out, cntr_out, xs_hbm, pay, zrows, dest_v, dest_s, cntc, cntr, sem):
    i = pl.program_id(0)
    n = pl.num_programs(0)
    slot = i % 2
    tm = x1.shape[0]

    @pl.when(i == 0)
    def _():
        cntc[...] = cntc_in[...]
        cntr[...] = cntr_in[...]

    xn = _rms(x1, gn_ref[...])
    xh = xn.astype(BF16)
    xl = (xn - xh.astype(F32)).astype(BF16)
    logits = _dot(xh, wrh_ref[...]) + _dot(xl, wrh_ref[...]) + _dot(xh, wrl_ref[...]) + br_ref[...]
    lane = lax.broadcasted_iota(I32, logits.shape, 1).astype(F32)
    big = jnp.float32(4 * LANES)
    neg = -jnp.inf
    gl = jnp.where((lane >= N_EXPERTS) & (lane < N_EXPERTS + N_EXPERT_GROUPS), logits, neg)
    gmax = jnp.max(gl, axis=-1, keepdims=True)
    g_val = 1.0 / jnp.sum(jnp.exp(gl - gmax), axis=-1, keepdims=True)
    g_idx = jnp.min(jnp.where(gl == gmax, lane, big), axis=-1, keepdims=True) - N_EXPERTS
    lo = g_idx * EXPERTS_PER_GROUP
    el = jnp.where((lane >= lo) & (lane < lo + EXPERTS_PER_GROUP), logits, neg)
    e1 = jnp.max(el, axis=-1, keepdims=True)
    i1 = jnp.min(jnp.where(el == e1, lane, big), axis=-1, keepdims=True)
    el2 = jnp.where(lane == i1, neg, el)
    e2 = jnp.max(el2, axis=-1, keepdims=True)
    i2 = jnp.min(jnp.where(el2 == e2, lane, big), axis=-1, keepdims=True)
    t = jnp.exp(e2 - e1)
    w1 = 1.0 / (1.0 + t)
    w2 = t / (1.0 + t)
    wts = g_val * (jnp.where(lane == i1 - lo, w1, 0.0) + jnp.where(lane == i2 - lo, w2, 0.0))

    def wait_slot(s):
        pltpu.make_async_copy(pay.at[s], xs_hbm.at[pl.ds(0, tm), :], sem.at[s]).wait()

    @pl.when(i >= 2)
    def _():
        wait_slot(slot)

    pay[slot, :, :PAY_X] = xn
    pay[slot, :, PAY_X:] = wts

    oh = (lane == g_idx).astype(F32)
    oht = oh.T
    rank = _dot(oht.astype(BF16), triu_ref[...])
    grp = lax.broadcasted_iota(I32, (LANES, 1), 0).astype(F32)
    base = grp * float(cap) + cntc[:, 0:1]
    dest = jnp.sum(oht * (rank + base), axis=0, keepdims=True).astype(I32)
    cntc[...] = cntc[...] + jnp.sum(oht, axis=1, keepdims=True)
    cntr[...] = cntr[...] + jnp.sum(oh, axis=0, keepdims=True)
    dest_ref[0] = dest
    dest_v[...] = dest
    pltpu.sync_copy(dest_v, dest_s)

    for s in range(2):
        @pl.when(slot == s)
        def _():
            for r in range(tm):
                pltpu.make_async_copy(pay.at[s, pl.ds(r, 1), :],
                                      xs_hbm.at[pl.ds(dest_s[0, r], 1), :], sem.at[s]).start()

    @pl.when(i == n - 1)
    def _():
        cntc_out[...] = cntc[...]
        cntr_out[...] = cntr[...]

        @pl.when(n >= 2)
        def _():
            wait_slot(1 - slot)

        wait_slot(slot)
        if final:
            zrows[...] = jnp.zeros_like(zrows)
            dest_v[:, 0:LANES] = cntr[0:1, :].astype(I32)
            pltpu.sync_copy(dest_v, dest_s)
            for g in range(N_EXPERT_GROUPS):
                c_g = dest_s[0, g]
                c_up = ((c_g + 7) // 8) * 8
                for k in range(7):
                    @pl.when(c_g + k < c_up)
                    def _():
                        one = pltpu.make_async_copy(zrows.at[pl.ds(0, 1), :],
                                                    xs_hbm.at[pl.ds(g * cap + c_g + k, 1), :], sem.at[1])
                        one.start()
                        one.wait()
                start = pl.multiple_of(g * cap + c_up, 8)
                pltpu.make_async_copy(zrows, xs_hbm.at[pl.ds(start, MOE_TILE), :], sem.at[0]).start()
            for g in range(N_EXPERT_GROUPS):
                pltpu.make_async_copy(zrows, xs_hbm.at[pl.ds(0, MOE_TILE), :], sem.at[0]).wait()


def _oproj_kernel(has_prev, final, cap, *refs):
    (o_ref, x_ref, wo_ref, bo_ref) = refs[:4]
    rest = refs[4:]
    if has_prev:
        rest = rest[:7] + rest[8:]
    x1 = x_ref[...] + _dot(o_ref[...].astype(BF16), wo_ref[...]) + bo_ref[...]
    x1_ref = rest[7]
    x1_ref[...] = x1
    _route_scatter(final, cap, x1, *rest[:7], *rest[8:])


def _gelu_tanh(x):
    return x * (0.5 * (1.0 + jnp.tanh(math.sqrt(2.0 / math.pi) * (x + 0.044715 * (x * x * x)))))


def _glu_kernel(has_prev, final, cap, *refs):
    (y_ref, x_ref, gm_ref, d_ref, wa_ref, wb_ref) = refs[:6]
    rest = refs[6:]
    if has_prev:
        rest = rest[:7] + rest[8:]
    x = x_ref[...]
    u = _rms(x, gm_ref[...])
    z = _gelu_tanh(y_ref[...] + d_ref[...] * u).astype(BF16)
    x1 = x + _dot(z, wa_ref[...]) * _sigmoid(_dot(z, wb_ref[...]))
    x1_ref = rest[7]
    x1_ref[...] = x1
    _route_scatter(final, cap, x1, *rest[:7], *rest[8:])


def _row_spec(tm, width):
    return pl.BlockSpec((tm, width), lambda i: (i, 0))


def _const_spec(shape):
    return pl.BlockSpec(shape, lambda i: (0,) * len(shape))


def _mixer_call(body, name, lead_args, lead_specs, x2d, router, triu, cnt, xs_prev, final, cap):
    t = x2d.shape[0]
    tm = TOKEN_TILE
    n_tiles = t // tm
    gn, wrh, wrl, br = router
    cntc, cntr = cnt
    has_prev = xs_prev is not None
    in_specs = lead_specs + [
        _const_spec((1, D_MODEL)), _const_spec((D_MODEL, LANES)), _const_spec((D_MODEL, LANES)),
        _const_spec((1, LANES)), _const_spec((tm, tm)), _const_spec((LANES, LANES)), _const_spec((8, LANES)),
    ]
    args = list(lead_args) + [gn, wrh, wrl, br, triu, cntc, cntr]
    aliases = {}
    if has_prev:
        in_specs.append(pl.BlockSpec(memory_space=pl.ANY))
        args.append(xs_prev)
        aliases = {len(args) - 1: 4}
    x1, dest, cntc2, cntr2, xs = pl.pallas_call(
        functools.partial(body, has_prev, final, cap),
        grid=(n_tiles,),
        in_specs=in_specs,
        out_specs=[
            _row_spec(tm, D_MODEL),
            pl.BlockSpec((1, 1, tm), lambda i: (i, 0, 0)),
            _const_spec((LANES, LANES)), _const_spec((8, LANES)),
            pl.BlockSpec(memory_space=pl.ANY),
        ],
        out_shape=[
            jax.ShapeDtypeStruct((t, D_MODEL), F32),
            jax.ShapeDtypeStruct((n_tiles, 1, tm), I32),
            jax.ShapeDtypeStruct((LANES, LANES), F32),
            jax.ShapeDtypeStruct((8, LANES), F32),
            jax.ShapeDtypeStruct((N_EXPERT_GROUPS * cap, PAY_W), F32),
        ],
        scratch_shapes=[
            pltpu.VMEM((2, tm, PAY_W), F32),
            pltpu.VMEM((MOE_TILE, PAY_W), F32),
            pltpu.VMEM((1, tm), I32),
            pltpu.SMEM((1, tm), I32),
            pltpu.VMEM((LANES, LANES), F32),
            pltpu.VMEM((8, LANES), F32),
            pltpu.SemaphoreType.DMA((2,)),
        ],
        input_output_aliases=aliases,
        compiler_params=_cparams(("arbitrary",)),
        name=name,
    )(*args)
    return x1, dest, (cntc2, cntr2), xs


def _oproj_call(o2d, x2d, wo_bf, bo, router, triu, cnt, xs_prev, final, cap):
    tm = TOKEN_TILE
    lead_specs = [_row_spec(tm, Q_DIM), _row_spec(tm, D_MODEL), _const_spec((Q_DIM, D_MODEL)),
                  _const_spec((1, D_MODEL))]
    return _mixer_call(_oproj_kernel, "oproj_route", [o2d, x2d, wo_bf, bo], lead_specs, x2d,
                       router, triu, cnt, xs_prev, final, cap)


def _glu_call(y2d, x2d, gm, d, wa_bf, wb_bf, router, triu, cnt, xs_prev, final, cap):
    tm = TOKEN_TILE
    lead_specs = [_row_spec(tm, D_MODEL), _row_spec(tm, D_MODEL), _const_spec((1, D_MODEL)),
                  _const_spec((1, D_MODEL)), _const_spec((D_MODEL, D_MODEL)),
                  _const_spec((D_MODEL, D_MODEL))]
    return _mixer_call(_glu_kernel, "glu_route", [y2d, x2d, gm, d, wa_bf, wb_bf], lead_specs, x2d,
                       router, triu, cnt, xs_prev, final, cap)


def _expert_kernel(blk_in_ref, blk_out_ref, grp_ref, valid_ref, xs_ref, wg_ref, wu_ref, wd_ref,
                   ys_ref, wgu, wdn, hid):
    j = pl.program_id(0)
    f = EXPERT_FF
    changed = (j == 0) | (grp_ref[j] != grp_ref[jnp.maximum(j - 1, 0)])

    @pl.when(changed)
    def _():
        for e in range(EXPERTS_PER_GROUP):
            wgu[e, :, :f] = wg_ref[e].astype(BF16)
            wgu[e, :, f:] = wu_ref[e].astype(BF16)
            wdn[e * f:(e + 1) * f, :] = wd_ref[e].astype(BF16)

    @pl.when(valid_ref[j] == 1)
    def _():
        x = xs_ref[:, :PAY_X].astype(BF16)
        wts = xs_ref[:, PAY_X:]
        for e in range(EXPERTS_PER_GROUP):
            gu = _dot(x, wgu[e])
            g, u = gu[:, :f], gu[:, f:]
            hid[:, e * f:(e + 1) * f] = ((g * _sigmoid(g)) * u * wts[:, e:e + 1]).astype(BF16)
        ys_ref[...] = _dot(hid[...], wdn[...])

    @pl.when(valid_ref[j] == 0)
    def _():
        ys_ref[...] = jnp.zeros_like(ys_ref)


def _expert_tile_map(cnt_row, cap, n_steps):
    per = cap // MOE_TILE
    c = cnt_row[0, :N_EXPERT_GROUPS].astype(I32)
    tiles = (c + MOE_TILE - 1) // MOE_TILE
    ends = jnp.cumsum(tiles)
    starts = ends - tiles
    total = ends[-1]
    j = jnp.arange(n_steps, dtype=I32)
    jj = jnp.maximum(jnp.minimum(j, total - 1), 0)
    g = jnp.sum((jj[:, None] >= ends[None, :]).astype(I32), axis=1)
    blk_in = g * per + jj - starts[g]
    valid = (j < total).astype(I32)
    blk_out = jnp.where(valid == 1, blk_in, N_EXPERT_GROUPS * per)
    return blk_in, blk_out, g, valid


def _expert_call(cnt_row, xs, w_gate, w_up, w_down, layer, cap, t_total):
    n_steps = t_total // MOE_TILE + N_EXPERT_GROUPS
    blk_in, blk_out, grp, valid = _expert_tile_map(cnt_row, cap, n_steps)
    maps = (blk_in, blk_out, grp + layer * N_EXPERT_GROUPS, valid)
    e, f = EXPERTS_PER_GROUP, EXPERT_FF
    w_gate = w_gate.reshape(-1, D_MODEL, f)
    w_up = w_up.reshape(-1, D_MODEL, f)
    w_down = w_down.reshape(-1, f, D_MODEL)
    grid_spec = pltpu.PrefetchScalarGridSpec(
        num_scalar_prefetch=4,
        grid=(n_steps,),
        in_specs=[
            pl.BlockSpec((MOE_TILE, PAY_W), lambda j, bi, bo, gr, va: (bi[j], 0)),
            pl.BlockSpec((e, D_MODEL, f), lambda j, bi, bo, gr, va: (gr[j], 0, 0)),
            pl.BlockSpec((e, D_MODEL, f), lambda j, bi, bo, gr, va: (gr[j], 0, 0)),
            pl.BlockSpec((e, f, D_MODEL), lambda j, bi, bo, gr, va: (gr[j], 0, 0)),
        ],
        out_specs=pl.BlockSpec((MOE_TILE, D_MODEL), lambda j, bi, bo, gr, va: (bo[j], 0)),
        scratch_shapes=[
            pltpu.VMEM((e, D_MODEL, 2 * f), BF16),
            pltpu.VMEM((e * f, D_MODEL), BF16),
            pltpu.VMEM((MOE_TILE, e * f), BF16),
        ],
    )
    return pl.pallas_call(
        _expert_kernel,
        grid_spec=grid_spec,
        out_shape=jax.ShapeDtypeStruct((N_EXPERT_GROUPS * cap + MOE_TILE, D_MODEL), F32),
        compiler_params=_cparams(("arbitrary",)),
        name="moe_experts",
    )(*maps, xs, w_gate, w_up, w_down)


def _combine_kernel(emit_x, dest_ref, x_ref, gnext_ref, ys_hbm, *rest):
    outs, (ybuf, sem) = rest[:-2], rest[-2:]
    i = pl.program_id(0)
    n_tiles = pl.num_programs(0) - 1
    slot = i % 2
    tm = x_ref.shape[0]

    for s in range(2):
        @pl.when((i < n_tiles) & (slot == s))
        def _():
            for r in range(tm):
                pltpu.make_async_copy(ys_hbm.at[pl.ds(dest_ref[0, 0, r], 1), :],
                                      ybuf.at[s, pl.ds(r, 1), :], sem.at[s]).start()

    @pl.when(i >= 1)
    def _():
        prev = 1 - slot
        pltpu.make_async_copy(ys_hbm.at[pl.ds(0, tm), :], ybuf.at[prev], sem.at[prev]).wait()
        x2 = x_ref[...] + ybuf[prev]
        normed = _rms(x2, gnext_ref[...])
        if emit_x:
            outs[0][...] = x2
            outs[1][...] = normed
        else:
            outs[0][...] = normed


def _combine_call(dest, x2d, ys, gnext, emit_x):
    t = x2d.shape[0]
    tm = TOKEN_TILE
    n_tiles = t // tm
    n_out = 2 if emit_x else 1
    done = lambda i: (jnp.maximum(i - 1, 0), 0)
    return pl.pallas_call(
        functools.partial(_combine_kernel, emit_x),
        grid=(n_tiles + 1,),
        in_specs=[
            pl.BlockSpec((1, 1, tm), lambda i: (jnp.minimum(i, n_tiles - 1), 0, 0),
                         memory_space=pltpu.SMEM),
            pl.BlockSpec((tm, D_MODEL), done), _const_spec((1, D_MODEL)),
            pl.BlockSpec(memory_space=pl.ANY),
        ],
        out_specs=[pl.BlockSpec((tm, D_MODEL), done)] * n_out,
        out_shape=[jax.ShapeDtypeStruct((t, D_MODEL), F32)] * n_out,
        scratch_shapes=[pltpu.VMEM((2, tm, D_MODEL), F32), pltpu.SemaphoreType.DMA((2,))],
        compiler_params=_cparams(("arbitrary",)),
        name="moe_combine",
    )(dest, x2d, gnext, ys)


def _s5_state_in(u_ref_val, wre_ref, wim_ref, sre_ref, sim_ref, pair_w):
    for m in range(S5_GB // 2):
        up = u_ref_val(m * pair_w, pair_w)
        sre_ref[:, m * LANES:(m + 1) * LANES] = _dot(up, wre_ref[m])
        sim_ref[:, m * LANES:(m + 1) * LANES] = _dot(up, wim_ref[m])


def _s5_outputs(u_ref_val, hre, him, m_ref, gre_ref, gim_ref, y_store, pair_w):
    gw = pair_w // 2
    for m in range(S5_GB // 2):
        hr = hre(m).astype(BF16)
        hi = him(m).astype(BF16)
        y = _dot(hr, gre_ref[m]) + _dot(hi, gim_ref[m])
        y0 = y[:, :gw] + _dot(u_ref_val(m * pair_w, gw), m_ref[2 * m])
        y1 = y[:, gw:] + _dot(u_ref_val(m * pair_w + gw, gw), m_ref[2 * m + 1])
        y_store(m * pair_w, gw, y0)
        y_store(m * pair_w + gw, gw, y1)


def _s5_flatten(load_rows, q, n, ut, uflat):
    gc = SSM_GROUP_CH
    qc = q * gc
    for s in range(q):
        ut[:, s * gc:(s + 1) * gc, :] = load_rows(s).T.reshape(S5_GB, gc, n)
    for g in range(S5_GB):
        uflat[:, g * qc:(g + 1) * qc] = ut[g].T.astype(BF16)


def _s5_unflatten(yflat, yt, store_rows, q, n):
    gc = SSM_GROUP_CH
    qc = q * gc
    for g in range(S5_GB):
        yt[g] = yflat[:, g * qc:(g + 1) * qc].T
    for t in range(q):
        store_rows(t, yt[:, t * gc:(t + 1) * gc, :].reshape(S5_GB * gc, n).T)


def _s5_prompt_kernel(u_ref, wre_ref, wim_ref, m_ref, gre_ref, gim_ref, aqr_ref, aqi_ref,
                      y_ref, her_ref, hei_ref, ut, uflat, sre, sim, hre, him, yflat, yt):
    pair_w = 2 * S5_CHUNK * SSM_GROUP_CH
    n_chunks = u_ref.shape[1] // S5_CHUNK
    _s5_flatten(lambda s: u_ref[0, pl.ds(s, n_chunks, stride=S5_CHUNK), :],
                S5_CHUNK, n_chunks, ut, uflat)
    u_val = lambda off, w: uflat[:, off:off + w]
    _s5_state_in(u_val, wre_ref, wim_ref, sre, sim, pair_w)
    ar = aqr_ref[...]
    ai = aqi_ref[...]

    def step(n, carry):
        hr, hi = carry
        hre[pl.ds(n, 1), :] = hr
        him[pl.ds(n, 1), :] = hi
        sr = sre[pl.ds(n, 1), :]
        si = sim[pl.ds(n, 1), :]
        return ar * hr - ai * hi + sr, ar * hi + ai * hr + si

    zero = jnp.zeros(ar.shape, F32)
    hr, hi = lax.fori_loop(0, n_chunks, step, (zero, zero))
    her_ref[0] = hr
    hei_ref[0] = hi

    def y_store(off, w, val):
        yflat[:, off:off + w] = val

    _s5_outputs(u_val, lambda m: hre[:, m * LANES:(m + 1) * LANES],
                lambda m: him[:, m * LANES:(m + 1) * LANES],
                m_ref, gre_ref, gim_ref, y_store, pair_w)

    def store_rows(t, val):
        y_ref[0, pl.ds(t, n_chunks, stride=S5_CHUNK), :] = val

    _s5_unflatten(yflat, yt, store_rows, S5_CHUNK, n_chunks)


def _s5_sample_kernel(n_new, u_ref, h0r_ref, h0i_ref, wre_ref, wim_ref, m_ref, gre_ref, gim_ref,
                      aqr_ref, aqi_ref, y_ref, hnr_ref, hni_ref, ut, uflat, sre, sim, yflat, yt):
    pair_w = 2 * n_new * SSM_GROUP_CH
    seqs = h0r_ref.shape[0]
    _s5_flatten(lambda s: u_ref[pl.ds(s, seqs, stride=n_new), :], n_new, seqs, ut, uflat)
    u_val = lambda off, w: uflat[:, off:off + w]
    _s5_state_in(u_val, wre_ref, wim_ref, sre, sim, pair_w)
    ar = aqr_ref[...]
    ai = aqi_ref[...]
    h0r = h0r_ref[...]
    h0i = h0i_ref[...]
    hnr_ref[...] = ar * h0r - ai * h0i + sre[...]
    hni_ref[...] = ar * h0i + ai * h0r + sim[...]

    def y_store(off, w, val):
        yflat[:, off:off + w] = val

    _s5_outputs(u_val, lambda m: h0r_ref[:, m * LANES:(m + 1) * LANES],
                lambda m: h0i_ref[:, m * LANES:(m + 1) * LANES],
                m_ref, gre_ref, gim_ref, y_store, pair_w)

    def store_rows(t, val):
        y_ref[pl.ds(t, seqs, stride=n_new), :] = val

    _s5_unflatten(yflat, yt, store_rows, n_new, seqs)


def _s5_weight_specs(q, idx):
    qc = q * SSM_GROUP_CH
    np_ = S5_GB // 2
    st = S5_GB * SSM_STATE
    return [
        pl.BlockSpec((np_, 2 * qc, LANES), lambda *a: (idx(*a), 0, 0)),
        pl.BlockSpec((np_, 2 * qc, LANES), lambda *a: (idx(*a), 0, 0)),
        pl.BlockSpec((S5_GB, qc, qc), lambda *a: (idx(*a), 0, 0)),
        pl.BlockSpec((np_, LANES, 2 * qc), lambda *a: (idx(*a), 0, 0)),
        pl.BlockSpec((np_, LANES, 2 * qc), lambda *a: (idx(*a), 0, 0)),
        pl.BlockSpec((1, st), lambda *a: (0, idx(*a))),
        pl.BlockSpec((1, st), lambda *a: (0, idx(*a))),
    ]


def _s5_prompt_call(u, w):
    b, seq, _ = u.shape
    n_chunks = seq // S5_CHUNK
    gbl = S5_GB * SSM_GROUP_CH
    qc = S5_CHUNK * SSM_GROUP_CH
    st = S5_GB * SSM_STATE
    n_gb = SSM_GROUPS // S5_GB
    gb_of = lambda g, i: g
    tok = pl.BlockSpec((1, seq, gbl), lambda g, i: (i, 0, g))
    return pl.pallas_call(
        _s5_prompt_kernel,
        grid=(n_gb, b),
        in_specs=[tok] + _s5_weight_specs(S5_CHUNK, gb_of),
        out_specs=[
            tok,
            pl.BlockSpec((1, 1, st), lambda g, i: (i, 0, g)),
            pl.BlockSpec((1, 1, st), lambda g, i: (i, 0, g)),
        ],
        out_shape=[
            jax.ShapeDtypeStruct((b, seq, D_MODEL), F32),
            jax.ShapeDtypeStruct((b, 1, SSM_GROUPS * SSM_STATE), F32),
            jax.ShapeDtypeStruct((b, 1, SSM_GROUPS * SSM_STATE), F32),
        ],
        scratch_shapes=[
            pltpu.VMEM((S5_GB, qc, n_chunks), F32),
            pltpu.VMEM((n_chunks, S5_GB * qc), BF16),
            pltpu.VMEM((n_chunks, st), F32), pltpu.VMEM((n_chunks, st), F32),
            pltpu.VMEM((n_chunks, st), F32), pltpu.VMEM((n_chunks, st), F32),
            pltpu.VMEM((n_chunks, S5_GB * qc), F32),
            pltpu.VMEM((S5_GB, qc, n_chunks), F32),
        ],
        compiler_params=_cparams(("parallel", "parallel")),
        name="s5_prompt",
    )(u, *w)


def _s5_sample_call(u2d, h0r, h0i, w, n_new):
    t = u2d.shape[0]
    b = t // n_new
    gbl = S5_GB * SSM_GROUP_CH
    qc = n_new * SSM_GROUP_CH
    st = S5_GB * SSM_STATE
    n_gb = SSM_GROUPS // S5_GB
    gb_of = lambda g: g
    state = pl.BlockSpec((b, st), lambda g: (0, g))
    tok = pl.BlockSpec((t, gbl), lambda g: (0, g))
    return pl.pallas_call(
        functools.partial(_s5_sample_kernel, n_new),
        grid=(n_gb,),
        in_specs=[tok, state, state] + _s5_weight_specs(n_new, gb_of),
        out_specs=[tok, state, state],
        out_shape=[
            jax.ShapeDtypeStruct((t, D_MODEL), F32),
            jax.ShapeDtypeStruct((b, SSM_GROUPS * SSM_STATE), F32),
            jax.ShapeDtypeStruct((b, SSM_GROUPS * SSM_STATE), F32),
        ],
        scratch_shapes=[
            pltpu.VMEM((S5_GB, qc, b), F32),
            pltpu.VMEM((b, S5_GB * qc), BF16),
            pltpu.VMEM((b, st), F32), pltpu.VMEM((b, st), F32),
            pltpu.VMEM((b, S5_GB * qc), F32),
            pltpu.VMEM((S5_GB, qc, b), F32),
        ],
        compiler_params=_cparams(("parallel",)),
        name="s5_sample",
    )(u2d, h0r, h0i, *w)


def _block_diag_pairs(w):
    g, r, n = w.shape
    w2 = w.reshape(g // 2, 2, r, n)
    z = jnp.zeros((g // 2, r, n), w.dtype)
    top = jnp.concatenate([w2[:, 0], z], axis=2)
    bot = jnp.concatenate([z, w2[:, 1]], axis=2)
    return jnp.concatenate([top, bot], axis=1)


def _s5_discretize(a_re, a_im, log_dt, b_re, b_im):
    delta = jnp.exp(log_dt.astype(F32))[:, None]
    lr, li = a_re.astype(F32), a_im.astype(F32)
    mag = jnp.exp(delta * lr)
    abar_r = mag * jnp.cos(delta * li)
    abar_i = mag * jnp.sin(delta * li)
    nr, ni = abar_r - 1.0, abar_i
    den = lr * lr + li * li
    coef_r = ((nr * lr + ni * li) / den)[..., None]
    coef_i = ((ni * lr - nr * li) / den)[..., None]
    br, bi = b_re.astype(F32), b_im.astype(F32)
    return delta * lr, delta * li, coef_r * br - coef_i * bi, coef_r * bi + coef_i * br


def _s5_chunk_weights(log_mag, phase, bbar_r, bbar_i, c_re, c_im, q):
    g, p = log_mag.shape
    c = SSM_GROUP_CH
    hi = lax.Precision.HIGHEST
    k = jnp.arange(q + 1, dtype=F32)[:, None, None]
    mag = jnp.exp(k * log_mag[None])
    pw_r, pw_i = mag * jnp.cos(k * phase[None]), mag * jnp.sin(k * phase[None])
    rev_r, rev_i = pw_r[:q][::-1][..., None], pw_i[:q][::-1][..., None]
    wst_r = (rev_r * bbar_r[None] - rev_i * bbar_i[None]).transpose(1, 0, 3, 2).reshape(g, q * c, p)
    wst_i = (rev_r * bbar_i[None] + rev_i * bbar_r[None]).transpose(1, 0, 3, 2).reshape(g, q * c, p)
    x_r = pw_r[:q, :, :, None] * bbar_r[None] - pw_i[:q, :, :, None] * bbar_i[None]
    x_i = pw_r[:q, :, :, None] * bbar_i[None] + pw_i[:q, :, :, None] * bbar_r[None]
    cr, ci = c_re.astype(F32), c_im.astype(F32)
    ker = (jnp.einsum('gcp,tgpd->gtdc', cr, x_r, precision=hi)
           - jnp.einsum('gcp,tgpd->gtdc', ci, x_i, precision=hi))
    idx = jnp.arange(q)
    place = (idx[None, :, None] - idx[:, None, None] == idx[None, None, :]).astype(BF16)
    m = jnp.einsum('stk,gkdc->gsdtc', place, ker.astype(BF16),
                   preferred_element_type=F32).reshape(g, q * c, q * c)
    cr_t, ci_t = cr.transpose(0, 2, 1)[:, :, None, :], ci.transpose(0, 2, 1)[:, :, None, :]
    pr_t = pw_r[1:q + 1].transpose(1, 2, 0)[..., None]
    pi_t = pw_i[1:q + 1].transpose(1, 2, 0)[..., None]
    g_re = (cr_t * pr_t - ci_t * pi_t).reshape(g, p, q * c)
    g_im = (-(cr_t * pi_t + ci_t * pr_t)).reshape(g, p, q * c)
    return (_block_diag_pairs(wst_r).astype(BF16), _block_diag_pairs(wst_i).astype(BF16),
            m.astype(BF16), _block_diag_pairs(g_re).astype(BF16), _block_diag_pairs(g_im).astype(BF16),
            pw_r[q].reshape(1, g * p), pw_i[q].reshape(1, g * p))


def _rope_tables(pos):
    half = HEAD_DIM // 2
    inv = 1.0 / (ROPE_THETA ** (jnp.arange(half, dtype=F32) * (2.0 / HEAD_DIM)))
    ang = pos.astype(F32)[:, None] * inv[None, :]
    cos, sin = jnp.cos(ang), jnp.sin(ang)
    return jnp.tile(cos, (1, 4)), jnp.concatenate([-sin, sin, -sin, sin], axis=1)


def _router_weights(gain, w_rg, b_rg, w_re, b_re):
    pad = LANES - N_EXPERTS - N_EXPERT_GROUPS
    w = jnp.concatenate([w_re, w_rg, jnp.zeros((D_MODEL, pad), F32)], axis=1)
    b = jnp.concatenate([b_re, b_rg, jnp.zeros((pad,), F32)]).reshape(1, LANES)
    wh = w.astype(BF16)
    wl = (w - wh.astype(F32)).astype(BF16)
    return gain.reshape(1, -1).astype(F32), wh, wl, b


def kernel(x_prompt, x_sample, cache_k, cache_v, state_ssm_re, state_ssm_im, norm_mix, norm_ffn, norm_final, attn_w_qkv, attn_b_qkv, attn_w_o, attn_b_o, attn_sinks, ssm_a_re, ssm_a_im, ssm_log_dt, ssm_b_re, ssm_b_im, ssm_c_re, ssm_c_im, ssm_d, ssm_w_glu_a, ssm_w_glu_b, moe_w_router_group, moe_b_router_group, moe_w_router_expert, moe_b_router_expert, moe_w_gate, moe_w_up, moe_w_down):
    bsz, seq, _ = x_prompt.shape
    dbs, n_new, _ = x_sample.shape
    rows = cache_k.shape[2]
    tp, ts = bsz * seq, dbs * n_new
    cap = _moe_cap(tp + ts)
    xp = x_prompt.reshape(tp, D_MODEL)
    xs = x_sample.reshape(ts, D_MODEL)

    row1 = lambda v: v.reshape(1, -1).astype(F32)
    routers = [_router_weights(norm_ffn[l], moe_w_router_group[l], moe_b_router_group[l],
                               moe_w_router_expert[l], moe_b_router_expert[l]) for l in range(2)]
    triu = jnp.triu(jnp.ones((TOKEN_TILE, TOKEN_TILE), F32), 1).astype(BF16)
    cnt0 = (jnp.zeros((LANES, LANES), F32), jnp.zeros((8, LANES), F32))

    slots = jnp.asarray(HEAD_SLOTS, dtype=I32)

    def q_slots(w):
        lead = w.shape[:-1]
        qh = w[..., :Q_DIM].reshape(*lead, N_HEADS, HEAD_DIM)[..., slots, :].reshape(*lead, Q_DIM)
        return jnp.concatenate([qh, w[..., Q_DIM:]], axis=-1)

    wqkv = q_slots(attn_w_qkv[0]).astype(BF16)
    bqkv = row1(q_slots(attn_b_qkv[0]))
    wo = attn_w_o[0].reshape(N_HEADS, HEAD_DIM, D_MODEL)[slots].reshape(Q_DIM, D_MODEL).astype(BF16)
    bo = row1(attn_b_o[0])
    sinks = attn_sinks[0].astype(F32)[slots]
    g_mix0, g_mix1 = row1(norm_mix[0]), row1(norm_mix[1])
    cos_p, sin_p = _rope_tables(jnp.arange(seq, dtype=I32))
    pos_s = jnp.tile(PAST_LEN + jnp.arange(n_new, dtype=I32), dbs)
    cos_s, sin_s = _rope_tables(pos_s)

    qp, kp, vp = _qkv_call(xp, g_mix0, wqkv, bqkv, cos_p, sin_p, BF16)
    qs, ks, vs = _qkv_call(xs, g_mix0, wqkv, bqkv, cos_s, sin_s, F32)
    op = _attn_prompt_call(jnp.repeat(sinks, BLOCK).reshape(1, -1), qp.reshape(bsz, seq, Q_DIM),
                           kp.reshape(bsz, seq, KV_DIM),
                           vp.reshape(bsz, seq, KV_DIM))
    os_, nks, nvs = _attn_sample_call(jnp.repeat(sinks, n_new).reshape(1, -1), qs, ks, vs,
                                      cache_k[0].reshape(dbs, rows, KV_DIM),
                                      cache_v[0].reshape(dbs, rows, KV_DIM), n_new, 8)
    xp1, dest_p, cnt, rows_x = _oproj_call(op.reshape(tp, Q_DIM), xp, wo, bo, routers[0], triu,
                                           cnt0, None, False, cap)
    xs1, dest_s, cnt, rows_x = _oproj_call(os_, xs, wo, bo, routers[0], triu, cnt, rows_x, True, cap)
    rows_y = _expert_call(cnt[1], rows_x, moe_w_gate, moe_w_up, moe_w_down, 0, cap, tp + ts)
    xp2, up = _combine_call(dest_p, xp1, rows_y, g_mix1, True)
    xs2, us = _combine_call(dest_s, xs1, rows_y, g_mix1, True)

    disc = _s5_discretize(ssm_a_re[0], ssm_a_im[0], ssm_log_dt[0], ssm_b_re[0], ssm_b_im[0])
    w_p = _s5_chunk_weights(*disc, ssm_c_re[0], ssm_c_im[0], S5_CHUNK)
    w_s = _s5_chunk_weights(*disc, ssm_c_re[0], ssm_c_im[0], n_new)
    y_p, hpr, hpi = _s5_prompt_call(up.reshape(bsz, seq, D_MODEL), w_p)
    y_p = y_p.reshape(tp, D_MODEL)
    h0r = state_ssm_re[0].reshape(dbs, -1).astype(F32)
    h0i = state_ssm_im[0].reshape(dbs, -1).astype(F32)
    y_s, hsr, hsi = _s5_sample_call(us, h0r, h0i, w_s, n_new)

    wa, wb = ssm_w_glu_a[0].astype(BF16), ssm_w_glu_b[0].astype(BF16)
    d_row = row1(ssm_d[0])
    xp3, dest_p, cnt, rows_x = _glu_call(y_p, xp2, g_mix1, d_row, wa, wb, routers[1], triu,
                                         cnt0, None, False, cap)
    xs3, dest_s, cnt, rows_x = _glu_call(y_s, xs2, g_mix1, d_row, wa, wb, routers[1], triu,
                                         cnt, rows_x, True, cap)
    rows_y = _expert_call(cnt[1], rows_x, moe_w_gate, moe_w_up, moe_w_down, 1, cap, tp + ts)
    g_fin = row1(norm_final)
    (yp,) = _combine_call(dest_p, xp3, rows_y, g_fin, False)
    (ys,) = _combine_call(dest_s, xs3, rows_y, g_fin, False)

    kv5 = lambda a, n: a.reshape(1, n, rows, KV_HEADS, HEAD_DIM)
    st4 = lambda a, n: a.reshape(1, n, SSM_GROUPS, SSM_STATE)
    k_last = kp.reshape(bsz, seq, KV_DIM)[:, seq - WINDOW:]
    v_last = vp.reshape(bsz, seq, KV_DIM)[:, seq - WINDOW:]
    return (yp.reshape(bsz, seq, D_MODEL), ys.reshape(dbs, n_new, D_MODEL),
            k_last.reshape(1, bsz, WINDOW, KV_HEADS, HEAD_DIM), kv5(nks, dbs),
            v_last.reshape(1, bsz, WINDOW, KV_HEADS, HEAD_DIM), kv5(nvs, dbs),
            st4(hpr, bsz), st4(hsr, dbs), st4(hpi, bsz), st4(hsi, dbs))
```

```python
import functools
import math

import jax
import jax.numpy as jnp
from jax import lax
from jax.experimental import pallas as pl
from jax.experimental.pallas import tpu as pltpu

F32 = jnp.float32
BF16 = jnp.bfloat16
I32 = jnp.int32

D_MODEL = 1024
N_HEADS = 16
KV_HEADS = 4
HEAD_DIM = 64
Q_DIM = N_HEADS * HEAD_DIM
KV_DIM = KV_HEADS * HEAD_DIM
QKV_DIM = Q_DIM + 2 * KV_DIM
WINDOW = 128
BLOCK = 128
ROPE_THETA = 10000.0
PAST_LEN = 16384
SSM_GROUP_CH = 16
SSM_GROUPS = D_MODEL // SSM_GROUP_CH
SSM_STATE = 64
N_EXPERT_GROUPS = 4
EXPERTS_PER_GROUP = 8
N_EXPERTS = N_EXPERT_GROUPS * EXPERTS_PER_GROUP
EXPERT_FF = 128
NORM_EPS = 1e-5

LANES = 128
VMEM_LIMIT = 56 * 1024 * 1024
S5_CHUNK = 16
S5_GB = 8
TOKEN_TILE = 512
MOE_TILE = TOKEN_TILE
PAY_X = D_MODEL
PAY_W = PAY_X + LANES


def _cparams(sem):
    return pltpu.CompilerParams(dimension_semantics=sem, vmem_limit_bytes=VMEM_LIMIT)


def _rms(x, g):
    return x * lax.rsqrt(jnp.mean(x * x, axis=-1, keepdims=True) + NORM_EPS) * g


def _dot(a, b):
    return jnp.dot(a, b, preferred_element_type=F32)


def _dot_nt(a, b):
    return lax.dot_general(a, b, (((1,), (1,)), ((), ())), preferred_element_type=F32)


def _sigmoid(x):
    return 1.0 / (1.0 + jnp.exp(-x))


def _qkv_kernel(x_ref, g_ref, w_ref, b_ref, cos_ref, sin_ref, q_ref, k_ref, v_ref):
    xn = _rms(x_ref[...], g_ref[...])
    qkv = _dot(xn.astype(BF16), w_ref[...]) + b_ref[...]
    cos = cos_ref[...]
    sin = sin_ref[...]
    lane = lax.broadcasted_iota(I32, cos.shape, 1)
    first_half = (lane % HEAD_DIM) < (HEAD_DIM // 2)
    n_rot = (Q_DIM + KV_DIM) // LANES
    for c in range(n_rot):
        blk = qkv[:, c * LANES:(c + 1) * LANES]
        partner = jnp.where(first_half,
                            pltpu.roll(blk, LANES - HEAD_DIM // 2, 1),
                            pltpu.roll(blk, HEAD_DIM // 2, 1))
        rot = blk * cos + partner * sin
        if c < Q_DIM // LANES:
            q_ref[:, c * LANES:(c + 1) * LANES] = (rot * (HEAD_DIM ** -0.5)).astype(q_ref.dtype)
        else:
            k_ref[:, c * LANES - Q_DIM:(c + 1) * LANES - Q_DIM] = rot
    v_ref[...] = qkv[:, Q_DIM + KV_DIM:]


def _qkv_call(x2d, gain, w_bf, bias, cos_t, sin_t, pos_rows, q_dtype):
    t = x2d.shape[0]
    tm = TOKEN_TILE
    first, n_pos = pos_rows[0] // tm, pos_rows[1] // tm
    return pl.pallas_call(
        _qkv_kernel,
        grid=(t // tm,),
        in_specs=[
            pl.BlockSpec((tm, D_MODEL), lambda i: (i, 0)),
            pl.BlockSpec((1, D_MODEL), lambda i: (0, 0)),
            pl.BlockSpec((D_MODEL, QKV_DIM), lambda i: (0, 0)),
            pl.BlockSpec((1, QKV_DIM), lambda i: (0, 0)),
            pl.BlockSpec((tm, LANES), lambda i: (first + i % n_pos, 0)),
            pl.BlockSpec((tm, LANES), lambda i: (first + i % n_pos, 0)),
        ],
        out_specs=[
            pl.BlockSpec((tm, Q_DIM), lambda i: (i, 0)),
            pl.BlockSpec((tm, KV_DIM), lambda i: (i, 0)),
            pl.BlockSpec((tm, KV_DIM), lambda i: (i, 0)),
        ],
        out_shape=[
            jax.ShapeDtypeStruct((t, Q_DIM), q_dtype),
            jax.ShapeDtypeStruct((t, KV_DIM), F32),
            jax.ShapeDtypeStruct((t, KV_DIM), F32),
        ],
        compiler_params=_cparams(("parallel",)),
        name="qkv_rope",
    )(x2d, gain, w_bf, bias, cos_t, sin_t)


HEAD_SLOTS = tuple(8 * c + 4 * half + i for c in range(2) for i in range(4) for half in range(2))
LOG2E = math.log2(math.e)


def _attn_prompt_kernel(sink_ref, q_ref, kp_ref, kc_ref, vp_ref, vc_ref, o_ref):
    n = pl.program_id(1)
    k2 = jnp.concatenate([kp_ref[0], kc_ref[0]], axis=0).astype(BF16)
    v2 = jnp.concatenate([vp_ref[0], vc_ref[0]], axis=0).astype(BF16)
    key = lax.broadcasted_iota(I32, (2 * BLOCK, 2 * BLOCK), 0)
    qry = lax.broadcasted_iota(I32, (2 * BLOCK, 2 * BLOCK), 1) % BLOCK
    mask = (key >= qry) & (key <= qry + WINDOW) & ((n > 0) | (key >= BLOCK))
    low = lax.broadcasted_iota(I32, (BLOCK, LANES), 1) < HEAD_DIM
    keep_low = low.astype(BF16)
    keep_high = 1 - keep_low
    n_pairs = N_HEADS // 2
    chunk = lambda pr: slice((pr // (n_pairs // 2)) * LANES, (pr // (n_pairs // 2) + 1) * LANES)

    def scores(pr):
        qblk = q_ref[0, :, pr * LANES:(pr + 1) * LANES]
        qz = jnp.concatenate([qblk * keep_low, qblk * keep_high], axis=0)
        s = _dot_nt(k2[:, chunk(pr)], qz) * LOG2E
        return jnp.where(mask, s, -jnp.inf)

    def weights(pr, s):
        sink = sink_ref[:, 2 * pr * BLOCK:(2 * pr + 2) * BLOCK] * LOG2E
        m = jnp.maximum(jnp.max(s, axis=0, keepdims=True), sink)
        p = jnp.exp2(s - m)
        denom = jnp.sum(p, axis=0, keepdims=True) + jnp.exp2(sink - m)
        return (p * (1.0 / denom)).T.astype(BF16)

    def finish(pr, w):
        oz = _dot(w, v2[:, chunk(pr)])
        o_ref[0, :, pr * LANES:(pr + 1) * LANES] = jnp.where(low, oz[:BLOCK], oz[BLOCK:]).astype(o_ref.dtype)

    all_s = [scores(pr) for pr in range(n_pairs)]
    all_w = [weights(pr, s) for pr, s in enumerate(all_s)]
    for pr, w in enumerate(all_w):
        finish(pr, w)


def _attn_prompt_call(sinks, q, k, v):
    b, l, _ = q.shape
    nb = l // BLOCK
    kv_cur = pl.BlockSpec((1, BLOCK, KV_DIM), lambda i, n: (i, n, 0))
    kv_prev = pl.BlockSpec((1, BLOCK, KV_DIM), lambda i, n: (i, jnp.maximum(n - 1, 0), 0))
    return pl.pallas_call(
        _attn_prompt_kernel,
        grid=(b, nb),
        in_specs=[
            pl.BlockSpec((1, N_HEADS * BLOCK), lambda i, n: (0, 0)),
            pl.BlockSpec((1, BLOCK, Q_DIM), lambda i, n: (i, n, 0)),
            kv_prev, kv_cur, kv_prev, kv_cur,
        ],
        out_specs=pl.BlockSpec((1, BLOCK, Q_DIM), lambda i, n: (i, n, 0)),
        out_shape=jax.ShapeDtypeStruct((b, l, Q_DIM), BF16),
        compiler_params=_cparams(("parallel", "parallel")),
        name="attn_prompt",
    )(sinks, q, k, k, v, v)


def _attn_sample_kernel(n_new, seqs, sink_ref, q_ref, kn_ref, vn_ref, ck_ref, cv_ref,
                        o_ref, nk_ref, nv_ref):
    rows = ck_ref.shape[1]
    keys = 2 * rows
    n_cols = N_HEADS * n_new
    key = lax.broadcasted_iota(I32, (keys, n_cols), 0)
    qry = lax.broadcasted_iota(I32, (keys, n_cols), 1) % n_new
    mask = ((key < rows) & (key >= qry)) | ((key >= rows) & (key - rows <= qry))
    low = lax.broadcasted_iota(I32, (n_new, LANES), 1) < HEAD_DIM
    sink = sink_ref[...]
    pad = jnp.zeros((rows - n_new, KV_DIM), F32)

    def body(sb, carry):
        r0 = pl.multiple_of(sb * n_new, n_new)
        k_new = kn_ref[pl.ds(r0, n_new), :]
        v_new = vn_ref[pl.ds(r0, n_new), :]
        k_all = jnp.concatenate([ck_ref[sb], k_new, pad], axis=0).astype(BF16)
        v_all = jnp.concatenate([cv_ref[sb], v_new, pad], axis=0).astype(BF16)
        blocks = []
        for slot in range(N_HEADS):
            pair = q_ref[pl.ds(r0, n_new), (slot // 2) * LANES:(slot // 2 + 1) * LANES]
            blk = jnp.where(low if slot % 2 == 0 else ~low, pair, 0.0)
            zero = jnp.zeros_like(blk)
            blocks.append(jnp.concatenate([blk, zero] if slot < N_HEADS // 2 else [zero, blk], axis=1))
        qz = jnp.concatenate(blocks, axis=0).astype(BF16)
        s = _dot_nt(k_all, qz)
        s = jnp.where(mask, s, -jnp.inf)
        m = jnp.maximum(jnp.max(s, axis=0, keepdims=True), sink)
        p = jnp.exp(s - m)
        denom = jnp.sum(p, axis=0, keepdims=True) + jnp.exp(sink - m)
        w = (p * (1.0 / denom)).T.astype(BF16)
        oz = _dot(w, v_all)
        for pr in range(N_HEADS // 2):
            lanes = slice((pr // (N_HEADS // 4)) * LANES, (pr // (N_HEADS // 4) + 1) * LANES)
            a = oz[2 * pr * n_new:(2 * pr + 1) * n_new, lanes]
            b = oz[(2 * pr + 1) * n_new:(2 * pr + 2) * n_new, lanes]
            o_ref[pl.ds(r0, n_new), pr * LANES:(pr + 1) * LANES] = jnp.where(low, a, b)
        nk_ref[sb, 0:rows - n_new, :] = ck_ref[sb, n_new:rows, :]
        nk_ref[sb, rows - n_new:rows, :] = k_new
        nv_ref[sb, 0:rows - n_new, :] = cv_ref[sb, n_new:rows, :]
        nv_ref[sb, rows - n_new:rows, :] = v_new
        return carry

    lax.fori_loop(0, seqs, body, 0)


def _attn_sample_call(sink_row, q, k_new, v_new, cache_k, cache_v, n_new, seqs):
    b, rows, _ = cache_k.shape
    tok = pl.BlockSpec((seqs * n_new, Q_DIM), lambda i: (i, 0))
    tok_kv = pl.BlockSpec((seqs * n_new, KV_DIM), lambda i: (i, 0))
    cache = pl.BlockSpec((seqs, rows, KV_DIM), lambda i: (i, 0, 0))
    return pl.pallas_call(
        functools.partial(_attn_sample_kernel, n_new, seqs),
        grid=(b // seqs,),
        in_specs=[pl.BlockSpec((1, N_HEADS * n_new), lambda i: (0, 0)), tok, tok_kv, tok_kv, cache, cache],
        out_specs=[tok, cache, cache],
        out_shape=[
            jax.ShapeDtypeStruct((b * n_new, Q_DIM), F32),
            jax.ShapeDtypeStruct((b, rows, KV_DIM), F32),
            jax.ShapeDtypeStruct((b, rows, KV_DIM), F32),
        ],
        compiler_params=_cparams(("parallel",)),
        name="attn_sample",
    )(sink_row, q, k_new, v_new, cache_k, cache_v)


def _moe_cap(t_total):
    return t_total + MOE_TILE


def _expert_steps(cap):
    return cap // MOE_TILE - 1 + N_EXPERT_GROUPS


def _write_step_table(maps_ref, counts, cap):
    per = cap // MOE_TILE
    shift = MOE_TILE.bit_length() - 1
    ends = []
    for c in counts:
        tiles = lax.shift_right_logical(c + (MOE_TILE - 1), shift)
        ends.append(tiles if not ends else ends[-1] + tiles)
    total = ends[-1]
    for j in range(maps_ref.shape[1]):
        jj = jnp.maximum(jnp.minimum(j, total - 1), 0)
        g = sum((jj >= e).astype(I32) for e in ends[:-1])
        start = sum(jnp.where(g > k, ends[k] - (ends[k - 1] if k else 0), 0) for k in range(len(ends) - 1))
        blk_in = g * per + jj - start
        valid = (total > j).astype(I32)
        maps_ref[0, j] = blk_in
        maps_ref[1, j] = jnp.where(valid == 1, blk_in, len(counts) * per)
        maps_ref[2, j] = g
        maps_ref[3, j] = valid


def _route_scatter(final, cap, x1, gn_ref, wrh_ref, wrl_ref, br_ref, triu_ref, cntc_in, cntr_in,
                   dest_ref, cntc_out, cntr_out, xs_hbm, maps_ref, pay, zrows, dest_v, dest_s, cntc, cntr,
                   sem):
    i = pl.program_id(0)
    n = pl.num_programs(0)
    slot = i % 2
    tm = x1.shape[0]

    @pl.when(i == 0)
    def _():
        cntc[...] = cntc_in[...]
        cntr[...] = cntr_in[...]

    xn = _rms(x1, gn_ref[...])
    xh = xn.astype(BF16)
    xl = (xn - xh.astype(F32)).astype(BF16)
    logits = _dot(xh, wrh_ref[...]) + _dot(xl, wrh_ref[...]) + _dot(xh, wrl_ref[...]) + br_ref[...]
    lt = logits.T
    ge = N_EXPERTS // EXPERTS_PER_GROUP
    sub = lax.broadcasted_iota(I32, (EXPERTS_PER_GROUP, tm), 0).astype(F32)
    big = jnp.float32(LANES)
    neg = -jnp.inf
    gl = jnp.where(sub < N_EXPERT_GROUPS, lt[N_EXPERTS:N_EXPERTS + EXPERTS_PER_GROUP], neg)
    gmax = jnp.max(gl, axis=0, keepdims=True)
    g_val = 1.0 / jnp.sum(jnp.exp(gl - gmax), axis=0, keepdims=True)
    g_idx = jnp.min(jnp.where(gl == gmax, sub, big), axis=0, keepdims=True)
    el = lt[0:EXPERTS_PER_GROUP]
    for g in range(1, ge):
        el = jnp.where(g_idx == g, lt[g * EXPERTS_PER_GROUP:(g + 1) * EXPERTS_PER_GROUP], el)
    e1 = jnp.max(el, axis=0, keepdims=True)
    i1 = jnp.min(jnp.where(el == e1, sub, big), axis=0, keepdims=True)
    el2 = jnp.where(sub == i1, neg, el)
    e2 = jnp.max(el2, axis=0, keepdims=True)
    i2 = jnp.min(jnp.where(el2 == e2, sub, big), axis=0, keepdims=True)
    t = jnp.exp(e2 - e1)
    w1 = 1.0 / (1.0 + t)
    w2 = t / (1.0 + t)
    wts_t = g_val * (jnp.where(sub == i1, w1, 0.0) + jnp.where(sub == i2, w2, 0.0))
    wts = jnp.concatenate([wts_t, jnp.zeros((LANES - EXPERTS_PER_GROUP, tm), F32)], axis=0).T

    def wait_slot(s):
        pltpu.make_async_copy(pay.at[s], xs_hbm.at[pl.ds(0, tm), :], sem.at[s]).wait()

    @pl.when(i >= 2)
    def _():
        wait_slot(slot)

    pay[slot, :, :PAY_X] = xn
    pay[slot, :, PAY_X:] = wts

    grp_t = lax.broadcasted_iota(I32, (LANES, tm), 0).astype(F32)
    oht = (grp_t == g_idx).astype(F32)
    rank = _dot(oht.astype(BF16), triu_ref[...])
    grp = lax.broadcasted_iota(I32, (LANES, 1), 0).astype(F32)
    base = grp * float(cap) + cntc[:, 0:1]
    dest = jnp.sum(oht * (rank + base), axis=0, keepdims=True).astype(I32)
    cntc[...] = cntc[...] + jnp.sum(oht, axis=1, keepdims=True)
    dest_ref[0] = dest
    dest_v[...] = dest
    pltpu.sync_copy(dest_v, dest_s)

    for s in range(2):
        @pl.when(slot == s)
        def _():
            for r in range(tm):
                pltpu.make_async_copy(pay.at[s, pl.ds(r, 1), :],
                                      xs_hbm.at[pl.ds(dest_s[0, r], 1), :], sem.at[s]).start()

    @pl.when(i == n - 1)
    def _():
        cntc_out[...] = cntc[...]
        cntr[...] = cntc[...].T[0:8, :]
        cntr_out[...] = cntr[...]

        @pl.when(n >= 2)
        def _():
            wait_slot(1 - slot)

        wait_slot(slot)
        dest_v[:, 0:LANES] = cntr[0:1, :].astype(I32)
        pltpu.sync_copy(dest_v, dest_s)
        counts = [dest_s[0, g] for g in range(N_EXPERT_GROUPS)]
        _write_step_table(maps_ref, counts, cap)
        if final:
            zrows[...] = jnp.zeros_like(zrows)
            starts = []
            for g in range(N_EXPERT_GROUPS):
                c_g = counts[g]
                starts.append(pl.multiple_of(g * cap + lax.shift_left(lax.shift_right_logical(c_g + 7, 3), 3), 8))
                for k in range(7):
                    pltpu.make_async_copy(zrows.at[pl.ds(0, 1), :],
                                          xs_hbm.at[pl.ds(g * cap + c_g + k, 1), :], sem.at[1]).start()
            for _ in range(7 * N_EXPERT_GROUPS):
                pltpu.make_async_copy(zrows.at[pl.ds(0, 1), :], xs_hbm.at[pl.ds(0, 1), :], sem.at[1]).wait()
            for g in range(N_EXPERT_GROUPS):
                pltpu.make_async_copy(zrows, xs_hbm.at[pl.ds(starts[g], MOE_TILE), :], sem.at[0]).start()
            for g in range(N_EXPERT_GROUPS):
                pltpu.make_async_copy(zrows, xs_hbm.at[pl.ds(0, MOE_TILE), :], sem.at[0]).wait()


def _oproj_kernel(has_prev, final, cap, *refs):
    (o_ref, x_ref, wo_ref, bo_ref) = refs[:4]
    rest = refs[4:]
    if has_prev:
        rest = rest[:7] + rest[8:]
    x1 = x_ref[...] + _dot(o_ref[...].astype(BF16), wo_ref[...]) + bo_ref[...]
    x1_ref = rest[7]
    x1_ref[...] = x1
    _route_scatter(final, cap, x1, *rest[:7], *rest[8:])


def _gelu_tanh(x):
    return x * (0.5 * (1.0 + jnp.tanh(math.sqrt(2.0 / math.pi) * (x + 0.044715 * (x * x * x)))))


def _glu_kernel(has_prev, final, cap, *refs):
    (y_ref, x_ref, gm_ref, d_ref, wa_ref, wb_ref) = refs[:6]
    rest = refs[6:]
    if has_prev:
        rest = rest[:7] + rest[8:]
    x = x_ref[...]
    u = _rms(x, gm_ref[...])
    z = _gelu_tanh(y_ref[...] + d_ref[...] * u).astype(BF16)
    x1 = x + _dot(z, wa_ref[...]) * _sigmoid(_dot(z, wb_ref[...]))
    x1_ref = rest[7]
    x1_ref[...] = x1
    _route_scatter(final, cap, x1, *rest[:7], *rest[8:])


def _row_spec(tm, width):
    return pl.BlockSpec((tm, width), lambda i: (i, 0))


def _const_spec(shape):
    return pl.BlockSpec(shape, lambda i: (0,) * len(shape))


def _mixer_call(body, name, lead_args, lead_specs, x2d, router, triu, cnt, xs_prev, final, cap):
    t = x2d.shape[0]
    tm = TOKEN_TILE
    n_tiles = t // tm
    gn, wrh, wrl, br = router
    cntc, cntr = cnt
    has_prev = xs_prev is not None
    in_specs = lead_specs + [
        _const_spec((1, D_MODEL)), _const_spec((D_MODEL, LANES)), _const_spec((D_MODEL, LANES)),
        _const_spec((1, LANES)), _const_spec((tm, tm)), _const_spec((LANES, LANES)), _const_spec((8, LANES)),
    ]
    args = list(lead_args) + [gn, wrh, wrl, br, triu, cntc, cntr]
    aliases = {}
    if has_prev:
        in_specs.append(pl.BlockSpec(memory_space=pl.ANY))
        args.append(xs_prev)
        aliases = {len(args) - 1: 4}
    x1, dest, cntc2, cntr2, xs, maps = pl.pallas_call(
        functools.partial(body, has_prev, final, cap),
        grid=(n_tiles,),
        in_specs=in_specs,
        out_specs=[
            _row_spec(tm, D_MODEL),
            pl.BlockSpec((1, 1, tm), lambda i: (i, 0, 0)),
            _const_spec((LANES, LANES)), _const_spec((8, LANES)),
            pl.BlockSpec(memory_space=pl.ANY),
            pl.BlockSpec(memory_space=pltpu.SMEM),
        ],
        out_shape=[
            jax.ShapeDtypeStruct((t, D_MODEL), F32),
            jax.ShapeDtypeStruct((n_tiles, 1, tm), I32),
            jax.ShapeDtypeStruct((LANES, LANES), F32),
            jax.ShapeDtypeStruct((8, LANES), F32),
            jax.ShapeDtypeStruct((N_EXPERT_GROUPS * cap, PAY_W), F32),
            jax.ShapeDtypeStruct((4, _expert_steps(cap)), I32),
        ],
        scratch_shapes=[
            pltpu.VMEM((2, tm, PAY_W), F32),
            pltpu.VMEM((MOE_TILE, PAY_W), F32),
            pltpu.VMEM((1, tm), I32),
            pltpu.SMEM((1, tm), I32),
            pltpu.VMEM((LANES, LANES), F32),
            pltpu.VMEM((8, LANES), F32),
            pltpu.SemaphoreType.DMA((2,)),
        ],
        input_output_aliases=aliases,
        compiler_params=_cparams(("arbitrary",)),
        name=name,
    )(*args)
    return x1, dest, (cntc2, cntr2), xs, maps


def _oproj_call(o2d, x2d, wo_bf, bo, router, triu, cnt, xs_prev, final, cap):
    tm = TOKEN_TILE
    lead_specs = [_row_spec(tm, Q_DIM), _row_spec(tm, D_MODEL), _const_spec((Q_DIM, D_MODEL)),
                  _const_spec((1, D_MODEL))]
    return _mixer_call(_oproj_kernel, "oproj_route", [o2d, x2d, wo_bf, bo], lead_specs, x2d,
                       router, triu, cnt, xs_prev, final, cap)


def _glu_call(y2d, x2d, gm, d, wa_bf, wb_bf, router, triu, cnt, xs_prev, final, cap):
    tm = TOKEN_TILE
    lead_specs = [_row_spec(tm, D_MODEL), _row_spec(tm, D_MODEL), _const_spec((1, D_MODEL)),
                  _const_spec((1, D_MODEL)), _const_spec((D_MODEL, D_MODEL)),
                  _const_spec((D_MODEL, D_MODEL))]
    return _mixer_call(_glu_kernel, "glu_route", [y2d, x2d, gm, d, wa_bf, wb_bf], lead_specs, x2d,
                       router, triu, cnt, xs_prev, final, cap)


def _expert_kernel(maps_ref, xs_ref, wg_ref, wu_ref, wd_ref, ys_ref, wgu, wdn, hid):
    j = pl.program_id(0)
    f = EXPERT_FF
    changed = (j == 0) | (maps_ref[2, j] != maps_ref[2, jnp.maximum(j - 1, 0)])
    valid = maps_ref[3, j]

    @pl.when(changed)
    def _():
        for e in range(EXPERTS_PER_GROUP):
            wgu[e, :, :f] = wg_ref[e].astype(BF16)
            wgu[e, :, f:] = wu_ref[e].astype(BF16)
            wdn[e * f:(e + 1) * f, :] = wd_ref[e].astype(BF16)

    @pl.when(valid == 1)
    def _():
        x = xs_ref[:, :PAY_X].astype(BF16)
        wts = xs_ref[:, PAY_X:]
        for e in range(EXPERTS_PER_GROUP):
            gu = _dot(x, wgu[e])
            g, u = gu[:, :f], gu[:, f:]
            hid[:, e * f:(e + 1) * f] = ((g * _sigmoid(g)) * u * wts[:, e:e + 1]).astype(BF16)
        ys_ref[...] = _dot(hid[...], wdn[...])

    @pl.when(valid == 0)
    def _():
        ys_ref[...] = jnp.zeros_like(ys_ref)


def _expert_call(maps, xs, w_gate, w_up, w_down, layer, cap):
    n_steps = maps.shape[1]
    e, f = EXPERTS_PER_GROUP, EXPERT_FF
    w_gate = w_gate.reshape(-1, D_MODEL, f)
    w_up = w_up.reshape(-1, D_MODEL, f)
    w_down = w_down.reshape(-1, f, D_MODEL)
    w_blk = lambda j, m: (m[2, j] + layer * N_EXPERT_GROUPS, 0, 0)
    grid_spec = pltpu.PrefetchScalarGridSpec(
        num_scalar_prefetch=1,
        grid=(n_steps,),
        in_specs=[
            pl.BlockSpec((MOE_TILE, PAY_W), lambda j, m: (m[0, j], 0)),
            pl.BlockSpec((e, D_MODEL, f), w_blk),
            pl.BlockSpec((e, D_MODEL, f), w_blk),
            pl.BlockSpec((e, f, D_MODEL), w_blk),
        ],
        out_specs=pl.BlockSpec((MOE_TILE, D_MODEL), lambda j, m: (m[1, j], 0)),
        scratch_shapes=[
            pltpu.VMEM((e, D_MODEL, 2 * f), BF16),
            pltpu.VMEM((e * f, D_MODEL), BF16),
            pltpu.VMEM((MOE_TILE, e * f), BF16),
        ],
    )
    return pl.pallas_call(
        _expert_kernel,
        grid_spec=grid_spec,
        out_shape=jax.ShapeDtypeStruct((N_EXPERT_GROUPS * cap + MOE_TILE, D_MODEL), F32),
        compiler_params=_cparams(("arbitrary",)),
        name="moe_experts",
    )(maps, xs, w_gate, w_up, w_down)


def _combine_kernel(emit_x, dest_ref, x_ref, gnext_ref, ys_hbm, *rest):
    outs, (ybuf, sem) = rest[:-2], rest[-2:]
    i = pl.program_id(0)
    n_tiles = pl.num_programs(0) - 1
    slot = i % 2
    tm = x_ref.shape[0]

    for s in range(2):
        @pl.when((i < n_tiles) & (slot == s))
        def _():
            for r in range(tm):
                pltpu.make_async_copy(ys_hbm.at[pl.ds(dest_ref[0, 0, r], 1), :],
                                      ybuf.at[s, pl.ds(r, 1), :], sem.at[s]).start()

    @pl.when(i >= 1)
    def _():
        prev = 1 - slot
        pltpu.make_async_copy(ys_hbm.at[pl.ds(0, tm), :], ybuf.at[prev], sem.at[prev]).wait()
        x2 = x_ref[...] + ybuf[prev]
        normed = _rms(x2, gnext_ref[...])
        if emit_x:
            outs[0][...] = x2
            outs[1][...] = normed
        else:
            outs[0][...] = normed


def _combine_call(dest, x2d, ys, gnext, emit_x):
    t = x2d.shape[0]
    tm = TOKEN_TILE
    n_tiles = t // tm
    n_out = 2 if emit_x else 1
    done = lambda i: (jnp.maximum(i - 1, 0), 0)
    return pl.pallas_call(
        functools.partial(_combine_kernel, emit_x),
        grid=(n_tiles + 1,),
        in_specs=[
            pl.BlockSpec((1, 1, tm), lambda i: (jnp.minimum(i, n_tiles - 1), 0, 0),
                         memory_space=pltpu.SMEM),
            pl.BlockSpec((tm, D_MODEL), done), _const_spec((1, D_MODEL)),
            pl.BlockSpec(memory_space=pl.ANY),
        ],
        out_specs=[pl.BlockSpec((tm, D_MODEL), done)] * n_out,
        out_shape=[jax.ShapeDtypeStruct((t, D_MODEL), F32)] * n_out,
        scratch_shapes=[pltpu.VMEM((2, tm, D_MODEL), F32), pltpu.SemaphoreType.DMA((2,))],
        compiler_params=_cparams(("arbitrary",)),
        name="moe_combine",
    )(dest, x2d, gnext, ys)


def _s5_state_in(u_ref_val, wre_ref, wim_ref, sre_ref, sim_ref, pair_w):
    for m in range(S5_GB // 2):
        up = u_ref_val(m * pair_w, pair_w)
        sre_ref[:, m * LANES:(m + 1) * LANES] = _dot(up, wre_ref[m])
        sim_ref[:, m * LANES:(m + 1) * LANES] = _dot(up, wim_ref[m])


def _s5_outputs(u_ref_val, hre, him, m_ref, gre_ref, gim_ref, y_store, pair_w):
    gw = pair_w // 2
    for m in range(S5_GB // 2):
        hr = hre(m).astype(BF16)
        hi = him(m).astype(BF16)
        y = _dot(hr, gre_ref[m]) + _dot(hi, gim_ref[m])
        y0 = y[:, :gw] + _dot(u_ref_val(m * pair_w, gw), m_ref[2 * m])
        y1 = y[:, gw:] + _dot(u_ref_val(m * pair_w + gw, gw), m_ref[2 * m + 1])
        y_store(m * pair_w, gw, y0)
        y_store(m * pair_w + gw, gw, y1)


def _s5_flatten(load_rows, q, n, ut, uflat):
    gc = SSM_GROUP_CH
    qc = q * gc
    for s in range(q):
        ut[:, s * gc:(s + 1) * gc, :] = load_rows(s).T.reshape(S5_GB, gc, n)
    for g in range(S5_GB):
        uflat[:, g * qc:(g + 1) * qc] = ut[g].T.astype(BF16)


def _s5_unflatten(yflat, yt, store_rows, q, n):
    gc = SSM_GROUP_CH
    qc = q * gc
    for g in range(S5_GB):
        yt[g] = yflat[:, g * qc:(g + 1) * qc].T
    for t in range(q):
        store_rows(t, yt[:, t * gc:(t + 1) * gc, :].reshape(S5_GB * gc, n).T)


def _s5_prompt_kernel(u_ref, wre_ref, wim_ref, m_ref, gre_ref, gim_ref, aqr_ref, aqi_ref,
                      y_ref, her_ref, hei_ref, ut, uflat, sre, sim, hre, him, yflat, yt):
    pair_w = 2 * S5_CHUNK * SSM_GROUP_CH
    n_chunks = u_ref.shape[1] // S5_CHUNK
    _s5_flatten(lambda s: u_ref[0, pl.ds(s, n_chunks, stride=S5_CHUNK), :],
                S5_CHUNK, n_chunks, ut, uflat)
    u_val = lambda off, w: uflat[:, off:off + w]
    _s5_state_in(u_val, wre_ref, wim_ref, sre, sim, pair_w)
    ar = aqr_ref[...]
    ai = aqi_ref[...]

    def step(n, carry):
        hr, hi = carry
        hre[pl.ds(n, 1), :] = hr
        him[pl.ds(n, 1), :] = hi
        sr = sre[pl.ds(n, 1), :]
        si = sim[pl.ds(n, 1), :]
        return ar * hr - ai * hi + sr, ar * hi + ai * hr + si

    zero = jnp.zeros(ar.shape, F32)
    hr, hi = lax.fori_loop(0, n_chunks, step, (zero, zero))
    her_ref[0] = hr
    hei_ref[0] = hi

    def y_store(off, w, val):
        yflat[:, off:off + w] = val

    _s5_outputs(u_val, lambda m: hre[:, m * LANES:(m + 1) * LANES],
                lambda m: him[:, m * LANES:(m + 1) * LANES],
                m_ref, gre_ref, gim_ref, y_store, pair_w)

    def store_rows(t, val):
        y_ref[0, pl.ds(t, n_chunks, stride=S5_CHUNK), :] = val

    _s5_unflatten(yflat, yt, store_rows, S5_CHUNK, n_chunks)


def _s5_sample_kernel(n_new, u_ref, h0r_ref, h0i_ref, wre_ref, wim_ref, m_ref, gre_ref, gim_ref,
                      aqr_ref, aqi_ref, y_ref, hnr_ref, hni_ref, ut, uflat, sre, sim, yflat, yt):
    pair_w = 2 * n_new * SSM_GROUP_CH
    seqs = h0r_ref.shape[0]
    _s5_flatten(lambda s: u_ref[pl.ds(s, seqs, stride=n_new), :], n_new, seqs, ut, uflat)
    u_val = lambda off, w: uflat[:, off:off + w]
    _s5_state_in(u_val, wre_ref, wim_ref, sre, sim, pair_w)
    ar = aqr_ref[...]
    ai = aqi_ref[...]
    h0r = h0r_ref[...]
    h0i = h0i_ref[...]
    hnr_ref[...] = ar * h0r - ai * h0i + sre[...]
    hni_ref[...] = ar * h0i + ai * h0r + sim[...]

    def y_store(off, w, val):
        yflat[:, off:off + w] = val

    _s5_outputs(u_val, lambda m: h0r_ref[:, m * LANES:(m + 1) * LANES],
                lambda m: h0i_ref[:, m * LANES:(m + 1) * LANES],
                m_ref, gre_ref, gim_ref, y_store, pair_w)

    def store_rows(t, val):
        y_ref[pl.ds(t, seqs, stride=n_new), :] = val

    _s5_unflatten(yflat, yt, store_rows, n_new, seqs)


def _s5_weight_specs(q, idx):
    qc = q * SSM_GROUP_CH
    np_ = S5_GB // 2
    st = S5_GB * SSM_STATE
    return [
        pl.BlockSpec((np_, 2 * qc, LANES), lambda *a: (idx(*a), 0, 0)),
        pl.BlockSpec((np_, 2 * qc, LANES), lambda *a: (idx(*a), 0, 0)),
        pl.BlockSpec((S5_GB, qc, qc), lambda *a: (idx(*a), 0, 0)),
        pl.BlockSpec((np_, LANES, 2 * qc), lambda *a: (idx(*a), 0, 0)),
        pl.BlockSpec((np_, LANES, 2 * qc), lambda *a: (idx(*a), 0, 0)),
        pl.BlockSpec((1, st), lambda *a: (0, idx(*a))),
        pl.BlockSpec((1, st), lambda *a: (0, idx(*a))),
    ]


def _s5_prompt_call(u, w):
    b, seq, _ = u.shape
    n_chunks = seq // S5_CHUNK
    gbl = S5_GB * SSM_GROUP_CH
    qc = S5_CHUNK * SSM_GROUP_CH
    st = S5_GB * SSM_STATE
    n_gb = SSM_GROUPS // S5_GB
    gb_of = lambda g, i: g
    tok = pl.BlockSpec((1, seq, gbl), lambda g, i: (i, 0, g))
    return pl.pallas_call(
        _s5_prompt_kernel,
        grid=(n_gb, b),
        in_specs=[tok] + _s5_weight_specs(S5_CHUNK, gb_of),
        out_specs=[
            tok,
            pl.BlockSpec((1, 1, st), lambda g, i: (i, 0, g)),
            pl.BlockSpec((1, 1, st), lambda g, i: (i, 0, g)),
        ],
        out_shape=[
            jax.ShapeDtypeStruct((b, seq, D_MODEL), F32),
            jax.ShapeDtypeStruct((b, 1, SSM_GROUPS * SSM_STATE), F32),
            jax.ShapeDtypeStruct((b, 1, SSM_GROUPS * SSM_STATE), F32),
        ],
        scratch_shapes=[
            pltpu.VMEM((S5_GB, qc, n_chunks), F32),
            pltpu.VMEM((n_chunks, S5_GB * qc), BF16),
            pltpu.VMEM((n_chunks, st), F32), pltpu.VMEM((n_chunks, st), F32),
            pltpu.VMEM((n_chunks, st), F32), pltpu.VMEM((n_chunks, st), F32),
            pltpu.VMEM((n_chunks, S5_GB * qc), F32),
            pltpu.VMEM((S5_GB, qc, n_chunks), F32),
        ],
        compiler_params=_cparams(("parallel", "parallel")),
        name="s5_prompt",
    )(u, *w)


def _s5_sample_call(u2d, h0r, h0i, w, n_new):
    t = u2d.shape[0]
    b = t // n_new
    gbl = S5_GB * SSM_GROUP_CH
    qc = n_new * SSM_GROUP_CH
    st = S5_GB * SSM_STATE
    n_gb = SSM_GROUPS // S5_GB
    gb_of = lambda g: g
    state = pl.BlockSpec((b, st), lambda g: (0, g))
    tok = pl.BlockSpec((t, gbl), lambda g: (0, g))
    return pl.pallas_call(
        functools.partial(_s5_sample_kernel, n_new),
        grid=(n_gb,),
        in_specs=[tok, state, state] + _s5_weight_specs(n_new, gb_of),
        out_specs=[tok, state, state],
        out_shape=[
            jax.ShapeDtypeStruct((t, D_MODEL), F32),
            jax.ShapeDtypeStruct((b, SSM_GROUPS * SSM_STATE), F32),
            jax.ShapeDtypeStruct((b, SSM_GROUPS * SSM_STATE), F32),
        ],
        scratch_shapes=[
            pltpu.VMEM((S5_GB, qc, b), F32),
            pltpu.VMEM((b, S5_GB * qc), BF16),
            pltpu.VMEM((b, st), F32), pltpu.VMEM((b, st), F32),
            pltpu.VMEM((b, S5_GB * qc), F32),
            pltpu.VMEM((S5_GB, qc, b), F32),
        ],
        compiler_params=_cparams(("parallel",)),
        name="s5_sample",
    )(u2d, h0r, h0i, *w)


def _block_diag_pairs(w):
    g, r, n = w.shape
    w2 = w.reshape(g // 2, 2, r, n)
    z = jnp.zeros((g // 2, r, n), w.dtype)
    top = jnp.concatenate([w2[:, 0], z], axis=2)
    bot = jnp.concatenate([z, w2[:, 1]], axis=2)
    return jnp.concatenate([top, bot], axis=1)


def _s5_discretize(a_re, a_im, log_dt, b_re, b_im):
    delta = jnp.exp(log_dt.astype(F32))[:, None]
    lr, li = a_re.astype(F32), a_im.astype(F32)
    mag = jnp.exp(delta * lr)
    abar_r = mag * jnp.cos(delta * li)
    abar_i = mag * jnp.sin(delta * li)
    nr, ni = abar_r - 1.0, abar_i
    den = lr * lr + li * li
    coef_r = ((nr * lr + ni * li) / den)[..., None]
    coef_i = ((ni * lr - nr * li) / den)[..., None]
    br, bi = b_re.astype(F32), b_im.astype(F32)
    return delta * lr, delta * li, coef_r * br - coef_i * bi, coef_r * bi + coef_i * br


def _s5_chunk_weights(log_mag, phase, bbar_r, bbar_i, c_re, c_im, q):
    g, p = log_mag.shape
    c = SSM_GROUP_CH
    hi = lax.Precision.HIGHEST
    k = jnp.arange(q + 1, dtype=F32)[:, None, None]
    mag = jnp.exp(k * log_mag[None])
    pw_r, pw_i = mag * jnp.cos(k * phase[None]), mag * jnp.sin(k * phase[None])
    rev_r, rev_i = pw_r[:q][::-1][..., None], pw_i[:q][::-1][..., None]
    wst_r = (rev_r * bbar_r[None] - rev_i * bbar_i[None]).transpose(1, 0, 3, 2).reshape(g, q * c, p)
    wst_i = (rev_r * bbar_i[None] + rev_i * bbar_r[None]).transpose(1, 0, 3, 2).reshape(g, q * c, p)
    x_r = pw_r[:q, :, :, None] * bbar_r[None] - pw_i[:q, :, :, None] * bbar_i[None]
    x_i = pw_r[:q, :, :, None] * bbar_i[None] + pw_i[:q, :, :, None] * bbar_r[None]
    cr, ci = c_re.astype(F32), c_im.astype(F32)
    ker = (jnp.einsum('gcp,tgpd->gtdc', cr, x_r, precision=hi)
           - jnp.einsum('gcp,tgpd->gtdc', ci, x_i, precision=hi))
    idx = jnp.arange(q)
    place = (idx[None, :, None] - idx[:, None, None] == idx[None, None, :]).astype(BF16)
    m = jnp.einsum('stk,gkdc->gsdtc', place, ker.astype(BF16),
                   preferred_element_type=F32).reshape(g, q * c, q * c)
    cr_t, ci_t = cr.transpose(0, 2, 1)[:, :, None, :], ci.transpose(0, 2, 1)[:, :, None, :]
    pr_t = pw_r[1:q + 1].transpose(1, 2, 0)[..., None]
    pi_t = pw_i[1:q + 1].transpose(1, 2, 0)[..., None]
    g_re = (cr_t * pr_t - ci_t * pi_t).reshape(g, p, q * c)
    g_im = (-(cr_t * pi_t + ci_t * pr_t)).reshape(g, p, q * c)
    return (_block_diag_pairs(wst_r).astype(BF16), _block_diag_pairs(wst_i).astype(BF16),
            m.astype(BF16), _block_diag_pairs(g_re).astype(BF16), _block_diag_pairs(g_im).astype(BF16),
            pw_r[q].reshape(1, g * p), pw_i[q].reshape(1, g * p))


def _rope_tables(pos):
    half = HEAD_DIM // 2
    inv = 1.0 / (ROPE_THETA ** (jnp.arange(half, dtype=F32) * (2.0 / HEAD_DIM)))
    ang = pos.astype(F32)[:, None] * inv[None, :]
    cos, sin = jnp.cos(ang), jnp.sin(ang)
    return jnp.tile(cos, (1, 4)), jnp.concatenate([-sin, sin, -sin, sin], axis=1)


def _router_weights(gain, w_rg, b_rg, w_re, b_re):
    n_layers = gain.shape[0]
    pad = LANES - N_EXPERTS - N_EXPERT_GROUPS
    w = jnp.concatenate([w_re, w_rg, jnp.zeros((n_layers, D_MODEL, pad), F32)], axis=2)
    b = jnp.concatenate([b_re, b_rg, jnp.zeros((n_layers, pad), F32)], axis=1)
    wh = w.astype(BF16)
    wl = (w - wh.astype(F32)).astype(BF16)
    return [(gain[l].reshape(1, -1).astype(F32), wh[l], wl[l], b[l].reshape(1, LANES))
            for l in range(n_layers)]


def kernel(x_prompt, x_sample, cache_k, cache_v, state_ssm_re, state_ssm_im, norm_mix, norm_ffn, norm_final, attn_w_qkv, attn_b_qkv, attn_w_o, attn_b_o, attn_sinks, ssm_a_re, ssm_a_im, ssm_log_dt, ssm_b_re, ssm_b_im, ssm_c_re, ssm_c_im, ssm_d, ssm_w_glu_a, ssm_w_glu_b, moe_w_router_group, moe_b_router_group, moe_w_router_expert, moe_b_router_expert, moe_w_gate, moe_w_up, moe_w_down):
    bsz, seq, _ = x_prompt.shape
    dbs, n_new, _ = x_sample.shape
    rows = cache_k.shape[2]
    tp, ts = bsz * seq, dbs * n_new
    cap = _moe_cap(tp + ts)
    xp = x_prompt.reshape(tp, D_MODEL)
    xs = x_sample.reshape(ts, D_MODEL)

    row1 = lambda v: v.reshape(1, -1).astype(F32)
    routers = _router_weights(norm_ffn, moe_w_router_group, moe_b_router_group,
                              moe_w_router_expert, moe_b_router_expert)
    triu = jnp.triu(jnp.ones((TOKEN_TILE, TOKEN_TILE), F32), 1).astype(BF16)
    cnt0 = (jnp.zeros((LANES, LANES), F32), jnp.zeros((8, LANES), F32))

    slots = jnp.asarray(HEAD_SLOTS, dtype=I32)

    def q_slots(w):
        lead = w.shape[:-1]
        qh = w[..., :Q_DIM].reshape(*lead, N_HEADS, HEAD_DIM)[..., slots, :].reshape(*lead, Q_DIM)
        return jnp.concatenate([qh, w[..., Q_DIM:]], axis=-1)

    wqkv = q_slots(attn_w_qkv[0]).astype(BF16)
    bqkv = row1(q_slots(attn_b_qkv[0]))
    wo = attn_w_o[0].reshape(N_HEADS, HEAD_DIM, D_MODEL)[slots].reshape(Q_DIM, D_MODEL).astype(BF16)
    bo = row1(attn_b_o[0])
    sinks = attn_sinks[0].astype(F32)[slots]
    g_mix0, g_mix1 = row1(norm_mix[0]), row1(norm_mix[1])
    pos = jnp.concatenate([jnp.arange(seq, dtype=I32),
                           jnp.tile(PAST_LEN + jnp.arange(n_new, dtype=I32), dbs)])
    cos_t, sin_t = _rope_tables(pos)

    qp, kp, vp = _qkv_call(xp, g_mix0, wqkv, bqkv, cos_t, sin_t, (0, seq), BF16)
    qs, ks, vs = _qkv_call(xs, g_mix0, wqkv, bqkv, cos_t, sin_t, (seq, ts), F32)
    op = _attn_prompt_call(jnp.repeat(sinks, BLOCK).reshape(1, -1), qp.reshape(bsz, seq, Q_DIM),
                           kp.reshape(bsz, seq, KV_DIM),
                           vp.reshape(bsz, seq, KV_DIM))
    os_, nks, nvs = _attn_sample_call(jnp.repeat(sinks, n_new).reshape(1, -1), qs, ks, vs,
                                      cache_k[0].reshape(dbs, rows, KV_DIM),
                                      cache_v[0].reshape(dbs, rows, KV_DIM), n_new, 8)
    xp1, dest_p, cnt, rows_x, _ = _oproj_call(op.reshape(tp, Q_DIM), xp, wo, bo, routers[0], triu,
                                              cnt0, None, False, cap)
    xs1, dest_s, cnt, rows_x, steps = _oproj_call(os_, xs, wo, bo, routers[0], triu, cnt, rows_x,
                                                  True, cap)
    rows_y = _expert_call(steps, rows_x, moe_w_gate, moe_w_up, moe_w_down, 0, cap)
    xp2, up = _combine_call(dest_p, xp1, rows_y, g_mix1, True)
    xs2, us = _combine_call(dest_s, xs1, rows_y, g_mix1, True)

    disc = _s5_discretize(ssm_a_re[0], ssm_a_im[0], ssm_log_dt[0], ssm_b_re[0], ssm_b_im[0])
    w_p = _s5_chunk_weights(*disc, ssm_c_re[0], ssm_c_im[0], S5_CHUNK)
    w_s = _s5_chunk_weights(*disc, ssm_c_re[0], ssm_c_im[0], n_new)
    y_p, hpr, hpi = _s5_prompt_call(up.reshape(bsz, seq, D_MODEL), w_p)
    y_p = y_p.reshape(tp, D_MODEL)
    h0r = state_ssm_re[0].reshape(dbs, -1).astype(F32)
    h0i = state_ssm_im[0].reshape(dbs, -1).astype(F32)
    y_s, hsr, hsi = _s5_sample_call(us, h0r, h0i, w_s, n_new)

    wa, wb = ssm_w_glu_a[0].astype(BF16), ssm_w_glu_b[0].astype(BF16)
    d_row = row1(ssm_d[0])
    xp3, dest_p, cnt, rows_x, _ = _glu_call(y_p, xp2, g_mix1, d_row, wa, wb, routers[1], triu,
                                            cnt0, None, False, cap)
    xs3, dest_s, cnt, rows_x, steps = _glu_call(y_s, xs2, g_mix1, d_row, wa, wb, routers[1], triu,
                                                cnt, rows_x, True, cap)
    rows_y = _expert_call(steps, rows_x, moe_w_gate, moe_w_up, moe_w_down, 1, cap)
    g_fin = row1(norm_final)
    (yp,) = _combine_call(dest_p, xp3, rows_y, g_fin, False)
    (ys,) = _combine_call(dest_s, xs3, rows_y, g_fin, False)

    kv5 = lambda a, n: a.reshape(1, n, rows, KV_HEADS, HEAD_DIM)
    st4 = lambda a, n: a.reshape(1, n, SSM_GROUPS, SSM_STATE)
    k_last = kp.reshape(bsz, seq, KV_DIM)[:, seq - WINDOW:]
    v_last = vp.reshape(bsz, seq, KV_DIM)[:, seq - WINDOW:]
    return (yp.reshape(bsz, seq, D_MODEL), ys.reshape(dbs, n_new, D_MODEL),
            k_last.reshape(1, bsz, WINDOW, KV_HEADS, HEAD_DIM), kv5(nks, dbs),
            v_last.reshape(1, bsz, WINDOW, KV_HEADS, HEAD_DIM), kv5(nvs, dbs),
            st4(hpr, bsz), st4(hsr, dbs), st4(hpi, bsz), st4(hsi, dbs))
```

```python
import functools
import math

import jax
import jax.numpy as jnp
from jax import lax
from jax.experimental import pallas as pl
from jax.experimental.pallas import tpu as pltpu

F32 = jnp.float32
BF16 = jnp.bfloat16
I32 = jnp.int32

D_MODEL = 1024
N_HEADS = 16
KV_HEADS = 4
HEAD_DIM = 64
Q_DIM = N_HEADS * HEAD_DIM
KV_DIM = KV_HEADS * HEAD_DIM
QKV_DIM = Q_DIM + 2 * KV_DIM
WINDOW = 128
BLOCK = 128
ROPE_THETA = 10000.0
PAST_LEN = 16384
SSM_GROUP_CH = 16
SSM_GROUPS = D_MODEL // SSM_GROUP_CH
SSM_STATE = 64
N_EXPERT_GROUPS = 4
EXPERTS_PER_GROUP = 8
N_EXPERTS = N_EXPERT_GROUPS * EXPERTS_PER_GROUP
EXPERT_FF = 128
NORM_EPS = 1e-5

LANES = 128
VMEM_LIMIT = 56 * 1024 * 1024
S5_CHUNK = 16
S5_GB = 8
TOKEN_TILE = 512
MOE_TILE = TOKEN_TILE
PAY_X = D_MODEL
PAY_W = PAY_X + LANES


def _cparams(sem):
    return pltpu.CompilerParams(dimension_semantics=sem, vmem_limit_bytes=VMEM_LIMIT)


def _rms(x, g):
    return x * lax.rsqrt(jnp.mean(x * x, axis=-1, keepdims=True) + NORM_EPS) * g


def _dot(a, b):
    return jnp.dot(a, b, preferred_element_type=F32)


def _dot_nt(a, b):
    return lax.dot_general(a, b, (((1,), (1,)), ((), ())), preferred_element_type=F32)


def _sigmoid(x):
    return 1.0 / (1.0 + jnp.exp(-x))


def _qkv_kernel(x_ref, g_ref, w_ref, b_ref, cos_ref, sin_ref, q_ref, k_ref, v_ref):
    xn = _rms(x_ref[...], g_ref[...])
    qkv = _dot(xn.astype(BF16), w_ref[...]) + b_ref[...]
    cos = cos_ref[...]
    sin = sin_ref[...]
    lane = lax.broadcasted_iota(I32, cos.shape, 1)
    first_half = (lane % HEAD_DIM) < (HEAD_DIM // 2)
    n_rot = (Q_DIM + KV_DIM) // LANES
    for c in range(n_rot):
        blk = qkv[:, c * LANES:(c + 1) * LANES]
        partner = jnp.where(first_half,
                            pltpu.roll(blk, LANES - HEAD_DIM // 2, 1),
                            pltpu.roll(blk, HEAD_DIM // 2, 1))
        rot = blk * cos + partner * sin
        if c < Q_DIM // LANES:
            q_ref[:, c * LANES:(c + 1) * LANES] = (rot * (HEAD_DIM ** -0.5)).astype(q_ref.dtype)
        else:
            k_ref[:, c * LANES - Q_DIM:(c + 1) * LANES - Q_DIM] = rot
    v_ref[...] = qkv[:, Q_DIM + KV_DIM:]


def _qkv_call(x2d, gain, w_bf, bias, cos_t, sin_t, pos_rows, q_dtype):
    t = x2d.shape[0]
    tm = TOKEN_TILE
    first, n_pos = pos_rows[0] // tm, pos_rows[1] // tm
    return pl.pallas_call(
        _qkv_kernel,
        grid=(t // tm,),
        in_specs=[
            pl.BlockSpec((tm, D_MODEL), lambda i: (i, 0)),
            pl.BlockSpec((1, D_MODEL), lambda i: (0, 0)),
            pl.BlockSpec((D_MODEL, QKV_DIM), lambda i: (0, 0)),
            pl.BlockSpec((1, QKV_DIM), lambda i: (0, 0)),
            pl.BlockSpec((tm, LANES), lambda i: (first + i % n_pos, 0)),
            pl.BlockSpec((tm, LANES), lambda i: (first + i % n_pos, 0)),
        ],
        out_specs=[
            pl.BlockSpec((tm, Q_DIM), lambda i: (i, 0)),
            pl.BlockSpec((tm, KV_DIM), lambda i: (i, 0)),
            pl.BlockSpec((tm, KV_DIM), lambda i: (i, 0)),
        ],
        out_shape=[
            jax.ShapeDtypeStruct((t, Q_DIM), q_dtype),
            jax.ShapeDtypeStruct((t, KV_DIM), F32),
            jax.ShapeDtypeStruct((t, KV_DIM), F32),
        ],
        compiler_params=_cparams(("parallel",)),
        name="qkv_rope",
    )(x2d, gain, w_bf, bias, cos_t, sin_t)


HEAD_SLOTS = tuple(8 * c + 4 * half + i for c in range(2) for i in range(4) for half in range(2))
LOG2E = math.log2(math.e)


def _attn_prompt_kernel(sink_ref, q_ref, kp_ref, kc_ref, vp_ref, vc_ref, o_ref):
    n = pl.program_id(1)
    k2 = jnp.concatenate([kp_ref[0], kc_ref[0]], axis=0).astype(BF16)
    v2 = jnp.concatenate([vp_ref[0], vc_ref[0]], axis=0).astype(BF16)
    key = lax.broadcasted_iota(I32, (2 * BLOCK, 2 * BLOCK), 0)
    qry = lax.broadcasted_iota(I32, (2 * BLOCK, 2 * BLOCK), 1) % BLOCK
    mask = (key >= qry) & (key <= qry + WINDOW) & ((n > 0) | (key >= BLOCK))
    low = lax.broadcasted_iota(I32, (BLOCK, LANES), 1) < HEAD_DIM
    keep_low = low.astype(BF16)
    keep_high = 1 - keep_low
    n_pairs = N_HEADS // 2
    chunk = lambda pr: slice((pr // (n_pairs // 2)) * LANES, (pr // (n_pairs // 2) + 1) * LANES)

    def scores(pr):
        qblk = q_ref[0, :, pr * LANES:(pr + 1) * LANES]
        qz = jnp.concatenate([qblk * keep_low, qblk * keep_high], axis=0)
        s = _dot_nt(k2[:, chunk(pr)], qz) * LOG2E
        return jnp.where(mask, s, -jnp.inf)

    def weights(pr, s):
        sink = sink_ref[:, 2 * pr * BLOCK:(2 * pr + 2) * BLOCK] * LOG2E
        m = jnp.maximum(jnp.max(s, axis=0, keepdims=True), sink)
        p = jnp.exp2(s - m)
        denom = jnp.sum(p, axis=0, keepdims=True) + jnp.exp2(sink - m)
        return (p * (1.0 / denom)).T.astype(BF16)

    def finish(pr, w):
        oz = _dot(w, v2[:, chunk(pr)])
        o_ref[0, :, pr * LANES:(pr + 1) * LANES] = jnp.where(low, oz[:BLOCK], oz[BLOCK:]).astype(o_ref.dtype)

    all_s = [scores(pr) for pr in range(n_pairs)]
    all_w = [weights(pr, s) for pr, s in enumerate(all_s)]
    for pr, w in enumerate(all_w):
        finish(pr, w)


def _attn_prompt_call(sinks, q, k, v):
    b, l, _ = q.shape
    nb = l // BLOCK
    kv_cur = pl.BlockSpec((1, BLOCK, KV_DIM), lambda i, n: (i, n, 0))
    kv_prev = pl.BlockSpec((1, BLOCK, KV_DIM), lambda i, n: (i, jnp.maximum(n - 1, 0), 0))
    return pl.pallas_call(
        _attn_prompt_kernel,
        grid=(b, nb),
        in_specs=[
            pl.BlockSpec((1, N_HEADS * BLOCK), lambda i, n: (0, 0)),
            pl.BlockSpec((1, BLOCK, Q_DIM), lambda i, n: (i, n, 0)),
            kv_prev, kv_cur, kv_prev, kv_cur,
        ],
        out_specs=pl.BlockSpec((1, BLOCK, Q_DIM), lambda i, n: (i, n, 0)),
        out_shape=jax.ShapeDtypeStruct((b, l, Q_DIM), BF16),
        compiler_params=_cparams(("parallel", "parallel")),
        name="attn_prompt",
    )(sinks, q, k, k, v, v)


def _attn_sample_kernel(n_new, seqs, sink_ref, q_ref, kn_ref, vn_ref, ck_ref, cv_ref,
                        o_ref, nk_ref, nv_ref):
    rows = ck_ref.shape[1]
    keys = 2 * rows
    n_cols = N_HEADS * n_new
    key = lax.broadcasted_iota(I32, (keys, n_cols), 0)
    qry = lax.broadcasted_iota(I32, (keys, n_cols), 1) % n_new
    mask = ((key < rows) & (key >= qry)) | ((key >= rows) & (key - rows <= qry))
    low = lax.broadcasted_iota(I32, (n_new, LANES), 1) < HEAD_DIM
    sink = sink_ref[...]
    pad = jnp.zeros((rows - n_new, KV_DIM), F32)

    def body(sb, carry):
        r0 = pl.multiple_of(sb * n_new, n_new)
        k_new = kn_ref[pl.ds(r0, n_new), :]
        v_new = vn_ref[pl.ds(r0, n_new), :]
        k_all = jnp.concatenate([ck_ref[sb], k_new, pad], axis=0).astype(BF16)
        v_all = jnp.concatenate([cv_ref[sb], v_new, pad], axis=0).astype(BF16)
        blocks = []
        for slot in range(N_HEADS):
            pair = q_ref[pl.ds(r0, n_new), (slot // 2) * LANES:(slot // 2 + 1) * LANES]
            blk = jnp.where(low if slot % 2 == 0 else ~low, pair, 0.0)
            zero = jnp.zeros_like(blk)
            blocks.append(jnp.concatenate([blk, zero] if slot < N_HEADS // 2 else [zero, blk], axis=1))
        qz = jnp.concatenate(blocks, axis=0).astype(BF16)
        s = _dot_nt(k_all, qz)
        s = jnp.where(mask, s, -jnp.inf)
        m = jnp.maximum(jnp.max(s, axis=0, keepdims=True), sink)
        p = jnp.exp(s - m)
        denom = jnp.sum(p, axis=0, keepdims=True) + jnp.exp(sink - m)
        w = (p * (1.0 / denom)).T.astype(BF16)
        oz = _dot(w, v_all)
        for pr in range(N_HEADS // 2):
            lanes = slice((pr // (N_HEADS // 4)) * LANES, (pr // (N_HEADS // 4) + 1) * LANES)
            a = oz[2 * pr * n_new:(2 * pr + 1) * n_new, lanes]
            b = oz[(2 * pr + 1) * n_new:(2 * pr + 2) * n_new, lanes]
            o_ref[pl.ds(r0, n_new), pr * LANES:(pr + 1) * LANES] = jnp.where(low, a, b)
        nk_ref[sb, 0:rows - n_new, :] = ck_ref[sb, n_new:rows, :]
        nk_ref[sb, rows - n_new:rows, :] = k_new
        nv_ref[sb, 0:rows - n_new, :] = cv_ref[sb, n_new:rows, :]
        nv_ref[sb, rows - n_new:rows, :] = v_new
        return carry

    lax.fori_loop(0, seqs, body, 0)


def _attn_sample_call(sink_row, q, k_new, v_new, cache_k, cache_v, n_new, seqs):
    b, rows, _ = cache_k.shape
    tok = pl.BlockSpec((seqs * n_new, Q_DIM), lambda i: (i, 0))
    tok_kv = pl.BlockSpec((seqs * n_new, KV_DIM), lambda i: (i, 0))
    cache = pl.BlockSpec((seqs, rows, KV_DIM), lambda i: (i, 0, 0))
    return pl.pallas_call(
        functools.partial(_attn_sample_kernel, n_new, seqs),
        grid=(b // seqs,),
        in_specs=[pl.BlockSpec((1, N_HEADS * n_new), lambda i: (0, 0)), tok, tok_kv, tok_kv, cache, cache],
        out_specs=[tok, cache, cache],
        out_shape=[
            jax.ShapeDtypeStruct((b * n_new, Q_DIM), F32),
            jax.ShapeDtypeStruct((b, rows, KV_DIM), F32),
            jax.ShapeDtypeStruct((b, rows, KV_DIM), F32),
        ],
        compiler_params=_cparams(("parallel",)),
        name="attn_sample",
    )(sink_row, q, k_new, v_new, cache_k, cache_v)


def _moe_cap(t_total):
    return t_total + MOE_TILE


def _expert_steps(cap):
    return cap // MOE_TILE - 1 + N_EXPERT_GROUPS


def _write_step_table(maps_ref, counts, cap):
    per = cap // MOE_TILE
    shift = MOE_TILE.bit_length() - 1
    ends = []
    for c in counts:
        tiles = lax.shift_right_logical(c + (MOE_TILE - 1), shift)
        ends.append(tiles if not ends else ends[-1] + tiles)
    total = ends[-1]
    for j in range(maps_ref.shape[1]):
        jj = jnp.maximum(jnp.minimum(j, total - 1), 0)
        g = sum((jj >= e).astype(I32) for e in ends[:-1])
        start = sum(jnp.where(g > k, ends[k] - (ends[k - 1] if k else 0), 0) for k in range(len(ends) - 1))
        blk_in = g * per + jj - start
        valid = (total > j).astype(I32)
        maps_ref[0, j] = blk_in
        maps_ref[1, j] = jnp.where(valid == 1, blk_in, len(counts) * per)
        maps_ref[2, j] = g
        maps_ref[3, j] = valid


def _route_scatter(final, cap, x1, gn_ref, wrh_ref, wrl_ref, br_ref, triu_ref, cntc_in, cntr_in,
                   dest_ref, cntc_out, cntr_out, xs_hbm, maps_ref, pay, zrows, dest_v, dest_s, cntc, cntr,
                   sem):
    i = pl.program_id(0)
    n = pl.num_programs(0)
    slot = i % 2
    tm = x1.shape[0]

    @pl.when(i == 0)
    def _():
        cntc[...] = cntc_in[...]
        cntr[...] = cntr_in[...]

    xn = _rms(x1, gn_ref[...])
    xh = xn.astype(BF16)
    xl = (xn - xh.astype(F32)).astype(BF16)
    logits = _dot(xh, wrh_ref[...]) + _dot(xl, wrh_ref[...]) + _dot(xh, wrl_ref[...]) + br_ref[...]
    lt = logits.T
    ge = N_EXPERTS // EXPERTS_PER_GROUP
    sub = lax.broadcasted_iota(I32, (EXPERTS_PER_GROUP, tm), 0).astype(F32)
    big = jnp.float32(LANES)
    neg = -jnp.inf
    gl = jnp.where(sub < N_EXPERT_GROUPS, lt[N_EXPERTS:N_EXPERTS + EXPERTS_PER_GROUP], neg)
    gmax = jnp.max(gl, axis=0, keepdims=True)
    g_val = 1.0 / jnp.sum(jnp.exp(gl - gmax), axis=0, keepdims=True)
    g_idx = jnp.min(jnp.where(gl == gmax, sub, big), axis=0, keepdims=True)
    el = lt[0:EXPERTS_PER_GROUP]
    for g in range(1, ge):
        el = jnp.where(g_idx == g, lt[g * EXPERTS_PER_GROUP:(g + 1) * EXPERTS_PER_GROUP], el)
    e1 = jnp.max(el, axis=0, keepdims=True)
    i1 = jnp.min(jnp.where(el == e1, sub, big), axis=0, keepdims=True)
    el2 = jnp.where(sub == i1, neg, el)
    e2 = jnp.max(el2, axis=0, keepdims=True)
    i2 = jnp.min(jnp.where(el2 == e2, sub, big), axis=0, keepdims=True)
    t = jnp.exp(e2 - e1)
    w1 = 1.0 / (1.0 + t)
    w2 = t / (1.0 + t)
    wts_t = g_val * (jnp.where(sub == i1, w1, 0.0) + jnp.where(sub == i2, w2, 0.0))
    wts = jnp.concatenate([wts_t, jnp.zeros((LANES - EXPERTS_PER_GROUP, tm), F32)], axis=0).T

    def wait_slot(s):
        pltpu.make_async_copy(pay.at[s], xs_hbm.at[pl.ds(0, tm), :], sem.at[s]).wait()

    @pl.when(i >= 2)
    def _():
        wait_slot(slot)

    pay[slot, :, :PAY_X] = xn
    pay[slot, :, PAY_X:] = wts

    grp_t = lax.broadcasted_iota(I32, (LANES, tm), 0).astype(F32)
    oht = (grp_t == g_idx).astype(F32)
    rank = _dot(oht.astype(BF16), triu_ref[...])
    grp = lax.broadcasted_iota(I32, (LANES, 1), 0).astype(F32)
    base = grp * float(cap) + cntc[:, 0:1]
    dest = jnp.sum(oht * (rank + base), axis=0, keepdims=True).astype(I32)
    cntc[...] = cntc[...] + jnp.sum(oht, axis=1, keepdims=True)
    dest_ref[0] = dest
    dest_v[...] = dest
    pltpu.sync_copy(dest_v, dest_s)

    for s in range(2):
        @pl.when(slot == s)
        def _():
            for r in range(tm):
                pltpu.make_async_copy(pay.at[s, pl.ds(r, 1), :],
                                      xs_hbm.at[pl.ds(dest_s[0, r], 1), :], sem.at[s]).start()

    @pl.when(i == n - 1)
    def _():
        cntc_out[...] = cntc[...]
        cntr[...] = cntc[...].T[0:8, :]
        cntr_out[...] = cntr[...]

        @pl.when(n >= 2)
        def _():
            wait_slot(1 - slot)

        wait_slot(slot)
        dest_v[:, 0:LANES] = cntr[0:1, :].astype(I32)
        pltpu.sync_copy(dest_v, dest_s)
        counts = [dest_s[0, g] for g in range(N_EXPERT_GROUPS)]
        _write_step_table(maps_ref, counts, cap)
        if final:
            zrows[...] = jnp.zeros_like(zrows)
            starts = []
            for g in range(N_EXPERT_GROUPS):
                c_g = counts[g]
                starts.append(pl.multiple_of(g * cap + lax.shift_left(lax.shift_right_logical(c_g + 7, 3), 3), 8))
                for k in range(7):
                    pltpu.make_async_copy(zrows.at[pl.ds(0, 1), :],
                                          xs_hbm.at[pl.ds(g * cap + c_g + k, 1), :], sem.at[1]).start()
            for _ in range(7 * N_EXPERT_GROUPS):
                pltpu.make_async_copy(zrows.at[pl.ds(0, 1), :], xs_hbm.at[pl.ds(0, 1), :], sem.at[1]).wait()
            for g in range(N_EXPERT_GROUPS):
                pltpu.make_async_copy(zrows, xs_hbm.at[pl.ds(starts[g], MOE_TILE), :], sem.at[0]).start()
            for g in range(N_EXPERT_GROUPS):
                pltpu.make_async_copy(zrows, xs_hbm.at[pl.ds(0, MOE_TILE), :], sem.at[0]).wait()


def _oproj_kernel(has_prev, final, cap, *refs):
    (o_ref, x_ref, wo_ref, bo_ref) = refs[:4]
    rest = refs[4:]
    if has_prev:
        rest = rest[:7] + rest[8:]
    x1 = x_ref[...] + _dot(o_ref[...].astype(BF16), wo_ref[...]) + bo_ref[...]
    x1_ref = rest[7]
    x1_ref[...] = x1
    _route_scatter(final, cap, x1, *rest[:7], *rest[8:])


def _gelu_tanh(x):
    return x * (0.5 * (1.0 + jnp.tanh(math.sqrt(2.0 / math.pi) * (x + 0.044715 * (x * x * x)))))


def _glu_kernel(has_prev, final, cap, *refs):
    (y_ref, x_ref, gm_ref, d_ref, wa_ref, wb_ref) = refs[:6]
    rest = refs[6:]
    if has_prev:
        rest = rest[:7] + rest[8:]
    x = x_ref[...]
    u = _rms(x, gm_ref[...])
    z = _gelu_tanh(y_ref[...] + d_ref[...] * u).astype(BF16)
    x1 = x + _dot(z, wa_ref[...]) * _sigmoid(_dot(z, wb_ref[...]))
    x1_ref = rest[7]
    x1_ref[...] = x1
    _route_scatter(final, cap, x1, *rest[:7], *rest[8:])


def _row_spec(tm, width):
    return pl.BlockSpec((tm, width), lambda i: (i, 0))


def _const_spec(shape):
    return pl.BlockSpec(shape, lambda i: (0,) * len(shape))


def _mixer_call(body, name, lead_args, lead_specs, x2d, router, triu, cnt, xs_prev, final, cap):
    t = x2d.shape[0]
    tm = TOKEN_TILE
    n_tiles = t // tm
    gn, wrh, wrl, br = router
    cntc, cntr = cnt
    has_prev = xs_prev is not None
    in_specs = lead_specs + [
        _const_spec((1, D_MODEL)), _const_spec((D_MODEL, LANES)), _const_spec((D_MODEL, LANES)),
        _const_spec((1, LANES)), _const_spec((tm, tm)), _const_spec((LANES, LANES)), _const_spec((8, LANES)),
    ]
    args = list(lead_args) + [gn, wrh, wrl, br, triu, cntc, cntr]
    aliases = {}
    if has_prev:
        in_specs.append(pl.BlockSpec(memory_space=pl.ANY))
        args.append(xs_prev)
        aliases = {len(args) - 1: 4}
    x1, dest, cntc2, cntr2, xs, maps = pl.pallas_call(
        functools.partial(body, has_prev, final, cap),
        grid=(n_tiles,),
        in_specs=in_specs,
        out_specs=[
            _row_spec(tm, D_MODEL),
            pl.BlockSpec((1, 1, tm), lambda i: (i, 0, 0)),
            _const_spec((LANES, LANES)), _const_spec((8, LANES)),
            pl.BlockSpec(memory_space=pl.ANY),
            pl.BlockSpec(memory_space=pltpu.SMEM),
        ],
        out_shape=[
            jax.ShapeDtypeStruct((t, D_MODEL), F32),
            jax.ShapeDtypeStruct((n_tiles, 1, tm), I32),
            jax.ShapeDtypeStruct((LANES, LANES), F32),
            jax.ShapeDtypeStruct((8, LANES), F32),
            jax.ShapeDtypeStruct((N_EXPERT_GROUPS * cap, PAY_W), F32),
            jax.ShapeDtypeStruct((4, _expert_steps(cap)), I32),
        ],
        scratch_shapes=[
            pltpu.VMEM((2, tm, PAY_W), F32),
            pltpu.VMEM((MOE_TILE, PAY_W), F32),
            pltpu.VMEM((1, tm), I32),
            pltpu.SMEM((1, tm), I32),
            pltpu.VMEM((LANES, LANES), F32),
            pltpu.VMEM((8, LANES), F32),
            pltpu.SemaphoreType.DMA((2,)),
        ],
        input_output_aliases=aliases,
        compiler_params=_cparams(("arbitrary",)),
        name=name,
    )(*args)
    return x1, dest, (cntc2, cntr2), xs, maps


def _oproj_call(o2d, x2d, wo_bf, bo, router, triu, cnt, xs_prev, final, cap):
    tm = TOKEN_TILE
    lead_specs = [_row_spec(tm, Q_DIM), _row_spec(tm, D_MODEL), _const_spec((Q_DIM, D_MODEL)),
                  _const_spec((1, D_MODEL))]
    return _mixer_call(_oproj_kernel, "oproj_route", [o2d, x2d, wo_bf, bo], lead_specs, x2d,
                       router, triu, cnt, xs_prev, final, cap)


def _glu_call(y2d, x2d, gm, d, wa_bf, wb_bf, router, triu, cnt, xs_prev, final, cap):
    tm = TOKEN_TILE
    lead_specs = [_row_spec(tm, D_MODEL), _row_spec(tm, D_MODEL), _const_spec((1, D_MODEL)),
                  _const_spec((1, D_MODEL)), _const_spec((D_MODEL, D_MODEL)),
                  _const_spec((D_MODEL, D_MODEL))]
    return _mixer_call(_glu_kernel, "glu_route", [y2d, x2d, gm, d, wa_bf, wb_bf], lead_specs, x2d,
                       router, triu, cnt, xs_prev, final, cap)


def _expert_kernel(maps_ref, xs_ref, wg_ref, wu_ref, wd_ref, ys_ref, wgu, wdn, hid):
    j = pl.program_id(0)
    f = EXPERT_FF
    changed = (j == 0) | (maps_ref[2, j] != maps_ref[2, jnp.maximum(j - 1, 0)])
    valid = maps_ref[3, j]

    @pl.when(changed)
    def _():
        for e in range(EXPERTS_PER_GROUP):
            wgu[e, :, :f] = wg_ref[e].astype(BF16)
            wgu[e, :, f:] = wu_ref[e].astype(BF16)
            wdn[e * f:(e + 1) * f, :] = wd_ref[e].astype(BF16)

    @pl.when(valid == 1)
    def _():
        x = xs_ref[:, :PAY_X].astype(BF16)
        wts = xs_ref[:, PAY_X:]
        for e in range(EXPERTS_PER_GROUP):
            gu = _dot(x, wgu[e])
            g, u = gu[:, :f], gu[:, f:]
            hid[:, e * f:(e + 1) * f] = ((g * _sigmoid(g)) * u * wts[:, e:e + 1]).astype(BF16)
        ys_ref[...] = _dot(hid[...], wdn[...])

    @pl.when(valid == 0)
    def _():
        ys_ref[...] = jnp.zeros_like(ys_ref)


def _expert_call(maps, xs, w_gate, w_up, w_down, layer, cap):
    n_steps = maps.shape[1]
    e, f = EXPERTS_PER_GROUP, EXPERT_FF
    w_gate = w_gate.reshape(-1, D_MODEL, f)
    w_up = w_up.reshape(-1, D_MODEL, f)
    w_down = w_down.reshape(-1, f, D_MODEL)
    w_blk = lambda j, m: (m[2, j] + layer * N_EXPERT_GROUPS, 0, 0)
    grid_spec = pltpu.PrefetchScalarGridSpec(
        num_scalar_prefetch=1,
        grid=(n_steps,),
        in_specs=[
            pl.BlockSpec((MOE_TILE, PAY_W), lambda j, m: (m[0, j], 0)),
            pl.BlockSpec((e, D_MODEL, f), w_blk),
            pl.BlockSpec((e, D_MODEL, f), w_blk),
            pl.BlockSpec((e, f, D_MODEL), w_blk),
        ],
        out_specs=pl.BlockSpec((MOE_TILE, D_MODEL), lambda j, m: (m[1, j], 0)),
        scratch_shapes=[
            pltpu.VMEM((e, D_MODEL, 2 * f), BF16),
            pltpu.VMEM((e * f, D_MODEL), BF16),
            pltpu.VMEM((MOE_TILE, e * f), BF16),
        ],
    )
    return pl.pallas_call(
        _expert_kernel,
        grid_spec=grid_spec,
        out_shape=jax.ShapeDtypeStruct((N_EXPERT_GROUPS * cap + MOE_TILE, D_MODEL), F32),
        compiler_params=_cparams(("arbitrary",)),
        name="moe_experts",
    )(maps, xs, w_gate, w_up, w_down)


def _combine_kernel(emit_x, dest_ref, x_ref, gnext_ref, ys_hbm, *rest):
    outs, (ybuf, sem) = rest[:-2], rest[-2:]
    i = pl.program_id(0)
    n_tiles = pl.num_programs(0) - 1
    slot = i % 2
    tm = x_ref.shape[0]

    for s in range(2):
        @pl.when((i < n_tiles) & (slot == s))
        def _():
            for r in range(tm):
                pltpu.make_async_copy(ys_hbm.at[pl.ds(dest_ref[0, 0, r], 1), :],
                                      ybuf.at[s, pl.ds(r, 1), :], sem.at[s]).start()

    @pl.when(i >= 1)
    def _():
        prev = 1 - slot
        pltpu.make_async_copy(ys_hbm.at[pl.ds(0, tm), :], ybuf.at[prev], sem.at[prev]).wait()
        x2 = x_ref[...] + ybuf[prev]
        normed = _rms(x2, gnext_ref[...])
        if emit_x:
            outs[0][...] = x2
            outs[1][...] = normed
        else:
            outs[0][...] = normed


def _combine_call(dest, x2d, ys, gnext, emit_x):
    t = x2d.shape[0]
    tm = TOKEN_TILE
    n_tiles = t // tm
    n_out = 2 if emit_x else 1
    done = lambda i: (jnp.maximum(i - 1, 0), 0)
    return pl.pallas_call(
        functools.partial(_combine_kernel, emit_x),
        grid=(n_tiles + 1,),
        in_specs=[
            pl.BlockSpec((1, 1, tm), lambda i: (jnp.minimum(i, n_tiles - 1), 0, 0),
                         memory_space=pltpu.SMEM),
            pl.BlockSpec((tm, D_MODEL), done), _const_spec((1, D_MODEL)),
            pl.BlockSpec(memory_space=pl.ANY),
        ],
        out_specs=[pl.BlockSpec((tm, D_MODEL), done)] * n_out,
        out_shape=[jax.ShapeDtypeStruct((t, D_MODEL), F32)] * n_out,
        scratch_shapes=[pltpu.VMEM((2, tm, D_MODEL), F32), pltpu.SemaphoreType.DMA((2,))],
        compiler_params=_cparams(("arbitrary",)),
        name="moe_combine",
    )(dest, x2d, gnext, ys)


def _s5_state_in(u_ref_val, wre_ref, wim_ref, sre_ref, sim_ref, pair_w):
    for m in range(S5_GB // 2):
        up = u_ref_val(m * pair_w, pair_w)
        sre_ref[:, m * LANES:(m + 1) * LANES] = _dot(up, wre_ref[m])
        sim_ref[:, m * LANES:(m + 1) * LANES] = _dot(up, wim_ref[m])


def _s5_outputs(u_ref_val, hre, him, m_ref, gre_ref, gim_ref, y_store, pair_w):
    gw = pair_w // 2
    for m in range(S5_GB // 2):
        hr = hre(m).astype(BF16)
        hi = him(m).astype(BF16)
        y = _dot(hr, gre_ref[m]) + _dot(hi, gim_ref[m])
        y0 = y[:, :gw] + _dot(u_ref_val(m * pair_w, gw), m_ref[2 * m])
        y1 = y[:, gw:] + _dot(u_ref_val(m * pair_w + gw, gw), m_ref[2 * m + 1])
        y_store(m * pair_w, gw, y0)
        y_store(m * pair_w + gw, gw, y1)


def _s5_flatten(load_rows, q, n, ut, uflat):
    gc = SSM_GROUP_CH
    qc = q * gc
    for s in range(q):
        ut[:, s * gc:(s + 1) * gc, :] = load_rows(s).T.reshape(S5_GB, gc, n)
    for g in range(S5_GB):
        uflat[:, g * qc:(g + 1) * qc] = ut[g].T.astype(BF16)


def _s5_unflatten(yflat, yt, store_rows, q, n):
    gc = SSM_GROUP_CH
    qc = q * gc
    for g in range(S5_GB):
        yt[g] = yflat[:, g * qc:(g + 1) * qc].T
    for t in range(q):
        store_rows(t, yt[:, t * gc:(t + 1) * gc, :].reshape(S5_GB * gc, n).T)


def _s5_prompt_kernel(u_ref, wre_ref, wim_ref, m_ref, gre_ref, gim_ref, aqr_ref, aqi_ref,
                      y_ref, her_ref, hei_ref, ut, uflat, sre, sim, hre, him, yflat, yt):
    pair_w = 2 * S5_CHUNK * SSM_GROUP_CH
    n_chunks = u_ref.shape[1] // S5_CHUNK
    _s5_flatten(lambda s: u_ref[0, pl.ds(s, n_chunks, stride=S5_CHUNK), :],
                S5_CHUNK, n_chunks, ut, uflat)
    u_val = lambda off, w: uflat[:, off:off + w]
    _s5_state_in(u_val, wre_ref, wim_ref, sre, sim, pair_w)
    ar = aqr_ref[...]
    ai = aqi_ref[...]

    def step(n, carry):
        hr, hi = carry
        hre[pl.ds(n, 1), :] = hr
        him[pl.ds(n, 1), :] = hi
        sr = sre[pl.ds(n, 1), :]
        si = sim[pl.ds(n, 1), :]
        return ar * hr - ai * hi + sr, ar * hi + ai * hr + si

    zero = jnp.zeros(ar.shape, F32)
    hr, hi = lax.fori_loop(0, n_chunks, step, (zero, zero))
    her_ref[0] = hr
    hei_ref[0] = hi

    def y_store(off, w, val):
        yflat[:, off:off + w] = val

    _s5_outputs(u_val, lambda m: hre[:, m * LANES:(m + 1) * LANES],
                lambda m: him[:, m * LANES:(m + 1) * LANES],
                m_ref, gre_ref, gim_ref, y_store, pair_w)

    def store_rows(t, val):
        y_ref[0, pl.ds(t, n_chunks, stride=S5_CHUNK), :] = val

    _s5_unflatten(yflat, yt, store_rows, S5_CHUNK, n_chunks)


def _s5_sample_kernel(n_new, u_ref, h0r_ref, h0i_ref, wre_ref, wim_ref, m_ref, gre_ref, gim_ref,
                      aqr_ref, aqi_ref, y_ref, hnr_ref, hni_ref, ut, uflat, sre, sim, yflat, yt):
    pair_w = 2 * n_new * SSM_GROUP_CH
    seqs = h0r_ref.shape[0]
    _s5_flatten(lambda s: u_ref[pl.ds(s, seqs, stride=n_new), :], n_new, seqs, ut, uflat)
    u_val = lambda off, w: uflat[:, off:off + w]
    _s5_state_in(u_val, wre_ref, wim_ref, sre, sim, pair_w)
    ar = aqr_ref[...]
    ai = aqi_ref[...]
    h0r = h0r_ref[...]
    h0i = h0i_ref[...]
    hnr_ref[...] = ar * h0r - ai * h0i + sre[...]
    hni_ref[...] = ar * h0i + ai * h0r + sim[...]

    def y_store(off, w, val):
        yflat[:, off:off + w] = val

    _s5_outputs(u_val, lambda m: h0r_ref[:, m * LANES:(m + 1) * LANES],
                lambda m: h0i_ref[:, m * LANES:(m + 1) * LANES],
                m_ref, gre_ref, gim_ref, y_store, pair_w)

    def store_rows(t, val):
        y_ref[pl.ds(t, seqs, stride=n_new), :] = val

    _s5_unflatten(yflat, yt, store_rows, n_new, seqs)


def _s5_weight_specs(q, idx):
    qc = q * SSM_GROUP_CH
    np_ = S5_GB // 2
    st = S5_GB * SSM_STATE
    return [
        pl.BlockSpec((np_, 2 * qc, LANES), lambda *a: (idx(*a), 0, 0)),
        pl.BlockSpec((np_, 2 * qc, LANES), lambda *a: (idx(*a), 0, 0)),
        pl.BlockSpec((S5_GB, qc, qc), lambda *a: (idx(*a), 0, 0)),
        pl.BlockSpec((np_, LANES, 2 * qc), lambda *a: (idx(*a), 0, 0)),
        pl.BlockSpec((np_, LANES, 2 * qc), lambda *a: (idx(*a), 0, 0)),
        pl.BlockSpec((1, st), lambda *a: (0, idx(*a))),
        pl.BlockSpec((1, st), lambda *a: (0, idx(*a))),
    ]


def _s5_prompt_call(u, w):
    b, seq, _ = u.shape
    n_chunks = seq // S5_CHUNK
    gbl = S5_GB * SSM_GROUP_CH
    qc = S5_CHUNK * SSM_GROUP_CH
    st = S5_GB * SSM_STATE
    n_gb = SSM_GROUPS // S5_GB
    gb_of = lambda g, i: g
    tok = pl.BlockSpec((1, seq, gbl), lambda g, i: (i, 0, g))
    return pl.pallas_call(
        _s5_prompt_kernel,
        grid=(n_gb, b),
        in_specs=[tok] + _s5_weight_specs(S5_CHUNK, gb_of),
        out_specs=[
            tok,
            pl.BlockSpec((1, 1, st), lambda g, i: (i, 0, g)),
            pl.BlockSpec((1, 1, st), lambda g, i: (i, 0, g)),
        ],
        out_shape=[
            jax.ShapeDtypeStruct((b, seq, D_MODEL), F32),
            jax.ShapeDtypeStruct((b, 1, SSM_GROUPS * SSM_STATE), F32),
            jax.ShapeDtypeStruct((b, 1, SSM_GROUPS * SSM_STATE), F32),
        ],
        scratch_shapes=[
            pltpu.VMEM((S5_GB, qc, n_chunks), F32),
            pltpu.VMEM((n_chunks, S5_GB * qc), BF16),
            pltpu.VMEM((n_chunks, st), F32), pltpu.VMEM((n_chunks, st), F32),
            pltpu.VMEM((n_chunks, st), F32), pltpu.VMEM((n_chunks, st), F32),
            pltpu.VMEM((n_chunks, S5_GB * qc), F32),
            pltpu.VMEM((S5_GB, qc, n_chunks), F32),
        ],
        compiler_params=_cparams(("parallel", "parallel")),
        name="s5_prompt",
    )(u, *w)


def _s5_sample_call(u2d, h0r, h0i, w, n_new):
    t = u2d.shape[0]
    b = t // n_new
    gbl = S5_GB * SSM_GROUP_CH
    qc = n_new * SSM_GROUP_CH
    st = S5_GB * SSM_STATE
    n_gb = SSM_GROUPS // S5_GB
    gb_of = lambda g: g
    state = pl.BlockSpec((b, st), lambda g: (0, g))
    tok = pl.BlockSpec((t, gbl), lambda g: (0, g))
    return pl.pallas_call(
        functools.partial(_s5_sample_kernel, n_new),
        grid=(n_gb,),
        in_specs=[tok, state, state] + _s5_weight_specs(n_new, gb_of),
        out_specs=[tok, state, state],
        out_shape=[
            jax.ShapeDtypeStruct((t, D_MODEL), F32),
            jax.ShapeDtypeStruct((b, SSM_GROUPS * SSM_STATE), F32),
            jax.ShapeDtypeStruct((b, SSM_GROUPS * SSM_STATE), F32),
        ],
        scratch_shapes=[
            pltpu.VMEM((S5_GB, qc, b), F32),
            pltpu.VMEM((b, S5_GB * qc), BF16),
            pltpu.VMEM((b, st), F32), pltpu.VMEM((b, st), F32),
            pltpu.VMEM((b, S5_GB * qc), F32),
            pltpu.VMEM((S5_GB, qc, b), F32),
        ],
        compiler_params=_cparams(("parallel",)),
        name="s5_sample",
    )(u2d, h0r, h0i, *w)


def _s5_discretize(a_re, a_im, log_dt, b_re, b_im):
    delta = jnp.exp(log_dt.astype(F32))[:, None]
    lr, li = a_re.astype(F32), a_im.astype(F32)
    mag = jnp.exp(delta * lr)
    abar_r = mag * jnp.cos(delta * li)
    abar_i = mag * jnp.sin(delta * li)
    nr, ni = abar_r - 1.0, abar_i
    den = lr * lr + li * li
    coef_r = ((nr * lr + ni * li) / den)[..., None]
    coef_i = ((ni * lr - nr * li) / den)[..., None]
    br, bi = b_re.astype(F32), b_im.astype(F32)
    return delta * lr, delta * li, coef_r * br - coef_i * bi, coef_r * bi + coef_i * br


def _split3(x):
    x1 = x.astype(BF16)
    r1 = x - x1.astype(F32)
    x2 = r1.astype(BF16)
    x3 = (r1 - x2.astype(F32)).astype(BF16)
    return x1, x2, x3


def _s5_prep_kernel(q, bbr_ref, bbi_ref, ctr_ref, cti_ref, ppr_ref, ppi_ref, pnr_ref, pni_ref,
                    wre_ref, wim_ref, m_ref, gre_ref, gim_ref):
    c = SSM_GROUP_CH
    qc = q * c
    p2 = 2 * SSM_STATE
    col = lax.broadcasted_iota(I32, (c, qc), 1)
    row = lax.broadcasted_iota(I32, (c, qc), 0)
    by_ch = (col % c == row).astype(BF16)
    by_pos = (col // c == row).astype(BF16)

    def spread(x, sel):
        return sum(_dot(part, sel) for part in _split3(x))

    def dot_f32(a, b):
        a1, a2, _ = _split3(a)
        b1, b2, _ = _split3(b)
        return _dot(a1, b1) + _dot(a1, b2) + _dot(a2, b1)

    bc_r, bc_i = spread(bbr_ref[0], by_ch), spread(bbi_ref[0], by_ch)
    cc_r, cc_i = spread(ctr_ref[0], by_ch), spread(cti_ref[0], by_ch)
    pp_r, pp_i = spread(ppr_ref[0], by_pos), spread(ppi_ref[0], by_pos)
    pn_r, pn_i = spread(pnr_ref[0], by_pos), spread(pni_ref[0], by_pos)
    r_r = cc_r * pp_r - cc_i * pp_i
    r_i = cc_r * pp_i + cc_i * pp_r
    l_r = bc_r * pn_r - bc_i * pn_i
    l_i = bc_r * pn_i + bc_i * pn_r
    aq_r = ppr_ref[0][:, q - 1:q]
    aq_i = ppi_ref[0][:, q - 1:q]
    w_r = (l_r * aq_r - l_i * aq_i).T
    w_i = (l_r * aq_i + l_i * aq_r).T
    lt_r, lt_i = l_r.T, l_i.T
    grp_col = lax.broadcasted_iota(I32, (qc, p2), 1) // SSM_STATE
    grp_row = lax.broadcasted_iota(I32, (p2, qc), 0) // SSM_STATE
    causal = (lax.broadcasted_iota(I32, (qc, qc), 1) // c) >= (lax.broadcasted_iota(I32, (qc, qc), 0) // c)
    g_re, g_im = [], []
    for gl in range(2):
        mine = grp_col == gl
        wre_ref[0, gl * qc:(gl + 1) * qc, :] = jnp.where(mine, w_r, 0.0).astype(BF16)
        wim_ref[0, gl * qc:(gl + 1) * qc, :] = jnp.where(mine, w_i, 0.0).astype(BF16)
        ker = dot_f32(jnp.where(mine, lt_r, 0.0), r_r) - dot_f32(jnp.where(mine, lt_i, 0.0), r_i)
        m_ref[gl] = jnp.where(causal, ker, 0.0).astype(BF16)
        g_re.append(jnp.where(grp_row == gl, r_r, 0.0))
        g_im.append(jnp.where(grp_row == gl, -r_i, 0.0))
    gre_ref[0] = jnp.concatenate(g_re, axis=1).astype(BF16)
    gim_ref[0] = jnp.concatenate(g_im, axis=1).astype(BF16)


def _s5_chunk_weights(tables, q):
    n_pairs = SSM_GROUPS // 2
    qc = q * SSM_GROUP_CH
    p2 = 2 * SSM_STATE
    tab = pl.BlockSpec((1, p2, SSM_GROUP_CH), lambda m: (m, 0, 0))
    return pl.pallas_call(
        functools.partial(_s5_prep_kernel, q),
        grid=(n_pairs,),
        in_specs=[tab] * 8,
        out_specs=[
            pl.BlockSpec((1, 2 * qc, p2), lambda m: (m, 0, 0)),
            pl.BlockSpec((1, 2 * qc, p2), lambda m: (m, 0, 0)),
            pl.BlockSpec((2, qc, qc), lambda m: (m, 0, 0)),
            pl.BlockSpec((1, p2, 2 * qc), lambda m: (m, 0, 0)),
            pl.BlockSpec((1, p2, 2 * qc), lambda m: (m, 0, 0)),
        ],
        out_shape=[
            jax.ShapeDtypeStruct((n_pairs, 2 * qc, p2), BF16),
            jax.ShapeDtypeStruct((n_pairs, 2 * qc, p2), BF16),
            jax.ShapeDtypeStruct((SSM_GROUPS, qc, qc), BF16),
            jax.ShapeDtypeStruct((n_pairs, p2, 2 * qc), BF16),
            jax.ShapeDtypeStruct((n_pairs, p2, 2 * qc), BF16),
        ],
        compiler_params=_cparams(("parallel",)),
        name="s5_operators",
    )(*tables)


def _s5_tables(log_mag, phase, bbar_r, bbar_i, c_re, c_im, q_max):
    k = jnp.arange(1, q_max + 1, dtype=F32)[None, None, :]
    mag = jnp.exp(log_mag[:, :, None] * k)
    cos, sin = jnp.cos(phase[:, :, None] * k), jnp.sin(phase[:, :, None] * k)
    tabs = (bbar_r, bbar_i, c_re.astype(F32).transpose(0, 2, 1), c_im.astype(F32).transpose(0, 2, 1),
            mag * cos, mag * sin, cos / mag, -sin / mag)
    return tuple(t.reshape(SSM_GROUPS // 2, 2 * SSM_STATE, SSM_GROUP_CH) for t in tabs)


def _rope_tables(pos):
    half = HEAD_DIM // 2
    inv = 1.0 / (ROPE_THETA ** (jnp.arange(half, dtype=F32) * (2.0 / HEAD_DIM)))
    ang = pos.astype(F32)[:, None] * inv[None, :]
    cos, sin = jnp.cos(ang), jnp.sin(ang)
    return jnp.tile(cos, (1, 4)), jnp.concatenate([-sin, sin, -sin, sin], axis=1)


def _router_weights(gain, w_rg, b_rg, w_re, b_re):
    n_layers = gain.shape[0]
    pad = LANES - N_EXPERTS - N_EXPERT_GROUPS
    w = jnp.concatenate([w_re, w_rg, jnp.zeros((n_layers, D_MODEL, pad), F32)], axis=2)
    b = jnp.concatenate([b_re, b_rg, jnp.zeros((n_layers, pad), F32)], axis=1)
    wh = w.astype(BF16)
    wl = (w - wh.astype(F32)).astype(BF16)
    return [(gain[l].reshape(1, -1).astype(F32), wh[l], wl[l], b[l].reshape(1, LANES))
            for l in range(n_layers)]


def kernel(x_prompt, x_sample, cache_k, cache_v, state_ssm_re, state_ssm_im, norm_mix, norm_ffn, norm_final, attn_w_qkv, attn_b_qkv, attn_w_o, attn_b_o, attn_sinks, ssm_a_re, ssm_a_im, ssm_log_dt, ssm_b_re, ssm_b_im, ssm_c_re, ssm_c_im, ssm_d, ssm_w_glu_a, ssm_w_glu_b, moe_w_router_group, moe_b_router_group, moe_w_router_expert, moe_b_router_expert, moe_w_gate, moe_w_up, moe_w_down):
    bsz, seq, _ = x_prompt.shape
    dbs, n_new, _ = x_sample.shape
    rows = cache_k.shape[2]
    tp, ts = bsz * seq, dbs * n_new
    cap = _moe_cap(tp + ts)
    xp = x_prompt.reshape(tp, D_MODEL)
    xs = x_sample.reshape(ts, D_MODEL)

    row1 = lambda v: v.reshape(1, -1).astype(F32)
    routers = _router_weights(norm_ffn, moe_w_router_group, moe_b_router_group,
                              moe_w_router_expert, moe_b_router_expert)
    triu = jnp.triu(jnp.ones((TOKEN_TILE, TOKEN_TILE), F32), 1).astype(BF16)
    cnt0 = (jnp.zeros((LANES, LANES), F32), jnp.zeros((8, LANES), F32))

    slots = jnp.asarray(HEAD_SLOTS, dtype=I32)

    def q_slots(w):
        lead = w.shape[:-1]
        qh = w[..., :Q_DIM].reshape(*lead, N_HEADS, HEAD_DIM)[..., slots, :].reshape(*lead, Q_DIM)
        return jnp.concatenate([qh, w[..., Q_DIM:]], axis=-1)

    wqkv = q_slots(attn_w_qkv[0]).astype(BF16)
    bqkv = row1(q_slots(attn_b_qkv[0]))
    wo = attn_w_o[0].reshape(N_HEADS, HEAD_DIM, D_MODEL)[slots].reshape(Q_DIM, D_MODEL).astype(BF16)
    bo = row1(attn_b_o[0])
    sinks = attn_sinks[0].astype(F32)[slots]
    g_mix0, g_mix1 = row1(norm_mix[0]), row1(norm_mix[1])
    pos = jnp.concatenate([jnp.arange(seq, dtype=I32),
                           jnp.tile(PAST_LEN + jnp.arange(n_new, dtype=I32), dbs)])
    cos_t, sin_t = _rope_tables(pos)

    qp, kp, vp = _qkv_call(xp, g_mix0, wqkv, bqkv, cos_t, sin_t, (0, seq), BF16)
    qs, ks, vs = _qkv_call(xs, g_mix0, wqkv, bqkv, cos_t, sin_t, (seq, ts), F32)
    op = _attn_prompt_call(jnp.repeat(sinks, BLOCK).reshape(1, -1), qp.reshape(bsz, seq, Q_DIM),
                           kp.reshape(bsz, seq, KV_DIM),
                           vp.reshape(bsz, seq, KV_DIM))
    os_, nks, nvs = _attn_sample_call(jnp.repeat(sinks, n_new).reshape(1, -1), qs, ks, vs,
                                      cache_k[0].reshape(dbs, rows, KV_DIM),
                                      cache_v[0].reshape(dbs, rows, KV_DIM), n_new, 8)
    xp1, dest_p, cnt, rows_x, _ = _oproj_call(op.reshape(tp, Q_DIM), xp, wo, bo, routers[0], triu,
                                              cnt0, None, False, cap)
    xs1, dest_s, cnt, rows_x, steps = _oproj_call(os_, xs, wo, bo, routers[0], triu, cnt, rows_x,
                                                  True, cap)
    rows_y = _expert_call(steps, rows_x, moe_w_gate, moe_w_up, moe_w_down, 0, cap)
    xp2, up = _combine_call(dest_p, xp1, rows_y, g_mix1, True)
    xs2, us = _combine_call(dest_s, xs1, rows_y, g_mix1, True)

    disc = _s5_discretize(ssm_a_re[0], ssm_a_im[0], ssm_log_dt[0], ssm_b_re[0], ssm_b_im[0])
    tables = _s5_tables(*disc, ssm_c_re[0], ssm_c_im[0], S5_CHUNK)
    a_pow = lambda q: [tables[i][:, :, q - 1].reshape(1, -1) for i in (4, 5)]
    w_p = (*_s5_chunk_weights(tables, S5_CHUNK), *a_pow(S5_CHUNK))
    w_s = (*_s5_chunk_weights(tables, n_new), *a_pow(n_new))
    y_p, hpr, hpi = _s5_prompt_call(up.reshape(bsz, seq, D_MODEL), w_p)
    y_p = y_p.reshape(tp, D_MODEL)
    h0r = state_ssm_re[0].reshape(dbs, -1).astype(F32)
    h0i = state_ssm_im[0].reshape(dbs, -1).astype(F32)
    y_s, hsr, hsi = _s5_sample_call(us, h0r, h0i, w_s, n_new)

    wa, wb = ssm_w_glu_a[0].astype(BF16), ssm_w_glu_b[0].astype(BF16)
    d_row = row1(ssm_d[0])
    xp3, dest_p, cnt, rows_x, _ = _glu_call(y_p, xp2, g_mix1, d_row, wa, wb, routers[1], triu,
                                            cnt0, None, False, cap)
    xs3, dest_s, cnt, rows_x, steps = _glu_call(y_s, xs2, g_mix1, d_row, wa, wb, routers[1], triu,
                                                cnt, rows_x, True, cap)
    rows_y = _expert_call(steps, rows_x, moe_w_gate, moe_w_up, moe_w_down, 1, cap)
    g_fin = row1(norm_final)
    (yp,) = _combine_call(dest_p, xp3, rows_y, g_fin, False)
    (ys,) = _combine_call(dest_s, xs3, rows_y, g_fin, False)

    kv5 = lambda a, n: a.reshape(1, n, rows, KV_HEADS, HEAD_DIM)
    st4 = lambda a, n: a.reshape(1, n, SSM_GROUPS, SSM_STATE)
    k_last = kp.reshape(bsz, seq, KV_DIM)[:, seq - WINDOW:]
    v_last = vp.reshape(bsz, seq, KV_DIM)[:, seq - WINDOW:]
    return (yp.reshape(bsz, seq, D_MODEL), ys.reshape(dbs, n_new, D_MODEL),
            k_last.reshape(1, bsz, WINDOW, KV_HEADS, HEAD_DIM), kv5(nks, dbs),
            v_last.reshape(1, bsz, WINDOW, KV_HEADS, HEAD_DIM), kv5(nvs, dbs),
            st4(hpr, bsz), st4(hsr, dbs), st4(hpi, bsz), st4(hsi, dbs))
```

```python
import functools
import math

import jax
import jax.numpy as jnp
from jax import lax
from jax.experimental import pallas as pl
from jax.experimental.pallas import tpu as pltpu

F32 = jnp.float32
BF16 = jnp.bfloat16
I32 = jnp.int32

D_MODEL = 1024
N_HEADS = 16
KV_HEADS = 4
HEAD_DIM = 64
Q_DIM = N_HEADS * HEAD_DIM
KV_DIM = KV_HEADS * HEAD_DIM
QKV_DIM = Q_DIM + 2 * KV_DIM
WINDOW = 128
BLOCK = 128
ROPE_THETA = 10000.0
PAST_LEN = 16384
SSM_GROUP_CH = 16
SSM_GROUPS = D_MODEL // SSM_GROUP_CH
SSM_STATE = 64
N_EXPERT_GROUPS = 4
EXPERTS_PER_GROUP = 8
N_EXPERTS = N_EXPERT_GROUPS * EXPERTS_PER_GROUP
EXPERT_FF = 128
NORM_EPS = 1e-5

LANES = 128
VMEM_LIMIT = 56 * 1024 * 1024
S5_CHUNK = 16
S5_GB = 8
TOKEN_TILE = 512
MOE_TILE = TOKEN_TILE
PAY_X = D_MODEL
PAY_W = PAY_X + LANES


def _cparams(sem):
    return pltpu.CompilerParams(dimension_semantics=sem, vmem_limit_bytes=VMEM_LIMIT)


def _rms(x, g):
    return x * lax.rsqrt(jnp.mean(x * x, axis=-1, keepdims=True) + NORM_EPS) * g


def _dot(a, b):
    return jnp.dot(a, b, preferred_element_type=F32)


def _dot_nt(a, b):
    return lax.dot_general(a, b, (((1,), (1,)), ((), ())), preferred_element_type=F32)


def _sigmoid(x):
    return 1.0 / (1.0 + jnp.exp(-x))


def _qkv_kernel(x_ref, g_ref, w_ref, b_ref, cos_ref, sin_ref, q_ref, k_ref, v_ref):
    xn = _rms(x_ref[...], g_ref[...])
    qkv = _dot(xn.astype(BF16), w_ref[...]) + b_ref[...]
    cos = cos_ref[...]
    sin = sin_ref[...]
    lane = lax.broadcasted_iota(I32, cos.shape, 1)
    first_half = (lane % HEAD_DIM) < (HEAD_DIM // 2)
    n_rot = (Q_DIM + KV_DIM) // LANES
    for c in range(n_rot):
        blk = qkv[:, c * LANES:(c + 1) * LANES]
        partner = jnp.where(first_half,
                            pltpu.roll(blk, LANES - HEAD_DIM // 2, 1),
                            pltpu.roll(blk, HEAD_DIM // 2, 1))
        rot = blk * cos + partner * sin
        if c < Q_DIM // LANES:
            q_ref[:, c * LANES:(c + 1) * LANES] = (rot * (HEAD_DIM ** -0.5)).astype(q_ref.dtype)
        else:
            k_ref[:, c * LANES - Q_DIM:(c + 1) * LANES - Q_DIM] = rot
    v_ref[...] = qkv[:, Q_DIM + KV_DIM:]


def _qkv_call(x2d, gain, w_bf, bias, cos_t, sin_t, pos_rows, q_dtype):
    t = x2d.shape[0]
    tm = TOKEN_TILE
    first, n_pos = pos_rows[0] // tm, pos_rows[1] // tm
    return pl.pallas_call(
        _qkv_kernel,
        grid=(t // tm,),
        in_specs=[
            pl.BlockSpec((tm, D_MODEL), lambda i: (i, 0)),
            pl.BlockSpec((1, D_MODEL), lambda i: (0, 0)),
            pl.BlockSpec((D_MODEL, QKV_DIM), lambda i: (0, 0)),
            pl.BlockSpec((1, QKV_DIM), lambda i: (0, 0)),
            pl.BlockSpec((tm, LANES), lambda i: (first + i % n_pos, 0)),
            pl.BlockSpec((tm, LANES), lambda i: (first + i % n_pos, 0)),
        ],
        out_specs=[
            pl.BlockSpec((tm, Q_DIM), lambda i: (i, 0)),
            pl.BlockSpec((tm, KV_DIM), lambda i: (i, 0)),
            pl.BlockSpec((tm, KV_DIM), lambda i: (i, 0)),
        ],
        out_shape=[
            jax.ShapeDtypeStruct((t, Q_DIM), q_dtype),
            jax.ShapeDtypeStruct((t, KV_DIM), F32),
            jax.ShapeDtypeStruct((t, KV_DIM), F32),
        ],
        compiler_params=_cparams(("parallel",)),
        name="qkv_rope",
    )(x2d, gain, w_bf, bias, cos_t, sin_t)


HEAD_SLOTS = tuple(8 * c + 4 * half + i for c in range(2) for i in range(4) for half in range(2))
LOG2E = math.log2(math.e)


def _attn_prompt_kernel(sink_ref, q_ref, kp_ref, kc_ref, vp_ref, vc_ref, o_ref):
    n = pl.program_id(1)
    k2 = jnp.concatenate([kp_ref[0], kc_ref[0]], axis=0).astype(BF16)
    v2 = jnp.concatenate([vp_ref[0], vc_ref[0]], axis=0).astype(BF16)
    key = lax.broadcasted_iota(I32, (2 * BLOCK, 2 * BLOCK), 0)
    qry = lax.broadcasted_iota(I32, (2 * BLOCK, 2 * BLOCK), 1) % BLOCK
    mask = (key >= qry) & (key <= qry + WINDOW) & ((n > 0) | (key >= BLOCK))
    low = lax.broadcasted_iota(I32, (BLOCK, LANES), 1) < HEAD_DIM
    keep_low = low.astype(BF16)
    keep_high = 1 - keep_low
    n_pairs = N_HEADS // 2
    chunk = lambda pr: slice((pr // (n_pairs // 2)) * LANES, (pr // (n_pairs // 2) + 1) * LANES)

    def scores(pr):
        qblk = q_ref[0, :, pr * LANES:(pr + 1) * LANES]
        qz = jnp.concatenate([qblk * keep_low, qblk * keep_high], axis=0)
        s = _dot_nt(k2[:, chunk(pr)], qz) * LOG2E
        return jnp.where(mask, s, -jnp.inf)

    def weights(pr, s):
        sink = sink_ref[:, 2 * pr * BLOCK:(2 * pr + 2) * BLOCK] * LOG2E
        m = jnp.maximum(jnp.max(s, axis=0, keepdims=True), sink)
        p = jnp.exp2(s - m)
        denom = jnp.sum(p, axis=0, keepdims=True) + jnp.exp2(sink - m)
        return (p * (1.0 / denom)).T.astype(BF16)

    def finish(pr, w):
        oz = _dot(w, v2[:, chunk(pr)])
        o_ref[0, :, pr * LANES:(pr + 1) * LANES] = jnp.where(low, oz[:BLOCK], oz[BLOCK:]).astype(o_ref.dtype)

    all_s = [scores(pr) for pr in range(n_pairs)]
    all_w = [weights(pr, s) for pr, s in enumerate(all_s)]
    for pr, w in enumerate(all_w):
        finish(pr, w)


def _attn_prompt_call(sinks, q, k, v):
    b, l, _ = q.shape
    nb = l // BLOCK
    kv_cur = pl.BlockSpec((1, BLOCK, KV_DIM), lambda i, n: (i, n, 0))
    kv_prev = pl.BlockSpec((1, BLOCK, KV_DIM), lambda i, n: (i, jnp.maximum(n - 1, 0), 0))
    return pl.pallas_call(
        _attn_prompt_kernel,
        grid=(b, nb),
        in_specs=[
            pl.BlockSpec((1, N_HEADS * BLOCK), lambda i, n: (0, 0)),
            pl.BlockSpec((1, BLOCK, Q_DIM), lambda i, n: (i, n, 0)),
            kv_prev, kv_cur, kv_prev, kv_cur,
        ],
        out_specs=pl.BlockSpec((1, BLOCK, Q_DIM), lambda i, n: (i, n, 0)),
        out_shape=jax.ShapeDtypeStruct((b, l, Q_DIM), BF16),
        compiler_params=_cparams(("parallel", "parallel")),
        name="attn_prompt",
    )(sinks, q, k, k, v, v)


def _attn_sample_kernel(n_new, seqs, sink_ref, q_ref, kn_ref, vn_ref, ck_ref, cv_ref,
                        o_ref, nk_ref, nv_ref):
    rows = ck_ref.shape[1]
    keys = 2 * rows
    n_cols = N_HEADS * n_new
    key = lax.broadcasted_iota(I32, (keys, n_cols), 0)
    qry = lax.broadcasted_iota(I32, (keys, n_cols), 1) % n_new
    mask = ((key < rows) & (key >= qry)) | ((key >= rows) & (key - rows <= qry))
    low = lax.broadcasted_iota(I32, (n_new, LANES), 1) < HEAD_DIM
    sink = sink_ref[...]
    pad = jnp.zeros((rows - n_new, KV_DIM), F32)

    def body(sb, carry):
        r0 = pl.multiple_of(sb * n_new, n_new)
        k_new = kn_ref[pl.ds(r0, n_new), :]
        v_new = vn_ref[pl.ds(r0, n_new), :]
        k_all = jnp.concatenate([ck_ref[sb], k_new, pad], axis=0).astype(BF16)
        v_all = jnp.concatenate([cv_ref[sb], v_new, pad], axis=0).astype(BF16)
        blocks = []
        for slot in range(N_HEADS):
            pair = q_ref[pl.ds(r0, n_new), (slot // 2) * LANES:(slot // 2 + 1) * LANES]
            blk = jnp.where(low if slot % 2 == 0 else ~low, pair, 0.0)
            zero = jnp.zeros_like(blk)
            blocks.append(jnp.concatenate([blk, zero] if slot < N_HEADS // 2 else [zero, blk], axis=1))
        qz = jnp.concatenate(blocks, axis=0).astype(BF16)
        s = _dot_nt(k_all, qz)
        s = jnp.where(mask, s, -jnp.inf)
        m = jnp.maximum(jnp.max(s, axis=0, keepdims=True), sink)
        p = jnp.exp(s - m)
        denom = jnp.sum(p, axis=0, keepdims=True) + jnp.exp(sink - m)
        w = (p * (1.0 / denom)).T.astype(BF16)
        oz = _dot(w, v_all)
        for pr in range(N_HEADS // 2):
            lanes = slice((pr // (N_HEADS // 4)) * LANES, (pr // (N_HEADS // 4) + 1) * LANES)
            a = oz[2 * pr * n_new:(2 * pr + 1) * n_new, lanes]
            b = oz[(2 * pr + 1) * n_new:(2 * pr + 2) * n_new, lanes]
            o_ref[pl.ds(r0, n_new), pr * LANES:(pr + 1) * LANES] = jnp.where(low, a, b)
        nk_ref[sb, 0:rows - n_new, :] = ck_ref[sb, n_new:rows, :]
        nk_ref[sb, rows - n_new:rows, :] = k_new
        nv_ref[sb, 0:rows - n_new, :] = cv_ref[sb, n_new:rows, :]
        nv_ref[sb, rows - n_new:rows, :] = v_new
        return carry

    lax.fori_loop(0, seqs, body, 0)


def _attn_sample_call(sink_row, q, k_new, v_new, cache_k, cache_v, n_new, seqs):
    b, rows, _ = cache_k.shape
    tok = pl.BlockSpec((seqs * n_new, Q_DIM), lambda i: (i, 0))
    tok_kv = pl.BlockSpec((seqs * n_new, KV_DIM), lambda i: (i, 0))
    cache = pl.BlockSpec((seqs, rows, KV_DIM), lambda i: (i, 0, 0))
    return pl.pallas_call(
        functools.partial(_attn_sample_kernel, n_new, seqs),
        grid=(b // seqs,),
        in_specs=[pl.BlockSpec((1, N_HEADS * n_new), lambda i: (0, 0)), tok, tok_kv, tok_kv, cache, cache],
        out_specs=[tok, cache, cache],
        out_shape=[
            jax.ShapeDtypeStruct((b * n_new, Q_DIM), F32),
            jax.ShapeDtypeStruct((b, rows, KV_DIM), F32),
            jax.ShapeDtypeStruct((b, rows, KV_DIM), F32),
        ],
        compiler_params=_cparams(("parallel",)),
        name="attn_sample",
    )(sink_row, q, k_new, v_new, cache_k, cache_v)


def _moe_cap(t_total):
    return t_total + MOE_TILE


def _expert_steps(cap):
    return cap // MOE_TILE - 1 + N_EXPERT_GROUPS


def _write_step_table(maps_ref, counts, cap):
    per = cap // MOE_TILE
    shift = MOE_TILE.bit_length() - 1
    ends = []
    for c in counts:
        tiles = lax.shift_right_logical(c + (MOE_TILE - 1), shift)
        ends.append(tiles if not ends else ends[-1] + tiles)
    total = ends[-1]
    for j in range(maps_ref.shape[1]):
        jj = jnp.maximum(jnp.minimum(j, total - 1), 0)
        g = sum((jj >= e).astype(I32) for e in ends[:-1])
        start = sum(jnp.where(g > k, ends[k] - (ends[k - 1] if k else 0), 0) for k in range(len(ends) - 1))
        blk_in = g * per + jj - start
        valid = (total > j).astype(I32)
        maps_ref[0, j] = blk_in
        maps_ref[1, j] = jnp.where(valid == 1, blk_in, len(counts) * per)
        maps_ref[2, j] = g
        maps_ref[3, j] = valid


def _route_scatter(final, cap, x1, gn_ref, wrh_ref, wrl_ref, br_ref, triu_ref, cntc_in, cntr_in,
                   dest_ref, cntc_out, cntr_out, xs_hbm, maps_ref, pay, zrows, dest_v, dest_s, cntc, cntr,
                   sem):
    i = pl.program_id(0)
    n = pl.num_programs(0)
    slot = i % 2
    tm = x1.shape[0]

    @pl.when(i == 0)
    def _():
        cntc[...] = cntc_in[...]
        cntr[...] = cntr_in[...]

    xn = _rms(x1, gn_ref[...])
    xh = xn.astype(BF16)
    xl = (xn - xh.astype(F32)).astype(BF16)
    logits = _dot(xh, wrh_ref[...]) + _dot(xl, wrh_ref[...]) + _dot(xh, wrl_ref[...]) + br_ref[...]
    lt = logits.T
    ge = N_EXPERTS // EXPERTS_PER_GROUP
    sub = lax.broadcasted_iota(I32, (EXPERTS_PER_GROUP, tm), 0).astype(F32)
    big = jnp.float32(LANES)
    neg = -jnp.inf
    gl = jnp.where(sub < N_EXPERT_GROUPS, lt[N_EXPERTS:N_EXPERTS + EXPERTS_PER_GROUP], neg)
    gmax = jnp.max(gl, axis=0, keepdims=True)
    g_val = 1.0 / jnp.sum(jnp.exp(gl - gmax), axis=0, keepdims=True)
    g_idx = jnp.min(jnp.where(gl == gmax, sub, big), axis=0, keepdims=True)
    el = lt[0:EXPERTS_PER_GROUP]
    for g in range(1, ge):
        el = jnp.where(g_idx == g, lt[g * EXPERTS_PER_GROUP:(g + 1) * EXPERTS_PER_GROUP], el)
    e1 = jnp.max(el, axis=0, keepdims=True)
    i1 = jnp.min(jnp.where(el == e1, sub, big), axis=0, keepdims=True)
    el2 = jnp.where(sub == i1, neg, el)
    e2 = jnp.max(el2, axis=0, keepdims=True)
    i2 = jnp.min(jnp.where(el2 == e2, sub, big), axis=0, keepdims=True)
    t = jnp.exp(e2 - e1)
    w1 = 1.0 / (1.0 + t)
    w2 = t / (1.0 + t)
    wts_t = g_val * (jnp.where(sub == i1, w1, 0.0) + jnp.where(sub == i2, w2, 0.0))
    wts = jnp.concatenate([wts_t, jnp.zeros((LANES - EXPERTS_PER_GROUP, tm), F32)], axis=0).T

    def wait_slot(s):
        pltpu.make_async_copy(pay.at[s], xs_hbm.at[pl.ds(0, tm), :], sem.at[s]).wait()

    @pl.when(i >= 2)
    def _():
        wait_slot(slot)

    pay[slot, :, :PAY_X] = xn
    pay[slot, :, PAY_X:] = wts

    grp_t = lax.broadcasted_iota(I32, (LANES, tm), 0).astype(F32)
    oht = (grp_t == g_idx).astype(F32)
    rank = _dot(oht.astype(BF16), triu_ref[...])
    grp = lax.broadcasted_iota(I32, (LANES, 1), 0).astype(F32)
    base = grp * float(cap) + cntc[:, 0:1]
    dest = jnp.sum(oht * (rank + base), axis=0, keepdims=True).astype(I32)
    cntc[...] = cntc[...] + jnp.sum(oht, axis=1, keepdims=True)
    dest_ref[0] = dest
    dest_v[...] = dest
    pltpu.sync_copy(dest_v, dest_s)

    for s in range(2):
        @pl.when(slot == s)
        def _():
            for r in range(tm):
                pltpu.make_async_copy(pay.at[s, pl.ds(r, 1), :],
                                      xs_hbm.at[pl.ds(dest_s[0, r], 1), :], sem.at[s]).start()

    @pl.when(i == n - 1)
    def _():
        cntc_out[...] = cntc[...]
        cntr[...] = cntc[...].T[0:8, :]
        cntr_out[...] = cntr[...]

        @pl.when(n >= 2)
        def _():
            wait_slot(1 - slot)

        wait_slot(slot)
        dest_v[:, 0:LANES] = cntr[0:1, :].astype(I32)
        pltpu.sync_copy(dest_v, dest_s)
        counts = [dest_s[0, g] for g in range(N_EXPERT_GROUPS)]
        _write_step_table(maps_ref, counts, cap)
        if final:
            zrows[...] = jnp.zeros_like(zrows)
            starts = []
            for g in range(N_EXPERT_GROUPS):
                c_g = counts[g]
                starts.append(pl.multiple_of(g * cap + lax.shift_left(lax.shift_right_logical(c_g + 7, 3), 3), 8))
                for k in range(7):
                    pltpu.make_async_copy(zrows.at[pl.ds(0, 1), :],
                                          xs_hbm.at[pl.ds(g * cap + c_g + k, 1), :], sem.at[1]).start()
            for _ in range(7 * N_EXPERT_GROUPS):
                pltpu.make_async_copy(zrows.at[pl.ds(0, 1), :], xs_hbm.at[pl.ds(0, 1), :], sem.at[1]).wait()
            for g in range(N_EXPERT_GROUPS):
                pltpu.make_async_copy(zrows, xs_hbm.at[pl.ds(starts[g], MOE_TILE), :], sem.at[0]).start()
            for g in range(N_EXPERT_GROUPS):
                pltpu.make_async_copy(zrows, xs_hbm.at[pl.ds(0, MOE_TILE), :], sem.at[0]).wait()


def _oproj_kernel(has_prev, final, cap, *refs):
    (o_ref, x_ref, wo_ref, bo_ref) = refs[:4]
    rest = refs[4:]
    if has_prev:
        rest = rest[:7] + rest[8:]
    x1 = x_ref[...] + _dot(o_ref[...].astype(BF16), wo_ref[...]) + bo_ref[...]
    x1_ref = rest[7]
    x1_ref[...] = x1
    _route_scatter(final, cap, x1, *rest[:7], *rest[8:])


def _gelu_tanh(x):
    return x * (0.5 * (1.0 + jnp.tanh(math.sqrt(2.0 / math.pi) * (x + 0.044715 * (x * x * x)))))


def _glu_kernel(has_prev, final, cap, *refs):
    (y_ref, x_ref, gm_ref, d_ref, wa_ref, wb_ref) = refs[:6]
    rest = refs[6:]
    if has_prev:
        rest = rest[:7] + rest[8:]
    x = x_ref[...]
    u = _rms(x, gm_ref[...])
    z = _gelu_tanh(y_ref[...] + d_ref[...] * u).astype(BF16)
    x1 = x + _dot(z, wa_ref[...]) * _sigmoid(_dot(z, wb_ref[...]))
    x1_ref = rest[7]
    x1_ref[...] = x1
    _route_scatter(final, cap, x1, *rest[:7], *rest[8:])


def _row_spec(tm, width):
    return pl.BlockSpec((tm, width), lambda i: (i, 0))


def _const_spec(shape):
    return pl.BlockSpec(shape, lambda i: (0,) * len(shape))


def _mixer_call(body, name, lead_args, lead_specs, x2d, router, triu, cnt, xs_prev, final, cap):
    t = x2d.shape[0]
    tm = TOKEN_TILE
    n_tiles = t // tm
    gn, wrh, wrl, br = router
    cntc, cntr = cnt
    has_prev = xs_prev is not None
    in_specs = lead_specs + [
        _const_spec((1, D_MODEL)), _const_spec((D_MODEL, LANES)), _const_spec((D_MODEL, LANES)),
        _const_spec((1, LANES)), _const_spec((tm, tm)), _const_spec((LANES, LANES)), _const_spec((8, LANES)),
    ]
    args = list(lead_args) + [gn, wrh, wrl, br, triu, cntc, cntr]
    aliases = {}
    if has_prev:
        in_specs.append(pl.BlockSpec(memory_space=pl.ANY))
        args.append(xs_prev)
        aliases = {len(args) - 1: 4}
    x1, dest, cntc2, cntr2, xs, maps = pl.pallas_call(
        functools.partial(body, has_prev, final, cap),
        grid=(n_tiles,),
        in_specs=in_specs,
        out_specs=[
            _row_spec(tm, D_MODEL),
            pl.BlockSpec((1, 1, tm), lambda i: (i, 0, 0)),
            _const_spec((LANES, LANES)), _const_spec((8, LANES)),
            pl.BlockSpec(memory_space=pl.ANY),
            pl.BlockSpec(memory_space=pltpu.SMEM),
        ],
        out_shape=[
            jax.ShapeDtypeStruct((t, D_MODEL), F32),
            jax.ShapeDtypeStruct((n_tiles, 1, tm), I32),
            jax.ShapeDtypeStruct((LANES, LANES), F32),
            jax.ShapeDtypeStruct((8, LANES), F32),
            jax.ShapeDtypeStruct((N_EXPERT_GROUPS * cap, PAY_W), F32),
            jax.ShapeDtypeStruct((4, _expert_steps(cap)), I32),
        ],
        scratch_shapes=[
            pltpu.VMEM((2, tm, PAY_W), F32),
            pltpu.VMEM((MOE_TILE, PAY_W), F32),
            pltpu.VMEM((1, tm), I32),
            pltpu.SMEM((1, tm), I32),
            pltpu.VMEM((LANES, LANES), F32),
            pltpu.VMEM((8, LANES), F32),
            pltpu.SemaphoreType.DMA((2,)),
        ],
        input_output_aliases=aliases,
        compiler_params=_cparams(("arbitrary",)),
        name=name,
    )(*args)
    return x1, dest, (cntc2, cntr2), xs, maps


def _oproj_call(o2d, x2d, wo_bf, bo, router, triu, cnt, xs_prev, final, cap):
    tm = TOKEN_TILE
    lead_specs = [_row_spec(tm, Q_DIM), _row_spec(tm, D_MODEL), _const_spec((Q_DIM, D_MODEL)),
                  _const_spec((1, D_MODEL))]
    return _mixer_call(_oproj_kernel, "oproj_route", [o2d, x2d, wo_bf, bo], lead_specs, x2d,
                       router, triu, cnt, xs_prev, final, cap)


def _glu_call(y2d, x2d, gm, d, wa_bf, wb_bf, router, triu, cnt, xs_prev, final, cap):
    tm = TOKEN_TILE
    lead_specs = [_row_spec(tm, D_MODEL), _row_spec(tm, D_MODEL), _const_spec((1, D_MODEL)),
                  _const_spec((1, D_MODEL)), _const_spec((D_MODEL, D_MODEL)),
                  _const_spec((D_MODEL, D_MODEL))]
    return _mixer_call(_glu_kernel, "glu_route", [y2d, x2d, gm, d, wa_bf, wb_bf], lead_specs, x2d,
                       router, triu, cnt, xs_prev, final, cap)


def _expert_kernel(maps_ref, xs_ref, wg_ref, wu_ref, wd_ref, ys_ref, wgu, wdn, hid):
    j = pl.program_id(0)
    f = EXPERT_FF
    changed = (j == 0) | (maps_ref[2, j] != maps_ref[2, jnp.maximum(j - 1, 0)])
    valid = maps_ref[3, j]

    @pl.when(changed)
    def _():
        for e in range(EXPERTS_PER_GROUP):
            wgu[e, :, :f] = wg_ref[e].astype(BF16)
            wgu[e, :, f:] = wu_ref[e].astype(BF16)
            wdn[e * f:(e + 1) * f, :] = wd_ref[e].astype(BF16)

    @pl.when(valid == 1)
    def _():
        x = xs_ref[:, :PAY_X].astype(BF16)
        wts = xs_ref[:, PAY_X:]
        for e in range(EXPERTS_PER_GROUP):
            gu = _dot(x, wgu[e])
            g, u = gu[:, :f], gu[:, f:]
            hid[:, e * f:(e + 1) * f] = ((g * _sigmoid(g)) * u * wts[:, e:e + 1]).astype(BF16)
        ys_ref[...] = _dot(hid[...], wdn[...])

    @pl.when(valid == 0)
    def _():
        ys_ref[...] = jnp.zeros_like(ys_ref)


def _expert_call(maps, xs, w_gate, w_up, w_down, layer, cap):
    n_steps = maps.shape[1]
    e, f = EXPERTS_PER_GROUP, EXPERT_FF
    w_gate = w_gate.reshape(-1, D_MODEL, f)
    w_up = w_up.reshape(-1, D_MODEL, f)
    w_down = w_down.reshape(-1, f, D_MODEL)
    w_blk = lambda j, m: (m[2, j] + layer * N_EXPERT_GROUPS, 0, 0)
    grid_spec = pltpu.PrefetchScalarGridSpec(
        num_scalar_prefetch=1,
        grid=(n_steps,),
        in_specs=[
            pl.BlockSpec((MOE_TILE, PAY_W), lambda j, m: (m[0, j], 0)),
            pl.BlockSpec((e, D_MODEL, f), w_blk),
            pl.BlockSpec((e, D_MODEL, f), w_blk),
            pl.BlockSpec((e, f, D_MODEL), w_blk),
        ],
        out_specs=pl.BlockSpec((MOE_TILE, D_MODEL), lambda j, m: (m[1, j], 0)),
        scratch_shapes=[
            pltpu.VMEM((e, D_MODEL, 2 * f), BF16),
            pltpu.VMEM((e * f, D_MODEL), BF16),
            pltpu.VMEM((MOE_TILE, e * f), BF16),
        ],
    )
    return pl.pallas_call(
        _expert_kernel,
        grid_spec=grid_spec,
        out_shape=jax.ShapeDtypeStruct((N_EXPERT_GROUPS * cap + MOE_TILE, D_MODEL), F32),
        compiler_params=_cparams(("arbitrary",)),
        name="moe_experts",
    )(maps, xs, w_gate, w_up, w_down)


def _combine_kernel(emit_x, dest_ref, x_ref, gnext_ref, ys_hbm, *rest):
    outs, (ybuf, sem) = rest[:-2], rest[-2:]
    i = pl.program_id(0)
    n_tiles = pl.num_programs(0) - 1
    slot = i % 2
    tm = x_ref.shape[0]

    for s in range(2):
        @pl.when((i < n_tiles) & (slot == s))
        def _():
            for r in range(tm):
                pltpu.make_async_copy(ys_hbm.at[pl.ds(dest_ref[0, 0, r], 1), :],
                                      ybuf.at[s, pl.ds(r, 1), :], sem.at[s]).start()

    @pl.when(i >= 1)
    def _():
        prev = 1 - slot
        pltpu.make_async_copy(ys_hbm.at[pl.ds(0, tm), :], ybuf.at[prev], sem.at[prev]).wait()
        x2 = x_ref[...] + ybuf[prev]
        normed = _rms(x2, gnext_ref[...])
        if emit_x:
            outs[0][...] = x2
            outs[1][...] = normed
        else:
            outs[0][...] = normed


def _combine_call(dest, x2d, ys, gnext, emit_x):
    t = x2d.shape[0]
    tm = TOKEN_TILE
    n_tiles = t // tm
    n_out = 2 if emit_x else 1
    done = lambda i: (jnp.maximum(i - 1, 0), 0)
    return pl.pallas_call(
        functools.partial(_combine_kernel, emit_x),
        grid=(n_tiles + 1,),
        in_specs=[
            pl.BlockSpec((1, 1, tm), lambda i: (jnp.minimum(i, n_tiles - 1), 0, 0),
                         memory_space=pltpu.SMEM),
            pl.BlockSpec((tm, D_MODEL), done), _const_spec((1, D_MODEL)),
            pl.BlockSpec(memory_space=pl.ANY),
        ],
        out_specs=[pl.BlockSpec((tm, D_MODEL), done)] * n_out,
        out_shape=[jax.ShapeDtypeStruct((t, D_MODEL), F32)] * n_out,
        scratch_shapes=[pltpu.VMEM((2, tm, D_MODEL), F32), pltpu.SemaphoreType.DMA((2,))],
        compiler_params=_cparams(("arbitrary",)),
        name="moe_combine",
    )(dest, x2d, gnext, ys)


def _s5_state_in(u_ref_val, wre_ref, wim_ref, sre_ref, sim_ref, pair_w):
    for m in range(S5_GB // 2):
        up = u_ref_val(m * pair_w, pair_w)
        sre_ref[:, m * LANES:(m + 1) * LANES] = _dot(up, wre_ref[m])
        sim_ref[:, m * LANES:(m + 1) * LANES] = _dot(up, wim_ref[m])


def _s5_outputs(u_ref_val, hre, him, m_ref, gre_ref, gim_ref, y_store, pair_w):
    gw = pair_w // 2
    for m in range(S5_GB // 2):
        hr = hre(m).astype(BF16)
        hi = him(m).astype(BF16)
        y = _dot(hr, gre_ref[m]) + _dot(hi, gim_ref[m])
        y0 = y[:, :gw] + _dot(u_ref_val(m * pair_w, gw), m_ref[2 * m])
        y1 = y[:, gw:] + _dot(u_ref_val(m * pair_w + gw, gw), m_ref[2 * m + 1])
        y_store(m * pair_w, gw, y0)
        y_store(m * pair_w + gw, gw, y1)


def _s5_flatten(load_rows, q, n, ut, uflat):
    gc = SSM_GROUP_CH
    qc = q * gc
    for s in range(q):
        ut[:, s * gc:(s + 1) * gc, :] = load_rows(s).astype(BF16).T.reshape(S5_GB, gc, n)
    for g in range(S5_GB):
        uflat[:, g * qc:(g + 1) * qc] = ut[g].T


def _s5_unflatten(yflat, yt, store_rows, q, n):
    gc = SSM_GROUP_CH
    qc = q * gc
    for g in range(S5_GB):
        yt[g] = yflat[:, g * qc:(g + 1) * qc].T
    for t in range(q):
        store_rows(t, yt[:, t * gc:(t + 1) * gc, :].reshape(S5_GB * gc, n).T)


def _s5_prompt_kernel(u_ref, wre_ref, wim_ref, m_ref, gre_ref, gim_ref, aqr_ref, aqi_ref,
                      y_ref, her_ref, hei_ref, ut, uflat, sre, sim, hre, him, yflat, yt):
    pair_w = 2 * S5_CHUNK * SSM_GROUP_CH
    n_chunks = u_ref.shape[1] // S5_CHUNK
    _s5_flatten(lambda s: u_ref[0, pl.ds(s, n_chunks, stride=S5_CHUNK), :],
                S5_CHUNK, n_chunks, ut, uflat)
    u_val = lambda off, w: uflat[:, off:off + w]
    _s5_state_in(u_val, wre_ref, wim_ref, sre, sim, pair_w)
    ar = aqr_ref[...]
    ai = aqi_ref[...]

    def step(n, carry):
        hr, hi = carry
        hre[pl.ds(n, 1), :] = hr
        him[pl.ds(n, 1), :] = hi
        sr = sre[pl.ds(n, 1), :]
        si = sim[pl.ds(n, 1), :]
        return ar * hr - ai * hi + sr, ar * hi + ai * hr + si

    zero = jnp.zeros(ar.shape, F32)
    hr, hi = lax.fori_loop(0, n_chunks, step, (zero, zero))
    her_ref[0] = hr
    hei_ref[0] = hi

    def y_store(off, w, val):
        yflat[:, off:off + w] = val

    _s5_outputs(u_val, lambda m: hre[:, m * LANES:(m + 1) * LANES],
                lambda m: him[:, m * LANES:(m + 1) * LANES],
                m_ref, gre_ref, gim_ref, y_store, pair_w)

    def store_rows(t, val):
        y_ref[0, pl.ds(t, n_chunks, stride=S5_CHUNK), :] = val

    _s5_unflatten(yflat, yt, store_rows, S5_CHUNK, n_chunks)


def _s5_sample_kernel(n_new, u_ref, h0r_ref, h0i_ref, wre_ref, wim_ref, m_ref, gre_ref, gim_ref,
                      aqr_ref, aqi_ref, y_ref, hnr_ref, hni_ref, ut, uflat, sre, sim, yflat, yt):
    pair_w = 2 * n_new * SSM_GROUP_CH
    seqs = h0r_ref.shape[0]
    _s5_flatten(lambda s: u_ref[pl.ds(s, seqs, stride=n_new), :], n_new, seqs, ut, uflat)
    u_val = lambda off, w: uflat[:, off:off + w]
    _s5_state_in(u_val, wre_ref, wim_ref, sre, sim, pair_w)
    ar = aqr_ref[...]
    ai = aqi_ref[...]
    h0r = h0r_ref[...]
    h0i = h0i_ref[...]
    hnr_ref[...] = ar * h0r - ai * h0i + sre[...]
    hni_ref[...] = ar * h0i + ai * h0r + sim[...]

    def y_store(off, w, val):
        yflat[:, off:off + w] = val

    _s5_outputs(u_val, lambda m: h0r_ref[:, m * LANES:(m + 1) * LANES],
                lambda m: h0i_ref[:, m * LANES:(m + 1) * LANES],
                m_ref, gre_ref, gim_ref, y_store, pair_w)

    def store_rows(t, val):
        y_ref[pl.ds(t, seqs, stride=n_new), :] = val

    _s5_unflatten(yflat, yt, store_rows, n_new, seqs)


def _s5_weight_specs(q, idx):
    qc = q * SSM_GROUP_CH
    np_ = S5_GB // 2
    st = S5_GB * SSM_STATE
    return [
        pl.BlockSpec((np_, 2 * qc, LANES), lambda *a: (idx(*a), 0, 0)),
        pl.BlockSpec((np_, 2 * qc, LANES), lambda *a: (idx(*a), 0, 0)),
        pl.BlockSpec((S5_GB, qc, qc), lambda *a: (idx(*a), 0, 0)),
        pl.BlockSpec((np_, LANES, 2 * qc), lambda *a: (idx(*a), 0, 0)),
        pl.BlockSpec((np_, LANES, 2 * qc), lambda *a: (idx(*a), 0, 0)),
        pl.BlockSpec((1, st), lambda *a: (0, idx(*a))),
        pl.BlockSpec((1, st), lambda *a: (0, idx(*a))),
    ]


def _s5_prompt_call(u, w):
    b, seq, _ = u.shape
    n_chunks = seq // S5_CHUNK
    gbl = S5_GB * SSM_GROUP_CH
    qc = S5_CHUNK * SSM_GROUP_CH
    st = S5_GB * SSM_STATE
    n_gb = SSM_GROUPS // S5_GB
    gb_of = lambda g, i: g
    tok = pl.BlockSpec((1, seq, gbl), lambda g, i: (i, 0, g))
    return pl.pallas_call(
        _s5_prompt_kernel,
        grid=(n_gb, b),
        in_specs=[tok] + _s5_weight_specs(S5_CHUNK, gb_of),
        out_specs=[
            tok,
            pl.BlockSpec((1, 1, st), lambda g, i: (i, 0, g)),
            pl.BlockSpec((1, 1, st), lambda g, i: (i, 0, g)),
        ],
        out_shape=[
            jax.ShapeDtypeStruct((b, seq, D_MODEL), F32),
            jax.ShapeDtypeStruct((b, 1, SSM_GROUPS * SSM_STATE), F32),
            jax.ShapeDtypeStruct((b, 1, SSM_GROUPS * SSM_STATE), F32),
        ],
        scratch_shapes=[
            pltpu.VMEM((S5_GB, qc, n_chunks), BF16),
            pltpu.VMEM((n_chunks, S5_GB * qc), BF16),
            pltpu.VMEM((n_chunks, st), F32), pltpu.VMEM((n_chunks, st), F32),
            pltpu.VMEM((n_chunks, st), F32), pltpu.VMEM((n_chunks, st), F32),
            pltpu.VMEM((n_chunks, S5_GB * qc), F32),
            pltpu.VMEM((S5_GB, qc, n_chunks), F32),
        ],
        compiler_params=_cparams(("parallel", "parallel")),
        name="s5_prompt",
    )(u, *w)


def _s5_sample_call(u2d, h0r, h0i, w, n_new):
    t = u2d.shape[0]
    b = t // n_new
    gbl = S5_GB * SSM_GROUP_CH
    qc = n_new * SSM_GROUP_CH
    st = S5_GB * SSM_STATE
    n_gb = SSM_GROUPS // S5_GB
    gb_of = lambda g: g
    state = pl.BlockSpec((b, st), lambda g: (0, g))
    tok = pl.BlockSpec((t, gbl), lambda g: (0, g))
    return pl.pallas_call(
        functools.partial(_s5_sample_kernel, n_new),
        grid=(n_gb,),
        in_specs=[tok, state, state] + _s5_weight_specs(n_new, gb_of),
        out_specs=[tok, state, state],
        out_shape=[
            jax.ShapeDtypeStruct((t, D_MODEL), F32),
            jax.ShapeDtypeStruct((b, SSM_GROUPS * SSM_STATE), F32),
            jax.ShapeDtypeStruct((b, SSM_GROUPS * SSM_STATE), F32),
        ],
        scratch_shapes=[
            pltpu.VMEM((S5_GB, qc, b), BF16),
            pltpu.VMEM((b, S5_GB * qc), BF16),
            pltpu.VMEM((b, st), F32), pltpu.VMEM((b, st), F32),
            pltpu.VMEM((b, S5_GB * qc), F32),
            pltpu.VMEM((S5_GB, qc, b), F32),
        ],
        compiler_params=_cparams(("parallel",)),
        name="s5_sample",
    )(u2d, h0r, h0i, *w)


def _s5_discretize(a_re, a_im, log_dt, b_re, b_im):
    delta = jnp.exp(log_dt.astype(F32))[:, None]
    lr, li = a_re.astype(F32), a_im.astype(F32)
    mag = jnp.exp(delta * lr)
    abar_r = mag * jnp.cos(delta * li)
    abar_i = mag * jnp.sin(delta * li)
    nr, ni = abar_r - 1.0, abar_i
    den = lr * lr + li * li
    coef_r = ((nr * lr + ni * li) / den)[..., None]
    coef_i = ((ni * lr - nr * li) / den)[..., None]
    br, bi = b_re.astype(F32), b_im.astype(F32)
    return delta * lr, delta * li, coef_r * br - coef_i * bi, coef_r * bi + coef_i * br


def _split3(x):
    x1 = x.astype(BF16)
    r1 = x - x1.astype(F32)
    x2 = r1.astype(BF16)
    x3 = (r1 - x2.astype(F32)).astype(BF16)
    return x1, x2, x3


def _s5_prep_kernel(qs, bbr_ref, bbi_ref, ctr_ref, cti_ref, ppr_ref, ppi_ref, pnr_ref, pni_ref,
                    *out_refs):
    c = SSM_GROUP_CH
    qc = max(qs) * c
    p2 = 2 * SSM_STATE
    col = lax.broadcasted_iota(I32, (c, qc), 1)
    row = lax.broadcasted_iota(I32, (c, qc), 0)
    by_ch = (col % c == row).astype(BF16)
    by_pos = (col // c == row).astype(BF16)

    def spread(x, sel):
        return sum(_dot(part, sel) for part in _split3(x))

    def dot_f32(a, b):
        a1, a2, _ = _split3(a)
        b1, b2, _ = _split3(b)
        return _dot(a1, b1) + _dot(a1, b2) + _dot(a2, b1)

    bc_r, bc_i = spread(bbr_ref[0], by_ch), spread(bbi_ref[0], by_ch)
    cc_r, cc_i = spread(ctr_ref[0], by_ch), spread(cti_ref[0], by_ch)
    pp_r, pp_i = spread(ppr_ref[0], by_pos), spread(ppi_ref[0], by_pos)
    pn_r, pn_i = spread(pnr_ref[0], by_pos), spread(pni_ref[0], by_pos)
    r_r = cc_r * pp_r - cc_i * pp_i
    r_i = cc_r * pp_i + cc_i * pp_r
    l_r = bc_r * pn_r - bc_i * pn_i
    l_i = bc_r * pn_i + bc_i * pn_r
    lt_r, lt_i = l_r.T, l_i.T
    grp_col = lax.broadcasted_iota(I32, (qc, p2), 1) // SSM_STATE
    causal =(lax.broadcasted_iota(I32, (qc, qc), 1) // c) >= (lax.broadcasted_iota(I32, (qc, qc), 0) // c)
    kers = [jnp.where(causal, dot_f32(jnp.where(grp_col == gl, lt_r, 0.0), r_r)
                      - dot_f32(jnp.where(grp_col == gl, lt_i, 0.0), r_i), 0.0) for gl in range(2)]
    for n, q in enumerate(qs):
        wre_ref, wim_ref, m_ref, gre_ref, gim_ref = out_refs[5 * n:5 * n + 5]
        w = q * c
        aq_r = ppr_ref[0][:, q - 1:q]
        aq_i = ppi_ref[0][:, q - 1:q]
        w_r = (l_r[:, :w] * aq_r - l_i[:, :w] * aq_i).T
        w_i = (l_r[:, :w] * aq_i + l_i[:, :w] * aq_r).T
        g_re, g_im = [], []
        col_grp = lax.broadcasted_iota(I32, (w, p2), 1) // SSM_STATE
        row_grp = lax.broadcasted_iota(I32, (p2, w), 0) // SSM_STATE
        for gl in range(2):
            wre_ref[0, gl * w:(gl + 1) * w, :] = jnp.where(col_grp == gl, w_r, 0.0).astype(BF16)
            wim_ref[0, gl * w:(gl + 1) * w, :] = jnp.where(col_grp == gl, w_i, 0.0).astype(BF16)
            m_ref[gl] = kers[gl][:w, :w].astype(BF16)
            g_re.append(jnp.where(row_grp == gl, r_r[:, :w], 0.0))
            g_im.append(jnp.where(row_grp == gl, -r_i[:, :w], 0.0))
        gre_ref[0] = jnp.concatenate(g_re, axis=1).astype(BF16)
        gim_ref[0] = jnp.concatenate(g_im, axis=1).astype(BF16)


def _s5_chunk_weights(tables, qs):
    n_pairs = SSM_GROUPS // 2
    p2 = 2 * SSM_STATE
    tab = pl.BlockSpec((1, p2, SSM_GROUP_CH), lambda m: (m, 0, 0))
    out_specs, out_shape = [], []
    for q in qs:
        qc = q * SSM_GROUP_CH
        out_specs += [pl.BlockSpec((1, 2 * qc, p2), lambda m: (m, 0, 0)),
                      pl.BlockSpec((1, 2 * qc, p2), lambda m: (m, 0, 0)),
                      pl.BlockSpec((2, qc, qc), lambda m: (m, 0, 0)),
                      pl.BlockSpec((1, p2, 2 * qc), lambda m: (m, 0, 0)),
                      pl.BlockSpec((1, p2, 2 * qc), lambda m: (m, 0, 0))]
        out_shape += [jax.ShapeDtypeStruct((n_pairs, 2 * qc, p2), BF16),
                      jax.ShapeDtypeStruct((n_pairs, 2 * qc, p2), BF16),
                      jax.ShapeDtypeStruct((SSM_GROUPS, qc, qc), BF16),
                      jax.ShapeDtypeStruct((n_pairs, p2, 2 * qc), BF16),
                      jax.ShapeDtypeStruct((n_pairs, p2, 2 * qc), BF16)]
    outs = pl.pallas_call(
        functools.partial(_s5_prep_kernel, tuple(qs)),
        grid=(n_pairs,),
        in_specs=[tab] * 8,
        out_specs=out_specs, out_shape=out_shape,
        compiler_params=_cparams(("parallel",)),
        name="s5_operators",
    )(*tables)
    return [tuple(outs[5 * n:5 * n + 5]) for n in range(len(qs))]


def _s5_tables(log_mag, phase, bbar_r, bbar_i, c_re, c_im, q_max):
    k = jnp.arange(1, q_max + 1, dtype=F32)[None, None, :]
    mag = jnp.exp(log_mag[:, :, None] * k)
    cos, sin = jnp.cos(phase[:, :, None] * k), jnp.sin(phase[:, :, None] * k)
    tabs = (bbar_r, bbar_i, c_re.astype(F32).transpose(0, 2, 1), c_im.astype(F32).transpose(0, 2, 1),
            mag * cos, mag * sin, cos / mag, -sin / mag)
    return tuple(t.reshape(SSM_GROUPS // 2, 2 * SSM_STATE, SSM_GROUP_CH) for t in tabs)


def _rope_tables(pos):
    half = HEAD_DIM // 2
    inv = 1.0 / (ROPE_THETA ** (jnp.arange(half, dtype=F32) * (2.0 / HEAD_DIM)))
    ang = pos.astype(F32)[:, None] * inv[None, :]
    cos, sin = jnp.cos(ang), jnp.sin(ang)
    return jnp.tile(cos, (1, 4)), jnp.concatenate([-sin, sin, -sin, sin], axis=1)


def _router_weights(gain, w_rg, b_rg, w_re, b_re):
    n_layers = gain.shape[0]
    pad = LANES - N_EXPERTS - N_EXPERT_GROUPS
    w = jnp.concatenate([w_re, w_rg, jnp.zeros((n_layers, D_MODEL, pad), F32)], axis=2)
    b = jnp.concatenate([b_re, b_rg, jnp.zeros((n_layers, pad), F32)], axis=1)
    wh = w.astype(BF16)
    wl = (w - wh.astype(F32)).astype(BF16)
    return [(gain[l].reshape(1, -1).astype(F32), wh[l], wl[l], b[l].reshape(1, LANES))
            for l in range(n_layers)]


def kernel(x_prompt, x_sample, cache_k, cache_v, state_ssm_re, state_ssm_im, norm_mix, norm_ffn, norm_final, attn_w_qkv, attn_b_qkv, attn_w_o, attn_b_o, attn_sinks, ssm_a_re, ssm_a_im, ssm_log_dt, ssm_b_re, ssm_b_im, ssm_c_re, ssm_c_im, ssm_d, ssm_w_glu_a, ssm_w_glu_b, moe_w_router_group, moe_b_router_group, moe_w_router_expert, moe_b_router_expert, moe_w_gate, moe_w_up, moe_w_down):
    bsz, seq, _ = x_prompt.shape
    dbs, n_new, _ = x_sample.shape
    rows = cache_k.shape[2]
    tp, ts = bsz * seq, dbs * n_new
    cap = _moe_cap(tp + ts)
    xp = x_prompt.reshape(tp, D_MODEL)
    xs = x_sample.reshape(ts, D_MODEL)

    row1 = lambda v: v.reshape(1, -1).astype(F32)
    routers = _router_weights(norm_ffn, moe_w_router_group, moe_b_router_group,
                              moe_w_router_expert, moe_b_router_expert)
    triu = jnp.triu(jnp.ones((TOKEN_TILE, TOKEN_TILE), F32), 1).astype(BF16)
    cnt0 = (jnp.zeros((LANES, LANES), F32), jnp.zeros((8, LANES), F32))

    slots = jnp.asarray(HEAD_SLOTS, dtype=I32)

    def q_slots(w):
        lead = w.shape[:-1]
        qh = w[..., :Q_DIM].reshape(*lead, N_HEADS, HEAD_DIM)[..., slots, :].reshape(*lead, Q_DIM)
        return jnp.concatenate([qh, w[..., Q_DIM:]], axis=-1)

    wqkv = q_slots(attn_w_qkv[0]).astype(BF16)
    bqkv = row1(q_slots(attn_b_qkv[0]))
    wo = attn_w_o[0].reshape(N_HEADS, HEAD_DIM, D_MODEL)[slots].reshape(Q_DIM, D_MODEL).astype(BF16)
    bo = row1(attn_b_o[0])
    sinks = attn_sinks[0].astype(F32)[slots]
    g_mix0, g_mix1 = row1(norm_mix[0]), row1(norm_mix[1])
    pos = jnp.concatenate([jnp.arange(seq, dtype=I32),
                           jnp.tile(PAST_LEN + jnp.arange(n_new, dtype=I32), dbs)])
    cos_t, sin_t = _rope_tables(pos)

    qp, kp, vp = _qkv_call(xp, g_mix0, wqkv, bqkv, cos_t, sin_t, (0, seq), BF16)
    qs, ks, vs = _qkv_call(xs, g_mix0, wqkv, bqkv, cos_t, sin_t, (seq, ts), F32)
    op = _attn_prompt_call(jnp.repeat(sinks, BLOCK).reshape(1, -1), qp.reshape(bsz, seq, Q_DIM),
                           kp.reshape(bsz, seq, KV_DIM),
                           vp.reshape(bsz, seq, KV_DIM))
    os_, nks, nvs = _attn_sample_call(jnp.repeat(sinks, n_new).reshape(1, -1), qs, ks, vs,
                                      cache_k[0].reshape(dbs, rows, KV_DIM),
                                      cache_v[0].reshape(dbs, rows, KV_DIM), n_new, 8)
    xp1, dest_p, cnt, rows_x, _ = _oproj_call(op.reshape(tp, Q_DIM), xp, wo, bo, routers[0], triu,
                                              cnt0, None, False, cap)
    xs1, dest_s, cnt, rows_x, steps = _oproj_call(os_, xs, wo, bo, routers[0], triu, cnt, rows_x,
                                                  True, cap)
    rows_y = _expert_call(steps, rows_x, moe_w_gate, moe_w_up, moe_w_down, 0, cap)
    xp2, up = _combine_call(dest_p, xp1, rows_y, g_mix1, True)
    xs2, us = _combine_call(dest_s, xs1, rows_y, g_mix1, True)

    disc = _s5_discretize(ssm_a_re[0], ssm_a_im[0], ssm_log_dt[0], ssm_b_re[0], ssm_b_im[0])
    tables = _s5_tables(*disc, ssm_c_re[0], ssm_c_im[0], S5_CHUNK)
    a_pow = lambda q: [tables[i][:, :, q - 1].reshape(1, -1) for i in (4, 5)]
    ops_p, ops_s = _s5_chunk_weights(tables, (S5_CHUNK, n_new))
    w_p = (*ops_p, *a_pow(S5_CHUNK))
    w_s = (*ops_s, *a_pow(n_new))
    y_p, hpr, hpi = _s5_prompt_call(up.reshape(bsz, seq, D_MODEL), w_p)
    y_p = y_p.reshape(tp, D_MODEL)
    h0r = state_ssm_re[0].reshape(dbs, -1).astype(F32)
    h0i = state_ssm_im[0].reshape(dbs, -1).astype(F32)
    y_s, hsr, hsi = _s5_sample_call(us, h0r, h0i, w_s, n_new)

    wa, wb = ssm_w_glu_a[0].astype(BF16), ssm_w_glu_b[0].astype(BF16)
    d_row = row1(ssm_d[0])
    xp3, dest_p, cnt, rows_x, _ = _glu_call(y_p, xp2, g_mix1, d_row, wa, wb, routers[1], triu,
                                            cnt0, None, False, cap)
    xs3, dest_s, cnt, rows_x, steps = _glu_call(y_s, xs2, g_mix1, d_row, wa, wb, routers[1], triu,
                                                cnt, rows_x, True, cap)
    rows_y = _expert_call(steps, rows_x, moe_w_gate, moe_w_up, moe_w_down, 1, cap)
    g_fin = row1(norm_final)
    (yp,) = _combine_call(dest_p, xp3, rows_y, g_fin, False)
    (ys,) = _combine_call(dest_s, xs3, rows_y, g_fin, False)

    kv5 = lambda a, n: a.reshape(1, n, rows, KV_HEADS, HEAD_DIM)
    st4 = lambda a, n: a.reshape(1, n, SSM_GROUPS, SSM_STATE)
    k_last = kp.reshape(bsz, seq, KV_DIM)[:, seq - WINDOW:]
    v_last = vp.reshape(bsz, seq, KV_DIM)[:, seq - WINDOW:]
    return (yp.reshape(bsz, seq, D_MODEL), ys.reshape(dbs, n_new, D_MODEL),
            k_last.reshape(1, bsz, WINDOW, KV_HEADS, HEAD_DIM), kv5(nks, dbs),
            v_last.reshape(1, bsz, WINDOW, KV_HEADS, HEAD_DIM), kv5(nvs, dbs),
            st4(hpr, bsz), st4(hsr, dbs), st4(hpi, bsz), st4(hsi, dbs))
```

```python
import functools
import math

import jax
import jax.numpy as jnp
from jax import lax
from jax.experimental import pallas as pl
from jax.experimental.pallas import tpu as pltpu

F32 = jnp.float32
BF16 = jnp.bfloat16
I32 = jnp.int32

D_MODEL = 1024
N_HEADS = 16
KV_HEADS = 4
HEAD_DIM = 64
Q_DIM = N_HEADS * HEAD_DIM
KV_DIM = KV_HEADS * HEAD_DIM
QKV_DIM = Q_DIM + 2 * KV_DIM
WINDOW = 128
BLOCK = 128
ROPE_THETA = 10000.0
PAST_LEN = 16384
SSM_GROUP_CH = 16
SSM_GROUPS = D_MODEL // SSM_GROUP_CH
SSM_STATE = 64
N_EXPERT_GROUPS = 4
EXPERTS_PER_GROUP = 8
N_EXPERTS = N_EXPERT_GROUPS * EXPERTS_PER_GROUP
EXPERT_FF = 128
NORM_EPS = 1e-5

LANES = 128
VMEM_LIMIT = 56 * 1024 * 1024
S5_CHUNK = 16
S5_GB = 8
TOKEN_TILE = 1024
MOE_TILE = 512
PAY_X = D_MODEL
PAY_W = PAY_X + LANES


def _cparams(sem):
    return pltpu.CompilerParams(dimension_semantics=sem, vmem_limit_bytes=VMEM_LIMIT)


def _rms(x, g):
    return x * lax.rsqrt(jnp.mean(x * x, axis=-1, keepdims=True) + NORM_EPS) * g


def _dot(a, b):
    return jnp.dot(a, b, preferred_element_type=F32)


def _dot_nt(a, b):
    return lax.dot_general(a, b, (((1,), (1,)), ((), ())), preferred_element_type=F32)


def _sigmoid(x):
    return 1.0 / (1.0 + jnp.exp(-x))


def _qkv_kernel(x_ref, g_ref, w_ref, b_ref, cos_ref, sin_ref, q_ref, k_ref, v_ref):
    xn = _rms(x_ref[...], g_ref[...])
    qkv = _dot(xn.astype(BF16), w_ref[...]) + b_ref[...]
    cos = cos_ref[...]
    sin = sin_ref[...]
    lane = lax.broadcasted_iota(I32, cos.shape, 1)
    first_half = (lane % HEAD_DIM) < (HEAD_DIM // 2)
    n_rot = (Q_DIM + KV_DIM) // LANES
    for c in range(n_rot):
        blk = qkv[:, c * LANES:(c + 1) * LANES]
        partner = jnp.where(first_half,
                            pltpu.roll(blk, LANES - HEAD_DIM // 2, 1),
                            pltpu.roll(blk, HEAD_DIM // 2, 1))
        rot = blk * cos + partner * sin
        if c < Q_DIM // LANES:
            q_ref[:, c * LANES:(c + 1) * LANES] = (rot * (HEAD_DIM ** -0.5)).astype(q_ref.dtype)
        else:
            k_ref[:, c * LANES - Q_DIM:(c + 1) * LANES - Q_DIM] = rot
    v_ref[...] = qkv[:, Q_DIM + KV_DIM:]


def _qkv_call(x2d, gain, w_bf, bias, cos_t, sin_t, pos_rows, q_dtype):
    t = x2d.shape[0]
    tm = TOKEN_TILE
    first, n_pos = pos_rows[0] // tm, pos_rows[1] // tm
    return pl.pallas_call(
        _qkv_kernel,
        grid=(t // tm,),
        in_specs=[
            pl.BlockSpec((tm, D_MODEL), lambda i: (i, 0)),
            pl.BlockSpec((1, D_MODEL), lambda i: (0, 0)),
            pl.BlockSpec((D_MODEL, QKV_DIM), lambda i: (0, 0)),
            pl.BlockSpec((1, QKV_DIM), lambda i: (0, 0)),
            pl.BlockSpec((tm, LANES), lambda i: (first + i % n_pos, 0)),
            pl.BlockSpec((tm, LANES), lambda i: (first + i % n_pos, 0)),
        ],
        out_specs=[
            pl.BlockSpec((tm, Q_DIM), lambda i: (i, 0)),
            pl.BlockSpec((tm, KV_DIM), lambda i: (i, 0)),
            pl.BlockSpec((tm, KV_DIM), lambda i: (i, 0)),
        ],
        out_shape=[
            jax.ShapeDtypeStruct((t, Q_DIM), q_dtype),
            jax.ShapeDtypeStruct((t, KV_DIM), F32),
            jax.ShapeDtypeStruct((t, KV_DIM), F32),
        ],
        compiler_params=_cparams(("parallel",)),
        name="qkv_rope",
    )(x2d, gain, w_bf, bias, cos_t, sin_t)


HEAD_SLOTS = tuple(8 * c + 4 * half + i for c in range(2) for i in range(4) for half in range(2))
LOG2E = math.log2(math.e)


def _attn_prompt_kernel(sink_ref, q_ref, kp_ref, kc_ref, vp_ref, vc_ref, o_ref):
    n = pl.program_id(1)
    k2 = jnp.concatenate([kp_ref[0], kc_ref[0]], axis=0).astype(BF16)
    v2 = jnp.concatenate([vp_ref[0], vc_ref[0]], axis=0).astype(BF16)
    key = lax.broadcasted_iota(I32, (2 * BLOCK, 2 * BLOCK), 0)
    qry = lax.broadcasted_iota(I32, (2 * BLOCK, 2 * BLOCK), 1) % BLOCK
    mask = (key >= qry) & (key <= qry + WINDOW) & ((n > 0) | (key >= BLOCK))
    low = lax.broadcasted_iota(I32, (BLOCK, LANES), 1) < HEAD_DIM
    keep_low = low.astype(BF16)
    keep_high = 1 - keep_low
    n_pairs = N_HEADS // 2
    chunk = lambda pr: slice((pr // (n_pairs // 2)) * LANES, (pr // (n_pairs // 2) + 1) * LANES)

    def scores(pr):
        qblk = q_ref[0, :, pr * LANES:(pr + 1) * LANES]
        qz = jnp.concatenate([qblk * keep_low, qblk * keep_high], axis=0)
        s = _dot_nt(k2[:, chunk(pr)], qz) * LOG2E
        return jnp.where(mask, s, -jnp.inf)

    def weights(pr, s):
        sink = sink_ref[:, 2 * pr * BLOCK:(2 * pr + 2) * BLOCK] * LOG2E
        m = jnp.maximum(jnp.max(s, axis=0, keepdims=True), sink)
        p = jnp.exp2(s - m)
        denom = jnp.sum(p, axis=0, keepdims=True) + jnp.exp2(sink - m)
        return (p * (1.0 / denom)).T.astype(BF16)

    def finish(pr, w):
        oz = _dot(w, v2[:, chunk(pr)])
        o_ref[0, :, pr * LANES:(pr + 1) * LANES] = jnp.where(low, oz[:BLOCK], oz[BLOCK:]).astype(o_ref.dtype)

    all_s = [scores(pr) for pr in range(n_pairs)]
    all_w = [weights(pr, s) for pr, s in enumerate(all_s)]
    for pr, w in enumerate(all_w):
        finish(pr, w)


def _attn_prompt_call(sinks, q, k, v):
    b, l, _ = q.shape
    nb = l // BLOCK
    kv_cur = pl.BlockSpec((1, BLOCK, KV_DIM), lambda i, n: (i, n, 0))
    kv_prev = pl.BlockSpec((1, BLOCK, KV_DIM), lambda i, n: (i, jnp.maximum(n - 1, 0), 0))
    return pl.pallas_call(
        _attn_prompt_kernel,
        grid=(b, nb),
        in_specs=[
            pl.BlockSpec((1, N_HEADS * BLOCK), lambda i, n: (0, 0)),
            pl.BlockSpec((1, BLOCK, Q_DIM), lambda i, n: (i, n, 0)),
            kv_prev, kv_cur, kv_prev, kv_cur,
        ],
        out_specs=pl.BlockSpec((1, BLOCK, Q_DIM), lambda i, n: (i, n, 0)),
        out_shape=jax.ShapeDtypeStruct((b, l, Q_DIM), BF16),
        compiler_params=_cparams(("parallel", "parallel")),
        name="attn_prompt",
    )(sinks, q, k, k, v, v)


def _attn_sample_kernel(n_new, seqs, sink_ref, q_ref, kn_ref, vn_ref, ck_ref, cv_ref,
                        o_ref, nk_ref, nv_ref):
    rows = ck_ref.shape[1]
    keys = 2 * rows
    n_cols = N_HEADS * n_new
    key = lax.broadcasted_iota(I32, (keys, n_cols), 0)
    qry = lax.broadcasted_iota(I32, (keys, n_cols), 1) % n_new
    mask = ((key < rows) & (key >= qry)) | ((key >= rows) & (key - rows <= qry))
    low = lax.broadcasted_iota(I32, (n_new, LANES), 1) < HEAD_DIM
    sink = sink_ref[...]
    pad = jnp.zeros((rows - n_new, KV_DIM), F32)

    def body(sb, carry):
        r0 = pl.multiple_of(sb * n_new, n_new)
        k_new = kn_ref[pl.ds(r0, n_new), :]
        v_new = vn_ref[pl.ds(r0, n_new), :]
        k_all = jnp.concatenate([ck_ref[sb], k_new, pad], axis=0).astype(BF16)
        v_all = jnp.concatenate([cv_ref[sb], v_new, pad], axis=0).astype(BF16)
        blocks = []
        for slot in range(N_HEADS):
            pair = q_ref[pl.ds(r0, n_new), (slot // 2) * LANES:(slot // 2 + 1) * LANES]
            blk = jnp.where(low if slot % 2 == 0 else ~low, pair, 0.0)
            zero = jnp.zeros_like(blk)
            blocks.append(jnp.concatenate([blk, zero] if slot < N_HEADS // 2 else [zero, blk], axis=1))
        qz = jnp.concatenate(blocks, axis=0).astype(BF16)
        s = _dot_nt(k_all, qz)
        s = jnp.where(mask, s, -jnp.inf)
        m = jnp.maximum(jnp.max(s, axis=0, keepdims=True), sink)
        p = jnp.exp(s - m)
        denom = jnp.sum(p, axis=0, keepdims=True) + jnp.exp(sink - m)
        w = (p * (1.0 / denom)).T.astype(BF16)
        oz = _dot(w, v_all)
        for pr in range(N_HEADS // 2):
            lanes = slice((pr // (N_HEADS // 4)) * LANES, (pr // (N_HEADS // 4) + 1) * LANES)
            a = oz[2 * pr * n_new:(2 * pr + 1) * n_new, lanes]
            b = oz[(2 * pr + 1) * n_new:(2 * pr + 2) * n_new, lanes]
            o_ref[pl.ds(r0, n_new), pr * LANES:(pr + 1) * LANES] = jnp.where(low, a, b)
        nk_ref[sb, 0:rows - n_new, :] = ck_ref[sb, n_new:rows, :]
        nk_ref[sb, rows - n_new:rows, :] = k_new
        nv_ref[sb, 0:rows - n_new, :] = cv_ref[sb, n_new:rows, :]
        nv_ref[sb, rows - n_new:rows, :] = v_new
        return carry

    lax.fori_loop(0, seqs, body, 0)


def _attn_sample_call(sink_row, q, k_new, v_new, cache_k, cache_v, n_new, seqs):
    b, rows, _ = cache_k.shape
    tok = pl.BlockSpec((seqs * n_new, Q_DIM), lambda i: (i, 0))
    tok_kv = pl.BlockSpec((seqs * n_new, KV_DIM), lambda i: (i, 0))
    cache = pl.BlockSpec((seqs, rows, KV_DIM), lambda i: (i, 0, 0))
    return pl.pallas_call(
        functools.partial(_attn_sample_kernel, n_new, seqs),
        grid=(b // seqs,),
        in_specs=[pl.BlockSpec((1, N_HEADS * n_new), lambda i: (0, 0)), tok, tok_kv, tok_kv, cache, cache],
        out_specs=[tok, cache, cache],
        out_shape=[
            jax.ShapeDtypeStruct((b * n_new, Q_DIM), F32),
            jax.ShapeDtypeStruct((b, rows, KV_DIM), F32),
            jax.ShapeDtypeStruct((b, rows, KV_DIM), F32),
        ],
        compiler_params=_cparams(("parallel",)),
        name="attn_sample",
    )(sink_row, q, k_new, v_new, cache_k, cache_v)


def _moe_cap(t_total):
    return t_total + MOE_TILE


def _expert_steps(cap):
    return cap // MOE_TILE - 1 + N_EXPERT_GROUPS


def _write_step_table(maps_ref, counts, cap):
    per = cap // MOE_TILE
    shift = MOE_TILE.bit_length() - 1
    ends = []
    for c in counts:
        tiles = lax.shift_right_logical(c + (MOE_TILE - 1), shift)
        ends.append(tiles if not ends else ends[-1] + tiles)
    total = ends[-1]
    for j in range(maps_ref.shape[1]):
        jj = jnp.maximum(jnp.minimum(j, total - 1), 0)
        g = sum((jj >= e).astype(I32) for e in ends[:-1])
        start = sum(jnp.where(g > k, ends[k] - (ends[k - 1] if k else 0), 0) for k in range(len(ends) - 1))
        blk_in = g * per + jj - start
        valid = (total > j).astype(I32)
        maps_ref[0, j] = blk_in
        maps_ref[1, j] = jnp.where(valid == 1, blk_in, len(counts) * per)
        maps_ref[2, j] = g
        maps_ref[3, j] = valid


def _route_scatter(final, cap, x1, gn_ref, wrh_ref, wrl_ref, br_ref, triu_ref, cntc_in, cntr_in,
                   dest_ref, cntc_out, cntr_out, xs_hbm, maps_ref, pay, zrows, dest_v, dest_s, cntc, cntr,
                   sem):
    i = pl.program_id(0)
    n = pl.num_programs(0)
    slot = i % 2
    tm = x1.shape[0]

    @pl.when(i == 0)
    def _():
        cntc[...] = cntc_in[...]
        cntr[...] = cntr_in[...]

    xn = _rms(x1, gn_ref[...])
    xh = xn.astype(BF16)
    xl = (xn - xh.astype(F32)).astype(BF16)
    logits = _dot(xh, wrh_ref[...]) + _dot(xl, wrh_ref[...]) + _dot(xh, wrl_ref[...]) + br_ref[...]
    lt = logits.T
    ge = N_EXPERTS // EXPERTS_PER_GROUP
    sub = lax.broadcasted_iota(I32, (EXPERTS_PER_GROUP, tm), 0).astype(F32)
    big = jnp.float32(LANES)
    neg = -jnp.inf
    gl = jnp.where(sub < N_EXPERT_GROUPS, lt[N_EXPERTS:N_EXPERTS + EXPERTS_PER_GROUP], neg)
    gmax = jnp.max(gl, axis=0, keepdims=True)
    g_val = 1.0 / jnp.sum(jnp.exp(gl - gmax), axis=0, keepdims=True)
    g_idx = jnp.min(jnp.where(gl == gmax, sub, big), axis=0, keepdims=True)
    el = lt[0:EXPERTS_PER_GROUP]
    for g in range(1, ge):
        el = jnp.where(g_idx == g, lt[g * EXPERTS_PER_GROUP:(g + 1) * EXPERTS_PER_GROUP], el)
    e1 = jnp.max(el, axis=0, keepdims=True)
    i1 = jnp.min(jnp.where(el == e1, sub, big), axis=0, keepdims=True)
    el2 = jnp.where(sub == i1, neg, el)
    e2 = jnp.max(el2, axis=0, keepdims=True)
    i2 = jnp.min(jnp.where(el2 == e2, sub, big), axis=0, keepdims=True)
    t = jnp.exp(e2 - e1)
    w1 = 1.0 / (1.0 + t)
    w2 = t / (1.0 + t)
    wts_t = g_val * (jnp.where(sub == i1, w1, 0.0) + jnp.where(sub == i2, w2, 0.0))
    wts = jnp.concatenate([wts_t, jnp.zeros((LANES - EXPERTS_PER_GROUP, tm), F32)], axis=0).T

    def wait_slot(s):
        pltpu.make_async_copy(pay.at[s], xs_hbm.at[pl.ds(0, tm), :], sem.at[s]).wait()

    @pl.when(i >= 2)
    def _():
        wait_slot(slot)

    pay[slot, :, :PAY_X] = xn
    pay[slot, :, PAY_X:] = wts

    grp_t = lax.broadcasted_iota(I32, (LANES, tm), 0).astype(F32)
    oht = (grp_t == g_idx).astype(F32)
    rank = _dot(oht.astype(BF16), triu_ref[...])
    grp = lax.broadcasted_iota(I32, (LANES, 1), 0).astype(F32)
    base = grp * float(cap) + cntc[:, 0:1]
    dest = jnp.sum(oht * (rank + base), axis=0, keepdims=True).astype(I32)
    cntc[...] = cntc[...] + jnp.sum(oht, axis=1, keepdims=True)
    dest_ref[0] = dest
    dest_v[...] = dest
    pltpu.sync_copy(dest_v, dest_s)

    for s in range(2):
        @pl.when(slot == s)
        def _():
            for r in range(tm):
                pltpu.make_async_copy(pay.at[s, pl.ds(r, 1), :],
                                      xs_hbm.at[pl.ds(dest_s[0, r], 1), :], sem.at[s]).start()

    @pl.when(i == n - 1)
    def _():
        cntc_out[...] = cntc[...]
        cntr[...] = cntc[...].T[0:8, :]
        cntr_out[...] = cntr[...]

        @pl.when(n >= 2)
        def _():
            wait_slot(1 - slot)

        wait_slot(slot)
        dest_v[:, 0:LANES] = cntr[0:1, :].astype(I32)
        pltpu.sync_copy(dest_v, dest_s)
        counts = [dest_s[0, g] for g in range(N_EXPERT_GROUPS)]
        _write_step_table(maps_ref, counts, cap)
        if final:
            zrows[...] = jnp.zeros_like(zrows)
            starts = []
            for g in range(N_EXPERT_GROUPS):
                c_g = counts[g]
                starts.append(pl.multiple_of(g * cap + lax.shift_left(lax.shift_right_logical(c_g + 7, 3), 3), 8))
                for k in range(7):
                    pltpu.make_async_copy(zrows.at[pl.ds(0, 1), :],
                                          xs_hbm.at[pl.ds(g * cap + c_g + k, 1), :], sem.at[1]).start()
            for _ in range(7 * N_EXPERT_GROUPS):
                pltpu.make_async_copy(zrows.at[pl.ds(0, 1), :], xs_hbm.at[pl.ds(0, 1), :], sem.at[1]).wait()
            for g in range(N_EXPERT_GROUPS):
                pltpu.make_async_copy(zrows, xs_hbm.at[pl.ds(starts[g], MOE_TILE), :], sem.at[0]).start()
            for g in range(N_EXPERT_GROUPS):
                pltpu.make_async_copy(zrows, xs_hbm.at[pl.ds(0, MOE_TILE), :], sem.at[0]).wait()


def _oproj_kernel(has_prev, final, cap, *refs):
    (o_ref, x_ref, wo_ref, bo_ref) = refs[:4]
    rest = refs[4:]
    if has_prev:
        rest = rest[:7] + rest[8:]
    x1 = x_ref[...] + _dot(o_ref[...].astype(BF16), wo_ref[...]) + bo_ref[...]
    x1_ref = rest[7]
    x1_ref[...] = x1
    _route_scatter(final, cap, x1, *rest[:7], *rest[8:])


def _gelu_tanh(x):
    return x * (0.5 * (1.0 + jnp.tanh(math.sqrt(2.0 / math.pi) * (x + 0.044715 * (x * x * x)))))


def _glu_kernel(has_prev, final, cap, *refs):
    (y_ref, x_ref, gm_ref, d_ref, wa_ref, wb_ref) = refs[:6]
    rest = refs[6:]
    if has_prev:
        rest = rest[:7] + rest[8:]
    x = x_ref[...]
    u = _rms(x, gm_ref[...])
    z = _gelu_tanh(y_ref[...] + d_ref[...] * u).astype(BF16)
    x1 = x + _dot(z, wa_ref[...]) * _sigmoid(_dot(z, wb_ref[...]))
    x1_ref = rest[7]
    x1_ref[...] = x1
    _route_scatter(final, cap, x1, *rest[:7], *rest[8:])


def _row_spec(tm, width):
    return pl.BlockSpec((tm, width), lambda i: (i, 0))


def _const_spec(shape):
    return pl.BlockSpec(shape, lambda i: (0,) * len(shape))


def _mixer_call(body, name, lead_args, lead_specs, x2d, router, triu, cnt, xs_prev, final, cap):
    t = x2d.shape[0]
    tm = TOKEN_TILE
    n_tiles = t // tm
    gn, wrh, wrl, br = router
    cntc, cntr = cnt
    has_prev = xs_prev is not None
    in_specs = lead_specs + [
        _const_spec((1, D_MODEL)), _const_spec((D_MODEL, LANES)), _const_spec((D_MODEL, LANES)),
        _const_spec((1, LANES)), _const_spec((tm, tm)), _const_spec((LANES, LANES)), _const_spec((8, LANES)),
    ]
    args = list(lead_args) + [gn, wrh, wrl, br, triu, cntc, cntr]
    aliases = {}
    if has_prev:
        in_specs.append(pl.BlockSpec(memory_space=pl.ANY))
        args.append(xs_prev)
        aliases = {len(args) - 1: 4}
    x1, dest, cntc2, cntr2, xs, maps = pl.pallas_call(
        functools.partial(body, has_prev, final, cap),
        grid=(n_tiles,),
        in_specs=in_specs,
        out_specs=[
            _row_spec(tm, D_MODEL),
            pl.BlockSpec((1, 1, tm), lambda i: (i, 0, 0)),
            _const_spec((LANES, LANES)), _const_spec((8, LANES)),
            pl.BlockSpec(memory_space=pl.ANY),
            pl.BlockSpec(memory_space=pltpu.SMEM),
        ],
        out_shape=[
            jax.ShapeDtypeStruct((t, D_MODEL), F32),
            jax.ShapeDtypeStruct((n_tiles, 1, tm), I32),
            jax.ShapeDtypeStruct((LANES, LANES), F32),
            jax.ShapeDtypeStruct((8, LANES), F32),
            jax.ShapeDtypeStruct((N_EXPERT_GROUPS * cap, PAY_W), F32),
            jax.ShapeDtypeStruct((4, _expert_steps(cap)), I32),
        ],
        scratch_shapes=[
            pltpu.VMEM((2, tm, PAY_W), F32),
            pltpu.VMEM((MOE_TILE, PAY_W), F32),
            pltpu.VMEM((1, tm), I32),
            pltpu.SMEM((1, tm), I32),
            pltpu.VMEM((LANES, LANES), F32),
            pltpu.VMEM((8, LANES), F32),
            pltpu.SemaphoreType.DMA((2,)),
        ],
        input_output_aliases=aliases,
        compiler_params=_cparams(("arbitrary",)),
        name=name,
    )(*args)
    return x1, dest, (cntc2, cntr2), xs, maps


def _oproj_call(o2d, x2d, wo_bf, bo, router, triu, cnt, xs_prev, final, cap):
    tm = TOKEN_TILE
    lead_specs = [_row_spec(tm, Q_DIM), _row_spec(tm, D_MODEL), _const_spec((Q_DIM, D_MODEL)),
                  _const_spec((1, D_MODEL))]
    return _mixer_call(_oproj_kernel, "oproj_route", [o2d, x2d, wo_bf, bo], lead_specs, x2d,
                       router, triu, cnt, xs_prev, final, cap)


def _glu_call(y2d, x2d, gm, d, wa_bf, wb_bf, router, triu, cnt, xs_prev, final, cap):
    tm = TOKEN_TILE
    lead_specs = [_row_spec(tm, D_MODEL), _row_spec(tm, D_MODEL), _const_spec((1, D_MODEL)),
                  _const_spec((1, D_MODEL)), _const_spec((D_MODEL, D_MODEL)),
                  _const_spec((D_MODEL, D_MODEL))]
    return _mixer_call(_glu_kernel, "glu_route", [y2d, x2d, gm, d, wa_bf, wb_bf], lead_specs, x2d,
                       router, triu, cnt, xs_prev, final, cap)


def _expert_kernel(maps_ref, xs_ref, wg_ref, wu_ref, wd_ref, ys_ref, wgu, wdn, hid):
    j = pl.program_id(0)
    f = EXPERT_FF
    changed = (j == 0) | (maps_ref[2, j] != maps_ref[2, jnp.maximum(j - 1, 0)])
    valid = maps_ref[3, j]

    @pl.when(changed)
    def _():
        for e in range(EXPERTS_PER_GROUP):
            wgu[e, :, :f] = wg_ref[e].astype(BF16)
            wgu[e, :, f:] = wu_ref[e].astype(BF16)
            wdn[e * f:(e + 1) * f, :] = wd_ref[e].astype(BF16)

    @pl.when(valid == 1)
    def _():
        x = xs_ref[:, :PAY_X].astype(BF16)
        wts = xs_ref[:, PAY_X:]
        for e in range(EXPERTS_PER_GROUP):
            gu = _dot(x, wgu[e])
            g, u = gu[:, :f], gu[:, f:]
            hid[:, e * f:(e + 1) * f] = ((g * _sigmoid(g)) * u * wts[:, e:e + 1]).astype(BF16)
        ys_ref[...] = _dot(hid[...], wdn[...])

    @pl.when(valid == 0)
    def _():
        ys_ref[...] = jnp.zeros_like(ys_ref)


def _expert_call(maps, xs, w_gate, w_up, w_down, layer, cap):
    n_steps = maps.shape[1]
    e, f = EXPERTS_PER_GROUP, EXPERT_FF
    w_gate = w_gate.reshape(-1, D_MODEL, f)
    w_up = w_up.reshape(-1, D_MODEL, f)
    w_down = w_down.reshape(-1, f, D_MODEL)
    w_blk = lambda j, m: (m[2, j] + layer * N_EXPERT_GROUPS, 0, 0)
    grid_spec = pltpu.PrefetchScalarGridSpec(
        num_scalar_prefetch=1,
        grid=(n_steps,),
        in_specs=[
            pl.BlockSpec((MOE_TILE, PAY_W), lambda j, m: (m[0, j], 0)),
            pl.BlockSpec((e, D_MODEL, f), w_blk),
            pl.BlockSpec((e, D_MODEL, f), w_blk),
            pl.BlockSpec((e, f, D_MODEL), w_blk),
        ],
        out_specs=pl.BlockSpec((MOE_TILE, D_MODEL), lambda j, m: (m[1, j], 0)),
        scratch_shapes=[
            pltpu.VMEM((e, D_MODEL, 2 * f), BF16),
            pltpu.VMEM((e * f, D_MODEL), BF16),
            pltpu.VMEM((MOE_TILE, e * f), BF16),
        ],
    )
    return pl.pallas_call(
        _expert_kernel,
        grid_spec=grid_spec,
        out_shape=jax.ShapeDtypeStruct((N_EXPERT_GROUPS * cap + MOE_TILE, D_MODEL), F32),
        compiler_params=_cparams(("arbitrary",)),
        name="moe_experts",
    )(maps, xs, w_gate, w_up, w_down)


def _combine_kernel(emit_x, dest_ref, x_ref, gnext_ref, ys_hbm, *rest):
    outs, (ybuf, sem) = rest[:-2], rest[-2:]
    i = pl.program_id(0)
    n_tiles = pl.num_programs(0) - 1
    slot = i % 2
    tm = x_ref.shape[0]

    for s in range(2):
        @pl.when((i < n_tiles) & (slot == s))
        def _():
            for r in range(tm):
                pltpu.make_async_copy(ys_hbm.at[pl.ds(dest_ref[0, 0, r], 1), :],
                                      ybuf.at[s, pl.ds(r, 1), :], sem.at[s]).start()

    @pl.when(i >= 1)
    def _():
        prev = 1 - slot
        pltpu.make_async_copy(ys_hbm.at[pl.ds(0, tm), :], ybuf.at[prev], sem.at[prev]).wait()
        x2 = x_ref[...] + ybuf[prev]
        normed = _rms(x2, gnext_ref[...])
        if emit_x:
            outs[0][...] = x2
            outs[1][...] = normed
        else:
            outs[0][...] = normed


def _combine_call(dest, x2d, ys, gnext, emit_x):
    t = x2d.shape[0]
    tm = TOKEN_TILE
    n_tiles = t // tm
    n_out = 2 if emit_x else 1
    done = lambda i: (jnp.maximum(i - 1, 0), 0)
    return pl.pallas_call(
        functools.partial(_combine_kernel, emit_x),
        grid=(n_tiles + 1,),
        in_specs=[
            pl.BlockSpec((1, 1, tm), lambda i: (jnp.minimum(i, n_tiles - 1), 0, 0),
                         memory_space=pltpu.SMEM),
            pl.BlockSpec((tm, D_MODEL), done), _const_spec((1, D_MODEL)),
            pl.BlockSpec(memory_space=pl.ANY),
        ],
        out_specs=[pl.BlockSpec((tm, D_MODEL), done)] * n_out,
        out_shape=[jax.ShapeDtypeStruct((t, D_MODEL), F32)] * n_out,
        scratch_shapes=[pltpu.VMEM((2, tm, D_MODEL), F32), pltpu.SemaphoreType.DMA((2,))],
        compiler_params=_cparams(("arbitrary",)),
        name="moe_combine",
    )(dest, x2d, gnext, ys)


def _s5_state_in(u_ref_val, wre_ref, wim_ref, sre_ref, sim_ref, pair_w):
    for m in range(S5_GB // 2):
        up = u_ref_val(m * pair_w, pair_w)
        sre_ref[:, m * LANES:(m + 1) * LANES] = _dot(up, wre_ref[m])
        sim_ref[:, m * LANES:(m + 1) * LANES] = _dot(up, wim_ref[m])


def _s5_outputs(u_ref_val, hre, him, m_ref, gre_ref, gim_ref, y_store, pair_w):
    gw = pair_w // 2
    for m in range(S5_GB // 2):
        hr = hre(m).astype(BF16)
        hi = him(m).astype(BF16)
        y = _dot(hr, gre_ref[m]) + _dot(hi, gim_ref[m])
        y0 = y[:, :gw] + _dot(u_ref_val(m * pair_w, gw), m_ref[2 * m])
        y1 = y[:, gw:] + _dot(u_ref_val(m * pair_w + gw, gw), m_ref[2 * m + 1])
        y_store(m * pair_w, gw, y0)
        y_store(m * pair_w + gw, gw, y1)


def _s5_flatten(load_rows, q, n, ut, uflat):
    gc = SSM_GROUP_CH
    qc = q * gc
    for s in range(q):
        ut[:, s * gc:(s + 1) * gc, :] = load_rows(s).astype(BF16).T.reshape(S5_GB, gc, n)
    for g in range(S5_GB):
        uflat[:, g * qc:(g + 1) * qc] = ut[g].T


def _s5_unflatten(yflat, yt, store_rows, q, n):
    gc = SSM_GROUP_CH
    qc = q * gc
    for g in range(S5_GB):
        yt[g] = yflat[:, g * qc:(g + 1) * qc].T
    for t in range(q):
        store_rows(t, yt[:, t * gc:(t + 1) * gc, :].reshape(S5_GB * gc, n).T)


def _s5_prompt_kernel(u_ref, wre_ref, wim_ref, m_ref, gre_ref, gim_ref, aqr_ref, aqi_ref,
                      y_ref, her_ref, hei_ref, ut, uflat, sre, sim, hre, him, yflat, yt):
    pair_w = 2 * S5_CHUNK * SSM_GROUP_CH
    n_chunks = u_ref.shape[1] // S5_CHUNK
    _s5_flatten(lambda s: u_ref[0, pl.ds(s, n_chunks, stride=S5_CHUNK), :],
                S5_CHUNK, n_chunks, ut, uflat)
    u_val = lambda off, w: uflat[:, off:off + w]
    _s5_state_in(u_val, wre_ref, wim_ref, sre, sim, pair_w)
    ar = aqr_ref[...]
    ai = aqi_ref[...]

    def step(n, carry):
        hr, hi = carry
        hre[pl.ds(n, 1), :] = hr
        him[pl.ds(n, 1), :] = hi
        sr = sre[pl.ds(n, 1), :]
        si = sim[pl.ds(n, 1), :]
        return ar * hr - ai * hi + sr, ar * hi + ai * hr + si

    zero = jnp.zeros(ar.shape, F32)
    hr, hi = lax.fori_loop(0, n_chunks, step, (zero, zero))
    her_ref[0] = hr
    hei_ref[0] = hi

    def y_store(off, w, val):
        yflat[:, off:off + w] = val

    _s5_outputs(u_val, lambda m: hre[:, m * LANES:(m + 1) * LANES],
                lambda m: him[:, m * LANES:(m + 1) * LANES],
                m_ref, gre_ref, gim_ref, y_store, pair_w)

    def store_rows(t, val):
        y_ref[0, pl.ds(t, n_chunks, stride=S5_CHUNK), :] = val

    _s5_unflatten(yflat, yt, store_rows, S5_CHUNK, n_chunks)


def _s5_sample_kernel(n_new, u_ref, h0r_ref, h0i_ref, wre_ref, wim_ref, m_ref, gre_ref, gim_ref,
                      aqr_ref, aqi_ref, y_ref, hnr_ref, hni_ref, ut, uflat, sre, sim, yflat, yt):
    pair_w = 2 * n_new * SSM_GROUP_CH
    seqs = h0r_ref.shape[0]
    _s5_flatten(lambda s: u_ref[pl.ds(s, seqs, stride=n_new), :], n_new, seqs, ut, uflat)
    u_val = lambda off, w: uflat[:, off:off + w]
    _s5_state_in(u_val, wre_ref, wim_ref, sre, sim, pair_w)
    ar = aqr_ref[...]
    ai = aqi_ref[...]
    h0r = h0r_ref[...]
    h0i = h0i_ref[...]
    hnr_ref[...] = ar * h0r - ai * h0i + sre[...]
    hni_ref[...] = ar * h0i + ai * h0r + sim[...]

    def y_store(off, w, val):
        yflat[:, off:off + w] = val

    _s5_outputs(u_val, lambda m: h0r_ref[:, m * LANES:(m + 1) * LANES],
                lambda m: h0i_ref[:, m * LANES:(m + 1) * LANES],
                m_ref, gre_ref, gim_ref, y_store, pair_w)

    def store_rows(t, val):
        y_ref[pl.ds(t, seqs, stride=n_new), :] = val

    _s5_unflatten(yflat, yt, store_rows, n_new, seqs)


def _s5_weight_specs(q, idx):
    qc = q * SSM_GROUP_CH
    np_ = S5_GB // 2
    st = S5_GB * SSM_STATE
    return [
        pl.BlockSpec((np_, 2 * qc, LANES), lambda *a: (idx(*a), 0, 0)),
        pl.BlockSpec((np_, 2 * qc, LANES), lambda *a: (idx(*a), 0, 0)),
        pl.BlockSpec((S5_GB, qc, qc), lambda *a: (idx(*a), 0, 0)),
        pl.BlockSpec((np_, LANES, 2 * qc), lambda *a: (idx(*a), 0, 0)),
        pl.BlockSpec((np_, LANES, 2 * qc), lambda *a: (idx(*a), 0, 0)),
        pl.BlockSpec((1, st), lambda *a: (0, idx(*a))),
        pl.BlockSpec((1, st), lambda *a: (0, idx(*a))),
    ]


def _s5_prompt_call(u, w):
    b, seq, _ = u.shape
    n_chunks = seq // S5_CHUNK
    gbl = S5_GB * SSM_GROUP_CH
    qc = S5_CHUNK * SSM_GROUP_CH
    st = S5_GB * SSM_STATE
    n_gb = SSM_GROUPS // S5_GB
    gb_of = lambda g, i: g
    tok = pl.BlockSpec((1, seq, gbl), lambda g, i: (i, 0, g))
    return pl.pallas_call(
        _s5_prompt_kernel,
        grid=(n_gb, b),
        in_specs=[tok] + _s5_weight_specs(S5_CHUNK, gb_of),
        out_specs=[
            tok,
            pl.BlockSpec((1, 1, st), lambda g, i: (i, 0, g)),
            pl.BlockSpec((1, 1, st), lambda g, i: (i, 0, g)),
        ],
        out_shape=[
            jax.ShapeDtypeStruct((b, seq, D_MODEL), F32),
            jax.ShapeDtypeStruct((b, 1, SSM_GROUPS * SSM_STATE), F32),
            jax.ShapeDtypeStruct((b, 1, SSM_GROUPS * SSM_STATE), F32),
        ],
        scratch_shapes=[
            pltpu.VMEM((S5_GB, qc, n_chunks), BF16),
            pltpu.VMEM((n_chunks, S5_GB * qc), BF16),
            pltpu.VMEM((n_chunks, st), F32), pltpu.VMEM((n_chunks, st), F32),
            pltpu.VMEM((n_chunks, st), F32), pltpu.VMEM((n_chunks, st), F32),
            pltpu.VMEM((n_chunks, S5_GB * qc), F32),
            pltpu.VMEM((S5_GB, qc, n_chunks), F32),
        ],
        compiler_params=_cparams(("parallel", "parallel")),
        name="s5_prompt",
    )(u, *w)


def _s5_sample_call(u2d, h0r, h0i, w, n_new):
    t = u2d.shape[0]
    b = t // n_new
    gbl = S5_GB * SSM_GROUP_CH
    qc = n_new * SSM_GROUP_CH
    st = S5_GB * SSM_STATE
    n_gb = SSM_GROUPS // S5_GB
    gb_of = lambda g: g
    state = pl.BlockSpec((b, st), lambda g: (0, g))
    tok = pl.BlockSpec((t, gbl), lambda g: (0, g))
    return pl.pallas_call(
        functools.partial(_s5_sample_kernel, n_new),
        grid=(n_gb,),
        in_specs=[tok, state, state] + _s5_weight_specs(n_new, gb_of),
        out_specs=[tok, state, state],
        out_shape=[
            jax.ShapeDtypeStruct((t, D_MODEL), F32),
            jax.ShapeDtypeStruct((b, SSM_GROUPS * SSM_STATE), F32),
            jax.ShapeDtypeStruct((b, SSM_GROUPS * SSM_STATE), F32),
        ],
        scratch_shapes=[
            pltpu.VMEM((S5_GB, qc, b), BF16),
            pltpu.VMEM((b, S5_GB * qc), BF16),
            pltpu.VMEM((b, st), F32), pltpu.VMEM((b, st), F32),
            pltpu.VMEM((b, S5_GB * qc), F32),
            pltpu.VMEM((S5_GB, qc, b), F32),
        ],
        compiler_params=_cparams(("parallel",)),
        name="s5_sample",
    )(u2d, h0r, h0i, *w)


def _s5_discretize(a_re, a_im, log_dt, b_re, b_im):
    delta = jnp.exp(log_dt.astype(F32))[:, None]
    lr, li = a_re.astype(F32), a_im.astype(F32)
    mag = jnp.exp(delta * lr)
    abar_r = mag * jnp.cos(delta * li)
    abar_i = mag * jnp.sin(delta * li)
    nr, ni = abar_r - 1.0, abar_i
    den = lr * lr + li * li
    coef_r = ((nr * lr + ni * li) / den)[..., None]
    coef_i = ((ni * lr - nr * li) / den)[..., None]
    br, bi = b_re.astype(F32), b_im.astype(F32)
    return delta * lr, delta * li, coef_r * br - coef_i * bi, coef_r * bi + coef_i * br


def _split3(x):
    x1 = x.astype(BF16)
    r1 = x - x1.astype(F32)
    x2 = r1.astype(BF16)
    x3 = (r1 - x2.astype(F32)).astype(BF16)
    return x1, x2, x3


def _s5_prep_kernel(qs, bbr_ref, bbi_ref, ctr_ref, cti_ref, ppr_ref, ppi_ref, pnr_ref, pni_ref,
                    *out_refs):
    c = SSM_GROUP_CH
    qc = max(qs) * c
    p2 = 2 * SSM_STATE
    col = lax.broadcasted_iota(I32, (c, qc), 1)
    row = lax.broadcasted_iota(I32, (c, qc), 0)
    by_ch = (col % c == row).astype(BF16)
    by_pos = (col // c == row).astype(BF16)

    def spread(x, sel):
        return sum(_dot(part, sel) for part in _split3(x))

    def dot_f32(a, b):
        a1, a2, _ = _split3(a)
        b1, b2, _ = _split3(b)
        return _dot(a1, b1) + _dot(a1, b2) + _dot(a2, b1)

    bc_r, bc_i = spread(bbr_ref[0], by_ch), spread(bbi_ref[0], by_ch)
    cc_r, cc_i = spread(ctr_ref[0], by_ch), spread(cti_ref[0], by_ch)
    pp_r, pp_i = spread(ppr_ref[0], by_pos), spread(ppi_ref[0], by_pos)
    pn_r, pn_i = spread(pnr_ref[0], by_pos), spread(pni_ref[0], by_pos)
    r_r = cc_r * pp_r - cc_i * pp_i
    r_i = cc_r * pp_i + cc_i * pp_r
    l_r = bc_r * pn_r - bc_i * pn_i
    l_i = bc_r * pn_i + bc_i * pn_r
    lt_r, lt_i = l_r.T, l_i.T
    grp_col = lax.broadcasted_iota(I32, (qc, p2), 1) // SSM_STATE
    causal =(lax.broadcasted_iota(I32, (qc, qc), 1) // c) >= (lax.broadcasted_iota(I32, (qc, qc), 0) // c)
    kers = [jnp.where(causal, dot_f32(jnp.where(grp_col == gl, lt_r, 0.0), r_r)
                      - dot_f32(jnp.where(grp_col == gl, lt_i, 0.0), r_i), 0.0) for gl in range(2)]
    for n, q in enumerate(qs):
        wre_ref, wim_ref, m_ref, gre_ref, gim_ref = out_refs[5 * n:5 * n + 5]
        w = q * c
        aq_r = ppr_ref[0][:, q - 1:q]
        aq_i = ppi_ref[0][:, q - 1:q]
        w_r = (l_r[:, :w] * aq_r - l_i[:, :w] * aq_i).T
        w_i = (l_r[:, :w] * aq_i + l_i[:, :w] * aq_r).T
        g_re, g_im = [], []
        col_grp = lax.broadcasted_iota(I32, (w, p2), 1) // SSM_STATE
        row_grp = lax.broadcasted_iota(I32, (p2, w), 0) // SSM_STATE
        for gl in range(2):
            wre_ref[0, gl * w:(gl + 1) * w, :] = jnp.where(col_grp == gl, w_r, 0.0).astype(BF16)
            wim_ref[0, gl * w:(gl + 1) * w, :] = jnp.where(col_grp == gl, w_i, 0.0).astype(BF16)
            m_ref[gl] = kers[gl][:w, :w].astype(BF16)
            g_re.append(jnp.where(row_grp == gl, r_r[:, :w], 0.0))
            g_im.append(jnp.where(row_grp == gl, -r_i[:, :w], 0.0))
        gre_ref[0] = jnp.concatenate(g_re, axis=1).astype(BF16)
        gim_ref[0] = jnp.concatenate(g_im, axis=1).astype(BF16)


def _s5_chunk_weights(tables, qs):
    n_pairs = SSM_GROUPS // 2
    p2 = 2 * SSM_STATE
    tab = pl.BlockSpec((1, p2, SSM_GROUP_CH), lambda m: (m, 0, 0))
    out_specs, out_shape = [], []
    for q in qs:
        qc = q * SSM_GROUP_CH
        out_specs += [pl.BlockSpec((1, 2 * qc, p2), lambda m: (m, 0, 0)),
                      pl.BlockSpec((1, 2 * qc, p2), lambda m: (m, 0, 0)),
                      pl.BlockSpec((2, qc, qc), lambda m: (m, 0, 0)),
                      pl.BlockSpec((1, p2, 2 * qc), lambda m: (m, 0, 0)),
                      pl.BlockSpec((1, p2, 2 * qc), lambda m: (m, 0, 0))]
        out_shape += [jax.ShapeDtypeStruct((n_pairs, 2 * qc, p2), BF16),
                      jax.ShapeDtypeStruct((n_pairs, 2 * qc, p2), BF16),
                      jax.ShapeDtypeStruct((SSM_GROUPS, qc, qc), BF16),
                      jax.ShapeDtypeStruct((n_pairs, p2, 2 * qc), BF16),
                      jax.ShapeDtypeStruct((n_pairs, p2, 2 * qc), BF16)]
    outs = pl.pallas_call(
        functools.partial(_s5_prep_kernel, tuple(qs)),
        grid=(n_pairs,),
        in_specs=[tab] * 8,
        out_specs=out_specs, out_shape=out_shape,
        compiler_params=_cparams(("parallel",)),
        name="s5_operators",
    )(*tables)
    return [tuple(outs[5 * n:5 * n + 5]) for n in range(len(qs))]


def _s5_tables(log_mag, phase, bbar_r, bbar_i, c_re, c_im, q_max):
    k = jnp.arange(1, q_max + 1, dtype=F32)[None, None, :]
    mag = jnp.exp(log_mag[:, :, None] * k)
    cos, sin = jnp.cos(phase[:, :, None] * k), jnp.sin(phase[:, :, None] * k)
    tabs = (bbar_r, bbar_i, c_re.astype(F32).transpose(0, 2, 1), c_im.astype(F32).transpose(0, 2, 1),
            mag * cos, mag * sin, cos / mag, -sin / mag)
    return tuple(t.reshape(SSM_GROUPS // 2, 2 * SSM_STATE, SSM_GROUP_CH) for t in tabs)


def _rope_tables(pos):
    half = HEAD_DIM // 2
    inv = 1.0 / (ROPE_THETA ** (jnp.arange(half, dtype=F32) * (2.0 / HEAD_DIM)))
    ang = pos.astype(F32)[:, None] * inv[None, :]
    cos, sin = jnp.cos(ang), jnp.sin(ang)
    return jnp.tile(cos, (1, 4)), jnp.concatenate([-sin, sin, -sin, sin], axis=1)


def _router_weights(gain, w_rg, b_rg, w_re, b_re):
    n_layers = gain.shape[0]
    pad = LANES - N_EXPERTS - N_EXPERT_GROUPS
    w = jnp.concatenate([w_re, w_rg, jnp.zeros((n_layers, D_MODEL, pad), F32)], axis=2)
    b = jnp.concatenate([b_re, b_rg, jnp.zeros((n_layers, pad), F32)], axis=1)
    wh = w.astype(BF16)
    wl = (w - wh.astype(F32)).astype(BF16)
    return [(gain[l].reshape(1, -1).astype(F32), wh[l], wl[l], b[l].reshape(1, LANES))
            for l in range(n_layers)]


def kernel(x_prompt, x_sample, cache_k, cache_v, state_ssm_re, state_ssm_im, norm_mix, norm_ffn, norm_final, attn_w_qkv, attn_b_qkv, attn_w_o, attn_b_o, attn_sinks, ssm_a_re, ssm_a_im, ssm_log_dt, ssm_b_re, ssm_b_im, ssm_c_re, ssm_c_im, ssm_d, ssm_w_glu_a, ssm_w_glu_b, moe_w_router_group, moe_b_router_group, moe_w_router_expert, moe_b_router_expert, moe_w_gate, moe_w_up, moe_w_down):
    bsz, seq, _ = x_prompt.shape
    dbs, n_new, _ = x_sample.shape
    rows = cache_k.shape[2]
    tp, ts = bsz * seq, dbs * n_new
    cap = _moe_cap(tp + ts)
    xp = x_prompt.reshape(tp, D_MODEL)
    xs = x_sample.reshape(ts, D_MODEL)

    row1 = lambda v: v.reshape(1, -1).astype(F32)
    routers = _router_weights(norm_ffn, moe_w_router_group, moe_b_router_group,
                              moe_w_router_expert, moe_b_router_expert)
    triu = jnp.triu(jnp.ones((TOKEN_TILE, TOKEN_TILE), F32), 1).astype(BF16)
    cnt0 = (jnp.zeros((LANES, LANES), F32), jnp.zeros((8, LANES), F32))

    slots = jnp.asarray(HEAD_SLOTS, dtype=I32)

    def q_slots(w):
        lead = w.shape[:-1]
        qh = w[..., :Q_DIM].reshape(*lead, N_HEADS, HEAD_DIM)[..., slots, :].reshape(*lead, Q_DIM)
        return jnp.concatenate([qh, w[..., Q_DIM:]], axis=-1)

    wqkv = q_slots(attn_w_qkv[0]).astype(BF16)
    bqkv = row1(q_slots(attn_b_qkv[0]))
    wo = attn_w_o[0].reshape(N_HEADS, HEAD_DIM, D_MODEL)[slots].reshape(Q_DIM, D_MODEL).astype(BF16)
    bo = row1(attn_b_o[0])
    sinks = attn_sinks[0].astype(F32)[slots]
    g_mix0, g_mix1 = row1(norm_mix[0]), row1(norm_mix[1])
    pos = jnp.concatenate([jnp.arange(seq, dtype=I32),
                           jnp.tile(PAST_LEN + jnp.arange(n_new, dtype=I32), dbs)])
    cos_t, sin_t = _rope_tables(pos)

    qp, kp, vp = _qkv_call(xp, g_mix0, wqkv, bqkv, cos_t, sin_t, (0, seq), BF16)
    qs, ks, vs = _qkv_call(xs, g_mix0, wqkv, bqkv, cos_t, sin_t, (seq, ts), F32)
    op = _attn_prompt_call(jnp.repeat(sinks, BLOCK).reshape(1, -1), qp.reshape(bsz, seq, Q_DIM),
                           kp.reshape(bsz, seq, KV_DIM),
                           vp.reshape(bsz, seq, KV_DIM))
    os_, nks, nvs = _attn_sample_call(jnp.repeat(sinks, n_new).reshape(1, -1), qs, ks, vs,
                                      cache_k[0].reshape(dbs, rows, KV_DIM),
                                      cache_v[0].reshape(dbs, rows, KV_DIM), n_new, 8)
    xp1, dest_p, cnt, rows_x, _ = _oproj_call(op.reshape(tp, Q_DIM), xp, wo, bo, routers[0], triu,
                                              cnt0, None, False, cap)
    xs1, dest_s, cnt, rows_x, steps = _oproj_call(os_, xs, wo, bo, routers[0], triu, cnt, rows_x,
                                                  True, cap)
    rows_y = _expert_call(steps, rows_x, moe_w_gate, moe_w_up, moe_w_down, 0, cap)
    xp2, up = _combine_call(dest_p, xp1, rows_y, g_mix1, True)
    xs2, us = _combine_call(dest_s, xs1, rows_y, g_mix1, True)

    disc = _s5_discretize(ssm_a_re[0], ssm_a_im[0], ssm_log_dt[0], ssm_b_re[0], ssm_b_im[0])
    tables = _s5_tables(*disc, ssm_c_re[0], ssm_c_im[0], S5_CHUNK)
    a_pow = lambda q: [tables[i][:, :, q - 1].reshape(1, -1) for i in (4, 5)]
    ops_p, ops_s = _s5_chunk_weights(tables, (S5_CHUNK, n_new))
    w_p = (*ops_p, *a_pow(S5_CHUNK))
    w_s = (*ops_s, *a_pow(n_new))
    y_p, hpr, hpi = _s5_prompt_call(up.reshape(bsz, seq, D_MODEL), w_p)
    y_p = y_p.reshape(tp, D_MODEL)
    h0r = state_ssm_re[0].reshape(dbs, -1).astype(F32)
    h0i = state_ssm_im[0].reshape(dbs, -1).astype(F32)
    y_s, hsr, hsi = _s5_sample_call(us, h0r, h0i, w_s, n_new)

    wa, wb = ssm_w_glu_a[0].astype(BF16), ssm_w_glu_b[0].astype(BF16)
    d_row = row1(ssm_d[0])
    xp3, dest_p, cnt, rows_x, _ = _glu_call(y_p, xp2, g_mix1, d_row, wa, wb, routers[1], triu,
                                            cnt0, None, False, cap)
    xs3, dest_s, cnt, rows_x, steps = _glu_call(y_s, xs2, g_mix1, d_row, wa, wb, routers[1], triu,
                                                cnt, rows_x, True, cap)
    rows_y = _expert_call(steps, rows_x, moe_w_gate, moe_w_up, moe_w_down, 1, cap)
    g_fin = row1(norm_final)
    (yp,) = _combine_call(dest_p, xp3, rows_y, g_fin, False)
    (ys,) = _combine_call(dest_s, xs3, rows_y, g_fin, False)

    kv5 = lambda a, n: a.reshape(1, n, rows, KV_HEADS, HEAD_DIM)
    st4 = lambda a, n: a.reshape(1, n, SSM_GROUPS, SSM_STATE)
    k_last = kp.reshape(bsz, seq, KV_DIM)[:, seq - WINDOW:]
    v_last = vp.reshape(bsz, seq, KV_DIM)[:, seq - WINDOW:]
    return (yp.reshape(bsz, seq, D_MODEL), ys.reshape(dbs, n_new, D_MODEL),
            k_last.reshape(1, bsz, WINDOW, KV_HEADS, HEAD_DIM), kv5(nks, dbs),
            v_last.reshape(1, bsz, WINDOW, KV_HEADS, HEAD_DIM), kv5(nvs, dbs),
            st4(hpr, bsz), st4(hsr, dbs), st4(hpi, bsz), st4(hsi, dbs))
```

```python
import collections
import functools
import math

import jax
import jax.numpy as jnp
from jax import lax
from jax.experimental import pallas as pl
from jax.experimental.pallas import tpu as pltpu

F32 = jnp.float32
BF16 = jnp.bfloat16
I32 = jnp.int32

D_MODEL = 1024
N_HEADS = 16
KV_HEADS = 4
HEAD_DIM = 64
Q_DIM = N_HEADS * HEAD_DIM
KV_DIM = KV_HEADS * HEAD_DIM
QKV_DIM = Q_DIM + 2 * KV_DIM
WINDOW = 128
BLOCK = 128
ROPE_THETA = 10000.0
PAST_LEN = 16384
SSM_GROUP_CH = 16
SSM_GROUPS = D_MODEL // SSM_GROUP_CH
SSM_STATE = 64
N_EXPERT_GROUPS = 4
EXPERTS_PER_GROUP = 8
N_EXPERTS = N_EXPERT_GROUPS * EXPERTS_PER_GROUP
EXPERT_FF = 128
NORM_EPS = 1e-5

LANES = 128
VMEM_LIMIT = 56 * 1024 * 1024
S5_CHUNK = 16
S5_GB = 8
TOKEN_TILE = 1024
MOE_TILE = 512
ROUTE_PARTS = 2
COMBINE_TILE = 512
PAY_X = D_MODEL
PAY_W = PAY_X + LANES


def _cparams(sem):
    return pltpu.CompilerParams(dimension_semantics=sem, vmem_limit_bytes=VMEM_LIMIT)


def _rms(x, g):
    return x * lax.rsqrt(jnp.mean(x * x, axis=-1, keepdims=True) + NORM_EPS) * g


def _dot(a, b):
    return jnp.dot(a, b, preferred_element_type=F32)


def _dot_nt(a, b):
    return lax.dot_general(a, b, (((1,), (1,)), ((), ())), preferred_element_type=F32)


def _sigmoid(x):
    return 1.0 / (1.0 + jnp.exp(-x))


def _qkv_kernel(x_ref, g_ref, w_ref, b_ref, cos_ref, sin_ref, q_ref, k_ref, v_ref):
    xn = _rms(x_ref[...], g_ref[...])
    qkv = _dot(xn.astype(BF16), w_ref[...]) + b_ref[...]
    cos = cos_ref[...]
    sin = sin_ref[...]
    lane = lax.broadcasted_iota(I32, cos.shape, 1)
    first_half = (lane % HEAD_DIM) < (HEAD_DIM // 2)
    n_rot = (Q_DIM + KV_DIM) // LANES
    for c in range(n_rot):
        blk = qkv[:, c * LANES:(c + 1) * LANES]
        partner = jnp.where(first_half,
                            pltpu.roll(blk, LANES - HEAD_DIM // 2, 1),
                            pltpu.roll(blk, HEAD_DIM // 2, 1))
        rot = blk * cos + partner * sin
        if c < Q_DIM // LANES:
            q_ref[:, c * LANES:(c + 1) * LANES] = (rot * (HEAD_DIM ** -0.5)).astype(q_ref.dtype)
        else:
            k_ref[:, c * LANES - Q_DIM:(c + 1) * LANES - Q_DIM] = rot
    v_ref[...] = qkv[:, Q_DIM + KV_DIM:]


def _qkv_call(x2d, gain, w_bf, bias, cos_t, sin_t, pos_rows, q_dtype):
    t = x2d.shape[0]
    tm = TOKEN_TILE
    first, n_pos = pos_rows[0] // tm, pos_rows[1] // tm
    return pl.pallas_call(
        _qkv_kernel,
        grid=(t // tm,),
        in_specs=[
            pl.BlockSpec((tm, D_MODEL), lambda i: (i, 0)),
            pl.BlockSpec((1, D_MODEL), lambda i: (0, 0)),
            pl.BlockSpec((D_MODEL, QKV_DIM), lambda i: (0, 0)),
            pl.BlockSpec((1, QKV_DIM), lambda i: (0, 0)),
            pl.BlockSpec((tm, LANES), lambda i: (first + i % n_pos, 0)),
            pl.BlockSpec((tm, LANES), lambda i: (first + i % n_pos, 0)),
        ],
        out_specs=[
            pl.BlockSpec((tm, Q_DIM), lambda i: (i, 0)),
            pl.BlockSpec((tm, KV_DIM), lambda i: (i, 0)),
            pl.BlockSpec((tm, KV_DIM), lambda i: (i, 0)),
        ],
        out_shape=[
            jax.ShapeDtypeStruct((t, Q_DIM), q_dtype),
            jax.ShapeDtypeStruct((t, KV_DIM), F32),
            jax.ShapeDtypeStruct((t, KV_DIM), F32),
        ],
        compiler_params=_cparams(("parallel",)),
        name="qkv_rope",
    )(x2d, gain, w_bf, bias, cos_t, sin_t)


HEAD_SLOTS = tuple(8 * c + 4 * half + i for c in range(2) for i in range(4) for half in range(2))
LOG2E = math.log2(math.e)


def _attn_prompt_kernel(sink_ref, q_ref, kp_ref, kc_ref, vp_ref, vc_ref, o_ref):
    n = pl.program_id(1)
    k2 = jnp.concatenate([kp_ref[0], kc_ref[0]], axis=0).astype(BF16)
    v2 = jnp.concatenate([vp_ref[0], vc_ref[0]], axis=0).astype(BF16)
    key = lax.broadcasted_iota(I32, (2 * BLOCK, 2 * BLOCK), 0)
    qry = lax.broadcasted_iota(I32, (2 * BLOCK, 2 * BLOCK), 1) % BLOCK
    mask = (key >= qry) & (key <= qry + WINDOW) & ((n > 0) | (key >= BLOCK))
    low = lax.broadcasted_iota(I32, (BLOCK, LANES), 1) < HEAD_DIM
    keep_low = low.astype(BF16)
    keep_high = 1 - keep_low
    n_pairs = N_HEADS // 2
    chunk = lambda pr: slice((pr // (n_pairs // 2)) * LANES, (pr // (n_pairs // 2) + 1) * LANES)

    def scores(pr):
        qblk = q_ref[0, :, pr * LANES:(pr + 1) * LANES]
        qz = jnp.concatenate([qblk * keep_low, qblk * keep_high], axis=0)
        s = _dot_nt(k2[:, chunk(pr)], qz) * LOG2E
        return jnp.where(mask, s, -jnp.inf)

    def weights(pr, s):
        sink = sink_ref[:, 2 * pr * BLOCK:(2 * pr + 2) * BLOCK] * LOG2E
        m = jnp.maximum(jnp.max(s, axis=0, keepdims=True), sink)
        p = jnp.exp2(s - m)
        denom = jnp.sum(p, axis=0, keepdims=True) + jnp.exp2(sink - m)
        return (p * (1.0 / denom)).T.astype(BF16)

    def finish(pr, w):
        oz = _dot(w, v2[:, chunk(pr)])
        o_ref[0, :, pr * LANES:(pr + 1) * LANES] = jnp.where(low, oz[:BLOCK], oz[BLOCK:]).astype(o_ref.dtype)

    all_s = [scores(pr) for pr in range(n_pairs)]
    all_w = [weights(pr, s) for pr, s in enumerate(all_s)]
    for pr, w in enumerate(all_w):
        finish(pr, w)


def _attn_prompt_call(sinks, q, k, v):
    b, l, _ = q.shape
    nb = l // BLOCK
    kv_cur = pl.BlockSpec((1, BLOCK, KV_DIM), lambda i, n: (i, n, 0))
    kv_prev = pl.BlockSpec((1, BLOCK, KV_DIM), lambda i, n: (i, jnp.maximum(n - 1, 0), 0))
    return pl.pallas_call(
        _attn_prompt_kernel,
        grid=(b, nb),
        in_specs=[
            pl.BlockSpec((1, N_HEADS * BLOCK), lambda i, n: (0, 0)),
            pl.BlockSpec((1, BLOCK, Q_DIM), lambda i, n: (i, n, 0)),
            kv_prev, kv_cur, kv_prev, kv_cur,
        ],
        out_specs=pl.BlockSpec((1, BLOCK, Q_DIM), lambda i, n: (i, n, 0)),
        out_shape=jax.ShapeDtypeStruct((b, l, Q_DIM), BF16),
        compiler_params=_cparams(("parallel", "parallel")),
        name="attn_prompt",
    )(sinks, q, k, k, v, v)


def _attn_sample_kernel(n_new, seqs, sink_ref, q_ref, kn_ref, vn_ref, ck_ref, cv_ref,
                        o_ref, nk_ref, nv_ref):
    rows = ck_ref.shape[1]
    keys = 2 * rows
    n_cols = N_HEADS * n_new
    key = lax.broadcasted_iota(I32, (keys, n_cols), 0)
    qry = lax.broadcasted_iota(I32, (keys, n_cols), 1) % n_new
    mask = ((key < rows) & (key >= qry)) | ((key >= rows) & (key - rows <= qry))
    low = lax.broadcasted_iota(I32, (n_new, LANES), 1) < HEAD_DIM
    sink = sink_ref[...]
    pad = jnp.zeros((rows - n_new, KV_DIM), F32)

    def body(sb, carry):
        r0 = pl.multiple_of(sb * n_new, n_new)
        k_new = kn_ref[pl.ds(r0, n_new), :]
        v_new = vn_ref[pl.ds(r0, n_new), :]
        k_all = jnp.concatenate([ck_ref[sb], k_new, pad], axis=0).astype(BF16)
        v_all = jnp.concatenate([cv_ref[sb], v_new, pad], axis=0).astype(BF16)
        blocks = []
        for slot in range(N_HEADS):
            pair = q_ref[pl.ds(r0, n_new), (slot // 2) * LANES:(slot // 2 + 1) * LANES]
            blk = jnp.where(low if slot % 2 == 0 else ~low, pair, 0.0)
            zero = jnp.zeros_like(blk)
            blocks.append(jnp.concatenate([blk, zero] if slot < N_HEADS // 2 else [zero, blk], axis=1))
        qz = jnp.concatenate(blocks, axis=0).astype(BF16)
        s = _dot_nt(k_all, qz)
        s = jnp.where(mask, s, -jnp.inf)
        m = jnp.maximum(jnp.max(s, axis=0, keepdims=True), sink)
        p = jnp.exp(s - m)
        denom = jnp.sum(p, axis=0, keepdims=True) + jnp.exp(sink - m)
        w = (p * (1.0 / denom)).T.astype(BF16)
        oz = _dot(w, v_all)
        for pr in range(N_HEADS // 2):
            lanes = slice((pr // (N_HEADS // 4)) * LANES, (pr // (N_HEADS // 4) + 1) * LANES)
            a = oz[2 * pr * n_new:(2 * pr + 1) * n_new, lanes]
            b = oz[(2 * pr + 1) * n_new:(2 * pr + 2) * n_new, lanes]
            o_ref[pl.ds(r0, n_new), pr * LANES:(pr + 1) * LANES] = jnp.where(low, a, b)
        nk_ref[sb, 0:rows - n_new, :] = ck_ref[sb, n_new:rows, :]
        nk_ref[sb, rows - n_new:rows, :] = k_new
        nv_ref[sb, 0:rows - n_new, :] = cv_ref[sb, n_new:rows, :]
        nv_ref[sb, rows - n_new:rows, :] = v_new
        return carry

    lax.fori_loop(0, seqs, body, 0)


def _attn_sample_call(sink_row, q, k_new, v_new, cache_k, cache_v, n_new, seqs):
    b, rows, _ = cache_k.shape
    tok = pl.BlockSpec((seqs * n_new, Q_DIM), lambda i: (i, 0))
    tok_kv = pl.BlockSpec((seqs * n_new, KV_DIM), lambda i: (i, 0))
    cache = pl.BlockSpec((seqs, rows, KV_DIM), lambda i: (i, 0, 0))
    return pl.pallas_call(
        functools.partial(_attn_sample_kernel, n_new, seqs),
        grid=(b // seqs,),
        in_specs=[pl.BlockSpec((1, N_HEADS * n_new), lambda i: (0, 0)), tok, tok_kv, tok_kv, cache, cache],
        out_specs=[tok, cache, cache],
        out_shape=[
            jax.ShapeDtypeStruct((b * n_new, Q_DIM), F32),
            jax.ShapeDtypeStruct((b, rows, KV_DIM), F32),
            jax.ShapeDtypeStruct((b, rows, KV_DIM), F32),
        ],
        compiler_params=_cparams(("parallel",)),
        name="attn_sample",
    )(sink_row, q, k_new, v_new, cache_k, cache_v)


def _moe_cap(t_total):
    return t_total + MOE_TILE


def _expert_steps(cap):
    return cap // MOE_TILE - 1 + N_EXPERT_GROUPS


def _write_step_table(maps_ref, counts, cap):
    per = cap // MOE_TILE
    shift = MOE_TILE.bit_length() - 1
    ends = []
    for c in counts:
        tiles = lax.shift_right_logical(c + (MOE_TILE - 1), shift)
        ends.append(tiles if not ends else ends[-1] + tiles)
    total = ends[-1]
    for j in range(maps_ref.shape[1]):
        jj = jnp.maximum(jnp.minimum(j, total - 1), 0)
        g = sum((jj >= e).astype(I32) for e in ends[:-1])
        start = sum(jnp.where(g > k, ends[k] - (ends[k - 1] if k else 0), 0) for k in range(len(ends) - 1))
        blk_in = g * per + jj - start
        valid = (total > j).astype(I32)
        maps_ref[0, j] = blk_in
        maps_ref[1, j] = jnp.where(valid == 1, blk_in, len(counts) * per)
        maps_ref[2, j] = g
        maps_ref[3, j] = valid


def _route_rows(x1, gn_ref, wrh_ref, wrl_ref, br_ref):
    tm = x1.shape[0]
    xn = _rms(x1, gn_ref[...])
    xh = xn.astype(BF16)
    xl = (xn - xh.astype(F32)).astype(BF16)
    logits = _dot(xh, wrh_ref[...]) + _dot(xl, wrh_ref[...]) + _dot(xh, wrl_ref[...]) + br_ref[...]
    lt = logits.T
    ge = N_EXPERTS // EXPERTS_PER_GROUP
    sub = lax.broadcasted_iota(I32, (EXPERTS_PER_GROUP, tm), 0).astype(F32)
    big = jnp.float32(LANES)
    neg = -jnp.inf
    gl = jnp.where(sub < N_EXPERT_GROUPS, lt[N_EXPERTS:N_EXPERTS + EXPERTS_PER_GROUP], neg)
    gmax = jnp.max(gl, axis=0, keepdims=True)
    g_val = 1.0 / jnp.sum(jnp.exp(gl - gmax), axis=0, keepdims=True)
    g_idx = jnp.min(jnp.where(gl == gmax, sub, big), axis=0, keepdims=True)
    el = lt[0:EXPERTS_PER_GROUP]
    for g in range(1, ge):
        el = jnp.where(g_idx == g, lt[g * EXPERTS_PER_GROUP:(g + 1) * EXPERTS_PER_GROUP], el)
    e1 = jnp.max(el, axis=0, keepdims=True)
    i1 = jnp.min(jnp.where(el == e1, sub, big), axis=0, keepdims=True)
    el2 = jnp.where(sub == i1, neg, el)
    e2 = jnp.max(el2, axis=0, keepdims=True)
    i2 = jnp.min(jnp.where(el2 == e2, sub, big), axis=0, keepdims=True)
    t = jnp.exp(e2 - e1)
    w1 = 1.0 / (1.0 + t)
    w2 = t / (1.0 + t)
    wts_t = g_val * (jnp.where(sub == i1, w1, 0.0) + jnp.where(sub == i2, w2, 0.0))
    wts = jnp.concatenate([wts_t, jnp.zeros((LANES - EXPERTS_PER_GROUP, tm), F32)], axis=0).T
    return xn, wts, g_idx


def _route_begin(cntc_in, cntr_in, xs_hbm, pay, cntc, cntr, sem):
    i = pl.program_id(0)
    slot = i % 2

    @pl.when(i == 0)
    def _():
        cntc[...] = cntc_in[...]
        cntr[...] = cntr_in[...]

    @pl.when(i >= 2)
    def _():
        pltpu.make_async_copy(pay.at[slot], xs_hbm.at[pl.ds(0, pay.shape[1]), :], sem.at[slot]).wait()


def _route_scatter(final, cap, parts, triu_ref, dest_ref, cntc_out, cntr_out, xs_hbm, maps_ref,
                   pay, zrows, dest_v, dest_s, cntc, cntr, sem):
    i = pl.program_id(0)
    n = pl.num_programs(0)
    slot = i % 2
    tm = pay.shape[1]

    def wait_slot(s):
        pltpu.make_async_copy(pay.at[s], xs_hbm.at[pl.ds(0, tm), :], sem.at[s]).wait()

    r0 = 0
    for xn, wts, _ in parts:
        pay[slot, r0:r0 + xn.shape[0], :PAY_X] = xn
        pay[slot, r0:r0 + xn.shape[0], PAY_X:] = wts
        r0 += xn.shape[0]
    g_idx = jnp.concatenate([p[2] for p in parts], axis=1)

    grp_t = lax.broadcasted_iota(I32, (LANES, tm), 0).astype(F32)
    oht = (grp_t == g_idx).astype(F32)
    rank = _dot(oht.astype(BF16), triu_ref[...])
    grp = lax.broadcasted_iota(I32, (LANES, 1), 0).astype(F32)
    base = grp * float(cap) + cntc[:, 0:1]
    dest = jnp.sum(oht * (rank + base), axis=0, keepdims=True).astype(I32)
    cntc[...] = cntc[...] + jnp.sum(oht, axis=1, keepdims=True)
    dest_ref[0] = dest
    dest_v[...] = dest
    pltpu.sync_copy(dest_v, dest_s)

    for s in range(2):
        @pl.when(slot == s)
        def _():
            for r in range(tm):
                pltpu.make_async_copy(pay.at[s, pl.ds(r, 1), :],
                                      xs_hbm.at[pl.ds(dest_s[0, r], 1), :], sem.at[s]).start()

    @pl.when(i == n - 1)
    def _():
        cntc_out[...] = cntc[...]
        cntr[...] = cntc[...].T[0:8, :]
        cntr_out[...] = cntr[...]

        @pl.when(n >= 2)
        def _():
            wait_slot(1 - slot)

        wait_slot(slot)
        dest_v[:, 0:LANES] = cntr[0:1, :].astype(I32)
        pltpu.sync_copy(dest_v, dest_s)
        counts = [dest_s[0, g] for g in range(N_EXPERT_GROUPS)]
        _write_step_table(maps_ref, counts, cap)
        if final:
            zrows[...] = jnp.zeros_like(zrows)
            starts = []
            for g in range(N_EXPERT_GROUPS):
                c_g = counts[g]
                starts.append(pl.multiple_of(g * cap + lax.shift_left(lax.shift_right_logical(c_g + 7, 3), 3), 8))
                for k in range(7):
                    pltpu.make_async_copy(zrows.at[pl.ds(0, 1), :],
                                          xs_hbm.at[pl.ds(g * cap + c_g + k, 1), :], sem.at[1]).start()
            for _ in range(7 * N_EXPERT_GROUPS):
                pltpu.make_async_copy(zrows.at[pl.ds(0, 1), :], xs_hbm.at[pl.ds(0, 1), :], sem.at[1]).wait()
            for g in range(N_EXPERT_GROUPS):
                pltpu.make_async_copy(zrows, xs_hbm.at[pl.ds(starts[g], MOE_TILE), :], sem.at[0]).start()
            for g in range(N_EXPERT_GROUPS):
                pltpu.make_async_copy(zrows, xs_hbm.at[pl.ds(0, MOE_TILE), :], sem.at[0]).wait()


RouteRefs = collections.namedtuple(
    "RouteRefs", "gn wrh wrl br triu cntc_in cntr_in x1 dest cntc_out cntr_out xs maps "
                 "pay zrows dest_v dest_s cntc cntr sem")


def _route_refs(rest, has_prev):
    if has_prev:
        rest = rest[:7] + rest[8:]
    return RouteRefs(*rest)


def _route_tile(final, cap, r, x1_parts):
    parts = [_route_rows(x1, r.gn, r.wrh, r.wrl, r.br) for x1 in x1_parts]
    _route_scatter(final, cap, parts, r.triu, r.dest, r.cntc_out, r.cntr_out, r.xs, r.maps,
                   r.pay, r.zrows, r.dest_v, r.dest_s, r.cntc, r.cntr, r.sem)


def _oproj_kernel(has_prev, final, cap, *refs):
    (o_ref, x_ref, wo_ref, bo_ref) = refs[:4]
    r = _route_refs(refs[4:], has_prev)
    _route_begin(r.cntc_in, r.cntr_in, r.xs, r.pay, r.cntc, r.cntr, r.sem)
    rows = x_ref.shape[0] // ROUTE_PARTS
    x1_parts = []
    for h in range(ROUTE_PARTS):
        rs = slice(h * rows, (h + 1) * rows)
        x1 = x_ref[rs, :] + _dot(o_ref[rs, :].astype(BF16), wo_ref[...]) + bo_ref[...]
        r.x1[rs, :] = x1
        x1_parts.append(x1)
    _route_tile(final, cap, r, x1_parts)


def _gelu_tanh(x):
    return x * (0.5 * (1.0 + jnp.tanh(math.sqrt(2.0 / math.pi) * (x + 0.044715 * (x * x * x)))))


def _glu_kernel(has_prev, final, cap, *refs):
    (y_ref, x_ref, gm_ref, d_ref, wa_ref, wb_ref) = refs[:6]
    r = _route_refs(refs[6:], has_prev)
    _route_begin(r.cntc_in, r.cntr_in, r.xs, r.pay, r.cntc, r.cntr, r.sem)
    rows = x_ref.shape[0] // ROUTE_PARTS
    gated = []
    for h in range(ROUTE_PARTS):
        rs = slice(h * rows, (h + 1) * rows)
        x = x_ref[rs, :]
        z = _gelu_tanh(y_ref[rs, :] + d_ref[...] * _rms(x, gm_ref[...])).astype(BF16)
        gated.append((rs, x, _dot(z, wa_ref[...]), _dot(z, wb_ref[...])))
    x1_parts = []
    for rs, x, a, b in gated:
        x1 = x + a * _sigmoid(b)
        r.x1[rs, :] = x1
        x1_parts.append(x1)
    _route_tile(final, cap, r, x1_parts)


def _row_spec(tm, width):
    return pl.BlockSpec((tm, width), lambda i: (i, 0))


def _const_spec(shape):
    return pl.BlockSpec(shape, lambda i: (0,) * len(shape))


def _mixer_call(body, name, lead_args, lead_specs, x2d, router, triu, cnt, xs_prev, final, cap):
    t = x2d.shape[0]
    tm = TOKEN_TILE
    n_tiles = t // tm
    gn, wrh, wrl, br = router
    cntc, cntr = cnt
    has_prev = xs_prev is not None
    in_specs = lead_specs + [
        _const_spec((1, D_MODEL)), _const_spec((D_MODEL, LANES)), _const_spec((D_MODEL, LANES)),
        _const_spec((1, LANES)), _const_spec((tm, tm)), _const_spec((LANES, LANES)), _const_spec((8, LANES)),
    ]
    args = list(lead_args) + [gn, wrh, wrl, br, triu, cntc, cntr]
    aliases = {}
    if has_prev:
        in_specs.append(pl.BlockSpec(memory_space=pl.ANY))
        args.append(xs_prev)
        aliases = {len(args) - 1: 4}
    x1, dest, cntc2, cntr2, xs, maps = pl.pallas_call(
        functools.partial(body, has_prev, final, cap),
        grid=(n_tiles,),
        in_specs=in_specs,
        out_specs=[
            _row_spec(tm, D_MODEL),
            pl.BlockSpec((1, 1, tm), lambda i: (i, 0, 0)),
            _const_spec((LANES, LANES)), _const_spec((8, LANES)),
            pl.BlockSpec(memory_space=pl.ANY),
            pl.BlockSpec(memory_space=pltpu.SMEM),
        ],
        out_shape=[
            jax.ShapeDtypeStruct((t, D_MODEL), F32),
            jax.ShapeDtypeStruct((n_tiles, 1, tm), I32),
            jax.ShapeDtypeStruct((LANES, LANES), F32),
            jax.ShapeDtypeStruct((8, LANES), F32),
            jax.ShapeDtypeStruct((N_EXPERT_GROUPS * cap, PAY_W), F32),
            jax.ShapeDtypeStruct((4, _expert_steps(cap)), I32),
        ],
        scratch_shapes=[
            pltpu.VMEM((2, tm, PAY_W), F32),
            pltpu.VMEM((MOE_TILE, PAY_W), F32),
            pltpu.VMEM((1, tm), I32),
            pltpu.SMEM((1, tm), I32),
            pltpu.VMEM((LANES, LANES), F32),
            pltpu.VMEM((8, LANES), F32),
            pltpu.SemaphoreType.DMA((2,)),
        ],
        input_output_aliases=aliases,
        compiler_params=_cparams(("arbitrary",)),
        name=name,
    )(*args)
    return x1, dest, (cntc2, cntr2), xs, maps


def _oproj_call(o2d, x2d, wo_bf, bo, router, triu, cnt, xs_prev, final, cap):
    tm = TOKEN_TILE
    lead_specs = [_row_spec(tm, Q_DIM), _row_spec(tm, D_MODEL), _const_spec((Q_DIM, D_MODEL)),
                  _const_spec((1, D_MODEL))]
    return _mixer_call(_oproj_kernel, "oproj_route", [o2d, x2d, wo_bf, bo], lead_specs, x2d,
                       router, triu, cnt, xs_prev, final, cap)


def _glu_call(y2d, x2d, gm, d, wa_bf, wb_bf, router, triu, cnt, xs_prev, final, cap):
    tm = TOKEN_TILE
    lead_specs = [_row_spec(tm, D_MODEL), _row_spec(tm, D_MODEL), _const_spec((1, D_MODEL)),
                  _const_spec((1, D_MODEL)), _const_spec((D_MODEL, D_MODEL)),
                  _const_spec((D_MODEL, D_MODEL))]
    return _mixer_call(_glu_kernel, "glu_route", [y2d, x2d, gm, d, wa_bf, wb_bf], lead_specs, x2d,
                       router, triu, cnt, xs_prev, final, cap)


def _expert_kernel(maps_ref, xs_ref, wg_ref, wu_ref, wd_ref, ys_ref, wgu, wdn, hid):
    j = pl.program_id(0)
    f = EXPERT_FF
    changed = (j == 0) | (maps_ref[2, j] != maps_ref[2, jnp.maximum(j - 1, 0)])
    valid = maps_ref[3, j]

    @pl.when(changed)
    def _():
        for e in range(EXPERTS_PER_GROUP):
            wgu[e, :, :f] = wg_ref[e].astype(BF16)
            wgu[e, :, f:] = wu_ref[e].astype(BF16)
            wdn[e * f:(e + 1) * f, :] = wd_ref[e].astype(BF16)

    @pl.when(valid == 1)
    def _():
        x = xs_ref[:, :PAY_X].astype(BF16)
        wts = xs_ref[:, PAY_X:]
        for e in range(EXPERTS_PER_GROUP):
            gu = _dot(x, wgu[e])
            g, u = gu[:, :f], gu[:, f:]
            hid[:, e * f:(e + 1) * f] = ((g * _sigmoid(g)) * u * wts[:, e:e + 1]).astype(BF16)
        ys_ref[...] = _dot(hid[...], wdn[...])

    @pl.when(valid == 0)
    def _():
        ys_ref[...] = jnp.zeros_like(ys_ref)


def _expert_call(maps, xs, w_gate, w_up, w_down, layer, cap):
    n_steps = maps.shape[1]
    e, f = EXPERTS_PER_GROUP, EXPERT_FF
    w_gate = w_gate.reshape(-1, D_MODEL, f)
    w_up = w_up.reshape(-1, D_MODEL, f)
    w_down = w_down.reshape(-1, f, D_MODEL)
    w_blk = lambda j, m: (m[2, j] + layer * N_EXPERT_GROUPS, 0, 0)
    grid_spec = pltpu.PrefetchScalarGridSpec(
        num_scalar_prefetch=1,
        grid=(n_steps,),
        in_specs=[
            pl.BlockSpec((MOE_TILE, PAY_W), lambda j, m: (m[0, j], 0)),
            pl.BlockSpec((e, D_MODEL, f), w_blk),
            pl.BlockSpec((e, D_MODEL, f), w_blk),
            pl.BlockSpec((e, f, D_MODEL), w_blk),
        ],
        out_specs=pl.BlockSpec((MOE_TILE, D_MODEL), lambda j, m: (m[1, j], 0)),
        scratch_shapes=[
            pltpu.VMEM((e, D_MODEL, 2 * f), BF16),
            pltpu.VMEM((e * f, D_MODEL), BF16),
            pltpu.VMEM((MOE_TILE, e * f), BF16),
        ],
    )
    return pl.pallas_call(
        _expert_kernel,
        grid_spec=grid_spec,
        out_shape=jax.ShapeDtypeStruct((N_EXPERT_GROUPS * cap + MOE_TILE, D_MODEL), F32),
        compiler_params=_cparams(("arbitrary",)),
        name="moe_experts",
    )(maps, xs, w_gate, w_up, w_down)


def _combine_kernel(emit_x, dest_ref, x_ref, gnext_ref, ys_hbm, *rest):
    outs, (ybuf, sem) = rest[:-2], rest[-2:]
    i = pl.program_id(0)
    n_tiles = pl.num_programs(0) - 1
    slot = i % 2
    tm = x_ref.shape[0]

    for s in range(2):
        @pl.when((i < n_tiles) & (slot == s))
        def _():
            for r in range(tm):
                pltpu.make_async_copy(ys_hbm.at[pl.ds(dest_ref[0, 0, r], 1), :],
                                      ybuf.at[s, pl.ds(r, 1), :], sem.at[s]).start()

    @pl.when(i >= 1)
    def _():
        prev = 1 - slot
        pltpu.make_async_copy(ys_hbm.at[pl.ds(0, tm), :], ybuf.at[prev], sem.at[prev]).wait()
        x2 = x_ref[...] + ybuf[prev]
        normed = _rms(x2, gnext_ref[...])
        if emit_x:
            outs[0][...] = x2
            outs[1][...] = normed
        else:
            outs[0][...] = normed


def _combine_call(dest, x2d, ys, gnext, emit_x):
    t = x2d.shape[0]
    tm = COMBINE_TILE
    n_tiles = t // tm
    dest = dest.reshape(n_tiles, 1, tm)
    n_out = 2 if emit_x else 1
    done = lambda i: (jnp.maximum(i - 1, 0), 0)
    return pl.pallas_call(
        functools.partial(_combine_kernel, emit_x),
        grid=(n_tiles + 1,),
        in_specs=[
            pl.BlockSpec((1, 1, tm), lambda i: (jnp.minimum(i, n_tiles - 1), 0, 0),
                         memory_space=pltpu.SMEM),
            pl.BlockSpec((tm, D_MODEL), done), _const_spec((1, D_MODEL)),
            pl.BlockSpec(memory_space=pl.ANY),
        ],
        out_specs=[pl.BlockSpec((tm, D_MODEL), done)] * n_out,
        out_shape=[jax.ShapeDtypeStruct((t, D_MODEL), F32)] * n_out,
        scratch_shapes=[pltpu.VMEM((2, tm, D_MODEL), F32), pltpu.SemaphoreType.DMA((2,))],
        compiler_params=_cparams(("arbitrary",)),
        name="moe_combine",
    )(dest, x2d, gnext, ys)


def _s5_state_in(u_ref_val, wre_ref, wim_ref, sre_ref, sim_ref, pair_w):
    for m in range(S5_GB // 2):
        up = u_ref_val(m * pair_w, pair_w)
        sre_ref[:, m * LANES:(m + 1) * LANES] = _dot(up, wre_ref[m])
        sim_ref[:, m * LANES:(m + 1) * LANES] = _dot(up, wim_ref[m])


def _s5_outputs(u_ref_val, hre, him, m_ref, gre_ref, gim_ref, y_store, pair_w):
    gw = pair_w // 2
    for m in range(S5_GB // 2):
        hr = hre(m).astype(BF16)
        hi = him(m).astype(BF16)
        y = _dot(hr, gre_ref[m]) + _dot(hi, gim_ref[m])
        y0 = y[:, :gw] + _dot(u_ref_val(m * pair_w, gw), m_ref[2 * m])
        y1 = y[:, gw:] + _dot(u_ref_val(m * pair_w + gw, gw), m_ref[2 * m + 1])
        y_store(m * pair_w, gw, y0)
        y_store(m * pair_w + gw, gw, y1)


def _s5_flatten(load_rows, q, n, ut, uflat):
    gc = SSM_GROUP_CH
    qc = q * gc
    for s in range(q):
        ut[:, s * gc:(s + 1) * gc, :] = load_rows(s).astype(BF16).T.reshape(S5_GB, gc, n)
    for g in range(S5_GB):
        uflat[:, g * qc:(g + 1) * qc] = ut[g].T


def _s5_unflatten(yflat, yt, store_rows, q, n):
    gc = SSM_GROUP_CH
    qc = q * gc
    for g in range(S5_GB):
        yt[g] = yflat[:, g * qc:(g + 1) * qc].T
    for t in range(q):
        store_rows(t, yt[:, t * gc:(t + 1) * gc, :].reshape(S5_GB * gc, n).T)


def _s5_prompt_kernel(u_ref, wre_ref, wim_ref, m_ref, gre_ref, gim_ref, aqr_ref, aqi_ref,
                      y_ref, her_ref, hei_ref, ut, uflat, sre, sim, hre, him, yflat, yt):
    pair_w = 2 * S5_CHUNK * SSM_GROUP_CH
    n_chunks = u_ref.shape[1] // S5_CHUNK
    _s5_flatten(lambda s: u_ref[0, pl.ds(s, n_chunks, stride=S5_CHUNK), :],
                S5_CHUNK, n_chunks, ut, uflat)
    u_val = lambda off, w: uflat[:, off:off + w]
    _s5_state_in(u_val, wre_ref, wim_ref, sre, sim, pair_w)
    ar = aqr_ref[...]
    ai = aqi_ref[...]

    def step(n, carry):
        hr, hi = carry
        hre[pl.ds(n, 1), :] = hr
        him[pl.ds(n, 1), :] = hi
        sr = sre[pl.ds(n, 1), :]
        si = sim[pl.ds(n, 1), :]
        return ar * hr - ai * hi + sr, ar * hi + ai * hr + si

    zero = jnp.zeros(ar.shape, F32)
    hr, hi = lax.fori_loop(0, n_chunks, step, (zero, zero))
    her_ref[0] = hr
    hei_ref[0] = hi

    def y_store(off, w, val):
        yflat[:, off:off + w] = val

    _s5_outputs(u_val, lambda m: hre[:, m * LANES:(m + 1) * LANES],
                lambda m: him[:, m * LANES:(m + 1) * LANES],
                m_ref, gre_ref, gim_ref, y_store, pair_w)

    def store_rows(t, val):
        y_ref[0, pl.ds(t, n_chunks, stride=S5_CHUNK), :] = val

    _s5_unflatten(yflat, yt, store_rows, S5_CHUNK, n_chunks)


def _s5_sample_kernel(n_new, u_ref, h0r_ref, h0i_ref, wre_ref, wim_ref, m_ref, gre_ref, gim_ref,
                      aqr_ref, aqi_ref, y_ref, hnr_ref, hni_ref, ut, uflat, sre, sim, yflat, yt):
    pair_w = 2 * n_new * SSM_GROUP_CH
    seqs = h0r_ref.shape[0]
    _s5_flatten(lambda s: u_ref[pl.ds(s, seqs, stride=n_new), :], n_new, seqs, ut, uflat)
    u_val = lambda off, w: uflat[:, off:off + w]
    _s5_state_in(u_val, wre_ref, wim_ref, sre, sim, pair_w)
    ar = aqr_ref[...]
    ai = aqi_ref[...]
    h0r = h0r_ref[...]
    h0i = h0i_ref[...]
    hnr_ref[...] = ar * h0r - ai * h0i + sre[...]
    hni_ref[...] = ar * h0i + ai * h0r + sim[...]

    def y_store(off, w, val):
        yflat[:, off:off + w] = val

    _s5_outputs(u_val, lambda m: h0r_ref[:, m * LANES:(m + 1) * LANES],
                lambda m: h0i_ref[:, m * LANES:(m + 1) * LANES],
                m_ref, gre_ref, gim_ref, y_store, pair_w)

    def store_rows(t, val):
        y_ref[pl.ds(t, seqs, stride=n_new), :] = val

    _s5_unflatten(yflat, yt, store_rows, n_new, seqs)


def _s5_weight_specs(q, idx):
    qc = q * SSM_GROUP_CH
    np_ = S5_GB // 2
    st = S5_GB * SSM_STATE
    return [
        pl.BlockSpec((np_, 2 * qc, LANES), lambda *a: (idx(*a), 0, 0)),
        pl.BlockSpec((np_, 2 * qc, LANES), lambda *a: (idx(*a), 0, 0)),
        pl.BlockSpec((S5_GB, qc, qc), lambda *a: (idx(*a), 0, 0)),
        pl.BlockSpec((np_, LANES, 2 * qc), lambda *a: (idx(*a), 0, 0)),
        pl.BlockSpec((np_, LANES, 2 * qc), lambda *a: (idx(*a), 0, 0)),
        pl.BlockSpec((1, st), lambda *a: (0, idx(*a))),
        pl.BlockSpec((1, st), lambda *a: (0, idx(*a))),
    ]


def _s5_prompt_call(u, w):
    b, seq, _ = u.shape
    n_chunks = seq // S5_CHUNK
    gbl = S5_GB * SSM_GROUP_CH
    qc = S5_CHUNK * SSM_GROUP_CH
    st = S5_GB * SSM_STATE
    n_gb = SSM_GROUPS // S5_GB
    gb_of = lambda g, i: g
    tok = pl.BlockSpec((1, seq, gbl), lambda g, i: (i, 0, g))
    return pl.pallas_call(
        _s5_prompt_kernel,
        grid=(n_gb, b),
        in_specs=[tok] + _s5_weight_specs(S5_CHUNK, gb_of),
        out_specs=[
            tok,
            pl.BlockSpec((1, 1, st), lambda g, i: (i, 0, g)),
            pl.BlockSpec((1, 1, st), lambda g, i: (i, 0, g)),
        ],
        out_shape=[
            jax.ShapeDtypeStruct((b, seq, D_MODEL), F32),
            jax.ShapeDtypeStruct((b, 1, SSM_GROUPS * SSM_STATE), F32),
            jax.ShapeDtypeStruct((b, 1, SSM_GROUPS * SSM_STATE), F32),
        ],
        scratch_shapes=[
            pltpu.VMEM((S5_GB, qc, n_chunks), BF16),
            pltpu.VMEM((n_chunks, S5_GB * qc), BF16),
            pltpu.VMEM((n_chunks, st), F32), pltpu.VMEM((n_chunks, st), F32),
            pltpu.VMEM((n_chunks, st), F32), pltpu.VMEM((n_chunks, st), F32),
            pltpu.VMEM((n_chunks, S5_GB * qc), F32),
            pltpu.VMEM((S5_GB, qc, n_chunks), F32),
        ],
        compiler_params=_cparams(("parallel", "parallel")),
        name="s5_prompt",
    )(u, *w)


def _s5_sample_call(u2d, h0r, h0i, w, n_new):
    t = u2d.shape[0]
    b = t // n_new
    gbl = S5_GB * SSM_GROUP_CH
    qc = n_new * SSM_GROUP_CH
    st = S5_GB * SSM_STATE
    n_gb = SSM_GROUPS // S5_GB
    gb_of = lambda g: g
    state = pl.BlockSpec((b, st), lambda g: (0, g))
    tok = pl.BlockSpec((t, gbl), lambda g: (0, g))
    return pl.pallas_call(
        functools.partial(_s5_sample_kernel, n_new),
        grid=(n_gb,),
        in_specs=[tok, state, state] + _s5_weight_specs(n_new, gb_of),
        out_specs=[tok, state, state],
        out_shape=[
            jax.ShapeDtypeStruct((t, D_MODEL), F32),
            jax.ShapeDtypeStruct((b, SSM_GROUPS * SSM_STATE), F32),
            jax.ShapeDtypeStruct((b, SSM_GROUPS * SSM_STATE), F32),
        ],
        scratch_shapes=[
            pltpu.VMEM((S5_GB, qc, b), BF16),
            pltpu.VMEM((b, S5_GB * qc), BF16),
            pltpu.VMEM((b, st), F32), pltpu.VMEM((b, st), F32),
            pltpu.VMEM((b, S5_GB * qc), F32),
            pltpu.VMEM((S5_GB, qc, b), F32),
        ],
        compiler_params=_cparams(("parallel",)),
        name="s5_sample",
    )(u2d, h0r, h0i, *w)


def _s5_discretize(a_re, a_im, log_dt, b_re, b_im):
    delta = jnp.exp(log_dt.astype(F32))[:, None]
    lr, li = a_re.astype(F32), a_im.astype(F32)
    mag = jnp.exp(delta * lr)
    abar_r = mag * jnp.cos(delta * li)
    abar_i = mag * jnp.sin(delta * li)
    nr, ni = abar_r - 1.0, abar_i
    den = lr * lr + li * li
    coef_r = ((nr * lr + ni * li) / den)[..., None]
    coef_i = ((ni * lr - nr * li) / den)[..., None]
    br, bi = b_re.astype(F32), b_im.astype(F32)
    return delta * lr, delta * li, coef_r * br - coef_i * bi, coef_r * bi + coef_i * br


def _split3(x):
    x1 = x.astype(BF16)
    r1 = x - x1.astype(F32)
    x2 = r1.astype(BF16)
    x3 = (r1 - x2.astype(F32)).astype(BF16)
    return x1, x2, x3


def _s5_prep_kernel(qs, bbr_ref, bbi_ref, ctr_ref, cti_ref, ppr_ref, ppi_ref, pnr_ref, pni_ref,
                    *out_refs):
    c = SSM_GROUP_CH
    qc = max(qs) * c
    p2 = 2 * SSM_STATE
    col = lax.broadcasted_iota(I32, (c, qc), 1)
    row = lax.broadcasted_iota(I32, (c, qc), 0)
    by_ch = (col % c == row).astype(BF16)
    by_pos = (col // c == row).astype(BF16)

    def spread(x, sel):
        return sum(_dot(part, sel) for part in _split3(x))

    def dot_f32(a, b):
        a1, a2, _ = _split3(a)
        b1, b2, _ = _split3(b)
        return _dot(a1, b1) + _dot(a1, b2) + _dot(a2, b1)

    bc_r, bc_i = spread(bbr_ref[0], by_ch), spread(bbi_ref[0], by_ch)
    cc_r, cc_i = spread(ctr_ref[0], by_ch), spread(cti_ref[0], by_ch)
    pp_r, pp_i = spread(ppr_ref[0], by_pos), spread(ppi_ref[0], by_pos)
    pn_r, pn_i = spread(pnr_ref[0], by_pos), spread(pni_ref[0], by_pos)
    r_r = cc_r * pp_r - cc_i * pp_i
    r_i = cc_r * pp_i + cc_i * pp_r
    l_r = bc_r * pn_r - bc_i * pn_i
    l_i = bc_r * pn_i + bc_i * pn_r
    lt_r, lt_i = l_r.T, l_i.T
    grp_col = lax.broadcasted_iota(I32, (qc, p2), 1) // SSM_STATE
    causal =(lax.broadcasted_iota(I32, (qc, qc), 1) // c) >= (lax.broadcasted_iota(I32, (qc, qc), 0) // c)
    kers = [jnp.where(causal, dot_f32(jnp.where(grp_col == gl, lt_r, 0.0), r_r)
                      - dot_f32(jnp.where(grp_col == gl, lt_i, 0.0), r_i), 0.0) for gl in range(2)]
    for n, q in enumerate(qs):
        wre_ref, wim_ref, m_ref, gre_ref, gim_ref = out_refs[5 * n:5 * n + 5]
        w = q * c
        aq_r = ppr_ref[0][:, q - 1:q]
        aq_i = ppi_ref[0][:, q - 1:q]
        w_r = (l_r[:, :w] * aq_r - l_i[:, :w] * aq_i).T
        w_i = (l_r[:, :w] * aq_i + l_i[:, :w] * aq_r).T
        g_re, g_im = [], []
        col_grp = lax.broadcasted_iota(I32, (w, p2), 1) // SSM_STATE
        row_grp = lax.broadcasted_iota(I32, (p2, w), 0) // SSM_STATE
        for gl in range(2):
            wre_ref[0, gl * w:(gl + 1) * w, :] = jnp.where(col_grp == gl, w_r, 0.0).astype(BF16)
            wim_ref[0, gl * w:(gl + 1) * w, :] = jnp.where(col_grp == gl, w_i, 0.0).astype(BF16)
            m_ref[gl] = kers[gl][:w, :w].astype(BF16)
            g_re.append(jnp.where(row_grp == gl, r_r[:, :w], 0.0))
            g_im.append(jnp.where(row_grp == gl, -r_i[:, :w], 0.0))
        gre_ref[0] = jnp.concatenate(g_re, axis=1).astype(BF16)
        gim_ref[0] = jnp.concatenate(g_im, axis=1).astype(BF16)


def _s5_chunk_weights(tables, qs):
    n_pairs = SSM_GROUPS // 2
    p2 = 2 * SSM_STATE
    tab = pl.BlockSpec((1, p2, SSM_GROUP_CH), lambda m: (m, 0, 0))
    out_specs, out_shape = [], []
    for q in qs:
        qc = q * SSM_GROUP_CH
        out_specs += [pl.BlockSpec((1, 2 * qc, p2), lambda m: (m, 0, 0)),
                      pl.BlockSpec((1, 2 * qc, p2), lambda m: (m, 0, 0)),
                      pl.BlockSpec((2, qc, qc), lambda m: (m, 0, 0)),
                      pl.BlockSpec((1, p2, 2 * qc), lambda m: (m, 0, 0)),
                      pl.BlockSpec((1, p2, 2 * qc), lambda m: (m, 0, 0))]
        out_shape += [jax.ShapeDtypeStruct((n_pairs, 2 * qc, p2), BF16),
                      jax.ShapeDtypeStruct((n_pairs, 2 * qc, p2), BF16),
                      jax.ShapeDtypeStruct((SSM_GROUPS, qc, qc), BF16),
                      jax.ShapeDtypeStruct((n_pairs, p2, 2 * qc), BF16),
                      jax.ShapeDtypeStruct((n_pairs, p2, 2 * qc), BF16)]
    outs = pl.pallas_call(
        functools.partial(_s5_prep_kernel, tuple(qs)),
        grid=(n_pairs,),
        in_specs=[tab] * 8,
        out_specs=out_specs, out_shape=out_shape,
        compiler_params=_cparams(("parallel",)),
        name="s5_operators",
    )(*tables)
    return [tuple(outs[5 * n:5 * n + 5]) for n in range(len(qs))]


def _s5_tables(log_mag, phase, bbar_r, bbar_i, c_re, c_im, q_max):
    k = jnp.arange(1, q_max + 1, dtype=F32)[None, None, :]
    mag = jnp.exp(log_mag[:, :, None] * k)
    cos, sin = jnp.cos(phase[:, :, None] * k), jnp.sin(phase[:, :, None] * k)
    tabs = (bbar_r, bbar_i, c_re.astype(F32).transpose(0, 2, 1), c_im.astype(F32).transpose(0, 2, 1),
            mag * cos, mag * sin, cos / mag, -sin / mag)
    return tuple(t.reshape(SSM_GROUPS // 2, 2 * SSM_STATE, SSM_GROUP_CH) for t in tabs)


def _rope_tables(pos):
    half = HEAD_DIM // 2
    inv = 1.0 / (ROPE_THETA ** (jnp.arange(half, dtype=F32) * (2.0 / HEAD_DIM)))
    ang = pos.astype(F32)[:, None] * inv[None, :]
    cos, sin = jnp.cos(ang), jnp.sin(ang)
    return jnp.tile(cos, (1, 4)), jnp.concatenate([-sin, sin, -sin, sin], axis=1)


def _router_weights(gain, w_rg, b_rg, w_re, b_re):
    n_layers = gain.shape[0]
    pad = LANES - N_EXPERTS - N_EXPERT_GROUPS
    w = jnp.concatenate([w_re, w_rg, jnp.zeros((n_layers, D_MODEL, pad), F32)], axis=2)
    b = jnp.concatenate([b_re, b_rg, jnp.zeros((n_layers, pad), F32)], axis=1)
    wh = w.astype(BF16)
    wl = (w - wh.astype(F32)).astype(BF16)
    return [(gain[l].reshape(1, -1).astype(F32), wh[l], wl[l], b[l].reshape(1, LANES))
            for l in range(n_layers)]


def kernel(x_prompt, x_sample, cache_k, cache_v, state_ssm_re, state_ssm_im, norm_mix, norm_ffn, norm_final, attn_w_qkv, attn_b_qkv, attn_w_o, attn_b_o, attn_sinks, ssm_a_re, ssm_a_im, ssm_log_dt, ssm_b_re, ssm_b_im, ssm_c_re, ssm_c_im, ssm_d, ssm_w_glu_a, ssm_w_glu_b, moe_w_router_group, moe_b_router_group, moe_w_router_expert, moe_b_router_expert, moe_w_gate, moe_w_up, moe_w_down):
    bsz, seq, _ = x_prompt.shape
    dbs, n_new, _ = x_sample.shape
    rows = cache_k.shape[2]
    tp, ts = bsz * seq, dbs * n_new
    cap = _moe_cap(tp + ts)
    xp = x_prompt.reshape(tp, D_MODEL)
    xs = x_sample.reshape(ts, D_MODEL)

    row1 = lambda v: v.reshape(1, -1).astype(F32)
    routers = _router_weights(norm_ffn, moe_w_router_group, moe_b_router_group,
                              moe_w_router_expert, moe_b_router_expert)
    triu = jnp.triu(jnp.ones((TOKEN_TILE, TOKEN_TILE), F32), 1).astype(BF16)
    cnt0 = (jnp.zeros((LANES, LANES), F32), jnp.zeros((8, LANES), F32))

    slots = jnp.asarray(HEAD_SLOTS, dtype=I32)

    def q_slots(w):
        lead = w.shape[:-1]
        qh = w[..., :Q_DIM].reshape(*lead, N_HEADS, HEAD_DIM)[..., slots, :].reshape(*lead, Q_DIM)
        return jnp.concatenate([qh, w[..., Q_DIM:]], axis=-1)

    wqkv = q_slots(attn_w_qkv[0]).astype(BF16)
    bqkv = row1(q_slots(attn_b_qkv[0]))
    wo = attn_w_o[0].reshape(N_HEADS, HEAD_DIM, D_MODEL)[slots].reshape(Q_DIM, D_MODEL).astype(BF16)
    bo = row1(attn_b_o[0])
    sinks = attn_sinks[0].astype(F32)[slots]
    g_mix0, g_mix1 = row1(norm_mix[0]), row1(norm_mix[1])
    pos = jnp.concatenate([jnp.arange(seq, dtype=I32),
                           jnp.tile(PAST_LEN + jnp.arange(n_new, dtype=I32), dbs)])
    cos_t, sin_t = _rope_tables(pos)

    qp, kp, vp = _qkv_call(xp, g_mix0, wqkv, bqkv, cos_t, sin_t, (0, seq), BF16)
    qs, ks, vs = _qkv_call(xs, g_mix0, wqkv, bqkv, cos_t, sin_t, (seq, ts), F32)
    op = _attn_prompt_call(jnp.repeat(sinks, BLOCK).reshape(1, -1), qp.reshape(bsz, seq, Q_DIM),
                           kp.reshape(bsz, seq, KV_DIM),
                           vp.reshape(bsz, seq, KV_DIM))
    os_, nks, nvs = _attn_sample_call(jnp.repeat(sinks, n_new).reshape(1, -1), qs, ks, vs,
                                      cache_k[0].reshape(dbs, rows, KV_DIM),
                                      cache_v[0].reshape(dbs, rows, KV_DIM), n_new, 8)
    xp1, dest_p, cnt, rows_x, _ = _oproj_call(op.reshape(tp, Q_DIM), xp, wo, bo, routers[0], triu,
                                              cnt0, None, False, cap)
    xs1, dest_s, cnt, rows_x, steps = _oproj_call(os_, xs, wo, bo, routers[0], triu, cnt, rows_x,
                                                  True, cap)
    rows_y = _expert_call(steps, rows_x, moe_w_gate, moe_w_up, moe_w_down, 0, cap)
    xp2, up = _combine_call(dest_p, xp1, rows_y, g_mix1, True)
    xs2, us = _combine_call(dest_s, xs1, rows_y, g_mix1, True)

    disc = _s5_discretize(ssm_a_re[0], ssm_a_im[0], ssm_log_dt[0], ssm_b_re[0], ssm_b_im[0])
    tables = _s5_tables(*disc, ssm_c_re[0], ssm_c_im[0], S5_CHUNK)
    a_pow = lambda q: [tables[i][:, :, q - 1].reshape(1, -1) for i in (4, 5)]
    ops_p, ops_s = _s5_chunk_weights(tables, (S5_CHUNK, n_new))
    w_p = (*ops_p, *a_pow(S5_CHUNK))
    w_s = (*ops_s, *a_pow(n_new))
    y_p, hpr, hpi = _s5_prompt_call(up.reshape(bsz, seq, D_MODEL), w_p)
    y_p = y_p.reshape(tp, D_MODEL)
    h0r = state_ssm_re[0].reshape(dbs, -1).astype(F32)
    h0i = state_ssm_im[0].reshape(dbs, -1).astype(F32)
    y_s, hsr, hsi = _s5_sample_call(us, h0r, h0i, w_s, n_new)

    wa, wb = ssm_w_glu_a[0].astype(BF16), ssm_w_glu_b[0].astype(BF16)
    d_row = row1(ssm_d[0])
    xp3, dest_p, cnt, rows_x, _ = _glu_call(y_p, xp2, g_mix1, d_row, wa, wb, routers[1], triu,
                                            cnt0, None, False, cap)
    xs3, dest_s, cnt, rows_x, steps = _glu_call(y_s, xs2, g_mix1, d_row, wa, wb, routers[1], triu,
                                                cnt, rows_x, True, cap)
    rows_y = _expert_call(steps, rows_x, moe_w_gate, moe_w_up, moe_w_down, 1, cap)
    g_fin = row1(norm_final)
    (yp,) = _combine_call(dest_p, xp3, rows_y, g_fin, False)
    (ys,) = _combine_call(dest_s, xs3, rows_y, g_fin, False)

    kv5 = lambda a, n: a.reshape(1, n, rows, KV_HEADS, HEAD_DIM)
    st4 = lambda a, n: a.reshape(1, n, SSM_GROUPS, SSM_STATE)
    k_last = kp.reshape(bsz, seq, KV_DIM)[:, seq - WINDOW:]
    v_last = vp.reshape(bsz, seq, KV_DIM)[:, seq - WINDOW:]
    return (yp.reshape(bsz, seq, D_MODEL), ys.reshape(dbs, n_new, D_MODEL),
            k_last.reshape(1, bsz, WINDOW, KV_HEADS, HEAD_DIM), kv5(nks, dbs),
            v_last.reshape(1, bsz, WINDOW, KV_HEADS, HEAD_DIM), kv5(nvs, dbs),
            st4(hpr, bsz), st4(hsr, dbs), st4(hpi, bsz), st4(hsi, dbs))
```

```python
import collections
import functools
import math

import jax
import jax.numpy as jnp
from jax import lax
from jax.experimental import pallas as pl
from jax.experimental.pallas import tpu as pltpu

F32 = jnp.float32
BF16 = jnp.bfloat16
I32 = jnp.int32

D_MODEL = 1024
N_HEADS = 16
KV_HEADS = 4
HEAD_DIM = 64
Q_DIM = N_HEADS * HEAD_DIM
KV_DIM = KV_HEADS * HEAD_DIM
QKV_DIM = Q_DIM + 2 * KV_DIM
WINDOW = 128
BLOCK = 128
ROPE_THETA = 10000.0
PAST_LEN = 16384
SSM_GROUP_CH = 16
SSM_GROUPS = D_MODEL // SSM_GROUP_CH
SSM_STATE = 64
N_EXPERT_GROUPS = 4
EXPERTS_PER_GROUP = 8
N_EXPERTS = N_EXPERT_GROUPS * EXPERTS_PER_GROUP
EXPERT_FF = 128
NORM_EPS = 1e-5

LANES = 128
VMEM_LIMIT = 56 * 1024 * 1024
S5_CHUNK = 16
S5_GB = 8
S5_SEQS = 4
TOKEN_TILE = 1024
MOE_TILE = 512
ROUTE_PARTS = 2
COMBINE_TILE = 512
PAY_X = D_MODEL
PAY_W = PAY_X + LANES


def _cparams(sem):
    return pltpu.CompilerParams(dimension_semantics=sem, vmem_limit_bytes=VMEM_LIMIT)


def _rms(x, g):
    return x * lax.rsqrt(jnp.mean(x * x, axis=-1, keepdims=True) + NORM_EPS) * g


def _dot(a, b):
    return jnp.dot(a, b, preferred_element_type=F32)


def _dot_nt(a, b):
    return lax.dot_general(a, b, (((1,), (1,)), ((), ())), preferred_element_type=F32)


def _sigmoid(x):
    return 1.0 / (1.0 + jnp.exp(-x))


def _qkv_kernel(x_ref, g_ref, w_ref, b_ref, cos_ref, sin_ref, q_ref, k_ref, v_ref):
    xn = _rms(x_ref[...], g_ref[...])
    qkv = _dot(xn.astype(BF16), w_ref[...]) + b_ref[...]
    cos = cos_ref[...]
    sin = sin_ref[...]
    lane = lax.broadcasted_iota(I32, cos.shape, 1)
    first_half = (lane % HEAD_DIM) < (HEAD_DIM // 2)
    n_rot = (Q_DIM + KV_DIM) // LANES
    for c in range(n_rot):
        blk = qkv[:, c * LANES:(c + 1) * LANES]
        partner = jnp.where(first_half,
                            pltpu.roll(blk, LANES - HEAD_DIM // 2, 1),
                            pltpu.roll(blk, HEAD_DIM // 2, 1))
        rot = blk * cos + partner * sin
        if c < Q_DIM // LANES:
            q_ref[:, c * LANES:(c + 1) * LANES] = (rot * (HEAD_DIM ** -0.5)).astype(q_ref.dtype)
        else:
            k_ref[:, c * LANES - Q_DIM:(c + 1) * LANES - Q_DIM] = rot
    v_ref[...] = qkv[:, Q_DIM + KV_DIM:]


def _qkv_call(x2d, gain, w_bf, bias, cos_t, sin_t, pos_rows, q_dtype):
    t = x2d.shape[0]
    tm = TOKEN_TILE
    first, n_pos = pos_rows[0] // tm, pos_rows[1] // tm
    return pl.pallas_call(
        _qkv_kernel,
        grid=(t // tm,),
        in_specs=[
            pl.BlockSpec((tm, D_MODEL), lambda i: (i, 0)),
            pl.BlockSpec((1, D_MODEL), lambda i: (0, 0)),
            pl.BlockSpec((D_MODEL, QKV_DIM), lambda i: (0, 0)),
            pl.BlockSpec((1, QKV_DIM), lambda i: (0, 0)),
            pl.BlockSpec((tm, LANES), lambda i: (first + i % n_pos, 0)),
            pl.BlockSpec((tm, LANES), lambda i: (first + i % n_pos, 0)),
        ],
        out_specs=[
            pl.BlockSpec((tm, Q_DIM), lambda i: (i, 0)),
            pl.BlockSpec((tm, KV_DIM), lambda i: (i, 0)),
            pl.BlockSpec((tm, KV_DIM), lambda i: (i, 0)),
        ],
        out_shape=[
            jax.ShapeDtypeStruct((t, Q_DIM), q_dtype),
            jax.ShapeDtypeStruct((t, KV_DIM), F32),
            jax.ShapeDtypeStruct((t, KV_DIM), F32),
        ],
        compiler_params=_cparams(("parallel",)),
        name="qkv_rope",
    )(x2d, gain, w_bf, bias, cos_t, sin_t)


HEAD_SLOTS = tuple(8 * c + 4 * half + i for c in range(2) for i in range(4) for half in range(2))
LOG2E = math.log2(math.e)


def _attn_prompt_kernel(sink_ref, q_ref, kp_ref, kc_ref, vp_ref, vc_ref, o_ref):
    n = pl.program_id(1)
    k2 = jnp.concatenate([kp_ref[0], kc_ref[0]], axis=0).astype(BF16)
    v2 = jnp.concatenate([vp_ref[0], vc_ref[0]], axis=0).astype(BF16)
    key = lax.broadcasted_iota(I32, (2 * BLOCK, 2 * BLOCK), 0)
    qry = lax.broadcasted_iota(I32, (2 * BLOCK, 2 * BLOCK), 1) % BLOCK
    mask = (key >= qry) & (key <= qry + WINDOW) & ((n > 0) | (key >= BLOCK))
    low = lax.broadcasted_iota(I32, (BLOCK, LANES), 1) < HEAD_DIM
    keep_low = low.astype(BF16)
    keep_high = 1 - keep_low
    n_pairs = N_HEADS // 2
    chunk = lambda pr: slice((pr // (n_pairs // 2)) * LANES, (pr // (n_pairs // 2) + 1) * LANES)

    def scores(pr):
        qblk = q_ref[0, :, pr * LANES:(pr + 1) * LANES]
        qz = jnp.concatenate([qblk * keep_low, qblk * keep_high], axis=0)
        s = _dot_nt(k2[:, chunk(pr)], qz) * LOG2E
        return jnp.where(mask, s, -jnp.inf)

    def weights(pr, s):
        sink = sink_ref[:, 2 * pr * BLOCK:(2 * pr + 2) * BLOCK] * LOG2E
        m = jnp.maximum(jnp.max(s, axis=0, keepdims=True), sink)
        p = jnp.exp2(s - m)
        denom = jnp.sum(p, axis=0, keepdims=True) + jnp.exp2(sink - m)
        return (p * (1.0 / denom)).T.astype(BF16)

    def finish(pr, w):
        oz = _dot(w, v2[:, chunk(pr)])
        o_ref[0, :, pr * LANES:(pr + 1) * LANES] = jnp.where(low, oz[:BLOCK], oz[BLOCK:]).astype(o_ref.dtype)

    all_s = [scores(pr) for pr in range(n_pairs)]
    all_w = [weights(pr, s) for pr, s in enumerate(all_s)]
    for pr, w in enumerate(all_w):
        finish(pr, w)


def _attn_prompt_call(sinks, q, k, v):
    b, l, _ = q.shape
    nb = l // BLOCK
    kv_cur = pl.BlockSpec((1, BLOCK, KV_DIM), lambda i, n: (i, n, 0))
    kv_prev = pl.BlockSpec((1, BLOCK, KV_DIM), lambda i, n: (i, jnp.maximum(n - 1, 0), 0))
    return pl.pallas_call(
        _attn_prompt_kernel,
        grid=(b, nb),
        in_specs=[
            pl.BlockSpec((1, N_HEADS * BLOCK), lambda i, n: (0, 0)),
            pl.BlockSpec((1, BLOCK, Q_DIM), lambda i, n: (i, n, 0)),
            kv_prev, kv_cur, kv_prev, kv_cur,
        ],
        out_specs=pl.BlockSpec((1, BLOCK, Q_DIM), lambda i, n: (i, n, 0)),
        out_shape=jax.ShapeDtypeStruct((b, l, Q_DIM), BF16),
        compiler_params=_cparams(("parallel", "parallel")),
        name="attn_prompt",
    )(sinks, q, k, k, v, v)


def _attn_sample_kernel(n_new, seqs, sink_ref, q_ref, kn_ref, vn_ref, ck_ref, cv_ref,
                        o_ref, nk_ref, nv_ref):
    rows = ck_ref.shape[1]
    keys = 2 * rows
    n_cols = N_HEADS * n_new
    key = lax.broadcasted_iota(I32, (keys, n_cols), 0)
    qry = lax.broadcasted_iota(I32, (keys, n_cols), 1) % n_new
    mask = ((key < rows) & (key >= qry)) | ((key >= rows) & (key - rows <= qry))
    low = lax.broadcasted_iota(I32, (n_new, LANES), 1) < HEAD_DIM
    sink = sink_ref[...]
    pad = jnp.zeros((rows - n_new, KV_DIM), F32)

    def body(sb, carry):
        r0 = pl.multiple_of(sb * n_new, n_new)
        k_new = kn_ref[pl.ds(r0, n_new), :]
        v_new = vn_ref[pl.ds(r0, n_new), :]
        k_all = jnp.concatenate([ck_ref[sb], k_new, pad], axis=0).astype(BF16)
        v_all = jnp.concatenate([cv_ref[sb], v_new, pad], axis=0).astype(BF16)
        blocks = []
        for slot in range(N_HEADS):
            pair = q_ref[pl.ds(r0, n_new), (slot // 2) * LANES:(slot // 2 + 1) * LANES]
            blk = jnp.where(low if slot % 2 == 0 else ~low, pair, 0.0)
            zero = jnp.zeros_like(blk)
            blocks.append(jnp.concatenate([blk, zero] if slot < N_HEADS // 2 else [zero, blk], axis=1))
        qz = jnp.concatenate(blocks, axis=0).astype(BF16)
        s = _dot_nt(k_all, qz)
        s = jnp.where(mask, s, -jnp.inf)
        m = jnp.maximum(jnp.max(s, axis=0, keepdims=True), sink)
        p = jnp.exp(s - m)
        denom = jnp.sum(p, axis=0, keepdims=True) + jnp.exp(sink - m)
        w = (p * (1.0 / denom)).T.astype(BF16)
        oz = _dot(w, v_all)
        for pr in range(N_HEADS // 2):
            lanes = slice((pr // (N_HEADS // 4)) * LANES, (pr // (N_HEADS // 4) + 1) * LANES)
            a = oz[2 * pr * n_new:(2 * pr + 1) * n_new, lanes]
            b = oz[(2 * pr + 1) * n_new:(2 * pr + 2) * n_new, lanes]
            o_ref[pl.ds(r0, n_new), pr * LANES:(pr + 1) * LANES] = jnp.where(low, a, b)
        nk_ref[sb, 0:rows - n_new, :] = ck_ref[sb, n_new:rows, :]
        nk_ref[sb, rows - n_new:rows, :] = k_new
        nv_ref[sb, 0:rows - n_new, :] = cv_ref[sb, n_new:rows, :]
        nv_ref[sb, rows - n_new:rows, :] = v_new
        return carry

    lax.fori_loop(0, seqs, body, 0)


def _attn_sample_call(sink_row, q, k_new, v_new, cache_k, cache_v, n_new, seqs):
    b, rows, _ = cache_k.shape
    tok = pl.BlockSpec((seqs * n_new, Q_DIM), lambda i: (i, 0))
    tok_kv = pl.BlockSpec((seqs * n_new, KV_DIM), lambda i: (i, 0))
    cache = pl.BlockSpec((seqs, rows, KV_DIM), lambda i: (i, 0, 0))
    return pl.pallas_call(
        functools.partial(_attn_sample_kernel, n_new, seqs),
        grid=(b // seqs,),
        in_specs=[pl.BlockSpec((1, N_HEADS * n_new), lambda i: (0, 0)), tok, tok_kv, tok_kv, cache, cache],
        out_specs=[tok, cache, cache],
        out_shape=[
            jax.ShapeDtypeStruct((b * n_new, Q_DIM), F32),
            jax.ShapeDtypeStruct((b, rows, KV_DIM), F32),
            jax.ShapeDtypeStruct((b, rows, KV_DIM), F32),
        ],
        compiler_params=_cparams(("parallel",)),
        name="attn_sample",
    )(sink_row, q, k_new, v_new, cache_k, cache_v)


def _moe_cap(t_total):
    return t_total + MOE_TILE


def _expert_steps(cap):
    return cap // MOE_TILE - 1 + N_EXPERT_GROUPS


def _write_step_table(maps_ref, counts, cap):
    per = cap // MOE_TILE
    shift = MOE_TILE.bit_length() - 1
    ends = []
    for c in counts:
        tiles = lax.shift_right_logical(c + (MOE_TILE - 1), shift)
        ends.append(tiles if not ends else ends[-1] + tiles)
    total = ends[-1]
    for j in range(maps_ref.shape[1]):
        jj = jnp.maximum(jnp.minimum(j, total - 1), 0)
        g = sum((jj >= e).astype(I32) for e in ends[:-1])
        start = sum(jnp.where(g > k, ends[k] - (ends[k - 1] if k else 0), 0) for k in range(len(ends) - 1))
        blk_in = g * per + jj - start
        valid = (total > j).astype(I32)
        maps_ref[0, j] = blk_in
        maps_ref[1, j] = jnp.where(valid == 1, blk_in, len(counts) * per)
        maps_ref[2, j] = g
        maps_ref[3, j] = valid


def _route_rows(x1, gn_ref, wrh_ref, wrl_ref, br_ref):
    tm = x1.shape[0]
    xn = _rms(x1, gn_ref[...])
    xh = xn.astype(BF16)
    xl = (xn - xh.astype(F32)).astype(BF16)
    logits = _dot(xh, wrh_ref[...]) + _dot(xl, wrh_ref[...]) + _dot(xh, wrl_ref[...]) + br_ref[...]
    lt = logits.T
    ge = N_EXPERTS // EXPERTS_PER_GROUP
    sub = lax.broadcasted_iota(I32, (EXPERTS_PER_GROUP, tm), 0).astype(F32)
    big = jnp.float32(LANES)
    neg = -jnp.inf
    gl = jnp.where(sub < N_EXPERT_GROUPS, lt[N_EXPERTS:N_EXPERTS + EXPERTS_PER_GROUP], neg)
    gmax = jnp.max(gl, axis=0, keepdims=True)
    g_val = 1.0 / jnp.sum(jnp.exp(gl - gmax), axis=0, keepdims=True)
    g_idx = jnp.min(jnp.where(gl == gmax, sub, big), axis=0, keepdims=True)
    el = lt[0:EXPERTS_PER_GROUP]
    for g in range(1, ge):
        el = jnp.where(g_idx == g, lt[g * EXPERTS_PER_GROUP:(g + 1) * EXPERTS_PER_GROUP], el)
    e1 = jnp.max(el, axis=0, keepdims=True)
    i1 = jnp.min(jnp.where(el == e1, sub, big), axis=0, keepdims=True)
    el2 = jnp.where(sub == i1, neg, el)
    e2 = jnp.max(el2, axis=0, keepdims=True)
    i2 = jnp.min(jnp.where(el2 == e2, sub, big), axis=0, keepdims=True)
    t = jnp.exp(e2 - e1)
    w1 = 1.0 / (1.0 + t)
    w2 = t / (1.0 + t)
    wts_t = g_val * (jnp.where(sub == i1, w1, 0.0) + jnp.where(sub == i2, w2, 0.0))
    wts = jnp.concatenate([wts_t, jnp.zeros((LANES - EXPERTS_PER_GROUP, tm), F32)], axis=0).T
    return xn, wts, g_idx


def _route_begin(cntc_in, cntr_in, xs_hbm, pay, cntc, cntr, sem):
    i = pl.program_id(0)
    slot = i % 2

    @pl.when(i == 0)
    def _():
        cntc[...] = cntc_in[...]
        cntr[...] = cntr_in[...]

    @pl.when(i >= 2)
    def _():
        pltpu.make_async_copy(pay.at[slot], xs_hbm.at[pl.ds(0, pay.shape[1]), :], sem.at[slot]).wait()


def _route_scatter(final, cap, parts, triu_ref, dest_ref, cntc_out, cntr_out, xs_hbm, maps_ref,
                   pay, zrows, dest_v, dest_s, cntc, cntr, sem):
    i = pl.program_id(0)
    n = pl.num_programs(0)
    slot = i % 2
    tm = pay.shape[1]

    def wait_slot(s):
        pltpu.make_async_copy(pay.at[s], xs_hbm.at[pl.ds(0, tm), :], sem.at[s]).wait()

    r0 = 0
    for xn, wts, _ in parts:
        pay[slot, r0:r0 + xn.shape[0], :PAY_X] = xn
        pay[slot, r0:r0 + xn.shape[0], PAY_X:] = wts
        r0 += xn.shape[0]
    g_idx = jnp.concatenate([p[2] for p in parts], axis=1)

    grp_t = lax.broadcasted_iota(I32, (LANES, tm), 0).astype(F32)
    oht = (grp_t == g_idx).astype(F32)
    rank = _dot(oht.astype(BF16), triu_ref[...])
    grp = lax.broadcasted_iota(I32, (LANES, 1), 0).astype(F32)
    base = grp * float(cap) + cntc[:, 0:1]
    dest = jnp.sum(oht * (rank + base), axis=0, keepdims=True).astype(I32)
    cntc[...] = cntc[...] + jnp.sum(oht, axis=1, keepdims=True)
    dest_ref[0] = dest
    dest_v[...] = dest
    pltpu.sync_copy(dest_v, dest_s)

    for s in range(2):
        @pl.when(slot == s)
        def _():
            for r in range(tm):
                pltpu.make_async_copy(pay.at[s, pl.ds(r, 1), :],
                                      xs_hbm.at[pl.ds(dest_s[0, r], 1), :], sem.at[s]).start()

    @pl.when(i == n - 1)
    def _():
        cntc_out[...] = cntc[...]
        cntr[...] = cntc[...].T[0:8, :]
        cntr_out[...] = cntr[...]

        @pl.when(n >= 2)
        def _():
            wait_slot(1 - slot)

        wait_slot(slot)
        dest_v[:, 0:LANES] = cntr[0:1, :].astype(I32)
        pltpu.sync_copy(dest_v, dest_s)
        counts = [dest_s[0, g] for g in range(N_EXPERT_GROUPS)]
        _write_step_table(maps_ref, counts, cap)
        if final:
            zrows[...] = jnp.zeros_like(zrows)
            starts = []
            for g in range(N_EXPERT_GROUPS):
                c_g = counts[g]
                starts.append(pl.multiple_of(g * cap + lax.shift_left(lax.shift_right_logical(c_g + 7, 3), 3), 8))
                for k in range(7):
                    pltpu.make_async_copy(zrows.at[pl.ds(0, 1), :],
                                          xs_hbm.at[pl.ds(g * cap + c_g + k, 1), :], sem.at[1]).start()
            for _ in range(7 * N_EXPERT_GROUPS):
                pltpu.make_async_copy(zrows.at[pl.ds(0, 1), :], xs_hbm.at[pl.ds(0, 1), :], sem.at[1]).wait()
            for g in range(N_EXPERT_GROUPS):
                pltpu.make_async_copy(zrows, xs_hbm.at[pl.ds(starts[g], MOE_TILE), :], sem.at[0]).start()
            for g in range(N_EXPERT_GROUPS):
                pltpu.make_async_copy(zrows, xs_hbm.at[pl.ds(0, MOE_TILE), :], sem.at[0]).wait()


RouteRefs = collections.namedtuple(
    "RouteRefs", "gn wrh wrl br triu cntc_in cntr_in x1 dest cntc_out cntr_out xs maps "
                 "pay zrows dest_v dest_s cntc cntr sem")


def _route_refs(rest, has_prev):
    if has_prev:
        rest = rest[:7] + rest[8:]
    return RouteRefs(*rest)


def _route_tile(final, cap, r, x1_parts):
    parts = [_route_rows(x1, r.gn, r.wrh, r.wrl, r.br) for x1 in x1_parts]
    _route_scatter(final, cap, parts, r.triu, r.dest, r.cntc_out, r.cntr_out, r.xs, r.maps,
                   r.pay, r.zrows, r.dest_v, r.dest_s, r.cntc, r.cntr, r.sem)


def _oproj_kernel(has_prev, final, cap, *refs):
    (o_ref, x_ref, wo_ref, bo_ref) = refs[:4]
    r = _route_refs(refs[4:], has_prev)
    _route_begin(r.cntc_in, r.cntr_in, r.xs, r.pay, r.cntc, r.cntr, r.sem)
    rows = x_ref.shape[0] // ROUTE_PARTS
    x1_parts = []
    for h in range(ROUTE_PARTS):
        rs = slice(h * rows, (h + 1) * rows)
        x1 = x_ref[rs, :] + _dot(o_ref[rs, :].astype(BF16), wo_ref[...]) + bo_ref[...]
        r.x1[rs, :] = x1
        x1_parts.append(x1)
    _route_tile(final, cap, r, x1_parts)


def _gelu_tanh(x):
    return x * (0.5 * (1.0 + jnp.tanh(math.sqrt(2.0 / math.pi) * (x + 0.044715 * (x * x * x)))))


def _glu_kernel(has_prev, final, cap, *refs):
    (y_ref, x_ref, gm_ref, d_ref, wa_ref, wb_ref) = refs[:6]
    r = _route_refs(refs[6:], has_prev)
    _route_begin(r.cntc_in, r.cntr_in, r.xs, r.pay, r.cntc, r.cntr, r.sem)
    rows = x_ref.shape[0] // ROUTE_PARTS
    gated = []
    for h in range(ROUTE_PARTS):
        rs = slice(h * rows, (h + 1) * rows)
        x = x_ref[rs, :]
        z = _gelu_tanh(y_ref[rs, :] + d_ref[...] * _rms(x, gm_ref[...])).astype(BF16)
        gated.append((rs, x, _dot(z, wa_ref[...]), _dot(z, wb_ref[...])))
    x1_parts = []
    for rs, x, a, b in gated:
        x1 = x + a * _sigmoid(b)
        r.x1[rs, :] = x1
        x1_parts.append(x1)
    _route_tile(final, cap, r, x1_parts)


def _row_spec(tm, width):
    return pl.BlockSpec((tm, width), lambda i: (i, 0))


def _const_spec(shape):
    return pl.BlockSpec(shape, lambda i: (0,) * len(shape))


def _mixer_call(body, name, lead_args, lead_specs, x2d, router, triu, cnt, xs_prev, final, cap):
    t = x2d.shape[0]
    tm = TOKEN_TILE
    n_tiles = t // tm
    gn, wrh, wrl, br = router
    cntc, cntr = cnt
    has_prev = xs_prev is not None
    in_specs = lead_specs + [
        _const_spec((1, D_MODEL)), _const_spec((D_MODEL, LANES)), _const_spec((D_MODEL, LANES)),
        _const_spec((1, LANES)), _const_spec((tm, tm)), _const_spec((LANES, LANES)), _const_spec((8, LANES)),
    ]
    args = list(lead_args) + [gn, wrh, wrl, br, triu, cntc, cntr]
    aliases = {}
    if has_prev:
        in_specs.append(pl.BlockSpec(memory_space=pl.ANY))
        args.append(xs_prev)
        aliases = {len(args) - 1: 4}
    x1, dest, cntc2, cntr2, xs, maps = pl.pallas_call(
        functools.partial(body, has_prev, final, cap),
        grid=(n_tiles,),
        in_specs=in_specs,
        out_specs=[
            _row_spec(tm, D_MODEL),
            pl.BlockSpec((1, 1, tm), lambda i: (i, 0, 0)),
            _const_spec((LANES, LANES)), _const_spec((8, LANES)),
            pl.BlockSpec(memory_space=pl.ANY),
            pl.BlockSpec(memory_space=pltpu.SMEM),
        ],
        out_shape=[
            jax.ShapeDtypeStruct((t, D_MODEL), F32),
            jax.ShapeDtypeStruct((n_tiles, 1, tm), I32),
            jax.ShapeDtypeStruct((LANES, LANES), F32),
            jax.ShapeDtypeStruct((8, LANES), F32),
            jax.ShapeDtypeStruct((N_EXPERT_GROUPS * cap, PAY_W), F32),
            jax.ShapeDtypeStruct((4, _expert_steps(cap)), I32),
        ],
        scratch_shapes=[
            pltpu.VMEM((2, tm, PAY_W), F32),
            pltpu.VMEM((MOE_TILE, PAY_W), F32),
            pltpu.VMEM((1, tm), I32),
            pltpu.SMEM((1, tm), I32),
            pltpu.VMEM((LANES, LANES), F32),
            pltpu.VMEM((8, LANES), F32),
            pltpu.SemaphoreType.DMA((2,)),
        ],
        input_output_aliases=aliases,
        compiler_params=_cparams(("arbitrary",)),
        name=name,
    )(*args)
    return x1, dest, (cntc2, cntr2), xs, maps


def _oproj_call(o2d, x2d, wo_bf, bo, router, triu, cnt, xs_prev, final, cap):
    tm = TOKEN_TILE
    lead_specs = [_row_spec(tm, Q_DIM), _row_spec(tm, D_MODEL), _const_spec((Q_DIM, D_MODEL)),
                  _const_spec((1, D_MODEL))]
    return _mixer_call(_oproj_kernel, "oproj_route", [o2d, x2d, wo_bf, bo], lead_specs, x2d,
                       router, triu, cnt, xs_prev, final, cap)


def _glu_call(y2d, x2d, gm, d, wa_bf, wb_bf, router, triu, cnt, xs_prev, final, cap):
    tm = TOKEN_TILE
    lead_specs = [_row_spec(tm, D_MODEL), _row_spec(tm, D_MODEL), _const_spec((1, D_MODEL)),
                  _const_spec((1, D_MODEL)), _const_spec((D_MODEL, D_MODEL)),
                  _const_spec((D_MODEL, D_MODEL))]
    return _mixer_call(_glu_kernel, "glu_route", [y2d, x2d, gm, d, wa_bf, wb_bf], lead_specs, x2d,
                       router, triu, cnt, xs_prev, final, cap)


def _expert_kernel(maps_ref, xs_ref, wg_ref, wu_ref, wd_ref, ys_ref, wgu, wdn, hid):
    j = pl.program_id(0)
    f = EXPERT_FF
    changed = (j == 0) | (maps_ref[2, j] != maps_ref[2, jnp.maximum(j - 1, 0)])
    valid = maps_ref[3, j]

    @pl.when(changed)
    def _():
        for e in range(EXPERTS_PER_GROUP):
            wgu[e, :, :f] = wg_ref[e].astype(BF16)
            wgu[e, :, f:] = wu_ref[e].astype(BF16)
            wdn[e * f:(e + 1) * f, :] = wd_ref[e].astype(BF16)

    @pl.when(valid == 1)
    def _():
        x = xs_ref[:, :PAY_X].astype(BF16)
        wts = xs_ref[:, PAY_X:]
        for e in range(EXPERTS_PER_GROUP):
            gu = _dot(x, wgu[e])
            g, u = gu[:, :f], gu[:, f:]
            hid[:, e * f:(e + 1) * f] = ((g * _sigmoid(g)) * u * wts[:, e:e + 1]).astype(BF16)
        ys_ref[...] = _dot(hid[...], wdn[...])

    @pl.when(valid == 0)
    def _():
        ys_ref[...] = jnp.zeros_like(ys_ref)


def _expert_call(maps, xs, w_gate, w_up, w_down, layer, cap):
    n_steps = maps.shape[1]
    e, f = EXPERTS_PER_GROUP, EXPERT_FF
    w_gate = w_gate.reshape(-1, D_MODEL, f)
    w_up = w_up.reshape(-1, D_MODEL, f)
    w_down = w_down.reshape(-1, f, D_MODEL)
    w_blk = lambda j, m: (m[2, j] + layer * N_EXPERT_GROUPS, 0, 0)
    grid_spec = pltpu.PrefetchScalarGridSpec(
        num_scalar_prefetch=1,
        grid=(n_steps,),
        in_specs=[
            pl.BlockSpec((MOE_TILE, PAY_W), lambda j, m: (m[0, j], 0)),
            pl.BlockSpec((e, D_MODEL, f), w_blk),
            pl.BlockSpec((e, D_MODEL, f), w_blk),
            pl.BlockSpec((e, f, D_MODEL), w_blk),
        ],
        out_specs=pl.BlockSpec((MOE_TILE, D_MODEL), lambda j, m: (m[1, j], 0)),
        scratch_shapes=[
            pltpu.VMEM((e, D_MODEL, 2 * f), BF16),
            pltpu.VMEM((e * f, D_MODEL), BF16),
            pltpu.VMEM((MOE_TILE, e * f), BF16),
        ],
    )
    return pl.pallas_call(
        _expert_kernel,
        grid_spec=grid_spec,
        out_shape=jax.ShapeDtypeStruct((N_EXPERT_GROUPS * cap + MOE_TILE, D_MODEL), F32),
        compiler_params=_cparams(("arbitrary",)),
        name="moe_experts",
    )(maps, xs, w_gate, w_up, w_down)


def _combine_kernel(emit_x, dest_ref, x_ref, gnext_ref, ys_hbm, *rest):
    outs, (ybuf, sem) = rest[:-2], rest[-2:]
    i = pl.program_id(0)
    n_tiles = pl.num_programs(0) - 1
    slot = i % 2
    tm = x_ref.shape[0]

    for s in range(2):
        @pl.when((i < n_tiles) & (slot == s))
        def _():
            for r in range(tm):
                pltpu.make_async_copy(ys_hbm.at[pl.ds(dest_ref[0, 0, r], 1), :],
                                      ybuf.at[s, pl.ds(r, 1), :], sem.at[s]).start()

    @pl.when(i >= 1)
    def _():
        prev = 1 - slot
        pltpu.make_async_copy(ys_hbm.at[pl.ds(0, tm), :], ybuf.at[prev], sem.at[prev]).wait()
        x2 = x_ref[...] + ybuf[prev]
        normed = _rms(x2, gnext_ref[...])
        if emit_x:
            outs[0][...] = x2
            outs[1][...] = normed
        else:
            outs[0][...] = normed


def _combine_call(dest, x2d, ys, gnext, emit_x):
    t = x2d.shape[0]
    tm = COMBINE_TILE
    n_tiles = t // tm
    dest = dest.reshape(n_tiles, 1, tm)
    n_out = 2 if emit_x else 1
    done = lambda i: (jnp.maximum(i - 1, 0), 0)
    return pl.pallas_call(
        functools.partial(_combine_kernel, emit_x),
        grid=(n_tiles + 1,),
        in_specs=[
            pl.BlockSpec((1, 1, tm), lambda i: (jnp.minimum(i, n_tiles - 1), 0, 0),
                         memory_space=pltpu.SMEM),
            pl.BlockSpec((tm, D_MODEL), done), _const_spec((1, D_MODEL)),
            pl.BlockSpec(memory_space=pl.ANY),
        ],
        out_specs=[pl.BlockSpec((tm, D_MODEL), done)] * n_out,
        out_shape=[jax.ShapeDtypeStruct((t, D_MODEL), F32)] * n_out,
        scratch_shapes=[pltpu.VMEM((2, tm, D_MODEL), F32), pltpu.SemaphoreType.DMA((2,))],
        compiler_params=_cparams(("arbitrary",)),
        name="moe_combine",
    )(dest, x2d, gnext, ys)


def _s5_state_in(u_ref_val, wre_ref, wim_ref, store_re, store_im, pair_w):
    for m in range(S5_GB // 2):
        up = u_ref_val(m * pair_w, pair_w)
        store_re(m, _dot(up, wre_ref[m]))
        store_im(m, _dot(up, wim_ref[m]))


def _s5_outputs(u_ref_val, hre, him, m_ref, gre_ref, gim_ref, y_store, pair_w):
    gw = pair_w // 2
    for m in range(S5_GB // 2):
        hr = hre(m).astype(BF16)
        hi = him(m).astype(BF16)
        y = _dot(hr, gre_ref[m]) + _dot(hi, gim_ref[m])
        y0 = y[:, :gw] + _dot(u_ref_val(m * pair_w, gw), m_ref[2 * m])
        y1 = y[:, gw:] + _dot(u_ref_val(m * pair_w + gw, gw), m_ref[2 * m + 1])
        y_store(m * pair_w, gw, y0)
        y_store(m * pair_w + gw, gw, y1)


def _s5_flatten(load_rows, q, n, ut, uflat):
    gc = SSM_GROUP_CH
    qc = q * gc
    for s in range(q):
        ut[:, s * gc:(s + 1) * gc, :] = load_rows(s).astype(BF16).T.reshape(S5_GB, gc, n)
    for g in range(S5_GB):
        uflat[:, g * qc:(g + 1) * qc] = ut[g].T


def _s5_unflatten(yflat, yt, store_rows, q, n):
    gc = SSM_GROUP_CH
    qc = q * gc
    for g in range(S5_GB):
        yt[g] = yflat[:, g * qc:(g + 1) * qc].T
    for t in range(q):
        store_rows(t, yt[:, t * gc:(t + 1) * gc, :].reshape(S5_GB * gc, n).T)


def _s5_prompt_kernel(u_ref, wre_ref, wim_ref, m_ref, gre_ref, gim_ref, aqr_ref, aqi_ref,
                      y_ref, her_ref, hei_ref, ut, uflat, sre, sim, hre, him, yflat, yt):
    pair_w = 2 * S5_CHUNK * SSM_GROUP_CH
    n_chunks = u_ref.shape[1] // S5_CHUNK
    n_pairs = S5_GB // 2
    seq_rows = lambda b: pl.ds(b, n_chunks, stride=S5_SEQS)
    for b in range(S5_SEQS):
        _s5_flatten(lambda s: u_ref[b, pl.ds(s, n_chunks, stride=S5_CHUNK), :],
                    S5_CHUNK, n_chunks, ut, uflat.at[b])
        u_val = lambda off, w: uflat[b, :, off:off + w]

        def store_re(m, val):
            sre[m, seq_rows(b), :] = val

        def store_im(m, val):
            sim[m, seq_rows(b), :] = val

        _s5_state_in(u_val, wre_ref, wim_ref, store_re, store_im, pair_w)
    ar = [aqr_ref[:, m * LANES:(m + 1) * LANES] for m in range(n_pairs)]
    ai = [aqi_ref[:, m * LANES:(m + 1) * LANES] for m in range(n_pairs)]

    def step(n, carry):
        rows = pl.ds(pl.multiple_of(n * S5_SEQS, S5_SEQS), S5_SEQS)
        out = []
        for m in range(n_pairs):
            hr, hi = carry[2 * m], carry[2 * m + 1]
            hre[m, rows, :] = hr
            him[m, rows, :] = hi
            out.append(ar[m] * hr - ai[m] * hi + sre[m, rows, :])
            out.append(ar[m] * hi + ai[m] * hr + sim[m, rows, :])
        return tuple(out)

    zero = jnp.zeros((S5_SEQS, LANES), F32)
    last = lax.fori_loop(0, n_chunks, step, (zero,) * (2 * n_pairs))
    for b in range(S5_SEQS):
        for m in range(n_pairs):
            her_ref[b, :, m * LANES:(m + 1) * LANES] = last[2 * m][b:b + 1]
            hei_ref[b, :, m * LANES:(m + 1) * LANES] = last[2 * m + 1][b:b + 1]

    def y_store(off, w, val):
        yflat[:, off:off + w] = val

    for b in range(S5_SEQS):
        u_val = lambda off, w: uflat[b, :, off:off + w]
        _s5_outputs(u_val, lambda m: hre[m, seq_rows(b), :], lambda m: him[m, seq_rows(b), :],
                    m_ref, gre_ref, gim_ref, y_store, pair_w)

        def store_rows(t, val):
            y_ref[b, pl.ds(t, n_chunks, stride=S5_CHUNK), :] = val

        _s5_unflatten(yflat, yt, store_rows, S5_CHUNK, n_chunks)


def _s5_sample_kernel(n_new, u_ref, h0r_ref, h0i_ref, wre_ref, wim_ref, m_ref, gre_ref, gim_ref,
                      aqr_ref, aqi_ref, y_ref, hnr_ref, hni_ref, ut, uflat, sre, sim, yflat, yt):
    pair_w = 2 * n_new * SSM_GROUP_CH
    seqs = h0r_ref.shape[0]
    _s5_flatten(lambda s: u_ref[pl.ds(s, seqs, stride=n_new), :], n_new, seqs, ut, uflat)
    u_val = lambda off, w: uflat[:, off:off + w]

    def store_re(m, val):
        sre[:, m * LANES:(m + 1) * LANES] = val

    def store_im(m, val):
        sim[:, m * LANES:(m + 1) * LANES] = val

    _s5_state_in(u_val, wre_ref, wim_ref, store_re, store_im, pair_w)
    ar = aqr_ref[...]
    ai = aqi_ref[...]
    h0r = h0r_ref[...]
    h0i = h0i_ref[...]
    hnr_ref[...] = ar * h0r - ai * h0i + sre[...]
    hni_ref[...] = ar * h0i + ai * h0r + sim[...]

    def y_store(off, w, val):
        yflat[:, off:off + w] = val

    _s5_outputs(u_val, lambda m: h0r_ref[:, m * LANES:(m + 1) * LANES],
                lambda m: h0i_ref[:, m * LANES:(m + 1) * LANES],
                m_ref, gre_ref, gim_ref, y_store, pair_w)

    def store_rows(t, val):
        y_ref[pl.ds(t, seqs, stride=n_new), :] = val

    _s5_unflatten(yflat, yt, store_rows, n_new, seqs)


def _s5_weight_specs(q, idx):
    qc = q * SSM_GROUP_CH
    np_ = S5_GB // 2
    st = S5_GB * SSM_STATE
    return [
        pl.BlockSpec((np_, 2 * qc, LANES), lambda *a: (idx(*a), 0, 0)),
        pl.BlockSpec((np_, 2 * qc, LANES), lambda *a: (idx(*a), 0, 0)),
        pl.BlockSpec((S5_GB, qc, qc), lambda *a: (idx(*a), 0, 0)),
        pl.BlockSpec((np_, LANES, 2 * qc), lambda *a: (idx(*a), 0, 0)),
        pl.BlockSpec((np_, LANES, 2 * qc), lambda *a: (idx(*a), 0, 0)),
        pl.BlockSpec((1, st), lambda *a: (0, idx(*a))),
        pl.BlockSpec((1, st), lambda *a: (0, idx(*a))),
    ]


def _s5_prompt_call(u, w):
    b, seq, _ = u.shape
    n_chunks = seq // S5_CHUNK
    gbl = S5_GB * SSM_GROUP_CH
    qc = S5_CHUNK * SSM_GROUP_CH
    st = S5_GB * SSM_STATE
    n_gb = SSM_GROUPS // S5_GB
    gb_of = lambda g, i: g
    tok = pl.BlockSpec((S5_SEQS, seq, gbl), lambda g, i: (i, 0, g))
    slab = pltpu.VMEM((S5_GB // 2, n_chunks * S5_SEQS, LANES), F32)
    return pl.pallas_call(
        _s5_prompt_kernel,
        grid=(n_gb, b // S5_SEQS),
        in_specs=[tok] + _s5_weight_specs(S5_CHUNK, gb_of),
        out_specs=[
            tok,
            pl.BlockSpec((S5_SEQS, 1, st), lambda g, i: (i, 0, g)),
            pl.BlockSpec((S5_SEQS, 1, st), lambda g, i: (i, 0, g)),
        ],
        out_shape=[
            jax.ShapeDtypeStruct((b, seq, D_MODEL), F32),
            jax.ShapeDtypeStruct((b, 1, SSM_GROUPS * SSM_STATE), F32),
            jax.ShapeDtypeStruct((b, 1, SSM_GROUPS * SSM_STATE), F32),
        ],
        scratch_shapes=[
            pltpu.VMEM((S5_GB, qc, n_chunks), BF16),
            pltpu.VMEM((S5_SEQS, n_chunks, S5_GB * qc), BF16),
            slab, slab, slab, slab,
            pltpu.VMEM((n_chunks, S5_GB * qc), F32),
            pltpu.VMEM((S5_GB, qc, n_chunks), F32),
        ],
        compiler_params=_cparams(("parallel", "parallel")),
        name="s5_prompt",
    )(u, *w)


def _s5_sample_call(u2d, h0r, h0i, w, n_new):
    t = u2d.shape[0]
    b = t // n_new
    gbl = S5_GB * SSM_GROUP_CH
    qc = n_new * SSM_GROUP_CH
    st = S5_GB * SSM_STATE
    n_gb = SSM_GROUPS // S5_GB
    gb_of = lambda g: g
    state = pl.BlockSpec((b, st), lambda g: (0, g))
    tok = pl.BlockSpec((t, gbl), lambda g: (0, g))
    return pl.pallas_call(
        functools.partial(_s5_sample_kernel, n_new),
        grid=(n_gb,),
        in_specs=[tok, state, state] + _s5_weight_specs(n_new, gb_of),
        out_specs=[tok, state, state],
        out_shape=[
            jax.ShapeDtypeStruct((t, D_MODEL), F32),
            jax.ShapeDtypeStruct((b, SSM_GROUPS * SSM_STATE), F32),
            jax.ShapeDtypeStruct((b, SSM_GROUPS * SSM_STATE), F32),
        ],
        scratch_shapes=[
            pltpu.VMEM((S5_GB, qc, b), BF16),
            pltpu.VMEM((b, S5_GB * qc), BF16),
            pltpu.VMEM((b, st), F32), pltpu.VMEM((b, st), F32),
            pltpu.VMEM((b, S5_GB * qc), F32),
            pltpu.VMEM((S5_GB, qc, b), F32),
        ],
        compiler_params=_cparams(("parallel",)),
        name="s5_sample",
    )(u2d, h0r, h0i, *w)


def _s5_discretize(a_re, a_im, log_dt, b_re, b_im):
    delta = jnp.exp(log_dt.astype(F32))[:, None]
    lr, li = a_re.astype(F32), a_im.astype(F32)
    mag = jnp.exp(delta * lr)
    abar_r = mag * jnp.cos(delta * li)
    abar_i = mag * jnp.sin(delta * li)
    nr, ni = abar_r - 1.0, abar_i
    den = lr * lr + li * li
    coef_r = ((nr * lr + ni * li) / den)[..., None]
    coef_i = ((ni * lr - nr * li) / den)[..., None]
    br, bi = b_re.astype(F32), b_im.astype(F32)
    return delta * lr, delta * li, coef_r * br - coef_i * bi, coef_r * bi + coef_i * br


def _split3(x):
    x1 = x.astype(BF16)
    r1 = x - x1.astype(F32)
    x2 = r1.astype(BF16)
    x3 = (r1 - x2.astype(F32)).astype(BF16)
    return x1, x2, x3


def _s5_prep_kernel(qs, bbr_ref, bbi_ref, ctr_ref, cti_ref, ppr_ref, ppi_ref, pnr_ref, pni_ref,
                    *out_refs):
    c = SSM_GROUP_CH
    qc = max(qs) * c
    p2 = 2 * SSM_STATE
    col = lax.broadcasted_iota(I32, (c, qc), 1)
    row = lax.broadcasted_iota(I32, (c, qc), 0)
    by_ch = (col % c == row).astype(BF16)
    by_pos = (col // c == row).astype(BF16)

    def spread(x, sel):
        return sum(_dot(part, sel) for part in _split3(x))

    def dot_f32(a, b):
        a1, a2, _ = _split3(a)
        b1, b2, _ = _split3(b)
        return _dot(a1, b1) + _dot(a1, b2) + _dot(a2, b1)

    bc_r, bc_i = spread(bbr_ref[0], by_ch), spread(bbi_ref[0], by_ch)
    cc_r, cc_i = spread(ctr_ref[0], by_ch), spread(cti_ref[0], by_ch)
    pp_r, pp_i = spread(ppr_ref[0], by_pos), spread(ppi_ref[0], by_pos)
    pn_r, pn_i = spread(pnr_ref[0], by_pos), spread(pni_ref[0], by_pos)
    r_r = cc_r * pp_r - cc_i * pp_i
    r_i = cc_r * pp_i + cc_i * pp_r
    l_r = bc_r * pn_r - bc_i * pn_i
    l_i = bc_r * pn_i + bc_i * pn_r
    lt_r, lt_i = l_r.T, l_i.T
    grp_col = lax.broadcasted_iota(I32, (qc, p2), 1) // SSM_STATE
    causal =(lax.broadcasted_iota(I32, (qc, qc), 1) // c) >= (lax.broadcasted_iota(I32, (qc, qc), 0) // c)
    kers = [jnp.where(causal, dot_f32(jnp.where(grp_col == gl, lt_r, 0.0), r_r)
                      - dot_f32(jnp.where(grp_col == gl, lt_i, 0.0), r_i), 0.0) for gl in range(2)]
    for n, q in enumerate(qs):
        wre_ref, wim_ref, m_ref, gre_ref, gim_ref = out_refs[5 * n:5 * n + 5]
        w = q * c
        aq_r = ppr_ref[0][:, q - 1:q]
        aq_i = ppi_ref[0][:, q - 1:q]
        w_r = (l_r[:, :w] * aq_r - l_i[:, :w] * aq_i).T
        w_i = (l_r[:, :w] * aq_i + l_i[:, :w] * aq_r).T
        g_re, g_im = [], []
        col_grp = lax.broadcasted_iota(I32, (w, p2), 1) // SSM_STATE
        row_grp = lax.broadcasted_iota(I32, (p2, w), 0) // SSM_STATE
        for gl in range(2):
            wre_ref[0, gl * w:(gl + 1) * w, :] = jnp.where(col_grp == gl, w_r, 0.0).astype(BF16)
            wim_ref[0, gl * w:(gl + 1) * w, :] = jnp.where(col_grp == gl, w_i, 0.0).astype(BF16)
            m_ref[gl] = kers[gl][:w, :w].astype(BF16)
            g_re.append(jnp.where(row_grp == gl, r_r[:, :w], 0.0))
            g_im.append(jnp.where(row_grp == gl, -r_i[:, :w], 0.0))
        gre_ref[0] = jnp.concatenate(g_re, axis=1).astype(BF16)
        gim_ref[0] = jnp.concatenate(g_im, axis=1).astype(BF16)


def _s5_chunk_weights(tables, qs):
    n_pairs = SSM_GROUPS // 2
    p2 = 2 * SSM_STATE
    tab = pl.BlockSpec((1, p2, SSM_GROUP_CH), lambda m: (m, 0, 0))
    out_specs, out_shape = [], []
    for q in qs:
        qc = q * SSM_GROUP_CH
        out_specs += [pl.BlockSpec((1, 2 * qc, p2), lambda m: (m, 0, 0)),
                      pl.BlockSpec((1, 2 * qc, p2), lambda m: (m, 0, 0)),
                      pl.BlockSpec((2, qc, qc), lambda m: (m, 0, 0)),
                      pl.BlockSpec((1, p2, 2 * qc), lambda m: (m, 0, 0)),
                      pl.BlockSpec((1, p2, 2 * qc), lambda m: (m, 0, 0))]
        out_shape += [jax.ShapeDtypeStruct((n_pairs, 2 * qc, p2), BF16),
                      jax.ShapeDtypeStruct((n_pairs, 2 * qc, p2), BF16),
                      jax.ShapeDtypeStruct((SSM_GROUPS, qc, qc), BF16),
                      jax.ShapeDtypeStruct((n_pairs, p2, 2 * qc), BF16),
                      jax.ShapeDtypeStruct((n_pairs, p2, 2 * qc), BF16)]
    outs = pl.pallas_call(
        functools.partial(_s5_prep_kernel, tuple(qs)),
        grid=(n_pairs,),
        in_specs=[tab] * 8,
        out_specs=out_specs, out_shape=out_shape,
        compiler_params=_cparams(("parallel",)),
        name="s5_operators",
    )(*tables)
    return [tuple(outs[5 * n:5 * n + 5]) for n in range(len(qs))]


def _s5_tables(log_mag, phase, bbar_r, bbar_i, c_re, c_im, q_max):
    k = jnp.arange(1, q_max + 1, dtype=F32)[None, None, :]
    mag = jnp.exp(log_mag[:, :, None] * k)
    cos, sin = jnp.cos(phase[:, :, None] * k), jnp.sin(phase[:, :, None] * k)
    tabs = (bbar_r, bbar_i, c_re.astype(F32).transpose(0, 2, 1), c_im.astype(F32).transpose(0, 2, 1),
            mag * cos, mag * sin, cos / mag, -sin / mag)
    return tuple(t.reshape(SSM_GROUPS // 2, 2 * SSM_STATE, SSM_GROUP_CH) for t in tabs)


def _rope_tables(pos):
    half = HEAD_DIM // 2
    inv = 1.0 / (ROPE_THETA ** (jnp.arange(half, dtype=F32) * (2.0 / HEAD_DIM)))
    ang = pos.astype(F32)[:, None] * inv[None, :]
    cos, sin = jnp.cos(ang), jnp.sin(ang)
    return jnp.tile(cos, (1, 4)), jnp.concatenate([-sin, sin, -sin, sin], axis=1)


def _router_weights(gain, w_rg, b_rg, w_re, b_re):
    n_layers = gain.shape[0]
    pad = LANES - N_EXPERTS - N_EXPERT_GROUPS
    w = jnp.concatenate([w_re, w_rg, jnp.zeros((n_layers, D_MODEL, pad), F32)], axis=2)
    b = jnp.concatenate([b_re, b_rg, jnp.zeros((n_layers, pad), F32)], axis=1)
    wh = w.astype(BF16)
    wl = (w - wh.astype(F32)).astype(BF16)
    return [(gain[l].reshape(1, -1).astype(F32), wh[l], wl[l], b[l].reshape(1, LANES))
            for l in range(n_layers)]


def kernel(x_prompt, x_sample, cache_k, cache_v, state_ssm_re, state_ssm_im, norm_mix, norm_ffn, norm_final, attn_w_qkv, attn_b_qkv, attn_w_o, attn_b_o, attn_sinks, ssm_a_re, ssm_a_im, ssm_log_dt, ssm_b_re, ssm_b_im, ssm_c_re, ssm_c_im, ssm_d, ssm_w_glu_a, ssm_w_glu_b, moe_w_router_group, moe_b_router_group, moe_w_router_expert, moe_b_router_expert, moe_w_gate, moe_w_up, moe_w_down):
    bsz, seq, _ = x_prompt.shape
    dbs, n_new, _ = x_sample.shape
    rows = cache_k.shape[2]
    tp, ts = bsz * seq, dbs * n_new
    cap = _moe_cap(tp + ts)
    xp = x_prompt.reshape(tp, D_MODEL)
    xs = x_sample.reshape(ts, D_MODEL)

    row1 = lambda v: v.reshape(1, -1).astype(F32)
    routers = _router_weights(norm_ffn, moe_w_router_group, moe_b_router_group,
                              moe_w_router_expert, moe_b_router_expert)
    triu = jnp.triu(jnp.ones((TOKEN_TILE, TOKEN_TILE), F32), 1).astype(BF16)
    cnt0 = (jnp.zeros((LANES, LANES), F32), jnp.zeros((8, LANES), F32))

    slots = jnp.asarray(HEAD_SLOTS, dtype=I32)

    def q_slots(w):
        lead = w.shape[:-1]
        qh = w[..., :Q_DIM].reshape(*lead, N_HEADS, HEAD_DIM)[..., slots, :].reshape(*lead, Q_DIM)
        return jnp.concatenate([qh, w[..., Q_DIM:]], axis=-1)

    wqkv = q_slots(attn_w_qkv[0]).astype(BF16)
    bqkv = row1(q_slots(attn_b_qkv[0]))
    wo = attn_w_o[0].reshape(N_HEADS, HEAD_DIM, D_MODEL)[slots].reshape(Q_DIM, D_MODEL).astype(BF16)
    bo = row1(attn_b_o[0])
    sinks = attn_sinks[0].astype(F32)[slots]
    g_mix0, g_mix1 = row1(norm_mix[0]), row1(norm_mix[1])
    pos = jnp.concatenate([jnp.arange(seq, dtype=I32),
                           jnp.tile(PAST_LEN + jnp.arange(n_new, dtype=I32), dbs)])
    cos_t, sin_t = _rope_tables(pos)

    qp, kp, vp = _qkv_call(xp, g_mix0, wqkv, bqkv, cos_t, sin_t, (0, seq), BF16)
    qs, ks, vs = _qkv_call(xs, g_mix0, wqkv, bqkv, cos_t, sin_t, (seq, ts), F32)
    op = _attn_prompt_call(jnp.repeat(sinks, BLOCK).reshape(1, -1), qp.reshape(bsz, seq, Q_DIM),
                           kp.reshape(bsz, seq, KV_DIM),
                           vp.reshape(bsz, seq, KV_DIM))
    os_, nks, nvs = _attn_sample_call(jnp.repeat(sinks, n_new).reshape(1, -1), qs, ks, vs,
                                      cache_k[0].reshape(dbs, rows, KV_DIM),
                                      cache_v[0].reshape(dbs, rows, KV_DIM), n_new, 8)
    xp1, dest_p, cnt, rows_x, _ = _oproj_call(op.reshape(tp, Q_DIM), xp, wo, bo, routers[0], triu,
                                              cnt0, None, False, cap)
    xs1, dest_s, cnt, rows_x, steps = _oproj_call(os_, xs, wo, bo, routers[0], triu, cnt, rows_x,
                                                  True, cap)
    rows_y = _expert_call(steps, rows_x, moe_w_gate, moe_w_up, moe_w_down, 0, cap)
    xp2, up = _combine_call(dest_p, xp1, rows_y, g_mix1, True)
    xs2, us = _combine_call(dest_s, xs1, rows_y, g_mix1, True)

    disc = _s5_discretize(ssm_a_re[0], ssm_a_im[0], ssm_log_dt[0], ssm_b_re[0], ssm_b_im[0])
    tables = _s5_tables(*disc, ssm_c_re[0], ssm_c_im[0], S5_CHUNK)
    a_pow = lambda q: [tables[i][:, :, q - 1].reshape(1, -1) for i in (4, 5)]
    ops_p, ops_s = _s5_chunk_weights(tables, (S5_CHUNK, n_new))
    w_p = (*ops_p, *a_pow(S5_CHUNK))
    w_s = (*ops_s, *a_pow(n_new))
    y_p, hpr, hpi = _s5_prompt_call(up.reshape(bsz, seq, D_MODEL), w_p)
    y_p = y_p.reshape(tp, D_MODEL)
    h0r = state_ssm_re[0].reshape(dbs, -1).astype(F32)
    h0i = state_ssm_im[0].reshape(dbs, -1).astype(F32)
    y_s, hsr, hsi = _s5_sample_call(us, h0r, h0i, w_s, n_new)

    wa, wb = ssm_w_glu_a[0].astype(BF16), ssm_w_glu_b[0].astype(BF16)
    d_row = row1(ssm_d[0])
    xp3, dest_p, cnt, rows_x, _ = _glu_call(y_p, xp2, g_mix1, d_row, wa, wb, routers[1], triu,
                                            cnt0, None, False, cap)
    xs3, dest_s, cnt, rows_x, steps = _glu_call(y_s, xs2, g_mix1, d_row, wa, wb, routers[1], triu,
                                                cnt, rows_x, True, cap)
    rows_y = _expert_call(steps, rows_x, moe_w_gate, moe_w_up, moe_w_down, 1, cap)
    g_fin = row1(norm_final)
    (yp,) = _combine_call(dest_p, xp3, rows_y, g_fin, False)
    (ys,) = _combine_call(dest_s, xs3, rows_y, g_fin, False)

    kv5 = lambda a, n: a.reshape(1, n, rows, KV_HEADS, HEAD_DIM)
    st4 = lambda a, n: a.reshape(1, n, SSM_GROUPS, SSM_STATE)
    k_last = kp.reshape(bsz, seq, KV_DIM)[:, seq - WINDOW:]
    v_last = vp.reshape(bsz, seq, KV_DIM)[:, seq - WINDOW:]
    return (yp.reshape(bsz, seq, D_MODEL), ys.reshape(dbs, n_new, D_MODEL),
            k_last.reshape(1, bsz, WINDOW, KV_HEADS, HEAD_DIM), kv5(nks, dbs),
            v_last.reshape(1, bsz, WINDOW, KV_HEADS, HEAD_DIM), kv5(nvs, dbs),
            st4(hpr, bsz), st4(hsr, dbs), st4(hpi, bsz), st4(hsi, dbs))
```

```python
import collections
import functools
import math

import jax
import jax.numpy as jnp
from jax import lax
from jax.experimental import pallas as pl
from jax.experimental.pallas import tpu as pltpu

F32 = jnp.float32
BF16 = jnp.bfloat16
I32 = jnp.int32

D_MODEL = 1024
N_HEADS = 16
KV_HEADS = 4
HEAD_DIM = 64
Q_DIM = N_HEADS * HEAD_DIM
KV_DIM = KV_HEADS * HEAD_DIM
QKV_DIM = Q_DIM + 2 * KV_DIM
WINDOW = 128
BLOCK = 128
ROPE_THETA = 10000.0
PAST_LEN = 16384
SSM_GROUP_CH = 16
SSM_GROUPS = D_MODEL // SSM_GROUP_CH
SSM_STATE = 64
N_EXPERT_GROUPS = 4
EXPERTS_PER_GROUP = 8
N_EXPERTS = N_EXPERT_GROUPS * EXPERTS_PER_GROUP
EXPERT_FF = 128
NORM_EPS = 1e-5

LANES = 128
VMEM_LIMIT = 56 * 1024 * 1024
S5_CHUNK = 16
S5_GB = 8
S5_SEQS = 4
TOKEN_TILE = 1024
MOE_TILE = 512
ROUTE_PARTS = 2
COMBINE_TILE = 512
PAY_X = D_MODEL
PAY_W = PAY_X + LANES


def _cparams(sem):
    return pltpu.CompilerParams(dimension_semantics=sem, vmem_limit_bytes=VMEM_LIMIT)


def _rms(x, g):
    return x * lax.rsqrt(jnp.mean(x * x, axis=-1, keepdims=True) + NORM_EPS) * g


def _dot(a, b):
    return jnp.dot(a, b, preferred_element_type=F32)


def _dot_nt(a, b):
    return lax.dot_general(a, b, (((1,), (1,)), ((), ())), preferred_element_type=F32)


def _sigmoid(x):
    return 1.0 / (1.0 + jnp.exp(-x))


def _qkv_kernel(x_ref, g_ref, w_ref, b_ref, cos_ref, sin_ref, q_ref, k_ref, v_ref):
    xn = _rms(x_ref[...], g_ref[...])
    qkv = _dot(xn.astype(BF16), w_ref[...]) + b_ref[...]
    cos = cos_ref[...]
    sin = sin_ref[...]
    lane = lax.broadcasted_iota(I32, cos.shape, 1)
    first_half = (lane % HEAD_DIM) < (HEAD_DIM // 2)
    n_rot = (Q_DIM + KV_DIM) // LANES
    for c in range(n_rot):
        blk = qkv[:, c * LANES:(c + 1) * LANES]
        partner = jnp.where(first_half,
                            pltpu.roll(blk, LANES - HEAD_DIM // 2, 1),
                            pltpu.roll(blk, HEAD_DIM // 2, 1))
        rot = blk * cos + partner * sin
        if c < Q_DIM // LANES:
            q_ref[:, c * LANES:(c + 1) * LANES] = (rot * (HEAD_DIM ** -0.5)).astype(q_ref.dtype)
        else:
            k_ref[:, c * LANES - Q_DIM:(c + 1) * LANES - Q_DIM] = rot
    v_ref[...] = qkv[:, Q_DIM + KV_DIM:]


def _qkv_call(x2d, gain, w_bf, bias, cos_t, sin_t, pos_rows, q_dtype):
    t = x2d.shape[0]
    tm = TOKEN_TILE
    first, n_pos = pos_rows[0] // tm, pos_rows[1] // tm
    return pl.pallas_call(
        _qkv_kernel,
        grid=(t // tm,),
        in_specs=[
            pl.BlockSpec((tm, D_MODEL), lambda i: (i, 0)),
            pl.BlockSpec((1, D_MODEL), lambda i: (0, 0)),
            pl.BlockSpec((D_MODEL, QKV_DIM), lambda i: (0, 0)),
            pl.BlockSpec((1, QKV_DIM), lambda i: (0, 0)),
            pl.BlockSpec((tm, LANES), lambda i: (first + i % n_pos, 0)),
            pl.BlockSpec((tm, LANES), lambda i: (first + i % n_pos, 0)),
        ],
        out_specs=[
            pl.BlockSpec((tm, Q_DIM), lambda i: (i, 0)),
            pl.BlockSpec((tm, KV_DIM), lambda i: (i, 0)),
            pl.BlockSpec((tm, KV_DIM), lambda i: (i, 0)),
        ],
        out_shape=[
            jax.ShapeDtypeStruct((t, Q_DIM), q_dtype),
            jax.ShapeDtypeStruct((t, KV_DIM), F32),
            jax.ShapeDtypeStruct((t, KV_DIM), F32),
        ],
        compiler_params=_cparams(("parallel",)),
        name="qkv_rope",
    )(x2d, gain, w_bf, bias, cos_t, sin_t)


HEAD_SLOTS = tuple(8 * c + 4 * half + i for c in range(2) for i in range(4) for half in range(2))
LOG2E = math.log2(math.e)


def _attn_prompt_kernel(sink_ref, q_ref, kp_ref, kc_ref, vp_ref, vc_ref, o_ref):
    n = pl.program_id(1)
    k2 = jnp.concatenate([kp_ref[0], kc_ref[0]], axis=0).astype(BF16)
    v2 = jnp.concatenate([vp_ref[0], vc_ref[0]], axis=0).astype(BF16)
    sub = BLOCK // 2
    win = WINDOW + sub
    cols = 4 * sub
    key = lax.broadcasted_iota(I32, (win, cols), 0)
    qry = lax.broadcasted_iota(I32, (win, cols), 1) % sub
    band = (key >= qry) & (key <= qry + WINDOW)
    masks = [band & ((n > 0) | (key + h * sub >= BLOCK)) for h in range(2)]
    low = lax.broadcasted_iota(I32, (sub, LANES), 1) < HEAD_DIM
    keep_low = low.astype(BF16)
    keep_high = 1 - keep_low
    no_keys = jnp.zeros((2 * BLOCK - win, cols), F32)
    tasks = [(quad, h) for quad in range(N_HEADS // 4) for h in range(2)]
    chunk = lambda quad: slice((quad // 2) * LANES, (quad // 2 + 1) * LANES)

    def scores(quad, h):
        blocks = []
        for pr in (2 * quad, 2 * quad + 1):
            qblk = q_ref[0, h * sub:(h + 1) * sub, pr * LANES:(pr + 1) * LANES]
            blocks += [qblk * keep_low, qblk * keep_high]
        qz = jnp.concatenate(blocks, axis=0)
        s = _dot_nt(k2[h * sub:h * sub + win, chunk(quad)], qz) * LOG2E
        return jnp.where(masks[h], s, -jnp.inf)

    def weights(quad, h, s):
        sink = sink_ref[:, quad * cols:(quad + 1) * cols] * LOG2E
        m = jnp.maximum(jnp.max(s, axis=0, keepdims=True), sink)
        p = jnp.exp2(s - m)
        denom = jnp.sum(p, axis=0, keepdims=True) + jnp.exp2(sink - m)
        w = p * (1.0 / denom)
        w = jnp.concatenate([w, no_keys] if h == 0 else [no_keys, w], axis=0)
        return w.T.astype(BF16)

    def finish(quad, h, w):
        oz = _dot(w, v2[:, chunk(quad)])
        for j, pr in enumerate((2 * quad, 2 * quad + 1)):
            o_ref[0, h * sub:(h + 1) * sub, pr * LANES:(pr + 1) * LANES] = jnp.where(
                low, oz[2 * j * sub:(2 * j + 1) * sub], oz[(2 * j + 1) * sub:(2 * j + 2) * sub]
            ).astype(o_ref.dtype)

    all_s = [scores(*t) for t in tasks]
    all_w = [weights(*t, s) for t, s in zip(tasks, all_s)]
    for t, w in zip(tasks, all_w):
        finish(*t, w)


def _attn_prompt_call(sinks, q, k, v):
    b, l, _ = q.shape
    nb = l // BLOCK
    kv_cur = pl.BlockSpec((1, BLOCK, KV_DIM), lambda i, n: (i, n, 0))
    kv_prev = pl.BlockSpec((1, BLOCK, KV_DIM), lambda i, n: (i, jnp.maximum(n - 1, 0), 0))
    return pl.pallas_call(
        _attn_prompt_kernel,
        grid=(b, nb),
        in_specs=[
            pl.BlockSpec((1, N_HEADS * BLOCK // 2), lambda i, n: (0, 0)),
            pl.BlockSpec((1, BLOCK, Q_DIM), lambda i, n: (i, n, 0)),
            kv_prev, kv_cur, kv_prev, kv_cur,
        ],
        out_specs=pl.BlockSpec((1, BLOCK, Q_DIM), lambda i, n: (i, n, 0)),
        out_shape=jax.ShapeDtypeStruct((b, l, Q_DIM), BF16),
        compiler_params=_cparams(("parallel", "parallel")),
        name="attn_prompt",
    )(sinks, q, k, k, v, v)


def _attn_sample_kernel(n_new, seqs, sink_ref, q_ref, kn_ref, vn_ref, ck_ref, cv_ref,
                        o_ref, nk_ref, nv_ref):
    rows = ck_ref.shape[1]
    keys = 2 * rows
    n_cols = N_HEADS * n_new
    key = lax.broadcasted_iota(I32, (keys, n_cols), 0)
    qry = lax.broadcasted_iota(I32, (keys, n_cols), 1) % n_new
    mask = ((key < rows) & (key >= qry)) | ((key >= rows) & (key - rows <= qry))
    low = lax.broadcasted_iota(I32, (n_new, LANES), 1) < HEAD_DIM
    sink = sink_ref[...]
    pad = jnp.zeros((rows - n_new, KV_DIM), F32)

    def body(sb, carry):
        r0 = pl.multiple_of(sb * n_new, n_new)
        k_new = kn_ref[pl.ds(r0, n_new), :]
        v_new = vn_ref[pl.ds(r0, n_new), :]
        k_all = jnp.concatenate([ck_ref[sb], k_new, pad], axis=0).astype(BF16)
        v_all = jnp.concatenate([cv_ref[sb], v_new, pad], axis=0).astype(BF16)
        blocks = []
        for slot in range(N_HEADS):
            pair = q_ref[pl.ds(r0, n_new), (slot // 2) * LANES:(slot // 2 + 1) * LANES]
            blk = jnp.where(low if slot % 2 == 0 else ~low, pair, 0.0)
            zero = jnp.zeros_like(blk)
            blocks.append(jnp.concatenate([blk, zero] if slot < N_HEADS // 2 else [zero, blk], axis=1))
        qz = jnp.concatenate(blocks, axis=0).astype(BF16)
        s = _dot_nt(k_all, qz)
        s = jnp.where(mask, s, -jnp.inf)
        m = jnp.maximum(jnp.max(s, axis=0, keepdims=True), sink)
        p = jnp.exp(s - m)
        denom = jnp.sum(p, axis=0, keepdims=True) + jnp.exp(sink - m)
        w = (p * (1.0 / denom)).T.astype(BF16)
        oz = _dot(w, v_all)
        for pr in range(N_HEADS // 2):
            lanes = slice((pr // (N_HEADS // 4)) * LANES, (pr // (N_HEADS // 4) + 1) * LANES)
            a = oz[2 * pr * n_new:(2 * pr + 1) * n_new, lanes]
            b = oz[(2 * pr + 1) * n_new:(2 * pr + 2) * n_new, lanes]
            o_ref[pl.ds(r0, n_new), pr * LANES:(pr + 1) * LANES] = jnp.where(low, a, b)
        nk_ref[sb, 0:rows - n_new, :] = ck_ref[sb, n_new:rows, :]
        nk_ref[sb, rows - n_new:rows, :] = k_new
        nv_ref[sb, 0:rows - n_new, :] = cv_ref[sb, n_new:rows, :]
        nv_ref[sb, rows - n_new:rows, :] = v_new
        return carry

    lax.fori_loop(0, seqs, body, 0)


def _attn_sample_call(sink_row, q, k_new, v_new, cache_k, cache_v, n_new, seqs):
    b, rows, _ = cache_k.shape
    tok = pl.BlockSpec((seqs * n_new, Q_DIM), lambda i: (i, 0))
    tok_kv = pl.BlockSpec((seqs * n_new, KV_DIM), lambda i: (i, 0))
    cache = pl.BlockSpec((seqs, rows, KV_DIM), lambda i: (i, 0, 0))
    return pl.pallas_call(
        functools.partial(_attn_sample_kernel, n_new, seqs),
        grid=(b // seqs,),
        in_specs=[pl.BlockSpec((1, N_HEADS * n_new), lambda i: (0, 0)), tok, tok_kv, tok_kv, cache, cache],
        out_specs=[tok, cache, cache],
        out_shape=[
            jax.ShapeDtypeStruct((b * n_new, Q_DIM), F32),
            jax.ShapeDtypeStruct((b, rows, KV_DIM), F32),
            jax.ShapeDtypeStruct((b, rows, KV_DIM), F32),
        ],
        compiler_params=_cparams(("parallel",)),
        name="attn_sample",
    )(sink_row, q, k_new, v_new, cache_k, cache_v)


def _moe_cap(t_total):
    return t_total + MOE_TILE


def _expert_steps(cap):
    return cap // MOE_TILE - 1 + N_EXPERT_GROUPS


def _write_step_table(maps_ref, counts, cap):
    per = cap // MOE_TILE
    shift = MOE_TILE.bit_length() - 1
    ends = []
    for c in counts:
        tiles = lax.shift_right_logical(c + (MOE_TILE - 1), shift)
        ends.append(tiles if not ends else ends[-1] + tiles)
    total = ends[-1]
    for j in range(maps_ref.shape[1]):
        jj = jnp.maximum(jnp.minimum(j, total - 1), 0)
        g = sum((jj >= e).astype(I32) for e in ends[:-1])
        start = sum(jnp.where(g > k, ends[k] - (ends[k - 1] if k else 0), 0) for k in range(len(ends) - 1))
        blk_in = g * per + jj - start
        valid = (total > j).astype(I32)
        maps_ref[0, j] = blk_in
        maps_ref[1, j] = jnp.where(valid == 1, blk_in, len(counts) * per)
        maps_ref[2, j] = g
        maps_ref[3, j] = valid


def _route_rows(x1, gn_ref, wrh_ref, wrl_ref, br_ref):
    tm = x1.shape[0]
    xn = _rms(x1, gn_ref[...])
    xh = xn.astype(BF16)
    xl = (xn - xh.astype(F32)).astype(BF16)
    logits = _dot(xh, wrh_ref[...]) + _dot(xl, wrh_ref[...]) + _dot(xh, wrl_ref[...]) + br_ref[...]
    lt = logits.T
    ge = N_EXPERTS // EXPERTS_PER_GROUP
    sub = lax.broadcasted_iota(I32, (EXPERTS_PER_GROUP, tm), 0).astype(F32)
    big = jnp.float32(LANES)
    neg = -jnp.inf
    gl = jnp.where(sub < N_EXPERT_GROUPS, lt[N_EXPERTS:N_EXPERTS + EXPERTS_PER_GROUP], neg)
    gmax = jnp.max(gl, axis=0, keepdims=True)
    g_val = 1.0 / jnp.sum(jnp.exp(gl - gmax), axis=0, keepdims=True)
    g_idx = jnp.min(jnp.where(gl == gmax, sub, big), axis=0, keepdims=True)
    el = lt[0:EXPERTS_PER_GROUP]
    for g in range(1, ge):
        el = jnp.where(g_idx == g, lt[g * EXPERTS_PER_GROUP:(g + 1) * EXPERTS_PER_GROUP], el)
    e1 = jnp.max(el, axis=0, keepdims=True)
    i1 = jnp.min(jnp.where(el == e1, sub, big), axis=0, keepdims=True)
    el2 = jnp.where(sub == i1, neg, el)
    e2 = jnp.max(el2, axis=0, keepdims=True)
    i2 = jnp.min(jnp.where(el2 == e2, sub, big), axis=0, keepdims=True)
    t = jnp.exp(e2 - e1)
    w1 = 1.0 / (1.0 + t)
    w2 = t / (1.0 + t)
    wts_t = g_val * (jnp.where(sub == i1, w1, 0.0) + jnp.where(sub == i2, w2, 0.0))
    wts = jnp.concatenate([wts_t, jnp.zeros((LANES - EXPERTS_PER_GROUP, tm), F32)], axis=0).T
    return xn, wts, g_idx


def _route_begin(cntc_in, cntr_in, xs_hbm, pay, cntc, cntr, sem):
    i = pl.program_id(0)
    slot = i % 2

    @pl.when(i == 0)
    def _():
        cntc[...] = cntc_in[...]
        cntr[...] = cntr_in[...]

    @pl.when(i >= 2)
    def _():
        pltpu.make_async_copy(pay.at[slot], xs_hbm.at[pl.ds(0, pay.shape[1]), :], sem.at[slot]).wait()


def _route_scatter(final, cap, parts, triu_ref, dest_ref, cntc_out, cntr_out, xs_hbm, maps_ref,
                   pay, zrows, dest_v, dest_s, cntc, cntr, sem):
    i = pl.program_id(0)
    n = pl.num_programs(0)
    slot = i % 2
    tm = pay.shape[1]

    def wait_slot(s):
        pltpu.make_async_copy(pay.at[s], xs_hbm.at[pl.ds(0, tm), :], sem.at[s]).wait()

    r0 = 0
    for xn, wts, _ in parts:
        pay[slot, r0:r0 + xn.shape[0], :PAY_X] = xn
        pay[slot, r0:r0 + xn.shape[0], PAY_X:] = wts
        r0 += xn.shape[0]
    g_idx = jnp.concatenate([p[2] for p in parts], axis=1)

    grp_t = lax.broadcasted_iota(I32, (LANES, tm), 0).astype(F32)
    oht = (grp_t == g_idx).astype(F32)
    rank = _dot(oht.astype(BF16), triu_ref[...])
    grp = lax.broadcasted_iota(I32, (LANES, 1), 0).astype(F32)
    base = grp * float(cap) + cntc[:, 0:1]
    dest = jnp.sum(oht * (rank + base), axis=0, keepdims=True).astype(I32)
    cntc[...] = cntc[...] + jnp.sum(oht, axis=1, keepdims=True)
    dest_ref[0] = dest
    dest_v[...] = dest
    pltpu.sync_copy(dest_v, dest_s)

    for s in range(2):
        @pl.when(slot == s)
        def _():
            for r in range(tm):
                pltpu.make_async_copy(pay.at[s, pl.ds(r, 1), :],
                                      xs_hbm.at[pl.ds(dest_s[0, r], 1), :], sem.at[s]).start()

    @pl.when(i == n - 1)
    def _():
        cntc_out[...] = cntc[...]
        cntr[...] = cntc[...].T[0:8, :]
        cntr_out[...] = cntr[...]

        @pl.when(n >= 2)
        def _():
            wait_slot(1 - slot)

        wait_slot(slot)
        dest_v[:, 0:LANES] = cntr[0:1, :].astype(I32)
        pltpu.sync_copy(dest_v, dest_s)
        counts = [dest_s[0, g] for g in range(N_EXPERT_GROUPS)]
        _write_step_table(maps_ref, counts, cap)
        if final:
            zrows[...] = jnp.zeros_like(zrows)
            starts = []
            for g in range(N_EXPERT_GROUPS):
                c_g = counts[g]
                starts.append(pl.multiple_of(g * cap + lax.shift_left(lax.shift_right_logical(c_g + 7, 3), 3), 8))
                for k in range(7):
                    pltpu.make_async_copy(zrows.at[pl.ds(0, 1), :],
                                          xs_hbm.at[pl.ds(g * cap + c_g + k, 1), :], sem.at[1]).start()
            for _ in range(7 * N_EXPERT_GROUPS):
                pltpu.make_async_copy(zrows.at[pl.ds(0, 1), :], xs_hbm.at[pl.ds(0, 1), :], sem.at[1]).wait()
            for g in range(N_EXPERT_GROUPS):
                pltpu.make_async_copy(zrows, xs_hbm.at[pl.ds(starts[g], MOE_TILE), :], sem.at[0]).start()
            for g in range(N_EXPERT_GROUPS):
                pltpu.make_async_copy(zrows, xs_hbm.at[pl.ds(0, MOE_TILE), :], sem.at[0]).wait()


RouteRefs = collections.namedtuple(
    "RouteRefs", "gn wrh wrl br triu cntc_in cntr_in x1 dest cntc_out cntr_out xs maps "
                 "pay zrows dest_v dest_s cntc cntr sem")


def _route_refs(rest, has_prev):
    if has_prev:
        rest = rest[:7] + rest[8:]
    return RouteRefs(*rest)


def _route_tile(final, cap, r, x1_parts):
    parts = [_route_rows(x1, r.gn, r.wrh, r.wrl, r.br) for x1 in x1_parts]
    _route_scatter(final, cap, parts, r.triu, r.dest, r.cntc_out, r.cntr_out, r.xs, r.maps,
                   r.pay, r.zrows, r.dest_v, r.dest_s, r.cntc, r.cntr, r.sem)


def _oproj_kernel(has_prev, final, cap, *refs):
    (o_ref, x_ref, wo_ref, bo_ref) = refs[:4]
    r = _route_refs(refs[4:], has_prev)
    _route_begin(r.cntc_in, r.cntr_in, r.xs, r.pay, r.cntc, r.cntr, r.sem)
    rows = x_ref.shape[0] // ROUTE_PARTS
    x1_parts = []
    for h in range(ROUTE_PARTS):
        rs = slice(h * rows, (h + 1) * rows)
        x1 = x_ref[rs, :] + _dot(o_ref[rs, :].astype(BF16), wo_ref[...]) + bo_ref[...]
        r.x1[rs, :] = x1
        x1_parts.append(x1)
    _route_tile(final, cap, r, x1_parts)


def _gelu_tanh(x):
    return x * (0.5 * (1.0 + jnp.tanh(math.sqrt(2.0 / math.pi) * (x + 0.044715 * (x * x * x)))))


def _glu_kernel(has_prev, final, cap, *refs):
    (y_ref, x_ref, gm_ref, d_ref, wa_ref, wb_ref) = refs[:6]
    r = _route_refs(refs[6:], has_prev)
    _route_begin(r.cntc_in, r.cntr_in, r.xs, r.pay, r.cntc, r.cntr, r.sem)
    rows = x_ref.shape[0] // ROUTE_PARTS
    gated = []
    for h in range(ROUTE_PARTS):
        rs = slice(h * rows, (h + 1) * rows)
        x = x_ref[rs, :]
        z = _gelu_tanh(y_ref[rs, :] + d_ref[...] * _rms(x, gm_ref[...])).astype(BF16)
        gated.append((rs, x, _dot(z, wa_ref[...]), _dot(z, wb_ref[...])))
    x1_parts = []
    for rs, x, a, b in gated:
        x1 = x + a * _sigmoid(b)
        r.x1[rs, :] = x1
        x1_parts.append(x1)
    _route_tile(final, cap, r, x1_parts)


def _row_spec(tm, width):
    return pl.BlockSpec((tm, width), lambda i: (i, 0))


def _const_spec(shape):
    return pl.BlockSpec(shape, lambda i: (0,) * len(shape))


def _mixer_call(body, name, lead_args, lead_specs, x2d, router, triu, cnt, xs_prev, final, cap):
    t = x2d.shape[0]
    tm = TOKEN_TILE
    n_tiles = t // tm
    gn, wrh, wrl, br = router
    cntc, cntr = cnt
    has_prev = xs_prev is not None
    in_specs = lead_specs + [
        _const_spec((1, D_MODEL)), _const_spec((D_MODEL, LANES)), _const_spec((D_MODEL, LANES)),
        _const_spec((1, LANES)), _const_spec((tm, tm)), _const_spec((LANES, LANES)), _const_spec((8, LANES)),
    ]
    args = list(lead_args) + [gn, wrh, wrl, br, triu, cntc, cntr]
    aliases = {}
    if has_prev:
        in_specs.append(pl.BlockSpec(memory_space=pl.ANY))
        args.append(xs_prev)
        aliases = {len(args) - 1: 4}
    x1, dest, cntc2, cntr2, xs, maps = pl.pallas_call(
        functools.partial(body, has_prev, final, cap),
        grid=(n_tiles,),
        in_specs=in_specs,
        out_specs=[
            _row_spec(tm, D_MODEL),
            pl.BlockSpec((1, 1, tm), lambda i: (i, 0, 0)),
            _const_spec((LANES, LANES)), _const_spec((8, LANES)),
            pl.BlockSpec(memory_space=pl.ANY),
            pl.BlockSpec(memory_space=pltpu.SMEM),
        ],
        out_shape=[
            jax.ShapeDtypeStruct((t, D_MODEL), F32),
            jax.ShapeDtypeStruct((n_tiles, 1, tm), I32),
            jax.ShapeDtypeStruct((LANES, LANES), F32),
            jax.ShapeDtypeStruct((8, LANES), F32),
            jax.ShapeDtypeStruct((N_EXPERT_GROUPS * cap, PAY_W), F32),
            jax.ShapeDtypeStruct((4, _expert_steps(cap)), I32),
        ],
        scratch_shapes=[
            pltpu.VMEM((2, tm, PAY_W), F32),
            pltpu.VMEM((MOE_TILE, PAY_W), F32),
            pltpu.VMEM((1, tm), I32),
            pltpu.SMEM((1, tm), I32),
            pltpu.VMEM((LANES, LANES), F32),
            pltpu.VMEM((8, LANES), F32),
            pltpu.SemaphoreType.DMA((2,)),
        ],
        input_output_aliases=aliases,
        compiler_params=_cparams(("arbitrary",)),
        name=name,
    )(*args)
    return x1, dest, (cntc2, cntr2), xs, maps


def _oproj_call(o2d, x2d, wo_bf, bo, router, triu, cnt, xs_prev, final, cap):
    tm = TOKEN_TILE
    lead_specs = [_row_spec(tm, Q_DIM), _row_spec(tm, D_MODEL), _const_spec((Q_DIM, D_MODEL)),
                  _const_spec((1, D_MODEL))]
    return _mixer_call(_oproj_kernel, "oproj_route", [o2d, x2d, wo_bf, bo], lead_specs, x2d,
                       router, triu, cnt, xs_prev, final, cap)


def _glu_call(y2d, x2d, gm, d, wa_bf, wb_bf, router, triu, cnt, xs_prev, final, cap):
    tm = TOKEN_TILE
    lead_specs = [_row_spec(tm, D_MODEL), _row_spec(tm, D_MODEL), _const_spec((1, D_MODEL)),
                  _const_spec((1, D_MODEL)), _const_spec((D_MODEL, D_MODEL)),
                  _const_spec((D_MODEL, D_MODEL))]
    return _mixer_call(_glu_kernel, "glu_route", [y2d, x2d, gm, d, wa_bf, wb_bf], lead_specs, x2d,
                       router, triu, cnt, xs_prev, final, cap)


def _expert_kernel(maps_ref, xs_ref, wg_ref, wu_ref, wd_ref, ys_ref, wgu, wdn, hid):
    j = pl.program_id(0)
    f = EXPERT_FF
    changed = (j == 0) | (maps_ref[2, j] != maps_ref[2, jnp.maximum(j - 1, 0)])
    valid = maps_ref[3, j]

    @pl.when(changed)
    def _():
        for e in range(EXPERTS_PER_GROUP):
            wgu[e, :, :f] = wg_ref[e].astype(BF16)
            wgu[e, :, f:] = wu_ref[e].astype(BF16)
            wdn[e * f:(e + 1) * f, :] = wd_ref[e].astype(BF16)

    @pl.when(valid == 1)
    def _():
        x = xs_ref[:, :PAY_X].astype(BF16)
        wts = xs_ref[:, PAY_X:]
        for e in range(EXPERTS_PER_GROUP):
            gu = _dot(x, wgu[e])
            g, u = gu[:, :f], gu[:, f:]
            hid[:, e * f:(e + 1) * f] = ((g * _sigmoid(g)) * u * wts[:, e:e + 1]).astype(BF16)
        ys_ref[...] = _dot(hid[...], wdn[...])

    @pl.when(valid == 0)
    def _():
        ys_ref[...] = jnp.zeros_like(ys_ref)


def _expert_call(maps, xs, w_gate, w_up, w_down, layer, cap):
    n_steps = maps.shape[1]
    e, f = EXPERTS_PER_GROUP, EXPERT_FF
    w_gate = w_gate.reshape(-1, D_MODEL, f)
    w_up = w_up.reshape(-1, D_MODEL, f)
    w_down = w_down.reshape(-1, f, D_MODEL)
    w_blk = lambda j, m: (m[2, j] + layer * N_EXPERT_GROUPS, 0, 0)
    grid_spec = pltpu.PrefetchScalarGridSpec(
        num_scalar_prefetch=1,
        grid=(n_steps,),
        in_specs=[
            pl.BlockSpec((MOE_TILE, PAY_W), lambda j, m: (m[0, j], 0)),
            pl.BlockSpec((e, D_MODEL, f), w_blk),
            pl.BlockSpec((e, D_MODEL, f), w_blk),
            pl.BlockSpec((e, f, D_MODEL), w_blk),
        ],
        out_specs=pl.BlockSpec((MOE_TILE, D_MODEL), lambda j, m: (m[1, j], 0)),
        scratch_shapes=[
            pltpu.VMEM((e, D_MODEL, 2 * f), BF16),
            pltpu.VMEM((e * f, D_MODEL), BF16),
            pltpu.VMEM((MOE_TILE, e * f), BF16),
        ],
    )
    return pl.pallas_call(
        _expert_kernel,
        grid_spec=grid_spec,
        out_shape=jax.ShapeDtypeStruct((N_EXPERT_GROUPS * cap + MOE_TILE, D_MODEL), F32),
        compiler_params=_cparams(("arbitrary",)),
        name="moe_experts",
    )(maps, xs, w_gate, w_up, w_down)


def _combine_kernel(emit_x, dest_ref, x_ref, gnext_ref, ys_hbm, *rest):
    outs, (ybuf, sem) = rest[:-2], rest[-2:]
    i = pl.program_id(0)
    n_tiles = pl.num_programs(0) - 1
    slot = i % 2
    tm = x_ref.shape[0]

    for s in range(2):
        @pl.when((i < n_tiles) & (slot == s))
        def _():
            for r in range(tm):
                pltpu.make_async_copy(ys_hbm.at[pl.ds(dest_ref[0, 0, r], 1), :],
                                      ybuf.at[s, pl.ds(r, 1), :], sem.at[s]).start()

    @pl.when(i >= 1)
    def _():
        prev = 1 - slot
        pltpu.make_async_copy(ys_hbm.at[pl.ds(0, tm), :], ybuf.at[prev], sem.at[prev]).wait()
        x2 = x_ref[...] + ybuf[prev]
        normed = _rms(x2, gnext_ref[...])
        if emit_x:
            outs[0][...] = x2
            outs[1][...] = normed
        else:
            outs[0][...] = normed


def _combine_call(dest, x2d, ys, gnext, emit_x):
    t = x2d.shape[0]
    tm = COMBINE_TILE
    n_tiles = t // tm
    dest = dest.reshape(n_tiles, 1, tm)
    n_out = 2 if emit_x else 1
    done = lambda i: (jnp.maximum(i - 1, 0), 0)
    return pl.pallas_call(
        functools.partial(_combine_kernel, emit_x),
        grid=(n_tiles + 1,),
        in_specs=[
            pl.BlockSpec((1, 1, tm), lambda i: (jnp.minimum(i, n_tiles - 1), 0, 0),
                         memory_space=pltpu.SMEM),
            pl.BlockSpec((tm, D_MODEL), done), _const_spec((1, D_MODEL)),
            pl.BlockSpec(memory_space=pl.ANY),
        ],
        out_specs=[pl.BlockSpec((tm, D_MODEL), done)] * n_out,
        out_shape=[jax.ShapeDtypeStruct((t, D_MODEL), F32)] * n_out,
        scratch_shapes=[pltpu.VMEM((2, tm, D_MODEL), F32), pltpu.SemaphoreType.DMA((2,))],
        compiler_params=_cparams(("arbitrary",)),
        name="moe_combine",
    )(dest, x2d, gnext, ys)


def _s5_state_in(u_ref_val, wre_ref, wim_ref, store_re, store_im, pair_w):
    for m in range(S5_GB // 2):
        up = u_ref_val(m * pair_w, pair_w)
        store_re(m, _dot(up, wre_ref[m]))
        store_im(m, _dot(up, wim_ref[m]))


def _s5_outputs(u_ref_val, hre, him, m_ref, gre_ref, gim_ref, y_store, pair_w):
    gw = pair_w // 2
    for m in range(S5_GB // 2):
        hr = hre(m).astype(BF16)
        hi = him(m).astype(BF16)
        y = _dot(hr, gre_ref[m]) + _dot(hi, gim_ref[m])
        y0 = y[:, :gw] + _dot(u_ref_val(m * pair_w, gw), m_ref[2 * m])
        y1 = y[:, gw:] + _dot(u_ref_val(m * pair_w + gw, gw), m_ref[2 * m + 1])
        y_store(m * pair_w, gw, y0)
        y_store(m * pair_w + gw, gw, y1)


def _s5_flatten(load_rows, q, n, ut, uflat):
    gc = SSM_GROUP_CH
    qc = q * gc
    for s in range(q):
        ut[:, s * gc:(s + 1) * gc, :] = load_rows(s).astype(BF16).T.reshape(S5_GB, gc, n)
    for g in range(S5_GB):
        uflat[:, g * qc:(g + 1) * qc] = ut[g].T


def _s5_unflatten(yflat, yt, store_rows, q, n):
    gc = SSM_GROUP_CH
    qc = q * gc
    for g in range(S5_GB):
        yt[g] = yflat[:, g * qc:(g + 1) * qc].T
    for t in range(q):
        store_rows(t, yt[:, t * gc:(t + 1) * gc, :].reshape(S5_GB * gc, n).T)


def _s5_prompt_kernel(u_ref, wre_ref, wim_ref, m_ref, gre_ref, gim_ref, aqr_ref, aqi_ref,
                      y_ref, her_ref, hei_ref, ut, uflat, sre, sim, hre, him, yflat, yt):
    pair_w = 2 * S5_CHUNK * SSM_GROUP_CH
    n_chunks = u_ref.shape[1] // S5_CHUNK
    n_pairs = S5_GB // 2
    seq_rows = lambda b: pl.ds(b, n_chunks, stride=S5_SEQS)
    for b in range(S5_SEQS):
        _s5_flatten(lambda s: u_ref[b, pl.ds(s, n_chunks, stride=S5_CHUNK), :],
                    S5_CHUNK, n_chunks, ut, uflat.at[b])
        u_val = lambda off, w: uflat[b, :, off:off + w]

        def store_re(m, val):
            sre[m, seq_rows(b), :] = val

        def store_im(m, val):
            sim[m, seq_rows(b), :] = val

        _s5_state_in(u_val, wre_ref, wim_ref, store_re, store_im, pair_w)
    ar = [aqr_ref[:, m * LANES:(m + 1) * LANES] for m in range(n_pairs)]
    ai = [aqi_ref[:, m * LANES:(m + 1) * LANES] for m in range(n_pairs)]

    def step(n, carry):
        rows = pl.ds(pl.multiple_of(n * S5_SEQS, S5_SEQS), S5_SEQS)
        out = []
        for m in range(n_pairs):
            hr, hi = carry[2 * m], carry[2 * m + 1]
            hre[m, rows, :] = hr
            him[m, rows, :] = hi
            out.append(ar[m] * hr - ai[m] * hi + sre[m, rows, :])
            out.append(ar[m] * hi + ai[m] * hr + sim[m, rows, :])
        return tuple(out)

    zero = jnp.zeros((S5_SEQS, LANES), F32)
    last = lax.fori_loop(0, n_chunks, step, (zero,) * (2 * n_pairs))
    for b in range(S5_SEQS):
        for m in range(n_pairs):
            her_ref[b, :, m * LANES:(m + 1) * LANES] = last[2 * m][b:b + 1]
            hei_ref[b, :, m * LANES:(m + 1) * LANES] = last[2 * m + 1][b:b + 1]

    def y_store(off, w, val):
        yflat[:, off:off + w] = val

    for b in range(S5_SEQS):
        u_val = lambda off, w: uflat[b, :, off:off + w]
        _s5_outputs(u_val, lambda m: hre[m, seq_rows(b), :], lambda m: him[m, seq_rows(b), :],
                    m_ref, gre_ref, gim_ref, y_store, pair_w)

        def store_rows(t, val):
            y_ref[b, pl.ds(t, n_chunks, stride=S5_CHUNK), :] = val

        _s5_unflatten(yflat, yt, store_rows, S5_CHUNK, n_chunks)


def _s5_sample_kernel(n_new, u_ref, h0r_ref, h0i_ref, wre_ref, wim_ref, m_ref, gre_ref, gim_ref,
                      aqr_ref, aqi_ref, y_ref, hnr_ref, hni_ref, ut, uflat, sre, sim, yflat, yt):
    pair_w = 2 * n_new * SSM_GROUP_CH
    seqs = h0r_ref.shape[0]
    _s5_flatten(lambda s: u_ref[pl.ds(s, seqs, stride=n_new), :], n_new, seqs, ut, uflat)
    u_val = lambda off, w: uflat[:, off:off + w]

    def store_re(m, val):
        sre[:, m * LANES:(m + 1) * LANES] = val

    def store_im(m, val):
        sim[:, m * LANES:(m + 1) * LANES] = val

    _s5_state_in(u_val, wre_ref, wim_ref, store_re, store_im, pair_w)
    ar = aqr_ref[...]
    ai = aqi_ref[...]
    h0r = h0r_ref[...]
    h0i = h0i_ref[...]
    hnr_ref[...] = ar * h0r - ai * h0i + sre[...]
    hni_ref[...] = ar * h0i + ai * h0r + sim[...]

    def y_store(off, w, val):
        yflat[:, off:off + w] = val

    _s5_outputs(u_val, lambda m: h0r_ref[:, m * LANES:(m + 1) * LANES],
                lambda m: h0i_ref[:, m * LANES:(m + 1) * LANES],
                m_ref, gre_ref, gim_ref, y_store, pair_w)

    def store_rows(t, val):
        y_ref[pl.ds(t, seqs, stride=n_new), :] = val

    _s5_unflatten(yflat, yt, store_rows, n_new, seqs)


def _s5_weight_specs(q, idx):
    qc = q * SSM_GROUP_CH
    np_ = S5_GB // 2
    st = S5_GB * SSM_STATE
    return [
        pl.BlockSpec((np_, 2 * qc, LANES), lambda *a: (idx(*a), 0, 0)),
        pl.BlockSpec((np_, 2 * qc, LANES), lambda *a: (idx(*a), 0, 0)),
        pl.BlockSpec((S5_GB, qc, qc), lambda *a: (idx(*a), 0, 0)),
        pl.BlockSpec((np_, LANES, 2 * qc), lambda *a: (idx(*a), 0, 0)),
        pl.BlockSpec((np_, LANES, 2 * qc), lambda *a: (idx(*a), 0, 0)),
        pl.BlockSpec((1, st), lambda *a: (0, idx(*a))),
        pl.BlockSpec((1, st), lambda *a: (0, idx(*a))),
    ]


def _s5_prompt_call(u, w):
    b, seq, _ = u.shape
    n_chunks = seq // S5_CHUNK
    gbl = S5_GB * SSM_GROUP_CH
    qc = S5_CHUNK * SSM_GROUP_CH
    st = S5_GB * SSM_STATE
    n_gb = SSM_GROUPS // S5_GB
    gb_of = lambda g, i: g
    tok = pl.BlockSpec((S5_SEQS, seq, gbl), lambda g, i: (i, 0, g))
    slab = pltpu.VMEM((S5_GB // 2, n_chunks * S5_SEQS, LANES), F32)
    return pl.pallas_call(
        _s5_prompt_kernel,
        grid=(n_gb, b // S5_SEQS),
        in_specs=[tok] + _s5_weight_specs(S5_CHUNK, gb_of),
        out_specs=[
            tok,
            pl.BlockSpec((S5_SEQS, 1, st), lambda g, i: (i, 0, g)),
            pl.BlockSpec((S5_SEQS, 1, st), lambda g, i: (i, 0, g)),
        ],
        out_shape=[
            jax.ShapeDtypeStruct((b, seq, D_MODEL), F32),
            jax.ShapeDtypeStruct((b, 1, SSM_GROUPS * SSM_STATE), F32),
            jax.ShapeDtypeStruct((b, 1, SSM_GROUPS * SSM_STATE), F32),
        ],
        scratch_shapes=[
            pltpu.VMEM((S5_GB, qc, n_chunks), BF16),
            pltpu.VMEM((S5_SEQS, n_chunks, S5_GB * qc), BF16),
            slab, slab, slab, slab,
            pltpu.VMEM((n_chunks, S5_GB * qc), F32),
            pltpu.VMEM((S5_GB, qc, n_chunks), F32),
        ],
        compiler_params=_cparams(("parallel", "parallel")),
        name="s5_prompt",
    )(u, *w)


def _s5_sample_call(u2d, h0r, h0i, w, n_new):
    t = u2d.shape[0]
    b = t // n_new
    gbl = S5_GB * SSM_GROUP_CH
    qc = n_new * SSM_GROUP_CH
    st = S5_GB * SSM_STATE
    n_gb = SSM_GROUPS // S5_GB
    gb_of = lambda g: g
    state = pl.BlockSpec((b, st), lambda g: (0, g))
    tok = pl.BlockSpec((t, gbl), lambda g: (0, g))
    return pl.pallas_call(
        functools.partial(_s5_sample_kernel, n_new),
        grid=(n_gb,),
        in_specs=[tok, state, state] + _s5_weight_specs(n_new, gb_of),
        out_specs=[tok, state, state],
        out_shape=[
            jax.ShapeDtypeStruct((t, D_MODEL), F32),
            jax.ShapeDtypeStruct((b, SSM_GROUPS * SSM_STATE), F32),
            jax.ShapeDtypeStruct((b, SSM_GROUPS * SSM_STATE), F32),
        ],
        scratch_shapes=[
            pltpu.VMEM((S5_GB, qc, b), BF16),
            pltpu.VMEM((b, S5_GB * qc), BF16),
            pltpu.VMEM((b, st), F32), pltpu.VMEM((b, st), F32),
            pltpu.VMEM((b, S5_GB * qc), F32),
            pltpu.VMEM((S5_GB, qc, b), F32),
        ],
        compiler_params=_cparams(("parallel",)),
        name="s5_sample",
    )(u2d, h0r, h0i, *w)


def _s5_discretize(a_re, a_im, log_dt, b_re, b_im):
    delta = jnp.exp(log_dt.astype(F32))[:, None]
    lr, li = a_re.astype(F32), a_im.astype(F32)
    mag = jnp.exp(delta * lr)
    abar_r = mag * jnp.cos(delta * li)
    abar_i = mag * jnp.sin(delta * li)
    nr, ni = abar_r - 1.0, abar_i
    den = lr * lr + li * li
    coef_r = ((nr * lr + ni * li) / den)[..., None]
    coef_i = ((ni * lr - nr * li) / den)[..., None]
    br, bi = b_re.astype(F32), b_im.astype(F32)
    return delta * lr, delta * li, coef_r * br - coef_i * bi, coef_r * bi + coef_i * br


def _split3(x):
    x1 = x.astype(BF16)
    r1 = x - x1.astype(F32)
    x2 = r1.astype(BF16)
    x3 = (r1 - x2.astype(F32)).astype(BF16)
    return x1, x2, x3


def _s5_prep_kernel(qs, bbr_ref, bbi_ref, ctr_ref, cti_ref, ppr_ref, ppi_ref, pnr_ref, pni_ref,
                    *out_refs):
    c = SSM_GROUP_CH
    qc = max(qs) * c
    p2 = 2 * SSM_STATE
    col = lax.broadcasted_iota(I32, (c, qc), 1)
    row = lax.broadcasted_iota(I32, (c, qc), 0)
    by_ch = (col % c == row).astype(BF16)
    by_pos = (col // c == row).astype(BF16)

    def spread(x, sel):
        return sum(_dot(part, sel) for part in _split3(x))

    def dot_f32(a, b):
        a1, a2, _ = _split3(a)
        b1, b2, _ = _split3(b)
        return _dot(a1, b1) + _dot(a1, b2) + _dot(a2, b1)

    bc_r, bc_i = spread(bbr_ref[0], by_ch), spread(bbi_ref[0], by_ch)
    cc_r, cc_i = spread(ctr_ref[0], by_ch), spread(cti_ref[0], by_ch)
    pp_r, pp_i = spread(ppr_ref[0], by_pos), spread(ppi_ref[0], by_pos)
    pn_r, pn_i = spread(pnr_ref[0], by_pos), spread(pni_ref[0], by_pos)
    r_r = cc_r * pp_r - cc_i * pp_i
    r_i = cc_r * pp_i + cc_i * pp_r
    l_r = bc_r * pn_r - bc_i * pn_i
    l_i = bc_r * pn_i + bc_i * pn_r
    lt_r, lt_i = l_r.T, l_i.T
    grp_col = lax.broadcasted_iota(I32, (qc, p2), 1) // SSM_STATE
    causal =(lax.broadcasted_iota(I32, (qc, qc), 1) // c) >= (lax.broadcasted_iota(I32, (qc, qc), 0) // c)
    kers = [jnp.where(causal, dot_f32(jnp.where(grp_col == gl, lt_r, 0.0), r_r)
                      - dot_f32(jnp.where(grp_col == gl, lt_i, 0.0), r_i), 0.0) for gl in range(2)]
    for n, q in enumerate(qs):
        wre_ref, wim_ref, m_ref, gre_ref, gim_ref = out_refs[5 * n:5 * n + 5]
        w = q * c
        aq_r = ppr_ref[0][:, q - 1:q]
        aq_i = ppi_ref[0][:, q - 1:q]
        w_r = (l_r[:, :w] * aq_r - l_i[:, :w] * aq_i).T
        w_i = (l_r[:, :w] * aq_i + l_i[:, :w] * aq_r).T
        g_re, g_im = [], []
        col_grp = lax.broadcasted_iota(I32, (w, p2), 1) // SSM_STATE
        row_grp = lax.broadcasted_iota(I32, (p2, w), 0) // SSM_STATE
        for gl in range(2):
            wre_ref[0, gl * w:(gl + 1) * w, :] = jnp.where(col_grp == gl, w_r, 0.0).astype(BF16)
            wim_ref[0, gl * w:(gl + 1) * w, :] = jnp.where(col_grp == gl, w_i, 0.0).astype(BF16)
            m_ref[gl] = kers[gl][:w, :w].astype(BF16)
            g_re.append(jnp.where(row_grp == gl, r_r[:, :w], 0.0))
            g_im.append(jnp.where(row_grp == gl, -r_i[:, :w], 0.0))
        gre_ref[0] = jnp.concatenate(g_re, axis=1).astype(BF16)
        gim_ref[0] = jnp.concatenate(g_im, axis=1).astype(BF16)


def _s5_chunk_weights(tables, qs):
    n_pairs = SSM_GROUPS // 2
    p2 = 2 * SSM_STATE
    tab = pl.BlockSpec((1, p2, SSM_GROUP_CH), lambda m: (m, 0, 0))
    out_specs, out_shape = [], []
    for q in qs:
        qc = q * SSM_GROUP_CH
        out_specs += [pl.BlockSpec((1, 2 * qc, p2), lambda m: (m, 0, 0)),
                      pl.BlockSpec((1, 2 * qc, p2), lambda m: (m, 0, 0)),
                      pl.BlockSpec((2, qc, qc), lambda m: (m, 0, 0)),
                      pl.BlockSpec((1, p2, 2 * qc), lambda m: (m, 0, 0)),
                      pl.BlockSpec((1, p2, 2 * qc), lambda m: (m, 0, 0))]
        out_shape += [jax.ShapeDtypeStruct((n_pairs, 2 * qc, p2), BF16),
                      jax.ShapeDtypeStruct((n_pairs, 2 * qc, p2), BF16),
                      jax.ShapeDtypeStruct((SSM_GROUPS, qc, qc), BF16),
                      jax.ShapeDtypeStruct((n_pairs, p2, 2 * qc), BF16),
                      jax.ShapeDtypeStruct((n_pairs, p2, 2 * qc), BF16)]
    outs = pl.pallas_call(
        functools.partial(_s5_prep_kernel, tuple(qs)),
        grid=(n_pairs,),
        in_specs=[tab] * 8,
        out_specs=out_specs, out_shape=out_shape,
        compiler_params=_cparams(("parallel",)),
        name="s5_operators",
    )(*tables)
    return [tuple(outs[5 * n:5 * n + 5]) for n in range(len(qs))]


def _s5_tables(log_mag, phase, bbar_r, bbar_i, c_re, c_im, q_max):
    k = jnp.arange(1, q_max + 1, dtype=F32)[None, None, :]
    mag = jnp.exp(log_mag[:, :, None] * k)
    cos, sin = jnp.cos(phase[:, :, None] * k), jnp.sin(phase[:, :, None] * k)
    tabs = (bbar_r, bbar_i, c_re.astype(F32).transpose(0, 2, 1), c_im.astype(F32).transpose(0, 2, 1),
            mag * cos, mag * sin, cos / mag, -sin / mag)
    return tuple(t.reshape(SSM_GROUPS // 2, 2 * SSM_STATE, SSM_GROUP_CH) for t in tabs)


def _rope_tables(pos):
    half = HEAD_DIM // 2
    inv = 1.0 / (ROPE_THETA ** (jnp.arange(half, dtype=F32) * (2.0 / HEAD_DIM)))
    ang = pos.astype(F32)[:, None] * inv[None, :]
    cos, sin = jnp.cos(ang), jnp.sin(ang)
    return jnp.tile(cos, (1, 4)), jnp.concatenate([-sin, sin, -sin, sin], axis=1)


def _router_weights(gain, w_rg, b_rg, w_re, b_re):
    n_layers = gain.shape[0]
    pad = LANES - N_EXPERTS - N_EXPERT_GROUPS
    w = jnp.concatenate([w_re, w_rg, jnp.zeros((n_layers, D_MODEL, pad), F32)], axis=2)
    b = jnp.concatenate([b_re, b_rg, jnp.zeros((n_layers, pad), F32)], axis=1)
    wh = w.astype(BF16)
    wl = (w - wh.astype(F32)).astype(BF16)
    return [(gain[l].reshape(1, -1).astype(F32), wh[l], wl[l], b[l].reshape(1, LANES))
            for l in range(n_layers)]


def kernel(x_prompt, x_sample, cache_k, cache_v, state_ssm_re, state_ssm_im, norm_mix, norm_ffn, norm_final, attn_w_qkv, attn_b_qkv, attn_w_o, attn_b_o, attn_sinks, ssm_a_re, ssm_a_im, ssm_log_dt, ssm_b_re, ssm_b_im, ssm_c_re, ssm_c_im, ssm_d, ssm_w_glu_a, ssm_w_glu_b, moe_w_router_group, moe_b_router_group, moe_w_router_expert, moe_b_router_expert, moe_w_gate, moe_w_up, moe_w_down):
    bsz, seq, _ = x_prompt.shape
    dbs, n_new, _ = x_sample.shape
    rows = cache_k.shape[2]
    tp, ts = bsz * seq, dbs * n_new
    cap = _moe_cap(tp + ts)
    xp = x_prompt.reshape(tp, D_MODEL)
    xs = x_sample.reshape(ts, D_MODEL)

    row1 = lambda v: v.reshape(1, -1).astype(F32)
    routers = _router_weights(norm_ffn, moe_w_router_group, moe_b_router_group,
                              moe_w_router_expert, moe_b_router_expert)
    triu = jnp.triu(jnp.ones((TOKEN_TILE, TOKEN_TILE), F32), 1).astype(BF16)
    cnt0 = (jnp.zeros((LANES, LANES), F32), jnp.zeros((8, LANES), F32))

    slots = jnp.asarray(HEAD_SLOTS, dtype=I32)

    def q_slots(w):
        lead = w.shape[:-1]
        qh = w[..., :Q_DIM].reshape(*lead, N_HEADS, HEAD_DIM)[..., slots, :].reshape(*lead, Q_DIM)
        return jnp.concatenate([qh, w[..., Q_DIM:]], axis=-1)

    wqkv = q_slots(attn_w_qkv[0]).astype(BF16)
    bqkv = row1(q_slots(attn_b_qkv[0]))
    wo = attn_w_o[0].reshape(N_HEADS, HEAD_DIM, D_MODEL)[slots].reshape(Q_DIM, D_MODEL).astype(BF16)
    bo = row1(attn_b_o[0])
    sinks = attn_sinks[0].astype(F32)[slots]
    g_mix0, g_mix1 = row1(norm_mix[0]), row1(norm_mix[1])
    pos = jnp.concatenate([jnp.arange(seq, dtype=I32),
                           jnp.tile(PAST_LEN + jnp.arange(n_new, dtype=I32), dbs)])
    cos_t, sin_t = _rope_tables(pos)

    qp, kp, vp = _qkv_call(xp, g_mix0, wqkv, bqkv, cos_t, sin_t, (0, seq), BF16)
    qs, ks, vs = _qkv_call(xs, g_mix0, wqkv, bqkv, cos_t, sin_t, (seq, ts), F32)
    op = _attn_prompt_call(jnp.repeat(sinks, BLOCK // 2).reshape(1, -1), qp.reshape(bsz, seq, Q_DIM),
                           kp.reshape(bsz, seq, KV_DIM),
                           vp.reshape(bsz, seq, KV_DIM))
    os_, nks, nvs = _attn_sample_call(jnp.repeat(sinks, n_new).reshape(1, -1), qs, ks, vs,
                                      cache_k[0].reshape(dbs, rows, KV_DIM),
                                      cache_v[0].reshape(dbs, rows, KV_DIM), n_new, 8)
    xp1, dest_p, cnt, rows_x, _ = _oproj_call(op.reshape(tp, Q_DIM), xp, wo, bo, routers[0], triu,
                                              cnt0, None, False, cap)
    xs1, dest_s, cnt, rows_x, steps = _oproj_call(os_, xs, wo, bo, routers[0], triu, cnt, rows_x,
                                                  True, cap)
    rows_y = _expert_call(steps, rows_x, moe_w_gate, moe_w_up, moe_w_down, 0, cap)
    xp2, up = _combine_call(dest_p, xp1, rows_y, g_mix1, True)
    xs2, us = _combine_call(dest_s, xs1, rows_y, g_mix1, True)

    disc = _s5_discretize(ssm_a_re[0], ssm_a_im[0], ssm_log_dt[0], ssm_b_re[0], ssm_b_im[0])
    tables = _s5_tables(*disc, ssm_c_re[0], ssm_c_im[0], S5_CHUNK)
    a_pow = lambda q: [tables[i][:, :, q - 1].reshape(1, -1) for i in (4, 5)]
    ops_p, ops_s = _s5_chunk_weights(tables, (S5_CHUNK, n_new))
    w_p = (*ops_p, *a_pow(S5_CHUNK))
    w_s = (*ops_s, *a_pow(n_new))
    y_p, hpr, hpi = _s5_prompt_call(up.reshape(bsz, seq, D_MODEL), w_p)
    y_p = y_p.reshape(tp, D_MODEL)
    h0r = state_ssm_re[0].reshape(dbs, -1).astype(F32)
    h0i = state_ssm_im[0].reshape(dbs, -1).astype(F32)
    y_s, hsr, hsi = _s5_sample_call(us, h0r, h0i, w_s, n_new)

    wa, wb = ssm_w_glu_a[0].astype(BF16), ssm_w_glu_b[0].astype(BF16)
    d_row = row1(ssm_d[0])
    xp3, dest_p, cnt, rows_x, _ = _glu_call(y_p, xp2, g_mix1, d_row, wa, wb, routers[1], triu,
                                            cnt0, None, False, cap)
    xs3, dest_s, cnt, rows_x, steps = _glu_call(y_s, xs2, g_mix1, d_row, wa, wb, routers[1], triu,
                                                cnt, rows_x, True, cap)
    rows_y = _expert_call(steps, rows_x, moe_w_gate, moe_w_up, moe_w_down, 1, cap)
    g_fin = row1(norm_final)
    (yp,) = _combine_call(dest_p, xp3, rows_y, g_fin, False)
    (ys,) = _combine_call(dest_s, xs3, rows_y, g_fin, False)

    kv5 = lambda a, n: a.reshape(1, n, rows, KV_HEADS, HEAD_DIM)
    st4 = lambda a, n: a.reshape(1, n, SSM_GROUPS, SSM_STATE)
    k_last = kp.reshape(bsz, seq, KV_DIM)[:, seq - WINDOW:]
    v_last = vp.reshape(bsz, seq, KV_DIM)[:, seq - WINDOW:]
    return (yp.reshape(bsz, seq, D_MODEL), ys.reshape(dbs, n_new, D_MODEL),
            k_last.reshape(1, bsz, WINDOW, KV_HEADS, HEAD_DIM), kv5(nks, dbs),
            v_last.reshape(1, bsz, WINDOW, KV_HEADS, HEAD_DIM), kv5(nvs, dbs),
            st4(hpr, bsz), st4(hsr, dbs), st4(hpi, bsz), st4(hsi, dbs))
```

```python
import collections
import functools
import math

import jax
import jax.numpy as jnp
from jax import lax
from jax.experimental import pallas as pl
from jax.experimental.pallas import tpu as pltpu

F32 = jnp.float32
BF16 = jnp.bfloat16
I32 = jnp.int32

D_MODEL = 1024
N_HEADS = 16
KV_HEADS = 4
HEAD_DIM = 64
Q_DIM = N_HEADS * HEAD_DIM
KV_DIM = KV_HEADS * HEAD_DIM
QKV_DIM = Q_DIM + 2 * KV_DIM
WINDOW = 128
BLOCK = 128
ROPE_THETA = 10000.0
PAST_LEN = 16384
SSM_GROUP_CH = 16
SSM_GROUPS = D_MODEL // SSM_GROUP_CH
SSM_STATE = 64
N_EXPERT_GROUPS = 4
EXPERTS_PER_GROUP = 8
N_EXPERTS = N_EXPERT_GROUPS * EXPERTS_PER_GROUP
EXPERT_FF = 128
NORM_EPS = 1e-5

LANES = 128
VMEM_LIMIT = 56 * 1024 * 1024
S5_CHUNK = 16
S5_GB = 8
S5_SEQS = 4
TOKEN_TILE = 1024
MOE_TILE = 512
ROUTE_PARTS = 2
COMBINE_TILE = 512
ATTN_PARTS = 4
ATTN_BLOCKS = 4
PAY_X = D_MODEL
PAY_W = PAY_X + LANES


def _cparams(sem):
    return pltpu.CompilerParams(dimension_semantics=sem, vmem_limit_bytes=VMEM_LIMIT)


def _rms(x, g):
    return x * lax.rsqrt(jnp.mean(x * x, axis=-1, keepdims=True) + NORM_EPS) * g


def _dot(a, b):
    return jnp.dot(a, b, preferred_element_type=F32)


def _dot_nt(a, b):
    return lax.dot_general(a, b, (((1,), (1,)), ((), ())), preferred_element_type=F32)


def _sigmoid(x):
    return 1.0 / (1.0 + jnp.exp(-x))


def _qkv_kernel(x_ref, g_ref, w_ref, b_ref, cos_ref, sin_ref, q_ref, k_ref, v_ref):
    xn = _rms(x_ref[...], g_ref[...])
    qkv = _dot(xn.astype(BF16), w_ref[...]) + b_ref[...]
    cos = cos_ref[...]
    sin = sin_ref[...]
    lane = lax.broadcasted_iota(I32, cos.shape, 1)
    first_half = (lane % HEAD_DIM) < (HEAD_DIM // 2)
    n_rot = (Q_DIM + KV_DIM) // LANES
    for c in range(n_rot):
        blk = qkv[:, c * LANES:(c + 1) * LANES]
        partner = jnp.where(first_half,
                            pltpu.roll(blk, LANES - HEAD_DIM // 2, 1),
                            pltpu.roll(blk, HEAD_DIM // 2, 1))
        rot = blk * cos + partner * sin
        if c < Q_DIM // LANES:
            q_ref[:, c * LANES:(c + 1) * LANES] = (rot * (HEAD_DIM ** -0.5)).astype(q_ref.dtype)
        else:
            k_ref[:, c * LANES - Q_DIM:(c + 1) * LANES - Q_DIM] = rot
    v_ref[...] = qkv[:, Q_DIM + KV_DIM:]


def _qkv_call(x2d, gain, w_bf, bias, cos_t, sin_t, pos_rows, q_dtype):
    t = x2d.shape[0]
    tm = TOKEN_TILE
    first, n_pos = pos_rows[0] // tm, pos_rows[1] // tm
    return pl.pallas_call(
        _qkv_kernel,
        grid=(t // tm,),
        in_specs=[
            pl.BlockSpec((tm, D_MODEL), lambda i: (i, 0)),
            pl.BlockSpec((1, D_MODEL), lambda i: (0, 0)),
            pl.BlockSpec((D_MODEL, QKV_DIM), lambda i: (0, 0)),
            pl.BlockSpec((1, QKV_DIM), lambda i: (0, 0)),
            pl.BlockSpec((tm, LANES), lambda i: (first + i % n_pos, 0)),
            pl.BlockSpec((tm, LANES), lambda i: (first + i % n_pos, 0)),
        ],
        out_specs=[
            pl.BlockSpec((tm, Q_DIM), lambda i: (i, 0)),
            pl.BlockSpec((tm, KV_DIM), lambda i: (i, 0)),
            pl.BlockSpec((tm, KV_DIM), lambda i: (i, 0)),
        ],
        out_shape=[
            jax.ShapeDtypeStruct((t, Q_DIM), q_dtype),
            jax.ShapeDtypeStruct((t, KV_DIM), F32),
            jax.ShapeDtypeStruct((t, KV_DIM), F32),
        ],
        compiler_params=_cparams(("parallel",)),
        name="qkv_rope",
    )(x2d, gain, w_bf, bias, cos_t, sin_t)


HEAD_SLOTS = tuple(8 * c + 4 * half + i for c in range(2) for i in range(4) for half in range(2))
LOG2E = math.log2(math.e)


def _attn_prompt_kernel(sink_ref, q_ref, *refs):
    nk = ATTN_BLOCKS + 1
    k_refs, v_refs, o_ref = refs[:nk], refs[nk:2 * nk], refs[2 * nk]
    first = pl.program_id(1) * ATTN_BLOCKS
    kb = [r[0].astype(BF16) for r in k_refs]
    vb = [r[0].astype(BF16) for r in v_refs]
    k2s = [jnp.concatenate(kb[i:i + 2], axis=0) for i in range(ATTN_BLOCKS)]
    v2s = [jnp.concatenate(vb[i:i + 2], axis=0) for i in range(ATTN_BLOCKS)]
    sub = BLOCK // ATTN_PARTS
    win = WINDOW + sub
    cols = 2 * BLOCK
    spt = cols // sub
    key = lax.broadcasted_iota(I32, (win, cols), 0)
    qry = lax.broadcasted_iota(I32, (win, cols), 1) % sub
    band = (key >= qry) & (key <= qry + WINDOW)
    masks = [[band & ((first + blk > 0) | (key + h * sub >= BLOCK)) for h in range(ATTN_PARTS)]
             for blk in range(ATTN_BLOCKS)]
    low = lax.broadcasted_iota(I32, (sub, LANES), 1) < HEAD_DIM
    keep_low = low.astype(BF16)
    keep_high = 1 - keep_low
    tasks = [(blk, grp, h) for blk in range(ATTN_BLOCKS) for grp in range(N_HEADS // spt)
             for h in range(ATTN_PARTS)]
    pairs = lambda grp: range(grp * spt // 2, (grp + 1) * spt // 2)
    chunk = lambda grp: slice((grp * spt // 8) * LANES, (grp * spt // 8 + 1) * LANES)
    q_rows = lambda blk, h: slice(blk * BLOCK + h * sub, blk * BLOCK + (h + 1) * sub)

    def scores(blk, grp, h):
        blocks = []
        for pr in pairs(grp):
            qblk = q_ref[0, q_rows(blk, h), pr * LANES:(pr + 1) * LANES]
            blocks += [qblk * keep_low, qblk * keep_high]
        qz = jnp.concatenate(blocks, axis=0)
        s = _dot_nt(k2s[blk][h * sub:h * sub + win, chunk(grp)], qz) * LOG2E
        return jnp.where(masks[blk][h], s, -jnp.inf)

    def weights(blk, grp, h, s):
        sink = sink_ref[:, grp * cols:(grp + 1) * cols] * LOG2E
        m = jnp.maximum(jnp.max(s, axis=0, keepdims=True), sink)
        p = jnp.exp2(s - m)
        denom = jnp.sum(p, axis=0, keepdims=True) + jnp.exp2(sink - m)
        rows = [jnp.zeros((h * sub, cols), F32), p * (1.0 / denom),
                jnp.zeros((2 * BLOCK - win - h * sub, cols), F32)]
        w = jnp.concatenate([r for r in rows if r.shape[0]], axis=0)
        return w.T.astype(BF16)

    def finish(blk, grp, h, w):
        oz = _dot(w, v2s[blk][:, chunk(grp)])
        for j, pr in enumerate(pairs(grp)):
            o_ref[0, q_rows(blk, h), pr * LANES:(pr + 1) * LANES] = jnp.where(
                low, oz[2 * j * sub:(2 * j + 1) * sub], oz[(2 * j + 1) * sub:(2 * j + 2) * sub]
            ).astype(o_ref.dtype)

    all_s = [scores(*t) for t in tasks]
    all_w = [weights(*t, s) for t, s in zip(tasks, all_s)]
    for t, w in zip(tasks, all_w):
        finish(*t, w)


def _attn_prompt_call(sinks, q, k, v):
    b, l, _ = q.shape
    nb = l // (BLOCK * ATTN_BLOCKS)
    kv = [pl.BlockSpec((1, BLOCK, KV_DIM),
                       lambda i, n, d=d: (i, jnp.maximum(n * ATTN_BLOCKS + d, 0), 0))
          for d in range(-1, ATTN_BLOCKS)]
    qo = pl.BlockSpec((1, BLOCK * ATTN_BLOCKS, Q_DIM), lambda i, n: (i, n, 0))
    return pl.pallas_call(
        _attn_prompt_kernel,
        grid=(b, nb),
        in_specs=[pl.BlockSpec((1, N_HEADS * BLOCK // ATTN_PARTS), lambda i, n: (0, 0)), qo] + kv + kv,
        out_specs=qo,
        out_shape=jax.ShapeDtypeStruct((b, l, Q_DIM), BF16),
        compiler_params=_cparams(("parallel", "parallel")),
        name="attn_prompt",
    )(sinks, q, *([k] * len(kv)), *([v] * len(kv)))


def _attn_sample_kernel(n_new, seqs, sink_ref, q_ref, kn_ref, vn_ref, ck_ref, cv_ref,
                        o_ref, nk_ref, nv_ref):
    rows = ck_ref.shape[1]
    keys = 2 * rows
    n_cols = N_HEADS * n_new
    key = lax.broadcasted_iota(I32, (keys, n_cols), 0)
    qry = lax.broadcasted_iota(I32, (keys, n_cols), 1) % n_new
    mask = ((key < rows) & (key >= qry)) | ((key >= rows) & (key - rows <= qry))
    low = lax.broadcasted_iota(I32, (n_new, LANES), 1) < HEAD_DIM
    sink = sink_ref[...]
    pad = jnp.zeros((rows - n_new, KV_DIM), F32)

    def body(sb, carry):
        r0 = pl.multiple_of(sb * n_new, n_new)
        k_new = kn_ref[pl.ds(r0, n_new), :]
        v_new = vn_ref[pl.ds(r0, n_new), :]
        k_all = jnp.concatenate([ck_ref[sb], k_new, pad], axis=0).astype(BF16)
        v_all = jnp.concatenate([cv_ref[sb], v_new, pad], axis=0).astype(BF16)
        blocks = []
        for slot in range(N_HEADS):
            pair = q_ref[pl.ds(r0, n_new), (slot // 2) * LANES:(slot // 2 + 1) * LANES]
            blk = jnp.where(low if slot % 2 == 0 else ~low, pair, 0.0)
            zero = jnp.zeros_like(blk)
            blocks.append(jnp.concatenate([blk, zero] if slot < N_HEADS // 2 else [zero, blk], axis=1))
        qz = jnp.concatenate(blocks, axis=0).astype(BF16)
        s = _dot_nt(k_all, qz)
        s = jnp.where(mask, s, -jnp.inf)
        m = jnp.maximum(jnp.max(s, axis=0, keepdims=True), sink)
        p = jnp.exp(s - m)
        denom = jnp.sum(p, axis=0, keepdims=True) + jnp.exp(sink - m)
        w = (p * (1.0 / denom)).T.astype(BF16)
        oz = _dot(w, v_all)
        for pr in range(N_HEADS // 2):
            lanes = slice((pr // (N_HEADS // 4)) * LANES, (pr // (N_HEADS // 4) + 1) * LANES)
            a = oz[2 * pr * n_new:(2 * pr + 1) * n_new, lanes]
            b = oz[(2 * pr + 1) * n_new:(2 * pr + 2) * n_new, lanes]
            o_ref[pl.ds(r0, n_new), pr * LANES:(pr + 1) * LANES] = jnp.where(low, a, b)
        nk_ref[sb, 0:rows - n_new, :] = ck_ref[sb, n_new:rows, :]
        nk_ref[sb, rows - n_new:rows, :] = k_new
        nv_ref[sb, 0:rows - n_new, :] = cv_ref[sb, n_new:rows, :]
        nv_ref[sb, rows - n_new:rows, :] = v_new
        return carry

    lax.fori_loop(0, seqs, body, 0)


def _attn_sample_call(sink_row, q, k_new, v_new, cache_k, cache_v, n_new, seqs):
    b, rows, _ = cache_k.shape
    tok = pl.BlockSpec((seqs * n_new, Q_DIM), lambda i: (i, 0))
    tok_kv = pl.BlockSpec((seqs * n_new, KV_DIM), lambda i: (i, 0))
    cache = pl.BlockSpec((seqs, rows, KV_DIM), lambda i: (i, 0, 0))
    return pl.pallas_call(
        functools.partial(_attn_sample_kernel, n_new, seqs),
        grid=(b // seqs,),
        in_specs=[pl.BlockSpec((1, N_HEADS * n_new), lambda i: (0, 0)), tok, tok_kv, tok_kv, cache, cache],
        out_specs=[tok, cache, cache],
        out_shape=[
            jax.ShapeDtypeStruct((b * n_new, Q_DIM), F32),
            jax.ShapeDtypeStruct((b, rows, KV_DIM), F32),
            jax.ShapeDtypeStruct((b, rows, KV_DIM), F32),
        ],
        compiler_params=_cparams(("parallel",)),
        name="attn_sample",
    )(sink_row, q, k_new, v_new, cache_k, cache_v)


def _moe_cap(t_total):
    return t_total + MOE_TILE


def _expert_steps(cap):
    return cap // MOE_TILE - 1 + N_EXPERT_GROUPS


def _write_step_table(maps_ref, counts, cap):
    per = cap // MOE_TILE
    shift = MOE_TILE.bit_length() - 1
    ends = []
    for c in counts:
        tiles = lax.shift_right_logical(c + (MOE_TILE - 1), shift)
        ends.append(tiles if not ends else ends[-1] + tiles)
    total = ends[-1]
    for j in range(maps_ref.shape[1]):
        jj = jnp.maximum(jnp.minimum(j, total - 1), 0)
        g = sum((jj >= e).astype(I32) for e in ends[:-1])
        start = sum(jnp.where(g > k, ends[k] - (ends[k - 1] if k else 0), 0) for k in range(len(ends) - 1))
        blk_in = g * per + jj - start
        valid = (total > j).astype(I32)
        maps_ref[0, j] = blk_in
        maps_ref[1, j] = jnp.where(valid == 1, blk_in, len(counts) * per)
        maps_ref[2, j] = g
        maps_ref[3, j] = valid


def _route_rows(x1, gn_ref, wrh_ref, wrl_ref, br_ref):
    tm = x1.shape[0]
    xn = _rms(x1, gn_ref[...])
    xh = xn.astype(BF16)
    xl = (xn - xh.astype(F32)).astype(BF16)
    logits = _dot(xh, wrh_ref[...]) + _dot(xl, wrh_ref[...]) + _dot(xh, wrl_ref[...]) + br_ref[...]
    lt = logits.T
    ge = N_EXPERTS // EXPERTS_PER_GROUP
    sub = lax.broadcasted_iota(I32, (EXPERTS_PER_GROUP, tm), 0).astype(F32)
    big = jnp.float32(LANES)
    neg = -jnp.inf
    gl = jnp.where(sub < N_EXPERT_GROUPS, lt[N_EXPERTS:N_EXPERTS + EXPERTS_PER_GROUP], neg)
    gmax = jnp.max(gl, axis=0, keepdims=True)
    g_val = 1.0 / jnp.sum(jnp.exp(gl - gmax), axis=0, keepdims=True)
    g_idx = jnp.min(jnp.where(gl == gmax, sub, big), axis=0, keepdims=True)
    el = lt[0:EXPERTS_PER_GROUP]
    for g in range(1, ge):
        el = jnp.where(g_idx == g, lt[g * EXPERTS_PER_GROUP:(g + 1) * EXPERTS_PER_GROUP], el)
    e1 = jnp.max(el, axis=0, keepdims=True)
    i1 = jnp.min(jnp.where(el == e1, sub, big), axis=0, keepdims=True)
    el2 = jnp.where(sub == i1, neg, el)
    e2 = jnp.max(el2, axis=0, keepdims=True)
    i2 = jnp.min(jnp.where(el2 == e2, sub, big), axis=0, keepdims=True)
    t = jnp.exp(e2 - e1)
    w1 = 1.0 / (1.0 + t)
    w2 = t / (1.0 + t)
    wts_t = g_val * (jnp.where(sub == i1, w1, 0.0) + jnp.where(sub == i2, w2, 0.0))
    wts = jnp.concatenate([wts_t, jnp.zeros((LANES - EXPERTS_PER_GROUP, tm), F32)], axis=0).T
    return xn, wts, g_idx


def _route_begin(cntc_in, cntr_in, xs_hbm, pay, cntc, cntr, sem):
    i = pl.program_id(0)
    slot = i % 2

    @pl.when(i == 0)
    def _():
        cntc[...] = cntc_in[...]
        cntr[...] = cntr_in[...]

    @pl.when(i >= 2)
    def _():
        pltpu.make_async_copy(pay.at[slot], xs_hbm.at[pl.ds(0, pay.shape[1]), :], sem.at[slot]).wait()


def _route_scatter(final, cap, parts, triu_ref, dest_ref, cntc_out, cntr_out, xs_hbm, maps_ref,
                   pay, zrows, dest_v, dest_s, cntc, cntr, sem):
    i = pl.program_id(0)
    n = pl.num_programs(0)
    slot = i % 2
    tm = pay.shape[1]

    def wait_slot(s):
        pltpu.make_async_copy(pay.at[s], xs_hbm.at[pl.ds(0, tm), :], sem.at[s]).wait()

    r0 = 0
    for xn, wts, _ in parts:
        pay[slot, r0:r0 + xn.shape[0], :PAY_X] = xn
        pay[slot, r0:r0 + xn.shape[0], PAY_X:] = wts
        r0 += xn.shape[0]
    g_idx = jnp.concatenate([p[2] for p in parts], axis=1)

    grp_t = lax.broadcasted_iota(I32, (LANES, tm), 0).astype(F32)
    oht = (grp_t == g_idx).astype(F32)
    rank = _dot(oht.astype(BF16), triu_ref[...])
    grp = lax.broadcasted_iota(I32, (LANES, 1), 0).astype(F32)
    base = grp * float(cap) + cntc[:, 0:1]
    dest = jnp.sum(oht * (rank + base), axis=0, keepdims=True).astype(I32)
    cntc[...] = cntc[...] + jnp.sum(oht, axis=1, keepdims=True)
    dest_ref[0] = dest
    dest_v[...] = dest
    pltpu.sync_copy(dest_v, dest_s)

    for s in range(2):
        @pl.when(slot == s)
        def _():
            for r in range(tm):
                pltpu.make_async_copy(pay.at[s, pl.ds(r, 1), :],
                                      xs_hbm.at[pl.ds(dest_s[0, r], 1), :], sem.at[s]).start()

    @pl.when(i == n - 1)
    def _():
        cntc_out[...] = cntc[...]
        cntr[...] = cntc[...].T[0:8, :]
        cntr_out[...] = cntr[...]

        @pl.when(n >= 2)
        def _():
            wait_slot(1 - slot)

        wait_slot(slot)
        dest_v[:, 0:LANES] = cntr[0:1, :].astype(I32)
        pltpu.sync_copy(dest_v, dest_s)
        counts = [dest_s[0, g] for g in range(N_EXPERT_GROUPS)]
        _write_step_table(maps_ref, counts, cap)
        if final:
            zrows[...] = jnp.zeros_like(zrows)
            starts = []
            for g in range(N_EXPERT_GROUPS):
                c_g = counts[g]
                starts.append(pl.multiple_of(g * cap + lax.shift_left(lax.shift_right_logical(c_g + 7, 3), 3), 8))
                for k in range(7):
                    pltpu.make_async_copy(zrows.at[pl.ds(0, 1), :],
                                          xs_hbm.at[pl.ds(g * cap + c_g + k, 1), :], sem.at[1]).start()
            for _ in range(7 * N_EXPERT_GROUPS):
                pltpu.make_async_copy(zrows.at[pl.ds(0, 1), :], xs_hbm.at[pl.ds(0, 1), :], sem.at[1]).wait()
            for g in range(N_EXPERT_GROUPS):
                pltpu.make_async_copy(zrows, xs_hbm.at[pl.ds(starts[g], MOE_TILE), :], sem.at[0]).start()
            for g in range(N_EXPERT_GROUPS):
                pltpu.make_async_copy(zrows, xs_hbm.at[pl.ds(0, MOE_TILE), :], sem.at[0]).wait()


RouteRefs = collections.namedtuple(
    "RouteRefs", "gn wrh wrl br triu cntc_in cntr_in x1 dest cntc_out cntr_out xs maps "
                 "pay zrows dest_v dest_s cntc cntr sem")


def _route_refs(rest, has_prev):
    if has_prev:
        rest = rest[:7] + rest[8:]
    return RouteRefs(*rest)


def _route_tile(final, cap, r, x1_parts):
    parts = [_route_rows(x1, r.gn, r.wrh, r.wrl, r.br) for x1 in x1_parts]
    _route_scatter(final, cap, parts, r.triu, r.dest, r.cntc_out, r.cntr_out, r.xs, r.maps,
                   r.pay, r.zrows, r.dest_v, r.dest_s, r.cntc, r.cntr, r.sem)


def _oproj_kernel(has_prev, final, cap, *refs):
    (o_ref, x_ref, wo_ref, bo_ref) = refs[:4]
    r = _route_refs(refs[4:], has_prev)
    _route_begin(r.cntc_in, r.cntr_in, r.xs, r.pay, r.cntc, r.cntr, r.sem)
    rows = x_ref.shape[0] // ROUTE_PARTS
    x1_parts = []
    for h in range(ROUTE_PARTS):
        rs = slice(h * rows, (h + 1) * rows)
        x1 = x_ref[rs, :] + _dot(o_ref[rs, :].astype(BF16), wo_ref[...]) + bo_ref[...]
        r.x1[rs, :] = x1
        x1_parts.append(x1)
    _route_tile(final, cap, r, x1_parts)


def _gelu_tanh(x):
    return x * (0.5 * (1.0 + jnp.tanh(math.sqrt(2.0 / math.pi) * (x + 0.044715 * (x * x * x)))))


def _glu_kernel(has_prev, final, cap, *refs):
    (y_ref, x_ref, gm_ref, d_ref, wa_ref, wb_ref) = refs[:6]
    r = _route_refs(refs[6:], has_prev)
    _route_begin(r.cntc_in, r.cntr_in, r.xs, r.pay, r.cntc, r.cntr, r.sem)
    rows = x_ref.shape[0] // ROUTE_PARTS
    gated = []
    for h in range(ROUTE_PARTS):
        rs = slice(h * rows, (h + 1) * rows)
        x = x_ref[rs, :]
        z = _gelu_tanh(y_ref[rs, :] + d_ref[...] * _rms(x, gm_ref[...])).astype(BF16)
        gated.append((rs, x, _dot(z, wa_ref[...]), _dot(z, wb_ref[...])))
    x1_parts = []
    for rs, x, a, b in gated:
        x1 = x + a * _sigmoid(b)
        r.x1[rs, :] = x1
        x1_parts.append(x1)
    _route_tile(final, cap, r, x1_parts)


def _row_spec(tm, width):
    return pl.BlockSpec((tm, width), lambda i: (i, 0))


def _const_spec(shape):
    return pl.BlockSpec(shape, lambda i: (0,) * len(shape))


def _mixer_call(body, name, lead_args, lead_specs, x2d, router, triu, cnt, xs_prev, final, cap):
    t = x2d.shape[0]
    tm = TOKEN_TILE
    n_tiles = t // tm
    gn, wrh, wrl, br = router
    cntc, cntr = cnt
    has_prev = xs_prev is not None
    in_specs = lead_specs + [
        _const_spec((1, D_MODEL)), _const_spec((D_MODEL, LANES)), _const_spec((D_MODEL, LANES)),
        _const_spec((1, LANES)), _const_spec((tm, tm)), _const_spec((LANES, LANES)), _const_spec((8, LANES)),
    ]
    args = list(lead_args) + [gn, wrh, wrl, br, triu, cntc, cntr]
    aliases = {}
    if has_prev:
        in_specs.append(pl.BlockSpec(memory_space=pl.ANY))
        args.append(xs_prev)
        aliases = {len(args) - 1: 4}
    x1, dest, cntc2, cntr2, xs, maps = pl.pallas_call(
        functools.partial(body, has_prev, final, cap),
        grid=(n_tiles,),
        in_specs=in_specs,
        out_specs=[
            _row_spec(tm, D_MODEL),
            pl.BlockSpec((1, 1, tm), lambda i: (i, 0, 0)),
            _const_spec((LANES, LANES)), _const_spec((8, LANES)),
            pl.BlockSpec(memory_space=pl.ANY),
            pl.BlockSpec(memory_space=pltpu.SMEM),
        ],
        out_shape=[
            jax.ShapeDtypeStruct((t, D_MODEL), F32),
            jax.ShapeDtypeStruct((n_tiles, 1, tm), I32),
            jax.ShapeDtypeStruct((LANES, LANES), F32),
            jax.ShapeDtypeStruct((8, LANES), F32),
            jax.ShapeDtypeStruct((N_EXPERT_GROUPS * cap, PAY_W), F32),
            jax.ShapeDtypeStruct((4, _expert_steps(cap)), I32),
        ],
        scratch_shapes=[
            pltpu.VMEM((2, tm, PAY_W), F32),
            pltpu.VMEM((MOE_TILE, PAY_W), F32),
            pltpu.VMEM((1, tm), I32),
            pltpu.SMEM((1, tm), I32),
            pltpu.VMEM((LANES, LANES), F32),
            pltpu.VMEM((8, LANES), F32),
            pltpu.SemaphoreType.DMA((2,)),
        ],
        input_output_aliases=aliases,
        compiler_params=_cparams(("arbitrary",)),
        name=name,
    )(*args)
    return x1, dest, (cntc2, cntr2), xs, maps


def _oproj_call(o2d, x2d, wo_bf, bo, router, triu, cnt, xs_prev, final, cap):
    tm = TOKEN_TILE
    lead_specs = [_row_spec(tm, Q_DIM), _row_spec(tm, D_MODEL), _const_spec((Q_DIM, D_MODEL)),
                  _const_spec((1, D_MODEL))]
    return _mixer_call(_oproj_kernel, "oproj_route", [o2d, x2d, wo_bf, bo], lead_specs, x2d,
                       router, triu, cnt, xs_prev, final, cap)


def _glu_call(y2d, x2d, gm, d, wa_bf, wb_bf, router, triu, cnt, xs_prev, final, cap):
    tm = TOKEN_TILE
    lead_specs = [_row_spec(tm, D_MODEL), _row_spec(tm, D_MODEL), _const_spec((1, D_MODEL)),
                  _const_spec((1, D_MODEL)), _const_spec((D_MODEL, D_MODEL)),
                  _const_spec((D_MODEL, D_MODEL))]
    return _mixer_call(_glu_kernel, "glu_route", [y2d, x2d, gm, d, wa_bf, wb_bf], lead_specs, x2d,
                       router, triu, cnt, xs_prev, final, cap)


def _expert_kernel(maps_ref, xs_ref, wg_ref, wu_ref, wd_ref, ys_ref, wgu, wdn, hid):
    j = pl.program_id(0)
    f = EXPERT_FF
    changed = (j == 0) | (maps_ref[2, j] != maps_ref[2, jnp.maximum(j - 1, 0)])
    valid = maps_ref[3, j]

    @pl.when(changed)
    def _():
        for e in range(EXPERTS_PER_GROUP):
            wgu[e, :, :f] = wg_ref[e].astype(BF16)
            wgu[e, :, f:] = wu_ref[e].astype(BF16)
            wdn[e * f:(e + 1) * f, :] = wd_ref[e].astype(BF16)

    @pl.when(valid == 1)
    def _():
        x = xs_ref[:, :PAY_X].astype(BF16)
        wts = xs_ref[:, PAY_X:]
        for e in range(EXPERTS_PER_GROUP):
            gu = _dot(x, wgu[e])
            g, u = gu[:, :f], gu[:, f:]
            hid[:, e * f:(e + 1) * f] = ((g * _sigmoid(g)) * u * wts[:, e:e + 1]).astype(BF16)
        ys_ref[...] = _dot(hid[...], wdn[...])

    @pl.when(valid == 0)
    def _():
        ys_ref[...] = jnp.zeros_like(ys_ref)


def _expert_call(maps, xs, w_gate, w_up, w_down, layer, cap):
    n_steps = maps.shape[1]
    e, f = EXPERTS_PER_GROUP, EXPERT_FF
    w_gate = w_gate.reshape(-1, D_MODEL, f)
    w_up = w_up.reshape(-1, D_MODEL, f)
    w_down = w_down.reshape(-1, f, D_MODEL)
    w_blk = lambda j, m: (m[2, j] + layer * N_EXPERT_GROUPS, 0, 0)
    grid_spec = pltpu.PrefetchScalarGridSpec(
        num_scalar_prefetch=1,
        grid=(n_steps,),
        in_specs=[
            pl.BlockSpec((MOE_TILE, PAY_W), lambda j, m: (m[0, j], 0)),
            pl.BlockSpec((e, D_MODEL, f), w_blk),
            pl.BlockSpec((e, D_MODEL, f), w_blk),
            pl.BlockSpec((e, f, D_MODEL), w_blk),
        ],
        out_specs=pl.BlockSpec((MOE_TILE, D_MODEL), lambda j, m: (m[1, j], 0)),
        scratch_shapes=[
            pltpu.VMEM((e, D_MODEL, 2 * f), BF16),
            pltpu.VMEM((e * f, D_MODEL), BF16),
            pltpu.VMEM((MOE_TILE, e * f), BF16),
        ],
    )
    return pl.pallas_call(
        _expert_kernel,
        grid_spec=grid_spec,
        out_shape=jax.ShapeDtypeStruct((N_EXPERT_GROUPS * cap + MOE_TILE, D_MODEL), F32),
        compiler_params=_cparams(("arbitrary",)),
        name="moe_experts",
    )(maps, xs, w_gate, w_up, w_down)


def _combine_kernel(emit_x, dest_ref, x_ref, gnext_ref, ys_hbm, *rest):
    outs, (ybuf, sem) = rest[:-2], rest[-2:]
    i = pl.program_id(0)
    n_tiles = pl.num_programs(0) - 1
    slot = i % 2
    tm = x_ref.shape[0]

    for s in range(2):
        @pl.when((i < n_tiles) & (slot == s))
        def _():
            for r in range(tm):
                pltpu.make_async_copy(ys_hbm.at[pl.ds(dest_ref[0, 0, r], 1), :],
                                      ybuf.at[s, pl.ds(r, 1), :], sem.at[s]).start()

    @pl.when(i >= 1)
    def _():
        prev = 1 - slot
        pltpu.make_async_copy(ys_hbm.at[pl.ds(0, tm), :], ybuf.at[prev], sem.at[prev]).wait()
        x2 = x_ref[...] + ybuf[prev]
        normed = _rms(x2, gnext_ref[...])
        if emit_x:
            outs[0][...] = x2
            outs[1][...] = normed
        else:
            outs[0][...] = normed


def _combine_call(dest, x2d, ys, gnext, emit_x):
    t = x2d.shape[0]
    tm = COMBINE_TILE
    n_tiles = t // tm
    dest = dest.reshape(n_tiles, 1, tm)
    n_out = 2 if emit_x else 1
    done = lambda i: (jnp.maximum(i - 1, 0), 0)
    return pl.pallas_call(
        functools.partial(_combine_kernel, emit_x),
        grid=(n_tiles + 1,),
        in_specs=[
            pl.BlockSpec((1, 1, tm), lambda i: (jnp.minimum(i, n_tiles - 1), 0, 0),
                         memory_space=pltpu.SMEM),
            pl.BlockSpec((tm, D_MODEL), done), _const_spec((1, D_MODEL)),
            pl.BlockSpec(memory_space=pl.ANY),
        ],
        out_specs=[pl.BlockSpec((tm, D_MODEL), done)] * n_out,
        out_shape=[jax.ShapeDtypeStruct((t, D_MODEL), F32)] * n_out,
        scratch_shapes=[pltpu.VMEM((2, tm, D_MODEL), F32), pltpu.SemaphoreType.DMA((2,))],
        compiler_params=_cparams(("arbitrary",)),
        name="moe_combine",
    )(dest, x2d, gnext, ys)


def _s5_state_in(u_ref_val, wre_ref, wim_ref, store_re, store_im, pair_w):
    for m in range(S5_GB // 2):
        up = u_ref_val(m * pair_w, pair_w)
        store_re(m, _dot(up, wre_ref[m]))
        store_im(m, _dot(up, wim_ref[m]))


def _s5_outputs(u_ref_val, hre, him, m_ref, gre_ref, gim_ref, y_store, pair_w):
    gw = pair_w // 2
    for m in range(S5_GB // 2):
        hr = hre(m).astype(BF16)
        hi = him(m).astype(BF16)
        y = _dot(hr, gre_ref[m]) + _dot(hi, gim_ref[m])
        y0 = y[:, :gw] + _dot(u_ref_val(m * pair_w, gw), m_ref[2 * m])
        y1 = y[:, gw:] + _dot(u_ref_val(m * pair_w + gw, gw), m_ref[2 * m + 1])
        y_store(m * pair_w, gw, y0)
        y_store(m * pair_w + gw, gw, y1)


def _s5_flatten(load_rows, q, n, ut, uflat):
    gc = SSM_GROUP_CH
    qc = q * gc
    for s in range(q):
        ut[:, s * gc:(s + 1) * gc, :] = load_rows(s).astype(BF16).T.reshape(S5_GB, gc, n)
    for g in range(S5_GB):
        uflat[:, g * qc:(g + 1) * qc] = ut[g].T


def _s5_unflatten(yflat, yt, store_rows, q, n):
    gc = SSM_GROUP_CH
    qc = q * gc
    for g in range(S5_GB):
        yt[g] = yflat[:, g * qc:(g + 1) * qc].T
    for t in range(q):
        store_rows(t, yt[:, t * gc:(t + 1) * gc, :].reshape(S5_GB * gc, n).T)


def _s5_prompt_kernel(u_ref, wre_ref, wim_ref, m_ref, gre_ref, gim_ref, aqr_ref, aqi_ref,
                      y_ref, her_ref, hei_ref, ut, uflat, sre, sim, hre, him, yflat, yt):
    pair_w = 2 * S5_CHUNK * SSM_GROUP_CH
    n_chunks = u_ref.shape[1] // S5_CHUNK
    n_pairs = S5_GB // 2
    seq_rows = lambda b: pl.ds(b, n_chunks, stride=S5_SEQS)
    for b in range(S5_SEQS):
        _s5_flatten(lambda s: u_ref[b, pl.ds(s, n_chunks, stride=S5_CHUNK), :],
                    S5_CHUNK, n_chunks, ut, uflat.at[b])
        u_val = lambda off, w: uflat[b, :, off:off + w]

        def store_re(m, val):
            sre[m, seq_rows(b), :] = val

        def store_im(m, val):
            sim[m, seq_rows(b), :] = val

        _s5_state_in(u_val, wre_ref, wim_ref, store_re, store_im, pair_w)
    ar = [aqr_ref[:, m * LANES:(m + 1) * LANES] for m in range(n_pairs)]
    ai = [aqi_ref[:, m * LANES:(m + 1) * LANES] for m in range(n_pairs)]

    def step(n, carry):
        rows = pl.ds(pl.multiple_of(n * S5_SEQS, S5_SEQS), S5_SEQS)
        out = []
        for m in range(n_pairs):
            hr, hi = carry[2 * m], carry[2 * m + 1]
            hre[m, rows, :] = hr
            him[m, rows, :] = hi
            out.append(ar[m] * hr - ai[m] * hi + sre[m, rows, :])
            out.append(ar[m] * hi + ai[m] * hr + sim[m, rows, :])
        return tuple(out)

    zero = jnp.zeros((S5_SEQS, LANES), F32)
    last = lax.fori_loop(0, n_chunks, step, (zero,) * (2 * n_pairs))
    for b in range(S5_SEQS):
        for m in range(n_pairs):
            her_ref[b, :, m * LANES:(m + 1) * LANES] = last[2 * m][b:b + 1]
            hei_ref[b, :, m * LANES:(m + 1) * LANES] = last[2 * m + 1][b:b + 1]

    def y_store(off, w, val):
        yflat[:, off:off + w] = val

    for b in range(S5_SEQS):
        u_val = lambda off, w: uflat[b, :, off:off + w]
        _s5_outputs(u_val, lambda m: hre[m, seq_rows(b), :], lambda m: him[m, seq_rows(b), :],
                    m_ref, gre_ref, gim_ref, y_store, pair_w)

        def store_rows(t, val):
            y_ref[b, pl.ds(t, n_chunks, stride=S5_CHUNK), :] = val

        _s5_unflatten(yflat, yt, store_rows, S5_CHUNK, n_chunks)


def _s5_sample_kernel(n_new, u_ref, h0r_ref, h0i_ref, wre_ref, wim_ref, m_ref, gre_ref, gim_ref,
                      aqr_ref, aqi_ref, y_ref, hnr_ref, hni_ref, ut, uflat, sre, sim, yflat, yt):
    pair_w = 2 * n_new * SSM_GROUP_CH
    seqs = h0r_ref.shape[0]
    _s5_flatten(lambda s: u_ref[pl.ds(s, seqs, stride=n_new), :], n_new, seqs, ut, uflat)
    u_val = lambda off, w: uflat[:, off:off + w]

    def store_re(m, val):
        sre[:, m * LANES:(m + 1) * LANES] = val

    def store_im(m, val):
        sim[:, m * LANES:(m + 1) * LANES] = val

    _s5_state_in(u_val, wre_ref, wim_ref, store_re, store_im, pair_w)
    ar = aqr_ref[...]
    ai = aqi_ref[...]
    h0r = h0r_ref[...]
    h0i = h0i_ref[...]
    hnr_ref[...] = ar * h0r - ai * h0i + sre[...]
    hni_ref[...] = ar * h0i + ai * h0r + sim[...]

    def y_store(off, w, val):
        yflat[:, off:off + w] = val

    _s5_outputs(u_val, lambda m: h0r_ref[:, m * LANES:(m + 1) * LANES],
                lambda m: h0i_ref[:, m * LANES:(m + 1) * LANES],
                m_ref, gre_ref, gim_ref, y_store, pair_w)

    def store_rows(t, val):
        y_ref[pl.ds(t, seqs, stride=n_new), :] = val

    _s5_unflatten(yflat, yt, store_rows, n_new, seqs)


def _s5_weight_specs(q, idx):
    qc = q * SSM_GROUP_CH
    np_ = S5_GB // 2
    st = S5_GB * SSM_STATE
    return [
        pl.BlockSpec((np_, 2 * qc, LANES), lambda *a: (idx(*a), 0, 0)),
        pl.BlockSpec((np_, 2 * qc, LANES), lambda *a: (idx(*a), 0, 0)),
        pl.BlockSpec((S5_GB, qc, qc), lambda *a: (idx(*a), 0, 0)),
        pl.BlockSpec((np_, LANES, 2 * qc), lambda *a: (idx(*a), 0, 0)),
        pl.BlockSpec((np_, LANES, 2 * qc), lambda *a: (idx(*a), 0, 0)),
        pl.BlockSpec((1, st), lambda *a: (0, idx(*a))),
        pl.BlockSpec((1, st), lambda *a: (0, idx(*a))),
    ]


def _s5_prompt_call(u, w):
    b, seq, _ = u.shape
    n_chunks = seq // S5_CHUNK
    gbl = S5_GB * SSM_GROUP_CH
    qc = S5_CHUNK * SSM_GROUP_CH
    st = S5_GB * SSM_STATE
    n_gb = SSM_GROUPS // S5_GB
    gb_of = lambda g, i: g
    tok = pl.BlockSpec((S5_SEQS, seq, gbl), lambda g, i: (i, 0, g))
    slab = pltpu.VMEM((S5_GB // 2, n_chunks * S5_SEQS, LANES), F32)
    return pl.pallas_call(
        _s5_prompt_kernel,
        grid=(n_gb, b // S5_SEQS),
        in_specs=[tok] + _s5_weight_specs(S5_CHUNK, gb_of),
        out_specs=[
            tok,
            pl.BlockSpec((S5_SEQS, 1, st), lambda g, i: (i, 0, g)),
            pl.BlockSpec((S5_SEQS, 1, st), lambda g, i: (i, 0, g)),
        ],
        out_shape=[
            jax.ShapeDtypeStruct((b, seq, D_MODEL), F32),
            jax.ShapeDtypeStruct((b, 1, SSM_GROUPS * SSM_STATE), F32),
            jax.ShapeDtypeStruct((b, 1, SSM_GROUPS * SSM_STATE), F32),
        ],
        scratch_shapes=[
            pltpu.VMEM((S5_GB, qc, n_chunks), BF16),
            pltpu.VMEM((S5_SEQS, n_chunks, S5_GB * qc), BF16),
            slab, slab, slab, slab,
            pltpu.VMEM((n_chunks, S5_GB * qc), F32),
            pltpu.VMEM((S5_GB, qc, n_chunks), F32),
        ],
        compiler_params=_cparams(("parallel", "parallel")),
        name="s5_prompt",
    )(u, *w)


def _s5_sample_call(u2d, h0r, h0i, w, n_new):
    t = u2d.shape[0]
    b = t // n_new
    gbl = S5_GB * SSM_GROUP_CH
    qc = n_new * SSM_GROUP_CH
    st = S5_GB * SSM_STATE
    n_gb = SSM_GROUPS // S5_GB
    gb_of = lambda g: g
    state = pl.BlockSpec((b, st), lambda g: (0, g))
    tok = pl.BlockSpec((t, gbl), lambda g: (0, g))
    return pl.pallas_call(
        functools.partial(_s5_sample_kernel, n_new),
        grid=(n_gb,),
        in_specs=[tok, state, state] + _s5_weight_specs(n_new, gb_of),
        out_specs=[tok, state, state],
        out_shape=[
            jax.ShapeDtypeStruct((t, D_MODEL), F32),
            jax.ShapeDtypeStruct((b, SSM_GROUPS * SSM_STATE), F32),
            jax.ShapeDtypeStruct((b, SSM_GROUPS * SSM_STATE), F32),
        ],
        scratch_shapes=[
            pltpu.VMEM((S5_GB, qc, b), BF16),
            pltpu.VMEM((b, S5_GB * qc), BF16),
            pltpu.VMEM((b, st), F32), pltpu.VMEM((b, st), F32),
            pltpu.VMEM((b, S5_GB * qc), F32),
            pltpu.VMEM((S5_GB, qc, b), F32),
        ],
        compiler_params=_cparams(("parallel",)),
        name="s5_sample",
    )(u2d, h0r, h0i, *w)


def _s5_discretize(a_re, a_im, log_dt, b_re, b_im):
    delta = jnp.exp(log_dt.astype(F32))[:, None]
    lr, li = a_re.astype(F32), a_im.astype(F32)
    mag = jnp.exp(delta * lr)
    abar_r = mag * jnp.cos(delta * li)
    abar_i = mag * jnp.sin(delta * li)
    nr, ni = abar_r - 1.0, abar_i
    den = lr * lr + li * li
    coef_r = ((nr * lr + ni * li) / den)[..., None]
    coef_i = ((ni * lr - nr * li) / den)[..., None]
    br, bi = b_re.astype(F32), b_im.astype(F32)
    return delta * lr, delta * li, coef_r * br - coef_i * bi, coef_r * bi + coef_i * br


def _split2(x):
    x1 = x.astype(BF16)
    return x1, (x - x1.astype(F32)).astype(BF16)


def _s5_prep_kernel(qs, bbr_ref, bbi_ref, ctr_ref, cti_ref, ppr_ref, ppi_ref, pnr_ref, pni_ref,
                    *out_refs):
    c = SSM_GROUP_CH
    qc = max(qs) * c
    p2 = 2 * SSM_STATE
    col = lax.broadcasted_iota(I32, (c, qc), 1)
    row = lax.broadcasted_iota(I32, (c, qc), 0)
    by_ch = (col % c == row).astype(BF16)
    by_pos = (col // c == row).astype(BF16)

    def spread(x, sel):
        return sum(_dot(part, sel) for part in _split2(x))

    def dot_f32(a, b):
        a1, a2 = _split2(a)
        b1, b2 = _split2(b)
        return _dot(a1, b1) + _dot(a1, b2) + _dot(a2, b1)

    bc_r, bc_i = spread(bbr_ref[0], by_ch), spread(bbi_ref[0], by_ch)
    cc_r, cc_i = spread(ctr_ref[0], by_ch), spread(cti_ref[0], by_ch)
    pp_r, pp_i = spread(ppr_ref[0], by_pos), spread(ppi_ref[0], by_pos)
    pn_r, pn_i = spread(pnr_ref[0], by_pos), spread(pni_ref[0], by_pos)
    r_r = cc_r * pp_r - cc_i * pp_i
    r_i = cc_r * pp_i + cc_i * pp_r
    l_r = bc_r * pn_r - bc_i * pn_i
    l_i = bc_r * pn_i + bc_i * pn_r
    lt_r, lt_i = l_r.T, l_i.T
    grp_col = lax.broadcasted_iota(I32, (qc, p2), 1) // SSM_STATE
    causal =(lax.broadcasted_iota(I32, (qc, qc), 1) // c) >= (lax.broadcasted_iota(I32, (qc, qc), 0) // c)
    kers = [jnp.where(causal, dot_f32(jnp.where(grp_col == gl, lt_r, 0.0), r_r)
                      - dot_f32(jnp.where(grp_col == gl, lt_i, 0.0), r_i), 0.0) for gl in range(2)]
    for n, q in enumerate(qs):
        wre_ref, wim_ref, m_ref, gre_ref, gim_ref = out_refs[5 * n:5 * n + 5]
        w = q * c
        aq_r = ppr_ref[0][:, q - 1:q]
        aq_i = ppi_ref[0][:, q - 1:q]
        w_r = (l_r[:, :w] * aq_r - l_i[:, :w] * aq_i).T
        w_i = (l_r[:, :w] * aq_i + l_i[:, :w] * aq_r).T
        g_re, g_im = [], []
        col_grp = lax.broadcasted_iota(I32, (w, p2), 1) // SSM_STATE
        row_grp = lax.broadcasted_iota(I32, (p2, w), 0) // SSM_STATE
        for gl in range(2):
            wre_ref[0, gl * w:(gl + 1) * w, :] = jnp.where(col_grp == gl, w_r, 0.0).astype(BF16)
            wim_ref[0, gl * w:(gl + 1) * w, :] = jnp.where(col_grp == gl, w_i, 0.0).astype(BF16)
            m_ref[gl] = kers[gl][:w, :w].astype(BF16)
            g_re.append(jnp.where(row_grp == gl, r_r[:, :w], 0.0))
            g_im.append(jnp.where(row_grp == gl, -r_i[:, :w], 0.0))
        gre_ref[0] = jnp.concatenate(g_re, axis=1).astype(BF16)
        gim_ref[0] = jnp.concatenate(g_im, axis=1).astype(BF16)


def _s5_chunk_weights(tables, qs):
    n_pairs = SSM_GROUPS // 2
    p2 = 2 * SSM_STATE
    tab = pl.BlockSpec((1, p2, SSM_GROUP_CH), lambda m: (m, 0, 0))
    out_specs, out_shape = [], []
    for q in qs:
        qc = q * SSM_GROUP_CH
        out_specs += [pl.BlockSpec((1, 2 * qc, p2), lambda m: (m, 0, 0)),
                      pl.BlockSpec((1, 2 * qc, p2), lambda m: (m, 0, 0)),
                      pl.BlockSpec((2, qc, qc), lambda m: (m, 0, 0)),
                      pl.BlockSpec((1, p2, 2 * qc), lambda m: (m, 0, 0)),
                      pl.BlockSpec((1, p2, 2 * qc), lambda m: (m, 0, 0))]
        out_shape += [jax.ShapeDtypeStruct((n_pairs, 2 * qc, p2), BF16),
                      jax.ShapeDtypeStruct((n_pairs, 2 * qc, p2), BF16),
                      jax.ShapeDtypeStruct((SSM_GROUPS, qc, qc), BF16),
                      jax.ShapeDtypeStruct((n_pairs, p2, 2 * qc), BF16),
                      jax.ShapeDtypeStruct((n_pairs, p2, 2 * qc), BF16)]
    outs = pl.pallas_call(
        functools.partial(_s5_prep_kernel, tuple(qs)),
        grid=(n_pairs,),
        in_specs=[tab] * 8,
        out_specs=out_specs, out_shape=out_shape,
        compiler_params=_cparams(("parallel",)),
        name="s5_operators",
    )(*tables)
    return [tuple(outs[5 * n:5 * n + 5]) for n in range(len(qs))]


def _s5_tables(log_mag, phase, bbar_r, bbar_i, c_re, c_im, q_max):
    k = jnp.arange(1, q_max + 1, dtype=F32)[None, None, :]
    mag = jnp.exp(log_mag[:, :, None] * k)
    cos, sin = jnp.cos(phase[:, :, None] * k), jnp.sin(phase[:, :, None] * k)
    tabs = (bbar_r, bbar_i, c_re.astype(F32).transpose(0, 2, 1), c_im.astype(F32).transpose(0, 2, 1),
            mag * cos, mag * sin, cos / mag, -sin / mag)
    return tuple(t.reshape(SSM_GROUPS // 2, 2 * SSM_STATE, SSM_GROUP_CH) for t in tabs)


def _rope_tables(pos):
    half = HEAD_DIM // 2
    inv = 1.0 / (ROPE_THETA ** (jnp.arange(half, dtype=F32) * (2.0 / HEAD_DIM)))
    ang = pos.astype(F32)[:, None] * inv[None, :]
    cos, sin = jnp.cos(ang), jnp.sin(ang)
    return jnp.tile(cos, (1, 4)), jnp.concatenate([-sin, sin, -sin, sin], axis=1)


def _router_weights(gain, w_rg, b_rg, w_re, b_re):
    n_layers = gain.shape[0]
    pad = LANES - N_EXPERTS - N_EXPERT_GROUPS
    w = jnp.concatenate([w_re, w_rg, jnp.zeros((n_layers, D_MODEL, pad), F32)], axis=2)
    b = jnp.concatenate([b_re, b_rg, jnp.zeros((n_layers, pad), F32)], axis=1)
    wh = w.astype(BF16)
    wl = (w - wh.astype(F32)).astype(BF16)
    return [(gain[l].reshape(1, -1).astype(F32), wh[l], wl[l], b[l].reshape(1, LANES))
            for l in range(n_layers)]


def kernel(x_prompt, x_sample, cache_k, cache_v, state_ssm_re, state_ssm_im, norm_mix, norm_ffn, norm_final, attn_w_qkv, attn_b_qkv, attn_w_o, attn_b_o, attn_sinks, ssm_a_re, ssm_a_im, ssm_log_dt, ssm_b_re, ssm_b_im, ssm_c_re, ssm_c_im, ssm_d, ssm_w_glu_a, ssm_w_glu_b, moe_w_router_group, moe_b_router_group, moe_w_router_expert, moe_b_router_expert, moe_w_gate, moe_w_up, moe_w_down):
    bsz, seq, _ = x_prompt.shape
    dbs, n_new, _ = x_sample.shape
    rows = cache_k.shape[2]
    tp, ts = bsz * seq, dbs * n_new
    cap = _moe_cap(tp + ts)
    xp = x_prompt.reshape(tp, D_MODEL)
    xs = x_sample.reshape(ts, D_MODEL)

    row1 = lambda v: v.reshape(1, -1).astype(F32)
    routers = _router_weights(norm_ffn, moe_w_router_group, moe_b_router_group,
                              moe_w_router_expert, moe_b_router_expert)
    triu = jnp.triu(jnp.ones((TOKEN_TILE, TOKEN_TILE), F32), 1).astype(BF16)
    cnt0 = (jnp.zeros((LANES, LANES), F32), jnp.zeros((8, LANES), F32))

    slots = jnp.asarray(HEAD_SLOTS, dtype=I32)

    def q_slots(w):
        lead = w.shape[:-1]
        qh = w[..., :Q_DIM].reshape(*lead, N_HEADS, HEAD_DIM)[..., slots, :].reshape(*lead, Q_DIM)
        return jnp.concatenate([qh, w[..., Q_DIM:]], axis=-1)

    wqkv = q_slots(attn_w_qkv[0]).astype(BF16)
    bqkv = row1(q_slots(attn_b_qkv[0]))
    wo = attn_w_o[0].reshape(N_HEADS, HEAD_DIM, D_MODEL)[slots].reshape(Q_DIM, D_MODEL).astype(BF16)
    bo = row1(attn_b_o[0])
    sinks = attn_sinks[0].astype(F32)[slots]
    g_mix0, g_mix1 = row1(norm_mix[0]), row1(norm_mix[1])
    pos = jnp.concatenate([jnp.arange(seq, dtype=I32),
                           jnp.tile(PAST_LEN + jnp.arange(n_new, dtype=I32), dbs)])
    cos_t, sin_t = _rope_tables(pos)

    qp, kp, vp = _qkv_call(xp, g_mix0, wqkv, bqkv, cos_t, sin_t, (0, seq), BF16)
    qs, ks, vs = _qkv_call(xs, g_mix0, wqkv, bqkv, cos_t, sin_t, (seq, ts), F32)
    op = _attn_prompt_call(jnp.repeat(sinks, BLOCK // ATTN_PARTS).reshape(1, -1),
                           qp.reshape(bsz, seq, Q_DIM),
                           kp.reshape(bsz, seq, KV_DIM),
                           vp.reshape(bsz, seq, KV_DIM))
    os_, nks, nvs = _attn_sample_call(jnp.repeat(sinks, n_new).reshape(1, -1), qs, ks, vs,
                                      cache_k[0].reshape(dbs, rows, KV_DIM),
                                      cache_v[0].reshape(dbs, rows, KV_DIM), n_new, 8)
    xp1, dest_p, cnt, rows_x, _ = _oproj_call(op.reshape(tp, Q_DIM), xp, wo, bo, routers[0], triu,
                                              cnt0, None, False, cap)
    xs1, dest_s, cnt, rows_x, steps = _oproj_call(os_, xs, wo, bo, routers[0], triu, cnt, rows_x,
                                                  True, cap)
    rows_y = _expert_call(steps, rows_x, moe_w_gate, moe_w_up, moe_w_down, 0, cap)
    xp2, up = _combine_call(dest_p, xp1, rows_y, g_mix1, True)
    xs2, us = _combine_call(dest_s, xs1, rows_y, g_mix1, True)

    disc = _s5_discretize(ssm_a_re[0], ssm_a_im[0], ssm_log_dt[0], ssm_b_re[0], ssm_b_im[0])
    tables = _s5_tables(*disc, ssm_c_re[0], ssm_c_im[0], S5_CHUNK)
    a_pow = lambda q: [tables[i][:, :, q - 1].reshape(1, -1) for i in (4, 5)]
    ops_p, ops_s = _s5_chunk_weights(tables, (S5_CHUNK, n_new))
    w_p = (*ops_p, *a_pow(S5_CHUNK))
    w_s = (*ops_s, *a_pow(n_new))
    y_p, hpr, hpi = _s5_prompt_call(up.reshape(bsz, seq, D_MODEL), w_p)
    y_p = y_p.reshape(tp, D_MODEL)
    h0r = state_ssm_re[0].reshape(dbs, -1).astype(F32)
    h0i = state_ssm_im[0].reshape(dbs, -1).astype(F32)
    y_s, hsr, hsi = _s5_sample_call(us, h0r, h0i, w_s, n_new)

    wa, wb = ssm_w_glu_a[0].astype(BF16), ssm_w_glu_b[0].astype(BF16)
    d_row = row1(ssm_d[0])
    xp3, dest_p, cnt, rows_x, _ = _glu_call(y_p, xp2, g_mix1, d_row, wa, wb, routers[1], triu,
                                            cnt0, None, False, cap)
    xs3, dest_s, cnt, rows_x, steps = _glu_call(y_s, xs2, g_mix1, d_row, wa, wb, routers[1], triu,
                                                cnt, rows_x, True, cap)
    rows_y = _expert_call(steps, rows_x, moe_w_gate, moe_w_up, moe_w_down, 1, cap)
    g_fin = row1(norm_final)
    (yp,) = _combine_call(dest_p, xp3, rows_y, g_fin, False)
    (ys,) = _combine_call(dest_s, xs3, rows_y, g_fin, False)

    kv5 = lambda a, n: a.reshape(1, n, rows, KV_HEADS, HEAD_DIM)
    st4 = lambda a, n: a.reshape(1, n, SSM_GROUPS, SSM_STATE)
    k_last = kp.reshape(bsz, seq, KV_DIM)[:, seq - WINDOW:]
    v_last = vp.reshape(bsz, seq, KV_DIM)[:, seq - WINDOW:]
    return (yp.reshape(bsz, seq, D_MODEL), ys.reshape(dbs, n_new, D_MODEL),
            k_last.reshape(1, bsz, WINDOW, KV_HEADS, HEAD_DIM), kv5(nks, dbs),
            v_last.reshape(1, bsz, WINDOW, KV_HEADS, HEAD_DIM), kv5(nvs, dbs),
            st4(hpr, bsz), st4(hsr, dbs), st4(hpi, bsz), st4(hsi, dbs))
```

```python
import collections
import functools
import math

import jax
import jax.numpy as jnp
from jax import lax
from jax.experimental import pallas as pl
from jax.experimental.pallas import tpu as pltpu

F32 = jnp.float32
BF16 = jnp.bfloat16
I32 = jnp.int32

D_MODEL = 1024
N_HEADS = 16
KV_HEADS = 4
HEAD_DIM = 64
Q_DIM = N_HEADS * HEAD_DIM
KV_DIM = KV_HEADS * HEAD_DIM
QKV_DIM = Q_DIM + 2 * KV_DIM
WINDOW = 128
BLOCK = 128
ROPE_THETA = 10000.0
PAST_LEN = 16384
SSM_GROUP_CH = 16
SSM_GROUPS = D_MODEL // SSM_GROUP_CH
SSM_STATE = 64
N_EXPERT_GROUPS = 4
EXPERTS_PER_GROUP = 8
N_EXPERTS = N_EXPERT_GROUPS * EXPERTS_PER_GROUP
EXPERT_FF = 128
NORM_EPS = 1e-5

LANES = 128
VMEM_LIMIT = 56 * 1024 * 1024
S5_CHUNK = 16
S5_GB = 8
S5_SEQS = 4
TOKEN_TILE = 1024
MOE_TILE = 512
ROUTE_PARTS = 2
COMBINE_TILE = 512
ATTN_PARTS = 4
ATTN_BLOCKS = 4
PAY_X = D_MODEL
PAY_W = PAY_X + LANES


def _cparams(sem):
    return pltpu.CompilerParams(dimension_semantics=sem, vmem_limit_bytes=VMEM_LIMIT)


def _rms(x, g):
    return x * lax.rsqrt(jnp.mean(x * x, axis=-1, keepdims=True) + NORM_EPS) * g


def _dot(a, b):
    return jnp.dot(a, b, preferred_element_type=F32)


def _dot_nt(a, b):
    return lax.dot_general(a, b, (((1,), (1,)), ((), ())), preferred_element_type=F32)


def _sigmoid(x):
    return 1.0 / (1.0 + jnp.exp(-x))


def _qkv_kernel(x_ref, g_ref, w_ref, b_ref, cos_ref, sin_ref, q_ref, k_ref, v_ref):
    xn = _rms(x_ref[...], g_ref[...])
    qkv = _dot(xn.astype(BF16), w_ref[...]) + b_ref[...]
    cos = cos_ref[...]
    sin = sin_ref[...]
    lane = lax.broadcasted_iota(I32, cos.shape, 1)
    first_half = (lane % HEAD_DIM) < (HEAD_DIM // 2)
    n_rot = (Q_DIM + KV_DIM) // LANES
    for c in range(n_rot):
        blk = qkv[:, c * LANES:(c + 1) * LANES]
        partner = jnp.where(first_half,
                            pltpu.roll(blk, LANES - HEAD_DIM // 2, 1),
                            pltpu.roll(blk, HEAD_DIM // 2, 1))
        rot = blk * cos + partner * sin
        if c < Q_DIM // LANES:
            q_ref[:, c * LANES:(c + 1) * LANES] = (rot * (HEAD_DIM ** -0.5)).astype(q_ref.dtype)
        else:
            k_ref[:, c * LANES - Q_DIM:(c + 1) * LANES - Q_DIM] = rot
    v_ref[...] = qkv[:, Q_DIM + KV_DIM:]


def _qkv_call(x2d, gain, w_bf, bias, cos_t, sin_t, pos_rows, q_dtype):
    t = x2d.shape[0]
    tm = TOKEN_TILE
    first, n_pos = pos_rows[0] // tm, pos_rows[1] // tm
    return pl.pallas_call(
        _qkv_kernel,
        grid=(t // tm,),
        in_specs=[
            pl.BlockSpec((tm, D_MODEL), lambda i: (i, 0)),
            pl.BlockSpec((1, D_MODEL), lambda i: (0, 0)),
            pl.BlockSpec((D_MODEL, QKV_DIM), lambda i: (0, 0)),
            pl.BlockSpec((1, QKV_DIM), lambda i: (0, 0)),
            pl.BlockSpec((tm, LANES), lambda i: (first + i % n_pos, 0)),
            pl.BlockSpec((tm, LANES), lambda i: (first + i % n_pos, 0)),
        ],
        out_specs=[
            pl.BlockSpec((tm, Q_DIM), lambda i: (i, 0)),
            pl.BlockSpec((tm, KV_DIM), lambda i: (i, 0)),
            pl.BlockSpec((tm, KV_DIM), lambda i: (i, 0)),
        ],
        out_shape=[
            jax.ShapeDtypeStruct((t, Q_DIM), q_dtype),
            jax.ShapeDtypeStruct((t, KV_DIM), F32),
            jax.ShapeDtypeStruct((t, KV_DIM), F32),
        ],
        compiler_params=_cparams(("parallel",)),
        name="qkv_rope",
    )(x2d, gain, w_bf, bias, cos_t, sin_t)


HEAD_SLOTS = tuple(8 * c + 4 * half + i for c in range(2) for i in range(4) for half in range(2))
LOG2E = math.log2(math.e)


def _attn_prompt_kernel(sink_ref, q_ref, *refs):
    nk = ATTN_BLOCKS + 1
    k_refs, v_refs, o_ref = refs[:nk], refs[nk:2 * nk], refs[2 * nk]
    first = pl.program_id(1) * ATTN_BLOCKS
    kb = [r[0].astype(BF16) for r in k_refs]
    vb = [r[0].astype(BF16) for r in v_refs]
    k2s = [jnp.concatenate(kb[i:i + 2], axis=0) for i in range(ATTN_BLOCKS)]
    v2s = [jnp.concatenate(vb[i:i + 2], axis=0) for i in range(ATTN_BLOCKS)]
    sub = BLOCK // ATTN_PARTS
    win = WINDOW + sub
    cols = 2 * BLOCK
    spt = cols // sub
    key = lax.broadcasted_iota(I32, (win, cols), 0)
    qry = lax.broadcasted_iota(I32, (win, cols), 1) % sub
    band = (key >= qry) & (key <= qry + WINDOW)
    masks = [[band & ((first + blk > 0) | (key + h * sub >= BLOCK)) for h in range(ATTN_PARTS)]
             for blk in range(ATTN_BLOCKS)]
    low = lax.broadcasted_iota(I32, (sub, LANES), 1) < HEAD_DIM
    keep_low = low.astype(BF16)
    keep_high = 1 - keep_low
    tasks = [(blk, grp, h) for blk in range(ATTN_BLOCKS) for grp in range(N_HEADS // spt)
             for h in range(ATTN_PARTS)]
    pairs = lambda grp: range(grp * spt // 2, (grp + 1) * spt // 2)
    chunk = lambda grp: slice((grp * spt // 8) * LANES, (grp * spt // 8 + 1) * LANES)
    q_rows = lambda blk, h: slice(blk * BLOCK + h * sub, blk * BLOCK + (h + 1) * sub)

    def scores(blk, grp, h):
        blocks = []
        for pr in pairs(grp):
            qblk = q_ref[0, q_rows(blk, h), pr * LANES:(pr + 1) * LANES]
            blocks += [qblk * keep_low, qblk * keep_high]
        qz = jnp.concatenate(blocks, axis=0)
        s = _dot_nt(k2s[blk][h * sub:h * sub + win, chunk(grp)], qz) * LOG2E
        return jnp.where(masks[blk][h], s, -jnp.inf)

    def weights(blk, grp, h, s):
        sink = sink_ref[:, grp * cols:(grp + 1) * cols] * LOG2E
        m = jnp.maximum(jnp.max(s, axis=0, keepdims=True), sink)
        p = jnp.exp2(s - m)
        denom = jnp.sum(p, axis=0, keepdims=True) + jnp.exp2(sink - m)
        rows = [jnp.zeros((h * sub, cols), F32), p * (1.0 / denom),
                jnp.zeros((2 * BLOCK - win - h * sub, cols), F32)]
        w = jnp.concatenate([r for r in rows if r.shape[0]], axis=0)
        return w.T.astype(BF16)

    def finish(blk, grp, h, w):
        oz = _dot(w, v2s[blk][:, chunk(grp)])
        for j, pr in enumerate(pairs(grp)):
            o_ref[0, q_rows(blk, h), pr * LANES:(pr + 1) * LANES] = jnp.where(
                low, oz[2 * j * sub:(2 * j + 1) * sub], oz[(2 * j + 1) * sub:(2 * j + 2) * sub]
            ).astype(o_ref.dtype)

    all_s = [scores(*t) for t in tasks]
    all_w = [weights(*t, s) for t, s in zip(tasks, all_s)]
    for t, w in zip(tasks, all_w):
        finish(*t, w)


def _attn_prompt_call(sinks, q, k, v):
    b, l, _ = q.shape
    nb = l // (BLOCK * ATTN_BLOCKS)
    kv = [pl.BlockSpec((1, BLOCK, KV_DIM),
                       lambda i, n, d=d: (i, jnp.maximum(n * ATTN_BLOCKS + d, 0), 0))
          for d in range(-1, ATTN_BLOCKS)]
    qo = pl.BlockSpec((1, BLOCK * ATTN_BLOCKS, Q_DIM), lambda i, n: (i, n, 0))
    return pl.pallas_call(
        _attn_prompt_kernel,
        grid=(b, nb),
        in_specs=[pl.BlockSpec((1, N_HEADS * BLOCK // ATTN_PARTS), lambda i, n: (0, 0)), qo] + kv + kv,
        out_specs=qo,
        out_shape=jax.ShapeDtypeStruct((b, l, Q_DIM), BF16),
        compiler_params=_cparams(("parallel", "parallel")),
        name="attn_prompt",
    )(sinks, q, *([k] * len(kv)), *([v] * len(kv)))


def _attn_sample_kernel(n_new, seqs, sink_ref, q_ref, kn_ref, vn_ref, ck_ref, cv_ref,
                        o_ref, nk_ref, nv_ref):
    rows = ck_ref.shape[1]
    keys = 2 * rows
    n_cols = N_HEADS * n_new
    key = lax.broadcasted_iota(I32, (keys, n_cols), 0)
    qry = lax.broadcasted_iota(I32, (keys, n_cols), 1) % n_new
    mask = ((key < rows) & (key >= qry)) | ((key >= rows) & (key - rows <= qry))
    low = lax.broadcasted_iota(I32, (n_new, LANES), 1) < HEAD_DIM
    sink = sink_ref[...]
    pad = jnp.zeros((rows - n_new, KV_DIM), F32)

    def scores(sb):
        r0 = sb * n_new
        k_new = kn_ref[r0:r0 + n_new, :]
        v_new = vn_ref[r0:r0 + n_new, :]
        nk_ref[sb, 0:rows - n_new, :] = ck_ref[sb, n_new:rows, :]
        nk_ref[sb, rows - n_new:rows, :] = k_new
        nv_ref[sb, 0:rows - n_new, :] = cv_ref[sb, n_new:rows, :]
        nv_ref[sb, rows - n_new:rows, :] = v_new
        k_all = jnp.concatenate([ck_ref[sb], k_new, pad], axis=0).astype(BF16)
        v_all = jnp.concatenate([cv_ref[sb], v_new, pad], axis=0).astype(BF16)
        blocks = []
        for slot in range(N_HEADS):
            pair = q_ref[r0:r0 + n_new, (slot // 2) * LANES:(slot // 2 + 1) * LANES]
            blk = jnp.where(low if slot % 2 == 0 else ~low, pair, 0.0)
            zero = jnp.zeros_like(blk)
            blocks.append(jnp.concatenate([blk, zero] if slot < N_HEADS // 2 else [zero, blk], axis=1))
        qz = jnp.concatenate(blocks, axis=0).astype(BF16)
        s = _dot_nt(k_all, qz)
        return jnp.where(mask, s, -jnp.inf), v_all

    def weights(s):
        m = jnp.maximum(jnp.max(s, axis=0, keepdims=True), sink)
        p = jnp.exp(s - m)
        denom = jnp.sum(p, axis=0, keepdims=True) + jnp.exp(sink - m)
        return (p * (1.0 / denom)).T.astype(BF16)

    def finish(sb, w, v_all):
        r0 = sb * n_new
        oz = _dot(w, v_all)
        for pr in range(N_HEADS // 2):
            lanes = slice((pr // (N_HEADS // 4)) * LANES, (pr // (N_HEADS // 4) + 1) * LANES)
            a = oz[2 * pr * n_new:(2 * pr + 1) * n_new, lanes]
            b = oz[(2 * pr + 1) * n_new:(2 * pr + 2) * n_new, lanes]
            o_ref[r0:r0 + n_new, pr * LANES:(pr + 1) * LANES] = jnp.where(low, a, b)

    all_s = [scores(sb) for sb in range(seqs)]
    all_w = [weights(s) for s, _ in all_s]
    for sb, (w, (_, v_all)) in enumerate(zip(all_w, all_s)):
        finish(sb, w, v_all)


def _attn_sample_call(sink_row, q, k_new, v_new, cache_k, cache_v, n_new, seqs):
    b, rows, _ = cache_k.shape
    tok = pl.BlockSpec((seqs * n_new, Q_DIM), lambda i: (i, 0))
    tok_kv = pl.BlockSpec((seqs * n_new, KV_DIM), lambda i: (i, 0))
    cache = pl.BlockSpec((seqs, rows, KV_DIM), lambda i: (i, 0, 0))
    return pl.pallas_call(
        functools.partial(_attn_sample_kernel, n_new, seqs),
        grid=(b // seqs,),
        in_specs=[pl.BlockSpec((1, N_HEADS * n_new), lambda i: (0, 0)), tok, tok_kv, tok_kv, cache, cache],
        out_specs=[tok, cache, cache],
        out_shape=[
            jax.ShapeDtypeStruct((b * n_new, Q_DIM), F32),
            jax.ShapeDtypeStruct((b, rows, KV_DIM), F32),
            jax.ShapeDtypeStruct((b, rows, KV_DIM), F32),
        ],
        compiler_params=_cparams(("parallel",)),
        name="attn_sample",
    )(sink_row, q, k_new, v_new, cache_k, cache_v)


def _moe_cap(t_total):
    return t_total + MOE_TILE


def _expert_steps(cap):
    return cap // MOE_TILE - 1 + N_EXPERT_GROUPS


def _write_step_table(maps_ref, counts, cap):
    per = cap // MOE_TILE
    shift = MOE_TILE.bit_length() - 1
    ends = []
    for c in counts:
        tiles = lax.shift_right_logical(c + (MOE_TILE - 1), shift)
        ends.append(tiles if not ends else ends[-1] + tiles)
    total = ends[-1]
    for j in range(maps_ref.shape[1]):
        jj = jnp.maximum(jnp.minimum(j, total - 1), 0)
        g = sum((jj >= e).astype(I32) for e in ends[:-1])
        start = sum(jnp.where(g > k, ends[k] - (ends[k - 1] if k else 0), 0) for k in range(len(ends) - 1))
        blk_in = g * per + jj - start
        valid = (total > j).astype(I32)
        maps_ref[0, j] = blk_in
        maps_ref[1, j] = jnp.where(valid == 1, blk_in, len(counts) * per)
        maps_ref[2, j] = g
        maps_ref[3, j] = valid


def _route_rows(x1, gn_ref, wrh_ref, wrl_ref, br_ref):
    tm = x1.shape[0]
    xn = _rms(x1, gn_ref[...])
    xh = xn.astype(BF16)
    xl = (xn - xh.astype(F32)).astype(BF16)
    logits = _dot(xh, wrh_ref[...]) + _dot(xl, wrh_ref[...]) + _dot(xh, wrl_ref[...]) + br_ref[...]
    lt = logits.T
    ge = N_EXPERTS // EXPERTS_PER_GROUP
    sub = lax.broadcasted_iota(I32, (EXPERTS_PER_GROUP, tm), 0).astype(F32)
    big = jnp.float32(LANES)
    neg = -jnp.inf
    gl = jnp.where(sub < N_EXPERT_GROUPS, lt[N_EXPERTS:N_EXPERTS + EXPERTS_PER_GROUP], neg)
    gmax = jnp.max(gl, axis=0, keepdims=True)
    g_val = 1.0 / jnp.sum(jnp.exp(gl - gmax), axis=0, keepdims=True)
    g_idx = jnp.min(jnp.where(gl == gmax, sub, big), axis=0, keepdims=True)
    el = lt[0:EXPERTS_PER_GROUP]
    for g in range(1, ge):
        el = jnp.where(g_idx == g, lt[g * EXPERTS_PER_GROUP:(g + 1) * EXPERTS_PER_GROUP], el)
    e1 = jnp.max(el, axis=0, keepdims=True)
    i1 = jnp.min(jnp.where(el == e1, sub, big), axis=0, keepdims=True)
    el2 = jnp.where(sub == i1, neg, el)
    e2 = jnp.max(el2, axis=0, keepdims=True)
    i2 = jnp.min(jnp.where(el2 == e2, sub, big), axis=0, keepdims=True)
    t = jnp.exp(e2 - e1)
    w1 = 1.0 / (1.0 + t)
    w2 = t / (1.0 + t)
    wts_t = g_val * (jnp.where(sub == i1, w1, 0.0) + jnp.where(sub == i2, w2, 0.0))
    wts = jnp.concatenate([wts_t, jnp.zeros((LANES - EXPERTS_PER_GROUP, tm), F32)], axis=0).T
    return xn, wts, g_idx


def _route_begin(cntc_in, cntr_in, xs_hbm, pay, cntc, cntr, sem):
    i = pl.program_id(0)
    slot = i % 2

    @pl.when(i == 0)
    def _():
        cntc[...] = cntc_in[...]
        cntr[...] = cntr_in[...]

    @pl.when(i >= 2)
    def _():
        pltpu.make_async_copy(pay.at[slot], xs_hbm.at[pl.ds(0, pay.shape[1]), :], sem.at[slot]).wait()


def _route_scatter(final, cap, parts, triu_ref, dest_ref, cntc_out, cntr_out, xs_hbm, maps_ref,
                   pay, zrows, dest_v, dest_s, cntc, cntr, sem):
    i = pl.program_id(0)
    n = pl.num_programs(0)
    slot = i % 2
    tm = pay.shape[1]

    def wait_slot(s):
        pltpu.make_async_copy(pay.at[s], xs_hbm.at[pl.ds(0, tm), :], sem.at[s]).wait()

    r0 = 0
    for xn, wts, _ in parts:
        pay[slot, r0:r0 + xn.shape[0], :PAY_X] = xn
        pay[slot, r0:r0 + xn.shape[0], PAY_X:] = wts
        r0 += xn.shape[0]
    g_idx = jnp.concatenate([p[2] for p in parts], axis=1)

    grp_t = lax.broadcasted_iota(I32, (LANES, tm), 0).astype(F32)
    oht = (grp_t == g_idx).astype(F32)
    rank = _dot(oht.astype(BF16), triu_ref[...])
    grp = lax.broadcasted_iota(I32, (LANES, 1), 0).astype(F32)
    base = grp * float(cap) + cntc[:, 0:1]
    dest = jnp.sum(oht * (rank + base), axis=0, keepdims=True).astype(I32)
    cntc[...] = cntc[...] + jnp.sum(oht, axis=1, keepdims=True)
    dest_ref[0] = dest
    dest_v[...] = dest
    pltpu.sync_copy(dest_v, dest_s)

    for s in range(2):
        @pl.when(slot == s)
        def _():
            for r in range(tm):
                pltpu.make_async_copy(pay.at[s, pl.ds(r, 1), :],
                                      xs_hbm.at[pl.ds(dest_s[0, r], 1), :], sem.at[s]).start()

    @pl.when(i == n - 1)
    def _():
        cntc_out[...] = cntc[...]
        cntr[...] = cntc[...].T[0:8, :]
        cntr_out[...] = cntr[...]

        @pl.when(n >= 2)
        def _():
            wait_slot(1 - slot)

        wait_slot(slot)
        dest_v[:, 0:LANES] = cntr[0:1, :].astype(I32)
        pltpu.sync_copy(dest_v, dest_s)
        counts = [dest_s[0, g] for g in range(N_EXPERT_GROUPS)]
        _write_step_table(maps_ref, counts, cap)
        if final:
            zrows[...] = jnp.zeros_like(zrows)
            starts = []
            for g in range(N_EXPERT_GROUPS):
                c_g = counts[g]
                starts.append(pl.multiple_of(g * cap + lax.shift_left(lax.shift_right_logical(c_g + 7, 3), 3), 8))
                for k in range(7):
                    pltpu.make_async_copy(zrows.at[pl.ds(0, 1), :],
                                          xs_hbm.at[pl.ds(g * cap + c_g + k, 1), :], sem.at[1]).start()
            for _ in range(7 * N_EXPERT_GROUPS):
                pltpu.make_async_copy(zrows.at[pl.ds(0, 1), :], xs_hbm.at[pl.ds(0, 1), :], sem.at[1]).wait()
            for g in range(N_EXPERT_GROUPS):
                pltpu.make_async_copy(zrows, xs_hbm.at[pl.ds(starts[g], MOE_TILE), :], sem.at[0]).start()
            for g in range(N_EXPERT_GROUPS):
                pltpu.make_async_copy(zrows, xs_hbm.at[pl.ds(0, MOE_TILE), :], sem.at[0]).wait()


RouteRefs = collections.namedtuple(
    "RouteRefs", "gn wrh wrl br triu cntc_in cntr_in x1 dest cntc_out cntr_out xs maps "
                 "pay zrows dest_v dest_s cntc cntr sem")


def _route_refs(rest, has_prev):
    if has_prev:
        rest = rest[:7] + rest[8:]
    return RouteRefs(*rest)


def _route_tile(final, cap, r, x1_parts):
    parts = [_route_rows(x1, r.gn, r.wrh, r.wrl, r.br) for x1 in x1_parts]
    _route_scatter(final, cap, parts, r.triu, r.dest, r.cntc_out, r.cntr_out, r.xs, r.maps,
                   r.pay, r.zrows, r.dest_v, r.dest_s, r.cntc, r.cntr, r.sem)


def _oproj_kernel(has_prev, final, cap, *refs):
    (o_ref, x_ref, wo_ref, bo_ref) = refs[:4]
    r = _route_refs(refs[4:], has_prev)
    _route_begin(r.cntc_in, r.cntr_in, r.xs, r.pay, r.cntc, r.cntr, r.sem)
    rows = x_ref.shape[0] // ROUTE_PARTS
    x1_parts = []
    for h in range(ROUTE_PARTS):
        rs = slice(h * rows, (h + 1) * rows)
        x1 = x_ref[rs, :] + _dot(o_ref[rs, :].astype(BF16), wo_ref[...]) + bo_ref[...]
        r.x1[rs, :] = x1
        x1_parts.append(x1)
    _route_tile(final, cap, r, x1_parts)


def _gelu_tanh(x):
    return x * (0.5 * (1.0 + jnp.tanh(math.sqrt(2.0 / math.pi) * (x + 0.044715 * (x * x * x)))))


def _glu_kernel(has_prev, final, cap, *refs):
    (y_ref, x_ref, gm_ref, d_ref, wa_ref, wb_ref) = refs[:6]
    r = _route_refs(refs[6:], has_prev)
    _route_begin(r.cntc_in, r.cntr_in, r.xs, r.pay, r.cntc, r.cntr, r.sem)
    rows = x_ref.shape[0] // ROUTE_PARTS
    gated = []
    for h in range(ROUTE_PARTS):
        rs = slice(h * rows, (h + 1) * rows)
        x = x_ref[rs, :]
        z = _gelu_tanh(y_ref[rs, :] + d_ref[...] * _rms(x, gm_ref[...])).astype(BF16)
        gated.append((rs, x, _dot(z, wa_ref[...]), _dot(z, wb_ref[...])))
    x1_parts = []
    for rs, x, a, b in gated:
        x1 = x + a * _sigmoid(b)
        r.x1[rs, :] = x1
        x1_parts.append(x1)
    _route_tile(final, cap, r, x1_parts)


def _row_spec(tm, width):
    return pl.BlockSpec((tm, width), lambda i: (i, 0))


def _const_spec(shape):
    return pl.BlockSpec(shape, lambda i: (0,) * len(shape))


def _mixer_call(body, name, lead_args, lead_specs, x2d, router, triu, cnt, xs_prev, final, cap):
    t = x2d.shape[0]
    tm = TOKEN_TILE
    n_tiles = t // tm
    gn, wrh, wrl, br = router
    cntc, cntr = cnt
    has_prev = xs_prev is not None
    in_specs = lead_specs + [
        _const_spec((1, D_MODEL)), _const_spec((D_MODEL, LANES)), _const_spec((D_MODEL, LANES)),
        _const_spec((1, LANES)), _const_spec((tm, tm)), _const_spec((LANES, LANES)), _const_spec((8, LANES)),
    ]
    args = list(lead_args) + [gn, wrh, wrl, br, triu, cntc, cntr]
    aliases = {}
    if has_prev:
        in_specs.append(pl.BlockSpec(memory_space=pl.ANY))
        args.append(xs_prev)
        aliases = {len(args) - 1: 4}
    x1, dest, cntc2, cntr2, xs, maps = pl.pallas_call(
        functools.partial(body, has_prev, final, cap),
        grid=(n_tiles,),
        in_specs=in_specs,
        out_specs=[
            _row_spec(tm, D_MODEL),
            pl.BlockSpec((1, 1, tm), lambda i: (i, 0, 0)),
            _const_spec((LANES, LANES)), _const_spec((8, LANES)),
            pl.BlockSpec(memory_space=pl.ANY),
            pl.BlockSpec(memory_space=pltpu.SMEM),
        ],
        out_shape=[
            jax.ShapeDtypeStruct((t, D_MODEL), F32),
            jax.ShapeDtypeStruct((n_tiles, 1, tm), I32),
            jax.ShapeDtypeStruct((LANES, LANES), F32),
            jax.ShapeDtypeStruct((8, LANES), F32),
            jax.ShapeDtypeStruct((N_EXPERT_GROUPS * cap, PAY_W), F32),
            jax.ShapeDtypeStruct((4, _expert_steps(cap)), I32),
        ],
        scratch_shapes=[
            pltpu.VMEM((2, tm, PAY_W), F32),
            pltpu.VMEM((MOE_TILE, PAY_W), F32),
            pltpu.VMEM((1, tm), I32),
            pltpu.SMEM((1, tm), I32),
            pltpu.VMEM((LANES, LANES), F32),
            pltpu.VMEM((8, LANES), F32),
            pltpu.SemaphoreType.DMA((2,)),
        ],
        input_output_aliases=aliases,
        compiler_params=_cparams(("arbitrary",)),
        name=name,
    )(*args)
    return x1, dest, (cntc2, cntr2), xs, maps


def _oproj_call(o2d, x2d, wo_bf, bo, router, triu, cnt, xs_prev, final, cap):
    tm = TOKEN_TILE
    lead_specs = [_row_spec(tm, Q_DIM), _row_spec(tm, D_MODEL), _const_spec((Q_DIM, D_MODEL)),
                  _const_spec((1, D_MODEL))]
    return _mixer_call(_oproj_kernel, "oproj_route", [o2d, x2d, wo_bf, bo], lead_specs, x2d,
                       router, triu, cnt, xs_prev, final, cap)


def _glu_call(y2d, x2d, gm, d, wa_bf, wb_bf, router, triu, cnt, xs_prev, final, cap):
    tm = TOKEN_TILE
    lead_specs = [_row_spec(tm, D_MODEL), _row_spec(tm, D_MODEL), _const_spec((1, D_MODEL)),
                  _const_spec((1, D_MODEL)), _const_spec((D_MODEL, D_MODEL)),
                  _const_spec((D_MODEL, D_MODEL))]
    return _mixer_call(_glu_kernel, "glu_route", [y2d, x2d, gm, d, wa_bf, wb_bf], lead_specs, x2d,
                       router, triu, cnt, xs_prev, final, cap)


def _expert_kernel(maps_ref, xs_ref, wg_ref, wu_ref, wd_ref, ys_ref, wgu, wdn, hid):
    j = pl.program_id(0)
    f = EXPERT_FF
    changed = (j == 0) | (maps_ref[2, j] != maps_ref[2, jnp.maximum(j - 1, 0)])
    valid = maps_ref[3, j]

    @pl.when(changed)
    def _():
        for e in range(EXPERTS_PER_GROUP):
            wgu[e, :, :f] = wg_ref[e].astype(BF16)
            wgu[e, :, f:] = wu_ref[e].astype(BF16)
            wdn[e * f:(e + 1) * f, :] = wd_ref[e].astype(BF16)

    @pl.when(valid == 1)
    def _():
        x = xs_ref[:, :PAY_X].astype(BF16)
        wts = xs_ref[:, PAY_X:]
        for e in range(EXPERTS_PER_GROUP):
            gu = _dot(x, wgu[e])
            g, u = gu[:, :f], gu[:, f:]
            hid[:, e * f:(e + 1) * f] = ((g * _sigmoid(g)) * u * wts[:, e:e + 1]).astype(BF16)
        ys_ref[...] = _dot(hid[...], wdn[...])

    @pl.when(valid == 0)
    def _():
        ys_ref[...] = jnp.zeros_like(ys_ref)


def _expert_call(maps, xs, w_gate, w_up, w_down, layer, cap):
    n_steps = maps.shape[1]
    e, f = EXPERTS_PER_GROUP, EXPERT_FF
    w_gate = w_gate.reshape(-1, D_MODEL, f)
    w_up = w_up.reshape(-1, D_MODEL, f)
    w_down = w_down.reshape(-1, f, D_MODEL)
    w_blk = lambda j, m: (m[2, j] + layer * N_EXPERT_GROUPS, 0, 0)
    grid_spec = pltpu.PrefetchScalarGridSpec(
        num_scalar_prefetch=1,
        grid=(n_steps,),
        in_specs=[
            pl.BlockSpec((MOE_TILE, PAY_W), lambda j, m: (m[0, j], 0)),
            pl.BlockSpec((e, D_MODEL, f), w_blk),
            pl.BlockSpec((e, D_MODEL, f), w_blk),
            pl.BlockSpec((e, f, D_MODEL), w_blk),
        ],
        out_specs=pl.BlockSpec((MOE_TILE, D_MODEL), lambda j, m: (m[1, j], 0)),
        scratch_shapes=[
            pltpu.VMEM((e, D_MODEL, 2 * f), BF16),
            pltpu.VMEM((e * f, D_MODEL), BF16),
            pltpu.VMEM((MOE_TILE, e * f), BF16),
        ],
    )
    return pl.pallas_call(
        _expert_kernel,
        grid_spec=grid_spec,
        out_shape=jax.ShapeDtypeStruct((N_EXPERT_GROUPS * cap + MOE_TILE, D_MODEL), F32),
        compiler_params=_cparams(("arbitrary",)),
        name="moe_experts",
    )(maps, xs, w_gate, w_up, w_down)


def _combine_kernel(emit_x, dest_ref, x_ref, gnext_ref, ys_hbm, *rest):
    outs, (ybuf, sem) = rest[:-2], rest[-2:]
    i = pl.program_id(0)
    n_tiles = pl.num_programs(0) - 1
    slot = i % 2
    tm = x_ref.shape[0]

    for s in range(2):
        @pl.when((i < n_tiles) & (slot == s))
        def _():
            for r in range(tm):
                pltpu.make_async_copy(ys_hbm.at[pl.ds(dest_ref[0, 0, r], 1), :],
                                      ybuf.at[s, pl.ds(r, 1), :], sem.at[s]).start()

    @pl.when(i >= 1)
    def _():
        prev = 1 - slot
        pltpu.make_async_copy(ys_hbm.at[pl.ds(0, tm), :], ybuf.at[prev], sem.at[prev]).wait()
        x2 = x_ref[...] + ybuf[prev]
        normed = _rms(x2, gnext_ref[...])
        if emit_x:
            outs[0][...] = x2
            outs[1][...] = normed
        else:
            outs[0][...] = normed


def _combine_call(dest, x2d, ys, gnext, emit_x):
    t = x2d.shape[0]
    tm = COMBINE_TILE
    n_tiles = t // tm
    dest = dest.reshape(n_tiles, 1, tm)
    n_out = 2 if emit_x else 1
    done = lambda i: (jnp.maximum(i - 1, 0), 0)
    return pl.pallas_call(
        functools.partial(_combine_kernel, emit_x),
        grid=(n_tiles + 1,),
        in_specs=[
            pl.BlockSpec((1, 1, tm), lambda i: (jnp.minimum(i, n_tiles - 1), 0, 0),
                         memory_space=pltpu.SMEM),
            pl.BlockSpec((tm, D_MODEL), done), _const_spec((1, D_MODEL)),
            pl.BlockSpec(memory_space=pl.ANY),
        ],
        out_specs=[pl.BlockSpec((tm, D_MODEL), done)] * n_out,
        out_shape=[jax.ShapeDtypeStruct((t, D_MODEL), F32)] * n_out,
        scratch_shapes=[pltpu.VMEM((2, tm, D_MODEL), F32), pltpu.SemaphoreType.DMA((2,))],
        compiler_params=_cparams(("arbitrary",)),
        name="moe_combine",
    )(dest, x2d, gnext, ys)


def _s5_state_in(u_ref_val, wre_ref, wim_ref, store_re, store_im, pair_w):
    for m in range(S5_GB // 2):
        up = u_ref_val(m * pair_w, pair_w)
        store_re(m, _dot(up, wre_ref[m]))
        store_im(m, _dot(up, wim_ref[m]))


def _s5_outputs(u_ref_val, hre, him, m_ref, gre_ref, gim_ref, y_store, pair_w):
    gw = pair_w // 2
    for m in range(S5_GB // 2):
        hr = hre(m).astype(BF16)
        hi = him(m).astype(BF16)
        y = _dot(hr, gre_ref[m]) + _dot(hi, gim_ref[m])
        y0 = y[:, :gw] + _dot(u_ref_val(m * pair_w, gw), m_ref[2 * m])
        y1 = y[:, gw:] + _dot(u_ref_val(m * pair_w + gw, gw), m_ref[2 * m + 1])
        y_store(m * pair_w, gw, y0)
        y_store(m * pair_w + gw, gw, y1)


def _s5_flatten(load_rows, q, n, ut, uflat):
    gc = SSM_GROUP_CH
    qc = q * gc
    for s in range(q):
        ut[:, s * gc:(s + 1) * gc, :] = load_rows(s).astype(BF16).T.reshape(S5_GB, gc, n)
    for g in range(S5_GB):
        uflat[:, g * qc:(g + 1) * qc] = ut[g].T


def _s5_unflatten(yflat, yt, store_rows, q, n):
    gc = SSM_GROUP_CH
    qc = q * gc
    for g in range(S5_GB):
        yt[g] = yflat[:, g * qc:(g + 1) * qc].T
    for t in range(q):
        store_rows(t, yt[:, t * gc:(t + 1) * gc, :].reshape(S5_GB * gc, n).T)


def _s5_prompt_kernel(u_ref, wre_ref, wim_ref, m_ref, gre_ref, gim_ref, aqr_ref, aqi_ref,
                      y_ref, her_ref, hei_ref, ut, uflat, sre, sim, hre, him, yflat, yt):
    pair_w = 2 * S5_CHUNK * SSM_GROUP_CH
    n_chunks = u_ref.shape[1] // S5_CHUNK
    n_pairs = S5_GB // 2
    seq_rows = lambda b: pl.ds(b, n_chunks, stride=S5_SEQS)
    for b in range(S5_SEQS):
        _s5_flatten(lambda s: u_ref[b, pl.ds(s, n_chunks, stride=S5_CHUNK), :],
                    S5_CHUNK, n_chunks, ut, uflat.at[b])
        u_val = lambda off, w: uflat[b, :, off:off + w]

        def store_re(m, val):
            sre[m, seq_rows(b), :] = val

        def store_im(m, val):
            sim[m, seq_rows(b), :] = val

        _s5_state_in(u_val, wre_ref, wim_ref, store_re, store_im, pair_w)
    ar = [aqr_ref[:, m * LANES:(m + 1) * LANES] for m in range(n_pairs)]
    ai = [aqi_ref[:, m * LANES:(m + 1) * LANES] for m in range(n_pairs)]

    def step(n, carry):
        rows = pl.ds(pl.multiple_of(n * S5_SEQS, S5_SEQS), S5_SEQS)
        out = []
        for m in range(n_pairs):
            hr, hi = carry[2 * m], carry[2 * m + 1]
            hre[m, rows, :] = hr
            him[m, rows, :] = hi
            out.append(ar[m] * hr - ai[m] * hi + sre[m, rows, :])
            out.append(ar[m] * hi + ai[m] * hr + sim[m, rows, :])
        return tuple(out)

    zero = jnp.zeros((S5_SEQS, LANES), F32)
    last = lax.fori_loop(0, n_chunks, step, (zero,) * (2 * n_pairs))
    for b in range(S5_SEQS):
        for m in range(n_pairs):
            her_ref[b, :, m * LANES:(m + 1) * LANES] = last[2 * m][b:b + 1]
            hei_ref[b, :, m * LANES:(m + 1) * LANES] = last[2 * m + 1][b:b + 1]

    def y_store(off, w, val):
        yflat[:, off:off + w] = val

    for b in range(S5_SEQS):
        u_val = lambda off, w: uflat[b, :, off:off + w]
        _s5_outputs(u_val, lambda m: hre[m, seq_rows(b), :], lambda m: him[m, seq_rows(b), :],
                    m_ref, gre_ref, gim_ref, y_store, pair_w)

        def store_rows(t, val):
            y_ref[b, pl.ds(t, n_chunks, stride=S5_CHUNK), :] = val

        _s5_unflatten(yflat, yt, store_rows, S5_CHUNK, n_chunks)


def _s5_sample_kernel(n_new, u_ref, h0r_ref, h0i_ref, wre_ref, wim_ref, m_ref, gre_ref, gim_ref,
                      aqr_ref, aqi_ref, y_ref, hnr_ref, hni_ref, ut, uflat, sre, sim, yflat, yt):
    pair_w = 2 * n_new * SSM_GROUP_CH
    seqs = h0r_ref.shape[0]
    _s5_flatten(lambda s: u_ref[pl.ds(s, seqs, stride=n_new), :], n_new, seqs, ut, uflat)
    u_val = lambda off, w: uflat[:, off:off + w]

    def store_re(m, val):
        sre[:, m * LANES:(m + 1) * LANES] = val

    def store_im(m, val):
        sim[:, m * LANES:(m + 1) * LANES] = val

    _s5_state_in(u_val, wre_ref, wim_ref, store_re, store_im, pair_w)
    ar = aqr_ref[...]
    ai = aqi_ref[...]
    h0r = h0r_ref[...]
    h0i = h0i_ref[...]
    hnr_ref[...] = ar * h0r - ai * h0i + sre[...]
    hni_ref[...] = ar * h0i + ai * h0r + sim[...]

    def y_store(off, w, val):
        yflat[:, off:off + w] = val

    _s5_outputs(u_val, lambda m: h0r_ref[:, m * LANES:(m + 1) * LANES],
                lambda m: h0i_ref[:, m * LANES:(m + 1) * LANES],
                m_ref, gre_ref, gim_ref, y_store, pair_w)

    def store_rows(t, val):
        y_ref[pl.ds(t, seqs, stride=n_new), :] = val

    _s5_unflatten(yflat, yt, store_rows, n_new, seqs)


def _s5_weight_specs(q, idx):
    qc = q * SSM_GROUP_CH
    np_ = S5_GB // 2
    st = S5_GB * SSM_STATE
    return [
        pl.BlockSpec((np_, 2 * qc, LANES), lambda *a: (idx(*a), 0, 0)),
        pl.BlockSpec((np_, 2 * qc, LANES), lambda *a: (idx(*a), 0, 0)),
        pl.BlockSpec((S5_GB, qc, qc), lambda *a: (idx(*a), 0, 0)),
        pl.BlockSpec((np_, LANES, 2 * qc), lambda *a: (idx(*a), 0, 0)),
        pl.BlockSpec((np_, LANES, 2 * qc), lambda *a: (idx(*a), 0, 0)),
        pl.BlockSpec((1, st), lambda *a: (0, idx(*a))),
        pl.BlockSpec((1, st), lambda *a: (0, idx(*a))),
    ]


def _s5_prompt_call(u, w):
    b, seq, _ = u.shape
    n_chunks = seq // S5_CHUNK
    gbl = S5_GB * SSM_GROUP_CH
    qc = S5_CHUNK * SSM_GROUP_CH
    st = S5_GB * SSM_STATE
    n_gb = SSM_GROUPS // S5_GB
    gb_of = lambda g, i: g
    tok = pl.BlockSpec((S5_SEQS, seq, gbl), lambda g, i: (i, 0, g))
    slab = pltpu.VMEM((S5_GB // 2, n_chunks * S5_SEQS, LANES), F32)
    return pl.pallas_call(
        _s5_prompt_kernel,
        grid=(n_gb, b // S5_SEQS),
        in_specs=[tok] + _s5_weight_specs(S5_CHUNK, gb_of),
        out_specs=[
            tok,
            pl.BlockSpec((S5_SEQS, 1, st), lambda g, i: (i, 0, g)),
            pl.BlockSpec((S5_SEQS, 1, st), lambda g, i: (i, 0, g)),
        ],
        out_shape=[
            jax.ShapeDtypeStruct((b, seq, D_MODEL), F32),
            jax.ShapeDtypeStruct((b, 1, SSM_GROUPS * SSM_STATE), F32),
            jax.ShapeDtypeStruct((b, 1, SSM_GROUPS * SSM_STATE), F32),
        ],
        scratch_shapes=[
            pltpu.VMEM((S5_GB, qc, n_chunks), BF16),
            pltpu.VMEM((S5_SEQS, n_chunks, S5_GB * qc), BF16),
            slab, slab, slab, slab,
            pltpu.VMEM((n_chunks, S5_GB * qc), F32),
            pltpu.VMEM((S5_GB, qc, n_chunks), F32),
        ],
        compiler_params=_cparams(("parallel", "parallel")),
        name="s5_prompt",
    )(u, *w)


def _s5_sample_call(u2d, h0r, h0i, w, n_new):
    t = u2d.shape[0]
    b = t // n_new
    gbl = S5_GB * SSM_GROUP_CH
    qc = n_new * SSM_GROUP_CH
    st = S5_GB * SSM_STATE
    n_gb = SSM_GROUPS // S5_GB
    gb_of = lambda g: g
    state = pl.BlockSpec((b, st), lambda g: (0, g))
    tok = pl.BlockSpec((t, gbl), lambda g: (0, g))
    return pl.pallas_call(
        functools.partial(_s5_sample_kernel, n_new),
        grid=(n_gb,),
        in_specs=[tok, state, state] + _s5_weight_specs(n_new, gb_of),
        out_specs=[tok, state, state],
        out_shape=[
            jax.ShapeDtypeStruct((t, D_MODEL), F32),
            jax.ShapeDtypeStruct((b, SSM_GROUPS * SSM_STATE), F32),
            jax.ShapeDtypeStruct((b, SSM_GROUPS * SSM_STATE), F32),
        ],
        scratch_shapes=[
            pltpu.VMEM((S5_GB, qc, b), BF16),
            pltpu.VMEM((b, S5_GB * qc), BF16),
            pltpu.VMEM((b, st), F32), pltpu.VMEM((b, st), F32),
            pltpu.VMEM((b, S5_GB * qc), F32),
            pltpu.VMEM((S5_GB, qc, b), F32),
        ],
        compiler_params=_cparams(("parallel",)),
        name="s5_sample",
    )(u2d, h0r, h0i, *w)


def _s5_discretize(a_re, a_im, log_dt, b_re, b_im):
    delta = jnp.exp(log_dt.astype(F32))[:, None]
    lr, li = a_re.astype(F32), a_im.astype(F32)
    mag = jnp.exp(delta * lr)
    abar_r = mag * jnp.cos(delta * li)
    abar_i = mag * jnp.sin(delta * li)
    nr, ni = abar_r - 1.0, abar_i
    den = lr * lr + li * li
    coef_r = ((nr * lr + ni * li) / den)[..., None]
    coef_i = ((ni * lr - nr * li) / den)[..., None]
    br, bi = b_re.astype(F32), b_im.astype(F32)
    return delta * lr, delta * li, coef_r * br - coef_i * bi, coef_r * bi + coef_i * br


def _split2(x):
    x1 = x.astype(BF16)
    return x1, (x - x1.astype(F32)).astype(BF16)


def _s5_prep_kernel(qs, bbr_ref, bbi_ref, ctr_ref, cti_ref, ppr_ref, ppi_ref, pnr_ref, pni_ref,
                    *out_refs):
    c = SSM_GROUP_CH
    qc = max(qs) * c
    p2 = 2 * SSM_STATE
    col = lax.broadcasted_iota(I32, (c, qc), 1)
    row = lax.broadcasted_iota(I32, (c, qc), 0)
    by_ch = (col % c == row).astype(BF16)
    by_pos = (col // c == row).astype(BF16)

    def spread(x, sel):
        return sum(_dot(part, sel) for part in _split2(x))

    def dot_f32(a, b):
        a1, a2 = _split2(a)
        b1, b2 = _split2(b)
        return _dot(a1, b1) + _dot(a1, b2) + _dot(a2, b1)

    bc_r, bc_i = spread(bbr_ref[0], by_ch), spread(bbi_ref[0], by_ch)
    cc_r, cc_i = spread(ctr_ref[0], by_ch), spread(cti_ref[0], by_ch)
    pp_r, pp_i = spread(ppr_ref[0], by_pos), spread(ppi_ref[0], by_pos)
    pn_r, pn_i = spread(pnr_ref[0], by_pos), spread(pni_ref[0], by_pos)
    r_r = cc_r * pp_r - cc_i * pp_i
    r_i = cc_r * pp_i + cc_i * pp_r
    l_r = bc_r * pn_r - bc_i * pn_i
    l_i = bc_r * pn_i + bc_i * pn_r
    lt_r, lt_i = l_r.T, l_i.T
    grp_col = lax.broadcasted_iota(I32, (qc, p2), 1) // SSM_STATE
    causal =(lax.broadcasted_iota(I32, (qc, qc), 1) // c) >= (lax.broadcasted_iota(I32, (qc, qc), 0) // c)
    kers = [jnp.where(causal, dot_f32(jnp.where(grp_col == gl, lt_r, 0.0), r_r)
                      - dot_f32(jnp.where(grp_col == gl, lt_i, 0.0), r_i), 0.0) for gl in range(2)]
    for n, q in enumerate(qs):
        wre_ref, wim_ref, m_ref, gre_ref, gim_ref = out_refs[5 * n:5 * n + 5]
        w = q * c
        aq_r = ppr_ref[0][:, q - 1:q]
        aq_i = ppi_ref[0][:, q - 1:q]
        w_r = (l_r[:, :w] * aq_r - l_i[:, :w] * aq_i).T
        w_i = (l_r[:, :w] * aq_i + l_i[:, :w] * aq_r).T
        g_re, g_im = [], []
        col_grp = lax.broadcasted_iota(I32, (w, p2), 1) // SSM_STATE
        row_grp = lax.broadcasted_iota(I32, (p2, w), 0) // SSM_STATE
        for gl in range(2):
            wre_ref[0, gl * w:(gl + 1) * w, :] = jnp.where(col_grp == gl, w_r, 0.0).astype(BF16)
            wim_ref[0, gl * w:(gl + 1) * w, :] = jnp.where(col_grp == gl, w_i, 0.0).astype(BF16)
            m_ref[gl] = kers[gl][:w, :w].astype(BF16)
            g_re.append(jnp.where(row_grp == gl, r_r[:, :w], 0.0))
            g_im.append(jnp.where(row_grp == gl, -r_i[:, :w], 0.0))
        gre_ref[0] = jnp.concatenate(g_re, axis=1).astype(BF16)
        gim_ref[0] = jnp.concatenate(g_im, axis=1).astype(BF16)


def _s5_chunk_weights(tables, qs):
    n_pairs = SSM_GROUPS // 2
    p2 = 2 * SSM_STATE
    tab = pl.BlockSpec((1, p2, SSM_GROUP_CH), lambda m: (m, 0, 0))
    out_specs, out_shape = [], []
    for q in qs:
        qc = q * SSM_GROUP_CH
        out_specs += [pl.BlockSpec((1, 2 * qc, p2), lambda m: (m, 0, 0)),
                      pl.BlockSpec((1, 2 * qc, p2), lambda m: (m, 0, 0)),
                      pl.BlockSpec((2, qc, qc), lambda m: (m, 0, 0)),
                      pl.BlockSpec((1, p2, 2 * qc), lambda m: (m, 0, 0)),
                      pl.BlockSpec((1, p2, 2 * qc), lambda m: (m, 0, 0))]
        out_shape += [jax.ShapeDtypeStruct((n_pairs, 2 * qc, p2), BF16),
                      jax.ShapeDtypeStruct((n_pairs, 2 * qc, p2), BF16),
                      jax.ShapeDtypeStruct((SSM_GROUPS, qc, qc), BF16),
                      jax.ShapeDtypeStruct((n_pairs, p2, 2 * qc), BF16),
                      jax.ShapeDtypeStruct((n_pairs, p2, 2 * qc), BF16)]
    outs = pl.pallas_call(
        functools.partial(_s5_prep_kernel, tuple(qs)),
        grid=(n_pairs,),
        in_specs=[tab] * 8,
        out_specs=out_specs, out_shape=out_shape,
        compiler_params=_cparams(("parallel",)),
        name="s5_operators",
    )(*tables)
    return [tuple(outs[5 * n:5 * n + 5]) for n in range(len(qs))]


def _s5_tables(log_mag, phase, bbar_r, bbar_i, c_re, c_im, q_max):
    k = jnp.arange(1, q_max + 1, dtype=F32)[None, None, :]
    mag = jnp.exp(log_mag[:, :, None] * k)
    cos, sin = jnp.cos(phase[:, :, None] * k), jnp.sin(phase[:, :, None] * k)
    tabs = (bbar_r, bbar_i, c_re.astype(F32).transpose(0, 2, 1), c_im.astype(F32).transpose(0, 2, 1),
            mag * cos, mag * sin, cos / mag, -sin / mag)
    return tuple(t.reshape(SSM_GROUPS // 2, 2 * SSM_STATE, SSM_GROUP_CH) for t in tabs)


def _rope_tables(pos):
    half = HEAD_DIM // 2
    inv = 1.0 / (ROPE_THETA ** (jnp.arange(half, dtype=F32) * (2.0 / HEAD_DIM)))
    ang = pos.astype(F32)[:, None] * inv[None, :]
    cos, sin = jnp.cos(ang), jnp.sin(ang)
    return jnp.tile(cos, (1, 4)), jnp.concatenate([-sin, sin, -sin, sin], axis=1)


def _router_weights(gain, w_rg, b_rg, w_re, b_re):
    n_layers = gain.shape[0]
    pad = LANES - N_EXPERTS - N_EXPERT_GROUPS
    w = jnp.concatenate([w_re, w_rg, jnp.zeros((n_layers, D_MODEL, pad), F32)], axis=2)
    b = jnp.concatenate([b_re, b_rg, jnp.zeros((n_layers, pad), F32)], axis=1)
    wh = w.astype(BF16)
    wl = (w - wh.astype(F32)).astype(BF16)
    return [(gain[l].reshape(1, -1).astype(F32), wh[l], wl[l], b[l].reshape(1, LANES))
            for l in range(n_layers)]


def kernel(x_prompt, x_sample, cache_k, cache_v, state_ssm_re, state_ssm_im, norm_mix, norm_ffn, norm_final, attn_w_qkv, attn_b_qkv, attn_w_o, attn_b_o, attn_sinks, ssm_a_re, ssm_a_im, ssm_log_dt, ssm_b_re, ssm_b_im, ssm_c_re, ssm_c_im, ssm_d, ssm_w_glu_a, ssm_w_glu_b, moe_w_router_group, moe_b_router_group, moe_w_router_expert, moe_b_router_expert, moe_w_gate, moe_w_up, moe_w_down):
    bsz, seq, _ = x_prompt.shape
    dbs, n_new, _ = x_sample.shape
    rows = cache_k.shape[2]
    tp, ts = bsz * seq, dbs * n_new
    cap = _moe_cap(tp + ts)
    xp = x_prompt.reshape(tp, D_MODEL)
    xs = x_sample.reshape(ts, D_MODEL)

    row1 = lambda v: v.reshape(1, -1).astype(F32)
    routers = _router_weights(norm_ffn, moe_w_router_group, moe_b_router_group,
                              moe_w_router_expert, moe_b_router_expert)
    triu = jnp.triu(jnp.ones((TOKEN_TILE, TOKEN_TILE), F32), 1).astype(BF16)
    cnt0 = (jnp.zeros((LANES, LANES), F32), jnp.zeros((8, LANES), F32))

    slots = jnp.asarray(HEAD_SLOTS, dtype=I32)

    def q_slots(w):
        lead = w.shape[:-1]
        qh = w[..., :Q_DIM].reshape(*lead, N_HEADS, HEAD_DIM)[..., slots, :].reshape(*lead, Q_DIM)
        return jnp.concatenate([qh, w[..., Q_DIM:]], axis=-1)

    wqkv = q_slots(attn_w_qkv[0]).astype(BF16)
    bqkv = row1(q_slots(attn_b_qkv[0]))
    wo = attn_w_o[0].reshape(N_HEADS, HEAD_DIM, D_MODEL)[slots].reshape(Q_DIM, D_MODEL).astype(BF16)
    bo = row1(attn_b_o[0])
    sinks = attn_sinks[0].astype(F32)[slots]
    g_mix0, g_mix1 = row1(norm_mix[0]), row1(norm_mix[1])
    pos = jnp.concatenate([jnp.arange(seq, dtype=I32),
                           jnp.tile(PAST_LEN + jnp.arange(n_new, dtype=I32), dbs)])
    cos_t, sin_t = _rope_tables(pos)

    qp, kp, vp = _qkv_call(xp, g_mix0, wqkv, bqkv, cos_t, sin_t, (0, seq), BF16)
    qs, ks, vs = _qkv_call(xs, g_mix0, wqkv, bqkv, cos_t, sin_t, (seq, ts), F32)
    op = _attn_prompt_call(jnp.repeat(sinks, BLOCK // ATTN_PARTS).reshape(1, -1),
                           qp.reshape(bsz, seq, Q_DIM),
                           kp.reshape(bsz, seq, KV_DIM),
                           vp.reshape(bsz, seq, KV_DIM))
    os_, nks, nvs = _attn_sample_call(jnp.repeat(sinks, n_new).reshape(1, -1), qs, ks, vs,
                                      cache_k[0].reshape(dbs, rows, KV_DIM),
                                      cache_v[0].reshape(dbs, rows, KV_DIM), n_new, 8)
    xp1, dest_p, cnt, rows_x, _ = _oproj_call(op.reshape(tp, Q_DIM), xp, wo, bo, routers[0], triu,
                                              cnt0, None, False, cap)
    xs1, dest_s, cnt, rows_x, steps = _oproj_call(os_, xs, wo, bo, routers[0], triu, cnt, rows_x,
                                                  True, cap)
    rows_y = _expert_call(steps, rows_x, moe_w_gate, moe_w_up, moe_w_down, 0, cap)
    xp2, up = _combine_call(dest_p, xp1, rows_y, g_mix1, True)
    xs2, us = _combine_call(dest_s, xs1, rows_y, g_mix1, True)

    disc = _s5_discretize(ssm_a_re[0], ssm_a_im[0], ssm_log_dt[0], ssm_b_re[0], ssm_b_im[0])
    tables = _s5_tables(*disc, ssm_c_re[0], ssm_c_im[0], S5_CHUNK)
    a_pow = lambda q: [tables[i][:, :, q - 1].reshape(1, -1) for i in (4, 5)]
    ops_p, ops_s = _s5_chunk_weights(tables, (S5_CHUNK, n_new))
    w_p = (*ops_p, *a_pow(S5_CHUNK))
    w_s = (*ops_s, *a_pow(n_new))
    y_p, hpr, hpi = _s5_prompt_call(up.reshape(bsz, seq, D_MODEL), w_p)
    y_p = y_p.reshape(tp, D_MODEL)
    h0r = state_ssm_re[0].reshape(dbs, -1).astype(F32)
    h0i = state_ssm_im[0].reshape(dbs, -1).astype(F32)
    y_s, hsr, hsi = _s5_sample_call(us, h0r, h0i, w_s, n_new)

    wa, wb = ssm_w_glu_a[0].astype(BF16), ssm_w_glu_b[0].astype(BF16)
    d_row = row1(ssm_d[0])
    xp3, dest_p, cnt, rows_x, _ = _glu_call(y_p, xp2, g_mix1, d_row, wa, wb, routers[1], triu,
                                            cnt0, None, False, cap)
    xs3, dest_s, cnt, rows_x, steps = _glu_call(y_s, xs2, g_mix1, d_row, wa, wb, routers[1], triu,
                                                cnt, rows_x, True, cap)
    rows_y = _expert_call(steps, rows_x, moe_w_gate, moe_w_up, moe_w_down, 1, cap)
    g_fin = row1(norm_final)
    (yp,) = _combine_call(dest_p, xp3, rows_y, g_fin, False)
    (ys,) = _combine_call(dest_s, xs3, rows_y, g_fin, False)

    kv5 = lambda a, n: a.reshape(1, n, rows, KV_HEADS, HEAD_DIM)
    st4 = lambda a, n: a.reshape(1, n, SSM_GROUPS, SSM_STATE)
    k_last = kp.reshape(bsz, seq, KV_DIM)[:, seq - WINDOW:]
    v_last = vp.reshape(bsz, seq, KV_DIM)[:, seq - WINDOW:]
    return (yp.reshape(bsz, seq, D_MODEL), ys.reshape(dbs, n_new, D_MODEL),
            k_last.reshape(1, bsz, WINDOW, KV_HEADS, HEAD_DIM), kv5(nks, dbs),
            v_last.reshape(1, bsz, WINDOW, KV_HEADS, HEAD_DIM), kv5(nvs, dbs),
            st4(hpr, bsz), st4(hsr, dbs), st4(hpi, bsz), st4(hsi, dbs))
```

```python
import collections
import functools
import math

import jax
import jax.numpy as jnp
from jax import lax
from jax.experimental import pallas as pl
from jax.experimental.pallas import tpu as pltpu

F32 = jnp.float32
BF16 = jnp.bfloat16
I32 = jnp.int32

D_MODEL = 1024
N_HEADS = 16
KV_HEADS = 4
HEAD_DIM = 64
Q_DIM = N_HEADS * HEAD_DIM
KV_DIM = KV_HEADS * HEAD_DIM
QKV_DIM = Q_DIM + 2 * KV_DIM
WINDOW = 128
BLOCK = 128
ROPE_THETA = 10000.0
PAST_LEN = 16384
SSM_GROUP_CH = 16
SSM_GROUPS = D_MODEL // SSM_GROUP_CH
SSM_STATE = 64
N_EXPERT_GROUPS = 4
EXPERTS_PER_GROUP = 8
N_EXPERTS = N_EXPERT_GROUPS * EXPERTS_PER_GROUP
EXPERT_FF = 128
NORM_EPS = 1e-5

LANES = 128
VMEM_LIMIT = 56 * 1024 * 1024
S5_CHUNK = 16
S5_GB = 8
S5_SEQS = 4
TOKEN_TILE = 1024
MOE_TILE = 512
ROUTE_PARTS = 2
COMBINE_TILE = 512
ATTN_PARTS = 4
ATTN_BLOCKS = 4
PAY_X = D_MODEL
PAY_W = PAY_X + LANES


def _cparams(sem):
    return pltpu.CompilerParams(dimension_semantics=sem, vmem_limit_bytes=VMEM_LIMIT)


def _rms(x, g):
    return x * lax.rsqrt(jnp.mean(x * x, axis=-1, keepdims=True) + NORM_EPS) * g


def _dot(a, b):
    return jnp.dot(a, b, preferred_element_type=F32)


def _dot_nt(a, b):
    return lax.dot_general(a, b, (((1,), (1,)), ((), ())), preferred_element_type=F32)


def _sigmoid(x):
    return 1.0 / (1.0 + jnp.exp(-x))


def _qkv_kernel(x_ref, g_ref, w_ref, b_ref, cos_ref, sin_ref, q_ref, k_ref, v_ref):
    rows = x_ref.shape[0] // ROUTE_PARTS
    spans = [pl.ds(p * rows, rows) for p in range(ROUTE_PARTS)]
    qkvs = [_dot(_rms(x_ref[sp, :], g_ref[...]).astype(BF16), w_ref[...]) + b_ref[...]
            for sp in spans]
    lane = lax.broadcasted_iota(I32, (rows, LANES), 1)
    first_half = (lane % HEAD_DIM) < (HEAD_DIM // 2)
    n_rot = (Q_DIM + KV_DIM) // LANES
    for sp, qkv in zip(spans, qkvs):
        cos = cos_ref[sp, :]
        sin = sin_ref[sp, :]
        for c in range(n_rot):
            blk = qkv[:, c * LANES:(c + 1) * LANES]
            partner = jnp.where(first_half,
                                pltpu.roll(blk, LANES - HEAD_DIM // 2, 1),
                                pltpu.roll(blk, HEAD_DIM // 2, 1))
            rot = blk * cos + partner * sin
            if c < Q_DIM // LANES:
                q_ref[sp, c * LANES:(c + 1) * LANES] = (rot * (HEAD_DIM ** -0.5)).astype(q_ref.dtype)
            else:
                k_ref[sp, c * LANES - Q_DIM:(c + 1) * LANES - Q_DIM] = rot
        v_ref[sp, :] = qkv[:, Q_DIM + KV_DIM:]


def _qkv_call(x2d, gain, w_bf, bias, cos_t, sin_t, pos_rows, q_dtype):
    t = x2d.shape[0]
    tm = TOKEN_TILE
    first, n_pos = pos_rows[0] // tm, pos_rows[1] // tm
    return pl.pallas_call(
        _qkv_kernel,
        grid=(t // tm,),
        in_specs=[
            pl.BlockSpec((tm, D_MODEL), lambda i: (i, 0)),
            pl.BlockSpec((1, D_MODEL), lambda i: (0, 0)),
            pl.BlockSpec((D_MODEL, QKV_DIM), lambda i: (0, 0)),
            pl.BlockSpec((1, QKV_DIM), lambda i: (0, 0)),
            pl.BlockSpec((tm, LANES), lambda i: (first + i % n_pos, 0)),
            pl.BlockSpec((tm, LANES), lambda i: (first + i % n_pos, 0)),
        ],
        out_specs=[
            pl.BlockSpec((tm, Q_DIM), lambda i: (i, 0)),
            pl.BlockSpec((tm, KV_DIM), lambda i: (i, 0)),
            pl.BlockSpec((tm, KV_DIM), lambda i: (i, 0)),
        ],
        out_shape=[
            jax.ShapeDtypeStruct((t, Q_DIM), q_dtype),
            jax.ShapeDtypeStruct((t, KV_DIM), F32),
            jax.ShapeDtypeStruct((t, KV_DIM), F32),
        ],
        compiler_params=_cparams(("parallel",)),
        name="qkv_rope",
    )(x2d, gain, w_bf, bias, cos_t, sin_t)


HEAD_SLOTS = tuple(8 * c + 4 * half + i for c in range(2) for i in range(4) for half in range(2))
LOG2E = math.log2(math.e)


def _attn_prompt_kernel(sink_ref, q_ref, *refs):
    nk = ATTN_BLOCKS + 1
    k_refs, v_refs, o_ref = refs[:nk], refs[nk:2 * nk], refs[2 * nk]
    first = pl.program_id(1) * ATTN_BLOCKS
    kb = [r[0].astype(BF16) for r in k_refs]
    vb = [r[0].astype(BF16) for r in v_refs]
    k2s = [jnp.concatenate(kb[i:i + 2], axis=0) for i in range(ATTN_BLOCKS)]
    v2s = [jnp.concatenate(vb[i:i + 2], axis=0) for i in range(ATTN_BLOCKS)]
    sub = BLOCK // ATTN_PARTS
    win = WINDOW + sub
    cols = 2 * BLOCK
    spt = cols // sub
    key = lax.broadcasted_iota(I32, (win, cols), 0)
    qry = lax.broadcasted_iota(I32, (win, cols), 1) % sub
    band = (key >= qry) & (key <= qry + WINDOW)
    masks = [[band & ((first + blk > 0) | (key + h * sub >= BLOCK)) for h in range(ATTN_PARTS)]
             for blk in range(ATTN_BLOCKS)]
    low = lax.broadcasted_iota(I32, (sub, LANES), 1) < HEAD_DIM
    keep_low = low.astype(BF16)
    keep_high = 1 - keep_low
    tasks = [(blk, grp, h) for blk in range(ATTN_BLOCKS) for grp in range(N_HEADS // spt)
             for h in range(ATTN_PARTS)]
    pairs = lambda grp: range(grp * spt // 2, (grp + 1) * spt // 2)
    chunk = lambda grp: slice((grp * spt // 8) * LANES, (grp * spt // 8 + 1) * LANES)
    q_rows = lambda blk, h: slice(blk * BLOCK + h * sub, blk * BLOCK + (h + 1) * sub)

    def scores(blk, grp, h):
        blocks = []
        for pr in pairs(grp):
            qblk = q_ref[0, q_rows(blk, h), pr * LANES:(pr + 1) * LANES]
            blocks += [qblk * keep_low, qblk * keep_high]
        qz = jnp.concatenate(blocks, axis=0)
        s = _dot_nt(k2s[blk][h * sub:h * sub + win, chunk(grp)], qz) * LOG2E
        return jnp.where(masks[blk][h], s, -jnp.inf)

    def weights(blk, grp, h, s):
        sink = sink_ref[:, grp * cols:(grp + 1) * cols] * LOG2E
        m = jnp.maximum(jnp.max(s, axis=0, keepdims=True), sink)
        p = jnp.exp2(s - m)
        denom = jnp.sum(p, axis=0, keepdims=True) + jnp.exp2(sink - m)
        rows = [jnp.zeros((h * sub, cols), F32), p * (1.0 / denom),
                jnp.zeros((2 * BLOCK - win - h * sub, cols), F32)]
        w = jnp.concatenate([r for r in rows if r.shape[0]], axis=0)
        return w.T.astype(BF16)

    def finish(blk, grp, h, w):
        oz = _dot(w, v2s[blk][:, chunk(grp)])
        for j, pr in enumerate(pairs(grp)):
            o_ref[0, q_rows(blk, h), pr * LANES:(pr + 1) * LANES] = jnp.where(
                low, oz[2 * j * sub:(2 * j + 1) * sub], oz[(2 * j + 1) * sub:(2 * j + 2) * sub]
            ).astype(o_ref.dtype)

    all_s = [scores(*t) for t in tasks]
    all_w = [weights(*t, s) for t, s in zip(tasks, all_s)]
    for t, w in zip(tasks, all_w):
        finish(*t, w)


def _attn_prompt_call(sinks, q, k, v):
    b, l, _ = q.shape
    nb = l // (BLOCK * ATTN_BLOCKS)
    kv = [pl.BlockSpec((1, BLOCK, KV_DIM),
                       lambda i, n, d=d: (i, jnp.maximum(n * ATTN_BLOCKS + d, 0), 0))
          for d in range(-1, ATTN_BLOCKS)]
    qo = pl.BlockSpec((1, BLOCK * ATTN_BLOCKS, Q_DIM), lambda i, n: (i, n, 0))
    return pl.pallas_call(
        _attn_prompt_kernel,
        grid=(b, nb),
        in_specs=[pl.BlockSpec((1, N_HEADS * BLOCK // ATTN_PARTS), lambda i, n: (0, 0)), qo] + kv + kv,
        out_specs=qo,
        out_shape=jax.ShapeDtypeStruct((b, l, Q_DIM), BF16),
        compiler_params=_cparams(("parallel", "parallel")),
        name="attn_prompt",
    )(sinks, q, *([k] * len(kv)), *([v] * len(kv)))


def _attn_sample_kernel(n_new, seqs, sink_ref, q_ref, kn_ref, vn_ref, ck_ref, cv_ref,
                        o_ref, nk_ref, nv_ref):
    rows = ck_ref.shape[1]
    keys = 2 * rows
    n_cols = N_HEADS * n_new
    key = lax.broadcasted_iota(I32, (keys, n_cols), 0)
    qry = lax.broadcasted_iota(I32, (keys, n_cols), 1) % n_new
    mask = ((key < rows) & (key >= qry)) | ((key >= rows) & (key - rows <= qry))
    low = lax.broadcasted_iota(I32, (n_new, LANES), 1) < HEAD_DIM
    sink = sink_ref[...]
    pad = jnp.zeros((rows - n_new, KV_DIM), F32)

    def scores(sb):
        r0 = sb * n_new
        k_new = kn_ref[r0:r0 + n_new, :]
        v_new = vn_ref[r0:r0 + n_new, :]
        nk_ref[sb, 0:rows - n_new, :] = ck_ref[sb, n_new:rows, :]
        nk_ref[sb, rows - n_new:rows, :] = k_new
        nv_ref[sb, 0:rows - n_new, :] = cv_ref[sb, n_new:rows, :]
        nv_ref[sb, rows - n_new:rows, :] = v_new
        k_all = jnp.concatenate([ck_ref[sb], k_new, pad], axis=0).astype(BF16)
        v_all = jnp.concatenate([cv_ref[sb], v_new, pad], axis=0).astype(BF16)
        blocks = []
        for slot in range(N_HEADS):
            pair = q_ref[r0:r0 + n_new, (slot // 2) * LANES:(slot // 2 + 1) * LANES]
            blk = jnp.where(low if slot % 2 == 0 else ~low, pair, 0.0)
            zero = jnp.zeros_like(blk)
            blocks.append(jnp.concatenate([blk, zero] if slot < N_HEADS // 2 else [zero, blk], axis=1))
        qz = jnp.concatenate(blocks, axis=0).astype(BF16)
        s = _dot_nt(k_all, qz)
        return jnp.where(mask, s, -jnp.inf), v_all

    def weights(s):
        m = jnp.maximum(jnp.max(s, axis=0, keepdims=True), sink)
        p = jnp.exp(s - m)
        denom = jnp.sum(p, axis=0, keepdims=True) + jnp.exp(sink - m)
        return (p * (1.0 / denom)).T.astype(BF16)

    def finish(sb, w, v_all):
        r0 = sb * n_new
        oz = _dot(w, v_all)
        for pr in range(N_HEADS // 2):
            lanes = slice((pr // (N_HEADS // 4)) * LANES, (pr // (N_HEADS // 4) + 1) * LANES)
            a = oz[2 * pr * n_new:(2 * pr + 1) * n_new, lanes]
            b = oz[(2 * pr + 1) * n_new:(2 * pr + 2) * n_new, lanes]
            o_ref[r0:r0 + n_new, pr * LANES:(pr + 1) * LANES] = jnp.where(low, a, b)

    all_s = [scores(sb) for sb in range(seqs)]
    all_w = [weights(s) for s, _ in all_s]
    for sb, (w, (_, v_all)) in enumerate(zip(all_w, all_s)):
        finish(sb, w, v_all)


def _attn_sample_call(sink_row, q, k_new, v_new, cache_k, cache_v, n_new, seqs):
    b, rows, _ = cache_k.shape
    tok = pl.BlockSpec((seqs * n_new, Q_DIM), lambda i: (i, 0))
    tok_kv = pl.BlockSpec((seqs * n_new, KV_DIM), lambda i: (i, 0))
    cache = pl.BlockSpec((seqs, rows, KV_DIM), lambda i: (i, 0, 0))
    return pl.pallas_call(
        functools.partial(_attn_sample_kernel, n_new, seqs),
        grid=(b // seqs,),
        in_specs=[pl.BlockSpec((1, N_HEADS * n_new), lambda i: (0, 0)), tok, tok_kv, tok_kv, cache, cache],
        out_specs=[tok, cache, cache],
        out_shape=[
            jax.ShapeDtypeStruct((b * n_new, Q_DIM), F32),
            jax.ShapeDtypeStruct((b, rows, KV_DIM), F32),
            jax.ShapeDtypeStruct((b, rows, KV_DIM), F32),
        ],
        compiler_params=_cparams(("parallel",)),
        name="attn_sample",
    )(sink_row, q, k_new, v_new, cache_k, cache_v)


def _moe_cap(t_total):
    return t_total + MOE_TILE


def _expert_steps(cap):
    return cap // MOE_TILE - 1 + N_EXPERT_GROUPS


def _write_step_table(maps_ref, counts, cap):
    per = cap // MOE_TILE
    shift = MOE_TILE.bit_length() - 1
    ends = []
    for c in counts:
        tiles = lax.shift_right_logical(c + (MOE_TILE - 1), shift)
        ends.append(tiles if not ends else ends[-1] + tiles)
    total = ends[-1]
    for j in range(maps_ref.shape[1]):
        jj = jnp.maximum(jnp.minimum(j, total - 1), 0)
        g = sum((jj >= e).astype(I32) for e in ends[:-1])
        start = sum(jnp.where(g > k, ends[k] - (ends[k - 1] if k else 0), 0) for k in range(len(ends) - 1))
        blk_in = g * per + jj - start
        valid = (total > j).astype(I32)
        maps_ref[0, j] = blk_in
        maps_ref[1, j] = jnp.where(valid == 1, blk_in, len(counts) * per)
        maps_ref[2, j] = g
        maps_ref[3, j] = valid


def _route_rows(x1, gn_ref, wrh_ref, wrl_ref, br_ref):
    tm = x1.shape[0]
    xn = _rms(x1, gn_ref[...])
    xh = xn.astype(BF16)
    xl = (xn - xh.astype(F32)).astype(BF16)
    logits = _dot(xh, wrh_ref[...]) + _dot(xl, wrh_ref[...]) + _dot(xh, wrl_ref[...]) + br_ref[...]
    lt = logits.T
    ge = N_EXPERTS // EXPERTS_PER_GROUP
    sub = lax.broadcasted_iota(I32, (EXPERTS_PER_GROUP, tm), 0).astype(F32)
    big = jnp.float32(LANES)
    neg = -jnp.inf
    gl = jnp.where(sub < N_EXPERT_GROUPS, lt[N_EXPERTS:N_EXPERTS + EXPERTS_PER_GROUP], neg)
    gmax = jnp.max(gl, axis=0, keepdims=True)
    g_val = 1.0 / jnp.sum(jnp.exp(gl - gmax), axis=0, keepdims=True)
    g_idx = jnp.min(jnp.where(gl == gmax, sub, big), axis=0, keepdims=True)
    el = lt[0:EXPERTS_PER_GROUP]
    for g in range(1, ge):
        el = jnp.where(g_idx == g, lt[g * EXPERTS_PER_GROUP:(g + 1) * EXPERTS_PER_GROUP], el)
    e1 = jnp.max(el, axis=0, keepdims=True)
    i1 = jnp.min(jnp.where(el == e1, sub, big), axis=0, keepdims=True)
    el2 = jnp.where(sub == i1, neg, el)
    e2 = jnp.max(el2, axis=0, keepdims=True)
    i2 = jnp.min(jnp.where(el2 == e2, sub, big), axis=0, keepdims=True)
    t = jnp.exp(e2 - e1)
    w1 = 1.0 / (1.0 + t)
    w2 = t / (1.0 + t)
    wts_t = g_val * (jnp.where(sub == i1, w1, 0.0) + jnp.where(sub == i2, w2, 0.0))
    wts = jnp.concatenate([wts_t, jnp.zeros((LANES - EXPERTS_PER_GROUP, tm), F32)], axis=0).T
    return xn, wts, g_idx


def _route_begin(cntc_in, cntr_in, xs_hbm, pay, cntc, cntr, sem):
    i = pl.program_id(0)
    slot = i % 2

    @pl.when(i == 0)
    def _():
        cntc[...] = cntc_in[...]
        cntr[...] = cntr_in[...]

    @pl.when(i >= 2)
    def _():
        pltpu.make_async_copy(pay.at[slot], xs_hbm.at[pl.ds(0, pay.shape[1]), :], sem.at[slot]).wait()


def _route_scatter(final, cap, parts, triu_ref, dest_ref, cntc_out, cntr_out, xs_hbm, maps_ref,
                   pay, zrows, dest_v, dest_s, cntc, cntr, sem):
    i = pl.program_id(0)
    n = pl.num_programs(0)
    slot = i % 2
    tm = pay.shape[1]

    def wait_slot(s):
        pltpu.make_async_copy(pay.at[s], xs_hbm.at[pl.ds(0, tm), :], sem.at[s]).wait()

    r0 = 0
    for xn, wts, _ in parts:
        pay[slot, r0:r0 + xn.shape[0], :PAY_X] = xn
        pay[slot, r0:r0 + xn.shape[0], PAY_X:] = wts
        r0 += xn.shape[0]
    g_idx = jnp.concatenate([p[2] for p in parts], axis=1)

    grp_t = lax.broadcasted_iota(I32, (LANES, tm), 0).astype(F32)
    oht = (grp_t == g_idx).astype(F32)
    rank = _dot(oht.astype(BF16), triu_ref[...])
    grp = lax.broadcasted_iota(I32, (LANES, 1), 0).astype(F32)
    base = grp * float(cap) + cntc[:, 0:1]
    dest = jnp.sum(oht * (rank + base), axis=0, keepdims=True).astype(I32)
    cntc[...] = cntc[...] + jnp.sum(oht, axis=1, keepdims=True)
    dest_ref[0] = dest
    dest_v[...] = dest
    pltpu.sync_copy(dest_v, dest_s)

    for s in range(2):
        @pl.when(slot == s)
        def _():
            for r in range(tm):
                pltpu.make_async_copy(pay.at[s, pl.ds(r, 1), :],
                                      xs_hbm.at[pl.ds(dest_s[0, r], 1), :], sem.at[s]).start()

    @pl.when(i == n - 1)
    def _():
        cntc_out[...] = cntc[...]
        cntr[...] = cntc[...].T[0:8, :]
        cntr_out[...] = cntr[...]

        @pl.when(n >= 2)
        def _():
            wait_slot(1 - slot)

        wait_slot(slot)
        dest_v[:, 0:LANES] = cntr[0:1, :].astype(I32)
        pltpu.sync_copy(dest_v, dest_s)
        counts = [dest_s[0, g] for g in range(N_EXPERT_GROUPS)]
        _write_step_table(maps_ref, counts, cap)
        if final:
            zrows[...] = jnp.zeros_like(zrows)
            starts = []
            for g in range(N_EXPERT_GROUPS):
                c_g = counts[g]
                starts.append(pl.multiple_of(g * cap + lax.shift_left(lax.shift_right_logical(c_g + 7, 3), 3), 8))
                for k in range(7):
                    pltpu.make_async_copy(zrows.at[pl.ds(0, 1), :],
                                          xs_hbm.at[pl.ds(g * cap + c_g + k, 1), :], sem.at[1]).start()
            for _ in range(7 * N_EXPERT_GROUPS):
                pltpu.make_async_copy(zrows.at[pl.ds(0, 1), :], xs_hbm.at[pl.ds(0, 1), :], sem.at[1]).wait()
            for g in range(N_EXPERT_GROUPS):
                pltpu.make_async_copy(zrows, xs_hbm.at[pl.ds(starts[g], MOE_TILE), :], sem.at[0]).start()
            for g in range(N_EXPERT_GROUPS):
                pltpu.make_async_copy(zrows, xs_hbm.at[pl.ds(0, MOE_TILE), :], sem.at[0]).wait()


RouteRefs = collections.namedtuple(
    "RouteRefs", "gn wrh wrl br triu cntc_in cntr_in x1 dest cntc_out cntr_out xs maps "
                 "pay zrows dest_v dest_s cntc cntr sem")


def _route_refs(rest, has_prev):
    if has_prev:
        rest = rest[:7] + rest[8:]
    return RouteRefs(*rest)


def _route_tile(final, cap, r, x1_parts):
    parts = [_route_rows(x1, r.gn, r.wrh, r.wrl, r.br) for x1 in x1_parts]
    _route_scatter(final, cap, parts, r.triu, r.dest, r.cntc_out, r.cntr_out, r.xs, r.maps,
                   r.pay, r.zrows, r.dest_v, r.dest_s, r.cntc, r.cntr, r.sem)


def _oproj_kernel(has_prev, final, cap, *refs):
    (o_ref, x_ref, wo_ref, bo_ref) = refs[:4]
    r = _route_refs(refs[4:], has_prev)
    _route_begin(r.cntc_in, r.cntr_in, r.xs, r.pay, r.cntc, r.cntr, r.sem)
    rows = x_ref.shape[0] // ROUTE_PARTS
    x1_parts = []
    for h in range(ROUTE_PARTS):
        rs = slice(h * rows, (h + 1) * rows)
        x1 = x_ref[rs, :] + _dot(o_ref[rs, :].astype(BF16), wo_ref[...]) + bo_ref[...]
        r.x1[rs, :] = x1
        x1_parts.append(x1)
    _route_tile(final, cap, r, x1_parts)


def _gelu_tanh(x):
    return x * (0.5 * (1.0 + jnp.tanh(math.sqrt(2.0 / math.pi) * (x + 0.044715 * (x * x * x)))))


def _glu_kernel(has_prev, final, cap, *refs):
    (y_ref, x_ref, gm_ref, d_ref, wa_ref, wb_ref) = refs[:6]
    r = _route_refs(refs[6:], has_prev)
    _route_begin(r.cntc_in, r.cntr_in, r.xs, r.pay, r.cntc, r.cntr, r.sem)
    rows = x_ref.shape[0] // ROUTE_PARTS
    gated = []
    for h in range(ROUTE_PARTS):
        rs = slice(h * rows, (h + 1) * rows)
        x = x_ref[rs, :]
        z = _gelu_tanh(y_ref[rs, :] + d_ref[...] * _rms(x, gm_ref[...])).astype(BF16)
        gated.append((rs, x, _dot(z, wa_ref[...]), _dot(z, wb_ref[...])))
    x1_parts = []
    for rs, x, a, b in gated:
        x1 = x + a * _sigmoid(b)
        r.x1[rs, :] = x1
        x1_parts.append(x1)
    _route_tile(final, cap, r, x1_parts)


def _row_spec(tm, width):
    return pl.BlockSpec((tm, width), lambda i: (i, 0))


def _const_spec(shape):
    return pl.BlockSpec(shape, lambda i: (0,) * len(shape))


def _mixer_call(body, name, lead_args, lead_specs, x2d, router, triu, cnt, xs_prev, final, cap):
    t = x2d.shape[0]
    tm = TOKEN_TILE
    n_tiles = t // tm
    gn, wrh, wrl, br = router
    cntc, cntr = cnt
    has_prev = xs_prev is not None
    in_specs = lead_specs + [
        _const_spec((1, D_MODEL)), _const_spec((D_MODEL, LANES)), _const_spec((D_MODEL, LANES)),
        _const_spec((1, LANES)), _const_spec((tm, tm)), _const_spec((LANES, LANES)), _const_spec((8, LANES)),
    ]
    args = list(lead_args) + [gn, wrh, wrl, br, triu, cntc, cntr]
    aliases = {}
    if has_prev:
        in_specs.append(pl.BlockSpec(memory_space=pl.ANY))
        args.append(xs_prev)
        aliases = {len(args) - 1: 4}
    x1, dest, cntc2, cntr2, xs, maps = pl.pallas_call(
        functools.partial(body, has_prev, final, cap),
        grid=(n_tiles,),
        in_specs=in_specs,
        out_specs=[
            _row_spec(tm, D_MODEL),
            pl.BlockSpec((1, 1, tm), lambda i: (i, 0, 0)),
            _const_spec((LANES, LANES)), _const_spec((8, LANES)),
            pl.BlockSpec(memory_space=pl.ANY),
            pl.BlockSpec(memory_space=pltpu.SMEM),
        ],
        out_shape=[
            jax.ShapeDtypeStruct((t, D_MODEL), F32),
            jax.ShapeDtypeStruct((n_tiles, 1, tm), I32),
            jax.ShapeDtypeStruct((LANES, LANES), F32),
            jax.ShapeDtypeStruct((8, LANES), F32),
            jax.ShapeDtypeStruct((N_EXPERT_GROUPS * cap, PAY_W), F32),
            jax.ShapeDtypeStruct((4, _expert_steps(cap)), I32),
        ],
        scratch_shapes=[
            pltpu.VMEM((2, tm, PAY_W), F32),
            pltpu.VMEM((MOE_TILE, PAY_W), F32),
            pltpu.VMEM((1, tm), I32),
            pltpu.SMEM((1, tm), I32),
            pltpu.VMEM((LANES, LANES), F32),
            pltpu.VMEM((8, LANES), F32),
            pltpu.SemaphoreType.DMA((2,)),
        ],
        input_output_aliases=aliases,
        compiler_params=_cparams(("arbitrary",)),
        name=name,
    )(*args)
    return x1, dest, (cntc2, cntr2), xs, maps


def _oproj_call(o2d, x2d, wo_bf, bo, router, triu, cnt, xs_prev, final, cap):
    tm = TOKEN_TILE
    lead_specs = [_row_spec(tm, Q_DIM), _row_spec(tm, D_MODEL), _const_spec((Q_DIM, D_MODEL)),
                  _const_spec((1, D_MODEL))]
    return _mixer_call(_oproj_kernel, "oproj_route", [o2d, x2d, wo_bf, bo], lead_specs, x2d,
                       router, triu, cnt, xs_prev, final, cap)


def _glu_call(y2d, x2d, gm, d, wa_bf, wb_bf, router, triu, cnt, xs_prev, final, cap):
    tm = TOKEN_TILE
    lead_specs = [_row_spec(tm, D_MODEL), _row_spec(tm, D_MODEL), _const_spec((1, D_MODEL)),
                  _const_spec((1, D_MODEL)), _const_spec((D_MODEL, D_MODEL)),
                  _const_spec((D_MODEL, D_MODEL))]
    return _mixer_call(_glu_kernel, "glu_route", [y2d, x2d, gm, d, wa_bf, wb_bf], lead_specs, x2d,
                       router, triu, cnt, xs_prev, final, cap)


def _expert_kernel(maps_ref, xs_ref, wg_ref, wu_ref, wd_ref, ys_ref, wgu, wdn, hid):
    j = pl.program_id(0)
    f = EXPERT_FF
    changed = (j == 0) | (maps_ref[2, j] != maps_ref[2, jnp.maximum(j - 1, 0)])
    valid = maps_ref[3, j]

    @pl.when(changed)
    def _():
        for e in range(EXPERTS_PER_GROUP):
            wgu[e, :, :f] = wg_ref[e].astype(BF16)
            wgu[e, :, f:] = wu_ref[e].astype(BF16)
            wdn[e * f:(e + 1) * f, :] = wd_ref[e].astype(BF16)

    @pl.when(valid == 1)
    def _():
        x = xs_ref[:, :PAY_X].astype(BF16)
        wts = xs_ref[:, PAY_X:]
        for e in range(EXPERTS_PER_GROUP):
            gu = _dot(x, wgu[e])
            g, u = gu[:, :f], gu[:, f:]
            hid[:, e * f:(e + 1) * f] = ((g * _sigmoid(g)) * u * wts[:, e:e + 1]).astype(BF16)
        ys_ref[...] = _dot(hid[...], wdn[...])

    @pl.when(valid == 0)
    def _():
        ys_ref[...] = jnp.zeros_like(ys_ref)


def _expert_call(maps, xs, w_gate, w_up, w_down, layer, cap):
    n_steps = maps.shape[1]
    e, f = EXPERTS_PER_GROUP, EXPERT_FF
    w_gate = w_gate.reshape(-1, D_MODEL, f)
    w_up = w_up.reshape(-1, D_MODEL, f)
    w_down = w_down.reshape(-1, f, D_MODEL)
    w_blk = lambda j, m: (m[2, j] + layer * N_EXPERT_GROUPS, 0, 0)
    grid_spec = pltpu.PrefetchScalarGridSpec(
        num_scalar_prefetch=1,
        grid=(n_steps,),
        in_specs=[
            pl.BlockSpec((MOE_TILE, PAY_W), lambda j, m: (m[0, j], 0)),
            pl.BlockSpec((e, D_MODEL, f), w_blk),
            pl.BlockSpec((e, D_MODEL, f), w_blk),
            pl.BlockSpec((e, f, D_MODEL), w_blk),
        ],
        out_specs=pl.BlockSpec((MOE_TILE, D_MODEL), lambda j, m: (m[1, j], 0)),
        scratch_shapes=[
            pltpu.VMEM((e, D_MODEL, 2 * f), BF16),
            pltpu.VMEM((e * f, D_MODEL), BF16),
            pltpu.VMEM((MOE_TILE, e * f), BF16),
        ],
    )
    return pl.pallas_call(
        _expert_kernel,
        grid_spec=grid_spec,
        out_shape=jax.ShapeDtypeStruct((N_EXPERT_GROUPS * cap + MOE_TILE, D_MODEL), F32),
        compiler_params=_cparams(("arbitrary",)),
        name="moe_experts",
    )(maps, xs, w_gate, w_up, w_down)


def _combine_kernel(emit_x, dest_ref, x_ref, gnext_ref, ys_hbm, *rest):
    outs, (ybuf, sem) = rest[:-2], rest[-2:]
    i = pl.program_id(0)
    n_tiles = pl.num_programs(0) - 1
    slot = i % 2
    tm = x_ref.shape[0]

    for s in range(2):
        @pl.when((i < n_tiles) & (slot == s))
        def _():
            for r in range(tm):
                pltpu.make_async_copy(ys_hbm.at[pl.ds(dest_ref[0, 0, r], 1), :],
                                      ybuf.at[s, pl.ds(r, 1), :], sem.at[s]).start()

    @pl.when(i >= 1)
    def _():
        prev = 1 - slot
        pltpu.make_async_copy(ys_hbm.at[pl.ds(0, tm), :], ybuf.at[prev], sem.at[prev]).wait()
        x2 = x_ref[...] + ybuf[prev]
        normed = _rms(x2, gnext_ref[...])
        if emit_x:
            outs[0][...] = x2
            outs[1][...] = normed
        else:
            outs[0][...] = normed


def _combine_call(dest, x2d, ys, gnext, emit_x):
    t = x2d.shape[0]
    tm = COMBINE_TILE
    n_tiles = t // tm
    dest = dest.reshape(n_tiles, 1, tm)
    n_out = 2 if emit_x else 1
    done = lambda i: (jnp.maximum(i - 1, 0), 0)
    return pl.pallas_call(
        functools.partial(_combine_kernel, emit_x),
        grid=(n_tiles + 1,),
        in_specs=[
            pl.BlockSpec((1, 1, tm), lambda i: (jnp.minimum(i, n_tiles - 1), 0, 0),
                         memory_space=pltpu.SMEM),
            pl.BlockSpec((tm, D_MODEL), done), _const_spec((1, D_MODEL)),
            pl.BlockSpec(memory_space=pl.ANY),
        ],
        out_specs=[pl.BlockSpec((tm, D_MODEL), done)] * n_out,
        out_shape=[jax.ShapeDtypeStruct((t, D_MODEL), F32)] * n_out,
        scratch_shapes=[pltpu.VMEM((2, tm, D_MODEL), F32), pltpu.SemaphoreType.DMA((2,))],
        compiler_params=_cparams(("arbitrary",)),
        name="moe_combine",
    )(dest, x2d, gnext, ys)


def _s5_state_in(u_ref_val, wre_ref, wim_ref, store_re, store_im, pair_w):
    for m in range(S5_GB // 2):
        up = u_ref_val(m * pair_w, pair_w)
        store_re(m, _dot(up, wre_ref[m]))
        store_im(m, _dot(up, wim_ref[m]))


def _s5_outputs(u_ref_val, hre, him, m_ref, gre_ref, gim_ref, y_store, pair_w):
    gw = pair_w // 2
    for m in range(S5_GB // 2):
        hr = hre(m).astype(BF16)
        hi = him(m).astype(BF16)
        y = _dot(hr, gre_ref[m]) + _dot(hi, gim_ref[m])
        y0 = y[:, :gw] + _dot(u_ref_val(m * pair_w, gw), m_ref[2 * m])
        y1 = y[:, gw:] + _dot(u_ref_val(m * pair_w + gw, gw), m_ref[2 * m + 1])
        y_store(m * pair_w, gw, y0)
        y_store(m * pair_w + gw, gw, y1)


def _s5_flatten(load_rows, q, n, ut, uflat):
    gc = SSM_GROUP_CH
    qc = q * gc
    for s in range(q):
        ut[:, s * gc:(s + 1) * gc, :] = load_rows(s).astype(BF16).T.reshape(S5_GB, gc, n)
    for g in range(S5_GB):
        uflat[:, g * qc:(g + 1) * qc] = ut[g].T


def _s5_unflatten(yflat, yt, store_rows, q, n):
    gc = SSM_GROUP_CH
    qc = q * gc
    for g in range(S5_GB):
        yt[g] = yflat[:, g * qc:(g + 1) * qc].T
    for t in range(q):
        store_rows(t, yt[:, t * gc:(t + 1) * gc, :].reshape(S5_GB * gc, n).T)


def _s5_prompt_kernel(u_ref, wre_ref, wim_ref, m_ref, gre_ref, gim_ref, aqr_ref, aqi_ref,
                      y_ref, her_ref, hei_ref, ut, uflat, sre, sim, hre, him, yflat, yt):
    pair_w = 2 * S5_CHUNK * SSM_GROUP_CH
    n_chunks = u_ref.shape[1] // S5_CHUNK
    n_pairs = S5_GB // 2
    seq_rows = lambda b: pl.ds(b, n_chunks, stride=S5_SEQS)
    for b in range(S5_SEQS):
        _s5_flatten(lambda s: u_ref[b, pl.ds(s, n_chunks, stride=S5_CHUNK), :],
                    S5_CHUNK, n_chunks, ut, uflat.at[b])
        u_val = lambda off, w: uflat[b, :, off:off + w]

        def store_re(m, val):
            sre[m, seq_rows(b), :] = val

        def store_im(m, val):
            sim[m, seq_rows(b), :] = val

        _s5_state_in(u_val, wre_ref, wim_ref, store_re, store_im, pair_w)
    ar = [aqr_ref[:, m * LANES:(m + 1) * LANES] for m in range(n_pairs)]
    ai = [aqi_ref[:, m * LANES:(m + 1) * LANES] for m in range(n_pairs)]

    def step(n, carry):
        rows = pl.ds(pl.multiple_of(n * S5_SEQS, S5_SEQS), S5_SEQS)
        out = []
        for m in range(n_pairs):
            hr, hi = carry[2 * m], carry[2 * m + 1]
            hre[m, rows, :] = hr
            him[m, rows, :] = hi
            out.append(ar[m] * hr - ai[m] * hi + sre[m, rows, :])
            out.append(ar[m] * hi + ai[m] * hr + sim[m, rows, :])
        return tuple(out)

    zero = jnp.zeros((S5_SEQS, LANES), F32)
    last = lax.fori_loop(0, n_chunks, step, (zero,) * (2 * n_pairs))
    for b in range(S5_SEQS):
        for m in range(n_pairs):
            her_ref[b, :, m * LANES:(m + 1) * LANES] = last[2 * m][b:b + 1]
            hei_ref[b, :, m * LANES:(m + 1) * LANES] = last[2 * m + 1][b:b + 1]

    def y_store(off, w, val):
        yflat[:, off:off + w] = val

    for b in range(S5_SEQS):
        u_val = lambda off, w: uflat[b, :, off:off + w]
        _s5_outputs(u_val, lambda m: hre[m, seq_rows(b), :], lambda m: him[m, seq_rows(b), :],
                    m_ref, gre_ref, gim_ref, y_store, pair_w)

        def store_rows(t, val):
            y_ref[b, pl.ds(t, n_chunks, stride=S5_CHUNK), :] = val

        _s5_unflatten(yflat, yt, store_rows, S5_CHUNK, n_chunks)


def _s5_sample_kernel(n_new, u_ref, h0r_ref, h0i_ref, wre_ref, wim_ref, m_ref, gre_ref, gim_ref,
                      aqr_ref, aqi_ref, y_ref, hnr_ref, hni_ref, ut, uflat, sre, sim, yflat, yt):
    pair_w = 2 * n_new * SSM_GROUP_CH
    seqs = h0r_ref.shape[0]
    _s5_flatten(lambda s: u_ref[pl.ds(s, seqs, stride=n_new), :], n_new, seqs, ut, uflat)
    u_val = lambda off, w: uflat[:, off:off + w]

    def store_re(m, val):
        sre[:, m * LANES:(m + 1) * LANES] = val

    def store_im(m, val):
        sim[:, m * LANES:(m + 1) * LANES] = val

    _s5_state_in(u_val, wre_ref, wim_ref, store_re, store_im, pair_w)
    ar = aqr_ref[...]
    ai = aqi_ref[...]
    h0r = h0r_ref[...]
    h0i = h0i_ref[...]
    hnr_ref[...] = ar * h0r - ai * h0i + sre[...]
    hni_ref[...] = ar * h0i + ai * h0r + sim[...]

    def y_store(off, w, val):
        yflat[:, off:off + w] = val

    _s5_outputs(u_val, lambda m: h0r_ref[:, m * LANES:(m + 1) * LANES],
                lambda m: h0i_ref[:, m * LANES:(m + 1) * LANES],
                m_ref, gre_ref, gim_ref, y_store, pair_w)

    def store_rows(t, val):
        y_ref[pl.ds(t, seqs, stride=n_new), :] = val

    _s5_unflatten(yflat, yt, store_rows, n_new, seqs)


def _s5_weight_specs(q, idx):
    qc = q * SSM_GROUP_CH
    np_ = S5_GB // 2
    st = S5_GB * SSM_STATE
    return [
        pl.BlockSpec((np_, 2 * qc, LANES), lambda *a: (idx(*a), 0, 0)),
        pl.BlockSpec((np_, 2 * qc, LANES), lambda *a: (idx(*a), 0, 0)),
        pl.BlockSpec((S5_GB, qc, qc), lambda *a: (idx(*a), 0, 0)),
        pl.BlockSpec((np_, LANES, 2 * qc), lambda *a: (idx(*a), 0, 0)),
        pl.BlockSpec((np_, LANES, 2 * qc), lambda *a: (idx(*a), 0, 0)),
        pl.BlockSpec((1, st), lambda *a: (0, idx(*a))),
        pl.BlockSpec((1, st), lambda *a: (0, idx(*a))),
    ]


def _s5_prompt_call(u, w):
    b, seq, _ = u.shape
    n_chunks = seq // S5_CHUNK
    gbl = S5_GB * SSM_GROUP_CH
    qc = S5_CHUNK * SSM_GROUP_CH
    st = S5_GB * SSM_STATE
    n_gb = SSM_GROUPS // S5_GB
    gb_of = lambda g, i: g
    tok = pl.BlockSpec((S5_SEQS, seq, gbl), lambda g, i: (i, 0, g))
    slab = pltpu.VMEM((S5_GB // 2, n_chunks * S5_SEQS, LANES), F32)
    return pl.pallas_call(
        _s5_prompt_kernel,
        grid=(n_gb, b // S5_SEQS),
        in_specs=[tok] + _s5_weight_specs(S5_CHUNK, gb_of),
        out_specs=[
            tok,
            pl.BlockSpec((S5_SEQS, 1, st), lambda g, i: (i, 0, g)),
            pl.BlockSpec((S5_SEQS, 1, st), lambda g, i: (i, 0, g)),
        ],
        out_shape=[
            jax.ShapeDtypeStruct((b, seq, D_MODEL), F32),
            jax.ShapeDtypeStruct((b, 1, SSM_GROUPS * SSM_STATE), F32),
            jax.ShapeDtypeStruct((b, 1, SSM_GROUPS * SSM_STATE), F32),
        ],
        scratch_shapes=[
            pltpu.VMEM((S5_GB, qc, n_chunks), BF16),
            pltpu.VMEM((S5_SEQS, n_chunks, S5_GB * qc), BF16),
            slab, slab, slab, slab,
            pltpu.VMEM((n_chunks, S5_GB * qc), F32),
            pltpu.VMEM((S5_GB, qc, n_chunks), F32),
        ],
        compiler_params=_cparams(("parallel", "parallel")),
        name="s5_prompt",
    )(u, *w)


def _s5_sample_call(u2d, h0r, h0i, w, n_new):
    t = u2d.shape[0]
    b = t // n_new
    gbl = S5_GB * SSM_GROUP_CH
    qc = n_new * SSM_GROUP_CH
    st = S5_GB * SSM_STATE
    n_gb = SSM_GROUPS // S5_GB
    gb_of = lambda g: g
    state = pl.BlockSpec((b, st), lambda g: (0, g))
    tok = pl.BlockSpec((t, gbl), lambda g: (0, g))
    return pl.pallas_call(
        functools.partial(_s5_sample_kernel, n_new),
        grid=(n_gb,),
        in_specs=[tok, state, state] + _s5_weight_specs(n_new, gb_of),
        out_specs=[tok, state, state],
        out_shape=[
            jax.ShapeDtypeStruct((t, D_MODEL), F32),
            jax.ShapeDtypeStruct((b, SSM_GROUPS * SSM_STATE), F32),
            jax.ShapeDtypeStruct((b, SSM_GROUPS * SSM_STATE), F32),
        ],
        scratch_shapes=[
            pltpu.VMEM((S5_GB, qc, b), BF16),
            pltpu.VMEM((b, S5_GB * qc), BF16),
            pltpu.VMEM((b, st), F32), pltpu.VMEM((b, st), F32),
            pltpu.VMEM((b, S5_GB * qc), F32),
            pltpu.VMEM((S5_GB, qc, b), F32),
        ],
        compiler_params=_cparams(("parallel",)),
        name="s5_sample",
    )(u2d, h0r, h0i, *w)


def _s5_discretize(a_re, a_im, log_dt, b_re, b_im):
    delta = jnp.exp(log_dt.astype(F32))[:, None]
    lr, li = a_re.astype(F32), a_im.astype(F32)
    mag = jnp.exp(delta * lr)
    abar_r = mag * jnp.cos(delta * li)
    abar_i = mag * jnp.sin(delta * li)
    nr, ni = abar_r - 1.0, abar_i
    den = lr * lr + li * li
    coef_r = ((nr * lr + ni * li) / den)[..., None]
    coef_i = ((ni * lr - nr * li) / den)[..., None]
    br, bi = b_re.astype(F32), b_im.astype(F32)
    return delta * lr, delta * li, coef_r * br - coef_i * bi, coef_r * bi + coef_i * br


def _split2(x):
    x1 = x.astype(BF16)
    return x1, (x - x1.astype(F32)).astype(BF16)


def _s5_prep_kernel(qs, bbr_ref, bbi_ref, ctr_ref, cti_ref, ppr_ref, ppi_ref, pnr_ref, pni_ref,
                    *out_refs):
    c = SSM_GROUP_CH
    qc = max(qs) * c
    p2 = 2 * SSM_STATE
    col = lax.broadcasted_iota(I32, (c, qc), 1)
    row = lax.broadcasted_iota(I32, (c, qc), 0)
    by_ch = (col % c == row).astype(BF16)
    by_pos = (col // c == row).astype(BF16)

    def spread(x, sel):
        return sum(_dot(part, sel) for part in _split2(x))

    def dot_f32(a, b):
        a1, a2 = _split2(a)
        b1, b2 = _split2(b)
        return _dot(a1, b1) + _dot(a1, b2) + _dot(a2, b1)

    bc_r, bc_i = spread(bbr_ref[0], by_ch), spread(bbi_ref[0], by_ch)
    cc_r, cc_i = spread(ctr_ref[0], by_ch), spread(cti_ref[0], by_ch)
    pp_r, pp_i = spread(ppr_ref[0], by_pos), spread(ppi_ref[0], by_pos)
    pn_r, pn_i = spread(pnr_ref[0], by_pos), spread(pni_ref[0], by_pos)
    r_r = cc_r * pp_r - cc_i * pp_i
    r_i = cc_r * pp_i + cc_i * pp_r
    l_r = bc_r * pn_r - bc_i * pn_i
    l_i = bc_r * pn_i + bc_i * pn_r
    lt_r, lt_i = l_r.T, l_i.T
    grp_col = lax.broadcasted_iota(I32, (qc, p2), 1) // SSM_STATE
    causal =(lax.broadcasted_iota(I32, (qc, qc), 1) // c) >= (lax.broadcasted_iota(I32, (qc, qc), 0) // c)
    kers = [jnp.where(causal, dot_f32(jnp.where(grp_col == gl, lt_r, 0.0), r_r)
                      - dot_f32(jnp.where(grp_col == gl, lt_i, 0.0), r_i), 0.0) for gl in range(2)]
    for n, q in enumerate(qs):
        wre_ref, wim_ref, m_ref, gre_ref, gim_ref = out_refs[5 * n:5 * n + 5]
        w = q * c
        aq_r = ppr_ref[0][:, q - 1:q]
        aq_i = ppi_ref[0][:, q - 1:q]
        w_r = (l_r[:, :w] * aq_r - l_i[:, :w] * aq_i).T
        w_i = (l_r[:, :w] * aq_i + l_i[:, :w] * aq_r).T
        g_re, g_im = [], []
        col_grp = lax.broadcasted_iota(I32, (w, p2), 1) // SSM_STATE
        row_grp = lax.broadcasted_iota(I32, (p2, w), 0) // SSM_STATE
        for gl in range(2):
            wre_ref[0, gl * w:(gl + 1) * w, :] = jnp.where(col_grp == gl, w_r, 0.0).astype(BF16)
            wim_ref[0, gl * w:(gl + 1) * w, :] = jnp.where(col_grp == gl, w_i, 0.0).astype(BF16)
            m_ref[gl] = kers[gl][:w, :w].astype(BF16)
            g_re.append(jnp.where(row_grp == gl, r_r[:, :w], 0.0))
            g_im.append(jnp.where(row_grp == gl, -r_i[:, :w], 0.0))
        gre_ref[0] = jnp.concatenate(g_re, axis=1).astype(BF16)
        gim_ref[0] = jnp.concatenate(g_im, axis=1).astype(BF16)


def _s5_chunk_weights(tables, qs):
    n_pairs = SSM_GROUPS // 2
    p2 = 2 * SSM_STATE
    tab = pl.BlockSpec((1, p2, SSM_GROUP_CH), lambda m: (m, 0, 0))
    out_specs, out_shape = [], []
    for q in qs:
        qc = q * SSM_GROUP_CH
        out_specs += [pl.BlockSpec((1, 2 * qc, p2), lambda m: (m, 0, 0)),
                      pl.BlockSpec((1, 2 * qc, p2), lambda m: (m, 0, 0)),
                      pl.BlockSpec((2, qc, qc), lambda m: (m, 0, 0)),
                      pl.BlockSpec((1, p2, 2 * qc), lambda m: (m, 0, 0)),
                      pl.BlockSpec((1, p2, 2 * qc), lambda m: (m, 0, 0))]
        out_shape += [jax.ShapeDtypeStruct((n_pairs, 2 * qc, p2), BF16),
                      jax.ShapeDtypeStruct((n_pairs, 2 * qc, p2), BF16),
                      jax.ShapeDtypeStruct((SSM_GROUPS, qc, qc), BF16),
                      jax.ShapeDtypeStruct((n_pairs, p2, 2 * qc), BF16),
                      jax.ShapeDtypeStruct((n_pairs, p2, 2 * qc), BF16)]
    outs = pl.pallas_call(
        functools.partial(_s5_prep_kernel, tuple(qs)),
        grid=(n_pairs,),
        in_specs=[tab] * 8,
        out_specs=out_specs, out_shape=out_shape,
        compiler_params=_cparams(("parallel",)),
        name="s5_operators",
    )(*tables)
    return [tuple(outs[5 * n:5 * n + 5]) for n in range(len(qs))]


def _s5_tables(log_mag, phase, bbar_r, bbar_i, c_re, c_im, q_max):
    k = jnp.arange(1, q_max + 1, dtype=F32)[None, None, :]
    mag = jnp.exp(log_mag[:, :, None] * k)
    cos, sin = jnp.cos(phase[:, :, None] * k), jnp.sin(phase[:, :, None] * k)
    tabs = (bbar_r, bbar_i, c_re.astype(F32).transpose(0, 2, 1), c_im.astype(F32).transpose(0, 2, 1),
            mag * cos, mag * sin, cos / mag, -sin / mag)
    return tuple(t.reshape(SSM_GROUPS // 2, 2 * SSM_STATE, SSM_GROUP_CH) for t in tabs)


def _rope_tables(pos):
    half = HEAD_DIM // 2
    inv = 1.0 / (ROPE_THETA ** (jnp.arange(half, dtype=F32) * (2.0 / HEAD_DIM)))
    ang = pos.astype(F32)[:, None] * inv[None, :]
    cos, sin = jnp.cos(ang), jnp.sin(ang)
    return jnp.tile(cos, (1, 4)), jnp.concatenate([-sin, sin, -sin, sin], axis=1)


def _router_weights(gain, w_rg, b_rg, w_re, b_re):
    n_layers = gain.shape[0]
    pad = LANES - N_EXPERTS - N_EXPERT_GROUPS
    w = jnp.concatenate([w_re, w_rg, jnp.zeros((n_layers, D_MODEL, pad), F32)], axis=2)
    b = jnp.concatenate([b_re, b_rg, jnp.zeros((n_layers, pad), F32)], axis=1)
    wh = w.astype(BF16)
    wl = (w - wh.astype(F32)).astype(BF16)
    return [(gain[l].reshape(1, -1).astype(F32), wh[l], wl[l], b[l].reshape(1, LANES))
            for l in range(n_layers)]


def kernel(x_prompt, x_sample, cache_k, cache_v, state_ssm_re, state_ssm_im, norm_mix, norm_ffn, norm_final, attn_w_qkv, attn_b_qkv, attn_w_o, attn_b_o, attn_sinks, ssm_a_re, ssm_a_im, ssm_log_dt, ssm_b_re, ssm_b_im, ssm_c_re, ssm_c_im, ssm_d, ssm_w_glu_a, ssm_w_glu_b, moe_w_router_group, moe_b_router_group, moe_w_router_expert, moe_b_router_expert, moe_w_gate, moe_w_up, moe_w_down):
    bsz, seq, _ = x_prompt.shape
    dbs, n_new, _ = x_sample.shape
    rows = cache_k.shape[2]
    tp, ts = bsz * seq, dbs * n_new
    cap = _moe_cap(tp + ts)
    xp = x_prompt.reshape(tp, D_MODEL)
    xs = x_sample.reshape(ts, D_MODEL)

    row1 = lambda v: v.reshape(1, -1).astype(F32)
    routers = _router_weights(norm_ffn, moe_w_router_group, moe_b_router_group,
                              moe_w_router_expert, moe_b_router_expert)
    triu = jnp.triu(jnp.ones((TOKEN_TILE, TOKEN_TILE), F32), 1).astype(BF16)
    cnt0 = (jnp.zeros((LANES, LANES), F32), jnp.zeros((8, LANES), F32))

    slots = jnp.asarray(HEAD_SLOTS, dtype=I32)

    def q_slots(w):
        lead = w.shape[:-1]
        qh = w[..., :Q_DIM].reshape(*lead, N_HEADS, HEAD_DIM)[..., slots, :].reshape(*lead, Q_DIM)
        return jnp.concatenate([qh, w[..., Q_DIM:]], axis=-1)

    wqkv = q_slots(attn_w_qkv[0]).astype(BF16)
    bqkv = row1(q_slots(attn_b_qkv[0]))
    wo = attn_w_o[0].reshape(N_HEADS, HEAD_DIM, D_MODEL)[slots].reshape(Q_DIM, D_MODEL).astype(BF16)
    bo = row1(attn_b_o[0])
    sinks = attn_sinks[0].astype(F32)[slots]
    g_mix0, g_mix1 = row1(norm_mix[0]), row1(norm_mix[1])
    pos = jnp.concatenate([jnp.arange(seq, dtype=I32),
                           jnp.tile(PAST_LEN + jnp.arange(n_new, dtype=I32), dbs)])
    cos_t, sin_t = _rope_tables(pos)

    qp, kp, vp = _qkv_call(xp, g_mix0, wqkv, bqkv, cos_t, sin_t, (0, seq), BF16)
    qs, ks, vs = _qkv_call(xs, g_mix0, wqkv, bqkv, cos_t, sin_t, (seq, ts), F32)
    op = _attn_prompt_call(jnp.repeat(sinks, BLOCK // ATTN_PARTS).reshape(1, -1),
                           qp.reshape(bsz, seq, Q_DIM),
                           kp.reshape(bsz, seq, KV_DIM),
                           vp.reshape(bsz, seq, KV_DIM))
    os_, nks, nvs = _attn_sample_call(jnp.repeat(sinks, n_new).reshape(1, -1), qs, ks, vs,
                                      cache_k[0].reshape(dbs, rows, KV_DIM),
                                      cache_v[0].reshape(dbs, rows, KV_DIM), n_new, 8)
    xp1, dest_p, cnt, rows_x, _ = _oproj_call(op.reshape(tp, Q_DIM), xp, wo, bo, routers[0], triu,
                                              cnt0, None, False, cap)
    xs1, dest_s, cnt, rows_x, steps = _oproj_call(os_, xs, wo, bo, routers[0], triu, cnt, rows_x,
                                                  True, cap)
    rows_y = _expert_call(steps, rows_x, moe_w_gate, moe_w_up, moe_w_down, 0, cap)
    xp2, up = _combine_call(dest_p, xp1, rows_y, g_mix1, True)
    xs2, us = _combine_call(dest_s, xs1, rows_y, g_mix1, True)

    disc = _s5_discretize(ssm_a_re[0], ssm_a_im[0], ssm_log_dt[0], ssm_b_re[0], ssm_b_im[0])
    tables = _s5_tables(*disc, ssm_c_re[0], ssm_c_im[0], S5_CHUNK)
    a_pow = lambda q: [tables[i][:, :, q - 1].reshape(1, -1) for i in (4, 5)]
    ops_p, ops_s = _s5_chunk_weights(tables, (S5_CHUNK, n_new))
    w_p = (*ops_p, *a_pow(S5_CHUNK))
    w_s = (*ops_s, *a_pow(n_new))
    y_p, hpr, hpi = _s5_prompt_call(up.reshape(bsz, seq, D_MODEL), w_p)
    y_p = y_p.reshape(tp, D_MODEL)
    h0r = state_ssm_re[0].reshape(dbs, -1).astype(F32)
    h0i = state_ssm_im[0].reshape(dbs, -1).astype(F32)
    y_s, hsr, hsi = _s5_sample_call(us, h0r, h0i, w_s, n_new)

    wa, wb = ssm_w_glu_a[0].astype(BF16), ssm_w_glu_b[0].astype(BF16)
    d_row = row1(ssm_d[0])
    xp3, dest_p, cnt, rows_x, _ = _glu_call(y_p, xp2, g_mix1, d_row, wa, wb, routers[1], triu,
                                            cnt0, None, False, cap)
    xs3, dest_s, cnt, rows_x, steps = _glu_call(y_s, xs2, g_mix1, d_row, wa, wb, routers[1], triu,
                                                cnt, rows_x, True, cap)
    rows_y = _expert_call(steps, rows_x, moe_w_gate, moe_w_up, moe_w_down, 1, cap)
    g_fin = row1(norm_final)
    (yp,) = _combine_call(dest_p, xp3, rows_y, g_fin, False)
    (ys,) = _combine_call(dest_s, xs3, rows_y, g_fin, False)

    kv5 = lambda a, n: a.reshape(1, n, rows, KV_HEADS, HEAD_DIM)
    st4 = lambda a, n: a.reshape(1, n, SSM_GROUPS, SSM_STATE)
    k_last = kp.reshape(bsz, seq, KV_DIM)[:, seq - WINDOW:]
    v_last = vp.reshape(bsz, seq, KV_DIM)[:, seq - WINDOW:]
    return (yp.reshape(bsz, seq, D_MODEL), ys.reshape(dbs, n_new, D_MODEL),
            k_last.reshape(1, bsz, WINDOW, KV_HEADS, HEAD_DIM), kv5(nks, dbs),
            v_last.reshape(1, bsz, WINDOW, KV_HEADS, HEAD_DIM), kv5(nvs, dbs),
            st4(hpr, bsz), st4(hsr, dbs), st4(hpi, bsz), st4(hsi, dbs))
```

```python
import collections
import functools
import math

import jax
import jax.numpy as jnp
from jax import lax
from jax.experimental import pallas as pl
from jax.experimental.pallas import tpu as pltpu

F32 = jnp.float32
BF16 = jnp.bfloat16
I32 = jnp.int32

D_MODEL = 1024
N_HEADS = 16
KV_HEADS = 4
HEAD_DIM = 64
Q_DIM = N_HEADS * HEAD_DIM
KV_DIM = KV_HEADS * HEAD_DIM
QKV_DIM = Q_DIM + 2 * KV_DIM
WINDOW = 128
BLOCK = 128
ROPE_THETA = 10000.0
PAST_LEN = 16384
SSM_GROUP_CH = 16
SSM_GROUPS = D_MODEL // SSM_GROUP_CH
SSM_STATE = 64
N_EXPERT_GROUPS = 4
EXPERTS_PER_GROUP = 8
N_EXPERTS = N_EXPERT_GROUPS * EXPERTS_PER_GROUP
EXPERT_FF = 128
NORM_EPS = 1e-5

LANES = 128
VMEM_LIMIT = 56 * 1024 * 1024
S5_CHUNK = 16
S5_GB = 8
S5_SEQS = 4
TOKEN_TILE = 1024
MOE_TILE = 512
ROUTE_PARTS = 2
COMBINE_TILE = 512
ATTN_PARTS = 4
ATTN_BLOCKS = 4
PAY_X = D_MODEL
PAY_W = PAY_X + LANES


def _cparams(sem):
    return pltpu.CompilerParams(dimension_semantics=sem, vmem_limit_bytes=VMEM_LIMIT)


def _rms(x, g):
    return x * lax.rsqrt(jnp.mean(x * x, axis=-1, keepdims=True) + NORM_EPS) * g


def _dot(a, b):
    return jnp.dot(a, b, preferred_element_type=F32)


def _dot_nt(a, b):
    return lax.dot_general(a, b, (((1,), (1,)), ((), ())), preferred_element_type=F32)


def _sigmoid(x):
    return 1.0 / (1.0 + jnp.exp(-x))


def _qkv_kernel(x_ref, g_ref, w_ref, b_ref, cos_ref, sin_ref, q_ref, k_ref, v_ref):
    xn = _rms(x_ref[...], g_ref[...])
    qkv = _dot(xn.astype(BF16), w_ref[...]) + b_ref[...]
    cos = cos_ref[...]
    sin = sin_ref[...]
    lane = lax.broadcasted_iota(I32, cos.shape, 1)
    first_half = (lane % HEAD_DIM) < (HEAD_DIM // 2)
    n_rot = (Q_DIM + KV_DIM) // LANES
    for c in range(n_rot):
        blk = qkv[:, c * LANES:(c + 1) * LANES]
        partner = jnp.where(first_half,
                            pltpu.roll(blk, LANES - HEAD_DIM // 2, 1),
                            pltpu.roll(blk, HEAD_DIM // 2, 1))
        rot = blk * cos + partner * sin
        if c < Q_DIM // LANES:
            q_ref[:, c * LANES:(c + 1) * LANES] = (rot * (HEAD_DIM ** -0.5)).astype(q_ref.dtype)
        else:
            k_ref[:, c * LANES - Q_DIM:(c + 1) * LANES - Q_DIM] = rot
    v_ref[...] = qkv[:, Q_DIM + KV_DIM:]


def _qkv_call(x2d, gain, w_bf, bias, cos_t, sin_t, pos_rows, q_dtype):
    t = x2d.shape[0]
    tm = min(TOKEN_TILE, t // 4)
    first, n_pos = pos_rows[0] // tm, pos_rows[1] // tm
    return pl.pallas_call(
        _qkv_kernel,
        grid=(t // tm,),
        in_specs=[
            pl.BlockSpec((tm, D_MODEL), lambda i: (i, 0)),
            pl.BlockSpec((1, D_MODEL), lambda i: (0, 0)),
            pl.BlockSpec((D_MODEL, QKV_DIM), lambda i: (0, 0)),
            pl.BlockSpec((1, QKV_DIM), lambda i: (0, 0)),
            pl.BlockSpec((tm, LANES), lambda i: (first + i % n_pos, 0)),
            pl.BlockSpec((tm, LANES), lambda i: (first + i % n_pos, 0)),
        ],
        out_specs=[
            pl.BlockSpec((tm, Q_DIM), lambda i: (i, 0)),
            pl.BlockSpec((tm, KV_DIM), lambda i: (i, 0)),
            pl.BlockSpec((tm, KV_DIM), lambda i: (i, 0)),
        ],
        out_shape=[
            jax.ShapeDtypeStruct((t, Q_DIM), q_dtype),
            jax.ShapeDtypeStruct((t, KV_DIM), F32),
            jax.ShapeDtypeStruct((t, KV_DIM), F32),
        ],
        compiler_params=_cparams(("parallel",)),
        name="qkv_rope",
    )(x2d, gain, w_bf, bias, cos_t, sin_t)


HEAD_SLOTS = tuple(8 * c + 4 * half + i for c in range(2) for i in range(4) for half in range(2))
LOG2E = math.log2(math.e)


def _attn_prompt_kernel(sink_ref, q_ref, *refs):
    nk = ATTN_BLOCKS + 1
    k_refs, v_refs, o_ref = refs[:nk], refs[nk:2 * nk], refs[2 * nk]
    first = pl.program_id(1) * ATTN_BLOCKS
    kb = [r[0].astype(BF16) for r in k_refs]
    vb = [r[0].astype(BF16) for r in v_refs]
    k2s = [jnp.concatenate(kb[i:i + 2], axis=0) for i in range(ATTN_BLOCKS)]
    v2s = [jnp.concatenate(vb[i:i + 2], axis=0) for i in range(ATTN_BLOCKS)]
    sub = BLOCK // ATTN_PARTS
    win = WINDOW + sub
    cols = 2 * BLOCK
    spt = cols // sub
    key = lax.broadcasted_iota(I32, (win, cols), 0)
    qry = lax.broadcasted_iota(I32, (win, cols), 1) % sub
    band = (key >= qry) & (key <= qry + WINDOW)
    masks = [[band & ((first + blk > 0) | (key + h * sub >= BLOCK)) for h in range(ATTN_PARTS)]
             for blk in range(ATTN_BLOCKS)]
    low = lax.broadcasted_iota(I32, (sub, LANES), 1) < HEAD_DIM
    keep_low = low.astype(BF16)
    keep_high = 1 - keep_low
    tasks = [(blk, grp, h) for blk in range(ATTN_BLOCKS) for grp in range(N_HEADS // spt)
             for h in range(ATTN_PARTS)]
    pairs = lambda grp: range(grp * spt // 2, (grp + 1) * spt // 2)
    chunk = lambda grp: slice((grp * spt // 8) * LANES, (grp * spt // 8 + 1) * LANES)
    q_rows = lambda blk, h: slice(blk * BLOCK + h * sub, blk * BLOCK + (h + 1) * sub)

    def scores(blk, grp, h):
        blocks = []
        for pr in pairs(grp):
            qblk = q_ref[0, q_rows(blk, h), pr * LANES:(pr + 1) * LANES]
            blocks += [qblk * keep_low, qblk * keep_high]
        qz = jnp.concatenate(blocks, axis=0)
        s = _dot_nt(k2s[blk][h * sub:h * sub + win, chunk(grp)], qz) * LOG2E
        return jnp.where(masks[blk][h], s, -jnp.inf)

    def weights(blk, grp, h, s):
        sink = sink_ref[:, grp * cols:(grp + 1) * cols] * LOG2E
        m = jnp.maximum(jnp.max(s, axis=0, keepdims=True), sink)
        p = jnp.exp2(s - m)
        denom = jnp.sum(p, axis=0, keepdims=True) + jnp.exp2(sink - m)
        rows = [jnp.zeros((h * sub, cols), F32), p * (1.0 / denom),
                jnp.zeros((2 * BLOCK - win - h * sub, cols), F32)]
        w = jnp.concatenate([r for r in rows if r.shape[0]], axis=0)
        return w.T.astype(BF16)

    def finish(blk, grp, h, w):
        oz = _dot(w, v2s[blk][:, chunk(grp)])
        for j, pr in enumerate(pairs(grp)):
            o_ref[0, q_rows(blk, h), pr * LANES:(pr + 1) * LANES] = jnp.where(
                low, oz[2 * j * sub:(2 * j + 1) * sub], oz[(2 * j + 1) * sub:(2 * j + 2) * sub]
            ).astype(o_ref.dtype)

    all_s = [scores(*t) for t in tasks]
    all_w = [weights(*t, s) for t, s in zip(tasks, all_s)]
    for t, w in zip(tasks, all_w):
        finish(*t, w)


def _attn_prompt_call(sinks, q, k, v):
    b, l, _ = q.shape
    nb = l // (BLOCK * ATTN_BLOCKS)
    kv = [pl.BlockSpec((1, BLOCK, KV_DIM),
                       lambda i, n, d=d: (i, jnp.maximum(n * ATTN_BLOCKS + d, 0), 0))
          for d in range(-1, ATTN_BLOCKS)]
    qo = pl.BlockSpec((1, BLOCK * ATTN_BLOCKS, Q_DIM), lambda i, n: (i, n, 0))
    return pl.pallas_call(
        _attn_prompt_kernel,
        grid=(b, nb),
        in_specs=[pl.BlockSpec((1, N_HEADS * BLOCK // ATTN_PARTS), lambda i, n: (0, 0)), qo] + kv + kv,
        out_specs=qo,
        out_shape=jax.ShapeDtypeStruct((b, l, Q_DIM), BF16),
        compiler_params=_cparams(("parallel", "parallel")),
        name="attn_prompt",
    )(sinks, q, *([k] * len(kv)), *([v] * len(kv)))


def _attn_sample_kernel(n_new, seqs, sink_ref, q_ref, kn_ref, vn_ref, ck_ref, cv_ref,
                        o_ref, nk_ref, nv_ref):
    rows = ck_ref.shape[1]
    keys = 2 * rows
    n_cols = N_HEADS * n_new
    key = lax.broadcasted_iota(I32, (keys, n_cols), 0)
    qry = lax.broadcasted_iota(I32, (keys, n_cols), 1) % n_new
    mask = ((key < rows) & (key >= qry)) | ((key >= rows) & (key - rows <= qry))
    low = lax.broadcasted_iota(I32, (n_new, LANES), 1) < HEAD_DIM
    sink = sink_ref[...]
    pad = jnp.zeros((rows - n_new, KV_DIM), F32)

    def scores(sb):
        r0 = sb * n_new
        k_new = kn_ref[r0:r0 + n_new, :]
        v_new = vn_ref[r0:r0 + n_new, :]
        nk_ref[sb, 0:rows - n_new, :] = ck_ref[sb, n_new:rows, :]
        nk_ref[sb, rows - n_new:rows, :] = k_new
        nv_ref[sb, 0:rows - n_new, :] = cv_ref[sb, n_new:rows, :]
        nv_ref[sb, rows - n_new:rows, :] = v_new
        k_all = jnp.concatenate([ck_ref[sb], k_new, pad], axis=0).astype(BF16)
        v_all = jnp.concatenate([cv_ref[sb], v_new, pad], axis=0).astype(BF16)
        blocks = []
        for slot in range(N_HEADS):
            pair = q_ref[r0:r0 + n_new, (slot // 2) * LANES:(slot // 2 + 1) * LANES]
            blk = jnp.where(low if slot % 2 == 0 else ~low, pair, 0.0)
            zero = jnp.zeros_like(blk)
            blocks.append(jnp.concatenate([blk, zero] if slot < N_HEADS // 2 else [zero, blk], axis=1))
        qz = jnp.concatenate(blocks, axis=0).astype(BF16)
        s = _dot_nt(k_all, qz)
        return jnp.where(mask, s, -jnp.inf), v_all

    def weights(s):
        m = jnp.maximum(jnp.max(s, axis=0, keepdims=True), sink)
        p = jnp.exp(s - m)
        denom = jnp.sum(p, axis=0, keepdims=True) + jnp.exp(sink - m)
        return (p * (1.0 / denom)).T.astype(BF16)

    def finish(sb, w, v_all):
        r0 = sb * n_new
        oz = _dot(w, v_all)
        for pr in range(N_HEADS // 2):
            lanes = slice((pr // (N_HEADS // 4)) * LANES, (pr // (N_HEADS // 4) + 1) * LANES)
            a = oz[2 * pr * n_new:(2 * pr + 1) * n_new, lanes]
            b = oz[(2 * pr + 1) * n_new:(2 * pr + 2) * n_new, lanes]
            o_ref[r0:r0 + n_new, pr * LANES:(pr + 1) * LANES] = jnp.where(low, a, b)

    all_s = [scores(sb) for sb in range(seqs)]
    all_w = [weights(s) for s, _ in all_s]
    for sb, (w, (_, v_all)) in enumerate(zip(all_w, all_s)):
        finish(sb, w, v_all)


def _attn_sample_call(sink_row, q, k_new, v_new, cache_k, cache_v, n_new, seqs):
    b, rows, _ = cache_k.shape
    tok = pl.BlockSpec((seqs * n_new, Q_DIM), lambda i: (i, 0))
    tok_kv = pl.BlockSpec((seqs * n_new, KV_DIM), lambda i: (i, 0))
    cache = pl.BlockSpec((seqs, rows, KV_DIM), lambda i: (i, 0, 0))
    return pl.pallas_call(
        functools.partial(_attn_sample_kernel, n_new, seqs),
        grid=(b // seqs,),
        in_specs=[pl.BlockSpec((1, N_HEADS * n_new), lambda i: (0, 0)), tok, tok_kv, tok_kv, cache, cache],
        out_specs=[tok, cache, cache],
        out_shape=[
            jax.ShapeDtypeStruct((b * n_new, Q_DIM), F32),
            jax.ShapeDtypeStruct((b, rows, KV_DIM), F32),
            jax.ShapeDtypeStruct((b, rows, KV_DIM), F32),
        ],
        compiler_params=_cparams(("parallel",)),
        name="attn_sample",
    )(sink_row, q, k_new, v_new, cache_k, cache_v)


def _moe_cap(t_total):
    return t_total + MOE_TILE


def _expert_steps(cap):
    return cap // MOE_TILE - 1 + N_EXPERT_GROUPS


def _write_step_table(maps_ref, counts, cap):
    per = cap // MOE_TILE
    shift = MOE_TILE.bit_length() - 1
    ends = []
    for c in counts:
        tiles = lax.shift_right_logical(c + (MOE_TILE - 1), shift)
        ends.append(tiles if not ends else ends[-1] + tiles)
    total = ends[-1]
    for j in range(maps_ref.shape[1]):
        jj = jnp.maximum(jnp.minimum(j, total - 1), 0)
        g = sum((jj >= e).astype(I32) for e in ends[:-1])
        start = sum(jnp.where(g > k, ends[k] - (ends[k - 1] if k else 0), 0) for k in range(len(ends) - 1))
        blk_in = g * per + jj - start
        valid = (total > j).astype(I32)
        maps_ref[0, j] = blk_in
        maps_ref[1, j] = jnp.where(valid == 1, blk_in, len(counts) * per)
        maps_ref[2, j] = g
        maps_ref[3, j] = valid


def _route_rows(x1, gn_ref, wrh_ref, wrl_ref, br_ref):
    tm = x1.shape[0]
    xn = _rms(x1, gn_ref[...])
    xh = xn.astype(BF16)
    xl = (xn - xh.astype(F32)).astype(BF16)
    logits = _dot(xh, wrh_ref[...]) + _dot(xl, wrh_ref[...]) + _dot(xh, wrl_ref[...]) + br_ref[...]
    lt = logits.T
    ge = N_EXPERTS // EXPERTS_PER_GROUP
    sub = lax.broadcasted_iota(I32, (EXPERTS_PER_GROUP, tm), 0).astype(F32)
    big = jnp.float32(LANES)
    neg = -jnp.inf
    gl = jnp.where(sub < N_EXPERT_GROUPS, lt[N_EXPERTS:N_EXPERTS + EXPERTS_PER_GROUP], neg)
    gmax = jnp.max(gl, axis=0, keepdims=True)
    g_val = 1.0 / jnp.sum(jnp.exp(gl - gmax), axis=0, keepdims=True)
    g_idx = jnp.min(jnp.where(gl == gmax, sub, big), axis=0, keepdims=True)
    el = lt[0:EXPERTS_PER_GROUP]
    for g in range(1, ge):
        el = jnp.where(g_idx == g, lt[g * EXPERTS_PER_GROUP:(g + 1) * EXPERTS_PER_GROUP], el)
    e1 = jnp.max(el, axis=0, keepdims=True)
    i1 = jnp.min(jnp.where(el == e1, sub, big), axis=0, keepdims=True)
    el2 = jnp.where(sub == i1, neg, el)
    e2 = jnp.max(el2, axis=0, keepdims=True)
    i2 = jnp.min(jnp.where(el2 == e2, sub, big), axis=0, keepdims=True)
    t = jnp.exp(e2 - e1)
    w1 = 1.0 / (1.0 + t)
    w2 = t / (1.0 + t)
    wts_t = g_val * (jnp.where(sub == i1, w1, 0.0) + jnp.where(sub == i2, w2, 0.0))
    wts = jnp.concatenate([wts_t, jnp.zeros((LANES - EXPERTS_PER_GROUP, tm), F32)], axis=0).T
    return xn, wts, g_idx


def _route_begin(cntc_in, cntr_in, xs_hbm, pay, cntc, cntr, sem):
    i = pl.program_id(0)
    slot = i % 2

    @pl.when(i == 0)
    def _():
        cntc[...] = cntc_in[...]
        cntr[...] = cntr_in[...]

    @pl.when(i >= 2)
    def _():
        pltpu.make_async_copy(pay.at[slot], xs_hbm.at[pl.ds(0, pay.shape[1]), :], sem.at[slot]).wait()


def _route_scatter(final, cap, parts, triu_ref, dest_ref, cntc_out, cntr_out, xs_hbm, maps_ref,
                   pay, zrows, dest_v, dest_s, cntc, cntr, sem):
    i = pl.program_id(0)
    n = pl.num_programs(0)
    slot = i % 2
    tm = pay.shape[1]

    def wait_slot(s):
        pltpu.make_async_copy(pay.at[s], xs_hbm.at[pl.ds(0, tm), :], sem.at[s]).wait()

    r0 = 0
    for xn, wts, _ in parts:
        pay[slot, r0:r0 + xn.shape[0], :PAY_X] = xn
        pay[slot, r0:r0 + xn.shape[0], PAY_X:] = wts
        r0 += xn.shape[0]
    g_idx = jnp.concatenate([p[2] for p in parts], axis=1)

    grp_t = lax.broadcasted_iota(I32, (LANES, tm), 0).astype(F32)
    oht = (grp_t == g_idx).astype(F32)
    rank = _dot(oht.astype(BF16), triu_ref[...])
    grp = lax.broadcasted_iota(I32, (LANES, 1), 0).astype(F32)
    base = grp * float(cap) + cntc[:, 0:1]
    dest = jnp.sum(oht * (rank + base), axis=0, keepdims=True).astype(I32)
    cntc[...] = cntc[...] + jnp.sum(oht, axis=1, keepdims=True)
    dest_ref[0] = dest
    dest_v[...] = dest
    pltpu.sync_copy(dest_v, dest_s)

    for s in range(2):
        @pl.when(slot == s)
        def _():
            for r in range(tm):
                pltpu.make_async_copy(pay.at[s, pl.ds(r, 1), :],
                                      xs_hbm.at[pl.ds(dest_s[0, r], 1), :], sem.at[s]).start()

    @pl.when(i == n - 1)
    def _():
        cntc_out[...] = cntc[...]
        cntr[...] = cntc[...].T[0:8, :]
        cntr_out[...] = cntr[...]

        @pl.when(n >= 2)
        def _():
            wait_slot(1 - slot)

        wait_slot(slot)
        dest_v[:, 0:LANES] = cntr[0:1, :].astype(I32)
        pltpu.sync_copy(dest_v, dest_s)
        counts = [dest_s[0, g] for g in range(N_EXPERT_GROUPS)]
        _write_step_table(maps_ref, counts, cap)
        if final:
            zrows[...] = jnp.zeros_like(zrows)
            starts = []
            for g in range(N_EXPERT_GROUPS):
                c_g = counts[g]
                starts.append(pl.multiple_of(g * cap + lax.shift_left(lax.shift_right_logical(c_g + 7, 3), 3), 8))
                for k in range(7):
                    pltpu.make_async_copy(zrows.at[pl.ds(0, 1), :],
                                          xs_hbm.at[pl.ds(g * cap + c_g + k, 1), :], sem.at[1]).start()
            for _ in range(7 * N_EXPERT_GROUPS):
                pltpu.make_async_copy(zrows.at[pl.ds(0, 1), :], xs_hbm.at[pl.ds(0, 1), :], sem.at[1]).wait()
            for g in range(N_EXPERT_GROUPS):
                pltpu.make_async_copy(zrows, xs_hbm.at[pl.ds(starts[g], MOE_TILE), :], sem.at[0]).start()
            for g in range(N_EXPERT_GROUPS):
                pltpu.make_async_copy(zrows, xs_hbm.at[pl.ds(0, MOE_TILE), :], sem.at[0]).wait()


RouteRefs = collections.namedtuple(
    "RouteRefs", "gn wrh wrl br triu cntc_in cntr_in x1 dest cntc_out cntr_out xs maps "
                 "pay zrows dest_v dest_s cntc cntr sem")


def _route_refs(rest, has_prev):
    if has_prev:
        rest = rest[:7] + rest[8:]
    return RouteRefs(*rest)


def _route_tile(final, cap, r, x1_parts):
    parts = [_route_rows(x1, r.gn, r.wrh, r.wrl, r.br) for x1 in x1_parts]
    _route_scatter(final, cap, parts, r.triu, r.dest, r.cntc_out, r.cntr_out, r.xs, r.maps,
                   r.pay, r.zrows, r.dest_v, r.dest_s, r.cntc, r.cntr, r.sem)


def _oproj_kernel(has_prev, final, cap, *refs):
    (o_ref, x_ref, wo_ref, bo_ref) = refs[:4]
    r = _route_refs(refs[4:], has_prev)
    _route_begin(r.cntc_in, r.cntr_in, r.xs, r.pay, r.cntc, r.cntr, r.sem)
    rows = x_ref.shape[0] // ROUTE_PARTS
    x1_parts = []
    for h in range(ROUTE_PARTS):
        rs = slice(h * rows, (h + 1) * rows)
        x1 = x_ref[rs, :] + _dot(o_ref[rs, :].astype(BF16), wo_ref[...]) + bo_ref[...]
        r.x1[rs, :] = x1
        x1_parts.append(x1)
    _route_tile(final, cap, r, x1_parts)


def _gelu_tanh(x):
    return x * (0.5 * (1.0 + jnp.tanh(math.sqrt(2.0 / math.pi) * (x + 0.044715 * (x * x * x)))))


def _glu_kernel(has_prev, final, cap, *refs):
    (y_ref, x_ref, gm_ref, d_ref, wa_ref, wb_ref) = refs[:6]
    r = _route_refs(refs[6:], has_prev)
    _route_begin(r.cntc_in, r.cntr_in, r.xs, r.pay, r.cntc, r.cntr, r.sem)
    rows = x_ref.shape[0] // ROUTE_PARTS
    gated = []
    for h in range(ROUTE_PARTS):
        rs = slice(h * rows, (h + 1) * rows)
        x = x_ref[rs, :]
        z = _gelu_tanh(y_ref[rs, :] + d_ref[...] * _rms(x, gm_ref[...])).astype(BF16)
        gated.append((rs, x, _dot(z, wa_ref[...]), _dot(z, wb_ref[...])))
    x1_parts = []
    for rs, x, a, b in gated:
        x1 = x + a * _sigmoid(b)
        r.x1[rs, :] = x1
        x1_parts.append(x1)
    _route_tile(final, cap, r, x1_parts)


def _row_spec(tm, width):
    return pl.BlockSpec((tm, width), lambda i: (i, 0))


def _const_spec(shape):
    return pl.BlockSpec(shape, lambda i: (0,) * len(shape))


def _mixer_call(body, name, lead_args, lead_specs, x2d, router, triu, cnt, xs_prev, final, cap):
    t = x2d.shape[0]
    tm = TOKEN_TILE
    n_tiles = t // tm
    gn, wrh, wrl, br = router
    cntc, cntr = cnt
    has_prev = xs_prev is not None
    in_specs = lead_specs + [
        _const_spec((1, D_MODEL)), _const_spec((D_MODEL, LANES)), _const_spec((D_MODEL, LANES)),
        _const_spec((1, LANES)), _const_spec((tm, tm)), _const_spec((LANES, LANES)), _const_spec((8, LANES)),
    ]
    args = list(lead_args) + [gn, wrh, wrl, br, triu, cntc, cntr]
    aliases = {}
    if has_prev:
        in_specs.append(pl.BlockSpec(memory_space=pl.ANY))
        args.append(xs_prev)
        aliases = {len(args) - 1: 4}
    x1, dest, cntc2, cntr2, xs, maps = pl.pallas_call(
        functools.partial(body, has_prev, final, cap),
        grid=(n_tiles,),
        in_specs=in_specs,
        out_specs=[
            _row_spec(tm, D_MODEL),
            pl.BlockSpec((1, 1, tm), lambda i: (i, 0, 0)),
            _const_spec((LANES, LANES)), _const_spec((8, LANES)),
            pl.BlockSpec(memory_space=pl.ANY),
            pl.BlockSpec(memory_space=pltpu.SMEM),
        ],
        out_shape=[
            jax.ShapeDtypeStruct((t, D_MODEL), F32),
            jax.ShapeDtypeStruct((n_tiles, 1, tm), I32),
            jax.ShapeDtypeStruct((LANES, LANES), F32),
            jax.ShapeDtypeStruct((8, LANES), F32),
            jax.ShapeDtypeStruct((N_EXPERT_GROUPS * cap, PAY_W), F32),
            jax.ShapeDtypeStruct((4, _expert_steps(cap)), I32),
        ],
        scratch_shapes=[
            pltpu.VMEM((2, tm, PAY_W), F32),
            pltpu.VMEM((MOE_TILE, PAY_W), F32),
            pltpu.VMEM((1, tm), I32),
            pltpu.SMEM((1, tm), I32),
            pltpu.VMEM((LANES, LANES), F32),
            pltpu.VMEM((8, LANES), F32),
            pltpu.SemaphoreType.DMA((2,)),
        ],
        input_output_aliases=aliases,
        compiler_params=_cparams(("arbitrary",)),
        name=name,
    )(*args)
    return x1, dest, (cntc2, cntr2), xs, maps


def _oproj_call(o2d, x2d, wo_bf, bo, router, triu, cnt, xs_prev, final, cap):
    tm = TOKEN_TILE
    lead_specs = [_row_spec(tm, Q_DIM), _row_spec(tm, D_MODEL), _const_spec((Q_DIM, D_MODEL)),
                  _const_spec((1, D_MODEL))]
    return _mixer_call(_oproj_kernel, "oproj_route", [o2d, x2d, wo_bf, bo], lead_specs, x2d,
                       router, triu, cnt, xs_prev, final, cap)


def _glu_call(y2d, x2d, gm, d, wa_bf, wb_bf, router, triu, cnt, xs_prev, final, cap):
    tm = TOKEN_TILE
    lead_specs = [_row_spec(tm, D_MODEL), _row_spec(tm, D_MODEL), _const_spec((1, D_MODEL)),
                  _const_spec((1, D_MODEL)), _const_spec((D_MODEL, D_MODEL)),
                  _const_spec((D_MODEL, D_MODEL))]
    return _mixer_call(_glu_kernel, "glu_route", [y2d, x2d, gm, d, wa_bf, wb_bf], lead_specs, x2d,
                       router, triu, cnt, xs_prev, final, cap)


def _expert_kernel(maps_ref, xs_ref, wg_ref, wu_ref, wd_ref, ys_ref, wgu, wdn, hid):
    j = pl.program_id(0)
    f = EXPERT_FF
    changed = (j == 0) | (maps_ref[2, j] != maps_ref[2, jnp.maximum(j - 1, 0)])
    valid = maps_ref[3, j]

    @pl.when(changed)
    def _():
        for e in range(EXPERTS_PER_GROUP):
            wgu[e, :, :f] = wg_ref[e].astype(BF16)
            wgu[e, :, f:] = wu_ref[e].astype(BF16)
            wdn[e * f:(e + 1) * f, :] = wd_ref[e].astype(BF16)

    @pl.when(valid == 1)
    def _():
        x = xs_ref[:, :PAY_X].astype(BF16)
        wts = xs_ref[:, PAY_X:]
        for e in range(EXPERTS_PER_GROUP):
            gu = _dot(x, wgu[e])
            g, u = gu[:, :f], gu[:, f:]
            hid[:, e * f:(e + 1) * f] = ((g * _sigmoid(g)) * u * wts[:, e:e + 1]).astype(BF16)
        ys_ref[...] = _dot(hid[...], wdn[...])

    @pl.when(valid == 0)
    def _():
        ys_ref[...] = jnp.zeros_like(ys_ref)


def _expert_call(maps, xs, w_gate, w_up, w_down, layer, cap):
    n_steps = maps.shape[1]
    e, f = EXPERTS_PER_GROUP, EXPERT_FF
    w_gate = w_gate.reshape(-1, D_MODEL, f)
    w_up = w_up.reshape(-1, D_MODEL, f)
    w_down = w_down.reshape(-1, f, D_MODEL)
    w_blk = lambda j, m: (m[2, j] + layer * N_EXPERT_GROUPS, 0, 0)
    grid_spec = pltpu.PrefetchScalarGridSpec(
        num_scalar_prefetch=1,
        grid=(n_steps,),
        in_specs=[
            pl.BlockSpec((MOE_TILE, PAY_W), lambda j, m: (m[0, j], 0)),
            pl.BlockSpec((e, D_MODEL, f), w_blk),
            pl.BlockSpec((e, D_MODEL, f), w_blk),
            pl.BlockSpec((e, f, D_MODEL), w_blk),
        ],
        out_specs=pl.BlockSpec((MOE_TILE, D_MODEL), lambda j, m: (m[1, j], 0)),
        scratch_shapes=[
            pltpu.VMEM((e, D_MODEL, 2 * f), BF16),
            pltpu.VMEM((e * f, D_MODEL), BF16),
            pltpu.VMEM((MOE_TILE, e * f), BF16),
        ],
    )
    return pl.pallas_call(
        _expert_kernel,
        grid_spec=grid_spec,
        out_shape=jax.ShapeDtypeStruct((N_EXPERT_GROUPS * cap + MOE_TILE, D_MODEL), F32),
        compiler_params=_cparams(("arbitrary",)),
        name="moe_experts",
    )(maps, xs, w_gate, w_up, w_down)


def _combine_kernel(emit_x, dest_ref, x_ref, gnext_ref, ys_hbm, *rest):
    outs, (ybuf, sem) = rest[:-2], rest[-2:]
    i = pl.program_id(0)
    n_tiles = pl.num_programs(0) - 1
    slot = i % 2
    tm = x_ref.shape[0]

    for s in range(2):
        @pl.when((i < n_tiles) & (slot == s))
        def _():
            for r in range(tm):
                pltpu.make_async_copy(ys_hbm.at[pl.ds(dest_ref[0, 0, r], 1), :],
                                      ybuf.at[s, pl.ds(r, 1), :], sem.at[s]).start()

    @pl.when(i >= 1)
    def _():
        prev = 1 - slot
        pltpu.make_async_copy(ys_hbm.at[pl.ds(0, tm), :], ybuf.at[prev], sem.at[prev]).wait()
        x2 = x_ref[...] + ybuf[prev]
        normed = _rms(x2, gnext_ref[...])
        if emit_x:
            outs[0][...] = x2
            outs[1][...] = normed
        else:
            outs[0][...] = normed


def _combine_call(dest, x2d, ys, gnext, emit_x):
    t = x2d.shape[0]
    tm = COMBINE_TILE
    n_tiles = t // tm
    dest = dest.reshape(n_tiles, 1, tm)
    n_out = 2 if emit_x else 1
    done = lambda i: (jnp.maximum(i - 1, 0), 0)
    return pl.pallas_call(
        functools.partial(_combine_kernel, emit_x),
        grid=(n_tiles + 1,),
        in_specs=[
            pl.BlockSpec((1, 1, tm), lambda i: (jnp.minimum(i, n_tiles - 1), 0, 0),
                         memory_space=pltpu.SMEM),
            pl.BlockSpec((tm, D_MODEL), done), _const_spec((1, D_MODEL)),
            pl.BlockSpec(memory_space=pl.ANY),
        ],
        out_specs=[pl.BlockSpec((tm, D_MODEL), done)] * n_out,
        out_shape=[jax.ShapeDtypeStruct((t, D_MODEL), F32)] * n_out,
        scratch_shapes=[pltpu.VMEM((2, tm, D_MODEL), F32), pltpu.SemaphoreType.DMA((2,))],
        compiler_params=_cparams(("arbitrary",)),
        name="moe_combine",
    )(dest, x2d, gnext, ys)


def _s5_state_in(u_ref_val, wre_ref, wim_ref, store_re, store_im, pair_w):
    for m in range(S5_GB // 2):
        up = u_ref_val(m * pair_w, pair_w)
        store_re(m, _dot(up, wre_ref[m]))
        store_im(m, _dot(up, wim_ref[m]))


def _s5_outputs(u_ref_val, hre, him, m_ref, gre_ref, gim_ref, y_store, pair_w):
    gw = pair_w // 2
    for m in range(S5_GB // 2):
        hr = hre(m).astype(BF16)
        hi = him(m).astype(BF16)
        y = _dot(hr, gre_ref[m]) + _dot(hi, gim_ref[m])
        y0 = y[:, :gw] + _dot(u_ref_val(m * pair_w, gw), m_ref[2 * m])
        y1 = y[:, gw:] + _dot(u_ref_val(m * pair_w + gw, gw), m_ref[2 * m + 1])
        y_store(m * pair_w, gw, y0)
        y_store(m * pair_w + gw, gw, y1)


def _s5_flatten(load_rows, q, n, ut, uflat):
    gc = SSM_GROUP_CH
    qc = q * gc
    for s in range(q):
        ut[:, s * gc:(s + 1) * gc, :] = load_rows(s).astype(BF16).T.reshape(S5_GB, gc, n)
    for g in range(S5_GB):
        uflat[:, g * qc:(g + 1) * qc] = ut[g].T


def _s5_unflatten(yflat, yt, store_rows, q, n):
    gc = SSM_GROUP_CH
    qc = q * gc
    for g in range(S5_GB):
        yt[g] = yflat[:, g * qc:(g + 1) * qc].T
    for t in range(q):
        store_rows(t, yt[:, t * gc:(t + 1) * gc, :].reshape(S5_GB * gc, n).T)


def _s5_prompt_kernel(u_ref, wre_ref, wim_ref, m_ref, gre_ref, gim_ref, aqr_ref, aqi_ref,
                      y_ref, her_ref, hei_ref, ut, uflat, sre, sim, hre, him, yflat, yt):
    pair_w = 2 * S5_CHUNK * SSM_GROUP_CH
    n_chunks = u_ref.shape[1] // S5_CHUNK
    n_pairs = S5_GB // 2
    seq_rows = lambda b: pl.ds(b, n_chunks, stride=S5_SEQS)
    for b in range(S5_SEQS):
        _s5_flatten(lambda s: u_ref[b, pl.ds(s, n_chunks, stride=S5_CHUNK), :],
                    S5_CHUNK, n_chunks, ut, uflat.at[b])
        u_val = lambda off, w: uflat[b, :, off:off + w]

        def store_re(m, val):
            sre[m, seq_rows(b), :] = val

        def store_im(m, val):
            sim[m, seq_rows(b), :] = val

        _s5_state_in(u_val, wre_ref, wim_ref, store_re, store_im, pair_w)
    ar = [aqr_ref[:, m * LANES:(m + 1) * LANES] for m in range(n_pairs)]
    ai = [aqi_ref[:, m * LANES:(m + 1) * LANES] for m in range(n_pairs)]

    def step(n, carry):
        rows = pl.ds(pl.multiple_of(n * S5_SEQS, S5_SEQS), S5_SEQS)
        out = []
        for m in range(n_pairs):
            hr, hi = carry[2 * m], carry[2 * m + 1]
            hre[m, rows, :] = hr
            him[m, rows, :] = hi
            out.append(ar[m] * hr - ai[m] * hi + sre[m, rows, :])
            out.append(ar[m] * hi + ai[m] * hr + sim[m, rows, :])
        return tuple(out)

    zero = jnp.zeros((S5_SEQS, LANES), F32)
    last = lax.fori_loop(0, n_chunks, step, (zero,) * (2 * n_pairs))
    for b in range(S5_SEQS):
        for m in range(n_pairs):
            her_ref[b, :, m * LANES:(m + 1) * LANES] = last[2 * m][b:b + 1]
            hei_ref[b, :, m * LANES:(m + 1) * LANES] = last[2 * m + 1][b:b + 1]

    def y_store(off, w, val):
        yflat[:, off:off + w] = val

    for b in range(S5_SEQS):
        u_val = lambda off, w: uflat[b, :, off:off + w]
        _s5_outputs(u_val, lambda m: hre[m, seq_rows(b), :], lambda m: him[m, seq_rows(b), :],
                    m_ref, gre_ref, gim_ref, y_store, pair_w)

        def store_rows(t, val):
            y_ref[b, pl.ds(t, n_chunks, stride=S5_CHUNK), :] = val

        _s5_unflatten(yflat, yt, store_rows, S5_CHUNK, n_chunks)


def _s5_sample_kernel(n_new, u_ref, h0r_ref, h0i_ref, wre_ref, wim_ref, m_ref, gre_ref, gim_ref,
                      aqr_ref, aqi_ref, y_ref, hnr_ref, hni_ref, ut, uflat, sre, sim, yflat, yt):
    pair_w = 2 * n_new * SSM_GROUP_CH
    seqs = h0r_ref.shape[0]
    _s5_flatten(lambda s: u_ref[pl.ds(s, seqs, stride=n_new), :], n_new, seqs, ut, uflat)
    u_val = lambda off, w: uflat[:, off:off + w]

    def store_re(m, val):
        sre[:, m * LANES:(m + 1) * LANES] = val

    def store_im(m, val):
        sim[:, m * LANES:(m + 1) * LANES] = val

    _s5_state_in(u_val, wre_ref, wim_ref, store_re, store_im, pair_w)
    ar = aqr_ref[...]
    ai = aqi_ref[...]
    h0r = h0r_ref[...]
    h0i = h0i_ref[...]
    hnr_ref[...] = ar * h0r - ai * h0i + sre[...]
    hni_ref[...] = ar * h0i + ai * h0r + sim[...]

    def y_store(off, w, val):
        yflat[:, off:off + w] = val

    _s5_outputs(u_val, lambda m: h0r_ref[:, m * LANES:(m + 1) * LANES],
                lambda m: h0i_ref[:, m * LANES:(m + 1) * LANES],
                m_ref, gre_ref, gim_ref, y_store, pair_w)

    def store_rows(t, val):
        y_ref[pl.ds(t, seqs, stride=n_new), :] = val

    _s5_unflatten(yflat, yt, store_rows, n_new, seqs)


def _s5_weight_specs(q, idx):
    qc = q * SSM_GROUP_CH
    np_ = S5_GB // 2
    st = S5_GB * SSM_STATE
    return [
        pl.BlockSpec((np_, 2 * qc, LANES), lambda *a: (idx(*a), 0, 0)),
        pl.BlockSpec((np_, 2 * qc, LANES), lambda *a: (idx(*a), 0, 0)),
        pl.BlockSpec((S5_GB, qc, qc), lambda *a: (idx(*a), 0, 0)),
        pl.BlockSpec((np_, LANES, 2 * qc), lambda *a: (idx(*a), 0, 0)),
        pl.BlockSpec((np_, LANES, 2 * qc), lambda *a: (idx(*a), 0, 0)),
        pl.BlockSpec((1, st), lambda *a: (0, idx(*a))),
        pl.BlockSpec((1, st), lambda *a: (0, idx(*a))),
    ]


def _s5_prompt_call(u, w):
    b, seq, _ = u.shape
    n_chunks = seq // S5_CHUNK
    gbl = S5_GB * SSM_GROUP_CH
    qc = S5_CHUNK * SSM_GROUP_CH
    st = S5_GB * SSM_STATE
    n_gb = SSM_GROUPS // S5_GB
    gb_of = lambda g, i: g
    tok = pl.BlockSpec((S5_SEQS, seq, gbl), lambda g, i: (i, 0, g))
    slab = pltpu.VMEM((S5_GB // 2, n_chunks * S5_SEQS, LANES), F32)
    return pl.pallas_call(
        _s5_prompt_kernel,
        grid=(n_gb, b // S5_SEQS),
        in_specs=[tok] + _s5_weight_specs(S5_CHUNK, gb_of),
        out_specs=[
            tok,
            pl.BlockSpec((S5_SEQS, 1, st), lambda g, i: (i, 0, g)),
            pl.BlockSpec((S5_SEQS, 1, st), lambda g, i: (i, 0, g)),
        ],
        out_shape=[
            jax.ShapeDtypeStruct((b, seq, D_MODEL), F32),
            jax.ShapeDtypeStruct((b, 1, SSM_GROUPS * SSM_STATE), F32),
            jax.ShapeDtypeStruct((b, 1, SSM_GROUPS * SSM_STATE), F32),
        ],
        scratch_shapes=[
            pltpu.VMEM((S5_GB, qc, n_chunks), BF16),
            pltpu.VMEM((S5_SEQS, n_chunks, S5_GB * qc), BF16),
            slab, slab, slab, slab,
            pltpu.VMEM((n_chunks, S5_GB * qc), F32),
            pltpu.VMEM((S5_GB, qc, n_chunks), F32),
        ],
        compiler_params=_cparams(("parallel", "parallel")),
        name="s5_prompt",
    )(u, *w)


def _s5_sample_call(u2d, h0r, h0i, w, n_new):
    t = u2d.shape[0]
    b = t // n_new
    gbl = S5_GB * SSM_GROUP_CH
    qc = n_new * SSM_GROUP_CH
    st = S5_GB * SSM_STATE
    n_gb = SSM_GROUPS // S5_GB
    gb_of = lambda g: g
    state = pl.BlockSpec((b, st), lambda g: (0, g))
    tok = pl.BlockSpec((t, gbl), lambda g: (0, g))
    return pl.pallas_call(
        functools.partial(_s5_sample_kernel, n_new),
        grid=(n_gb,),
        in_specs=[tok, state, state] + _s5_weight_specs(n_new, gb_of),
        out_specs=[tok, state, state],
        out_shape=[
            jax.ShapeDtypeStruct((t, D_MODEL), F32),
            jax.ShapeDtypeStruct((b, SSM_GROUPS * SSM_STATE), F32),
            jax.ShapeDtypeStruct((b, SSM_GROUPS * SSM_STATE), F32),
        ],
        scratch_shapes=[
            pltpu.VMEM((S5_GB, qc, b), BF16),
            pltpu.VMEM((b, S5_GB * qc), BF16),
            pltpu.VMEM((b, st), F32), pltpu.VMEM((b, st), F32),
            pltpu.VMEM((b, S5_GB * qc), F32),
            pltpu.VMEM((S5_GB, qc, b), F32),
        ],
        compiler_params=_cparams(("parallel",)),
        name="s5_sample",
    )(u2d, h0r, h0i, *w)


def _s5_discretize(a_re, a_im, log_dt, b_re, b_im):
    delta = jnp.exp(log_dt.astype(F32))[:, None]
    lr, li = a_re.astype(F32), a_im.astype(F32)
    mag = jnp.exp(delta * lr)
    abar_r = mag * jnp.cos(delta * li)
    abar_i = mag * jnp.sin(delta * li)
    nr, ni = abar_r - 1.0, abar_i
    den = lr * lr + li * li
    coef_r = ((nr * lr + ni * li) / den)[..., None]
    coef_i = ((ni * lr - nr * li) / den)[..., None]
    br, bi = b_re.astype(F32), b_im.astype(F32)
    return delta * lr, delta * li, coef_r * br - coef_i * bi, coef_r * bi + coef_i * br


def _split2(x):
    x1 = x.astype(BF16)
    return x1, (x - x1.astype(F32)).astype(BF16)


def _s5_prep_kernel(qs, bbr_ref, bbi_ref, ctr_ref, cti_ref, ppr_ref, ppi_ref, pnr_ref, pni_ref,
                    *out_refs):
    c = SSM_GROUP_CH
    qc = max(qs) * c
    p2 = 2 * SSM_STATE
    col = lax.broadcasted_iota(I32, (c, qc), 1)
    row = lax.broadcasted_iota(I32, (c, qc), 0)
    by_ch = (col % c == row).astype(BF16)
    by_pos = (col // c == row).astype(BF16)

    def spread(x, sel):
        return sum(_dot(part, sel) for part in _split2(x))

    def dot_f32(a, b):
        a1, a2 = _split2(a)
        b1, b2 = _split2(b)
        return _dot(a1, b1) + _dot(a1, b2) + _dot(a2, b1)

    bc_r, bc_i = spread(bbr_ref[0], by_ch), spread(bbi_ref[0], by_ch)
    cc_r, cc_i = spread(ctr_ref[0], by_ch), spread(cti_ref[0], by_ch)
    pp_r, pp_i = spread(ppr_ref[0], by_pos), spread(ppi_ref[0], by_pos)
    pn_r, pn_i = spread(pnr_ref[0], by_pos), spread(pni_ref[0], by_pos)
    r_r = cc_r * pp_r - cc_i * pp_i
    r_i = cc_r * pp_i + cc_i * pp_r
    l_r = bc_r * pn_r - bc_i * pn_i
    l_i = bc_r * pn_i + bc_i * pn_r
    lt_r, lt_i = l_r.T, l_i.T
    grp_col = lax.broadcasted_iota(I32, (qc, p2), 1) // SSM_STATE
    causal =(lax.broadcasted_iota(I32, (qc, qc), 1) // c) >= (lax.broadcasted_iota(I32, (qc, qc), 0) // c)
    kers = [jnp.where(causal, dot_f32(jnp.where(grp_col == gl, lt_r, 0.0), r_r)
                      - dot_f32(jnp.where(grp_col == gl, lt_i, 0.0), r_i), 0.0) for gl in range(2)]
    for n, q in enumerate(qs):
        wre_ref, wim_ref, m_ref, gre_ref, gim_ref = out_refs[5 * n:5 * n + 5]
        w = q * c
        aq_r = ppr_ref[0][:, q - 1:q]
        aq_i = ppi_ref[0][:, q - 1:q]
        w_r = (l_r[:, :w] * aq_r - l_i[:, :w] * aq_i).T
        w_i = (l_r[:, :w] * aq_i + l_i[:, :w] * aq_r).T
        g_re, g_im = [], []
        col_grp = lax.broadcasted_iota(I32, (w, p2), 1) // SSM_STATE
        row_grp = lax.broadcasted_iota(I32, (p2, w), 0) // SSM_STATE
        for gl in range(2):
            wre_ref[0, gl * w:(gl + 1) * w, :] = jnp.where(col_grp == gl, w_r, 0.0).astype(BF16)
            wim_ref[0, gl * w:(gl + 1) * w, :] = jnp.where(col_grp == gl, w_i, 0.0).astype(BF16)
            m_ref[gl] = kers[gl][:w, :w].astype(BF16)
            g_re.append(jnp.where(row_grp == gl, r_r[:, :w], 0.0))
            g_im.append(jnp.where(row_grp == gl, -r_i[:, :w], 0.0))
        gre_ref[0] = jnp.concatenate(g_re, axis=1).astype(BF16)
        gim_ref[0] = jnp.concatenate(g_im, axis=1).astype(BF16)


def _s5_chunk_weights(tables, qs):
    n_pairs = SSM_GROUPS // 2
    p2 = 2 * SSM_STATE
    tab = pl.BlockSpec((1, p2, SSM_GROUP_CH), lambda m: (m, 0, 0))
    out_specs, out_shape = [], []
    for q in qs:
        qc = q * SSM_GROUP_CH
        out_specs += [pl.BlockSpec((1, 2 * qc, p2), lambda m: (m, 0, 0)),
                      pl.BlockSpec((1, 2 * qc, p2), lambda m: (m, 0, 0)),
                      pl.BlockSpec((2, qc, qc), lambda m: (m, 0, 0)),
                      pl.BlockSpec((1, p2, 2 * qc), lambda m: (m, 0, 0)),
                      pl.BlockSpec((1, p2, 2 * qc), lambda m: (m, 0, 0))]
        out_shape += [jax.ShapeDtypeStruct((n_pairs, 2 * qc, p2), BF16),
                      jax.ShapeDtypeStruct((n_pairs, 2 * qc, p2), BF16),
                      jax.ShapeDtypeStruct((SSM_GROUPS, qc, qc), BF16),
                      jax.ShapeDtypeStruct((n_pairs, p2, 2 * qc), BF16),
                      jax.ShapeDtypeStruct((n_pairs, p2, 2 * qc), BF16)]
    outs = pl.pallas_call(
        functools.partial(_s5_prep_kernel, tuple(qs)),
        grid=(n_pairs,),
        in_specs=[tab] * 8,
        out_specs=out_specs, out_shape=out_shape,
        compiler_params=_cparams(("parallel",)),
        name="s5_operators",
    )(*tables)
    return [tuple(outs[5 * n:5 * n + 5]) for n in range(len(qs))]


def _s5_tables(log_mag, phase, bbar_r, bbar_i, c_re, c_im, q_max):
    k = jnp.arange(1, q_max + 1, dtype=F32)[None, None, :]
    mag = jnp.exp(log_mag[:, :, None] * k)
    cos, sin = jnp.cos(phase[:, :, None] * k), jnp.sin(phase[:, :, None] * k)
    tabs = (bbar_r, bbar_i, c_re.astype(F32).transpose(0, 2, 1), c_im.astype(F32).transpose(0, 2, 1),
            mag * cos, mag * sin, cos / mag, -sin / mag)
    return tuple(t.reshape(SSM_GROUPS // 2, 2 * SSM_STATE, SSM_GROUP_CH) for t in tabs)


def _rope_tables(pos):
    half = HEAD_DIM // 2
    inv = 1.0 / (ROPE_THETA ** (jnp.arange(half, dtype=F32) * (2.0 / HEAD_DIM)))
    ang = pos.astype(F32)[:, None] * inv[None, :]
    cos, sin = jnp.cos(ang), jnp.sin(ang)
    return jnp.tile(cos, (1, 4)), jnp.concatenate([-sin, sin, -sin, sin], axis=1)


def _router_weights(gain, w_rg, b_rg, w_re, b_re):
    n_layers = gain.shape[0]
    pad = LANES - N_EXPERTS - N_EXPERT_GROUPS
    w = jnp.concatenate([w_re, w_rg, jnp.zeros((n_layers, D_MODEL, pad), F32)], axis=2)
    b = jnp.concatenate([b_re, b_rg, jnp.zeros((n_layers, pad), F32)], axis=1)
    wh = w.astype(BF16)
    wl = (w - wh.astype(F32)).astype(BF16)
    return [(gain[l].reshape(1, -1).astype(F32), wh[l], wl[l], b[l].reshape(1, LANES))
            for l in range(n_layers)]


def kernel(x_prompt, x_sample, cache_k, cache_v, state_ssm_re, state_ssm_im, norm_mix, norm_ffn, norm_final, attn_w_qkv, attn_b_qkv, attn_w_o, attn_b_o, attn_sinks, ssm_a_re, ssm_a_im, ssm_log_dt, ssm_b_re, ssm_b_im, ssm_c_re, ssm_c_im, ssm_d, ssm_w_glu_a, ssm_w_glu_b, moe_w_router_group, moe_b_router_group, moe_w_router_expert, moe_b_router_expert, moe_w_gate, moe_w_up, moe_w_down):
    bsz, seq, _ = x_prompt.shape
    dbs, n_new, _ = x_sample.shape
    rows = cache_k.shape[2]
    tp, ts = bsz * seq, dbs * n_new
    cap = _moe_cap(tp + ts)
    xp = x_prompt.reshape(tp, D_MODEL)
    xs = x_sample.reshape(ts, D_MODEL)

    row1 = lambda v: v.reshape(1, -1).astype(F32)
    routers = _router_weights(norm_ffn, moe_w_router_group, moe_b_router_group,
                              moe_w_router_expert, moe_b_router_expert)
    triu = jnp.triu(jnp.ones((TOKEN_TILE, TOKEN_TILE), F32), 1).astype(BF16)
    cnt0 = (jnp.zeros((LANES, LANES), F32), jnp.zeros((8, LANES), F32))

    slots = jnp.asarray(HEAD_SLOTS, dtype=I32)

    def q_slots(w):
        lead = w.shape[:-1]
        qh = w[..., :Q_DIM].reshape(*lead, N_HEADS, HEAD_DIM)[..., slots, :].reshape(*lead, Q_DIM)
        return jnp.concatenate([qh, w[..., Q_DIM:]], axis=-1)

    wqkv = q_slots(attn_w_qkv[0]).astype(BF16)
    bqkv = row1(q_slots(attn_b_qkv[0]))
    wo = attn_w_o[0].reshape(N_HEADS, HEAD_DIM, D_MODEL)[slots].reshape(Q_DIM, D_MODEL).astype(BF16)
    bo = row1(attn_b_o[0])
    sinks = attn_sinks[0].astype(F32)[slots]
    g_mix0, g_mix1 = row1(norm_mix[0]), row1(norm_mix[1])
    pos = jnp.concatenate([jnp.arange(seq, dtype=I32),
                           jnp.tile(PAST_LEN + jnp.arange(n_new, dtype=I32), dbs)])
    cos_t, sin_t = _rope_tables(pos)

    qp, kp, vp = _qkv_call(xp, g_mix0, wqkv, bqkv, cos_t, sin_t, (0, seq), BF16)
    qs, ks, vs = _qkv_call(xs, g_mix0, wqkv, bqkv, cos_t, sin_t, (seq, ts), F32)
    op = _attn_prompt_call(jnp.repeat(sinks, BLOCK // ATTN_PARTS).reshape(1, -1),
                           qp.reshape(bsz, seq, Q_DIM),
                           kp.reshape(bsz, seq, KV_DIM),
                           vp.reshape(bsz, seq, KV_DIM))
    os_, nks, nvs = _attn_sample_call(jnp.repeat(sinks, n_new).reshape(1, -1), qs, ks, vs,
                                      cache_k[0].reshape(dbs, rows, KV_DIM),
                                      cache_v[0].reshape(dbs, rows, KV_DIM), n_new, 8)
    xp1, dest_p, cnt, rows_x, _ = _oproj_call(op.reshape(tp, Q_DIM), xp, wo, bo, routers[0], triu,
                                              cnt0, None, False, cap)
    xs1, dest_s, cnt, rows_x, steps = _oproj_call(os_, xs, wo, bo, routers[0], triu, cnt, rows_x,
                                                  True, cap)
    rows_y = _expert_call(steps, rows_x, moe_w_gate, moe_w_up, moe_w_down, 0, cap)
    xp2, up = _combine_call(dest_p, xp1, rows_y, g_mix1, True)
    xs2, us = _combine_call(dest_s, xs1, rows_y, g_mix1, True)

    disc = _s5_discretize(ssm_a_re[0], ssm_a_im[0], ssm_log_dt[0], ssm_b_re[0], ssm_b_im[0])
    tables = _s5_tables(*disc, ssm_c_re[0], ssm_c_im[0], S5_CHUNK)
    a_pow = lambda q: [tables[i][:, :, q - 1].reshape(1, -1) for i in (4, 5)]
    ops_p, ops_s = _s5_chunk_weights(tables, (S5_CHUNK, n_new))
    w_p = (*ops_p, *a_pow(S5_CHUNK))
    w_s = (*ops_s, *a_pow(n_new))
    y_p, hpr, hpi = _s5_prompt_call(up.reshape(bsz, seq, D_MODEL), w_p)
    y_p = y_p.reshape(tp, D_MODEL)
    h0r = state_ssm_re[0].reshape(dbs, -1).astype(F32)
    h0i = state_ssm_im[0].reshape(dbs, -1).astype(F32)
    y_s, hsr, hsi = _s5_sample_call(us, h0r, h0i, w_s, n_new)

    wa, wb = ssm_w_glu_a[0].astype(BF16), ssm_w_glu_b[0].astype(BF16)
    d_row = row1(ssm_d[0])
    xp3, dest_p, cnt, rows_x, _ = _glu_call(y_p, xp2, g_mix1, d_row, wa, wb, routers[1], triu,
                                            cnt0, None, False, cap)
    xs3, dest_s, cnt, rows_x, steps = _glu_call(y_s, xs2, g_mix1, d_row, wa, wb, routers[1], triu,
                                                cnt, rows_x, True, cap)
    rows_y = _expert_call(steps, rows_x, moe_w_gate, moe_w_up, moe_w_down, 1, cap)
    g_fin = row1(norm_final)
    (yp,) = _combine_call(dest_p, xp3, rows_y, g_fin, False)
    (ys,) = _combine_call(dest_s, xs3, rows_y, g_fin, False)

    kv5 = lambda a, n: a.reshape(1, n, rows, KV_HEADS, HEAD_DIM)
    st4 = lambda a, n: a.reshape(1, n, SSM_GROUPS, SSM_STATE)
    k_last = kp.reshape(bsz, seq, KV_DIM)[:, seq - WINDOW:]
    v_last = vp.reshape(bsz, seq, KV_DIM)[:, seq - WINDOW:]
    return (yp.reshape(bsz, seq, D_MODEL), ys.reshape(dbs, n_new, D_MODEL),
            k_last.reshape(1, bsz, WINDOW, KV_HEADS, HEAD_DIM), kv5(nks, dbs),
            v_last.reshape(1, bsz, WINDOW, KV_HEADS, HEAD_DIM), kv5(nvs, dbs),
            st4(hpr, bsz), st4(hsr, dbs), st4(hpi, bsz), st4(hsi, dbs))
```

```python
import collections
import functools
import math

import jax
import jax.numpy as jnp
from jax import lax
from jax.experimental import pallas as pl
from jax.experimental.pallas import tpu as pltpu

F32 = jnp.float32
BF16 = jnp.bfloat16
I32 = jnp.int32

D_MODEL = 1024
N_HEADS = 16
KV_HEADS = 4
HEAD_DIM = 64
Q_DIM = N_HEADS * HEAD_DIM
KV_DIM = KV_HEADS * HEAD_DIM
QKV_DIM = Q_DIM + 2 * KV_DIM
WINDOW = 128
BLOCK = 128
ROPE_THETA = 10000.0
PAST_LEN = 16384
SSM_GROUP_CH = 16
SSM_GROUPS = D_MODEL // SSM_GROUP_CH
SSM_STATE = 64
N_EXPERT_GROUPS = 4
EXPERTS_PER_GROUP = 8
N_EXPERTS = N_EXPERT_GROUPS * EXPERTS_PER_GROUP
EXPERT_FF = 128
NORM_EPS = 1e-5

LANES = 128
VMEM_LIMIT = 56 * 1024 * 1024
S5_CHUNK = 16
S5_GB = 8
S5_SEQS = 4
TOKEN_TILE = 1024
MOE_TILE = 512
ROUTE_PARTS = 2
COMBINE_TILE = 512
ATTN_PARTS = 4
ATTN_BLOCKS = 4
PAY_X = D_MODEL
PAY_W = PAY_X + LANES


def _cparams(sem):
    return pltpu.CompilerParams(dimension_semantics=sem, vmem_limit_bytes=VMEM_LIMIT)


def _rms(x, g):
    return x * lax.rsqrt(jnp.mean(x * x, axis=-1, keepdims=True) + NORM_EPS) * g


def _dot(a, b):
    return jnp.dot(a, b, preferred_element_type=F32)


def _dot_nt(a, b):
    return lax.dot_general(a, b, (((1,), (1,)), ((), ())), preferred_element_type=F32)


def _sigmoid(x):
    return 1.0 / (1.0 + jnp.exp(-x))


def _qkv_kernel(x_ref, g_ref, w_ref, b_ref, cos_ref, sin_ref, q_ref, k_ref, v_ref):
    xn = _rms(x_ref[...], g_ref[...])
    qkv = _dot(xn.astype(BF16), w_ref[...]) + b_ref[...]
    cos = cos_ref[...]
    sin = sin_ref[...]
    lane = lax.broadcasted_iota(I32, cos.shape, 1)
    first_half = (lane % HEAD_DIM) < (HEAD_DIM // 2)
    n_rot = (Q_DIM + KV_DIM) // LANES
    for c in range(n_rot):
        blk = qkv[:, c * LANES:(c + 1) * LANES]
        partner = jnp.where(first_half,
                            pltpu.roll(blk, LANES - HEAD_DIM // 2, 1),
                            pltpu.roll(blk, HEAD_DIM // 2, 1))
        rot = blk * cos + partner * sin
        if c < Q_DIM // LANES:
            q_ref[:, c * LANES:(c + 1) * LANES] = (rot * (HEAD_DIM ** -0.5)).astype(q_ref.dtype)
        else:
            k_ref[:, c * LANES - Q_DIM:(c + 1) * LANES - Q_DIM] = rot
    v_ref[...] = qkv[:, Q_DIM + KV_DIM:]


def _qkv_call(x2d, gain, w_bf, bias, cos_t, sin_t, pos_rows, q_dtype):
    t = x2d.shape[0]
    tm = TOKEN_TILE
    first, n_pos = pos_rows[0] // tm, pos_rows[1] // tm
    return pl.pallas_call(
        _qkv_kernel,
        grid=(t // tm,),
        in_specs=[
            pl.BlockSpec((tm, D_MODEL), lambda i: (i, 0)),
            pl.BlockSpec((1, D_MODEL), lambda i: (0, 0)),
            pl.BlockSpec((D_MODEL, QKV_DIM), lambda i: (0, 0)),
            pl.BlockSpec((1, QKV_DIM), lambda i: (0, 0)),
            pl.BlockSpec((tm, LANES), lambda i: (first + i % n_pos, 0)),
            pl.BlockSpec((tm, LANES), lambda i: (first + i % n_pos, 0)),
        ],
        out_specs=[
            pl.BlockSpec((tm, Q_DIM), lambda i: (i, 0)),
            pl.BlockSpec((tm, KV_DIM), lambda i: (i, 0)),
            pl.BlockSpec((tm, KV_DIM), lambda i: (i, 0)),
        ],
        out_shape=[
            jax.ShapeDtypeStruct((t, Q_DIM), q_dtype),
            jax.ShapeDtypeStruct((t, KV_DIM), F32),
            jax.ShapeDtypeStruct((t, KV_DIM), F32),
        ],
        compiler_params=_cparams(("parallel",)),
        name="qkv_rope",
    )(x2d, gain, w_bf, bias, cos_t, sin_t)


HEAD_SLOTS = tuple(8 * c + 4 * half + i for c in range(2) for i in range(4) for half in range(2))
LOG2E = math.log2(math.e)


def _attn_prompt_kernel(sink_ref, q_ref, *refs):
    nk = ATTN_BLOCKS + 1
    k_refs, v_refs, o_ref = refs[:nk], refs[nk:2 * nk], refs[2 * nk]
    first = pl.program_id(1) * ATTN_BLOCKS
    kb = [r[0].astype(BF16) for r in k_refs]
    vb = [r[0].astype(BF16) for r in v_refs]
    k2s = [jnp.concatenate(kb[i:i + 2], axis=0) for i in range(ATTN_BLOCKS)]
    v2s = [jnp.concatenate(vb[i:i + 2], axis=0) for i in range(ATTN_BLOCKS)]
    sub = BLOCK // ATTN_PARTS
    win = WINDOW + sub
    cols = 2 * BLOCK
    spt = cols // sub
    key = lax.broadcasted_iota(I32, (win, cols), 0)
    qry = lax.broadcasted_iota(I32, (win, cols), 1) % sub
    band = (key >= qry) & (key <= qry + WINDOW)
    masks = [[band & ((first + blk > 0) | (key + h * sub >= BLOCK)) for h in range(ATTN_PARTS)]
             for blk in range(ATTN_BLOCKS)]
    low = lax.broadcasted_iota(I32, (sub, LANES), 1) < HEAD_DIM
    keep_low = low.astype(BF16)
    keep_high = 1 - keep_low
    tasks = [(blk, grp, h) for blk in range(ATTN_BLOCKS) for grp in range(N_HEADS // spt)
             for h in range(ATTN_PARTS)]
    pairs = lambda grp: range(grp * spt // 2, (grp + 1) * spt // 2)
    chunk = lambda grp: slice((grp * spt // 8) * LANES, (grp * spt // 8 + 1) * LANES)
    q_rows = lambda blk, h: slice(blk * BLOCK + h * sub, blk * BLOCK + (h + 1) * sub)

    def scores(blk, grp, h):
        blocks = []
        for pr in pairs(grp):
            qblk = q_ref[0, q_rows(blk, h), pr * LANES:(pr + 1) * LANES]
            blocks += [qblk * keep_low, qblk * keep_high]
        qz = jnp.concatenate(blocks, axis=0)
        s = _dot_nt(k2s[blk][h * sub:h * sub + win, chunk(grp)], qz) * LOG2E
        return jnp.where(masks[blk][h], s, -jnp.inf)

    def weights(blk, grp, h, s):
        sink = sink_ref[:, grp * cols:(grp + 1) * cols] * LOG2E
        m = jnp.maximum(jnp.max(s, axis=0, keepdims=True), sink)
        p = jnp.exp2(s - m)
        denom = jnp.sum(p, axis=0, keepdims=True) + jnp.exp2(sink - m)
        rows = [jnp.zeros((h * sub, cols), F32), p * (1.0 / denom),
                jnp.zeros((2 * BLOCK - win - h * sub, cols), F32)]
        w = jnp.concatenate([r for r in rows if r.shape[0]], axis=0)
        return w.T.astype(BF16)

    def finish(blk, grp, h, w):
        oz = _dot(w, v2s[blk][:, chunk(grp)])
        for j, pr in enumerate(pairs(grp)):
            o_ref[0, q_rows(blk, h), pr * LANES:(pr + 1) * LANES] = jnp.where(
                low, oz[2 * j * sub:(2 * j + 1) * sub], oz[(2 * j + 1) * sub:(2 * j + 2) * sub]
            ).astype(o_ref.dtype)

    all_s = [scores(*t) for t in tasks]
    all_w = [weights(*t, s) for t, s in zip(tasks, all_s)]
    for t, w in zip(tasks, all_w):
        finish(*t, w)


def _attn_prompt_call(sinks, q, k, v):
    b, l, _ = q.shape
    nb = l // (BLOCK * ATTN_BLOCKS)
    kv = [pl.BlockSpec((1, BLOCK, KV_DIM),
                       lambda i, n, d=d: (i, jnp.maximum(n * ATTN_BLOCKS + d, 0), 0))
          for d in range(-1, ATTN_BLOCKS)]
    qo = pl.BlockSpec((1, BLOCK * ATTN_BLOCKS, Q_DIM), lambda i, n: (i, n, 0))
    return pl.pallas_call(
        _attn_prompt_kernel,
        grid=(b, nb),
        in_specs=[pl.BlockSpec((1, N_HEADS * BLOCK // ATTN_PARTS), lambda i, n: (0, 0)), qo] + kv + kv,
        out_specs=qo,
        out_shape=jax.ShapeDtypeStruct((b, l, Q_DIM), BF16),
        compiler_params=_cparams(("parallel", "parallel")),
        name="attn_prompt",
    )(sinks, q, *([k] * len(kv)), *([v] * len(kv)))


def _attn_sample_kernel(n_new, seqs, sink_ref, q_ref, kn_ref, vn_ref, ck_ref, cv_ref,
                        o_ref, nk_ref, nv_ref):
    rows = ck_ref.shape[1]
    keys = 2 * rows
    n_cols = N_HEADS * n_new
    key = lax.broadcasted_iota(I32, (keys, n_cols), 0)
    qry = lax.broadcasted_iota(I32, (keys, n_cols), 1) % n_new
    mask = ((key < rows) & (key >= qry)) | ((key >= rows) & (key - rows <= qry))
    low = lax.broadcasted_iota(I32, (n_new, LANES), 1) < HEAD_DIM
    sink = sink_ref[...]
    pad = jnp.zeros((rows - n_new, KV_DIM), F32)

    def scores(sb):
        r0 = sb * n_new
        k_new = kn_ref[r0:r0 + n_new, :]
        v_new = vn_ref[r0:r0 + n_new, :]
        nk_ref[sb, 0:rows - n_new, :] = ck_ref[sb, n_new:rows, :]
        nk_ref[sb, rows - n_new:rows, :] = k_new
        nv_ref[sb, 0:rows - n_new, :] = cv_ref[sb, n_new:rows, :]
        nv_ref[sb, rows - n_new:rows, :] = v_new
        k_all = jnp.concatenate([ck_ref[sb], k_new, pad], axis=0).astype(BF16)
        v_all = jnp.concatenate([cv_ref[sb], v_new, pad], axis=0).astype(BF16)
        blocks = []
        for slot in range(N_HEADS):
            pair = q_ref[r0:r0 + n_new, (slot // 2) * LANES:(slot // 2 + 1) * LANES]
            blk = jnp.where(low if slot % 2 == 0 else ~low, pair, 0.0)
            zero = jnp.zeros_like(blk)
            blocks.append(jnp.concatenate([blk, zero] if slot < N_HEADS // 2 else [zero, blk], axis=1))
        qz = jnp.concatenate(blocks, axis=0).astype(BF16)
        s = _dot_nt(k_all, qz)
        return jnp.where(mask, s, -jnp.inf), v_all

    def weights(s):
        m = jnp.maximum(jnp.max(s, axis=0, keepdims=True), sink)
        p = jnp.exp(s - m)
        denom = jnp.sum(p, axis=0, keepdims=True) + jnp.exp(sink - m)
        return (p * (1.0 / denom)).T.astype(BF16)

    def finish(sb, w, v_all):
        r0 = sb * n_new
        oz = _dot(w, v_all)
        for pr in range(N_HEADS // 2):
            lanes = slice((pr // (N_HEADS // 4)) * LANES, (pr // (N_HEADS // 4) + 1) * LANES)
            a = oz[2 * pr * n_new:(2 * pr + 1) * n_new, lanes]
            b = oz[(2 * pr + 1) * n_new:(2 * pr + 2) * n_new, lanes]
            o_ref[r0:r0 + n_new, pr * LANES:(pr + 1) * LANES] = jnp.where(low, a, b)

    all_s = [scores(sb) for sb in range(seqs)]
    all_w = [weights(s) for s, _ in all_s]
    for sb, (w, (_, v_all)) in enumerate(zip(all_w, all_s)):
        finish(sb, w, v_all)


def _attn_sample_call(sink_row, q, k_new, v_new, cache_k, cache_v, n_new, seqs):
    b, rows, _ = cache_k.shape
    tok = pl.BlockSpec((seqs * n_new, Q_DIM), lambda i: (i, 0))
    tok_kv = pl.BlockSpec((seqs * n_new, KV_DIM), lambda i: (i, 0))
    cache = pl.BlockSpec((seqs, rows, KV_DIM), lambda i: (i, 0, 0))
    return pl.pallas_call(
        functools.partial(_attn_sample_kernel, n_new, seqs),
        grid=(b // seqs,),
        in_specs=[pl.BlockSpec((1, N_HEADS * n_new), lambda i: (0, 0)), tok, tok_kv, tok_kv, cache, cache],
        out_specs=[tok, cache, cache],
        out_shape=[
            jax.ShapeDtypeStruct((b * n_new, Q_DIM), F32),
            jax.ShapeDtypeStruct((b, rows, KV_DIM), F32),
            jax.ShapeDtypeStruct((b, rows, KV_DIM), F32),
        ],
        compiler_params=_cparams(("parallel",)),
        name="attn_sample",
    )(sink_row, q, k_new, v_new, cache_k, cache_v)


def _moe_cap(t_total):
    return t_total + MOE_TILE


def _expert_steps(cap):
    return cap // MOE_TILE - 1 + N_EXPERT_GROUPS


def _write_step_table(maps_ref, counts, cap):
    per = cap // MOE_TILE
    shift = MOE_TILE.bit_length() - 1
    ends = []
    for c in counts:
        tiles = lax.shift_right_logical(c + (MOE_TILE - 1), shift)
        ends.append(tiles if not ends else ends[-1] + tiles)
    total = ends[-1]
    for j in range(maps_ref.shape[1]):
        jj = jnp.maximum(jnp.minimum(j, total - 1), 0)
        g = sum((jj >= e).astype(I32) for e in ends[:-1])
        start = sum(jnp.where(g > k, ends[k] - (ends[k - 1] if k else 0), 0) for k in range(len(ends) - 1))
        blk_in = g * per + jj - start
        valid = (total > j).astype(I32)
        maps_ref[0, j] = blk_in
        maps_ref[1, j] = jnp.where(valid == 1, blk_in, len(counts) * per)
        maps_ref[2, j] = g
        maps_ref[3, j] = valid


def _route_rows(x1, gn_ref, wrh_ref, wrl_ref, br_ref):
    tm = x1.shape[0]
    xn = _rms(x1, gn_ref[...])
    xh = xn.astype(BF16)
    xl = (xn - xh.astype(F32)).astype(BF16)
    logits = _dot(xh, wrh_ref[...]) + _dot(xl, wrh_ref[...]) + _dot(xh, wrl_ref[...]) + br_ref[...]
    lt = logits.T
    ge = N_EXPERTS // EXPERTS_PER_GROUP
    sub = lax.broadcasted_iota(I32, (EXPERTS_PER_GROUP, tm), 0).astype(F32)
    big = jnp.float32(LANES)
    neg = -jnp.inf
    gl = jnp.where(sub < N_EXPERT_GROUPS, lt[N_EXPERTS:N_EXPERTS + EXPERTS_PER_GROUP], neg)
    gmax = jnp.max(gl, axis=0, keepdims=True)
    g_val = 1.0 / jnp.sum(jnp.exp(gl - gmax), axis=0, keepdims=True)
    g_idx = jnp.min(jnp.where(gl == gmax, sub, big), axis=0, keepdims=True)
    el = lt[0:EXPERTS_PER_GROUP]
    for g in range(1, ge):
        el = jnp.where(g_idx == g, lt[g * EXPERTS_PER_GROUP:(g + 1) * EXPERTS_PER_GROUP], el)
    e1 = jnp.max(el, axis=0, keepdims=True)
    i1 = jnp.min(jnp.where(el == e1, sub, big), axis=0, keepdims=True)
    el2 = jnp.where(sub == i1, neg, el)
    e2 = jnp.max(el2, axis=0, keepdims=True)
    i2 = jnp.min(jnp.where(el2 == e2, sub, big), axis=0, keepdims=True)
    t = jnp.exp(e2 - e1)
    w1 = 1.0 / (1.0 + t)
    w2 = t / (1.0 + t)
    wts_t = g_val * (jnp.where(sub == i1, w1, 0.0) + jnp.where(sub == i2, w2, 0.0))
    wts = jnp.concatenate([wts_t, jnp.zeros((LANES - EXPERTS_PER_GROUP, tm), F32)], axis=0).T
    return xn, wts, g_idx


def _route_begin(cntc_in, cntr_in, xs_hbm, pay, cntc, cntr, sem):
    i = pl.program_id(0)
    slot = i % 2

    @pl.when(i == 0)
    def _():
        cntc[...] = cntc_in[...]
        cntr[...] = cntr_in[...]

    @pl.when(i >= 2)
    def _():
        pltpu.make_async_copy(pay.at[slot], xs_hbm.at[pl.ds(0, pay.shape[1]), :], sem.at[slot]).wait()


def _route_scatter(final, cap, parts, triu_ref, dest_ref, cntc_out, cntr_out, xs_hbm, maps_ref,
                   pay, zrows, dest_v, dest_s, cntc, cntr, sem):
    i = pl.program_id(0)
    n = pl.num_programs(0)
    slot = i % 2
    tm = pay.shape[1]

    def wait_slot(s):
        pltpu.make_async_copy(pay.at[s], xs_hbm.at[pl.ds(0, tm), :], sem.at[s]).wait()

    r0 = 0
    for xn, wts, _ in parts:
        pay[slot, r0:r0 + xn.shape[0], :PAY_X] = xn
        pay[slot, r0:r0 + xn.shape[0], PAY_X:] = wts
        r0 += xn.shape[0]
    g_idx = jnp.concatenate([p[2] for p in parts], axis=1)

    grp_t = lax.broadcasted_iota(I32, (LANES, tm), 0).astype(F32)
    oht = (grp_t == g_idx).astype(F32)
    rank = _dot(oht.astype(BF16), triu_ref[...])
    grp = lax.broadcasted_iota(I32, (LANES, 1), 0).astype(F32)
    base = grp * float(cap) + cntc[:, 0:1]
    dest = jnp.sum(oht * (rank + base), axis=0, keepdims=True).astype(I32)
    cntc[...] = cntc[...] + jnp.sum(oht, axis=1, keepdims=True)
    dest_ref[0] = dest
    dest_v[...] = dest
    pltpu.sync_copy(dest_v, dest_s)

    for s in range(2):
        @pl.when(slot == s)
        def _():
            for r in range(tm):
                pltpu.make_async_copy(pay.at[s, pl.ds(r, 1), :],
                                      xs_hbm.at[pl.ds(dest_s[0, r], 1), :],
                                      sem.at[s]).start(priority=r % 2)

    @pl.when(i == n - 1)
    def _():
        cntc_out[...] = cntc[...]
        cntr[...] = cntc[...].T[0:8, :]
        cntr_out[...] = cntr[...]

        @pl.when(n >= 2)
        def _():
            wait_slot(1 - slot)

        wait_slot(slot)
        dest_v[:, 0:LANES] = cntr[0:1, :].astype(I32)
        pltpu.sync_copy(dest_v, dest_s)
        counts = [dest_s[0, g] for g in range(N_EXPERT_GROUPS)]
        _write_step_table(maps_ref, counts, cap)
        if final:
            zrows[...] = jnp.zeros_like(zrows)
            starts = []
            for g in range(N_EXPERT_GROUPS):
                c_g = counts[g]
                starts.append(pl.multiple_of(g * cap + lax.shift_left(lax.shift_right_logical(c_g + 7, 3), 3), 8))
                for k in range(7):
                    pltpu.make_async_copy(zrows.at[pl.ds(0, 1), :],
                                          xs_hbm.at[pl.ds(g * cap + c_g + k, 1), :], sem.at[1]).start()
            for _ in range(7 * N_EXPERT_GROUPS):
                pltpu.make_async_copy(zrows.at[pl.ds(0, 1), :], xs_hbm.at[pl.ds(0, 1), :], sem.at[1]).wait()
            for g in range(N_EXPERT_GROUPS):
                pltpu.make_async_copy(zrows, xs_hbm.at[pl.ds(starts[g], MOE_TILE), :], sem.at[0]).start()
            for g in range(N_EXPERT_GROUPS):
                pltpu.make_async_copy(zrows, xs_hbm.at[pl.ds(0, MOE_TILE), :], sem.at[0]).wait()


RouteRefs = collections.namedtuple(
    "RouteRefs", "gn wrh wrl br triu cntc_in cntr_in x1 dest cntc_out cntr_out xs maps "
                 "pay zrows dest_v dest_s cntc cntr sem")


def _route_refs(rest, has_prev):
    if has_prev:
        rest = rest[:7] + rest[8:]
    return RouteRefs(*rest)


def _route_tile(final, cap, r, x1_parts):
    parts = [_route_rows(x1, r.gn, r.wrh, r.wrl, r.br) for x1 in x1_parts]
    _route_scatter(final, cap, parts, r.triu, r.dest, r.cntc_out, r.cntr_out, r.xs, r.maps,
                   r.pay, r.zrows, r.dest_v, r.dest_s, r.cntc, r.cntr, r.sem)


def _oproj_kernel(has_prev, final, cap, *refs):
    (o_ref, x_ref, wo_ref, bo_ref) = refs[:4]
    r = _route_refs(refs[4:], has_prev)
    _route_begin(r.cntc_in, r.cntr_in, r.xs, r.pay, r.cntc, r.cntr, r.sem)
    rows = x_ref.shape[0] // ROUTE_PARTS
    x1_parts = []
    for h in range(ROUTE_PARTS):
        rs = slice(h * rows, (h + 1) * rows)
        x1 = x_ref[rs, :] + _dot(o_ref[rs, :].astype(BF16), wo_ref[...]) + bo_ref[...]
        r.x1[rs, :] = x1
        x1_parts.append(x1)
    _route_tile(final, cap, r, x1_parts)


def _gelu_tanh(x):
    return x * (0.5 * (1.0 + jnp.tanh(math.sqrt(2.0 / math.pi) * (x + 0.044715 * (x * x * x)))))


def _glu_kernel(has_prev, final, cap, *refs):
    (y_ref, x_ref, gm_ref, d_ref, wa_ref, wb_ref) = refs[:6]
    r = _route_refs(refs[6:], has_prev)
    _route_begin(r.cntc_in, r.cntr_in, r.xs, r.pay, r.cntc, r.cntr, r.sem)
    rows = x_ref.shape[0] // ROUTE_PARTS
    gated = []
    for h in range(ROUTE_PARTS):
        rs = slice(h * rows, (h + 1) * rows)
        x = x_ref[rs, :]
        z = _gelu_tanh(y_ref[rs, :] + d_ref[...] * _rms(x, gm_ref[...])).astype(BF16)
        gated.append((rs, x, _dot(z, wa_ref[...]), _dot(z, wb_ref[...])))
    x1_parts = []
    for rs, x, a, b in gated:
        x1 = x + a * _sigmoid(b)
        r.x1[rs, :] = x1
        x1_parts.append(x1)
    _route_tile(final, cap, r, x1_parts)


def _row_spec(tm, width):
    return pl.BlockSpec((tm, width), lambda i: (i, 0))


def _const_spec(shape):
    return pl.BlockSpec(shape, lambda i: (0,) * len(shape))


def _mixer_call(body, name, lead_args, lead_specs, x2d, router, triu, cnt, xs_prev, final, cap):
    t = x2d.shape[0]
    tm = TOKEN_TILE
    n_tiles = t // tm
    gn, wrh, wrl, br = router
    cntc, cntr = cnt
    has_prev = xs_prev is not None
    in_specs = lead_specs + [
        _const_spec((1, D_MODEL)), _const_spec((D_MODEL, LANES)), _const_spec((D_MODEL, LANES)),
        _const_spec((1, LANES)), _const_spec((tm, tm)), _const_spec((LANES, LANES)), _const_spec((8, LANES)),
    ]
    args = list(lead_args) + [gn, wrh, wrl, br, triu, cntc, cntr]
    aliases = {}
    if has_prev:
        in_specs.append(pl.BlockSpec(memory_space=pl.ANY))
        args.append(xs_prev)
        aliases = {len(args) - 1: 4}
    x1, dest, cntc2, cntr2, xs, maps = pl.pallas_call(
        functools.partial(body, has_prev, final, cap),
        grid=(n_tiles,),
        in_specs=in_specs,
        out_specs=[
            _row_spec(tm, D_MODEL),
            pl.BlockSpec((1, 1, tm), lambda i: (i, 0, 0)),
            _const_spec((LANES, LANES)), _const_spec((8, LANES)),
            pl.BlockSpec(memory_space=pl.ANY),
            pl.BlockSpec(memory_space=pltpu.SMEM),
        ],
        out_shape=[
            jax.ShapeDtypeStruct((t, D_MODEL), F32),
            jax.ShapeDtypeStruct((n_tiles, 1, tm), I32),
            jax.ShapeDtypeStruct((LANES, LANES), F32),
            jax.ShapeDtypeStruct((8, LANES), F32),
            jax.ShapeDtypeStruct((N_EXPERT_GROUPS * cap, PAY_W), F32),
            jax.ShapeDtypeStruct((4, _expert_steps(cap)), I32),
        ],
        scratch_shapes=[
            pltpu.VMEM((2, tm, PAY_W), F32),
            pltpu.VMEM((MOE_TILE, PAY_W), F32),
            pltpu.VMEM((1, tm), I32),
            pltpu.SMEM((1, tm), I32),
            pltpu.VMEM((LANES, LANES), F32),
            pltpu.VMEM((8, LANES), F32),
            pltpu.SemaphoreType.DMA((2,)),
        ],
        input_output_aliases=aliases,
        compiler_params=_cparams(("arbitrary",)),
        name=name,
    )(*args)
    return x1, dest, (cntc2, cntr2), xs, maps


def _oproj_call(o2d, x2d, wo_bf, bo, router, triu, cnt, xs_prev, final, cap):
    tm = TOKEN_TILE
    lead_specs = [_row_spec(tm, Q_DIM), _row_spec(tm, D_MODEL), _const_spec((Q_DIM, D_MODEL)),
                  _const_spec((1, D_MODEL))]
    return _mixer_call(_oproj_kernel, "oproj_route", [o2d, x2d, wo_bf, bo], lead_specs, x2d,
                       router, triu, cnt, xs_prev, final, cap)


def _glu_call(y2d, x2d, gm, d, wa_bf, wb_bf, router, triu, cnt, xs_prev, final, cap):
    tm = TOKEN_TILE
    lead_specs = [_row_spec(tm, D_MODEL), _row_spec(tm, D_MODEL), _const_spec((1, D_MODEL)),
                  _const_spec((1, D_MODEL)), _const_spec((D_MODEL, D_MODEL)),
                  _const_spec((D_MODEL, D_MODEL))]
    return _mixer_call(_glu_kernel, "glu_route", [y2d, x2d, gm, d, wa_bf, wb_bf], lead_specs, x2d,
                       router, triu, cnt, xs_prev, final, cap)


def _expert_kernel(maps_ref, xs_ref, wg_ref, wu_ref, wd_ref, ys_ref, wgu, wdn, hid):
    j = pl.program_id(0)
    f = EXPERT_FF
    changed = (j == 0) | (maps_ref[2, j] != maps_ref[2, jnp.maximum(j - 1, 0)])
    valid = maps_ref[3, j]

    @pl.when(changed)
    def _():
        for e in range(EXPERTS_PER_GROUP):
            wgu[e, :, :f] = wg_ref[e].astype(BF16)
            wgu[e, :, f:] = wu_ref[e].astype(BF16)
            wdn[e * f:(e + 1) * f, :] = wd_ref[e].astype(BF16)

    @pl.when(valid == 1)
    def _():
        x = xs_ref[:, :PAY_X].astype(BF16)
        wts = xs_ref[:, PAY_X:]
        for e in range(EXPERTS_PER_GROUP):
            gu = _dot(x, wgu[e])
            g, u = gu[:, :f], gu[:, f:]
            hid[:, e * f:(e + 1) * f] = ((g * _sigmoid(g)) * u * wts[:, e:e + 1]).astype(BF16)
        ys_ref[...] = _dot(hid[...], wdn[...])

    @pl.when(valid == 0)
    def _():
        ys_ref[...] = jnp.zeros_like(ys_ref)


def _expert_call(maps, xs, w_gate, w_up, w_down, layer, cap):
    n_steps = maps.shape[1]
    e, f = EXPERTS_PER_GROUP, EXPERT_FF
    w_gate = w_gate.reshape(-1, D_MODEL, f)
    w_up = w_up.reshape(-1, D_MODEL, f)
    w_down = w_down.reshape(-1, f, D_MODEL)
    w_blk = lambda j, m: (m[2, j] + layer * N_EXPERT_GROUPS, 0, 0)
    grid_spec = pltpu.PrefetchScalarGridSpec(
        num_scalar_prefetch=1,
        grid=(n_steps,),
        in_specs=[
            pl.BlockSpec((MOE_TILE, PAY_W), lambda j, m: (m[0, j], 0)),
            pl.BlockSpec((e, D_MODEL, f), w_blk),
            pl.BlockSpec((e, D_MODEL, f), w_blk),
            pl.BlockSpec((e, f, D_MODEL), w_blk),
        ],
        out_specs=pl.BlockSpec((MOE_TILE, D_MODEL), lambda j, m: (m[1, j], 0)),
        scratch_shapes=[
            pltpu.VMEM((e, D_MODEL, 2 * f), BF16),
            pltpu.VMEM((e * f, D_MODEL), BF16),
            pltpu.VMEM((MOE_TILE, e * f), BF16),
        ],
    )
    return pl.pallas_call(
        _expert_kernel,
        grid_spec=grid_spec,
        out_shape=jax.ShapeDtypeStruct((N_EXPERT_GROUPS * cap + MOE_TILE, D_MODEL), F32),
        compiler_params=_cparams(("arbitrary",)),
        name="moe_experts",
    )(maps, xs, w_gate, w_up, w_down)


def _combine_kernel(emit_x, dest_ref, x_ref, gnext_ref, ys_hbm, *rest):
    outs, (ybuf, sem) = rest[:-2], rest[-2:]
    i = pl.program_id(0)
    n_tiles = pl.num_programs(0) - 1
    slot = i % 2
    tm = x_ref.shape[0]

    for s in range(2):
        @pl.when((i < n_tiles) & (slot == s))
        def _():
            for r in range(tm):
                pltpu.make_async_copy(ys_hbm.at[pl.ds(dest_ref[0, 0, r], 1), :],
                                      ybuf.at[s, pl.ds(r, 1), :],
                                      sem.at[s]).start(priority=r % 2)

    @pl.when(i >= 1)
    def _():
        prev = 1 - slot
        pltpu.make_async_copy(ys_hbm.at[pl.ds(0, tm), :], ybuf.at[prev], sem.at[prev]).wait()
        x2 = x_ref[...] + ybuf[prev]
        normed = _rms(x2, gnext_ref[...])
        if emit_x:
            outs[0][...] = x2
            outs[1][...] = normed
        else:
            outs[0][...] = normed


def _combine_call(dest, x2d, ys, gnext, emit_x):
    t = x2d.shape[0]
    tm = COMBINE_TILE
    n_tiles = t // tm
    dest = dest.reshape(n_tiles, 1, tm)
    n_out = 2 if emit_x else 1
    done = lambda i: (jnp.maximum(i - 1, 0), 0)
    return pl.pallas_call(
        functools.partial(_combine_kernel, emit_x),
        grid=(n_tiles + 1,),
        in_specs=[
            pl.BlockSpec((1, 1, tm), lambda i: (jnp.minimum(i, n_tiles - 1), 0, 0),
                         memory_space=pltpu.SMEM),
            pl.BlockSpec((tm, D_MODEL), done), _const_spec((1, D_MODEL)),
            pl.BlockSpec(memory_space=pl.ANY),
        ],
        out_specs=[pl.BlockSpec((tm, D_MODEL), done)] * n_out,
        out_shape=[jax.ShapeDtypeStruct((t, D_MODEL), F32)] * n_out,
        scratch_shapes=[pltpu.VMEM((2, tm, D_MODEL), F32), pltpu.SemaphoreType.DMA((2,))],
        compiler_params=_cparams(("arbitrary",)),
        name="moe_combine",
    )(dest, x2d, gnext, ys)


def _s5_state_in(u_ref_val, wre_ref, wim_ref, store_re, store_im, pair_w):
    for m in range(S5_GB // 2):
        up = u_ref_val(m * pair_w, pair_w)
        store_re(m, _dot(up, wre_ref[m]))
        store_im(m, _dot(up, wim_ref[m]))


def _s5_outputs(u_ref_val, hre, him, m_ref, gre_ref, gim_ref, y_store, pair_w):
    gw = pair_w // 2
    for m in range(S5_GB // 2):
        hr = hre(m).astype(BF16)
        hi = him(m).astype(BF16)
        y = _dot(hr, gre_ref[m]) + _dot(hi, gim_ref[m])
        y0 = y[:, :gw] + _dot(u_ref_val(m * pair_w, gw), m_ref[2 * m])
        y1 = y[:, gw:] + _dot(u_ref_val(m * pair_w + gw, gw), m_ref[2 * m + 1])
        y_store(m * pair_w, gw, y0)
        y_store(m * pair_w + gw, gw, y1)


def _s5_flatten(load_rows, q, n, ut, uflat):
    gc = SSM_GROUP_CH
    qc = q * gc
    for s in range(q):
        ut[:, s * gc:(s + 1) * gc, :] = load_rows(s).astype(BF16).T.reshape(S5_GB, gc, n)
    for g in range(S5_GB):
        uflat[:, g * qc:(g + 1) * qc] = ut[g].T


def _s5_unflatten(yflat, yt, store_rows, q, n):
    gc = SSM_GROUP_CH
    qc = q * gc
    for g in range(S5_GB):
        yt[g] = yflat[:, g * qc:(g + 1) * qc].T
    for t in range(q):
        store_rows(t, yt[:, t * gc:(t + 1) * gc, :].reshape(S5_GB * gc, n).T)


def _s5_prompt_kernel(u_ref, wre_ref, wim_ref, m_ref, gre_ref, gim_ref, aqr_ref, aqi_ref,
                      y_ref, her_ref, hei_ref, ut, uflat, sre, sim, hre, him, yflat, yt):
    pair_w = 2 * S5_CHUNK * SSM_GROUP_CH
    n_chunks = u_ref.shape[1] // S5_CHUNK
    n_pairs = S5_GB // 2
    seq_rows = lambda b: pl.ds(b, n_chunks, stride=S5_SEQS)
    for b in range(S5_SEQS):
        _s5_flatten(lambda s: u_ref[b, pl.ds(s, n_chunks, stride=S5_CHUNK), :],
                    S5_CHUNK, n_chunks, ut, uflat.at[b])
        u_val = lambda off, w: uflat[b, :, off:off + w]

        def store_re(m, val):
            sre[m, seq_rows(b), :] = val

        def store_im(m, val):
            sim[m, seq_rows(b), :] = val

        _s5_state_in(u_val, wre_ref, wim_ref, store_re, store_im, pair_w)
    ar = [aqr_ref[:, m * LANES:(m + 1) * LANES] for m in range(n_pairs)]
    ai = [aqi_ref[:, m * LANES:(m + 1) * LANES] for m in range(n_pairs)]

    def step(n, carry):
        rows = pl.ds(pl.multiple_of(n * S5_SEQS, S5_SEQS), S5_SEQS)
        out = []
        for m in range(n_pairs):
            hr, hi = carry[2 * m], carry[2 * m + 1]
            hre[m, rows, :] = hr
            him[m, rows, :] = hi
            out.append(ar[m] * hr - ai[m] * hi + sre[m, rows, :])
            out.append(ar[m] * hi + ai[m] * hr + sim[m, rows, :])
        return tuple(out)

    zero = jnp.zeros((S5_SEQS, LANES), F32)
    last = lax.fori_loop(0, n_chunks, step, (zero,) * (2 * n_pairs))
    for b in range(S5_SEQS):
        for m in range(n_pairs):
            her_ref[b, :, m * LANES:(m + 1) * LANES] = last[2 * m][b:b + 1]
            hei_ref[b, :, m * LANES:(m + 1) * LANES] = last[2 * m + 1][b:b + 1]

    def y_store(off, w, val):
        yflat[:, off:off + w] = val

    for b in range(S5_SEQS):
        u_val = lambda off, w: uflat[b, :, off:off + w]
        _s5_outputs(u_val, lambda m: hre[m, seq_rows(b), :], lambda m: him[m, seq_rows(b), :],
                    m_ref, gre_ref, gim_ref, y_store, pair_w)

        def store_rows(t, val):
            y_ref[b, pl.ds(t, n_chunks, stride=S5_CHUNK), :] = val

        _s5_unflatten(yflat, yt, store_rows, S5_CHUNK, n_chunks)


def _s5_sample_kernel(n_new, u_ref, h0r_ref, h0i_ref, wre_ref, wim_ref, m_ref, gre_ref, gim_ref,
                      aqr_ref, aqi_ref, y_ref, hnr_ref, hni_ref, ut, uflat, sre, sim, yflat, yt):
    pair_w = 2 * n_new * SSM_GROUP_CH
    seqs = h0r_ref.shape[0]
    _s5_flatten(lambda s: u_ref[pl.ds(s, seqs, stride=n_new), :], n_new, seqs, ut, uflat)
    u_val = lambda off, w: uflat[:, off:off + w]

    def store_re(m, val):
        sre[:, m * LANES:(m + 1) * LANES] = val

    def store_im(m, val):
        sim[:, m * LANES:(m + 1) * LANES] = val

    _s5_state_in(u_val, wre_ref, wim_ref, store_re, store_im, pair_w)
    ar = aqr_ref[...]
    ai = aqi_ref[...]
    h0r = h0r_ref[...]
    h0i = h0i_ref[...]
    hnr_ref[...] = ar * h0r - ai * h0i + sre[...]
    hni_ref[...] = ar * h0i + ai * h0r + sim[...]

    def y_store(off, w, val):
        yflat[:, off:off + w] = val

    _s5_outputs(u_val, lambda m: h0r_ref[:, m * LANES:(m + 1) * LANES],
                lambda m: h0i_ref[:, m * LANES:(m + 1) * LANES],
                m_ref, gre_ref, gim_ref, y_store, pair_w)

    def store_rows(t, val):
        y_ref[pl.ds(t, seqs, stride=n_new), :] = val

    _s5_unflatten(yflat, yt, store_rows, n_new, seqs)


def _s5_weight_specs(q, idx):
    qc = q * SSM_GROUP_CH
    np_ = S5_GB // 2
    st = S5_GB * SSM_STATE
    return [
        pl.BlockSpec((np_, 2 * qc, LANES), lambda *a: (idx(*a), 0, 0)),
        pl.BlockSpec((np_, 2 * qc, LANES), lambda *a: (idx(*a), 0, 0)),
        pl.BlockSpec((S5_GB, qc, qc), lambda *a: (idx(*a), 0, 0)),
        pl.BlockSpec((np_, LANES, 2 * qc), lambda *a: (idx(*a), 0, 0)),
        pl.BlockSpec((np_, LANES, 2 * qc), lambda *a: (idx(*a), 0, 0)),
        pl.BlockSpec((1, st), lambda *a: (0, idx(*a))),
        pl.BlockSpec((1, st), lambda *a: (0, idx(*a))),
    ]


def _s5_prompt_call(u, w):
    b, seq, _ = u.shape
    n_chunks = seq // S5_CHUNK
    gbl = S5_GB * SSM_GROUP_CH
    qc = S5_CHUNK * SSM_GROUP_CH
    st = S5_GB * SSM_STATE
    n_gb = SSM_GROUPS // S5_GB
    gb_of = lambda g, i: g
    tok = pl.BlockSpec((S5_SEQS, seq, gbl), lambda g, i: (i, 0, g))
    slab = pltpu.VMEM((S5_GB // 2, n_chunks * S5_SEQS, LANES), F32)
    return pl.pallas_call(
        _s5_prompt_kernel,
        grid=(n_gb, b // S5_SEQS),
        in_specs=[tok] + _s5_weight_specs(S5_CHUNK, gb_of),
        out_specs=[
            tok,
            pl.BlockSpec((S5_SEQS, 1, st), lambda g, i: (i, 0, g)),
            pl.BlockSpec((S5_SEQS, 1, st), lambda g, i: (i, 0, g)),
        ],
        out_shape=[
            jax.ShapeDtypeStruct((b, seq, D_MODEL), F32),
            jax.ShapeDtypeStruct((b, 1, SSM_GROUPS * SSM_STATE), F32),
            jax.ShapeDtypeStruct((b, 1, SSM_GROUPS * SSM_STATE), F32),
        ],
        scratch_shapes=[
            pltpu.VMEM((S5_GB, qc, n_chunks), BF16),
            pltpu.VMEM((S5_SEQS, n_chunks, S5_GB * qc), BF16),
            slab, slab, slab, slab,
            pltpu.VMEM((n_chunks, S5_GB * qc), F32),
            pltpu.VMEM((S5_GB, qc, n_chunks), F32),
        ],
        compiler_params=_cparams(("parallel", "parallel")),
        name="s5_prompt",
    )(u, *w)


def _s5_sample_call(u2d, h0r, h0i, w, n_new):
    t = u2d.shape[0]
    b = t // n_new
    gbl = S5_GB * SSM_GROUP_CH
    qc = n_new * SSM_GROUP_CH
    st = S5_GB * SSM_STATE
    n_gb = SSM_GROUPS // S5_GB
    gb_of = lambda g: g
    state = pl.BlockSpec((b, st), lambda g: (0, g))
    tok = pl.BlockSpec((t, gbl), lambda g: (0, g))
    return pl.pallas_call(
        functools.partial(_s5_sample_kernel, n_new),
        grid=(n_gb,),
        in_specs=[tok, state, state] + _s5_weight_specs(n_new, gb_of),
        out_specs=[tok, state, state],
        out_shape=[
            jax.ShapeDtypeStruct((t, D_MODEL), F32),
            jax.ShapeDtypeStruct((b, SSM_GROUPS * SSM_STATE), F32),
            jax.ShapeDtypeStruct((b, SSM_GROUPS * SSM_STATE), F32),
        ],
        scratch_shapes=[
            pltpu.VMEM((S5_GB, qc, b), BF16),
            pltpu.VMEM((b, S5_GB * qc), BF16),
            pltpu.VMEM((b, st), F32), pltpu.VMEM((b, st), F32),
            pltpu.VMEM((b, S5_GB * qc), F32),
            pltpu.VMEM((S5_GB, qc, b), F32),
        ],
        compiler_params=_cparams(("parallel",)),
        name="s5_sample",
    )(u2d, h0r, h0i, *w)


def _s5_discretize(a_re, a_im, log_dt, b_re, b_im):
    delta = jnp.exp(log_dt.astype(F32))[:, None]
    lr, li = a_re.astype(F32), a_im.astype(F32)
    mag = jnp.exp(delta * lr)
    abar_r = mag * jnp.cos(delta * li)
    abar_i = mag * jnp.sin(delta * li)
    nr, ni = abar_r - 1.0, abar_i
    den = lr * lr + li * li
    coef_r = ((nr * lr + ni * li) / den)[..., None]
    coef_i = ((ni * lr - nr * li) / den)[..., None]
    br, bi = b_re.astype(F32), b_im.astype(F32)
    return delta * lr, delta * li, coef_r * br - coef_i * bi, coef_r * bi + coef_i * br


def _split2(x):
    x1 = x.astype(BF16)
    return x1, (x - x1.astype(F32)).astype(BF16)


def _s5_prep_kernel(qs, bbr_ref, bbi_ref, ctr_ref, cti_ref, ppr_ref, ppi_ref, pnr_ref, pni_ref,
                    *out_refs):
    c = SSM_GROUP_CH
    qc = max(qs) * c
    p2 = 2 * SSM_STATE
    col = lax.broadcasted_iota(I32, (c, qc), 1)
    row = lax.broadcasted_iota(I32, (c, qc), 0)
    by_ch = (col % c == row).astype(BF16)
    by_pos = (col // c == row).astype(BF16)

    def spread(x, sel):
        return sum(_dot(part, sel) for part in _split2(x))

    def dot_f32(a, b):
        a1, a2 = _split2(a)
        b1, b2 = _split2(b)
        return _dot(a1, b1) + _dot(a1, b2) + _dot(a2, b1)

    bc_r, bc_i = spread(bbr_ref[0], by_ch), spread(bbi_ref[0], by_ch)
    cc_r, cc_i = spread(ctr_ref[0], by_ch), spread(cti_ref[0], by_ch)
    pp_r, pp_i = spread(ppr_ref[0], by_pos), spread(ppi_ref[0], by_pos)
    pn_r, pn_i = spread(pnr_ref[0], by_pos), spread(pni_ref[0], by_pos)
    r_r = cc_r * pp_r - cc_i * pp_i
    r_i = cc_r * pp_i + cc_i * pp_r
    l_r = bc_r * pn_r - bc_i * pn_i
    l_i = bc_r * pn_i + bc_i * pn_r
    lt_r, lt_i = l_r.T, l_i.T
    grp_col = lax.broadcasted_iota(I32, (qc, p2), 1) // SSM_STATE
    causal =(lax.broadcasted_iota(I32, (qc, qc), 1) // c) >= (lax.broadcasted_iota(I32, (qc, qc), 0) // c)
    kers = [jnp.where(causal, dot_f32(jnp.where(grp_col == gl, lt_r, 0.0), r_r)
                      - dot_f32(jnp.where(grp_col == gl, lt_i, 0.0), r_i), 0.0) for gl in range(2)]
    for n, q in enumerate(qs):
        wre_ref, wim_ref, m_ref, gre_ref, gim_ref = out_refs[5 * n:5 * n + 5]
        w = q * c
        aq_r = ppr_ref[0][:, q - 1:q]
        aq_i = ppi_ref[0][:, q - 1:q]
        w_r = (l_r[:, :w] * aq_r - l_i[:, :w] * aq_i).T
        w_i = (l_r[:, :w] * aq_i + l_i[:, :w] * aq_r).T
        g_re, g_im = [], []
        col_grp = lax.broadcasted_iota(I32, (w, p2), 1) // SSM_STATE
        row_grp = lax.broadcasted_iota(I32, (p2, w), 0) // SSM_STATE
        for gl in range(2):
            wre_ref[0, gl * w:(gl + 1) * w, :] = jnp.where(col_grp == gl, w_r, 0.0).astype(BF16)
            wim_ref[0, gl * w:(gl + 1) * w, :] = jnp.where(col_grp == gl, w_i, 0.0).astype(BF16)
            m_ref[gl] = kers[gl][:w, :w].astype(BF16)
            g_re.append(jnp.where(row_grp == gl, r_r[:, :w], 0.0))
            g_im.append(jnp.where(row_grp == gl, -r_i[:, :w], 0.0))
        gre_ref[0] = jnp.concatenate(g_re, axis=1).astype(BF16)
        gim_ref[0] = jnp.concatenate(g_im, axis=1).astype(BF16)


def _s5_chunk_weights(tables, qs):
    n_pairs = SSM_GROUPS // 2
    p2 = 2 * SSM_STATE
    tab = pl.BlockSpec((1, p2, SSM_GROUP_CH), lambda m: (m, 0, 0))
    out_specs, out_shape = [], []
    for q in qs:
        qc = q * SSM_GROUP_CH
        out_specs += [pl.BlockSpec((1, 2 * qc, p2), lambda m: (m, 0, 0)),
                      pl.BlockSpec((1, 2 * qc, p2), lambda m: (m, 0, 0)),
                      pl.BlockSpec((2, qc, qc), lambda m: (m, 0, 0)),
                      pl.BlockSpec((1, p2, 2 * qc), lambda m: (m, 0, 0)),
                      pl.BlockSpec((1, p2, 2 * qc), lambda m: (m, 0, 0))]
        out_shape += [jax.ShapeDtypeStruct((n_pairs, 2 * qc, p2), BF16),
                      jax.ShapeDtypeStruct((n_pairs, 2 * qc, p2), BF16),
                      jax.ShapeDtypeStruct((SSM_GROUPS, qc, qc), BF16),
                      jax.ShapeDtypeStruct((n_pairs, p2, 2 * qc), BF16),
                      jax.ShapeDtypeStruct((n_pairs, p2, 2 * qc), BF16)]
    outs = pl.pallas_call(
        functools.partial(_s5_prep_kernel, tuple(qs)),
        grid=(n_pairs,),
        in_specs=[tab] * 8,
        out_specs=out_specs, out_shape=out_shape,
        compiler_params=_cparams(("parallel",)),
        name="s5_operators",
    )(*tables)
    return [tuple(outs[5 * n:5 * n + 5]) for n in range(len(qs))]


def _s5_tables(log_mag, phase, bbar_r, bbar_i, c_re, c_im, q_max):
    k = jnp.arange(1, q_max + 1, dtype=F32)[None, None, :]
    mag = jnp.exp(log_mag[:, :, None] * k)
    cos, sin = jnp.cos(phase[:, :, None] * k), jnp.sin(phase[:, :, None] * k)
    tabs = (bbar_r, bbar_i, c_re.astype(F32).transpose(0, 2, 1), c_im.astype(F32).transpose(0, 2, 1),
            mag * cos, mag * sin, cos / mag, -sin / mag)
    return tuple(t.reshape(SSM_GROUPS // 2, 2 * SSM_STATE, SSM_GROUP_CH) for t in tabs)


def _rope_tables(pos):
    half = HEAD_DIM // 2
    inv = 1.0 / (ROPE_THETA ** (jnp.arange(half, dtype=F32) * (2.0 / HEAD_DIM)))
    ang = pos.astype(F32)[:, None] * inv[None, :]
    cos, sin = jnp.cos(ang), jnp.sin(ang)
    return jnp.tile(cos, (1, 4)), jnp.concatenate([-sin, sin, -sin, sin], axis=1)


def _router_weights(gain, w_rg, b_rg, w_re, b_re):
    n_layers = gain.shape[0]
    pad = LANES - N_EXPERTS - N_EXPERT_GROUPS
    w = jnp.concatenate([w_re, w_rg, jnp.zeros((n_layers, D_MODEL, pad), F32)], axis=2)
    b = jnp.concatenate([b_re, b_rg, jnp.zeros((n_layers, pad), F32)], axis=1)
    wh = w.astype(BF16)
    wl = (w - wh.astype(F32)).astype(BF16)
    return [(gain[l].reshape(1, -1).astype(F32), wh[l], wl[l], b[l].reshape(1, LANES))
            for l in range(n_layers)]


def kernel(x_prompt, x_sample, cache_k, cache_v, state_ssm_re, state_ssm_im, norm_mix, norm_ffn, norm_final, attn_w_qkv, attn_b_qkv, attn_w_o, attn_b_o, attn_sinks, ssm_a_re, ssm_a_im, ssm_log_dt, ssm_b_re, ssm_b_im, ssm_c_re, ssm_c_im, ssm_d, ssm_w_glu_a, ssm_w_glu_b, moe_w_router_group, moe_b_router_group, moe_w_router_expert, moe_b_router_expert, moe_w_gate, moe_w_up, moe_w_down):
    bsz, seq, _ = x_prompt.shape
    dbs, n_new, _ = x_sample.shape
    rows = cache_k.shape[2]
    tp, ts = bsz * seq, dbs * n_new
    cap = _moe_cap(tp + ts)
    xp = x_prompt.reshape(tp, D_MODEL)
    xs = x_sample.reshape(ts, D_MODEL)

    row1 = lambda v: v.reshape(1, -1).astype(F32)
    routers = _router_weights(norm_ffn, moe_w_router_group, moe_b_router_group,
                              moe_w_router_expert, moe_b_router_expert)
    triu = jnp.triu(jnp.ones((TOKEN_TILE, TOKEN_TILE), F32), 1).astype(BF16)
    cnt0 = (jnp.zeros((LANES, LANES), F32), jnp.zeros((8, LANES), F32))

    slots = jnp.asarray(HEAD_SLOTS, dtype=I32)

    def q_slots(w):
        lead = w.shape[:-1]
        qh = w[..., :Q_DIM].reshape(*lead, N_HEADS, HEAD_DIM)[..., slots, :].reshape(*lead, Q_DIM)
        return jnp.concatenate([qh, w[..., Q_DIM:]], axis=-1)

    wqkv = q_slots(attn_w_qkv[0]).astype(BF16)
    bqkv = row1(q_slots(attn_b_qkv[0]))
    wo = attn_w_o[0].reshape(N_HEADS, HEAD_DIM, D_MODEL)[slots].reshape(Q_DIM, D_MODEL).astype(BF16)
    bo = row1(attn_b_o[0])
    sinks = attn_sinks[0].astype(F32)[slots]
    g_mix0, g_mix1 = row1(norm_mix[0]), row1(norm_mix[1])
    pos = jnp.concatenate([jnp.arange(seq, dtype=I32),
                           jnp.tile(PAST_LEN + jnp.arange(n_new, dtype=I32), dbs)])
    cos_t, sin_t = _rope_tables(pos)

    qp, kp, vp = _qkv_call(xp, g_mix0, wqkv, bqkv, cos_t, sin_t, (0, seq), BF16)
    qs, ks, vs = _qkv_call(xs, g_mix0, wqkv, bqkv, cos_t, sin_t, (seq, ts), F32)
    op = _attn_prompt_call(jnp.repeat(sinks, BLOCK // ATTN_PARTS).reshape(1, -1),
                           qp.reshape(bsz, seq, Q_DIM),
                           kp.reshape(bsz, seq, KV_DIM),
                           vp.reshape(bsz, seq, KV_DIM))
    os_, nks, nvs = _attn_sample_call(jnp.repeat(sinks, n_new).reshape(1, -1), qs, ks, vs,
                                      cache_k[0].reshape(dbs, rows, KV_DIM),
                                      cache_v[0].reshape(dbs, rows, KV_DIM), n_new, 8)
    xp1, dest_p, cnt, rows_x, _ = _oproj_call(op.reshape(tp, Q_DIM), xp, wo, bo, routers[0], triu,
                                              cnt0, None, False, cap)
    xs1, dest_s, cnt, rows_x, steps = _oproj_call(os_, xs, wo, bo, routers[0], triu, cnt, rows_x,
                                                  True, cap)
    rows_y = _expert_call(steps, rows_x, moe_w_gate, moe_w_up, moe_w_down, 0, cap)
    xp2, up = _combine_call(dest_p, xp1, rows_y, g_mix1, True)
    xs2, us = _combine_call(dest_s, xs1, rows_y, g_mix1, True)

    disc = _s5_discretize(ssm_a_re[0], ssm_a_im[0], ssm_log_dt[0], ssm_b_re[0], ssm_b_im[0])
    tables = _s5_tables(*disc, ssm_c_re[0], ssm_c_im[0], S5_CHUNK)
    a_pow = lambda q: [tables[i][:, :, q - 1].reshape(1, -1) for i in (4, 5)]
    ops_p, ops_s = _s5_chunk_weights(tables, (S5_CHUNK, n_new))
    w_p = (*ops_p, *a_pow(S5_CHUNK))
    w_s = (*ops_s, *a_pow(n_new))
    y_p, hpr, hpi = _s5_prompt_call(up.reshape(bsz, seq, D_MODEL), w_p)
    y_p = y_p.reshape(tp, D_MODEL)
    h0r = state_ssm_re[0].reshape(dbs, -1).astype(F32)
    h0i = state_ssm_im[0].reshape(dbs, -1).astype(F32)
    y_s, hsr, hsi = _s5_sample_call(us, h0r, h0i, w_s, n_new)

    wa, wb = ssm_w_glu_a[0].astype(BF16), ssm_w_glu_b[0].astype(BF16)
    d_row = row1(ssm_d[0])
    xp3, dest_p, cnt, rows_x, _ = _glu_call(y_p, xp2, g_mix1, d_row, wa, wb, routers[1], triu,
                                            cnt0, None, False, cap)
    xs3, dest_s, cnt, rows_x, steps = _glu_call(y_s, xs2, g_mix1, d_row, wa, wb, routers[1], triu,
                                                cnt, rows_x, True, cap)
    rows_y = _expert_call(steps, rows_x, moe_w_gate, moe_w_up, moe_w_down, 1, cap)
    g_fin = row1(norm_final)
    (yp,) = _combine_call(dest_p, xp3, rows_y, g_fin, False)
    (ys,) = _combine_call(dest_s, xs3, rows_y, g_fin, False)

    kv5 = lambda a, n: a.reshape(1, n, rows, KV_HEADS, HEAD_DIM)
    st4 = lambda a, n: a.reshape(1, n, SSM_GROUPS, SSM_STATE)
    k_last = kp.reshape(bsz, seq, KV_DIM)[:, seq - WINDOW:]
    v_last = vp.reshape(bsz, seq, KV_DIM)[:, seq - WINDOW:]
    return (yp.reshape(bsz, seq, D_MODEL), ys.reshape(dbs, n_new, D_MODEL),
            k_last.reshape(1, bsz, WINDOW, KV_HEADS, HEAD_DIM), kv5(nks, dbs),
            v_last.reshape(1, bsz, WINDOW, KV_HEADS, HEAD_DIM), kv5(nvs, dbs),
            st4(hpr, bsz), st4(hsr, dbs), st4(hpi, bsz), st4(hsi, dbs))
```
